```python
import math, functools
import jax, jax.numpy as jnp
from jax import lax
import numpy as np

D_MODEL = 2048
BATCH = 8
SEQ = 4096
DEPTH = 4

N_MIXERS = 3
EPS = 1e-6
CONF_KERNEL_WIDTH = 31
POOL_WINDOWS = (2, 4, 8, 16)
N_POOL_GROUPS = len(POOL_WINDOWS)
POOL_GROUP_DIM = D_MODEL // N_POOL_GROUPS
HEAD_DIM = 64
N_HEADS = D_MODEL // HEAD_DIM
N_KV_HEADS = N_HEADS // 8
GROUP_SIZE = N_HEADS // N_KV_HEADS
WINDOW = 128
BLOCK = 128
ROT_DIM = HEAD_DIM // 4
ROPE_THETA = 500000.0
D_FF = 5632
FFN_CONV_WIDTH = 3

kernel_name = "hybrid_conv_pool_swa_trunk"


def rms_norm(x, g):
    xf = x.astype(jnp.float32)
    y = xf * lax.rsqrt(jnp.mean(xf * xf, axis=-1, keepdims=True) + EPS)
    return (y * g.astype(jnp.float32)).astype(x.dtype)


def layer_norm(x, g, b):
    xf = x.astype(jnp.float32)
    mu = jnp.mean(xf, axis=-1, keepdims=True)
    xc = xf - mu
    var = jnp.mean(xc * xc, axis=-1, keepdims=True)
    y = xc * lax.rsqrt(var + EPS) * g.astype(jnp.float32) + b.astype(jnp.float32)
    return y.astype(x.dtype)


def causal_depthwise_conv(x, w, b):
    width, channels = w.shape
    y = lax.conv_general_dilated(
        x, w[:, None, :].astype(x.dtype), window_strides=(1,), padding=[(width - 1, 0)],
        dimension_numbers=("NWC", "WIO", "NWC"), feature_group_count=channels)
    return y + b.astype(x.dtype)


def conformer_conv_module(h, w_in, b_in, dw_w, dw_b, ln_g, ln_b, w_out, b_out):
    u = h @ w_in + b_in
    a, gate = jnp.split(u, 2, axis=-1)
    u = a * jax.nn.sigmoid(gate)
    u = causal_depthwise_conv(u, dw_w, dw_b)
    u = jax.nn.silu(layer_norm(u, ln_g, ln_b))
    return u @ w_out + b_out


def multiscale_pool_mixer(h, w_group, scale):
    bsz, seq, dim = h.shape
    hf = h.astype(jnp.float32)
    cs = jnp.concatenate([jnp.zeros((bsz, 1, dim), jnp.float32), jnp.cumsum(hf, axis=1)], axis=1)
    upper = cs[:, 1:]
    t = jnp.arange(seq)
    pooled = []
    for g, w in enumerate(POOL_WINDOWS):
        sl = slice(g * POOL_GROUP_DIM, (g + 1) * POOL_GROUP_DIM)
        lower = jnp.concatenate(
            [jnp.zeros((bsz, w - 1, POOL_GROUP_DIM), jnp.float32), cs[:, :seq + 1 - w, sl]], axis=1)
        count = jnp.minimum(t + 1, w).astype(jnp.float32)[None, :, None]
        pooled.append((upper[..., sl] - lower) / count)
    mixed = (jnp.concatenate(pooled, axis=-1) - hf).astype(h.dtype)
    mixed = mixed.reshape(bsz, seq, N_POOL_GROUPS, POOL_GROUP_DIM)
    y = jnp.einsum("bsgc,gcd->bsgd", mixed, w_group).reshape(bsz, seq, dim)
    return y * scale


def apply_partial_rotary(x, cos, sin):
    half = ROT_DIM // 2
    xr = x[..., :ROT_DIM].astype(jnp.float32)
    x1, x2 = xr[..., :half], xr[..., half:]
    rot = jnp.concatenate([x1 * cos - x2 * sin, x2 * cos + x1 * sin], axis=-1)
    return jnp.concatenate([rot.astype(x.dtype), x[..., ROT_DIM:]], axis=-1)


def banded_sink_attention(q, k, v, sinks):
    bsz, seq = q.shape[:2]
    nb = seq // BLOCK
    scale = 1.0 / math.sqrt(HEAD_DIM)
    qb = q.reshape(bsz, nb, BLOCK, N_KV_HEADS, GROUP_SIZE, HEAD_DIM).transpose(1, 0, 3, 4, 2, 5)

    def band(t):
        tb = t.reshape(bsz, nb, BLOCK, N_KV_HEADS, HEAD_DIM)
        prev = jnp.concatenate([jnp.zeros_like(tb[:, :1]), tb[:, :-1]], axis=1)
        return jnp.concatenate([prev, tb], axis=2).transpose(1, 0, 3, 2, 4)

    kb, vb = band(k), band(v)
    qi = jnp.arange(BLOCK)[:, None]
    kj = jnp.arange(2 * BLOCK)[None, :]
    diff = qi + BLOCK - kj
    in_window = (diff >= 0) & (diff < WINDOW)
    sink = sinks.astype(jnp.float32).reshape(N_KV_HEADS, GROUP_SIZE)[None, :, :, None, None]

    def block_fn(args):
        n, qn, kn, vn = args
        s = jnp.einsum("bkgqd,bkjd->bkgqj", qn.astype(jnp.float32), kn.astype(jnp.float32)) * scale
        valid = in_window & ((n * BLOCK + kj - BLOCK) >= 0)
        s = jnp.where(valid, s, -jnp.inf)
        m = jnp.maximum(jnp.max(s, axis=-1, keepdims=True), sink)
        p = jnp.exp(s - m)
        denom = jnp.sum(p, axis=-1, keepdims=True) + jnp.exp(sink - m)
        o = jnp.einsum("bkgqj,bkjd->bkgqd", p, vn.astype(jnp.float32)) / denom
        return o.astype(q.dtype)

    out = lax.map(block_fn, (jnp.arange(nb), qb, kb, vb))
    return out.transpose(1, 0, 4, 2, 3, 5).reshape(bsz, seq, N_HEADS * HEAD_DIM)


def swa_sink_attention(h, positions, w_qkv, q_norm_g, k_norm_g, sinks, w_o):
    bsz, seq, _ = h.shape
    qkv = h @ w_qkv
    q, k, v = jnp.split(qkv, [N_HEADS * HEAD_DIM, (N_HEADS + N_KV_HEADS) * HEAD_DIM], axis=-1)
    q = rms_norm(q.reshape(bsz, seq, N_HEADS, HEAD_DIM), q_norm_g)
    k = rms_norm(k.reshape(bsz, seq, N_KV_HEADS, HEAD_DIM), k_norm_g)
    v = v.reshape(bsz, seq, N_KV_HEADS, HEAD_DIM)
    inv_freq = ROPE_THETA ** (-jnp.arange(0, ROT_DIM, 2, dtype=jnp.float32) / ROT_DIM)
    ang = positions.astype(jnp.float32)[..., None] * inv_freq
    cos, sin = jnp.cos(ang)[:, :, None, :], jnp.sin(ang)[:, :, None, :]
    q = apply_partial_rotary(q, cos, sin)
    k = apply_partial_rotary(k, cos, sin)
    o = banded_sink_attention(q, k, v, sinks)
    return o @ w_o


def conv_gated_mlp(h, w_up, dw_w, dw_b, w_down):
    u = causal_depthwise_conv(h @ w_up, dw_w, dw_b)
    gate, val = jnp.split(u, 2, axis=-1)
    return (jax.nn.silu(gate) * val) @ w_down


def _fwd_setup_inputs(seed: int = 0) -> dict:
    key = jax.random.key(seed)
    keys = iter(jax.random.split(key, 64))

    def nrm(shape, scale):
        return jax.random.normal(next(keys), shape, jnp.float32) * scale

    def gain(shape):
        return 1.0 + nrm(shape, 0.02)

    d = D_MODEL
    out = {}
    out["x"] = nrm((BATCH, SEQ, d), 1.0)
    offset = jax.random.randint(next(keys), (BATCH, 1), 0, 4096, dtype=jnp.int32)
    out["positions"] = (jnp.arange(SEQ, dtype=jnp.int32)[None, :] + offset).astype(jnp.int32)

    def add_conformer(p):
        out[p + "norm_g"] = gain((d,))
        out[p + "a_w_in"] = nrm((d, 2 * d), d ** -0.5)
        out[p + "a_b_in"] = nrm((2 * d,), 0.02)
        out[p + "a_dw_w"] = nrm((CONF_KERNEL_WIDTH, d), CONF_KERNEL_WIDTH ** -0.5)
        out[p + "a_dw_b"] = nrm((d,), 0.02)
        out[p + "a_ln_g"] = gain((d,))
        out[p + "a_ln_b"] = nrm((d,), 0.02)
        out[p + "a_w_out"] = nrm((d, d), d ** -0.5)
        out[p + "a_b_out"] = nrm((d,), 0.02)

    def add_ffn(p):
        out[p + "ffn_norm_g"] = gain((d,))
        out[p + "ffn_w_up"] = nrm((d, 2 * D_FF), d ** -0.5)
        out[p + "ffn_dw_w"] = nrm((FFN_CONV_WIDTH, 2 * D_FF), FFN_CONV_WIDTH ** -0.5)
        out[p + "ffn_dw_b"] = nrm((2 * D_FF,), 0.02)
        out[p + "ffn_w_down"] = nrm((D_FF, d), D_FF ** -0.5)

    add_conformer("l0_")
    add_ffn("l0_")
    out["l1_norm_g"] = gain((d,))
    out["l1_b_w_group"] = nrm((N_POOL_GROUPS, POOL_GROUP_DIM, POOL_GROUP_DIM), POOL_GROUP_DIM ** -0.5)
    out["l1_b_scale"] = 1.0 + nrm((d,), 0.1)
    add_ffn("l1_")
    out["l2_norm_g"] = gain((d,))
    out["l2_c_w_qkv"] = nrm((d, (N_HEADS + 2 * N_KV_HEADS) * HEAD_DIM), d ** -0.5)
    out["l2_c_q_norm_g"] = gain((HEAD_DIM,))
    out["l2_c_k_norm_g"] = gain((HEAD_DIM,))
    out["l2_c_sinks"] = nrm((N_HEADS,), 1.0)
    out["l2_c_w_o"] = nrm((N_HEADS * HEAD_DIM, d), (N_HEADS * HEAD_DIM) ** -0.5)
    add_ffn("l2_")
    add_conformer("l3_")
    add_ffn("l3_")
    return out


def _fwd_reference(x, positions,
              l0_norm_g, l0_a_w_in, l0_a_b_in, l0_a_dw_w, l0_a_dw_b, l0_a_ln_g, l0_a_ln_b, l0_a_w_out, l0_a_b_out,
              l0_ffn_norm_g, l0_ffn_w_up, l0_ffn_dw_w, l0_ffn_dw_b, l0_ffn_w_down,
              l1_norm_g, l1_b_w_group, l1_b_scale,
              l1_ffn_norm_g, l1_ffn_w_up, l1_ffn_dw_w, l1_ffn_dw_b, l1_ffn_w_down,
              l2_norm_g, l2_c_w_qkv, l2_c_q_norm_g, l2_c_k_norm_g, l2_c_sinks, l2_c_w_o,
              l2_ffn_norm_g, l2_ffn_w_up, l2_ffn_dw_w, l2_ffn_dw_b, l2_ffn_w_down,
              l3_norm_g, l3_a_w_in, l3_a_b_in, l3_a_dw_w, l3_a_dw_b, l3_a_ln_g, l3_a_ln_b, l3_a_w_out, l3_a_b_out,
              l3_ffn_norm_g, l3_ffn_w_up, l3_ffn_dw_w, l3_ffn_dw_b, l3_ffn_w_down):
    mixers = [
        lambda h: conformer_conv_module(h, l0_a_w_in, l0_a_b_in, l0_a_dw_w, l0_a_dw_b,
                                        l0_a_ln_g, l0_a_ln_b, l0_a_w_out, l0_a_b_out),
        lambda h: multiscale_pool_mixer(h, l1_b_w_group, l1_b_scale),
        lambda h: swa_sink_attention(h, positions, l2_c_w_qkv, l2_c_q_norm_g, l2_c_k_norm_g,
                                     l2_c_sinks, l2_c_w_o),
        lambda h: conformer_conv_module(h, l3_a_w_in, l3_a_b_in, l3_a_dw_w, l3_a_dw_b,
                                        l3_a_ln_g, l3_a_ln_b, l3_a_w_out, l3_a_b_out),
    ]
    mixer_norms = [l0_norm_g, l1_norm_g, l2_norm_g, l3_norm_g]
    ffns = [
        (l0_ffn_norm_g, l0_ffn_w_up, l0_ffn_dw_w, l0_ffn_dw_b, l0_ffn_w_down),
        (l1_ffn_norm_g, l1_ffn_w_up, l1_ffn_dw_w, l1_ffn_dw_b, l1_ffn_w_down),
        (l2_ffn_norm_g, l2_ffn_w_up, l2_ffn_dw_w, l2_ffn_dw_b, l2_ffn_w_down),
        (l3_ffn_norm_g, l3_ffn_w_up, l3_ffn_dw_w, l3_ffn_dw_b, l3_ffn_w_down),
    ]
    for i in range(DEPTH):
        x = x + mixers[i](rms_norm(x, mixer_norms[i]))
        g, w_up, dw_w, dw_b, w_down = ffns[i]
        x = x + conv_gated_mlp(rms_norm(x, g), w_up, dw_w, dw_b, w_down)
    return x


import jax as _jax
import jax.numpy as _jnp

TWIN_FORMAT = 'train_step'
FWD_PARAMS = ['x', 'positions', 'l0_norm_g', 'l0_a_w_in', 'l0_a_b_in', 'l0_a_dw_w', 'l0_a_dw_b', 'l0_a_ln_g', 'l0_a_ln_b', 'l0_a_w_out', 'l0_a_b_out', 'l0_ffn_norm_g', 'l0_ffn_w_up', 'l0_ffn_dw_w', 'l0_ffn_dw_b', 'l0_ffn_w_down', 'l1_norm_g', 'l1_b_w_group', 'l1_b_scale', 'l1_ffn_norm_g', 'l1_ffn_w_up', 'l1_ffn_dw_w', 'l1_ffn_dw_b', 'l1_ffn_w_down', 'l2_norm_g', 'l2_c_w_qkv', 'l2_c_q_norm_g', 'l2_c_k_norm_g', 'l2_c_sinks', 'l2_c_w_o', 'l2_ffn_norm_g', 'l2_ffn_w_up', 'l2_ffn_dw_w', 'l2_ffn_dw_b', 'l2_ffn_w_down', 'l3_norm_g', 'l3_a_w_in', 'l3_a_b_in', 'l3_a_dw_w', 'l3_a_dw_b', 'l3_a_ln_g', 'l3_a_ln_b', 'l3_a_w_out', 'l3_a_b_out', 'l3_ffn_norm_g', 'l3_ffn_w_up', 'l3_ffn_dw_w', 'l3_ffn_dw_b', 'l3_ffn_w_down']
TWIN_WEIGHTS = ['l0_norm_g', 'l0_a_w_in', 'l0_a_b_in', 'l0_a_dw_w', 'l0_a_dw_b', 'l0_a_ln_g', 'l0_a_ln_b', 'l0_a_w_out', 'l0_a_b_out', 'l0_ffn_norm_g', 'l0_ffn_w_up', 'l0_ffn_dw_w', 'l0_ffn_dw_b', 'l0_ffn_w_down', 'l1_norm_g', 'l1_b_w_group', 'l1_b_scale', 'l1_ffn_norm_g', 'l1_ffn_w_up', 'l1_ffn_dw_w', 'l1_ffn_dw_b', 'l1_ffn_w_down', 'l2_norm_g', 'l2_c_w_qkv', 'l2_c_q_norm_g', 'l2_c_k_norm_g', 'l2_c_sinks', 'l2_c_w_o', 'l2_ffn_norm_g', 'l2_ffn_w_up', 'l2_ffn_dw_w', 'l2_ffn_dw_b', 'l2_ffn_w_down', 'l3_norm_g', 'l3_a_w_in', 'l3_a_b_in', 'l3_a_dw_w', 'l3_a_dw_b', 'l3_a_ln_g', 'l3_a_ln_b', 'l3_a_w_out', 'l3_a_b_out', 'l3_ffn_norm_g', 'l3_ffn_w_up', 'l3_ffn_dw_w', 'l3_ffn_dw_b', 'l3_ffn_w_down']
TWIN_DIFF_INPUT = 'x'
TWIN_INPUTS = ['x', 'positions', 'l0_norm_g', 'l0_a_w_in', 'l0_a_b_in', 'l0_a_dw_w', 'l0_a_dw_b', 'l0_a_ln_g', 'l0_a_ln_b', 'l0_a_w_out', 'l0_a_b_out', 'l0_ffn_norm_g', 'l0_ffn_w_up', 'l0_ffn_dw_w', 'l0_ffn_dw_b', 'l0_ffn_w_down', 'l1_norm_g', 'l1_b_w_group', 'l1_b_scale', 'l1_ffn_norm_g', 'l1_ffn_w_up', 'l1_ffn_dw_w', 'l1_ffn_dw_b', 'l1_ffn_w_down', 'l2_norm_g', 'l2_c_w_qkv', 'l2_c_q_norm_g', 'l2_c_k_norm_g', 'l2_c_sinks', 'l2_c_w_o', 'l2_ffn_norm_g', 'l2_ffn_w_up', 'l2_ffn_dw_w', 'l2_ffn_dw_b', 'l2_ffn_w_down', 'l3_norm_g', 'l3_a_w_in', 'l3_a_b_in', 'l3_a_dw_w', 'l3_a_dw_b', 'l3_a_ln_g', 'l3_a_ln_b', 'l3_a_w_out', 'l3_a_b_out', 'l3_ffn_norm_g', 'l3_ffn_w_up', 'l3_ffn_dw_w', 'l3_ffn_dw_b', 'l3_ffn_w_down', 'loss_target', 'm_l0_norm_g', 'm_l0_a_w_in', 'm_l0_a_b_in', 'm_l0_a_dw_w', 'm_l0_a_dw_b', 'm_l0_a_ln_g', 'm_l0_a_ln_b', 'm_l0_a_w_out', 'm_l0_a_b_out', 'm_l0_ffn_norm_g', 'm_l0_ffn_w_up', 'm_l0_ffn_dw_w', 'm_l0_ffn_dw_b', 'm_l0_ffn_w_down', 'm_l1_norm_g', 'm_l1_b_w_group', 'm_l1_b_scale', 'm_l1_ffn_norm_g', 'm_l1_ffn_w_up', 'm_l1_ffn_dw_w', 'm_l1_ffn_dw_b', 'm_l1_ffn_w_down', 'm_l2_norm_g', 'm_l2_c_w_qkv', 'm_l2_c_q_norm_g', 'm_l2_c_k_norm_g', 'm_l2_c_sinks', 'm_l2_c_w_o', 'm_l2_ffn_norm_g', 'm_l2_ffn_w_up', 'm_l2_ffn_dw_w', 'm_l2_ffn_dw_b', 'm_l2_ffn_w_down', 'm_l3_norm_g', 'm_l3_a_w_in', 'm_l3_a_b_in', 'm_l3_a_dw_w', 'm_l3_a_dw_b', 'm_l3_a_ln_g', 'm_l3_a_ln_b', 'm_l3_a_w_out', 'm_l3_a_b_out', 'm_l3_ffn_norm_g', 'm_l3_ffn_w_up', 'm_l3_ffn_dw_w', 'm_l3_ffn_dw_b', 'm_l3_ffn_w_down', 'v_l0_norm_g', 'v_l0_a_w_in', 'v_l0_a_b_in', 'v_l0_a_dw_w', 'v_l0_a_dw_b', 'v_l0_a_ln_g', 'v_l0_a_ln_b', 'v_l0_a_w_out', 'v_l0_a_b_out', 'v_l0_ffn_norm_g', 'v_l0_ffn_w_up', 'v_l0_ffn_dw_w', 'v_l0_ffn_dw_b', 'v_l0_ffn_w_down', 'v_l1_norm_g', 'v_l1_b_w_group', 'v_l1_b_scale', 'v_l1_ffn_norm_g', 'v_l1_ffn_w_up', 'v_l1_ffn_dw_w', 'v_l1_ffn_dw_b', 'v_l1_ffn_w_down', 'v_l2_norm_g', 'v_l2_c_w_qkv', 'v_l2_c_q_norm_g', 'v_l2_c_k_norm_g', 'v_l2_c_sinks', 'v_l2_c_w_o', 'v_l2_ffn_norm_g', 'v_l2_ffn_w_up', 'v_l2_ffn_dw_w', 'v_l2_ffn_dw_b', 'v_l2_ffn_w_down', 'v_l3_norm_g', 'v_l3_a_w_in', 'v_l3_a_b_in', 'v_l3_a_dw_w', 'v_l3_a_dw_b', 'v_l3_a_ln_g', 'v_l3_a_ln_b', 'v_l3_a_w_out', 'v_l3_a_b_out', 'v_l3_ffn_norm_g', 'v_l3_ffn_w_up', 'v_l3_ffn_dw_w', 'v_l3_ffn_dw_b', 'v_l3_ffn_w_down']
TWIN_OUTPUTS = ['loss', 'grad_x', 'grad_l0_norm_g', 'grad_l0_a_w_in', 'grad_l0_a_b_in', 'grad_l0_a_dw_w', 'grad_l0_a_dw_b', 'grad_l0_a_ln_g', 'grad_l0_a_ln_b', 'grad_l0_a_w_out', 'grad_l0_a_b_out', 'grad_l0_ffn_norm_g', 'grad_l0_ffn_w_up', 'grad_l0_ffn_dw_w', 'grad_l0_ffn_dw_b', 'grad_l0_ffn_w_down', 'grad_l1_norm_g', 'grad_l1_b_w_group', 'grad_l1_b_scale', 'grad_l1_ffn_norm_g', 'grad_l1_ffn_w_up', 'grad_l1_ffn_dw_w', 'grad_l1_ffn_dw_b', 'grad_l1_ffn_w_down', 'grad_l2_norm_g', 'grad_l2_c_w_qkv', 'grad_l2_c_q_norm_g', 'grad_l2_c_k_norm_g', 'grad_l2_c_sinks', 'grad_l2_c_w_o', 'grad_l2_ffn_norm_g', 'grad_l2_ffn_w_up', 'grad_l2_ffn_dw_w', 'grad_l2_ffn_dw_b', 'grad_l2_ffn_w_down', 'grad_l3_norm_g', 'grad_l3_a_w_in', 'grad_l3_a_b_in', 'grad_l3_a_dw_w', 'grad_l3_a_dw_b', 'grad_l3_a_ln_g', 'grad_l3_a_ln_b', 'grad_l3_a_w_out', 'grad_l3_a_b_out', 'grad_l3_ffn_norm_g', 'grad_l3_ffn_w_up', 'grad_l3_ffn_dw_w', 'grad_l3_ffn_dw_b', 'grad_l3_ffn_w_down', 'delta_l0_norm_g', 'delta_l0_a_w_in', 'delta_l0_a_b_in', 'delta_l0_a_dw_w', 'delta_l0_a_dw_b', 'delta_l0_a_ln_g', 'delta_l0_a_ln_b', 'delta_l0_a_w_out', 'delta_l0_a_b_out', 'delta_l0_ffn_norm_g', 'delta_l0_ffn_w_up', 'delta_l0_ffn_dw_w', 'delta_l0_ffn_dw_b', 'delta_l0_ffn_w_down', 'delta_l1_norm_g', 'delta_l1_b_w_group', 'delta_l1_b_scale', 'delta_l1_ffn_norm_g', 'delta_l1_ffn_w_up', 'delta_l1_ffn_dw_w', 'delta_l1_ffn_dw_b', 'delta_l1_ffn_w_down', 'delta_l2_norm_g', 'delta_l2_c_w_qkv', 'delta_l2_c_q_norm_g', 'delta_l2_c_k_norm_g', 'delta_l2_c_sinks', 'delta_l2_c_w_o', 'delta_l2_ffn_norm_g', 'delta_l2_ffn_w_up', 'delta_l2_ffn_dw_w', 'delta_l2_ffn_dw_b', 'delta_l2_ffn_w_down', 'delta_l3_norm_g', 'delta_l3_a_w_in', 'delta_l3_a_b_in', 'delta_l3_a_dw_w', 'delta_l3_a_dw_b', 'delta_l3_a_ln_g', 'delta_l3_a_ln_b', 'delta_l3_a_w_out', 'delta_l3_a_b_out', 'delta_l3_ffn_norm_g', 'delta_l3_ffn_w_up', 'delta_l3_ffn_dw_w', 'delta_l3_ffn_dw_b', 'delta_l3_ffn_w_down', 'new_m_l0_norm_g', 'new_m_l0_a_w_in', 'new_m_l0_a_b_in', 'new_m_l0_a_dw_w', 'new_m_l0_a_dw_b', 'new_m_l0_a_ln_g', 'new_m_l0_a_ln_b', 'new_m_l0_a_w_out', 'new_m_l0_a_b_out', 'new_m_l0_ffn_norm_g', 'new_m_l0_ffn_w_up', 'new_m_l0_ffn_dw_w', 'new_m_l0_ffn_dw_b', 'new_m_l0_ffn_w_down', 'new_m_l1_norm_g', 'new_m_l1_b_w_group', 'new_m_l1_b_scale', 'new_m_l1_ffn_norm_g', 'new_m_l1_ffn_w_up', 'new_m_l1_ffn_dw_w', 'new_m_l1_ffn_dw_b', 'new_m_l1_ffn_w_down', 'new_m_l2_norm_g', 'new_m_l2_c_w_qkv', 'new_m_l2_c_q_norm_g', 'new_m_l2_c_k_norm_g', 'new_m_l2_c_sinks', 'new_m_l2_c_w_o', 'new_m_l2_ffn_norm_g', 'new_m_l2_ffn_w_up', 'new_m_l2_ffn_dw_w', 'new_m_l2_ffn_dw_b', 'new_m_l2_ffn_w_down', 'new_m_l3_norm_g', 'new_m_l3_a_w_in', 'new_m_l3_a_b_in', 'new_m_l3_a_dw_w', 'new_m_l3_a_dw_b', 'new_m_l3_a_ln_g', 'new_m_l3_a_ln_b', 'new_m_l3_a_w_out', 'new_m_l3_a_b_out', 'new_m_l3_ffn_norm_g', 'new_m_l3_ffn_w_up', 'new_m_l3_ffn_dw_w', 'new_m_l3_ffn_dw_b', 'new_m_l3_ffn_w_down', 'new_v_l0_norm_g', 'new_v_l0_a_w_in', 'new_v_l0_a_b_in', 'new_v_l0_a_dw_w', 'new_v_l0_a_dw_b', 'new_v_l0_a_ln_g', 'new_v_l0_a_ln_b', 'new_v_l0_a_w_out', 'new_v_l0_a_b_out', 'new_v_l0_ffn_norm_g', 'new_v_l0_ffn_w_up', 'new_v_l0_ffn_dw_w', 'new_v_l0_ffn_dw_b', 'new_v_l0_ffn_w_down', 'new_v_l1_norm_g', 'new_v_l1_b_w_group', 'new_v_l1_b_scale', 'new_v_l1_ffn_norm_g', 'new_v_l1_ffn_w_up', 'new_v_l1_ffn_dw_w', 'new_v_l1_ffn_dw_b', 'new_v_l1_ffn_w_down', 'new_v_l2_norm_g', 'new_v_l2_c_w_qkv', 'new_v_l2_c_q_norm_g', 'new_v_l2_c_k_norm_g', 'new_v_l2_c_sinks', 'new_v_l2_c_w_o', 'new_v_l2_ffn_norm_g', 'new_v_l2_ffn_w_up', 'new_v_l2_ffn_dw_w', 'new_v_l2_ffn_dw_b', 'new_v_l2_ffn_w_down', 'new_v_l3_norm_g', 'new_v_l3_a_w_in', 'new_v_l3_a_b_in', 'new_v_l3_a_dw_w', 'new_v_l3_a_dw_b', 'new_v_l3_a_ln_g', 'new_v_l3_a_ln_b', 'new_v_l3_a_w_out', 'new_v_l3_a_b_out', 'new_v_l3_ffn_norm_g', 'new_v_l3_ffn_w_up', 'new_v_l3_ffn_dw_w', 'new_v_l3_ffn_dw_b', 'new_v_l3_ffn_w_down']
TWIN_LEAF_KINDS = {'loss': 'loss', 'grad_x': 'grad_x', 'grad_l0_norm_g': 'grad_w', 'grad_l0_a_w_in': 'grad_w', 'grad_l0_a_b_in': 'grad_w', 'grad_l0_a_dw_w': 'grad_w', 'grad_l0_a_dw_b': 'grad_w', 'grad_l0_a_ln_g': 'grad_w', 'grad_l0_a_ln_b': 'grad_w', 'grad_l0_a_w_out': 'grad_w', 'grad_l0_a_b_out': 'grad_w', 'grad_l0_ffn_norm_g': 'grad_w', 'grad_l0_ffn_w_up': 'grad_w', 'grad_l0_ffn_dw_w': 'grad_w', 'grad_l0_ffn_dw_b': 'grad_w', 'grad_l0_ffn_w_down': 'grad_w', 'grad_l1_norm_g': 'grad_w', 'grad_l1_b_w_group': 'grad_w', 'grad_l1_b_scale': 'grad_w', 'grad_l1_ffn_norm_g': 'grad_w', 'grad_l1_ffn_w_up': 'grad_w', 'grad_l1_ffn_dw_w': 'grad_w', 'grad_l1_ffn_dw_b': 'grad_w', 'grad_l1_ffn_w_down': 'grad_w', 'grad_l2_norm_g': 'grad_w', 'grad_l2_c_w_qkv': 'grad_w', 'grad_l2_c_q_norm_g': 'grad_w', 'grad_l2_c_k_norm_g': 'grad_w', 'grad_l2_c_sinks': 'grad_w', 'grad_l2_c_w_o': 'grad_w', 'grad_l2_ffn_norm_g': 'grad_w', 'grad_l2_ffn_w_up': 'grad_w', 'grad_l2_ffn_dw_w': 'grad_w', 'grad_l2_ffn_dw_b': 'grad_w', 'grad_l2_ffn_w_down': 'grad_w', 'grad_l3_norm_g': 'grad_w', 'grad_l3_a_w_in': 'grad_w', 'grad_l3_a_b_in': 'grad_w', 'grad_l3_a_dw_w': 'grad_w', 'grad_l3_a_dw_b': 'grad_w', 'grad_l3_a_ln_g': 'grad_w', 'grad_l3_a_ln_b': 'grad_w', 'grad_l3_a_w_out': 'grad_w', 'grad_l3_a_b_out': 'grad_w', 'grad_l3_ffn_norm_g': 'grad_w', 'grad_l3_ffn_w_up': 'grad_w', 'grad_l3_ffn_dw_w': 'grad_w', 'grad_l3_ffn_dw_b': 'grad_w', 'grad_l3_ffn_w_down': 'grad_w', 'delta_l0_norm_g': 'delta_w', 'delta_l0_a_w_in': 'delta_w', 'delta_l0_a_b_in': 'delta_w', 'delta_l0_a_dw_w': 'delta_w', 'delta_l0_a_dw_b': 'delta_w', 'delta_l0_a_ln_g': 'delta_w', 'delta_l0_a_ln_b': 'delta_w', 'delta_l0_a_w_out': 'delta_w', 'delta_l0_a_b_out': 'delta_w', 'delta_l0_ffn_norm_g': 'delta_w', 'delta_l0_ffn_w_up': 'delta_w', 'delta_l0_ffn_dw_w': 'delta_w', 'delta_l0_ffn_dw_b': 'delta_w', 'delta_l0_ffn_w_down': 'delta_w', 'delta_l1_norm_g': 'delta_w', 'delta_l1_b_w_group': 'delta_w', 'delta_l1_b_scale': 'delta_w', 'delta_l1_ffn_norm_g': 'delta_w', 'delta_l1_ffn_w_up': 'delta_w', 'delta_l1_ffn_dw_w': 'delta_w', 'delta_l1_ffn_dw_b': 'delta_w', 'delta_l1_ffn_w_down': 'delta_w', 'delta_l2_norm_g': 'delta_w', 'delta_l2_c_w_qkv': 'delta_w', 'delta_l2_c_q_norm_g': 'delta_w', 'delta_l2_c_k_norm_g': 'delta_w', 'delta_l2_c_sinks': 'delta_w', 'delta_l2_c_w_o': 'delta_w', 'delta_l2_ffn_norm_g': 'delta_w', 'delta_l2_ffn_w_up': 'delta_w', 'delta_l2_ffn_dw_w': 'delta_w', 'delta_l2_ffn_dw_b': 'delta_w', 'delta_l2_ffn_w_down': 'delta_w', 'delta_l3_norm_g': 'delta_w', 'delta_l3_a_w_in': 'delta_w', 'delta_l3_a_b_in': 'delta_w', 'delta_l3_a_dw_w': 'delta_w', 'delta_l3_a_dw_b': 'delta_w', 'delta_l3_a_ln_g': 'delta_w', 'delta_l3_a_ln_b': 'delta_w', 'delta_l3_a_w_out': 'delta_w', 'delta_l3_a_b_out': 'delta_w', 'delta_l3_ffn_norm_g': 'delta_w', 'delta_l3_ffn_w_up': 'delta_w', 'delta_l3_ffn_dw_w': 'delta_w', 'delta_l3_ffn_dw_b': 'delta_w', 'delta_l3_ffn_w_down': 'delta_w', 'new_m_l0_norm_g': 'new_m', 'new_m_l0_a_w_in': 'new_m', 'new_m_l0_a_b_in': 'new_m', 'new_m_l0_a_dw_w': 'new_m', 'new_m_l0_a_dw_b': 'new_m', 'new_m_l0_a_ln_g': 'new_m', 'new_m_l0_a_ln_b': 'new_m', 'new_m_l0_a_w_out': 'new_m', 'new_m_l0_a_b_out': 'new_m', 'new_m_l0_ffn_norm_g': 'new_m', 'new_m_l0_ffn_w_up': 'new_m', 'new_m_l0_ffn_dw_w': 'new_m', 'new_m_l0_ffn_dw_b': 'new_m', 'new_m_l0_ffn_w_down': 'new_m', 'new_m_l1_norm_g': 'new_m', 'new_m_l1_b_w_group': 'new_m', 'new_m_l1_b_scale': 'new_m', 'new_m_l1_ffn_norm_g': 'new_m', 'new_m_l1_ffn_w_up': 'new_m', 'new_m_l1_ffn_dw_w': 'new_m', 'new_m_l1_ffn_dw_b': 'new_m', 'new_m_l1_ffn_w_down': 'new_m', 'new_m_l2_norm_g': 'new_m', 'new_m_l2_c_w_qkv': 'new_m', 'new_m_l2_c_q_norm_g': 'new_m', 'new_m_l2_c_k_norm_g': 'new_m', 'new_m_l2_c_sinks': 'new_m', 'new_m_l2_c_w_o': 'new_m', 'new_m_l2_ffn_norm_g': 'new_m', 'new_m_l2_ffn_w_up': 'new_m', 'new_m_l2_ffn_dw_w': 'new_m', 'new_m_l2_ffn_dw_b': 'new_m', 'new_m_l2_ffn_w_down': 'new_m', 'new_m_l3_norm_g': 'new_m', 'new_m_l3_a_w_in': 'new_m', 'new_m_l3_a_b_in': 'new_m', 'new_m_l3_a_dw_w': 'new_m', 'new_m_l3_a_dw_b': 'new_m', 'new_m_l3_a_ln_g': 'new_m', 'new_m_l3_a_ln_b': 'new_m', 'new_m_l3_a_w_out': 'new_m', 'new_m_l3_a_b_out': 'new_m', 'new_m_l3_ffn_norm_g': 'new_m', 'new_m_l3_ffn_w_up': 'new_m', 'new_m_l3_ffn_dw_w': 'new_m', 'new_m_l3_ffn_dw_b': 'new_m', 'new_m_l3_ffn_w_down': 'new_m', 'new_v_l0_norm_g': 'new_v', 'new_v_l0_a_w_in': 'new_v', 'new_v_l0_a_b_in': 'new_v', 'new_v_l0_a_dw_w': 'new_v', 'new_v_l0_a_dw_b': 'new_v', 'new_v_l0_a_ln_g': 'new_v', 'new_v_l0_a_ln_b': 'new_v', 'new_v_l0_a_w_out': 'new_v', 'new_v_l0_a_b_out': 'new_v', 'new_v_l0_ffn_norm_g': 'new_v', 'new_v_l0_ffn_w_up': 'new_v', 'new_v_l0_ffn_dw_w': 'new_v', 'new_v_l0_ffn_dw_b': 'new_v', 'new_v_l0_ffn_w_down': 'new_v', 'new_v_l1_norm_g': 'new_v', 'new_v_l1_b_w_group': 'new_v', 'new_v_l1_b_scale': 'new_v', 'new_v_l1_ffn_norm_g': 'new_v', 'new_v_l1_ffn_w_up': 'new_v', 'new_v_l1_ffn_dw_w': 'new_v', 'new_v_l1_ffn_dw_b': 'new_v', 'new_v_l1_ffn_w_down': 'new_v', 'new_v_l2_norm_g': 'new_v', 'new_v_l2_c_w_qkv': 'new_v', 'new_v_l2_c_q_norm_g': 'new_v', 'new_v_l2_c_k_norm_g': 'new_v', 'new_v_l2_c_sinks': 'new_v', 'new_v_l2_c_w_o': 'new_v', 'new_v_l2_ffn_norm_g': 'new_v', 'new_v_l2_ffn_w_up': 'new_v', 'new_v_l2_ffn_dw_w': 'new_v', 'new_v_l2_ffn_dw_b': 'new_v', 'new_v_l2_ffn_w_down': 'new_v', 'new_v_l3_norm_g': 'new_v', 'new_v_l3_a_w_in': 'new_v', 'new_v_l3_a_b_in': 'new_v', 'new_v_l3_a_dw_w': 'new_v', 'new_v_l3_a_dw_b': 'new_v', 'new_v_l3_a_ln_g': 'new_v', 'new_v_l3_a_ln_b': 'new_v', 'new_v_l3_a_w_out': 'new_v', 'new_v_l3_a_b_out': 'new_v', 'new_v_l3_ffn_norm_g': 'new_v', 'new_v_l3_ffn_w_up': 'new_v', 'new_v_l3_ffn_dw_w': 'new_v', 'new_v_l3_ffn_dw_b': 'new_v', 'new_v_l3_ffn_w_down': 'new_v'}


def _forward(args):
    return _fwd_reference(*[args[k] for k in FWD_PARAMS])


def _output_shape():
    out = _jax.eval_shape(lambda: _forward(_fwd_setup_inputs(0)))
    return out.shape, out.dtype

N_MICROBATCH = 1
ADAM_LR = 0.001
ADAM_B1 = 0.9
ADAM_B2 = 0.999
ADAM_EPS = 1e-08
ADAM_WD = 0.01
ADAM_STEP = 10
PER_EXAMPLE_BATCH_AXIS = {'x': 0, 'positions': 0, 'loss_target': 0}
SHARED_INPUTS = []
_WEIGHT_DTYPES = {'l0_norm_g': _jnp.float32, 'l0_a_w_in': _jnp.float32, 'l0_a_b_in': _jnp.float32, 'l0_a_dw_w': _jnp.float32, 'l0_a_dw_b': _jnp.float32, 'l0_a_ln_g': _jnp.float32, 'l0_a_ln_b': _jnp.float32, 'l0_a_w_out': _jnp.float32, 'l0_a_b_out': _jnp.float32, 'l0_ffn_norm_g': _jnp.float32, 'l0_ffn_w_up': _jnp.float32, 'l0_ffn_dw_w': _jnp.float32, 'l0_ffn_dw_b': _jnp.float32, 'l0_ffn_w_down': _jnp.float32, 'l1_norm_g': _jnp.float32, 'l1_b_w_group': _jnp.float32, 'l1_b_scale': _jnp.float32, 'l1_ffn_norm_g': _jnp.float32, 'l1_ffn_w_up': _jnp.float32, 'l1_ffn_dw_w': _jnp.float32, 'l1_ffn_dw_b': _jnp.float32, 'l1_ffn_w_down': _jnp.float32, 'l2_norm_g': _jnp.float32, 'l2_c_w_qkv': _jnp.float32, 'l2_c_q_norm_g': _jnp.float32, 'l2_c_k_norm_g': _jnp.float32, 'l2_c_sinks': _jnp.float32, 'l2_c_w_o': _jnp.float32, 'l2_ffn_norm_g': _jnp.float32, 'l2_ffn_w_up': _jnp.float32, 'l2_ffn_dw_w': _jnp.float32, 'l2_ffn_dw_b': _jnp.float32, 'l2_ffn_w_down': _jnp.float32, 'l3_norm_g': _jnp.float32, 'l3_a_w_in': _jnp.float32, 'l3_a_b_in': _jnp.float32, 'l3_a_dw_w': _jnp.float32, 'l3_a_dw_b': _jnp.float32, 'l3_a_ln_g': _jnp.float32, 'l3_a_ln_b': _jnp.float32, 'l3_a_w_out': _jnp.float32, 'l3_a_b_out': _jnp.float32, 'l3_ffn_norm_g': _jnp.float32, 'l3_ffn_w_up': _jnp.float32, 'l3_ffn_dw_w': _jnp.float32, 'l3_ffn_dw_b': _jnp.float32, 'l3_ffn_w_down': _jnp.float32}
MOMENT_SCALE = {'l0_norm_g': 3.734017e-01, 'l0_a_w_in': 2.635946e-01, 'l0_a_b_in': 3.875297e+00, 'l0_a_dw_w': 5.330397e-01, 'l0_a_dw_b': 8.697595e+00, 'l0_a_ln_g': 7.553400e+00, 'l0_a_ln_b': 6.802537e+00, 'l0_a_w_out': 1.921344e+00, 'l0_a_b_out': 1.064095e+01, 'l0_ffn_norm_g': 1.362274e+01, 'l0_ffn_w_up': 3.562304e-01, 'l0_ffn_dw_w': 1.835672e+00, 'l0_ffn_dw_b': 1.933922e+00, 'l0_ffn_w_down': 3.979989e-01, 'l1_norm_g': 1.228178e+01, 'l1_b_w_group': 6.689851e-01, 'l1_b_scale': 1.222229e+01, 'l1_ffn_norm_g': 1.305401e+01, 'l1_ffn_w_up': 2.707476e-01, 'l1_ffn_dw_w': 1.767746e+00, 'l1_ffn_dw_b': 1.894940e+00, 'l1_ffn_w_down': 2.759277e-01, 'l2_norm_g': 9.851811e-01, 'l2_c_w_qkv': 8.587879e-01, 'l2_c_q_norm_g': 5.577415e+00, 'l2_c_k_norm_g': 5.591144e+00, 'l2_c_sinks': 3.955945e-01, 'l2_c_w_o': 8.278902e-01, 'l2_ffn_norm_g': 1.299241e+01, 'l2_ffn_w_up': 2.784562e-01, 'l2_ffn_dw_w': 1.790735e+00, 'l2_ffn_dw_b': 1.741912e+00, 'l2_ffn_w_down': 2.254130e-01, 'l3_norm_g': 3.103196e-01, 'l3_a_w_in': 2.171548e-01, 'l3_a_b_in': 2.139238e+00, 'l3_a_dw_w': 4.345345e-01, 'l3_a_dw_b': 5.394621e+00, 'l3_a_ln_g': 7.485854e+00, 'l3_a_ln_b': 5.457803e+00, 'l3_a_w_out': 1.200715e+00, 'l3_a_b_out': 5.970515e+00, 'l3_ffn_norm_g': 1.335448e+01, 'l3_ffn_w_up': 3.375107e-01, 'l3_ffn_dw_w': 1.932203e+00, 'l3_ffn_dw_b': 1.815370e+00, 'l3_ffn_w_down': 2.212543e-01}


def _to_microbatches(a, axis):
    t = _jnp.moveaxis(a, axis, 0)
    t = t.reshape((N_MICROBATCH, t.shape[0] // N_MICROBATCH) + t.shape[1:])
    return _jnp.moveaxis(t, 1, axis + 1)


def setup_inputs(seed: int = 0) -> dict:
    inp = _fwd_setup_inputs(seed)
    key = _jax.random.fold_in(_jax.random.key(seed), 7919)
    shape, _ = _output_shape()
    out = dict(inp)
    out["loss_target"] = _jax.random.normal(_jax.random.fold_in(key, 0), shape, _jnp.float32)
    for i, name in enumerate(TWIN_WEIGHTS):
        w = inp[name].astype(_jnp.float32)
        if MOMENT_SCALE is None:
            s = _jnp.sqrt(_jnp.mean(_jnp.square(w)) + 1e-30)
        else:
            s = MOMENT_SCALE[name]
        km, kv = _jax.random.split(_jax.random.fold_in(key, i + 1))
        out[name] = w
        out["m_" + name] = s * _jax.random.normal(km, w.shape, _jnp.float32)
        out["v_" + name] = (s * s) * _jax.random.uniform(kv, w.shape, _jnp.float32, 0.5, 1.5)
    if N_MICROBATCH > 1:
        for name, axis in PER_EXAMPLE_BATCH_AXIS.items():
            out[name] = _to_microbatches(out[name], axis)
    return {'x': out['x'], 'positions': out['positions'], 'l0_norm_g': out['l0_norm_g'], 'l0_a_w_in': out['l0_a_w_in'], 'l0_a_b_in': out['l0_a_b_in'], 'l0_a_dw_w': out['l0_a_dw_w'], 'l0_a_dw_b': out['l0_a_dw_b'], 'l0_a_ln_g': out['l0_a_ln_g'], 'l0_a_ln_b': out['l0_a_ln_b'], 'l0_a_w_out': out['l0_a_w_out'], 'l0_a_b_out': out['l0_a_b_out'], 'l0_ffn_norm_g': out['l0_ffn_norm_g'], 'l0_ffn_w_up': out['l0_ffn_w_up'], 'l0_ffn_dw_w': out['l0_ffn_dw_w'], 'l0_ffn_dw_b': out['l0_ffn_dw_b'], 'l0_ffn_w_down': out['l0_ffn_w_down'], 'l1_norm_g': out['l1_norm_g'], 'l1_b_w_group': out['l1_b_w_group'], 'l1_b_scale': out['l1_b_scale'], 'l1_ffn_norm_g': out['l1_ffn_norm_g'], 'l1_ffn_w_up': out['l1_ffn_w_up'], 'l1_ffn_dw_w': out['l1_ffn_dw_w'], 'l1_ffn_dw_b': out['l1_ffn_dw_b'], 'l1_ffn_w_down': out['l1_ffn_w_down'], 'l2_norm_g': out['l2_norm_g'], 'l2_c_w_qkv': out['l2_c_w_qkv'], 'l2_c_q_norm_g': out['l2_c_q_norm_g'], 'l2_c_k_norm_g': out['l2_c_k_norm_g'], 'l2_c_sinks': out['l2_c_sinks'], 'l2_c_w_o': out['l2_c_w_o'], 'l2_ffn_norm_g': out['l2_ffn_norm_g'], 'l2_ffn_w_up': out['l2_ffn_w_up'], 'l2_ffn_dw_w': out['l2_ffn_dw_w'], 'l2_ffn_dw_b': out['l2_ffn_dw_b'], 'l2_ffn_w_down': out['l2_ffn_w_down'], 'l3_norm_g': out['l3_norm_g'], 'l3_a_w_in': out['l3_a_w_in'], 'l3_a_b_in': out['l3_a_b_in'], 'l3_a_dw_w': out['l3_a_dw_w'], 'l3_a_dw_b': out['l3_a_dw_b'], 'l3_a_ln_g': out['l3_a_ln_g'], 'l3_a_ln_b': out['l3_a_ln_b'], 'l3_a_w_out': out['l3_a_w_out'], 'l3_a_b_out': out['l3_a_b_out'], 'l3_ffn_norm_g': out['l3_ffn_norm_g'], 'l3_ffn_w_up': out['l3_ffn_w_up'], 'l3_ffn_dw_w': out['l3_ffn_dw_w'], 'l3_ffn_dw_b': out['l3_ffn_dw_b'], 'l3_ffn_w_down': out['l3_ffn_w_down'], 'loss_target': out['loss_target'], 'm_l0_norm_g': out['m_l0_norm_g'], 'm_l0_a_w_in': out['m_l0_a_w_in'], 'm_l0_a_b_in': out['m_l0_a_b_in'], 'm_l0_a_dw_w': out['m_l0_a_dw_w'], 'm_l0_a_dw_b': out['m_l0_a_dw_b'], 'm_l0_a_ln_g': out['m_l0_a_ln_g'], 'm_l0_a_ln_b': out['m_l0_a_ln_b'], 'm_l0_a_w_out': out['m_l0_a_w_out'], 'm_l0_a_b_out': out['m_l0_a_b_out'], 'm_l0_ffn_norm_g': out['m_l0_ffn_norm_g'], 'm_l0_ffn_w_up': out['m_l0_ffn_w_up'], 'm_l0_ffn_dw_w': out['m_l0_ffn_dw_w'], 'm_l0_ffn_dw_b': out['m_l0_ffn_dw_b'], 'm_l0_ffn_w_down': out['m_l0_ffn_w_down'], 'm_l1_norm_g': out['m_l1_norm_g'], 'm_l1_b_w_group': out['m_l1_b_w_group'], 'm_l1_b_scale': out['m_l1_b_scale'], 'm_l1_ffn_norm_g': out['m_l1_ffn_norm_g'], 'm_l1_ffn_w_up': out['m_l1_ffn_w_up'], 'm_l1_ffn_dw_w': out['m_l1_ffn_dw_w'], 'm_l1_ffn_dw_b': out['m_l1_ffn_dw_b'], 'm_l1_ffn_w_down': out['m_l1_ffn_w_down'], 'm_l2_norm_g': out['m_l2_norm_g'], 'm_l2_c_w_qkv': out['m_l2_c_w_qkv'], 'm_l2_c_q_norm_g': out['m_l2_c_q_norm_g'], 'm_l2_c_k_norm_g': out['m_l2_c_k_norm_g'], 'm_l2_c_sinks': out['m_l2_c_sinks'], 'm_l2_c_w_o': out['m_l2_c_w_o'], 'm_l2_ffn_norm_g': out['m_l2_ffn_norm_g'], 'm_l2_ffn_w_up': out['m_l2_ffn_w_up'], 'm_l2_ffn_dw_w': out['m_l2_ffn_dw_w'], 'm_l2_ffn_dw_b': out['m_l2_ffn_dw_b'], 'm_l2_ffn_w_down': out['m_l2_ffn_w_down'], 'm_l3_norm_g': out['m_l3_norm_g'], 'm_l3_a_w_in': out['m_l3_a_w_in'], 'm_l3_a_b_in': out['m_l3_a_b_in'], 'm_l3_a_dw_w': out['m_l3_a_dw_w'], 'm_l3_a_dw_b': out['m_l3_a_dw_b'], 'm_l3_a_ln_g': out['m_l3_a_ln_g'], 'm_l3_a_ln_b': out['m_l3_a_ln_b'], 'm_l3_a_w_out': out['m_l3_a_w_out'], 'm_l3_a_b_out': out['m_l3_a_b_out'], 'm_l3_ffn_norm_g': out['m_l3_ffn_norm_g'], 'm_l3_ffn_w_up': out['m_l3_ffn_w_up'], 'm_l3_ffn_dw_w': out['m_l3_ffn_dw_w'], 'm_l3_ffn_dw_b': out['m_l3_ffn_dw_b'], 'm_l3_ffn_w_down': out['m_l3_ffn_w_down'], 'v_l0_norm_g': out['v_l0_norm_g'], 'v_l0_a_w_in': out['v_l0_a_w_in'], 'v_l0_a_b_in': out['v_l0_a_b_in'], 'v_l0_a_dw_w': out['v_l0_a_dw_w'], 'v_l0_a_dw_b': out['v_l0_a_dw_b'], 'v_l0_a_ln_g': out['v_l0_a_ln_g'], 'v_l0_a_ln_b': out['v_l0_a_ln_b'], 'v_l0_a_w_out': out['v_l0_a_w_out'], 'v_l0_a_b_out': out['v_l0_a_b_out'], 'v_l0_ffn_norm_g': out['v_l0_ffn_norm_g'], 'v_l0_ffn_w_up': out['v_l0_ffn_w_up'], 'v_l0_ffn_dw_w': out['v_l0_ffn_dw_w'], 'v_l0_ffn_dw_b': out['v_l0_ffn_dw_b'], 'v_l0_ffn_w_down': out['v_l0_ffn_w_down'], 'v_l1_norm_g': out['v_l1_norm_g'], 'v_l1_b_w_group': out['v_l1_b_w_group'], 'v_l1_b_scale': out['v_l1_b_scale'], 'v_l1_ffn_norm_g': out['v_l1_ffn_norm_g'], 'v_l1_ffn_w_up': out['v_l1_ffn_w_up'], 'v_l1_ffn_dw_w': out['v_l1_ffn_dw_w'], 'v_l1_ffn_dw_b': out['v_l1_ffn_dw_b'], 'v_l1_ffn_w_down': out['v_l1_ffn_w_down'], 'v_l2_norm_g': out['v_l2_norm_g'], 'v_l2_c_w_qkv': out['v_l2_c_w_qkv'], 'v_l2_c_q_norm_g': out['v_l2_c_q_norm_g'], 'v_l2_c_k_norm_g': out['v_l2_c_k_norm_g'], 'v_l2_c_sinks': out['v_l2_c_sinks'], 'v_l2_c_w_o': out['v_l2_c_w_o'], 'v_l2_ffn_norm_g': out['v_l2_ffn_norm_g'], 'v_l2_ffn_w_up': out['v_l2_ffn_w_up'], 'v_l2_ffn_dw_w': out['v_l2_ffn_dw_w'], 'v_l2_ffn_dw_b': out['v_l2_ffn_dw_b'], 'v_l2_ffn_w_down': out['v_l2_ffn_w_down'], 'v_l3_norm_g': out['v_l3_norm_g'], 'v_l3_a_w_in': out['v_l3_a_w_in'], 'v_l3_a_b_in': out['v_l3_a_b_in'], 'v_l3_a_dw_w': out['v_l3_a_dw_w'], 'v_l3_a_dw_b': out['v_l3_a_dw_b'], 'v_l3_a_ln_g': out['v_l3_a_ln_g'], 'v_l3_a_ln_b': out['v_l3_a_ln_b'], 'v_l3_a_w_out': out['v_l3_a_w_out'], 'v_l3_a_b_out': out['v_l3_a_b_out'], 'v_l3_ffn_norm_g': out['v_l3_ffn_norm_g'], 'v_l3_ffn_w_up': out['v_l3_ffn_w_up'], 'v_l3_ffn_dw_w': out['v_l3_ffn_dw_w'], 'v_l3_ffn_dw_b': out['v_l3_ffn_dw_b'], 'v_l3_ffn_w_down': out['v_l3_ffn_w_down']}


def _loss(weights, diff, rest, loss_target):
    with _jax.named_scope("forward"):
        args = {**rest, TWIN_DIFF_INPUT: diff, **{k: w.astype(_WEIGHT_DTYPES[k]) for k, w in weights.items()}}
        y = _forward(args)
    with _jax.named_scope("loss_head"):
        err = _jnp.square(y.astype(_jnp.float32) - loss_target)
        return 0.5 * _jnp.sum(_jnp.mean(err, axis=-1)) if err.ndim else 0.5 * err


def _adamw(w, g, m, v):
    m = ADAM_B1 * m + (1.0 - ADAM_B1) * g
    v = ADAM_B2 * v + (1.0 - ADAM_B2) * _jnp.square(g)
    m_hat = m / (1.0 - ADAM_B1 ** ADAM_STEP)
    v_hat = v / (1.0 - ADAM_B2 ** ADAM_STEP)
    delta = -ADAM_LR * (m_hat / (_jnp.sqrt(v_hat) + ADAM_EPS) + ADAM_WD * w)
    return delta, m, v


def reference(x, positions, l0_norm_g, l0_a_w_in, l0_a_b_in, l0_a_dw_w, l0_a_dw_b, l0_a_ln_g, l0_a_ln_b, l0_a_w_out, l0_a_b_out, l0_ffn_norm_g, l0_ffn_w_up, l0_ffn_dw_w, l0_ffn_dw_b, l0_ffn_w_down, l1_norm_g, l1_b_w_group, l1_b_scale, l1_ffn_norm_g, l1_ffn_w_up, l1_ffn_dw_w, l1_ffn_dw_b, l1_ffn_w_down, l2_norm_g, l2_c_w_qkv, l2_c_q_norm_g, l2_c_k_norm_g, l2_c_sinks, l2_c_w_o, l2_ffn_norm_g, l2_ffn_w_up, l2_ffn_dw_w, l2_ffn_dw_b, l2_ffn_w_down, l3_norm_g, l3_a_w_in, l3_a_b_in, l3_a_dw_w, l3_a_dw_b, l3_a_ln_g, l3_a_ln_b, l3_a_w_out, l3_a_b_out, l3_ffn_norm_g, l3_ffn_w_up, l3_ffn_dw_w, l3_ffn_dw_b, l3_ffn_w_down, loss_target, m_l0_norm_g, m_l0_a_w_in, m_l0_a_b_in, m_l0_a_dw_w, m_l0_a_dw_b, m_l0_a_ln_g, m_l0_a_ln_b, m_l0_a_w_out, m_l0_a_b_out, m_l0_ffn_norm_g, m_l0_ffn_w_up, m_l0_ffn_dw_w, m_l0_ffn_dw_b, m_l0_ffn_w_down, m_l1_norm_g, m_l1_b_w_group, m_l1_b_scale, m_l1_ffn_norm_g, m_l1_ffn_w_up, m_l1_ffn_dw_w, m_l1_ffn_dw_b, m_l1_ffn_w_down, m_l2_norm_g, m_l2_c_w_qkv, m_l2_c_q_norm_g, m_l2_c_k_norm_g, m_l2_c_sinks, m_l2_c_w_o, m_l2_ffn_norm_g, m_l2_ffn_w_up, m_l2_ffn_dw_w, m_l2_ffn_dw_b, m_l2_ffn_w_down, m_l3_norm_g, m_l3_a_w_in, m_l3_a_b_in, m_l3_a_dw_w, m_l3_a_dw_b, m_l3_a_ln_g, m_l3_a_ln_b, m_l3_a_w_out, m_l3_a_b_out, m_l3_ffn_norm_g, m_l3_ffn_w_up, m_l3_ffn_dw_w, m_l3_ffn_dw_b, m_l3_ffn_w_down, v_l0_norm_g, v_l0_a_w_in, v_l0_a_b_in, v_l0_a_dw_w, v_l0_a_dw_b, v_l0_a_ln_g, v_l0_a_ln_b, v_l0_a_w_out, v_l0_a_b_out, v_l0_ffn_norm_g, v_l0_ffn_w_up, v_l0_ffn_dw_w, v_l0_ffn_dw_b, v_l0_ffn_w_down, v_l1_norm_g, v_l1_b_w_group, v_l1_b_scale, v_l1_ffn_norm_g, v_l1_ffn_w_up, v_l1_ffn_dw_w, v_l1_ffn_dw_b, v_l1_ffn_w_down, v_l2_norm_g, v_l2_c_w_qkv, v_l2_c_q_norm_g, v_l2_c_k_norm_g, v_l2_c_sinks, v_l2_c_w_o, v_l2_ffn_norm_g, v_l2_ffn_w_up, v_l2_ffn_dw_w, v_l2_ffn_dw_b, v_l2_ffn_w_down, v_l3_norm_g, v_l3_a_w_in, v_l3_a_b_in, v_l3_a_dw_w, v_l3_a_dw_b, v_l3_a_ln_g, v_l3_a_ln_b, v_l3_a_w_out, v_l3_a_b_out, v_l3_ffn_norm_g, v_l3_ffn_w_up, v_l3_ffn_dw_w, v_l3_ffn_dw_b, v_l3_ffn_w_down):
    given = dict(x=x, positions=positions, l0_norm_g=l0_norm_g, l0_a_w_in=l0_a_w_in, l0_a_b_in=l0_a_b_in, l0_a_dw_w=l0_a_dw_w, l0_a_dw_b=l0_a_dw_b, l0_a_ln_g=l0_a_ln_g, l0_a_ln_b=l0_a_ln_b, l0_a_w_out=l0_a_w_out, l0_a_b_out=l0_a_b_out, l0_ffn_norm_g=l0_ffn_norm_g, l0_ffn_w_up=l0_ffn_w_up, l0_ffn_dw_w=l0_ffn_dw_w, l0_ffn_dw_b=l0_ffn_dw_b, l0_ffn_w_down=l0_ffn_w_down, l1_norm_g=l1_norm_g, l1_b_w_group=l1_b_w_group, l1_b_scale=l1_b_scale, l1_ffn_norm_g=l1_ffn_norm_g, l1_ffn_w_up=l1_ffn_w_up, l1_ffn_dw_w=l1_ffn_dw_w, l1_ffn_dw_b=l1_ffn_dw_b, l1_ffn_w_down=l1_ffn_w_down, l2_norm_g=l2_norm_g, l2_c_w_qkv=l2_c_w_qkv, l2_c_q_norm_g=l2_c_q_norm_g, l2_c_k_norm_g=l2_c_k_norm_g, l2_c_sinks=l2_c_sinks, l2_c_w_o=l2_c_w_o, l2_ffn_norm_g=l2_ffn_norm_g, l2_ffn_w_up=l2_ffn_w_up, l2_ffn_dw_w=l2_ffn_dw_w, l2_ffn_dw_b=l2_ffn_dw_b, l2_ffn_w_down=l2_ffn_w_down, l3_norm_g=l3_norm_g, l3_a_w_in=l3_a_w_in, l3_a_b_in=l3_a_b_in, l3_a_dw_w=l3_a_dw_w, l3_a_dw_b=l3_a_dw_b, l3_a_ln_g=l3_a_ln_g, l3_a_ln_b=l3_a_ln_b, l3_a_w_out=l3_a_w_out, l3_a_b_out=l3_a_b_out, l3_ffn_norm_g=l3_ffn_norm_g, l3_ffn_w_up=l3_ffn_w_up, l3_ffn_dw_w=l3_ffn_dw_w, l3_ffn_dw_b=l3_ffn_dw_b, l3_ffn_w_down=l3_ffn_w_down, loss_target=loss_target, m_l0_norm_g=m_l0_norm_g, m_l0_a_w_in=m_l0_a_w_in, m_l0_a_b_in=m_l0_a_b_in, m_l0_a_dw_w=m_l0_a_dw_w, m_l0_a_dw_b=m_l0_a_dw_b, m_l0_a_ln_g=m_l0_a_ln_g, m_l0_a_ln_b=m_l0_a_ln_b, m_l0_a_w_out=m_l0_a_w_out, m_l0_a_b_out=m_l0_a_b_out, m_l0_ffn_norm_g=m_l0_ffn_norm_g, m_l0_ffn_w_up=m_l0_ffn_w_up, m_l0_ffn_dw_w=m_l0_ffn_dw_w, m_l0_ffn_dw_b=m_l0_ffn_dw_b, m_l0_ffn_w_down=m_l0_ffn_w_down, m_l1_norm_g=m_l1_norm_g, m_l1_b_w_group=m_l1_b_w_group, m_l1_b_scale=m_l1_b_scale, m_l1_ffn_norm_g=m_l1_ffn_norm_g, m_l1_ffn_w_up=m_l1_ffn_w_up, m_l1_ffn_dw_w=m_l1_ffn_dw_w, m_l1_ffn_dw_b=m_l1_ffn_dw_b, m_l1_ffn_w_down=m_l1_ffn_w_down, m_l2_norm_g=m_l2_norm_g, m_l2_c_w_qkv=m_l2_c_w_qkv, m_l2_c_q_norm_g=m_l2_c_q_norm_g, m_l2_c_k_norm_g=m_l2_c_k_norm_g, m_l2_c_sinks=m_l2_c_sinks, m_l2_c_w_o=m_l2_c_w_o, m_l2_ffn_norm_g=m_l2_ffn_norm_g, m_l2_ffn_w_up=m_l2_ffn_w_up, m_l2_ffn_dw_w=m_l2_ffn_dw_w, m_l2_ffn_dw_b=m_l2_ffn_dw_b, m_l2_ffn_w_down=m_l2_ffn_w_down, m_l3_norm_g=m_l3_norm_g, m_l3_a_w_in=m_l3_a_w_in, m_l3_a_b_in=m_l3_a_b_in, m_l3_a_dw_w=m_l3_a_dw_w, m_l3_a_dw_b=m_l3_a_dw_b, m_l3_a_ln_g=m_l3_a_ln_g, m_l3_a_ln_b=m_l3_a_ln_b, m_l3_a_w_out=m_l3_a_w_out, m_l3_a_b_out=m_l3_a_b_out, m_l3_ffn_norm_g=m_l3_ffn_norm_g, m_l3_ffn_w_up=m_l3_ffn_w_up, m_l3_ffn_dw_w=m_l3_ffn_dw_w, m_l3_ffn_dw_b=m_l3_ffn_dw_b, m_l3_ffn_w_down=m_l3_ffn_w_down, v_l0_norm_g=v_l0_norm_g, v_l0_a_w_in=v_l0_a_w_in, v_l0_a_b_in=v_l0_a_b_in, v_l0_a_dw_w=v_l0_a_dw_w, v_l0_a_dw_b=v_l0_a_dw_b, v_l0_a_ln_g=v_l0_a_ln_g, v_l0_a_ln_b=v_l0_a_ln_b, v_l0_a_w_out=v_l0_a_w_out, v_l0_a_b_out=v_l0_a_b_out, v_l0_ffn_norm_g=v_l0_ffn_norm_g, v_l0_ffn_w_up=v_l0_ffn_w_up, v_l0_ffn_dw_w=v_l0_ffn_dw_w, v_l0_ffn_dw_b=v_l0_ffn_dw_b, v_l0_ffn_w_down=v_l0_ffn_w_down, v_l1_norm_g=v_l1_norm_g, v_l1_b_w_group=v_l1_b_w_group, v_l1_b_scale=v_l1_b_scale, v_l1_ffn_norm_g=v_l1_ffn_norm_g, v_l1_ffn_w_up=v_l1_ffn_w_up, v_l1_ffn_dw_w=v_l1_ffn_dw_w, v_l1_ffn_dw_b=v_l1_ffn_dw_b, v_l1_ffn_w_down=v_l1_ffn_w_down, v_l2_norm_g=v_l2_norm_g, v_l2_c_w_qkv=v_l2_c_w_qkv, v_l2_c_q_norm_g=v_l2_c_q_norm_g, v_l2_c_k_norm_g=v_l2_c_k_norm_g, v_l2_c_sinks=v_l2_c_sinks, v_l2_c_w_o=v_l2_c_w_o, v_l2_ffn_norm_g=v_l2_ffn_norm_g, v_l2_ffn_w_up=v_l2_ffn_w_up, v_l2_ffn_dw_w=v_l2_ffn_dw_w, v_l2_ffn_dw_b=v_l2_ffn_dw_b, v_l2_ffn_w_down=v_l2_ffn_w_down, v_l3_norm_g=v_l3_norm_g, v_l3_a_w_in=v_l3_a_w_in, v_l3_a_b_in=v_l3_a_b_in, v_l3_a_dw_w=v_l3_a_dw_w, v_l3_a_dw_b=v_l3_a_dw_b, v_l3_a_ln_g=v_l3_a_ln_g, v_l3_a_ln_b=v_l3_a_ln_b, v_l3_a_w_out=v_l3_a_w_out, v_l3_a_b_out=v_l3_a_b_out, v_l3_ffn_norm_g=v_l3_ffn_norm_g, v_l3_ffn_w_up=v_l3_ffn_w_up, v_l3_ffn_dw_w=v_l3_ffn_dw_w, v_l3_ffn_dw_b=v_l3_ffn_dw_b, v_l3_ffn_w_down=v_l3_ffn_w_down)
    weights = {n: given[n] for n in TWIN_WEIGHTS}
    shared = {n: given[n] for n in SHARED_INPUTS}
    per_example = {n: given[n] for n in ['x', 'positions']}
    grad_fn = _jax.value_and_grad(_loss, argnums=(0, 1))

    def one_microbatch(ex, loss_target):
        ex = dict(ex)
        diff = ex.pop(TWIN_DIFF_INPUT)
        return grad_fn(weights, diff, {**shared, **ex}, loss_target)

    if N_MICROBATCH == 1:
        loss, (grad_w, grad_x) = one_microbatch(per_example, given["loss_target"])
    else:
        def body(carry, xs):
            loss_sum, grad_sum = carry
            l_k, (gw_k, gx_k) = one_microbatch(xs[0], xs[1])
            with _jax.named_scope("update"):
                return (loss_sum + l_k, _jax.tree.map(_jnp.add, grad_sum, gw_k)), gx_k

        init = (_jnp.zeros((), _jnp.float32), _jax.tree.map(_jnp.zeros_like, weights))
        (loss, grad_w), grad_x = _jax.lax.scan(body, init, (per_example, given["loss_target"]))
    with _jax.named_scope("update"):
        delta_w, new_m, new_v = {}, {}, {}
        for n in TWIN_WEIGHTS:
            delta_w[n], new_m[n], new_v[n] = _adamw(weights[n], grad_w[n], given["m_" + n], given["v_" + n])
    return (loss, grad_x, *[grad_w[n] for n in TWIN_WEIGHTS], *[delta_w[n] for n in TWIN_WEIGHTS],
            *[new_m[n] for n in TWIN_WEIGHTS], *[new_v[n] for n in TWIN_WEIGHTS])
```

```python
import functools
import math

import jax
import jax.numpy as jnp
from jax import lax
from jax.experimental import pallas as pl
from jax.experimental.pallas import tpu as pltpu

F32 = jnp.float32
BF16 = jnp.bfloat16
N_DEV = 8
EPS = 1e-6
LANES = 128
HEAD = 64
Q_BLOCK = 128
ROT_DIM = 16
ROPE_THETA = 500000.0
POOL_WINDOWS = (2, 4, 8, 16)
HALO = 32
ROWS = 256
VMEM_LIMIT = 56 * 1024 * 1024
ADAM_LR, ADAM_B1, ADAM_B2, ADAM_EPS, ADAM_WD, ADAM_STEP = 0.001, 0.9, 0.999, 1e-08, 0.01, 10
MESH_ID = pl.DeviceIdType.MESH


def _pcall(body, **kw):
    return pl.pallas_call(body, **kw)


def _params(sem=None, **kw):
    if sem is not None:
        kw["dimension_semantics"] = sem
    return pltpu.CompilerParams(vmem_limit_bytes=VMEM_LIMIT, **kw)


def _pick(dim, pref, mult=LANES):
    best = None
    d = mult
    while d <= min(dim, pref):
        if dim % d == 0:
            best = d
        d += mult
    return dim if best is None else best


def _sigmoid(x):
    return 1.0 / (1.0 + jnp.exp(-x))


def _fold8(p):
    r, c = p.shape
    return p.reshape(r // 8, 8, c).sum(axis=0)


def _rows(e, k, r):
    n = e.shape[0]
    if k % 8 == 0:
        return e[k:k + r]
    return pltpu.roll(e, n - k, 0)[0:r]


def _lshape(a):
    return a.shape if a.ndim == 2 else (a.shape[1], a.shape[0] * a.shape[2])


def _panel(a):
    return a.shape[1] if a.ndim == 2 else a.shape[2]


def _lspec(a, br, bc, rc):
    if a.ndim == 2:
        return pl.BlockSpec((br, bc), rc)
    per = a.shape[2] // bc

    def idx(i, j, k):
        r, c = rc(i, j, k)
        return (c // per, r, c % per)
    return pl.BlockSpec((None, br, bc), idx)


def _mm(a, b, dims, *, name, out_dtype=F32, out_stack=None, bias=None, res=None, tm=1024, tn=1024, tk=1024):
    (ar, ac), (br_, bc_) = _lshape(a), _lshape(b)
    if dims == "nn":
        m, k, n = ar, ac, bc_
        lim_m, lim_k, lim_n = m, min(_panel(a), k), _panel(b)
    elif dims == "nt":
        m, k, n = ar, ac, br_
        lim_m, lim_k, lim_n = m, math.gcd(_panel(a), _panel(b)), n
    else:
        m, k, n = ac, ar, bc_
        lim_m, lim_k, lim_n = _panel(a), k, _panel(b)
    if out_stack is not None:
        lim_n = math.gcd(lim_n, n // out_stack)
    sub = 16 if (out_dtype == BF16 or a.dtype == BF16) else 8
    tm = _pick(lim_m, tm, LANES if dims == "tn" else sub)
    tn = _pick(lim_n, tn)
    tk = _pick(lim_k, tk, sub if dims == "tn" else LANES)
    nk = k // tk
    if dims == "tn":
        a_spec = _lspec(a, tk, tm, lambda i, j, kk: (kk, i))
    else:
        a_spec = _lspec(a, tm, tk, lambda i, j, kk: (i, kk))
    if dims == "nt":
        b_spec = _lspec(b, tn, tk, lambda i, j, kk: (j, kk))
    else:
        b_spec = _lspec(b, tk, tn, lambda i, j, kk: (kk, j))
    contract = {"nn": ((1,), (0,)), "nt": ((1,), (1,)), "tn": ((0,), (0,))}[dims]
    in_specs, args = [a_spec, b_spec], [a, b]
    if bias is not None:
        in_specs.append(pl.BlockSpec((1, tn), lambda i, j, kk: (0, j)))
        args.append(bias)
    if res is not None:
        in_specs.append(pl.BlockSpec((tm, tn), lambda i, j, kk: (i, j)))
        args.append(res)
    if out_stack is None:
        out_shape = jax.ShapeDtypeStruct((m, n), out_dtype)
    else:
        out_shape = jax.ShapeDtypeStruct((out_stack, m, n // out_stack), out_dtype)
    o_spec = _lspec(out_shape, tm, tn, lambda i, j, kk: (i, j))
    has_bias, has_res = bias is not None, res is not None

    def body(*refs):
        a_ref, b_ref = refs[0], refs[1]
        pos = 2
        bias_ref = res_ref = None
        if has_bias:
            bias_ref = refs[pos]
            pos += 1
        if has_res:
            res_ref = refs[pos]
            pos += 1
        o_ref = refs[pos]

        def part():
            return lax.dot_general(a_ref[...].astype(BF16), b_ref[...].astype(BF16), (contract, ((), ())),
                                   preferred_element_type=F32)

        def finish(r):
            if has_bias:
                r = r + bias_ref[...]
            if has_res:
                r = r + res_ref[...]
            o_ref[...] = r.astype(out_dtype)

        if nk == 1:
            finish(part())
        else:
            acc = refs[pos + 1]
            kk = pl.program_id(2)

            @pl.when(kk == 0)
            def _():
                acc[...] = part()

            @pl.when(kk > 0)
            def _():
                acc[...] += part()

            @pl.when(kk == nk - 1)
            def _():
                finish(acc[...])

    scratch = [] if nk == 1 else [pltpu.VMEM((tm, tn), F32)]
    return _pcall(body, grid=(m // tm, n // tn, nk), in_specs=in_specs, out_specs=o_spec, out_shape=out_shape,
                  scratch_shapes=scratch, compiler_params=_params(("parallel", "parallel", "arbitrary")), name=name)(*args)


def _rms_fwd(x, g, out_dtype, name):
    t, d = x.shape
    tm = _pick(t, 512, 16)

    def body(x_ref, g_ref, o_ref):
        xv = x_ref[...]
        r = lax.rsqrt(jnp.mean(xv * xv, axis=-1, keepdims=True) + EPS)
        o_ref[...] = ((xv * r) * g_ref[...]).astype(out_dtype)

    return _pcall(body, grid=(t // tm,), in_specs=[pl.BlockSpec((tm, d), lambda i: (i, 0)), pl.BlockSpec((1, d), lambda i: (0, 0))],
                  out_specs=pl.BlockSpec((tm, d), lambda i: (i, 0)), out_shape=jax.ShapeDtypeStruct((t, d), out_dtype),
                  compiler_params=_params(("parallel",)), name=name)(x, g)


def _rms_bwd(dh, x, g, dres, name):
    t, d = x.shape
    tm = _pick(t, 256, 8)

    def body(dh_ref, x_ref, g_ref, dres_ref, dx_ref, dg_ref, cs_ref):
        xv, dhv, dr = x_ref[...], dh_ref[...], dres_ref[...]
        r = lax.rsqrt(jnp.mean(xv * xv, axis=-1, keepdims=True) + EPS)
        xh = xv * r
        dxh = dhv * g_ref[...]
        dx_ref[...] = dr + r * (dxh - xh * jnp.mean(dxh * xh, axis=-1, keepdims=True))
        pg = jnp.sum(dhv * xh, axis=0, keepdims=True)
        pc = jnp.sum(dr, axis=0, keepdims=True)

        @pl.when(pl.program_id(0) == 0)
        def _():
            dg_ref[...] = pg
            cs_ref[...] = pc

        @pl.when(pl.program_id(0) > 0)
        def _():
            dg_ref[...] += pg
            cs_ref[...] += pc

    row = pl.BlockSpec((tm, d), lambda i: (i, 0))
    vec = pl.BlockSpec((1, d), lambda i: (0, 0))
    return _pcall(body, grid=(t // tm,), in_specs=[row, row, vec, row], out_specs=[row, vec, vec],
                  out_shape=[jax.ShapeDtypeStruct((t, d), F32), jax.ShapeDtypeStruct((1, d), F32), jax.ShapeDtypeStruct((1, d), F32)],
                  compiler_params=_params(("arbitrary",)), name=name)(dh, x, g, dres)


def _loss_head(y, target, name):
    t, d = y.shape
    tm = _pick(t, 512, 8)

    def body(y_ref, t_ref, dy_ref, l_ref):
        e = y_ref[...] - t_ref[...]
        dy_ref[...] = e * (1.0 / d)
        part = 0.5 * jnp.sum(jnp.mean(e * e, axis=-1, keepdims=True), axis=0, keepdims=True)

        @pl.when(pl.program_id(0) == 0)
        def _():
            l_ref[...] = part

        @pl.when(pl.program_id(0) > 0)
        def _():
            l_ref[...] += part

    row = pl.BlockSpec((tm, d), lambda i: (i, 0))
    return _pcall(body, grid=(t // tm,), in_specs=[row, row], out_specs=[row, pl.BlockSpec((1, 1), lambda i: (0, 0))],
                  out_shape=[jax.ShapeDtypeStruct((t, d), F32), jax.ShapeDtypeStruct((1, 1), F32)],
                  compiler_params=_params(("arbitrary",)), name=name)(y, target)


def _ln_silu_fwd(c, g, b, name):
    t, d = c.shape
    tm = _pick(t, 512, 16)

    def body(c_ref, g_ref, b_ref, o_ref):
        cv = c_ref[...]
        xc = cv - jnp.mean(cv, axis=-1, keepdims=True)
        z = xc * lax.rsqrt(jnp.mean(xc * xc, axis=-1, keepdims=True) + EPS) * g_ref[...] + b_ref[...]
        o_ref[...] = (z * _sigmoid(z)).astype(BF16)

    row = pl.BlockSpec((tm, d), lambda i: (i, 0))
    vec = pl.BlockSpec((1, d), lambda i: (0, 0))
    return _pcall(body, grid=(t // tm,), in_specs=[row, vec, vec], out_specs=row, out_shape=jax.ShapeDtypeStruct((t, d), BF16),
                  compiler_params=_params(("parallel",)), name=name)(c, g, b)


def _ln_silu_bwd(ds, c, g, b, name):
    t, d = c.shape
    tm = _pick(t, 256, 8)

    def body(ds_ref, c_ref, g_ref, b_ref, dc_ref, dg_ref, db_ref):
        cv = c_ref[...]
        xc = cv - jnp.mean(cv, axis=-1, keepdims=True)
        r = lax.rsqrt(jnp.mean(xc * xc, axis=-1, keepdims=True) + EPS)
        ch = xc * r
        z = ch * g_ref[...] + b_ref[...]
        sg = _sigmoid(z)
        dz = ds_ref[...] * (sg * (1.0 + z * (1.0 - sg)))
        dch = dz * g_ref[...]
        dc_ref[...] = r * (dch - jnp.mean(dch, axis=-1, keepdims=True) - ch * jnp.mean(dch * ch, axis=-1, keepdims=True))
        pg = jnp.sum(dz * ch, axis=0, keepdims=True)
        pb = jnp.sum(dz, axis=0, keepdims=True)

        @pl.when(pl.program_id(0) == 0)
        def _():
            dg_ref[...] = pg
            db_ref[...] = pb

        @pl.when(pl.program_id(0) > 0)
        def _():
            dg_ref[...] += pg
            db_ref[...] += pb

    row = pl.BlockSpec((tm, d), lambda i: (i, 0))
    vec = pl.BlockSpec((1, d), lambda i: (0, 0))
    return _pcall(body, grid=(t // tm,), in_specs=[row, row, vec, vec], out_specs=[row, vec, vec],
                  out_shape=[jax.ShapeDtypeStruct((t, d), F32), jax.ShapeDtypeStruct((1, d), F32), jax.ShapeDtypeStruct((1, d), F32)],
                  compiler_params=_params(("arbitrary",)), name=name)(ds, c, g, b)


def _steps(t):
    return t // ROWS


def _conf_conv_fwd(u, dw_w, dw_b, name):
    t, d2 = u.shape
    d = d2 // 2
    c = LANES
    ns = d // c
    kc = dw_w.shape[0]

    def body(a_ref, g_ref, w_ref, b_ref, o_ref, pad):
        pad[0:HALO, :] = jnp.zeros((HALO, c), F32)

        def glu(i, _):
            base = pl.multiple_of(i * ROWS, ROWS)
            pad[pl.ds(base + HALO, ROWS), :] = a_ref[pl.ds(base, ROWS), :] * _sigmoid(g_ref[pl.ds(base, ROWS), :])
            return 0
        lax.fori_loop(0, _steps(t), glu, 0)

        def conv(i, _):
            base = pl.multiple_of(i * ROWS, ROWS)
            e = pad[pl.ds(base, ROWS + HALO), :]
            acc = jnp.zeros((ROWS, c), F32) + b_ref[...]
            for j in range(kc):
                acc = acc + w_ref[j:j + 1, :] * _rows(e, HALO - (kc - 1) + j, ROWS)
            o_ref[pl.ds(base, ROWS), :] = acc
            return 0
        lax.fori_loop(0, _steps(t), conv, 0)

    return _pcall(body, grid=(ns,),
                  in_specs=[pl.BlockSpec((t, c), lambda s: (0, s)), pl.BlockSpec((t, c), lambda s: (0, s + ns)),
                            pl.BlockSpec((kc, c), lambda s: (0, s)), pl.BlockSpec((1, c), lambda s: (0, s))],
                  out_specs=pl.BlockSpec((t, c), lambda s: (0, s)), out_shape=jax.ShapeDtypeStruct((t, d), F32),
                  scratch_shapes=[pltpu.VMEM((t + HALO, c), F32)], compiler_params=_params(("parallel",)), name=name)(u, u, dw_w, dw_b)


def _conf_conv_bwd(dc, u, dw_w, name):
    t, d = dc.shape
    c = LANES
    ns = d // c
    kc = dw_w.shape[0]

    def body(dc_ref, a_ref, g_ref, w_ref, du_ref, dww_ref, dwb_ref, db_ref, padv, padd, accw, accb):
        padv[0:HALO, :] = jnp.zeros((HALO, c), F32)
        padd[t:t + HALO, :] = jnp.zeros((HALO, c), F32)
        accw[...] = jnp.zeros_like(accw)
        accb[...] = jnp.zeros_like(accb)

        def fill(i, _):
            base = pl.multiple_of(i * ROWS, ROWS)
            padv[pl.ds(base + HALO, ROWS), :] = a_ref[pl.ds(base, ROWS), :] * _sigmoid(g_ref[pl.ds(base, ROWS), :])
            padd[pl.ds(base, ROWS), :] = dc_ref[pl.ds(base, ROWS), :]
            return 0
        lax.fori_loop(0, _steps(t), fill, 0)

        def step(i, _):
            base = pl.multiple_of(i * ROWS, ROWS)
            ev = padv[pl.ds(base, ROWS + HALO), :]
            ed = padd[pl.ds(base, ROWS + HALO), :]
            dcc = ed[0:ROWS]
            dv = jnp.zeros((ROWS, c), F32)
            for j in range(kc):
                dv = dv + w_ref[j:j + 1, :] * _rows(ed, kc - 1 - j, ROWS)
                accw[j] = accw[j] + _fold8(dcc * _rows(ev, HALO - (kc - 1) + j, ROWS))
            accb[0] = accb[0] + _fold8(dcc)
            av = a_ref[pl.ds(base, ROWS), :]
            sg = _sigmoid(g_ref[pl.ds(base, ROWS), :])
            da = dv * sg
            dg = dv * av * sg * (1.0 - sg)
            du_ref[0, pl.ds(base, ROWS), :] = da.astype(BF16)
            du_ref[1, pl.ds(base, ROWS), :] = dg.astype(BF16)
            accb[1] = accb[1] + _fold8(da)
            accb[2] = accb[2] + _fold8(dg)
            return 0
        lax.fori_loop(0, _steps(t), step, 0)
        for j in range(kc):
            dww_ref[j:j + 1, :] = jnp.sum(accw[j], axis=0, keepdims=True)
        dwb_ref[...] = jnp.sum(accb[0], axis=0, keepdims=True)
        db_ref[0] = jnp.sum(accb[1], axis=0, keepdims=True)
        db_ref[1] = jnp.sum(accb[2], axis=0, keepdims=True)

    return _pcall(body, grid=(ns,),
                  in_specs=[pl.BlockSpec((t, c), lambda s: (0, s)), pl.BlockSpec((t, c), lambda s: (0, s)),
                            pl.BlockSpec((t, c), lambda s: (0, s + ns)), pl.BlockSpec((kc, c), lambda s: (0, s))],
                  out_specs=[pl.BlockSpec((2, t, c), lambda s: (0, 0, s)), pl.BlockSpec((kc, c), lambda s: (0, s)),
                             pl.BlockSpec((1, c), lambda s: (0, s)), pl.BlockSpec((2, 1, c), lambda s: (0, 0, s))],
                  out_shape=[jax.ShapeDtypeStruct((2, t, d), BF16), jax.ShapeDtypeStruct((kc, d), F32),
                             jax.ShapeDtypeStruct((1, d), F32), jax.ShapeDtypeStruct((2, 1, d), F32)],
                  scratch_shapes=[pltpu.VMEM((t + HALO, c), F32), pltpu.VMEM((t + HALO, c), F32),
                                  pltpu.VMEM((kc, 8, c), F32), pltpu.VMEM((3, 8, c), F32)],
                  compiler_params=_params(("parallel",)), name=name)(dc, u, u, dw_w)


def _ffn_act_fwd(u0, dw_w, dw_b, name):
    t, f2 = u0.shape
    f = f2 // 2
    c = LANES
    ns = f // c
    kw = dw_w.shape[0]

    def body(g_ref, v_ref, wg_ref, wv_ref, bg_ref, bv_ref, o_ref):
        def step(i, _):
            base = pl.multiple_of(i * ROWS, ROWS)
            lo = pl.multiple_of(jnp.maximum(base - HALO, 0), HALO)
            keep = jnp.where(i > 0, 1.0, 0.0)
            eg = jnp.concatenate([g_ref[pl.ds(lo, HALO), :] * keep, g_ref[pl.ds(base, ROWS), :]], axis=0)
            ev = jnp.concatenate([v_ref[pl.ds(lo, HALO), :] * keep, v_ref[pl.ds(base, ROWS), :]], axis=0)
            gate = jnp.zeros((ROWS, c), F32) + bg_ref[...]
            val = jnp.zeros((ROWS, c), F32) + bv_ref[...]
            for j in range(kw):
                gate = gate + wg_ref[j:j + 1, :] * _rows(eg, HALO - (kw - 1) + j, ROWS)
                val = val + wv_ref[j:j + 1, :] * _rows(ev, HALO - (kw - 1) + j, ROWS)
            o_ref[pl.ds(base, ROWS), :] = (gate * _sigmoid(gate) * val).astype(BF16)
            return 0
        lax.fori_loop(0, _steps(t), step, 0)

    return _pcall(body, grid=(ns,),
                  in_specs=[pl.BlockSpec((t, c), lambda s: (0, s)), pl.BlockSpec((t, c), lambda s: (0, s + ns)),
                            pl.BlockSpec((kw, c), lambda s: (0, s)), pl.BlockSpec((kw, c), lambda s: (0, s + ns)),
                            pl.BlockSpec((1, c), lambda s: (0, s)), pl.BlockSpec((1, c), lambda s: (0, s + ns))],
                  out_specs=pl.BlockSpec((t, c), lambda s: (0, s)), out_shape=jax.ShapeDtypeStruct((t, f), BF16),
                  compiler_params=_params(("parallel",)), name=name)(u0, u0, dw_w, dw_w, dw_b, dw_b)


def _ffn_act_bwd(da, u0, dw_w, dw_b, name):
    t, f = da.shape
    c = LANES
    ns = f // c
    kw = dw_w.shape[0]

    def body(da_ref, g_ref, v_ref, wg_ref, wv_ref, bg_ref, bv_ref, du_ref, dww_ref, dwb_ref, padg, padv, accw, accb):
        padg[t:t + HALO, :] = jnp.zeros((HALO, c), F32)
        padv[t:t + HALO, :] = jnp.zeros((HALO, c), F32)
        accw[...] = jnp.zeros_like(accw)
        accb[...] = jnp.zeros_like(accb)

        def first(i, _):
            base = pl.multiple_of(i * ROWS, ROWS)
            lo = pl.multiple_of(jnp.maximum(base - HALO, 0), HALO)
            keep = jnp.where(i > 0, 1.0, 0.0)
            eg = jnp.concatenate([g_ref[pl.ds(lo, HALO), :] * keep, g_ref[pl.ds(base, ROWS), :]], axis=0)
            ev = jnp.concatenate([v_ref[pl.ds(lo, HALO), :] * keep, v_ref[pl.ds(base, ROWS), :]], axis=0)
            gate = jnp.zeros((ROWS, c), F32) + bg_ref[...]
            val = jnp.zeros((ROWS, c), F32) + bv_ref[...]
            sh_g, sh_v = [], []
            for j in range(kw):
                sh_g.append(_rows(eg, HALO - (kw - 1) + j, ROWS))
                sh_v.append(_rows(ev, HALO - (kw - 1) + j, ROWS))
                gate = gate + wg_ref[j:j + 1, :] * sh_g[j]
                val = val + wv_ref[j:j + 1, :] * sh_v[j]
            dav = da_ref[pl.ds(base, ROWS), :]
            sg = _sigmoid(gate)
            dgate = dav * val * (sg * (1.0 + gate * (1.0 - sg)))
            dval = dav * (gate * sg)
            padg[pl.ds(base, ROWS), :] = dgate
            padv[pl.ds(base, ROWS), :] = dval
            for j in range(kw):
                accw[j] = accw[j] + _fold8(dgate * sh_g[j])
                accw[kw + j] = accw[kw + j] + _fold8(dval * sh_v[j])
            accb[0] = accb[0] + _fold8(dgate)
            accb[1] = accb[1] + _fold8(dval)
            return 0
        lax.fori_loop(0, _steps(t), first, 0)

        def second(i, _):
            base = pl.multiple_of(i * ROWS, ROWS)
            eg = padg[pl.ds(base, ROWS + HALO), :]
            ev = padv[pl.ds(base, ROWS + HALO), :]
            dg = jnp.zeros((ROWS, c), F32)
            dv = jnp.zeros((ROWS, c), F32)
            for j in range(kw):
                dg = dg + wg_ref[j:j + 1, :] * _rows(eg, kw - 1 - j, ROWS)
                dv = dv + wv_ref[j:j + 1, :] * _rows(ev, kw - 1 - j, ROWS)
            du_ref[0, pl.ds(base, ROWS), :] = dg.astype(BF16)
            du_ref[1, pl.ds(base, ROWS), :] = dv.astype(BF16)
            return 0
        lax.fori_loop(0, _steps(t), second, 0)
        for j in range(kw):
            dww_ref[0, j:j + 1, :] = jnp.sum(accw[j], axis=0, keepdims=True)
            dww_ref[1, j:j + 1, :] = jnp.sum(accw[kw + j], axis=0, keepdims=True)
        dwb_ref[0] = jnp.sum(accb[0], axis=0, keepdims=True)
        dwb_ref[1] = jnp.sum(accb[1], axis=0, keepdims=True)

    return _pcall(body, grid=(ns,),
                  in_specs=[pl.BlockSpec((t, c), lambda s: (0, s)),
                            pl.BlockSpec((t, c), lambda s: (0, s)), pl.BlockSpec((t, c), lambda s: (0, s + ns)),
                            pl.BlockSpec((kw, c), lambda s: (0, s)), pl.BlockSpec((kw, c), lambda s: (0, s + ns)),
                            pl.BlockSpec((1, c), lambda s: (0, s)), pl.BlockSpec((1, c), lambda s: (0, s + ns))],
                  out_specs=[pl.BlockSpec((2, t, c), lambda s: (0, 0, s)), pl.BlockSpec((2, kw, c), lambda s: (0, 0, s)),
                             pl.BlockSpec((2, 1, c), lambda s: (0, 0, s))],
                  out_shape=[jax.ShapeDtypeStruct((2, t, f), BF16), jax.ShapeDtypeStruct((2, kw, f), F32),
                             jax.ShapeDtypeStruct((2, 1, f), F32)],
                  scratch_shapes=[pltpu.VMEM((t + HALO, c), F32), pltpu.VMEM((t + HALO, c), F32),
                                  pltpu.VMEM((2 * kw, 8, c), F32), pltpu.VMEM((2, 8, c), F32)],
                  compiler_params=_params(("parallel",)), name=name)(da, u0, u0, dw_w, dw_w, dw_b, dw_b)


def _window_of(group):
    w = jnp.float32(POOL_WINDOWS[-1])
    for k in range(len(POOL_WINDOWS) - 2, -1, -1):
        w = jnp.where(group == k, jnp.float32(POOL_WINDOWS[k]), w)
    return w


def _select_level(group, levels):
    out = levels[-1]
    for k in range(len(levels) - 2, -1, -1):
        out = jnp.where(group == k, levels[k], out)
    return out


def _pool_fwd(h, name):
    t, d = h.shape
    c = LANES
    per = d // len(POOL_WINDOWS) // c

    def body(h_ref, o_ref):
        group = pl.program_id(0)
        wf = _window_of(group)

        def step(i, _):
            base = pl.multiple_of(i * ROWS, ROWS)
            lo = pl.multiple_of(jnp.maximum(base - HALO, 0), HALO)
            keep = jnp.where(i > 0, 1.0, 0.0)
            cur = h_ref[pl.ds(base, ROWS), :]
            e = jnp.concatenate([h_ref[pl.ds(lo, HALO), :] * keep, cur], axis=0)
            n = ROWS + HALO
            levels = []
            s = e
            for k in range(len(POOL_WINDOWS)):
                s = s + pltpu.roll(s, 1 << k, 0)
                levels.append(s[HALO:n])
            tpos = (base + lax.broadcasted_iota(jnp.int32, (ROWS, c), 0) + 1).astype(F32)
            pooled = _select_level(group, levels) / jnp.minimum(tpos, wf)
            o_ref[pl.ds(base, ROWS), :] = (pooled - cur).astype(BF16)
            return 0
        lax.fori_loop(0, _steps(t), step, 0)

    return _pcall(body, grid=(len(POOL_WINDOWS), per), in_specs=[pl.BlockSpec((t, c), lambda g, s: (0, g * per + s))],
                  out_specs=pl.BlockSpec((t, c), lambda g, s: (0, g * per + s)), out_shape=jax.ShapeDtypeStruct((t, d), BF16),
                  compiler_params=_params(("parallel", "parallel")), name=name)(h)


def _pool_bwd(dm, name):
    t, d = dm.shape
    c = LANES
    per = d // len(POOL_WINDOWS) // c

    def body(dm_ref, o_ref, pad):
        group = pl.program_id(0)
        wf = _window_of(group)
        pad[t:t + HALO, :] = jnp.zeros((HALO, c), F32)

        def fill(i, _):
            base = pl.multiple_of(i * ROWS, ROWS)
            tpos = (base + lax.broadcasted_iota(jnp.int32, (ROWS, c), 0) + 1).astype(F32)
            pad[pl.ds(base, ROWS), :] = dm_ref[pl.ds(base, ROWS), :] / jnp.minimum(tpos, wf)
            return 0
        lax.fori_loop(0, _steps(t), fill, 0)

        def step(i, _):
            base = pl.multiple_of(i * ROWS, ROWS)
            n = ROWS + HALO
            s = pad[pl.ds(base, n), :]
            levels = []
            for k in range(len(POOL_WINDOWS)):
                s = s + pltpu.roll(s, n - (1 << k), 0)
                levels.append(s[0:ROWS])
            o_ref[pl.ds(base, ROWS), :] = _select_level(group, levels) - dm_ref[pl.ds(base, ROWS), :]
            return 0
        lax.fori_loop(0, _steps(t), step, 0)

    return _pcall(body, grid=(len(POOL_WINDOWS), per), in_specs=[pl.BlockSpec((t, c), lambda g, s: (0, g * per + s))],
                  out_specs=pl.BlockSpec((t, c), lambda g, s: (0, g * per + s)), out_shape=jax.ShapeDtypeStruct((t, d), F32),
                  scratch_shapes=[pltpu.VMEM((t + HALO, c), F32)], compiler_params=_params(("parallel", "parallel")), name=name)(dm)


def _pool_mm_fwd(mixed, wg, scale, res, name):
    t, d = mixed.shape
    ng, gd, _ = wg.shape
    tm = _pick(t, 1024, 16)

    def body(a_ref, w_ref, s_ref, r_ref, o_ref):
        y = jnp.dot(a_ref[...], w_ref[...], preferred_element_type=F32)
        o_ref[...] = r_ref[...] + y * s_ref[...]

    blk = pl.BlockSpec((tm, gd), lambda g, i: (i, g))
    return _pcall(body, grid=(ng, t // tm),
                  in_specs=[blk, pl.BlockSpec((None, gd, gd), lambda g, i: (g, 0, 0)), pl.BlockSpec((1, gd), lambda g, i: (0, g)), blk],
                  out_specs=blk, out_shape=jax.ShapeDtypeStruct((t, d), F32),
                  compiler_params=_params(("parallel", "parallel")), name=name)(mixed, wg, scale, res)


def _pool_mm_bwd(dy, mixed, wg, scale, name):
    t, d = mixed.shape
    ng, gd, _ = wg.shape
    tm = _pick(t, 1024, 16)

    def body(dy_ref, a_ref, w_ref, s_ref, dm_ref, dw_ref, ds_ref):
        a, w, dyv = a_ref[...], w_ref[...], dy_ref[...]
        y = jnp.dot(a, w, preferred_element_type=F32)
        dyp = (dyv * s_ref[...]).astype(BF16)
        dm_ref[...] = lax.dot_general(dyp, w, (((1,), (1,)), ((), ())), preferred_element_type=F32)
        pw = lax.dot_general(a, dyp, (((0,), (0,)), ((), ())), preferred_element_type=F32)
        ps = jnp.sum(dyv * y, axis=0, keepdims=True)

        @pl.when(pl.program_id(1) == 0)
        def _():
            dw_ref[...] = pw
            ds_ref[...] = ps

        @pl.when(pl.program_id(1) > 0)
        def _():
            dw_ref[...] += pw
            ds_ref[...] += ps

    blk = pl.BlockSpec((tm, gd), lambda g, i: (i, g))
    wsp = pl.BlockSpec((None, gd, gd), lambda g, i: (g, 0, 0))
    vec = pl.BlockSpec((1, gd), lambda g, i: (0, g))
    return _pcall(body, grid=(ng, t // tm), in_specs=[blk, blk, wsp, vec], out_specs=[blk, wsp, vec],
                  out_shape=[jax.ShapeDtypeStruct((t, d), F32), jax.ShapeDtypeStruct((ng, gd, gd), F32), jax.ShapeDtypeStruct((1, d), F32)],
                  compiler_params=_params(("parallel", "arbitrary")), name=name)(dy, mixed, wg, scale)


def _rope_tables(positions):
    half = ROT_DIM // 2
    inv_freq = ROPE_THETA ** (-jnp.arange(0, ROT_DIM, 2, dtype=F32) / ROT_DIM)
    ang = positions.astype(F32)[:, None] * inv_freq
    cos, sin = jnp.cos(ang), jnp.sin(ang)
    t = positions.shape[0]
    ones = jnp.ones((t, HEAD - ROT_DIM), F32)
    zeros = jnp.zeros((t, HEAD - ROT_DIM), F32)
    zh = jnp.zeros((t, half), F32)
    c = jnp.concatenate([cos, cos, ones], axis=1)
    s1 = jnp.concatenate([-sin, zh, zeros], axis=1)
    s2 = jnp.concatenate([zh, sin, zeros], axis=1)
    return tuple(jnp.concatenate([a, a], axis=1) for a in (c, s1, s2))


def _half_mean(v, lo):
    s_lo = jnp.sum(jnp.where(lo, v, 0.0), axis=-1, keepdims=True)
    s_hi = jnp.sum(jnp.where(lo, 0.0, v), axis=-1, keepdims=True)
    return jnp.where(lo, s_lo, s_hi) * (1.0 / HEAD)


def _qk_prep_fwd(qkv, tabs, gq2, gk2, n_q, n_kv, name):
    t, width = qkv.shape
    tm = _pick(t, 256, 16)
    nqc, nkc = n_q * HEAD // LANES, n_kv * HEAD // LANES

    def body(x_ref, c_ref, s1_ref, s2_ref, gq_ref, gk_ref, q_ref, k2_ref, v2_ref):
        lo = lax.broadcasted_iota(jnp.int32, (tm, LANES), 1) < HEAD
        cv, s1, s2 = c_ref[...], s1_ref[...], s2_ref[...]

        def normrot(xc, g2):
            y = xc * lax.rsqrt(_half_mean(xc * xc, lo) + EPS) * g2
            return y * cv + pltpu.roll(y, LANES - ROT_DIM // 2, 1) * s1 + pltpu.roll(y, ROT_DIM // 2, 1) * s2

        def twice(y, j):
            sw = pltpu.roll(y, HEAD, 1)
            k2 = jnp.where(lo, y, sw) if j == 0 else jnp.where(lo, sw, y)
            return k2.astype(BF16)

        for ch in range(nqc):
            q_ref[:, ch * LANES:(ch + 1) * LANES] = normrot(x_ref[:, ch * LANES:(ch + 1) * LANES], gq_ref[...]).astype(BF16)
        for ch in range(nkc):
            off = (nqc + ch) * LANES
            y = normrot(x_ref[:, off:off + LANES], gk_ref[...])
            voff = (nqc + nkc + ch) * LANES
            vv = x_ref[:, voff:voff + LANES]
            for j in range(2):
                k2_ref[:, (2 * ch + j) * LANES:(2 * ch + j + 1) * LANES] = twice(y, j)
                v2_ref[:, (2 * ch + j) * LANES:(2 * ch + j + 1) * LANES] = twice(vv, j)

    row = lambda w: pl.BlockSpec((tm, w), lambda i: (i, 0))
    vec = pl.BlockSpec((1, LANES), lambda i: (0, 0))
    return _pcall(body, grid=(t // tm,), in_specs=[row(width), row(LANES), row(LANES), row(LANES), vec, vec],
                  out_specs=[row(n_q * HEAD), row(n_kv * LANES), row(n_kv * LANES)],
                  out_shape=[jax.ShapeDtypeStruct((t, n_q * HEAD), BF16), jax.ShapeDtypeStruct((t, n_kv * LANES), BF16),
                             jax.ShapeDtypeStruct((t, n_kv * LANES), BF16)],
                  compiler_params=_params(("parallel",)), name=name)(qkv, *tabs, gq2, gk2)


def _qk_prep_bwd(dq, dk_cur, dk_prev, dv_cur, dv_prev, qkv, tabs, gq2, gk2, n_q, n_kv, name):
    t, width = qkv.shape
    tm = Q_BLOCK
    nb = t // tm
    nqc, nkc = n_q * HEAD // LANES, n_kv * HEAD // LANES

    def body(dq_ref, kc_ref, kp_ref, vc_ref, vp_ref, x_ref, c_ref, s1_ref, s2_ref, gq_ref, gk_ref, o_ref, dgq_ref, dgk_ref):
        lo = lax.broadcasted_iota(jnp.int32, (tm, LANES), 1) < HEAD
        cv, s1, s2 = c_ref[...], s1_ref[...], s2_ref[...]
        more = jnp.where(pl.program_id(0) < nb - 1, 1.0, 0.0)

        def back(dy, xc, g2):
            dyn = dy * cv + pltpu.roll(dy * s1, ROT_DIM // 2, 1) + pltpu.roll(dy * s2, LANES - ROT_DIM // 2, 1)
            r = lax.rsqrt(_half_mean(xc * xc, lo) + EPS)
            xh = xc * r
            dxh = dyn * g2
            return r * (dxh - xh * _half_mean(dxh * xh, lo)), jnp.sum(dyn * xh, axis=0, keepdims=True)

        def unfold(cur_ref, prev_ref, ch):
            d0 = cur_ref[:, (2 * ch) * LANES:(2 * ch + 1) * LANES] + more * prev_ref[:, (2 * ch) * LANES:(2 * ch + 1) * LANES]
            d1 = cur_ref[:, (2 * ch + 1) * LANES:(2 * ch + 2) * LANES] + more * prev_ref[:, (2 * ch + 1) * LANES:(2 * ch + 2) * LANES]
            return jnp.where(lo, d0 + pltpu.roll(d0, HEAD, 1), d1 + pltpu.roll(d1, HEAD, 1))

        pq = jnp.zeros((1, LANES), F32)
        for ch in range(nqc):
            sl = slice(ch * LANES, (ch + 1) * LANES)
            dx, pg = back(dq_ref[:, sl], x_ref[:, sl], gq_ref[...])
            o_ref[:, sl] = dx.astype(BF16)
            pq = pq + pg
        pk = jnp.zeros((1, LANES), F32)
        for ch in range(nkc):
            sl = slice((nqc + ch) * LANES, (nqc + ch + 1) * LANES)
            dx, pg = back(unfold(kc_ref, kp_ref, ch), x_ref[:, sl], gk_ref[...])
            o_ref[:, sl] = dx.astype(BF16)
            pk = pk + pg
            vs = slice((nqc + nkc + ch) * LANES, (nqc + nkc + ch + 1) * LANES)
            o_ref[:, vs] = unfold(vc_ref, vp_ref, ch).astype(BF16)

        @pl.when(pl.program_id(0) == 0)
        def _():
            dgq_ref[...] = pq
            dgk_ref[...] = pk

        @pl.when(pl.program_id(0) > 0)
        def _():
            dgq_ref[...] += pq
            dgk_ref[...] += pk

    row = lambda w: pl.BlockSpec((tm, w), lambda i: (i, 0))
    nxt = lambda w: pl.BlockSpec((tm, w), lambda i: (jnp.minimum(i + 1, nb - 1), 0))
    vec = pl.BlockSpec((1, LANES), lambda i: (0, 0))
    kvw = n_kv * LANES
    return _pcall(body, grid=(nb,),
                  in_specs=[row(n_q * HEAD), row(kvw), nxt(kvw), row(kvw), nxt(kvw), row(width), row(LANES), row(LANES), row(LANES), vec, vec],
                  out_specs=[row(width), vec, vec],
                  out_shape=[jax.ShapeDtypeStruct((t, width), BF16), jax.ShapeDtypeStruct((1, LANES), F32), jax.ShapeDtypeStruct((1, LANES), F32)],
                  compiler_params=_params(("arbitrary",)), name=name)(dq, dk_cur, dk_prev, dv_cur, dv_prev, qkv, *tabs, gq2, gk2)


def _band_scores(qh, kc, kp, n, sink_row, lo_row, is_lo):
    scale = 1.0 / math.sqrt(HEAD)
    nt = (((1,), (1,)), ((), ()))
    s_c = lax.dot_general(qh, kc, nt, preferred_element_type=F32) * scale
    s_p = lax.dot_general(qh, kp, nt, preferred_element_type=F32) * scale
    qi = lax.broadcasted_iota(jnp.int32, (Q_BLOCK, Q_BLOCK), 0)
    kj = lax.broadcasted_iota(jnp.int32, (Q_BLOCK, Q_BLOCK), 1)
    s_c = jnp.where(kj <= qi, s_c, -jnp.inf)
    s_p = jnp.where((kj > qi) & (n > 0), s_p, -jnp.inf)
    pick = lo_row if is_lo else jnp.logical_not(lo_row)
    sink = jnp.max(jnp.where(pick, sink_row, -jnp.inf), axis=-1, keepdims=True)
    return s_c, s_p, sink


def _attn_fwd(q, k2, v2, sink_tab, name):
    t, dq = q.shape
    nc = dq // LANES
    nb = t // Q_BLOCK
    per_kv = nc // (k2.shape[1] // LANES)

    def body(q_ref, kc_ref, kp_ref, vc_ref, vp_ref, s_ref, o_ref, lse_ref):
        n = pl.program_id(1)
        lo = lax.broadcasted_iota(jnp.int32, (Q_BLOCK, LANES), 1) < HEAD
        lo_row = lax.broadcasted_iota(jnp.int32, (1, LANES), 1) < HEAD
        qv = q_ref[...].astype(F32)
        kc, kp, vc, vp = kc_ref[...], kp_ref[...], vc_ref[...], vp_ref[...]
        outs, lses = [], []
        for is_lo in (True, False):
            qh = jnp.where(lo, qv, 0.0) if is_lo else jnp.where(lo, 0.0, qv)
            s_c, s_p, sink = _band_scores(qh.astype(BF16), kc, kp, n, s_ref[0:1, :], lo_row, is_lo)
            m = jnp.maximum(jnp.maximum(jnp.max(s_c, axis=-1, keepdims=True), jnp.max(s_p, axis=-1, keepdims=True)), sink)
            p_c, p_p = jnp.exp(s_c - m), jnp.exp(s_p - m)
            denom = jnp.sum(p_c, axis=-1, keepdims=True) + jnp.sum(p_p, axis=-1, keepdims=True) + jnp.exp(sink - m)
            pv = jnp.dot(p_c.astype(BF16), vc, preferred_element_type=F32) + jnp.dot(p_p.astype(BF16), vp, preferred_element_type=F32)
            outs.append(pv / denom)
            lses.append(m + jnp.log(denom))
        o_ref[...] = jnp.where(lo, outs[0], outs[1]).astype(BF16)
        lse_ref[...] = jnp.where(lo, lses[0], lses[1])

    qs = pl.BlockSpec((Q_BLOCK, LANES), lambda c, n: (n, c))
    cur = pl.BlockSpec((Q_BLOCK, LANES), lambda c, n: (n, c // per_kv))
    prev = pl.BlockSpec((Q_BLOCK, LANES), lambda c, n: (jnp.maximum(n - 1, 0), c // per_kv))
    return _pcall(body, grid=(nc, nb),
                  in_specs=[qs, cur, prev, cur, prev, pl.BlockSpec((8, LANES), lambda c, n: (c, 0))],
                  out_specs=[qs, pl.BlockSpec((None, Q_BLOCK, LANES), lambda c, n: (c, n, 0))],
                  out_shape=[jax.ShapeDtypeStruct((t, dq), BF16), jax.ShapeDtypeStruct((nc, t, LANES), F32)],
                  compiler_params=_params(("parallel", "parallel")), name=name)(q, k2, k2, v2, v2, sink_tab)


def _attn_bwd(do, q, o, lse, k2, v2, sink_tab, name):
    t, dq = q.shape
    nc = dq // LANES
    nb = t // Q_BLOCK
    nkv = k2.shape[1] // LANES
    per_kv = nc // nkv
    scale = 1.0 / math.sqrt(HEAD)
    tn_ = (((0,), (0,)), ((), ()))
    nt = (((1,), (1,)), ((), ()))

    def body(do_ref, q_ref, o_ref, lse_ref, kc_ref, kp_ref, vc_ref, vp_ref, s_ref,
             dq_ref, dkc_ref, dkp_ref, dvc_ref, dvp_ref, dsk_ref):
        n = pl.program_id(1)
        cc = pl.program_id(2)
        lo = lax.broadcasted_iota(jnp.int32, (Q_BLOCK, LANES), 1) < HEAD
        lo_row = lax.broadcasted_iota(jnp.int32, (1, LANES), 1) < HEAD
        qv, dov, ov, lsev = q_ref[...].astype(F32), do_ref[...], o_ref[...].astype(F32), lse_ref[...]
        kc, kp, vc, vp = kc_ref[...], kp_ref[...], vc_ref[...], vp_ref[...]
        dqs, dsinks = [], []
        dkc = dkp = dvc = dvp = None
        for is_lo in (True, False):
            half = lo if is_lo else jnp.logical_not(lo)
            qh = jnp.where(half, qv, 0.0).astype(BF16)
            doh = jnp.where(half, dov, 0.0)
            s_c, s_p, sink = _band_scores(qh, kc, kp, n, s_ref[0:1, :], lo_row, is_lo)
            lse_h = jnp.max(jnp.where(half, lsev, -jnp.inf), axis=-1, keepdims=True)
            p_c, p_p = jnp.exp(s_c - lse_h), jnp.exp(s_p - lse_h)
            delta = jnp.sum(doh * ov, axis=-1, keepdims=True)
            dob = doh.astype(BF16)
            ds_c = (p_c * (lax.dot_general(dob, vc, nt, preferred_element_type=F32) - delta)).astype(BF16)
            ds_p = (p_p * (lax.dot_general(dob, vp, nt, preferred_element_type=F32) - delta)).astype(BF16)
            dsinks.append(-jnp.sum(jnp.exp(sink - lse_h) * delta, axis=0, keepdims=True))
            dqs.append((jnp.dot(ds_c, kc, preferred_element_type=F32) + jnp.dot(ds_p, kp, preferred_element_type=F32)) * scale)
            parts = (lax.dot_general(ds_c, qh, tn_, preferred_element_type=F32) * scale,
                     lax.dot_general(ds_p, qh, tn_, preferred_element_type=F32) * scale,
                     lax.dot_general(p_c.astype(BF16), dob, tn_, preferred_element_type=F32),
                     lax.dot_general(p_p.astype(BF16), dob, tn_, preferred_element_type=F32))
            if dkc is None:
                dkc, dkp, dvc, dvp = parts
            else:
                dkc, dkp, dvc, dvp = dkc + parts[0], dkp + parts[1], dvc + parts[2], dvp + parts[3]
        dq_ref[...] = jnp.where(lo, dqs[0], dqs[1])
        dsk_ref[...] = jnp.zeros((8, LANES), F32) + jnp.where(lo_row, dsinks[0], dsinks[1])

        @pl.when(cc == 0)
        def _():
            dkc_ref[...] = dkc
            dkp_ref[...] = dkp
            dvc_ref[...] = dvc
            dvp_ref[...] = dvp

        @pl.when(cc > 0)
        def _():
            dkc_ref[...] += dkc
            dkp_ref[...] += dkp
            dvc_ref[...] += dvc
            dvp_ref[...] += dvp

    qs = pl.BlockSpec((Q_BLOCK, LANES), lambda k, n, cc: (n, k * per_kv + cc))
    cur = pl.BlockSpec((Q_BLOCK, LANES), lambda k, n, cc: (n, k))
    prev = pl.BlockSpec((Q_BLOCK, LANES), lambda k, n, cc: (jnp.maximum(n - 1, 0), k))
    kv_shape = jax.ShapeDtypeStruct((t, nkv * LANES), F32)
    return _pcall(body, grid=(nkv, nb, per_kv),
                  in_specs=[qs, qs, qs, pl.BlockSpec((None, Q_BLOCK, LANES), lambda k, n, cc: (k * per_kv + cc, n, 0)),
                            cur, prev, cur, prev, pl.BlockSpec((8, LANES), lambda k, n, cc: (k * per_kv + cc, 0))],
                  out_specs=[qs, cur, cur, cur, cur, pl.BlockSpec((None, 8, LANES), lambda k, n, cc: ((k * per_kv + cc) * nb + n, 0, 0))],
                  out_shape=[jax.ShapeDtypeStruct((t, dq), F32), kv_shape, kv_shape, kv_shape, kv_shape,
                             jax.ShapeDtypeStruct((nc * nb, 8, LANES), F32)],
                  compiler_params=_params(("parallel", "parallel", "arbitrary")), name=name)(do, q, o, lse, k2, k2, v2, v2, sink_tab)


def _peer(k):
    x, y, c = lax.axis_index("x"), lax.axis_index("y"), lax.axis_index("c")
    flip = lambda v, bit: 1 - v if bit else v
    return (flip(x, k & 4), flip(y, k & 2), flip(c, k & 1))


def _my_index():
    return 4 * lax.axis_index("x") + 2 * lax.axis_index("y") + lax.axis_index("c")


def _peer_index(k):
    px, py, pc = _peer(k)
    return 4 * px + 2 * py + pc


def _all_gather(shards, name):
    n = len(shards)
    any_spec = pl.BlockSpec(memory_space=pl.ANY)

    def body(*refs):
        ins, outs = refs[:n], refs[n:2 * n]
        send_sems, recv_sems, local_sems = refs[2 * n:]
        me = _my_index()
        local = [pltpu.make_async_copy(ins[a], outs[a].at[me], local_sems.at[a]) for a in range(n)]
        for cp in local:
            cp.start()
        sends = []
        for k in range(1, N_DEV):
            for a in range(n):
                cp = pltpu.make_async_remote_copy(src_ref=ins[a], dst_ref=outs[a].at[me], send_sem=send_sems.at[a, k - 1],
                                                  recv_sem=recv_sems.at[a, k - 1], device_id=_peer(k), device_id_type=MESH_ID)
                cp.start()
                sends.append(cp)
        for k in range(1, N_DEV):
            for a in range(n):
                pltpu.make_async_remote_copy(src_ref=ins[a], dst_ref=outs[a].at[_peer_index(k)], send_sem=send_sems.at[a, k - 1],
                                             recv_sem=recv_sems.at[a, k - 1], device_id=_peer(k), device_id_type=MESH_ID).wait_recv()
        for cp in sends:
            cp.wait_send()
        for cp in local:
            cp.wait()

    return _pcall(body, in_specs=[any_spec] * n, out_specs=[any_spec] * n,
                  out_shape=[jax.ShapeDtypeStruct((N_DEV,) + s.shape, s.dtype) for s in shards],
                  scratch_shapes=[pltpu.SemaphoreType.DMA((n, N_DEV - 1)), pltpu.SemaphoreType.DMA((n, N_DEV - 1)),
                                  pltpu.SemaphoreType.DMA((n,))],
                  name=name)(*shards)


def _scatter_exchange(stacks, name):
    n = len(stacks)
    any_spec = pl.BlockSpec(memory_space=pl.ANY)

    def body(*refs):
        ins, outs = refs[:n], refs[n:2 * n]
        send_sems, recv_sems, local_sems = refs[2 * n:]
        me = _my_index()
        local = [pltpu.make_async_copy(ins[a].at[me], outs[a].at[me], local_sems.at[a]) for a in range(n)]
        for cp in local:
            cp.start()
        sends = []
        for k in range(1, N_DEV):
            for a in range(n):
                cp = pltpu.make_async_remote_copy(src_ref=ins[a].at[_peer_index(k)], dst_ref=outs[a].at[me], send_sem=send_sems.at[a, k - 1],
                                                  recv_sem=recv_sems.at[a, k - 1], device_id=_peer(k), device_id_type=MESH_ID)
                cp.start()
                sends.append(cp)
        for k in range(1, N_DEV):
            for a in range(n):
                pltpu.make_async_remote_copy(src_ref=ins[a].at[me], dst_ref=outs[a].at[_peer_index(k)], send_sem=send_sems.at[a, k - 1],
                                             recv_sem=recv_sems.at[a, k - 1], device_id=_peer(k), device_id_type=MESH_ID).wait_recv()
        for cp in sends:
            cp.wait_send()
        for cp in local:
            cp.wait()

    return _pcall(body, in_specs=[any_spec] * n, out_specs=[any_spec] * n,
                  out_shape=[jax.ShapeDtypeStruct(s.shape, s.dtype) for s in stacks],
                  scratch_shapes=[pltpu.SemaphoreType.DMA((n, N_DEV - 1)), pltpu.SemaphoreType.DMA((n, N_DEV - 1)),
                                  pltpu.SemaphoreType.DMA((n,))],
                  name=name)(*stacks)


def _adam(g, w, m, v):
    m2 = ADAM_B1 * m + (1.0 - ADAM_B1) * g
    v2 = ADAM_B2 * v + (1.0 - ADAM_B2) * (g * g)
    m_hat = m2 / (1.0 - ADAM_B1 ** ADAM_STEP)
    v_hat = v2 / (1.0 - ADAM_B2 ** ADAM_STEP)
    delta = -ADAM_LR * (m_hat / (jnp.sqrt(v_hat) + ADAM_EPS) + ADAM_WD * w)
    return delta, m2, v2


def _sum_adam(parts, w, m, v, name):
    r, c = w.shape
    tr = _pick(r, max(8, (1 << 19) // c), 8)

    def body(p_ref, w_ref, m_ref, v_ref, g_ref, d_ref, m2_ref, v2_ref):
        g = p_ref[0].astype(F32)
        for j in range(1, N_DEV):
            g = g + p_ref[j].astype(F32)
        delta, m2, v2 = _adam(g, w_ref[...], m_ref[...], v_ref[...])
        g_ref[...] = g
        d_ref[...] = delta
        m2_ref[...] = m2
        v2_ref[...] = v2

    blk = pl.BlockSpec((tr, c), lambda i: (i, 0))
    shp = jax.ShapeDtypeStruct((r, c), F32)
    return _pcall(body, grid=(r // tr,), in_specs=[pl.BlockSpec((N_DEV, tr, c), lambda i: (0, i, 0)), blk, blk, blk],
                  out_specs=[blk] * 4, out_shape=[shp] * 4, compiler_params=_params(("parallel",)), name=name)(parts, w, m, v)


def _small_step(rep_parts, rep_w, rep_m, rep_v, sh_parts, sh_w, sh_m, sh_v, name):
    nr, ns = len(rep_parts), len(sh_parts)
    rows_r = [-(-p.shape[1] // LANES) for p in rep_parts]
    off_r = [sum(rows_r[:i]) for i in range(nr)]
    tot_r = -(-max(sum(rows_r), 8) // 8) * 8
    rows_s = [-(-p.shape[1] // 8) * 8 for p in sh_parts]
    cmax = max([p.shape[2] for p in sh_parts] + [LANES])
    off_s = [sum(rows_s[:i]) for i in range(ns)]
    tot_s = max(sum(rows_s), 8)
    vm = pl.BlockSpec(memory_space=pltpu.VMEM)

    def body(*refs):
        pos = 0

        def take(k):
            nonlocal pos
            out = refs[pos:pos + k]
            pos += k
            return out
        rp, rw, rm, rv = take(nr), take(nr), take(nr), take(nr)
        sp, sw, sm, sv = take(ns), take(ns), take(ns), take(ns)
        rg, rd, rm2, rv2 = take(nr), take(nr), take(nr), take(nr)
        sg, sd, sm2, sv2 = take(ns), take(ns), take(ns), take(ns)
        pack_r, got_r, pack_s, got_s, send_r, recv_r, send_s, recv_s = take(8)
        me = _my_index()
        pack_r[...] = jnp.zeros_like(pack_r)
        pack_s[...] = jnp.zeros_like(pack_s)
        for i in range(nr):
            nfull = rep_parts[i].shape[1]
            for rr in range(rows_r[i]):
                wdt = min(LANES, nfull - rr * LANES)
                pack_r[off_r[i] + rr:off_r[i] + rr + 1, 0:wdt] = rp[i][0:1, rr * LANES:rr * LANES + wdt]
        for i in range(ns):
            _, r_i, c_i = sh_parts[i].shape
            for j in range(N_DEV):
                pack_s[j, off_s[i]:off_s[i] + r_i, 0:c_i] = sp[i][j]
        got_r[me] = pack_r[...]
        got_s[me] = pack_s[me]
        sends = []
        for k in range(1, N_DEV):
            a = pltpu.make_async_remote_copy(src_ref=pack_r, dst_ref=got_r.at[me], send_sem=send_r.at[k - 1], recv_sem=recv_r.at[k - 1],
                                             device_id=_peer(k), device_id_type=MESH_ID)
            b = pltpu.make_async_remote_copy(src_ref=pack_s.at[_peer_index(k)], dst_ref=got_s.at[me], send_sem=send_s.at[k - 1],
                                             recv_sem=recv_s.at[k - 1], device_id=_peer(k), device_id_type=MESH_ID)
            a.start()
            b.start()
            sends += [a, b]
        for k in range(1, N_DEV):
            pltpu.make_async_remote_copy(src_ref=pack_r, dst_ref=got_r.at[_peer_index(k)], send_sem=send_r.at[k - 1],
                                         recv_sem=recv_r.at[k - 1], device_id=_peer(k), device_id_type=MESH_ID).wait_recv()
            pltpu.make_async_remote_copy(src_ref=pack_s.at[me], dst_ref=got_s.at[_peer_index(k)], send_sem=send_s.at[k - 1],
                                         recv_sem=recv_s.at[k - 1], device_id=_peer(k), device_id_type=MESH_ID).wait_recv()
        for cp in sends:
            cp.wait_send()
        tot_rep = got_r[0]
        tot_sh = got_s[0]
        for j in range(1, N_DEV):
            tot_rep = tot_rep + got_r[j]
            tot_sh = tot_sh + got_s[j]
        pack_r[...] = tot_rep
        pack_s[0] = tot_sh
        for i in range(nr):
            nfull = rep_parts[i].shape[1]
            for rr in range(rows_r[i]):
                wdt = min(LANES, nfull - rr * LANES)
                rg[i][0:1, rr * LANES:rr * LANES + wdt] = pack_r[off_r[i] + rr:off_r[i] + rr + 1, 0:wdt]
            g = rg[i][...]
            delta, m2, v2 = _adam(g, rw[i][...], rm[i][...], rv[i][...])
            rd[i][...] = delta
            rm2[i][...] = m2
            rv2[i][...] = v2
        for i in range(ns):
            _, r_i, c_i = sh_parts[i].shape
            g = pack_s[0, off_s[i]:off_s[i] + r_i, 0:c_i]
            delta, m2, v2 = _adam(g, sw[i][...], sm[i][...], sv[i][...])
            sg[i][...] = g
            sd[i][...] = delta
            sm2[i][...] = m2
            sv2[i][...] = v2

    rep_shapes = [jax.ShapeDtypeStruct(p.shape, F32) for p in rep_parts]
    sh_shapes = [jax.ShapeDtypeStruct(p.shape[1:], F32) for p in sh_parts]
    n_in = 4 * nr + 4 * ns
    outs = _pcall(body, in_specs=[vm] * n_in, out_specs=[vm] * n_in, out_shape=rep_shapes * 4 + sh_shapes * 4,
                  scratch_shapes=[pltpu.VMEM((tot_r, LANES), F32), pltpu.VMEM((N_DEV, tot_r, LANES), F32),
                                  pltpu.VMEM((N_DEV, tot_s, cmax), F32), pltpu.VMEM((N_DEV, tot_s, cmax), F32),
                                  pltpu.SemaphoreType.DMA((N_DEV - 1,)), pltpu.SemaphoreType.DMA((N_DEV - 1,)),
                                  pltpu.SemaphoreType.DMA((N_DEV - 1,)), pltpu.SemaphoreType.DMA((N_DEV - 1,))],
                  compiler_params=_params(), name=name)(
        *rep_parts, *rep_w, *rep_m, *rep_v, *sh_parts, *sh_w, *sh_m, *sh_v)
    rep_out = [outs[i * nr:(i + 1) * nr] for i in range(4)]
    sh_out = [outs[4 * nr + i * ns:4 * nr + (i + 1) * ns] for i in range(4)]
    return rep_out, sh_out


_CONF = ("norm_g", "a_w_in", "a_b_in", "a_dw_w", "a_dw_b", "a_ln_g", "a_ln_b", "a_w_out", "a_b_out")
_FFN = ("ffn_norm_g", "ffn_w_up", "ffn_dw_w", "ffn_dw_b", "ffn_w_down")
_POOL = ("norm_g", "b_w_group", "b_scale")
_ATTN = ("norm_g", "c_w_qkv", "c_q_norm_g", "c_k_norm_g", "c_sinks", "c_w_o")
_LAYERS = (_CONF + _FFN, _POOL + _FFN, _ATTN + _FFN, _CONF + _FFN)
_NAMES = tuple("l%d_%s" % (i, n) for i, names in enumerate(_LAYERS) for n in names)
_BIG = ("a_w_in", "a_w_out", "ffn_w_up", "ffn_w_down", "b_w_group", "c_w_qkv", "c_w_o")
_SHARDED_SMALL = ("a_dw_w", "ffn_dw_w")


def _pad_rows(a, mult=8):
    r = a.shape[0]
    rp = -(-r // mult) * mult
    return a if rp == r else jnp.pad(a, ((0, rp - r), (0, 0)))


def _unstack_cols(st, rows):
    s, r, cs = st.shape
    return jnp.transpose(st, (1, 0, 2)).reshape(r, s * cs)[:rows]


def _stack_cols(a):
    r, c = a.shape
    return jnp.transpose(a.reshape(r, N_DEV, c // N_DEV), (1, 0, 2))


def _row(v):
    return v.reshape(1, -1)


def _ffn_forward(x_mid, p, tag):
    h2 = _rms_fwd(x_mid, _row(p["ffn_norm_g"]), BF16, "rms_fwd_bf16")
    u0 = _mm(h2, p["ffn_w_up"], "nn", name="ffn_up", tn=1408, tk=2048)
    a = _ffn_act_fwd(u0, p["ffn_dw_w"], _row(p["ffn_dw_b"]), "ffn_act_fwd")
    x_out = _mm(a, p["ffn_w_down"], "nn", res=x_mid, name="ffn_down", tk=1408)
    return x_out, dict(h2=h2, u0=u0, a=a)


def _ffn_backward(dx_out, x_mid, p, sv, grads):
    da = _mm(dx_out, p["ffn_w_down"], "nt", name="ffn_down_dx", tn=1408, tk=2048)
    dwd = _mm(sv["a"], dx_out, "tn", out_dtype=BF16, name="ffn_down_dw", tm=1408)
    grads["ffn_w_down"] = dwd.reshape(N_DEV, dwd.shape[0] // N_DEV, dwd.shape[1])
    du0, dww, dwb = _ffn_act_bwd(da, sv["u0"], p["ffn_dw_w"], _row(p["ffn_dw_b"]), "ffn_act_bwd")
    kw = dww.shape[1]
    grads["ffn_dw_w"] = _stack_cols(jnp.transpose(dww, (1, 0, 2)).reshape(kw, -1))
    grads["ffn_dw_b"] = dwb.reshape(1, -1)
    dh2 = _mm(du0, p["ffn_w_up"], "nt", name="ffn_up_dx", tk=1408)
    grads["ffn_w_up"] = _mm(sv["h2"], du0, "tn", out_dtype=BF16, out_stack=N_DEV, name="ffn_up_dw", tn=1408)
    dx_mid, dg, _ = _rms_bwd(dh2, x_mid, _row(p["ffn_norm_g"]), dx_out, "rms_bwd")
    grads["ffn_norm_g"] = dg
    return dx_mid


def _conf_forward(x, p):
    h = _rms_fwd(x, _row(p["norm_g"]), BF16, "rms_fwd_bf16")
    u = _mm(h, p["a_w_in"], "nn", bias=_row(p["a_b_in"]), name="conf_in", tn=512, tk=2048)
    cpre = _conf_conv_fwd(u, p["a_dw_w"], _row(p["a_dw_b"]), "conf_conv_fwd")
    s = _ln_silu_fwd(cpre, _row(p["a_ln_g"]), _row(p["a_ln_b"]), "ln_silu_fwd")
    x_mid = _mm(s, p["a_w_out"], "nn", bias=_row(p["a_b_out"]), res=x, name="conf_out", tk=2048)
    return x_mid, dict(h=h, u=u, cpre=cpre, s=s)


def _conf_backward(dx_mid, x, p, sv, grads):
    ds = _mm(dx_mid, p["a_w_out"], "nt", name="conf_out_dx", tk=2048)
    dwo = _mm(sv["s"], dx_mid, "tn", out_dtype=BF16, name="conf_out_dw")
    grads["a_w_out"] = dwo.reshape(N_DEV, dwo.shape[0] // N_DEV, dwo.shape[1])
    dc, dlg, dlb = _ln_silu_bwd(ds, sv["cpre"], _row(p["a_ln_g"]), _row(p["a_ln_b"]), "ln_silu_bwd")
    grads["a_ln_g"], grads["a_ln_b"] = dlg, dlb
    du, dww, dwb, dbin = _conf_conv_bwd(dc, sv["u"], p["a_dw_w"], "conf_conv_bwd")
    grads["a_dw_w"] = _stack_cols(dww)
    grads["a_dw_b"] = dwb
    grads["a_b_in"] = dbin.reshape(1, -1)
    dh = _mm(du, p["a_w_in"], "nt", name="conf_in_dx", tk=512)
    grads["a_w_in"] = _mm(sv["h"], du, "tn", out_dtype=BF16, out_stack=N_DEV, name="conf_in_dw", tn=512)
    dx, dg, dbo = _rms_bwd(dh, x, _row(p["norm_g"]), dx_mid, "rms_bwd")
    grads["norm_g"] = dg
    grads["a_b_out"] = dbo
    return dx


def _pool_forward(x, p):
    h = _rms_fwd(x, _row(p["norm_g"]), F32, "rms_fwd_f32")
    mixed = _pool_fwd(h, "pool_fwd")
    x_mid = _pool_mm_fwd(mixed, p["b_w_group"], _row(p["b_scale"]), x, "pool_mm_fwd")
    return x_mid, dict(mixed=mixed)


def _pool_backward(dx_mid, x, p, sv, grads):
    dmixed, dwg, dscale = _pool_mm_bwd(dx_mid, sv["mixed"], p["b_w_group"], _row(p["b_scale"]), "pool_mm_bwd")
    ng, gd, _ = dwg.shape
    grads["b_w_group"] = jnp.transpose(dwg.reshape(ng, N_DEV, gd // N_DEV, gd), (1, 0, 2, 3)).reshape(N_DEV, ng * gd // N_DEV, gd).astype(BF16)
    grads["b_scale"] = dscale
    dh = _pool_bwd(dmixed, "pool_bwd")
    dx, dg, _ = _rms_bwd(dh, x, _row(p["norm_g"]), dx_mid, "rms_bwd")
    grads["norm_g"] = dg
    return dx


def _attn_tables(p, positions):
    n_q = p["c_w_o"].shape[0] // HEAD
    n_kv = n_q // 8
    tabs = _rope_tables(positions)
    gq2 = jnp.concatenate([p["c_q_norm_g"], p["c_q_norm_g"]]).reshape(1, LANES)
    gk2 = jnp.concatenate([p["c_k_norm_g"], p["c_k_norm_g"]]).reshape(1, LANES)
    sink_tab = jnp.repeat(jnp.repeat(p["c_sinks"].reshape(-1, 2), HEAD, axis=1), 8, axis=0)
    return n_q, n_kv, tabs, gq2, gk2, sink_tab


def _attn_forward(x, p, positions):
    n_q, n_kv, tabs, gq2, gk2, sink_tab = _attn_tables(p, positions)
    h = _rms_fwd(x, _row(p["norm_g"]), BF16, "rms_fwd_bf16")
    qkv = _mm(h, p["c_w_qkv"], "nn", name="attn_qkv", tn=1280, tk=2048)
    q, k2, v2 = _qk_prep_fwd(qkv, tabs, gq2, gk2, n_q, n_kv, "qk_prep_fwd")
    o, lse = _attn_fwd(q, k2, v2, sink_tab, "attn_fwd")
    x_mid = _mm(o, p["c_w_o"], "nn", res=x, name="attn_out", tk=2048)
    return x_mid, dict(h=h, qkv=qkv, q=q, k2=k2, v2=v2, o=o, lse=lse)


def _attn_backward(dx_mid, x, p, positions, sv, grads):
    n_q, n_kv, tabs, gq2, gk2, sink_tab = _attn_tables(p, positions)
    do = _mm(dx_mid, p["c_w_o"], "nt", name="attn_out_dx", tk=2048)
    dwo = _mm(sv["o"], dx_mid, "tn", out_dtype=BF16, name="attn_out_dw")
    grads["c_w_o"] = dwo.reshape(N_DEV, dwo.shape[0] // N_DEV, dwo.shape[1])
    dq, dkc, dkp, dvc, dvp, dsk = _attn_bwd(do, sv["q"], sv["o"], sv["lse"], sv["k2"], sv["v2"], sink_tab, "attn_bwd")
    nb = x.shape[0] // Q_BLOCK
    dsk = dsk.reshape(-1, nb, 8, LANES)[:, :, 0, :].sum(axis=1)
    grads["c_sinks"] = jnp.stack([dsk[:, 0], dsk[:, HEAD]], axis=1).reshape(1, -1)
    dqkv, dgq, dgk = _qk_prep_bwd(dq, dkc, dkp, dvc, dvp, sv["qkv"], tabs, gq2, gk2, n_q, n_kv, "qk_prep_bwd")
    grads["c_q_norm_g"] = dgq[:, :HEAD] + dgq[:, HEAD:]
    grads["c_k_norm_g"] = dgk[:, :HEAD] + dgk[:, HEAD:]
    dh = _mm(dqkv, p["c_w_qkv"], "nt", name="attn_qkv_dx", tk=1280)
    dwq = _mm(sv["h"], dqkv, "tn", out_dtype=BF16, name="attn_qkv_dw", tn=1280)
    grads["c_w_qkv"] = _stack_cols(dwq)
    dx, dg, _ = _rms_bwd(dh, x, _row(p["norm_g"]), dx_mid, "rms_bwd")
    grads["norm_g"] = dg
    return dx


def _gather_layer(raw, names, tag):
    order, shards = [], []
    for nme in names:
        w = raw[nme]
        if nme in _BIG:
            order.append(nme)
            shards.append(w.astype(BF16).reshape(-1, w.shape[-1]) if w.ndim == 3 else w.astype(BF16))
        elif nme in _SHARDED_SMALL:
            order.append(nme)
            shards.append(_pad_rows(w))
    got = dict(zip(order, _all_gather(shards, "gather_" + tag)))
    p = {}
    for nme in names:
        w = raw[nme]
        if nme in ("a_w_in", "ffn_w_up"):
            p[nme] = got[nme]
        elif nme == "c_w_qkv":
            p[nme] = _unstack_cols(got[nme], w.shape[0])
        elif nme in ("a_w_out", "ffn_w_down", "c_w_o"):
            p[nme] = got[nme].reshape(-1, w.shape[1])
        elif nme == "b_w_group":
            ng, gs, gd = w.shape
            p[nme] = jnp.transpose(got[nme].reshape(N_DEV, ng, gs, gd), (1, 0, 2, 3)).reshape(ng, N_DEV * gs, gd)
        elif nme in _SHARDED_SMALL:
            p[nme] = _unstack_cols(got[nme], w.shape[0])
        else:
            p[nme] = w
    return p


def kernel(*args):
    n_w = len(_NAMES)
    x, positions = args[0], args[1]
    weights = dict(zip(_NAMES, args[2:2 + n_w]))
    loss_target = args[2 + n_w]
    moms = dict(zip(_NAMES, args[3 + n_w:3 + 2 * n_w]))
    vels = dict(zip(_NAMES, args[3 + 2 * n_w:3 + 3 * n_w]))
    x0 = x[0]
    pos = positions[0]
    kinds = ("conf", "pool", "attn", "conf")

    params, saved, xs = [], [], [x0]
    cur = x0
    for li, names in enumerate(_LAYERS):
        raw = {n: weights["l%d_%s" % (li, n)] for n in names}
        p = _gather_layer(raw, names, kinds[li])
        if kinds[li] == "conf":
            x_mid, sv = _conf_forward(cur, p)
        elif kinds[li] == "pool":
            x_mid, sv = _pool_forward(cur, p)
        else:
            x_mid, sv = _attn_forward(cur, p, pos)
        x_out, sv_f = _ffn_forward(x_mid, p, kinds[li])
        params.append(p)
        saved.append((sv, sv_f, cur, x_mid))
        cur = x_out
    dy, loss_part = _loss_head(cur, loss_target[0], "loss_head")
    loss = lax.psum(loss_part[0, 0], ("x", "y", "c"))

    results = {}
    small_grads = {}
    dcur = dy
    for li in range(len(_LAYERS) - 1, -1, -1):
        p = params[li]
        sv, sv_f, x_in, x_mid = saved[li]
        grads = {}
        dmid = _ffn_backward(dcur, x_mid, p, sv_f, grads)
        if kinds[li] == "conf":
            dcur = _conf_backward(dmid, x_in, p, sv, grads)
        elif kinds[li] == "pool":
            dcur = _pool_backward(dmid, x_in, p, sv, grads)
        else:
            dcur = _attn_backward(dmid, x_in, p, pos, sv, grads)
        big = [n for n in _LAYERS[li] if n in _BIG]
        got = _scatter_exchange([grads[n] for n in big], "scatter_" + kinds[li])
        for n, parts in zip(big, got):
            full = "l%d_%s" % (li, n)
            w = weights[full]
            shp = w.shape
            w2 = w.reshape(-1, shp[-1])
            outs = _sum_adam(parts, w2, moms[full].reshape(w2.shape), vels[full].reshape(w2.shape), "adam_" + n)
            results[full] = tuple(o.reshape(shp) for o in outs)
        for n in _LAYERS[li]:
            if n not in _BIG:
                small_grads["l%d_%s" % (li, n)] = grads[n]

    rep = [n for n in _NAMES if n.split("_", 1)[1] not in _BIG and n.split("_", 1)[1] not in _SHARDED_SMALL]
    shd = [n for n in _NAMES if n.split("_", 1)[1] in _SHARDED_SMALL]
    rep_out, sh_out = _small_step(
        [small_grads[n] for n in rep], [_row(weights[n]) for n in rep], [_row(moms[n]) for n in rep], [_row(vels[n]) for n in rep],
        [jnp.stack([_pad_rows(small_grads[n][j]) for j in range(N_DEV)]) for n in shd],
        [_pad_rows(weights[n]) for n in shd], [_pad_rows(moms[n]) for n in shd], [_pad_rows(vels[n]) for n in shd], "small_step")
    for i, n in enumerate(rep):
        results[n] = tuple(rep_out[k][i].reshape(weights[n].shape) for k in range(4))
    for i, n in enumerate(shd):
        results[n] = tuple(sh_out[k][i][:weights[n].shape[0]] for k in range(4))

    grad_x = dcur[None]
    out = [loss, grad_x]
    for k in range(4):
        out += [results[n][k] for n in _NAMES]
    return tuple(out)
```

```python
import functools
import math

import jax
import jax.numpy as jnp
from jax import lax
from jax.experimental import pallas as pl
from jax.experimental.pallas import tpu as pltpu

F32 = jnp.float32
BF16 = jnp.bfloat16
N_DEV = 8
EPS = 1e-6
LANES = 128
HEAD = 64
Q_BLOCK = 128
ROT_DIM = 16
ROPE_THETA = 500000.0
POOL_WINDOWS = (2, 4, 8, 16)
HALO = 32
ROWS = 256
VMEM_LIMIT = 56 * 1024 * 1024
ADAM_LR, ADAM_B1, ADAM_B2, ADAM_EPS, ADAM_WD, ADAM_STEP = 0.001, 0.9, 0.999, 1e-08, 0.01, 10
MESH_ID = pl.DeviceIdType.MESH
MXU_FLOPS_PER_US = 7.5e8
HBM_BYTES_PER_US = 2.5e6


def _raw_call(body, **kw):
    return pl.pallas_call(body, **kw)


_STATE = {"comm": None}


def _pcall(body, carry_us=0.0, **kw):
    comm = _STATE["comm"]
    jobs = comm.take(carry_us) if (comm is not None and carry_us > 0) else []
    if not jobs:
        return _raw_call(body, **kw)
    return _carry(body, jobs, comm, kw)


def _params(sem=None, **kw):
    if sem is not None:
        kw["dimension_semantics"] = sem
    return pltpu.CompilerParams(vmem_limit_bytes=VMEM_LIMIT, **kw)


def _pick(dim, pref, mult=LANES):
    best = None
    d = mult
    while d <= min(dim, pref):
        if dim % d == 0:
            best = d
        d += mult
    return dim if best is None else best


def _sigmoid(x):
    return 1.0 / (1.0 + jnp.exp(-x))


def _fold8(p):
    r, c = p.shape
    return p.reshape(r // 8, 8, c).sum(axis=0)


def _rows(e, k, r):
    n = e.shape[0]
    if k % 8 == 0:
        return e[k:k + r]
    return pltpu.roll(e, n - k, 0)[0:r]


def _lshape(a):
    return a.shape if a.ndim == 2 else (a.shape[1], a.shape[0] * a.shape[2])


def _panel(a):
    return a.shape[1] if a.ndim == 2 else a.shape[2]


def _lspec(a, br, bc, rc):
    if a.ndim == 2:
        return pl.BlockSpec((br, bc), rc)
    per = a.shape[2] // bc

    def idx(i, j, k):
        r, c = rc(i, j, k)
        return (c // per, r, c % per)
    return pl.BlockSpec((None, br, bc), idx)


def _mm(a, b, dims, *, name, out_dtype=F32, out_stack=None, bias=None, res=None, tm=1024, tn=1024, tk=1024):
    (ar, ac), (br_, bc_) = _lshape(a), _lshape(b)
    if dims == "nn":
        m, k, n = ar, ac, bc_
        lim_m, lim_k, lim_n = m, min(_panel(a), k), _panel(b)
    elif dims == "nt":
        m, k, n = ar, ac, br_
        lim_m, lim_k, lim_n = m, math.gcd(_panel(a), _panel(b)), n
    else:
        m, k, n = ac, ar, bc_
        lim_m, lim_k, lim_n = _panel(a), k, _panel(b)
    if out_stack is not None:
        lim_n = math.gcd(lim_n, n // out_stack)
    sub = 16 if (out_dtype == BF16 or a.dtype == BF16) else 8
    tm = _pick(lim_m, tm, LANES if dims == "tn" else sub)
    tn = _pick(lim_n, tn)
    tk = _pick(lim_k, tk, sub if dims == "tn" else LANES)
    nk = k // tk
    if dims == "tn":
        a_spec = _lspec(a, tk, tm, lambda i, j, kk: (kk, i))
    else:
        a_spec = _lspec(a, tm, tk, lambda i, j, kk: (i, kk))
    if dims == "nt":
        b_spec = _lspec(b, tn, tk, lambda i, j, kk: (j, kk))
    else:
        b_spec = _lspec(b, tk, tn, lambda i, j, kk: (kk, j))
    contract = {"nn": ((1,), (0,)), "nt": ((1,), (1,)), "tn": ((0,), (0,))}[dims]
    in_specs, args = [a_spec, b_spec], [a, b]
    if bias is not None:
        in_specs.append(pl.BlockSpec((1, tn), lambda i, j, kk: (0, j)))
        args.append(bias)
    if res is not None:
        in_specs.append(pl.BlockSpec((tm, tn), lambda i, j, kk: (i, j)))
        args.append(res)
    if out_stack is None:
        out_shape = jax.ShapeDtypeStruct((m, n), out_dtype)
    else:
        out_shape = jax.ShapeDtypeStruct((out_stack, m, n // out_stack), out_dtype)
    o_spec = _lspec(out_shape, tm, tn, lambda i, j, kk: (i, j))
    has_bias, has_res = bias is not None, res is not None

    def body(*refs):
        a_ref, b_ref = refs[0], refs[1]
        pos = 2
        bias_ref = res_ref = None
        if has_bias:
            bias_ref = refs[pos]
            pos += 1
        if has_res:
            res_ref = refs[pos]
            pos += 1
        o_ref = refs[pos]

        def part():
            return lax.dot_general(a_ref[...].astype(BF16), b_ref[...].astype(BF16), (contract, ((), ())),
                                   preferred_element_type=F32)

        def finish(r):
            if has_bias:
                r = r + bias_ref[...]
            if has_res:
                r = r + res_ref[...]
            o_ref[...] = r.astype(out_dtype)

        if nk == 1:
            finish(part())
        else:
            acc = refs[pos + 1]
            kk = pl.program_id(2)

            @pl.when(kk == 0)
            def _():
                acc[...] = part()

            @pl.when(kk > 0)
            def _():
                acc[...] += part()

            @pl.when(kk == nk - 1)
            def _():
                finish(acc[...])

    scratch = [] if nk == 1 else [pltpu.VMEM((tm, tn), F32)]
    return _pcall(body, carry_us=2.0 * m * n * k / MXU_FLOPS_PER_US, grid=(m // tm, n // tn, nk), in_specs=in_specs, out_specs=o_spec, out_shape=out_shape,
                  scratch_shapes=scratch, compiler_params=_params(("parallel", "parallel", "arbitrary")), name=name)(*args)


def _rms_fwd(x, g, out_dtype, name):
    t, d = x.shape
    tm = _pick(t, 512, 16)

    def body(x_ref, g_ref, o_ref):
        xv = x_ref[...]
        r = lax.rsqrt(jnp.mean(xv * xv, axis=-1, keepdims=True) + EPS)
        o_ref[...] = ((xv * r) * g_ref[...]).astype(out_dtype)

    return _pcall(body, grid=(t // tm,), in_specs=[pl.BlockSpec((tm, d), lambda i: (i, 0)), pl.BlockSpec((1, d), lambda i: (0, 0))],
                  out_specs=pl.BlockSpec((tm, d), lambda i: (i, 0)), out_shape=jax.ShapeDtypeStruct((t, d), out_dtype),
                  compiler_params=_params(("parallel",)), name=name)(x, g)


def _rms_bwd(dh, x, g, dres, name):
    t, d = x.shape
    tm = _pick(t, 256, 8)

    def body(dh_ref, x_ref, g_ref, dres_ref, dx_ref, dg_ref, cs_ref):
        xv, dhv, dr = x_ref[...], dh_ref[...], dres_ref[...]
        r = lax.rsqrt(jnp.mean(xv * xv, axis=-1, keepdims=True) + EPS)
        xh = xv * r
        dxh = dhv * g_ref[...]
        dx_ref[...] = dr + r * (dxh - xh * jnp.mean(dxh * xh, axis=-1, keepdims=True))
        pg = jnp.sum(dhv * xh, axis=0, keepdims=True)
        pc = jnp.sum(dr, axis=0, keepdims=True)

        @pl.when(pl.program_id(0) == 0)
        def _():
            dg_ref[...] = pg
            cs_ref[...] = pc

        @pl.when(pl.program_id(0) > 0)
        def _():
            dg_ref[...] += pg
            cs_ref[...] += pc

    row = pl.BlockSpec((tm, d), lambda i: (i, 0))
    vec = pl.BlockSpec((1, d), lambda i: (0, 0))
    return _pcall(body, grid=(t // tm,), in_specs=[row, row, vec, row], out_specs=[row, vec, vec],
                  out_shape=[jax.ShapeDtypeStruct((t, d), F32), jax.ShapeDtypeStruct((1, d), F32), jax.ShapeDtypeStruct((1, d), F32)],
                  compiler_params=_params(("arbitrary",)), name=name)(dh, x, g, dres)


def _loss_head(y, target, name):
    t, d = y.shape
    tm = _pick(t, 512, 8)

    def body(y_ref, t_ref, dy_ref, l_ref):
        e = y_ref[...] - t_ref[...]
        dy_ref[...] = e * (1.0 / d)
        part = 0.5 * jnp.sum(jnp.mean(e * e, axis=-1, keepdims=True), axis=0, keepdims=True)

        @pl.when(pl.program_id(0) == 0)
        def _():
            l_ref[...] = part

        @pl.when(pl.program_id(0) > 0)
        def _():
            l_ref[...] += part

    row = pl.BlockSpec((tm, d), lambda i: (i, 0))
    return _pcall(body, grid=(t // tm,), in_specs=[row, row], out_specs=[row, pl.BlockSpec((1, 1), lambda i: (0, 0))],
                  out_shape=[jax.ShapeDtypeStruct((t, d), F32), jax.ShapeDtypeStruct((1, 1), F32)],
                  compiler_params=_params(("arbitrary",)), name=name)(y, target)


def _ln_silu_fwd(c, g, b, name):
    t, d = c.shape
    tm = _pick(t, 512, 16)

    def body(c_ref, g_ref, b_ref, o_ref):
        cv = c_ref[...]
        xc = cv - jnp.mean(cv, axis=-1, keepdims=True)
        z = xc * lax.rsqrt(jnp.mean(xc * xc, axis=-1, keepdims=True) + EPS) * g_ref[...] + b_ref[...]
        o_ref[...] = (z * _sigmoid(z)).astype(BF16)

    row = pl.BlockSpec((tm, d), lambda i: (i, 0))
    vec = pl.BlockSpec((1, d), lambda i: (0, 0))
    return _pcall(body, grid=(t // tm,), in_specs=[row, vec, vec], out_specs=row, out_shape=jax.ShapeDtypeStruct((t, d), BF16),
                  compiler_params=_params(("parallel",)), name=name)(c, g, b)


def _ln_silu_bwd(ds, c, g, b, name):
    t, d = c.shape
    tm = _pick(t, 256, 8)

    def body(ds_ref, c_ref, g_ref, b_ref, dc_ref, dg_ref, db_ref):
        cv = c_ref[...]
        xc = cv - jnp.mean(cv, axis=-1, keepdims=True)
        r = lax.rsqrt(jnp.mean(xc * xc, axis=-1, keepdims=True) + EPS)
        ch = xc * r
        z = ch * g_ref[...] + b_ref[...]
        sg = _sigmoid(z)
        dz = ds_ref[...] * (sg * (1.0 + z * (1.0 - sg)))
        dch = dz * g_ref[...]
        dc_ref[...] = r * (dch - jnp.mean(dch, axis=-1, keepdims=True) - ch * jnp.mean(dch * ch, axis=-1, keepdims=True))
        pg = jnp.sum(dz * ch, axis=0, keepdims=True)
        pb = jnp.sum(dz, axis=0, keepdims=True)

        @pl.when(pl.program_id(0) == 0)
        def _():
            dg_ref[...] = pg
            db_ref[...] = pb

        @pl.when(pl.program_id(0) > 0)
        def _():
            dg_ref[...] += pg
            db_ref[...] += pb

    row = pl.BlockSpec((tm, d), lambda i: (i, 0))
    vec = pl.BlockSpec((1, d), lambda i: (0, 0))
    return _pcall(body, grid=(t // tm,), in_specs=[row, row, vec, vec], out_specs=[row, vec, vec],
                  out_shape=[jax.ShapeDtypeStruct((t, d), F32), jax.ShapeDtypeStruct((1, d), F32), jax.ShapeDtypeStruct((1, d), F32)],
                  compiler_params=_params(("arbitrary",)), name=name)(ds, c, g, b)


def _steps(t):
    return t // ROWS


def _conf_conv_fwd(u, dw_w, dw_b, name):
    t, d2 = u.shape
    d = d2 // 2
    c = LANES
    ns = d // c
    kc = dw_w.shape[0]

    def body(a_ref, g_ref, w_ref, b_ref, o_ref, pad):
        pad[0:HALO, :] = jnp.zeros((HALO, c), F32)

        def glu(i, _):
            base = pl.multiple_of(i * ROWS, ROWS)
            pad[pl.ds(base + HALO, ROWS), :] = a_ref[pl.ds(base, ROWS), :] * _sigmoid(g_ref[pl.ds(base, ROWS), :])
            return 0
        lax.fori_loop(0, _steps(t), glu, 0)

        def conv(i, _):
            base = pl.multiple_of(i * ROWS, ROWS)
            e = pad[pl.ds(base, ROWS + HALO), :]
            acc = jnp.zeros((ROWS, c), F32) + b_ref[...]
            for j in range(kc):
                acc = acc + w_ref[j:j + 1, :] * _rows(e, HALO - (kc - 1) + j, ROWS)
            o_ref[pl.ds(base, ROWS), :] = acc
            return 0
        lax.fori_loop(0, _steps(t), conv, 0)

    return _pcall(body, grid=(ns,),
                  in_specs=[pl.BlockSpec((t, c), lambda s: (0, s)), pl.BlockSpec((t, c), lambda s: (0, s + ns)),
                            pl.BlockSpec((kc, c), lambda s: (0, s)), pl.BlockSpec((1, c), lambda s: (0, s))],
                  out_specs=pl.BlockSpec((t, c), lambda s: (0, s)), out_shape=jax.ShapeDtypeStruct((t, d), F32),
                  scratch_shapes=[pltpu.VMEM((t + HALO, c), F32)], compiler_params=_params(("parallel",)), name=name)(u, u, dw_w, dw_b)


def _conf_conv_bwd(dc, u, dw_w, name):
    t, d = dc.shape
    c = LANES
    ns = d // c
    kc = dw_w.shape[0]

    def body(dc_ref, a_ref, g_ref, w_ref, du_ref, dww_ref, dwb_ref, db_ref, padv, padd, accw, accb):
        padv[0:HALO, :] = jnp.zeros((HALO, c), F32)
        padd[t:t + HALO, :] = jnp.zeros((HALO, c), F32)
        accw[...] = jnp.zeros_like(accw)
        accb[...] = jnp.zeros_like(accb)

        def fill(i, _):
            base = pl.multiple_of(i * ROWS, ROWS)
            padv[pl.ds(base + HALO, ROWS), :] = a_ref[pl.ds(base, ROWS), :] * _sigmoid(g_ref[pl.ds(base, ROWS), :])
            padd[pl.ds(base, ROWS), :] = dc_ref[pl.ds(base, ROWS), :]
            return 0
        lax.fori_loop(0, _steps(t), fill, 0)

        def step(i, _):
            base = pl.multiple_of(i * ROWS, ROWS)
            ev = padv[pl.ds(base, ROWS + HALO), :]
            ed = padd[pl.ds(base, ROWS + HALO), :]
            dcc = ed[0:ROWS]
            dv = jnp.zeros((ROWS, c), F32)
            for j in range(kc):
                dv = dv + w_ref[j:j + 1, :] * _rows(ed, kc - 1 - j, ROWS)
                accw[j] = accw[j] + _fold8(dcc * _rows(ev, HALO - (kc - 1) + j, ROWS))
            accb[0] = accb[0] + _fold8(dcc)
            av = a_ref[pl.ds(base, ROWS), :]
            sg = _sigmoid(g_ref[pl.ds(base, ROWS), :])
            da = dv * sg
            dg = dv * av * sg * (1.0 - sg)
            du_ref[0, pl.ds(base, ROWS), :] = da.astype(BF16)
            du_ref[1, pl.ds(base, ROWS), :] = dg.astype(BF16)
            accb[1] = accb[1] + _fold8(da)
            accb[2] = accb[2] + _fold8(dg)
            return 0
        lax.fori_loop(0, _steps(t), step, 0)
        for j in range(kc):
            dww_ref[j:j + 1, :] = jnp.sum(accw[j], axis=0, keepdims=True)
        dwb_ref[...] = jnp.sum(accb[0], axis=0, keepdims=True)
        db_ref[0] = jnp.sum(accb[1], axis=0, keepdims=True)
        db_ref[1] = jnp.sum(accb[2], axis=0, keepdims=True)

    return _pcall(body, grid=(ns,),
                  in_specs=[pl.BlockSpec((t, c), lambda s: (0, s)), pl.BlockSpec((t, c), lambda s: (0, s)),
                            pl.BlockSpec((t, c), lambda s: (0, s + ns)), pl.BlockSpec((kc, c), lambda s: (0, s))],
                  out_specs=[pl.BlockSpec((2, t, c), lambda s: (0, 0, s)), pl.BlockSpec((kc, c), lambda s: (0, s)),
                             pl.BlockSpec((1, c), lambda s: (0, s)), pl.BlockSpec((2, 1, c), lambda s: (0, 0, s))],
                  out_shape=[jax.ShapeDtypeStruct((2, t, d), BF16), jax.ShapeDtypeStruct((kc, d), F32),
                             jax.ShapeDtypeStruct((1, d), F32), jax.ShapeDtypeStruct((2, 1, d), F32)],
                  scratch_shapes=[pltpu.VMEM((t + HALO, c), F32), pltpu.VMEM((t + HALO, c), F32),
                                  pltpu.VMEM((kc, 8, c), F32), pltpu.VMEM((3, 8, c), F32)],
                  compiler_params=_params(("parallel",)), name=name)(dc, u, u, dw_w)


def _ffn_act_fwd(u0, dw_w, dw_b, name):
    t, f2 = u0.shape
    f = f2 // 2
    c = LANES
    ns = f // c
    kw = dw_w.shape[0]

    def body(g_ref, v_ref, wg_ref, wv_ref, bg_ref, bv_ref, o_ref):
        def step(i, _):
            base = pl.multiple_of(i * ROWS, ROWS)
            lo = pl.multiple_of(jnp.maximum(base - HALO, 0), HALO)
            keep = jnp.where(i > 0, 1.0, 0.0)
            eg = jnp.concatenate([g_ref[pl.ds(lo, HALO), :] * keep, g_ref[pl.ds(base, ROWS), :]], axis=0)
            ev = jnp.concatenate([v_ref[pl.ds(lo, HALO), :] * keep, v_ref[pl.ds(base, ROWS), :]], axis=0)
            gate = jnp.zeros((ROWS, c), F32) + bg_ref[...]
            val = jnp.zeros((ROWS, c), F32) + bv_ref[...]
            for j in range(kw):
                gate = gate + wg_ref[j:j + 1, :] * _rows(eg, HALO - (kw - 1) + j, ROWS)
                val = val + wv_ref[j:j + 1, :] * _rows(ev, HALO - (kw - 1) + j, ROWS)
            o_ref[pl.ds(base, ROWS), :] = (gate * _sigmoid(gate) * val).astype(BF16)
            return 0
        lax.fori_loop(0, _steps(t), step, 0)

    return _pcall(body, grid=(ns,),
                  in_specs=[pl.BlockSpec((t, c), lambda s: (0, s)), pl.BlockSpec((t, c), lambda s: (0, s + ns)),
                            pl.BlockSpec((kw, c), lambda s: (0, s)), pl.BlockSpec((kw, c), lambda s: (0, s + ns)),
                            pl.BlockSpec((1, c), lambda s: (0, s)), pl.BlockSpec((1, c), lambda s: (0, s + ns))],
                  out_specs=pl.BlockSpec((t, c), lambda s: (0, s)), out_shape=jax.ShapeDtypeStruct((t, f), BF16),
                  compiler_params=_params(("parallel",)), name=name)(u0, u0, dw_w, dw_w, dw_b, dw_b)


def _ffn_act_bwd(da, u0, dw_w, dw_b, name):
    t, f = da.shape
    c = LANES
    ns = f // c
    kw = dw_w.shape[0]

    def body(da_ref, g_ref, v_ref, wg_ref, wv_ref, bg_ref, bv_ref, du_ref, dww_ref, dwb_ref, padg, padv, accw, accb):
        padg[t:t + HALO, :] = jnp.zeros((HALO, c), F32)
        padv[t:t + HALO, :] = jnp.zeros((HALO, c), F32)
        accw[...] = jnp.zeros_like(accw)
        accb[...] = jnp.zeros_like(accb)

        def first(i, _):
            base = pl.multiple_of(i * ROWS, ROWS)
            lo = pl.multiple_of(jnp.maximum(base - HALO, 0), HALO)
            keep = jnp.where(i > 0, 1.0, 0.0)
            eg = jnp.concatenate([g_ref[pl.ds(lo, HALO), :] * keep, g_ref[pl.ds(base, ROWS), :]], axis=0)
            ev = jnp.concatenate([v_ref[pl.ds(lo, HALO), :] * keep, v_ref[pl.ds(base, ROWS), :]], axis=0)
            gate = jnp.zeros((ROWS, c), F32) + bg_ref[...]
            val = jnp.zeros((ROWS, c), F32) + bv_ref[...]
            sh_g, sh_v = [], []
            for j in range(kw):
                sh_g.append(_rows(eg, HALO - (kw - 1) + j, ROWS))
                sh_v.append(_rows(ev, HALO - (kw - 1) + j, ROWS))
                gate = gate + wg_ref[j:j + 1, :] * sh_g[j]
                val = val + wv_ref[j:j + 1, :] * sh_v[j]
            dav = da_ref[pl.ds(base, ROWS), :]
            sg = _sigmoid(gate)
            dgate = dav * val * (sg * (1.0 + gate * (1.0 - sg)))
            dval = dav * (gate * sg)
            padg[pl.ds(base, ROWS), :] = dgate
            padv[pl.ds(base, ROWS), :] = dval
            for j in range(kw):
                accw[j] = accw[j] + _fold8(dgate * sh_g[j])
                accw[kw + j] = accw[kw + j] + _fold8(dval * sh_v[j])
            accb[0] = accb[0] + _fold8(dgate)
            accb[1] = accb[1] + _fold8(dval)
            return 0
        lax.fori_loop(0, _steps(t), first, 0)

        def second(i, _):
            base = pl.multiple_of(i * ROWS, ROWS)
            eg = padg[pl.ds(base, ROWS + HALO), :]
            ev = padv[pl.ds(base, ROWS + HALO), :]
            dg = jnp.zeros((ROWS, c), F32)
            dv = jnp.zeros((ROWS, c), F32)
            for j in range(kw):
                dg = dg + wg_ref[j:j + 1, :] * _rows(eg, kw - 1 - j, ROWS)
                dv = dv + wv_ref[j:j + 1, :] * _rows(ev, kw - 1 - j, ROWS)
            du_ref[0, pl.ds(base, ROWS), :] = dg.astype(BF16)
            du_ref[1, pl.ds(base, ROWS), :] = dv.astype(BF16)
            return 0
        lax.fori_loop(0, _steps(t), second, 0)
        for j in range(kw):
            dww_ref[0, j:j + 1, :] = jnp.sum(accw[j], axis=0, keepdims=True)
            dww_ref[1, j:j + 1, :] = jnp.sum(accw[kw + j], axis=0, keepdims=True)
        dwb_ref[0] = jnp.sum(accb[0], axis=0, keepdims=True)
        dwb_ref[1] = jnp.sum(accb[1], axis=0, keepdims=True)

    return _pcall(body, grid=(ns,),
                  in_specs=[pl.BlockSpec((t, c), lambda s: (0, s)),
                            pl.BlockSpec((t, c), lambda s: (0, s)), pl.BlockSpec((t, c), lambda s: (0, s + ns)),
                            pl.BlockSpec((kw, c), lambda s: (0, s)), pl.BlockSpec((kw, c), lambda s: (0, s + ns)),
                            pl.BlockSpec((1, c), lambda s: (0, s)), pl.BlockSpec((1, c), lambda s: (0, s + ns))],
                  out_specs=[pl.BlockSpec((2, t, c), lambda s: (0, 0, s)), pl.BlockSpec((2, kw, c), lambda s: (0, 0, s)),
                             pl.BlockSpec((2, 1, c), lambda s: (0, 0, s))],
                  out_shape=[jax.ShapeDtypeStruct((2, t, f), BF16), jax.ShapeDtypeStruct((2, kw, f), F32),
                             jax.ShapeDtypeStruct((2, 1, f), F32)],
                  scratch_shapes=[pltpu.VMEM((t + HALO, c), F32), pltpu.VMEM((t + HALO, c), F32),
                                  pltpu.VMEM((2 * kw, 8, c), F32), pltpu.VMEM((2, 8, c), F32)],
                  compiler_params=_params(("parallel",)), name=name)(da, u0, u0, dw_w, dw_w, dw_b, dw_b)


def _window_of(group):
    w = jnp.float32(POOL_WINDOWS[-1])
    for k in range(len(POOL_WINDOWS) - 2, -1, -1):
        w = jnp.where(group == k, jnp.float32(POOL_WINDOWS[k]), w)
    return w


def _select_level(group, levels):
    out = levels[-1]
    for k in range(len(levels) - 2, -1, -1):
        out = jnp.where(group == k, levels[k], out)
    return out


def _pool_fwd(h, name):
    t, d = h.shape
    c = LANES
    per = d // len(POOL_WINDOWS) // c

    def body(h_ref, o_ref):
        group = pl.program_id(0)
        wf = _window_of(group)

        def step(i, _):
            base = pl.multiple_of(i * ROWS, ROWS)
            lo = pl.multiple_of(jnp.maximum(base - HALO, 0), HALO)
            keep = jnp.where(i > 0, 1.0, 0.0)
            cur = h_ref[pl.ds(base, ROWS), :]
            e = jnp.concatenate([h_ref[pl.ds(lo, HALO), :] * keep, cur], axis=0)
            n = ROWS + HALO
            levels = []
            s = e
            for k in range(len(POOL_WINDOWS)):
                s = s + pltpu.roll(s, 1 << k, 0)
                levels.append(s[HALO:n])
            tpos = (base + lax.broadcasted_iota(jnp.int32, (ROWS, c), 0) + 1).astype(F32)
            pooled = _select_level(group, levels) / jnp.minimum(tpos, wf)
            o_ref[pl.ds(base, ROWS), :] = (pooled - cur).astype(BF16)
            return 0
        lax.fori_loop(0, _steps(t), step, 0)

    return _pcall(body, grid=(len(POOL_WINDOWS), per), in_specs=[pl.BlockSpec((t, c), lambda g, s: (0, g * per + s))],
                  out_specs=pl.BlockSpec((t, c), lambda g, s: (0, g * per + s)), out_shape=jax.ShapeDtypeStruct((t, d), BF16),
                  compiler_params=_params(("parallel", "parallel")), name=name)(h)


def _pool_bwd(dm, name):
    t, d = dm.shape
    c = LANES
    per = d // len(POOL_WINDOWS) // c

    def body(dm_ref, o_ref, pad):
        group = pl.program_id(0)
        wf = _window_of(group)
        pad[t:t + HALO, :] = jnp.zeros((HALO, c), F32)

        def fill(i, _):
            base = pl.multiple_of(i * ROWS, ROWS)
            tpos = (base + lax.broadcasted_iota(jnp.int32, (ROWS, c), 0) + 1).astype(F32)
            pad[pl.ds(base, ROWS), :] = dm_ref[pl.ds(base, ROWS), :] / jnp.minimum(tpos, wf)
            return 0
        lax.fori_loop(0, _steps(t), fill, 0)

        def step(i, _):
            base = pl.multiple_of(i * ROWS, ROWS)
            n = ROWS + HALO
            s = pad[pl.ds(base, n), :]
            levels = []
            for k in range(len(POOL_WINDOWS)):
                s = s + pltpu.roll(s, n - (1 << k), 0)
                levels.append(s[0:ROWS])
            o_ref[pl.ds(base, ROWS), :] = _select_level(group, levels) - dm_ref[pl.ds(base, ROWS), :]
            return 0
        lax.fori_loop(0, _steps(t), step, 0)

    return _pcall(body, grid=(len(POOL_WINDOWS), per), in_specs=[pl.BlockSpec((t, c), lambda g, s: (0, g * per + s))],
                  out_specs=pl.BlockSpec((t, c), lambda g, s: (0, g * per + s)), out_shape=jax.ShapeDtypeStruct((t, d), F32),
                  scratch_shapes=[pltpu.VMEM((t + HALO, c), F32)], compiler_params=_params(("parallel", "parallel")), name=name)(dm)


def _pool_mm_fwd(mixed, wg, scale, res, name):
    t, d = mixed.shape
    ng, gd, _ = wg.shape
    tm = _pick(t, 1024, 16)

    def body(a_ref, w_ref, s_ref, r_ref, o_ref):
        y = jnp.dot(a_ref[...], w_ref[...], preferred_element_type=F32)
        o_ref[...] = r_ref[...] + y * s_ref[...]

    blk = pl.BlockSpec((tm, gd), lambda g, i: (i, g))
    return _pcall(body, grid=(ng, t // tm),
                  in_specs=[blk, pl.BlockSpec((None, gd, gd), lambda g, i: (g, 0, 0)), pl.BlockSpec((1, gd), lambda g, i: (0, g)), blk],
                  out_specs=blk, out_shape=jax.ShapeDtypeStruct((t, d), F32),
                  compiler_params=_params(("parallel", "parallel")), name=name)(mixed, wg, scale, res)


def _pool_mm_bwd(dy, mixed, wg, scale, name):
    t, d = mixed.shape
    ng, gd, _ = wg.shape
    tm = _pick(t, 1024, 16)

    def body(dy_ref, a_ref, w_ref, s_ref, dm_ref, dw_ref, ds_ref):
        a, w, dyv = a_ref[...], w_ref[...], dy_ref[...]
        y = jnp.dot(a, w, preferred_element_type=F32)
        dyp = (dyv * s_ref[...]).astype(BF16)
        dm_ref[...] = lax.dot_general(dyp, w, (((1,), (1,)), ((), ())), preferred_element_type=F32)
        pw = lax.dot_general(a, dyp, (((0,), (0,)), ((), ())), preferred_element_type=F32)
        ps = jnp.sum(dyv * y, axis=0, keepdims=True)

        @pl.when(pl.program_id(1) == 0)
        def _():
            dw_ref[...] = pw
            ds_ref[...] = ps

        @pl.when(pl.program_id(1) > 0)
        def _():
            dw_ref[...] += pw
            ds_ref[...] += ps

    blk = pl.BlockSpec((tm, gd), lambda g, i: (i, g))
    wsp = pl.BlockSpec((None, gd, gd), lambda g, i: (g, 0, 0))
    vec = pl.BlockSpec((1, gd), lambda g, i: (0, g))
    return _pcall(body, grid=(ng, t // tm), in_specs=[blk, blk, wsp, vec], out_specs=[blk, wsp, vec],
                  out_shape=[jax.ShapeDtypeStruct((t, d), F32), jax.ShapeDtypeStruct((ng, gd, gd), F32), jax.ShapeDtypeStruct((1, d), F32)],
                  compiler_params=_params(("parallel", "arbitrary")), name=name)(dy, mixed, wg, scale)


def _rope_tables(positions):
    half = ROT_DIM // 2
    inv_freq = ROPE_THETA ** (-jnp.arange(0, ROT_DIM, 2, dtype=F32) / ROT_DIM)
    ang = positions.astype(F32)[:, None] * inv_freq
    cos, sin = jnp.cos(ang), jnp.sin(ang)
    t = positions.shape[0]
    ones = jnp.ones((t, HEAD - ROT_DIM), F32)
    zeros = jnp.zeros((t, HEAD - ROT_DIM), F32)
    zh = jnp.zeros((t, half), F32)
    c = jnp.concatenate([cos, cos, ones], axis=1)
    s1 = jnp.concatenate([-sin, zh, zeros], axis=1)
    s2 = jnp.concatenate([zh, sin, zeros], axis=1)
    return tuple(jnp.concatenate([a, a], axis=1) for a in (c, s1, s2))


def _half_mean(v, lo):
    s_lo = jnp.sum(jnp.where(lo, v, 0.0), axis=-1, keepdims=True)
    s_hi = jnp.sum(jnp.where(lo, 0.0, v), axis=-1, keepdims=True)
    return jnp.where(lo, s_lo, s_hi) * (1.0 / HEAD)


def _qk_prep_fwd(qkv, tabs, gq2, gk2, n_q, n_kv, name):
    t, width = qkv.shape
    tm = _pick(t, 256, 16)
    nqc, nkc = n_q * HEAD // LANES, n_kv * HEAD // LANES

    def body(x_ref, c_ref, s1_ref, s2_ref, gq_ref, gk_ref, q_ref, k2_ref, v2_ref):
        lo = lax.broadcasted_iota(jnp.int32, (tm, LANES), 1) < HEAD
        cv, s1, s2 = c_ref[...], s1_ref[...], s2_ref[...]

        def normrot(xc, g2):
            y = xc * lax.rsqrt(_half_mean(xc * xc, lo) + EPS) * g2
            return y * cv + pltpu.roll(y, LANES - ROT_DIM // 2, 1) * s1 + pltpu.roll(y, ROT_DIM // 2, 1) * s2

        def twice(y, j):
            sw = pltpu.roll(y, HEAD, 1)
            k2 = jnp.where(lo, y, sw) if j == 0 else jnp.where(lo, sw, y)
            return k2.astype(BF16)

        for ch in range(nqc):
            q_ref[:, ch * LANES:(ch + 1) * LANES] = normrot(x_ref[:, ch * LANES:(ch + 1) * LANES], gq_ref[...]).astype(BF16)
        for ch in range(nkc):
            off = (nqc + ch) * LANES
            y = normrot(x_ref[:, off:off + LANES], gk_ref[...])
            voff = (nqc + nkc + ch) * LANES
            vv = x_ref[:, voff:voff + LANES]
            for j in range(2):
                k2_ref[:, (2 * ch + j) * LANES:(2 * ch + j + 1) * LANES] = twice(y, j)
                v2_ref[:, (2 * ch + j) * LANES:(2 * ch + j + 1) * LANES] = twice(vv, j)

    row = lambda w: pl.BlockSpec((tm, w), lambda i: (i, 0))
    vec = pl.BlockSpec((1, LANES), lambda i: (0, 0))
    return _pcall(body, grid=(t // tm,), in_specs=[row(width), row(LANES), row(LANES), row(LANES), vec, vec],
                  out_specs=[row(n_q * HEAD), row(n_kv * LANES), row(n_kv * LANES)],
                  out_shape=[jax.ShapeDtypeStruct((t, n_q * HEAD), BF16), jax.ShapeDtypeStruct((t, n_kv * LANES), BF16),
                             jax.ShapeDtypeStruct((t, n_kv * LANES), BF16)],
                  compiler_params=_params(("parallel",)), name=name)(qkv, *tabs, gq2, gk2)


def _qk_prep_bwd(dq, dk_cur, dk_prev, dv_cur, dv_prev, qkv, tabs, gq2, gk2, n_q, n_kv, name):
    t, width = qkv.shape
    tm = Q_BLOCK
    nb = t // tm
    nqc, nkc = n_q * HEAD // LANES, n_kv * HEAD // LANES

    def body(dq_ref, kc_ref, kp_ref, vc_ref, vp_ref, x_ref, c_ref, s1_ref, s2_ref, gq_ref, gk_ref, o_ref, dgq_ref, dgk_ref):
        lo = lax.broadcasted_iota(jnp.int32, (tm, LANES), 1) < HEAD
        cv, s1, s2 = c_ref[...], s1_ref[...], s2_ref[...]
        more = jnp.where(pl.program_id(0) < nb - 1, 1.0, 0.0)

        def back(dy, xc, g2):
            dyn = dy * cv + pltpu.roll(dy * s1, ROT_DIM // 2, 1) + pltpu.roll(dy * s2, LANES - ROT_DIM // 2, 1)
            r = lax.rsqrt(_half_mean(xc * xc, lo) + EPS)
            xh = xc * r
            dxh = dyn * g2
            return r * (dxh - xh * _half_mean(dxh * xh, lo)), jnp.sum(dyn * xh, axis=0, keepdims=True)

        def unfold(cur_ref, prev_ref, ch):
            d0 = cur_ref[:, (2 * ch) * LANES:(2 * ch + 1) * LANES] + more * prev_ref[:, (2 * ch) * LANES:(2 * ch + 1) * LANES]
            d1 = cur_ref[:, (2 * ch + 1) * LANES:(2 * ch + 2) * LANES] + more * prev_ref[:, (2 * ch + 1) * LANES:(2 * ch + 2) * LANES]
            return jnp.where(lo, d0 + pltpu.roll(d0, HEAD, 1), d1 + pltpu.roll(d1, HEAD, 1))

        pq = jnp.zeros((1, LANES), F32)
        for ch in range(nqc):
            sl = slice(ch * LANES, (ch + 1) * LANES)
            dx, pg = back(dq_ref[:, sl], x_ref[:, sl], gq_ref[...])
            o_ref[:, sl] = dx.astype(BF16)
            pq = pq + pg
        pk = jnp.zeros((1, LANES), F32)
        for ch in range(nkc):
            sl = slice((nqc + ch) * LANES, (nqc + ch + 1) * LANES)
            dx, pg = back(unfold(kc_ref, kp_ref, ch), x_ref[:, sl], gk_ref[...])
            o_ref[:, sl] = dx.astype(BF16)
            pk = pk + pg
            vs = slice((nqc + nkc + ch) * LANES, (nqc + nkc + ch + 1) * LANES)
            o_ref[:, vs] = unfold(vc_ref, vp_ref, ch).astype(BF16)

        @pl.when(pl.program_id(0) == 0)
        def _():
            dgq_ref[...] = pq
            dgk_ref[...] = pk

        @pl.when(pl.program_id(0) > 0)
        def _():
            dgq_ref[...] += pq
            dgk_ref[...] += pk

    row = lambda w: pl.BlockSpec((tm, w), lambda i: (i, 0))
    nxt = lambda w: pl.BlockSpec((tm, w), lambda i: (jnp.minimum(i + 1, nb - 1), 0))
    vec = pl.BlockSpec((1, LANES), lambda i: (0, 0))
    kvw = n_kv * LANES
    return _pcall(body, grid=(nb,),
                  in_specs=[row(n_q * HEAD), row(kvw), nxt(kvw), row(kvw), nxt(kvw), row(width), row(LANES), row(LANES), row(LANES), vec, vec],
                  out_specs=[row(width), vec, vec],
                  out_shape=[jax.ShapeDtypeStruct((t, width), BF16), jax.ShapeDtypeStruct((1, LANES), F32), jax.ShapeDtypeStruct((1, LANES), F32)],
                  compiler_params=_params(("arbitrary",)), name=name)(dq, dk_cur, dk_prev, dv_cur, dv_prev, qkv, *tabs, gq2, gk2)


def _band_scores(qh, kc, kp, n, sink_row, lo_row, is_lo):
    scale = 1.0 / math.sqrt(HEAD)
    nt = (((1,), (1,)), ((), ()))
    s_c = lax.dot_general(qh, kc, nt, preferred_element_type=F32) * scale
    s_p = lax.dot_general(qh, kp, nt, preferred_element_type=F32) * scale
    qi = lax.broadcasted_iota(jnp.int32, (Q_BLOCK, Q_BLOCK), 0)
    kj = lax.broadcasted_iota(jnp.int32, (Q_BLOCK, Q_BLOCK), 1)
    s_c = jnp.where(kj <= qi, s_c, -jnp.inf)
    s_p = jnp.where((kj > qi) & (n > 0), s_p, -jnp.inf)
    pick = lo_row if is_lo else jnp.logical_not(lo_row)
    sink = jnp.max(jnp.where(pick, sink_row, -jnp.inf), axis=-1, keepdims=True)
    return s_c, s_p, sink


def _attn_fwd(q, k2, v2, sink_tab, name):
    t, dq = q.shape
    nc = dq // LANES
    nb = t // Q_BLOCK
    per_kv = nc // (k2.shape[1] // LANES)

    def body(q_ref, kc_ref, kp_ref, vc_ref, vp_ref, s_ref, o_ref, lse_ref):
        n = pl.program_id(1)
        lo = lax.broadcasted_iota(jnp.int32, (Q_BLOCK, LANES), 1) < HEAD
        lo_row = lax.broadcasted_iota(jnp.int32, (1, LANES), 1) < HEAD
        qv = q_ref[...].astype(F32)
        kc, kp, vc, vp = kc_ref[...], kp_ref[...], vc_ref[...], vp_ref[...]
        outs, lses = [], []
        for is_lo in (True, False):
            qh = jnp.where(lo, qv, 0.0) if is_lo else jnp.where(lo, 0.0, qv)
            s_c, s_p, sink = _band_scores(qh.astype(BF16), kc, kp, n, s_ref[0:1, :], lo_row, is_lo)
            m = jnp.maximum(jnp.maximum(jnp.max(s_c, axis=-1, keepdims=True), jnp.max(s_p, axis=-1, keepdims=True)), sink)
            p_c, p_p = jnp.exp(s_c - m), jnp.exp(s_p - m)
            denom = jnp.sum(p_c, axis=-1, keepdims=True) + jnp.sum(p_p, axis=-1, keepdims=True) + jnp.exp(sink - m)
            pv = jnp.dot(p_c.astype(BF16), vc, preferred_element_type=F32) + jnp.dot(p_p.astype(BF16), vp, preferred_element_type=F32)
            outs.append(pv / denom)
            lses.append(m + jnp.log(denom))
        o_ref[...] = jnp.where(lo, outs[0], outs[1]).astype(BF16)
        lse_ref[...] = jnp.where(lo, lses[0], lses[1])

    qs = pl.BlockSpec((Q_BLOCK, LANES), lambda c, n: (n, c))
    cur = pl.BlockSpec((Q_BLOCK, LANES), lambda c, n: (n, c // per_kv))
    prev = pl.BlockSpec((Q_BLOCK, LANES), lambda c, n: (jnp.maximum(n - 1, 0), c // per_kv))
    return _pcall(body, grid=(nc, nb),
                  in_specs=[qs, cur, prev, cur, prev, pl.BlockSpec((8, LANES), lambda c, n: (c, 0))],
                  out_specs=[qs, pl.BlockSpec((None, Q_BLOCK, LANES), lambda c, n: (c, n, 0))],
                  out_shape=[jax.ShapeDtypeStruct((t, dq), BF16), jax.ShapeDtypeStruct((nc, t, LANES), F32)],
                  compiler_params=_params(("parallel", "parallel")), name=name)(q, k2, k2, v2, v2, sink_tab)


def _attn_bwd(do, q, o, lse, k2, v2, sink_tab, name):
    t, dq = q.shape
    nc = dq // LANES
    nb = t // Q_BLOCK
    nkv = k2.shape[1] // LANES
    per_kv = nc // nkv
    scale = 1.0 / math.sqrt(HEAD)
    tn_ = (((0,), (0,)), ((), ()))
    nt = (((1,), (1,)), ((), ()))

    def body(do_ref, q_ref, o_ref, lse_ref, kc_ref, kp_ref, vc_ref, vp_ref, s_ref,
             dq_ref, dkc_ref, dkp_ref, dvc_ref, dvp_ref, dsk_ref):
        n = pl.program_id(1)
        cc = pl.program_id(2)
        lo = lax.broadcasted_iota(jnp.int32, (Q_BLOCK, LANES), 1) < HEAD
        lo_row = lax.broadcasted_iota(jnp.int32, (1, LANES), 1) < HEAD
        qv, dov, ov, lsev = q_ref[...].astype(F32), do_ref[...], o_ref[...].astype(F32), lse_ref[...]
        kc, kp, vc, vp = kc_ref[...], kp_ref[...], vc_ref[...], vp_ref[...]
        dqs, dsinks = [], []
        dkc = dkp = dvc = dvp = None
        for is_lo in (True, False):
            half = lo if is_lo else jnp.logical_not(lo)
            qh = jnp.where(half, qv, 0.0).astype(BF16)
            doh = jnp.where(half, dov, 0.0)
            s_c, s_p, sink = _band_scores(qh, kc, kp, n, s_ref[0:1, :], lo_row, is_lo)
            lse_h = jnp.max(jnp.where(half, lsev, -jnp.inf), axis=-1, keepdims=True)
            p_c, p_p = jnp.exp(s_c - lse_h), jnp.exp(s_p - lse_h)
            delta = jnp.sum(doh * ov, axis=-1, keepdims=True)
            dob = doh.astype(BF16)
            ds_c = (p_c * (lax.dot_general(dob, vc, nt, preferred_element_type=F32) - delta)).astype(BF16)
            ds_p = (p_p * (lax.dot_general(dob, vp, nt, preferred_element_type=F32) - delta)).astype(BF16)
            dsinks.append(-jnp.sum(jnp.exp(sink - lse_h) * delta, axis=0, keepdims=True))
            dqs.append((jnp.dot(ds_c, kc, preferred_element_type=F32) + jnp.dot(ds_p, kp, preferred_element_type=F32)) * scale)
            parts = (lax.dot_general(ds_c, qh, tn_, preferred_element_type=F32) * scale,
                     lax.dot_general(ds_p, qh, tn_, preferred_element_type=F32) * scale,
                     lax.dot_general(p_c.astype(BF16), dob, tn_, preferred_element_type=F32),
                     lax.dot_general(p_p.astype(BF16), dob, tn_, preferred_element_type=F32))
            if dkc is None:
                dkc, dkp, dvc, dvp = parts
            else:
                dkc, dkp, dvc, dvp = dkc + parts[0], dkp + parts[1], dvc + parts[2], dvp + parts[3]
        dq_ref[...] = jnp.where(lo, dqs[0], dqs[1])
        dsk_ref[...] = jnp.zeros((8, LANES), F32) + jnp.where(lo_row, dsinks[0], dsinks[1])

        @pl.when(cc == 0)
        def _():
            dkc_ref[...] = dkc
            dkp_ref[...] = dkp
            dvc_ref[...] = dvc
            dvp_ref[...] = dvp

        @pl.when(cc > 0)
        def _():
            dkc_ref[...] += dkc
            dkp_ref[...] += dkp
            dvc_ref[...] += dvc
            dvp_ref[...] += dvp

    qs = pl.BlockSpec((Q_BLOCK, LANES), lambda k, n, cc: (n, k * per_kv + cc))
    cur = pl.BlockSpec((Q_BLOCK, LANES), lambda k, n, cc: (n, k))
    prev = pl.BlockSpec((Q_BLOCK, LANES), lambda k, n, cc: (jnp.maximum(n - 1, 0), k))
    kv_shape = jax.ShapeDtypeStruct((t, nkv * LANES), F32)
    return _pcall(body, grid=(nkv, nb, per_kv),
                  in_specs=[qs, qs, qs, pl.BlockSpec((None, Q_BLOCK, LANES), lambda k, n, cc: (k * per_kv + cc, n, 0)),
                            cur, prev, cur, prev, pl.BlockSpec((8, LANES), lambda k, n, cc: (k * per_kv + cc, 0))],
                  out_specs=[qs, cur, cur, cur, cur, pl.BlockSpec((None, 8, LANES), lambda k, n, cc: ((k * per_kv + cc) * nb + n, 0, 0))],
                  out_shape=[jax.ShapeDtypeStruct((t, dq), F32), kv_shape, kv_shape, kv_shape, kv_shape,
                             jax.ShapeDtypeStruct((nc * nb, 8, LANES), F32)],
                  compiler_params=_params(("parallel", "parallel", "arbitrary")), name=name)(do, q, o, lse, k2, k2, v2, v2, sink_tab)


def _peer(k):
    x, y, c = lax.axis_index("x"), lax.axis_index("y"), lax.axis_index("c")
    flip = lambda v, bit: 1 - v if bit else v
    return (flip(x, k & 4), flip(y, k & 2), flip(c, k & 1))


def _my_index():
    return 4 * lax.axis_index("x") + 2 * lax.axis_index("y") + lax.axis_index("c")


def _peer_index(k):
    px, py, pc = _peer(k)
    return 4 * px + 2 * py + pc


def _all_gather(shards, name):
    n = len(shards)
    any_spec = pl.BlockSpec(memory_space=pl.ANY)

    def body(*refs):
        ins, outs = refs[:n], refs[n:2 * n]
        send_sems, recv_sems, local_sems = refs[2 * n:]
        me = _my_index()
        local = [pltpu.make_async_copy(ins[a], outs[a].at[me], local_sems.at[a]) for a in range(n)]
        for cp in local:
            cp.start()
        sends = []
        for k in range(1, N_DEV):
            for a in range(n):
                cp = pltpu.make_async_remote_copy(src_ref=ins[a], dst_ref=outs[a].at[me], send_sem=send_sems.at[a, k - 1],
                                                  recv_sem=recv_sems.at[a, k - 1], device_id=_peer(k), device_id_type=MESH_ID)
                cp.start()
                sends.append(cp)
        for k in range(1, N_DEV):
            for a in range(n):
                pltpu.make_async_remote_copy(src_ref=ins[a], dst_ref=outs[a].at[_peer_index(k)], send_sem=send_sems.at[a, k - 1],
                                             recv_sem=recv_sems.at[a, k - 1], device_id=_peer(k), device_id_type=MESH_ID).wait_recv()
        for cp in sends:
            cp.wait_send()
        for cp in local:
            cp.wait()

    return _pcall(body, in_specs=[any_spec] * n, out_specs=[any_spec] * n,
                  out_shape=[jax.ShapeDtypeStruct((N_DEV,) + s.shape, s.dtype) for s in shards],
                  scratch_shapes=[pltpu.SemaphoreType.DMA((n, N_DEV - 1)), pltpu.SemaphoreType.DMA((n, N_DEV - 1)),
                                  pltpu.SemaphoreType.DMA((n,))],
                  name=name)(*shards)


GATHER1_PEERS = (1, 2, 4, 6)
GATHER2_PEERS = (2, 4, 6)
SCATTER_PEERS = tuple(range(1, N_DEV))
MAX_SEMS = N_DEV - 1
MAX_JOBS = 6
US_PER_MB = {"gather1": 5.4, "gather2": 0.6, "scatter": 10.8}


class _Job:
    def __init__(self, key, kind, src):
        self.key, self.kind, self.src = key, kind, src
        shape = src.shape if kind != "gather1" else (N_DEV,) + src.shape
        self.out_shape = jax.ShapeDtypeStruct(shape, src.dtype)
        self.cost_us = US_PER_MB[kind] * math.prod(shape) * src.dtype.itemsize / 1e6


class _Comm:
    def __init__(self):
        self.queue, self.gathered, self.scattered = [], {}, []

    def push(self, key, kind, src):
        self.queue.append(_Job(key, kind, src))

    def take(self, budget_us):
        jobs = [j for j in self.queue if j.kind == "gather2"][:MAX_JOBS]
        used = sum(j.cost_us for j in jobs)
        for j in self.queue:
            if used >= budget_us or len(jobs) >= MAX_JOBS:
                break
            if j.kind != "gather2":
                jobs.append(j)
                used += j.cost_us
        self.queue = [j for j in self.queue if j not in jobs]
        return jobs

    def finish(self, job, result):
        if job.kind == "gather1":
            self.queue.insert(0, _Job(job.key, "gather2", result))
        elif job.kind == "gather2":
            self.gathered[job.key] = result
        else:
            self.scattered.append((job.key, result))

    def need(self, key):
        while key not in self.gathered:
            assert any(j.key == key for j in self.queue), key
            jobs = [j for j in self.queue if j.kind == "gather2"][:MAX_JOBS]
            if not any(j.key == key for j in jobs):
                for j in self.queue:
                    if j.kind == "gather1" and len(jobs) < MAX_JOBS:
                        jobs.append(j)
                        if j.key == key:
                            break
            self.flush(jobs)
        return self.gathered[key]

    def flush(self, jobs):
        self.queue = [j for j in self.queue if j not in jobs]

        def body(o_ref):
            o_ref[...] = jnp.zeros_like(o_ref)
        _carry(body, jobs, self, dict(in_specs=[], out_specs=pl.BlockSpec(memory_space=pltpu.VMEM),
                                      out_shape=jax.ShapeDtypeStruct((8, LANES), F32), name="exchange"))()


def _job_copies(job, src, dst, send_sems, recv_sems, local_sem):
    me = _my_index()
    peers = {"gather1": GATHER1_PEERS, "gather2": GATHER2_PEERS, "scatter": SCATTER_PEERS}[job.kind]
    sends, recvs = [], []
    for i, k in enumerate(peers):
        if job.kind == "gather1":
            s_ref, d_ref, to, got = src, dst.at[me], _peer(k), dst.at[_peer_index(k)]
        elif job.kind == "gather2":
            s_ref, d_ref, to, got = src.at[_peer_index(k)], dst.at[_peer_index(k)], _peer(1), dst.at[_peer_index(k | 1)]
        else:
            s_ref, d_ref, to, got = src.at[_peer_index(k)], dst.at[me], _peer(k), dst.at[_peer_index(k)]
        sends.append(pltpu.make_async_remote_copy(src_ref=s_ref, dst_ref=d_ref, send_sem=send_sems.at[i], recv_sem=recv_sems.at[i],
                                                  device_id=to, device_id_type=MESH_ID))
        recvs.append(pltpu.make_async_remote_copy(src_ref=s_ref, dst_ref=got, send_sem=send_sems.at[i], recv_sem=recv_sems.at[i],
                                                  device_id=to, device_id_type=MESH_ID))
    local = None
    if job.kind == "gather1":
        local = pltpu.make_async_copy(src, dst.at[me], local_sem)
    elif job.kind == "scatter":
        local = pltpu.make_async_copy(src.at[me], dst.at[me], local_sem)
    return sends, recvs, local


def _carry(body, jobs, comm, kw):
    kw = dict(kw)
    grid = tuple(kw.get("grid", ()))
    in_specs = list(kw["in_specs"])
    single = not isinstance(kw["out_specs"], (list, tuple))
    out_specs = [kw["out_specs"]] if single else list(kw["out_specs"])
    out_shape = [kw["out_shape"]] if single else list(kw["out_shape"])
    scratch = list(kw.get("scratch_shapes", []))
    n_in, n_out, n_scr, nj = len(in_specs), len(out_specs), len(scratch), len(jobs)
    any_spec = pl.BlockSpec(memory_space=pl.ANY)

    def wrapped(*refs):
        ins, rin = refs[:n_in], refs[n_in:n_in + nj]
        outs, rout = refs[n_in + nj:n_in + nj + n_out], refs[n_in + nj + n_out:n_in + 2 * nj + n_out]
        scr = refs[n_in + 2 * nj + n_out:n_in + 2 * nj + n_out + n_scr]
        send_sems, recv_sems, local_sems = refs[n_in + 2 * nj + n_out + n_scr:]

        def copies():
            return [_job_copies(job, rin[a], rout[a], send_sems.at[a], recv_sems.at[a], local_sems.at[a]) for a, job in enumerate(jobs)]

        def start():
            for sends, _, local in copies():
                if local is not None:
                    local.start()
                for cp in sends:
                    cp.start()

        def finish():
            for sends, recvs, local in copies():
                for cp in recvs:
                    cp.wait_recv()
                for cp in sends:
                    cp.wait_send()
                if local is not None:
                    local.wait()

        if grid:
            first = functools.reduce(jnp.logical_and, [pl.program_id(a) == 0 for a in range(len(grid))])
            last = functools.reduce(jnp.logical_and, [pl.program_id(a) == grid[a] - 1 for a in range(len(grid))])
            pl.when(first)(start)
            body(*ins, *outs, *scr)
            pl.when(last)(finish)
        else:
            start()
            body(*ins, *outs, *scr)
            finish()

    aliases = {n_in + a: n_out + a for a, job in enumerate(jobs) if job.kind == "gather2"}
    extra = dict(dimension_semantics=("arbitrary",) * len(grid)) if grid else {}
    call = _raw_call(wrapped, in_specs=in_specs + [any_spec] * nj, out_specs=out_specs + [any_spec] * nj,
                     out_shape=out_shape + [job.out_shape for job in jobs],
                     scratch_shapes=scratch + [pltpu.SemaphoreType.DMA((nj, MAX_SEMS)), pltpu.SemaphoreType.DMA((nj, MAX_SEMS)),
                                               pltpu.SemaphoreType.DMA((nj,))],
                     input_output_aliases=aliases, compiler_params=_params(**extra), name=kw["name"],
                     **({"grid": grid} if grid else {}))

    def run(*args):
        res = call(*args, *[job.src for job in jobs])
        for job, r in zip(jobs, res[n_out:]):
            comm.finish(job, r)
        return res[0] if single else list(res[:n_out])
    return run


def _adam(g, w, m, v):
    m2 = ADAM_B1 * m + (1.0 - ADAM_B1) * g
    v2 = ADAM_B2 * v + (1.0 - ADAM_B2) * (g * g)
    m_hat = m2 / (1.0 - ADAM_B1 ** ADAM_STEP)
    v_hat = v2 / (1.0 - ADAM_B2 ** ADAM_STEP)
    delta = -ADAM_LR * (m_hat / (jnp.sqrt(v_hat) + ADAM_EPS) + ADAM_WD * w)
    return delta, m2, v2


def _sum_adam(parts, w, m, v, name):
    r, c = w.shape
    tr = _pick(r, max(8, (1 << 19) // c), 8)

    def body(p_ref, w_ref, m_ref, v_ref, g_ref, d_ref, m2_ref, v2_ref):
        g = p_ref[0].astype(F32)
        for j in range(1, N_DEV):
            g = g + p_ref[j].astype(F32)
        delta, m2, v2 = _adam(g, w_ref[...], m_ref[...], v_ref[...])
        g_ref[...] = g
        d_ref[...] = delta
        m2_ref[...] = m2
        v2_ref[...] = v2

    blk = pl.BlockSpec((tr, c), lambda i: (i, 0))
    shp = jax.ShapeDtypeStruct((r, c), F32)
    return _pcall(body, carry_us=44.0 * r * c / HBM_BYTES_PER_US, grid=(r // tr,),
                  in_specs=[pl.BlockSpec((N_DEV, tr, c), lambda i: (0, i, 0)), blk, blk, blk],
                  out_specs=[blk] * 4, out_shape=[shp] * 4, compiler_params=_params(("parallel",)), name=name)(parts, w, m, v)


def _small_step(rep_parts, rep_w, rep_m, rep_v, sh_parts, sh_w, sh_m, sh_v, name):
    nr, ns = len(rep_parts), len(sh_parts)
    rows_r = [-(-p.shape[1] // LANES) for p in rep_parts]
    off_r = [sum(rows_r[:i]) for i in range(nr)]
    tot_r = -(-max(sum(rows_r), 8) // 8) * 8
    rows_s = [-(-p.shape[1] // 8) * 8 for p in sh_parts]
    cmax = max([p.shape[2] for p in sh_parts] + [LANES])
    off_s = [sum(rows_s[:i]) for i in range(ns)]
    tot_s = max(sum(rows_s), 8)
    vm = pl.BlockSpec(memory_space=pltpu.VMEM)

    def body(*refs):
        pos = 0

        def take(k):
            nonlocal pos
            out = refs[pos:pos + k]
            pos += k
            return out
        rp, rw, rm, rv = take(nr), take(nr), take(nr), take(nr)
        sp, sw, sm, sv = take(ns), take(ns), take(ns), take(ns)
        rg, rd, rm2, rv2 = take(nr), take(nr), take(nr), take(nr)
        sg, sd, sm2, sv2 = take(ns), take(ns), take(ns), take(ns)
        pack_r, got_r, pack_s, got_s, send_r, recv_r, send_s, recv_s = take(8)
        me = _my_index()
        pack_r[...] = jnp.zeros_like(pack_r)
        pack_s[...] = jnp.zeros_like(pack_s)
        for i in range(nr):
            nfull = rep_parts[i].shape[1]
            for rr in range(rows_r[i]):
                wdt = min(LANES, nfull - rr * LANES)
                pack_r[off_r[i] + rr:off_r[i] + rr + 1, 0:wdt] = rp[i][0:1, rr * LANES:rr * LANES + wdt]
        for i in range(ns):
            _, r_i, c_i = sh_parts[i].shape
            for j in range(N_DEV):
                pack_s[j, off_s[i]:off_s[i] + r_i, 0:c_i] = sp[i][j]
        got_r[me] = pack_r[...]
        got_s[me] = pack_s[me]
        sends = []
        for k in range(1, N_DEV):
            a = pltpu.make_async_remote_copy(src_ref=pack_r, dst_ref=got_r.at[me], send_sem=send_r.at[k - 1], recv_sem=recv_r.at[k - 1],
                                             device_id=_peer(k), device_id_type=MESH_ID)
            b = pltpu.make_async_remote_copy(src_ref=pack_s.at[_peer_index(k)], dst_ref=got_s.at[me], send_sem=send_s.at[k - 1],
                                             recv_sem=recv_s.at[k - 1], device_id=_peer(k), device_id_type=MESH_ID)
            a.start()
            b.start()
            sends += [a, b]
        for k in range(1, N_DEV):
            pltpu.make_async_remote_copy(src_ref=pack_r, dst_ref=got_r.at[_peer_index(k)], send_sem=send_r.at[k - 1],
                                         recv_sem=recv_r.at[k - 1], device_id=_peer(k), device_id_type=MESH_ID).wait_recv()
            pltpu.make_async_remote_copy(src_ref=pack_s.at[me], dst_ref=got_s.at[_peer_index(k)], send_sem=send_s.at[k - 1],
                                         recv_sem=recv_s.at[k - 1], device_id=_peer(k), device_id_type=MESH_ID).wait_recv()
        for cp in sends:
            cp.wait_send()
        tot_rep = got_r[0]
        tot_sh = got_s[0]
        for j in range(1, N_DEV):
            tot_rep = tot_rep + got_r[j]
            tot_sh = tot_sh + got_s[j]
        pack_r[...] = tot_rep
        pack_s[0] = tot_sh
        for i in range(nr):
            nfull = rep_parts[i].shape[1]
            for rr in range(rows_r[i]):
                wdt = min(LANES, nfull - rr * LANES)
                rg[i][0:1, rr * LANES:rr * LANES + wdt] = pack_r[off_r[i] + rr:off_r[i] + rr + 1, 0:wdt]
            g = rg[i][...]
            delta, m2, v2 = _adam(g, rw[i][...], rm[i][...], rv[i][...])
            rd[i][...] = delta
            rm2[i][...] = m2
            rv2[i][...] = v2
        for i in range(ns):
            _, r_i, c_i = sh_parts[i].shape
            g = pack_s[0, off_s[i]:off_s[i] + r_i, 0:c_i]
            delta, m2, v2 = _adam(g, sw[i][...], sm[i][...], sv[i][...])
            sg[i][...] = g
            sd[i][...] = delta
            sm2[i][...] = m2
            sv2[i][...] = v2

    rep_shapes = [jax.ShapeDtypeStruct(p.shape, F32) for p in rep_parts]
    sh_shapes = [jax.ShapeDtypeStruct(p.shape[1:], F32) for p in sh_parts]
    n_in = 4 * nr + 4 * ns
    outs = _pcall(body, in_specs=[vm] * n_in, out_specs=[vm] * n_in, out_shape=rep_shapes * 4 + sh_shapes * 4,
                  scratch_shapes=[pltpu.VMEM((tot_r, LANES), F32), pltpu.VMEM((N_DEV, tot_r, LANES), F32),
                                  pltpu.VMEM((N_DEV, tot_s, cmax), F32), pltpu.VMEM((N_DEV, tot_s, cmax), F32),
                                  pltpu.SemaphoreType.DMA((N_DEV - 1,)), pltpu.SemaphoreType.DMA((N_DEV - 1,)),
                                  pltpu.SemaphoreType.DMA((N_DEV - 1,)), pltpu.SemaphoreType.DMA((N_DEV - 1,))],
                  compiler_params=_params(), name=name)(
        *rep_parts, *rep_w, *rep_m, *rep_v, *sh_parts, *sh_w, *sh_m, *sh_v)
    rep_out = [outs[i * nr:(i + 1) * nr] for i in range(4)]
    sh_out = [outs[4 * nr + i * ns:4 * nr + (i + 1) * ns] for i in range(4)]
    return rep_out, sh_out


_CONF = ("norm_g", "a_w_in", "a_b_in", "a_dw_w", "a_dw_b", "a_ln_g", "a_ln_b", "a_w_out", "a_b_out")
_FFN = ("ffn_norm_g", "ffn_w_up", "ffn_dw_w", "ffn_dw_b", "ffn_w_down")
_POOL = ("norm_g", "b_w_group", "b_scale")
_ATTN = ("norm_g", "c_w_qkv", "c_q_norm_g", "c_k_norm_g", "c_sinks", "c_w_o")
_LAYERS = (_CONF + _FFN, _POOL + _FFN, _ATTN + _FFN, _CONF + _FFN)
_NAMES = tuple("l%d_%s" % (i, n) for i, names in enumerate(_LAYERS) for n in names)
_BIG = ("a_w_in", "a_w_out", "ffn_w_up", "ffn_w_down", "b_w_group", "c_w_qkv", "c_w_o")
_SHARDED_SMALL = ("a_dw_w", "ffn_dw_w")


def _pad_rows(a, mult=8):
    r = a.shape[0]
    rp = -(-r // mult) * mult
    return a if rp == r else jnp.pad(a, ((0, rp - r), (0, 0)))


def _unstack_cols(st, rows):
    s, r, cs = st.shape
    return jnp.transpose(st, (1, 0, 2)).reshape(r, s * cs)[:rows]


def _stack_cols(a):
    r, c = a.shape
    return jnp.transpose(a.reshape(r, N_DEV, c // N_DEV), (1, 0, 2))


def _row(v):
    return v.reshape(1, -1)


def _ffn_forward(x_mid, p, tag):
    h2 = _rms_fwd(x_mid, _row(p["ffn_norm_g"]), BF16, "rms_fwd_bf16")
    u0 = _mm(h2, p["ffn_w_up"], "nn", name="ffn_up", tn=1408, tk=2048)
    a = _ffn_act_fwd(u0, p["ffn_dw_w"], _row(p["ffn_dw_b"]), "ffn_act_fwd")
    x_out = _mm(a, p["ffn_w_down"], "nn", res=x_mid, name="ffn_down", tk=1408)
    return x_out, dict(h2=h2, u0=u0, a=a)


def _ffn_backward(dx_out, x_mid, p, sv, grads):
    da = _mm(dx_out, p["ffn_w_down"], "nt", name="ffn_down_dx", tn=1408, tk=2048)
    dwd = _mm(sv["a"], dx_out, "tn", out_dtype=BF16, name="ffn_down_dw", tm=1408)
    grads["ffn_w_down"] = dwd.reshape(N_DEV, dwd.shape[0] // N_DEV, dwd.shape[1])
    du0, dww, dwb = _ffn_act_bwd(da, sv["u0"], p["ffn_dw_w"], _row(p["ffn_dw_b"]), "ffn_act_bwd")
    kw = dww.shape[1]
    grads["ffn_dw_w"] = _stack_cols(jnp.transpose(dww, (1, 0, 2)).reshape(kw, -1))
    grads["ffn_dw_b"] = dwb.reshape(1, -1)
    dh2 = _mm(du0, p["ffn_w_up"], "nt", name="ffn_up_dx", tk=1408)
    grads["ffn_w_up"] = _mm(sv["h2"], du0, "tn", out_dtype=BF16, out_stack=N_DEV, name="ffn_up_dw", tn=1408)
    dx_mid, dg, _ = _rms_bwd(dh2, x_mid, _row(p["ffn_norm_g"]), dx_out, "rms_bwd")
    grads["ffn_norm_g"] = dg
    return dx_mid


def _conf_forward(x, p):
    h = _rms_fwd(x, _row(p["norm_g"]), BF16, "rms_fwd_bf16")
    u = _mm(h, p["a_w_in"], "nn", bias=_row(p["a_b_in"]), name="conf_in", tn=512, tk=2048)
    cpre = _conf_conv_fwd(u, p["a_dw_w"], _row(p["a_dw_b"]), "conf_conv_fwd")
    s = _ln_silu_fwd(cpre, _row(p["a_ln_g"]), _row(p["a_ln_b"]), "ln_silu_fwd")
    x_mid = _mm(s, p["a_w_out"], "nn", bias=_row(p["a_b_out"]), res=x, name="conf_out", tk=2048)
    return x_mid, dict(h=h, u=u, cpre=cpre, s=s)


def _conf_backward(dx_mid, x, p, sv, grads):
    ds = _mm(dx_mid, p["a_w_out"], "nt", name="conf_out_dx", tk=2048)
    dwo = _mm(sv["s"], dx_mid, "tn", out_dtype=BF16, name="conf_out_dw")
    grads["a_w_out"] = dwo.reshape(N_DEV, dwo.shape[0] // N_DEV, dwo.shape[1])
    dc, dlg, dlb = _ln_silu_bwd(ds, sv["cpre"], _row(p["a_ln_g"]), _row(p["a_ln_b"]), "ln_silu_bwd")
    grads["a_ln_g"], grads["a_ln_b"] = dlg, dlb
    du, dww, dwb, dbin = _conf_conv_bwd(dc, sv["u"], p["a_dw_w"], "conf_conv_bwd")
    grads["a_dw_w"] = _stack_cols(dww)
    grads["a_dw_b"] = dwb
    grads["a_b_in"] = dbin.reshape(1, -1)
    dh = _mm(du, p["a_w_in"], "nt", name="conf_in_dx", tk=512)
    grads["a_w_in"] = _mm(sv["h"], du, "tn", out_dtype=BF16, out_stack=N_DEV, name="conf_in_dw", tn=512)
    dx, dg, dbo = _rms_bwd(dh, x, _row(p["norm_g"]), dx_mid, "rms_bwd")
    grads["norm_g"] = dg
    grads["a_b_out"] = dbo
    return dx


def _pool_forward(x, p):
    h = _rms_fwd(x, _row(p["norm_g"]), F32, "rms_fwd_f32")
    mixed = _pool_fwd(h, "pool_fwd")
    x_mid = _pool_mm_fwd(mixed, p["b_w_group"], _row(p["b_scale"]), x, "pool_mm_fwd")
    return x_mid, dict(mixed=mixed)


def _pool_backward(dx_mid, x, p, sv, grads):
    dmixed, dwg, dscale = _pool_mm_bwd(dx_mid, sv["mixed"], p["b_w_group"], _row(p["b_scale"]), "pool_mm_bwd")
    ng, gd, _ = dwg.shape
    grads["b_w_group"] = jnp.transpose(dwg.reshape(ng, N_DEV, gd // N_DEV, gd), (1, 0, 2, 3)).reshape(N_DEV, ng * gd // N_DEV, gd).astype(BF16)
    grads["b_scale"] = dscale
    dh = _pool_bwd(dmixed, "pool_bwd")
    dx, dg, _ = _rms_bwd(dh, x, _row(p["norm_g"]), dx_mid, "rms_bwd")
    grads["norm_g"] = dg
    return dx


def _attn_tables(p, positions, d_model):
    n_q = d_model // HEAD
    n_kv = n_q // 8
    tabs = _rope_tables(positions)
    gq2 = jnp.concatenate([p["c_q_norm_g"], p["c_q_norm_g"]]).reshape(1, LANES)
    gk2 = jnp.concatenate([p["c_k_norm_g"], p["c_k_norm_g"]]).reshape(1, LANES)
    sink_tab = jnp.repeat(jnp.repeat(p["c_sinks"].reshape(-1, 2), HEAD, axis=1), 8, axis=0)
    return n_q, n_kv, tabs, gq2, gk2, sink_tab


def _attn_forward(x, p, positions):
    n_q, n_kv, tabs, gq2, gk2, sink_tab = _attn_tables(p, positions, x.shape[1])
    h = _rms_fwd(x, _row(p["norm_g"]), BF16, "rms_fwd_bf16")
    qkv = _mm(h, p["c_w_qkv"], "nn", name="attn_qkv", tn=1280, tk=2048)
    q, k2, v2 = _qk_prep_fwd(qkv, tabs, gq2, gk2, n_q, n_kv, "qk_prep_fwd")
    o, lse = _attn_fwd(q, k2, v2, sink_tab, "attn_fwd")
    x_mid = _mm(o, p["c_w_o"], "nn", res=x, name="attn_out", tk=2048)
    return x_mid, dict(h=h, qkv=qkv, q=q, k2=k2, v2=v2, o=o, lse=lse)


def _attn_backward(dx_mid, x, p, positions, sv, grads):
    n_q, n_kv, tabs, gq2, gk2, sink_tab = _attn_tables(p, positions, x.shape[1])
    do = _mm(dx_mid, p["c_w_o"], "nt", name="attn_out_dx", tk=2048)
    dwo = _mm(sv["o"], dx_mid, "tn", out_dtype=BF16, name="attn_out_dw")
    grads["c_w_o"] = dwo.reshape(N_DEV, dwo.shape[0] // N_DEV, dwo.shape[1])
    dq, dkc, dkp, dvc, dvp, dsk = _attn_bwd(do, sv["q"], sv["o"], sv["lse"], sv["k2"], sv["v2"], sink_tab, "attn_bwd")
    nb = x.shape[0] // Q_BLOCK
    dsk = dsk.reshape(-1, nb, 8, LANES)[:, :, 0, :].sum(axis=1)
    grads["c_sinks"] = jnp.stack([dsk[:, 0], dsk[:, HEAD]], axis=1).reshape(1, -1)
    dqkv, dgq, dgk = _qk_prep_bwd(dq, dkc, dkp, dvc, dvp, sv["qkv"], tabs, gq2, gk2, n_q, n_kv, "qk_prep_bwd")
    grads["c_q_norm_g"] = dgq[:, :HEAD] + dgq[:, HEAD:]
    grads["c_k_norm_g"] = dgk[:, :HEAD] + dgk[:, HEAD:]
    dh = _mm(dqkv, p["c_w_qkv"], "nt", name="attn_qkv_dx", tk=1280)
    dwq = _mm(sv["h"], dqkv, "tn", out_dtype=BF16, name="attn_qkv_dw", tn=1280)
    grads["c_w_qkv"] = _stack_cols(dwq)
    dx, dg, _ = _rms_bwd(dh, x, _row(p["norm_g"]), dx_mid, "rms_bwd")
    grads["norm_g"] = dg
    return dx


class _LayerWeights:
    def __init__(self, li, weights, small_full, comm):
        self.li, self.weights, self.small_full, self.comm, self.cache = li, weights, small_full, comm, {}

    def __getitem__(self, nme):
        if nme not in self.cache:
            self.cache[nme] = self.fetch(nme)
        return self.cache[nme]

    def fetch(self, nme):
        full = "l%d_%s" % (self.li, nme)
        w = self.weights[full]
        if nme in _SHARDED_SMALL:
            return _unstack_cols(self.small_full[full], w.shape[0])
        if nme not in _BIG:
            return w
        got = self.comm.need(full)
        if nme in ("a_w_in", "ffn_w_up"):
            return got
        if nme == "c_w_qkv":
            return _unstack_cols(got, w.shape[0])
        if nme == "b_w_group":
            ng, gs, gd = w.shape
            return jnp.transpose(got.reshape(N_DEV, ng, gs, gd), (1, 0, 2, 3)).reshape(ng, N_DEV * gs, gd)
        return got.reshape(-1, w.shape[1])


class _LayerGrads(dict):
    def __init__(self, li, comm):
        super().__init__()
        self.li, self.comm = li, comm

    def __setitem__(self, nme, value):
        if nme in _BIG:
            self.comm.push("l%d_%s" % (self.li, nme), "scatter", value)
        else:
            super().__setitem__(nme, value)


def kernel(*args):
    n_w = len(_NAMES)
    x, positions = args[0], args[1]
    weights = dict(zip(_NAMES, args[2:2 + n_w]))
    loss_target = args[2 + n_w]
    moms = dict(zip(_NAMES, args[3 + n_w:3 + 2 * n_w]))
    vels = dict(zip(_NAMES, args[3 + 2 * n_w:3 + 3 * n_w]))
    x0 = x[0]
    pos = positions[0]
    kinds = ("conf", "pool", "attn", "conf")
    comm = _Comm()
    _STATE["comm"] = comm
    shd = [n for n in _NAMES if n.split("_", 1)[1] in _SHARDED_SMALL]
    small_full = dict(zip(shd, _all_gather([_pad_rows(weights[n]) for n in shd], "gather_small")))
    for n in _NAMES:
        if n.split("_", 1)[1] in _BIG:
            w = weights[n]
            comm.push(n, "gather1", w.astype(BF16).reshape(-1, w.shape[-1]))
    results = {}

    def update_ready():
        while comm.scattered:
            full, parts = comm.scattered.pop(0)
            w = weights[full]
            w2 = w.reshape(-1, w.shape[-1])
            outs = _sum_adam(parts, w2, moms[full].reshape(w2.shape), vels[full].reshape(w2.shape), "adam_" + full.split("_", 1)[1])
            results[full] = tuple(o.reshape(w.shape) for o in outs)

    params, saved = [], []
    cur = x0
    for li, names in enumerate(_LAYERS):
        p = _LayerWeights(li, weights, small_full, comm)
        if kinds[li] == "conf":
            x_mid, sv = _conf_forward(cur, p)
        elif kinds[li] == "pool":
            x_mid, sv = _pool_forward(cur, p)
        else:
            x_mid, sv = _attn_forward(cur, p, pos)
        x_out, sv_f = _ffn_forward(x_mid, p, kinds[li])
        params.append(p)
        saved.append((sv, sv_f, cur, x_mid))
        cur = x_out
    dy, loss_part = _loss_head(cur, loss_target[0], "loss_head")
    loss = lax.psum(loss_part[0, 0], ("x", "y", "c"))

    small_grads = {}
    dcur = dy
    for li in range(len(_LAYERS) - 1, -1, -1):
        p = params[li]
        sv, sv_f, x_in, x_mid = saved[li]
        grads = _LayerGrads(li, comm)
        dmid = _ffn_backward(dcur, x_mid, p, sv_f, grads)
        update_ready()
        if kinds[li] == "conf":
            dcur = _conf_backward(dmid, x_in, p, sv, grads)
        elif kinds[li] == "pool":
            dcur = _pool_backward(dmid, x_in, p, sv, grads)
        else:
            dcur = _attn_backward(dmid, x_in, p, pos, sv, grads)
        update_ready()
        for n in _LAYERS[li]:
            if n not in _BIG:
                small_grads["l%d_%s" % (li, n)] = grads[n]
    while comm.queue or comm.scattered:
        if not comm.scattered:
            comm.flush(comm.take(1.0))
        update_ready()
    _STATE["comm"] = None

    rep = [n for n in _NAMES if n.split("_", 1)[1] not in _BIG and n.split("_", 1)[1] not in _SHARDED_SMALL]
    rep_out, sh_out = _small_step(
        [small_grads[n] for n in rep], [_row(weights[n]) for n in rep], [_row(moms[n]) for n in rep], [_row(vels[n]) for n in rep],
        [jnp.stack([_pad_rows(small_grads[n][j]) for j in range(N_DEV)]) for n in shd],
        [_pad_rows(weights[n]) for n in shd], [_pad_rows(moms[n]) for n in shd], [_pad_rows(vels[n]) for n in shd], "small_step")
    for i, n in enumerate(rep):
        results[n] = tuple(rep_out[k][i].reshape(weights[n].shape) for k in range(4))
    for i, n in enumerate(shd):
        results[n] = tuple(sh_out[k][i][:weights[n].shape[0]] for k in range(4))

    grad_x = dcur[None]
    out = [loss, grad_x]
    for k in range(4):
        out += [results[n][k] for n in _NAMES]
    return tuple(out)
```

```python
import functools
import math

import jax
import jax.numpy as jnp
from jax import lax
from jax.experimental import pallas as pl
from jax.experimental.pallas import tpu as pltpu

F32 = jnp.float32
BF16 = jnp.bfloat16
N_DEV = 8
EPS = 1e-6
LANES = 128
HEAD = 64
Q_BLOCK = 128
ROT_DIM = 16
ROPE_THETA = 500000.0
POOL_WINDOWS = (2, 4, 8, 16)
HALO = 32
ROWS = 256
VMEM_LIMIT = 56 * 1024 * 1024
ADAM_LR, ADAM_B1, ADAM_B2, ADAM_EPS, ADAM_WD, ADAM_STEP = 0.001, 0.9, 0.999, 1e-08, 0.01, 10
MESH_ID = pl.DeviceIdType.MESH
MXU_FLOPS_PER_US = 7.5e8
HBM_BYTES_PER_US = 2.5e6
ATTN_US_PER_STEP = 0.85


def _make_call(body, **kw):
    return pl.pallas_call(body, **kw)


_STATE = {"comm": None, "last": None}


def _raw_call(body, **kw):
    call = _make_call(body, **kw)

    def run(*args):
        last = _STATE["last"]
        if last is not None and args:
            first, _ = lax.optimization_barrier((args[0], last))
            args = (first,) + tuple(args[1:])
        res = call(*args)
        _STATE["last"] = res[0] if isinstance(res, (list, tuple)) else res
        return res
    return run


def _pcall(body, carry_us=0.0, **kw):
    comm = _STATE["comm"]
    jobs = comm.take(carry_us) if (comm is not None and carry_us > 0) else []
    if not jobs:
        return _raw_call(body, **kw)
    return _carry(body, jobs, comm, kw)


def _params(sem=None, **kw):
    if sem is not None:
        kw["dimension_semantics"] = sem
    return pltpu.CompilerParams(vmem_limit_bytes=VMEM_LIMIT, **kw)


def _pick(dim, pref, mult=LANES):
    best = None
    d = mult
    while d <= min(dim, pref):
        if dim % d == 0:
            best = d
        d += mult
    return dim if best is None else best


def _sigmoid(x):
    return 1.0 / (1.0 + jnp.exp(-x))


def _fold8(p):
    r, c = p.shape
    return p.reshape(r // 8, 8, c).sum(axis=0)


def _rows(e, k, r):
    n = e.shape[0]
    if k % 8 == 0:
        return e[k:k + r]
    return pltpu.roll(e, n - k, 0)[0:r]


def _lshape(a):
    return a.shape if a.ndim == 2 else (a.shape[1], a.shape[0] * a.shape[2])


def _panel(a):
    return a.shape[1] if a.ndim == 2 else a.shape[2]


def _lspec(a, br, bc, rc):
    if a.ndim == 2:
        return pl.BlockSpec((br, bc), rc)
    per = a.shape[2] // bc

    def idx(i, j, k):
        r, c = rc(i, j, k)
        return (c // per, r, c % per)
    return pl.BlockSpec((None, br, bc), idx)


def _mm(a, b, dims, *, name, out_dtype=F32, out_stack=None, bias=None, res=None, tm=1024, tn=1024, tk=1024):
    (ar, ac), (br_, bc_) = _lshape(a), _lshape(b)
    if dims == "nn":
        m, k, n = ar, ac, bc_
        lim_m, lim_k, lim_n = m, min(_panel(a), k), _panel(b)
    elif dims == "nt":
        m, k, n = ar, ac, br_
        lim_m, lim_k, lim_n = m, math.gcd(_panel(a), _panel(b)), n
    else:
        m, k, n = ac, ar, bc_
        lim_m, lim_k, lim_n = _panel(a), k, _panel(b)
    if out_stack is not None:
        lim_n = math.gcd(lim_n, n // out_stack)
    sub = 16 if (out_dtype == BF16 or a.dtype == BF16) else 8
    tm = _pick(lim_m, tm, LANES if dims == "tn" else sub)
    tn = _pick(lim_n, tn)
    tk = _pick(lim_k, tk, sub if dims == "tn" else LANES)
    nk = k // tk
    if dims == "tn":
        a_spec = _lspec(a, tk, tm, lambda i, j, kk: (kk, i))
    else:
        a_spec = _lspec(a, tm, tk, lambda i, j, kk: (i, kk))
    if dims == "nt":
        b_spec = _lspec(b, tn, tk, lambda i, j, kk: (j, kk))
    else:
        b_spec = _lspec(b, tk, tn, lambda i, j, kk: (kk, j))
    contract = {"nn": ((1,), (0,)), "nt": ((1,), (1,)), "tn": ((0,), (0,))}[dims]
    in_specs, args = [a_spec, b_spec], [a, b]
    if bias is not None:
        in_specs.append(pl.BlockSpec((1, tn), lambda i, j, kk: (0, j)))
        args.append(bias)
    if res is not None:
        in_specs.append(pl.BlockSpec((tm, tn), lambda i, j, kk: (i, j)))
        args.append(res)
    if out_stack is None:
        out_shape = jax.ShapeDtypeStruct((m, n), out_dtype)
    else:
        out_shape = jax.ShapeDtypeStruct((out_stack, m, n // out_stack), out_dtype)
    o_spec = _lspec(out_shape, tm, tn, lambda i, j, kk: (i, j))
    has_bias, has_res = bias is not None, res is not None

    def body(*refs):
        a_ref, b_ref = refs[0], refs[1]
        pos = 2
        bias_ref = res_ref = None
        if has_bias:
            bias_ref = refs[pos]
            pos += 1
        if has_res:
            res_ref = refs[pos]
            pos += 1
        o_ref = refs[pos]

        def part():
            return lax.dot_general(a_ref[...].astype(BF16), b_ref[...].astype(BF16), (contract, ((), ())),
                                   preferred_element_type=F32)

        def finish(r):
            if has_bias:
                r = r + bias_ref[...]
            if has_res:
                r = r + res_ref[...]
            o_ref[...] = r.astype(out_dtype)

        if nk == 1:
            finish(part())
        else:
            acc = refs[pos + 1]
            kk = pl.program_id(2)

            @pl.when(kk == 0)
            def _():
                acc[...] = part()

            @pl.when(kk > 0)
            def _():
                acc[...] += part()

            @pl.when(kk == nk - 1)
            def _():
                finish(acc[...])

    scratch = [] if nk == 1 else [pltpu.VMEM((tm, tn), F32)]
    return _pcall(body, carry_us=2.0 * m * n * k / MXU_FLOPS_PER_US, grid=(m // tm, n // tn, nk), in_specs=in_specs, out_specs=o_spec, out_shape=out_shape,
                  scratch_shapes=scratch, compiler_params=_params(("parallel", "parallel", "arbitrary")), name=name)(*args)


def _rms_fwd(x, g, out_dtype, name):
    t, d = x.shape
    tm = _pick(t, 512, 16)

    def body(x_ref, g_ref, o_ref):
        xv = x_ref[...]
        r = lax.rsqrt(jnp.mean(xv * xv, axis=-1, keepdims=True) + EPS)
        o_ref[...] = ((xv * r) * g_ref[...]).astype(out_dtype)

    return _pcall(body, grid=(t // tm,), in_specs=[pl.BlockSpec((tm, d), lambda i: (i, 0)), pl.BlockSpec((1, d), lambda i: (0, 0))],
                  out_specs=pl.BlockSpec((tm, d), lambda i: (i, 0)), out_shape=jax.ShapeDtypeStruct((t, d), out_dtype),
                  compiler_params=_params(("parallel",)), name=name)(x, g)


def _rms_bwd(dh, x, g, dres, name):
    t, d = x.shape
    tm = _pick(t, 256, 8)

    def body(dh_ref, x_ref, g_ref, dres_ref, dx_ref, dg_ref, cs_ref):
        xv, dhv, dr = x_ref[...], dh_ref[...], dres_ref[...]
        r = lax.rsqrt(jnp.mean(xv * xv, axis=-1, keepdims=True) + EPS)
        xh = xv * r
        dxh = dhv * g_ref[...]
        dx_ref[...] = dr + r * (dxh - xh * jnp.mean(dxh * xh, axis=-1, keepdims=True))
        pg = jnp.sum(dhv * xh, axis=0, keepdims=True)
        pc = jnp.sum(dr, axis=0, keepdims=True)

        @pl.when(pl.program_id(0) == 0)
        def _():
            dg_ref[...] = pg
            cs_ref[...] = pc

        @pl.when(pl.program_id(0) > 0)
        def _():
            dg_ref[...] += pg
            cs_ref[...] += pc

    row = pl.BlockSpec((tm, d), lambda i: (i, 0))
    vec = pl.BlockSpec((1, d), lambda i: (0, 0))
    return _pcall(body, grid=(t // tm,), in_specs=[row, row, vec, row], out_specs=[row, vec, vec],
                  out_shape=[jax.ShapeDtypeStruct((t, d), F32), jax.ShapeDtypeStruct((1, d), F32), jax.ShapeDtypeStruct((1, d), F32)],
                  compiler_params=_params(("arbitrary",)), name=name)(dh, x, g, dres)


def _loss_head(y, target, name):
    t, d = y.shape
    tm = _pick(t, 512, 8)

    def body(y_ref, t_ref, dy_ref, l_ref):
        e = y_ref[...] - t_ref[...]
        dy_ref[...] = e * (1.0 / d)
        part = 0.5 * jnp.sum(jnp.mean(e * e, axis=-1, keepdims=True), axis=0, keepdims=True)

        @pl.when(pl.program_id(0) == 0)
        def _():
            l_ref[...] = part

        @pl.when(pl.program_id(0) > 0)
        def _():
            l_ref[...] += part

    row = pl.BlockSpec((tm, d), lambda i: (i, 0))
    return _pcall(body, grid=(t // tm,), in_specs=[row, row], out_specs=[row, pl.BlockSpec((1, 1), lambda i: (0, 0))],
                  out_shape=[jax.ShapeDtypeStruct((t, d), F32), jax.ShapeDtypeStruct((1, 1), F32)],
                  compiler_params=_params(("arbitrary",)), name=name)(y, target)


def _ln_silu_fwd(c, g, b, name):
    t, d = c.shape
    tm = _pick(t, 512, 16)

    def body(c_ref, g_ref, b_ref, o_ref):
        cv = c_ref[...]
        xc = cv - jnp.mean(cv, axis=-1, keepdims=True)
        z = xc * lax.rsqrt(jnp.mean(xc * xc, axis=-1, keepdims=True) + EPS) * g_ref[...] + b_ref[...]
        o_ref[...] = (z * _sigmoid(z)).astype(BF16)

    row = pl.BlockSpec((tm, d), lambda i: (i, 0))
    vec = pl.BlockSpec((1, d), lambda i: (0, 0))
    return _pcall(body, grid=(t // tm,), in_specs=[row, vec, vec], out_specs=row, out_shape=jax.ShapeDtypeStruct((t, d), BF16),
                  compiler_params=_params(("parallel",)), name=name)(c, g, b)


def _ln_silu_bwd(ds, c, g, b, name):
    t, d = c.shape
    tm = _pick(t, 256, 8)

    def body(ds_ref, c_ref, g_ref, b_ref, dc_ref, dg_ref, db_ref):
        cv = c_ref[...]
        xc = cv - jnp.mean(cv, axis=-1, keepdims=True)
        r = lax.rsqrt(jnp.mean(xc * xc, axis=-1, keepdims=True) + EPS)
        ch = xc * r
        z = ch * g_ref[...] + b_ref[...]
        sg = _sigmoid(z)
        dz = ds_ref[...] * (sg * (1.0 + z * (1.0 - sg)))
        dch = dz * g_ref[...]
        dc_ref[...] = r * (dch - jnp.mean(dch, axis=-1, keepdims=True) - ch * jnp.mean(dch * ch, axis=-1, keepdims=True))
        pg = jnp.sum(dz * ch, axis=0, keepdims=True)
        pb = jnp.sum(dz, axis=0, keepdims=True)

        @pl.when(pl.program_id(0) == 0)
        def _():
            dg_ref[...] = pg
            db_ref[...] = pb

        @pl.when(pl.program_id(0) > 0)
        def _():
            dg_ref[...] += pg
            db_ref[...] += pb

    row = pl.BlockSpec((tm, d), lambda i: (i, 0))
    vec = pl.BlockSpec((1, d), lambda i: (0, 0))
    return _pcall(body, grid=(t // tm,), in_specs=[row, row, vec, vec], out_specs=[row, vec, vec],
                  out_shape=[jax.ShapeDtypeStruct((t, d), F32), jax.ShapeDtypeStruct((1, d), F32), jax.ShapeDtypeStruct((1, d), F32)],
                  compiler_params=_params(("arbitrary",)), name=name)(ds, c, g, b)


def _steps(t):
    return t // ROWS


def _conf_conv_fwd(u, dw_w, dw_b, name):
    t, d2 = u.shape
    d = d2 // 2
    c = LANES
    ns = d // c
    kc = dw_w.shape[0]

    def body(a_ref, g_ref, w_ref, b_ref, o_ref, pad):
        pad[0:HALO, :] = jnp.zeros((HALO, c), F32)

        def glu(i, _):
            base = pl.multiple_of(i * ROWS, ROWS)
            pad[pl.ds(base + HALO, ROWS), :] = a_ref[pl.ds(base, ROWS), :] * _sigmoid(g_ref[pl.ds(base, ROWS), :])
            return 0
        lax.fori_loop(0, _steps(t), glu, 0)

        def conv(i, _):
            base = pl.multiple_of(i * ROWS, ROWS)
            e = pad[pl.ds(base, ROWS + HALO), :]
            acc = jnp.zeros((ROWS, c), F32) + b_ref[...]
            for j in range(kc):
                acc = acc + w_ref[j:j + 1, :] * _rows(e, HALO - (kc - 1) + j, ROWS)
            o_ref[pl.ds(base, ROWS), :] = acc
            return 0
        lax.fori_loop(0, _steps(t), conv, 0)

    return _pcall(body, grid=(ns,),
                  in_specs=[pl.BlockSpec((t, c), lambda s: (0, s)), pl.BlockSpec((t, c), lambda s: (0, s + ns)),
                            pl.BlockSpec((kc, c), lambda s: (0, s)), pl.BlockSpec((1, c), lambda s: (0, s))],
                  out_specs=pl.BlockSpec((t, c), lambda s: (0, s)), out_shape=jax.ShapeDtypeStruct((t, d), F32),
                  scratch_shapes=[pltpu.VMEM((t + HALO, c), F32)], compiler_params=_params(("parallel",)), name=name)(u, u, dw_w, dw_b)


def _conf_conv_bwd(dc, u, dw_w, name):
    t, d = dc.shape
    c = LANES
    ns = d // c
    kc = dw_w.shape[0]

    def body(dc_ref, a_ref, g_ref, w_ref, du_ref, dww_ref, dwb_ref, db_ref, padv, padd, accw, accb):
        padv[0:HALO, :] = jnp.zeros((HALO, c), F32)
        padd[t:t + HALO, :] = jnp.zeros((HALO, c), F32)
        accw[...] = jnp.zeros_like(accw)
        accb[...] = jnp.zeros_like(accb)

        def fill(i, _):
            base = pl.multiple_of(i * ROWS, ROWS)
            padv[pl.ds(base + HALO, ROWS), :] = a_ref[pl.ds(base, ROWS), :] * _sigmoid(g_ref[pl.ds(base, ROWS), :])
            padd[pl.ds(base, ROWS), :] = dc_ref[pl.ds(base, ROWS), :]
            return 0
        lax.fori_loop(0, _steps(t), fill, 0)

        def step(i, _):
            base = pl.multiple_of(i * ROWS, ROWS)
            ev = padv[pl.ds(base, ROWS + HALO), :]
            ed = padd[pl.ds(base, ROWS + HALO), :]
            dcc = ed[0:ROWS]
            dv = jnp.zeros((ROWS, c), F32)
            for j in range(kc):
                dv = dv + w_ref[j:j + 1, :] * _rows(ed, kc - 1 - j, ROWS)
                accw[j] = accw[j] + _fold8(dcc * _rows(ev, HALO - (kc - 1) + j, ROWS))
            accb[0] = accb[0] + _fold8(dcc)
            av = a_ref[pl.ds(base, ROWS), :]
            sg = _sigmoid(g_ref[pl.ds(base, ROWS), :])
            da = dv * sg
            dg = dv * av * sg * (1.0 - sg)
            du_ref[0, pl.ds(base, ROWS), :] = da.astype(BF16)
            du_ref[1, pl.ds(base, ROWS), :] = dg.astype(BF16)
            accb[1] = accb[1] + _fold8(da)
            accb[2] = accb[2] + _fold8(dg)
            return 0
        lax.fori_loop(0, _steps(t), step, 0)
        for j in range(kc):
            dww_ref[j:j + 1, :] = jnp.sum(accw[j], axis=0, keepdims=True)
        dwb_ref[...] = jnp.sum(accb[0], axis=0, keepdims=True)
        db_ref[0] = jnp.sum(accb[1], axis=0, keepdims=True)
        db_ref[1] = jnp.sum(accb[2], axis=0, keepdims=True)

    return _pcall(body, grid=(ns,),
                  in_specs=[pl.BlockSpec((t, c), lambda s: (0, s)), pl.BlockSpec((t, c), lambda s: (0, s)),
                            pl.BlockSpec((t, c), lambda s: (0, s + ns)), pl.BlockSpec((kc, c), lambda s: (0, s))],
                  out_specs=[pl.BlockSpec((2, t, c), lambda s: (0, 0, s)), pl.BlockSpec((kc, c), lambda s: (0, s)),
                             pl.BlockSpec((1, c), lambda s: (0, s)), pl.BlockSpec((2, 1, c), lambda s: (0, 0, s))],
                  out_shape=[jax.ShapeDtypeStruct((2, t, d), BF16), jax.ShapeDtypeStruct((kc, d), F32),
                             jax.ShapeDtypeStruct((1, d), F32), jax.ShapeDtypeStruct((2, 1, d), F32)],
                  scratch_shapes=[pltpu.VMEM((t + HALO, c), F32), pltpu.VMEM((t + HALO, c), F32),
                                  pltpu.VMEM((kc, 8, c), F32), pltpu.VMEM((3, 8, c), F32)],
                  compiler_params=_params(("parallel",)), name=name)(dc, u, u, dw_w)


def _ffn_act_fwd(u0, dw_w, dw_b, name):
    t, f2 = u0.shape
    f = f2 // 2
    c = LANES
    ns = f // c
    kw = dw_w.shape[0]

    def body(g_ref, v_ref, wg_ref, wv_ref, bg_ref, bv_ref, o_ref):
        def step(i, _):
            base = pl.multiple_of(i * ROWS, ROWS)
            lo = pl.multiple_of(jnp.maximum(base - HALO, 0), HALO)
            keep = jnp.where(i > 0, 1.0, 0.0)
            eg = jnp.concatenate([g_ref[pl.ds(lo, HALO), :] * keep, g_ref[pl.ds(base, ROWS), :]], axis=0)
            ev = jnp.concatenate([v_ref[pl.ds(lo, HALO), :] * keep, v_ref[pl.ds(base, ROWS), :]], axis=0)
            gate = jnp.zeros((ROWS, c), F32) + bg_ref[...]
            val = jnp.zeros((ROWS, c), F32) + bv_ref[...]
            for j in range(kw):
                gate = gate + wg_ref[j:j + 1, :] * _rows(eg, HALO - (kw - 1) + j, ROWS)
                val = val + wv_ref[j:j + 1, :] * _rows(ev, HALO - (kw - 1) + j, ROWS)
            o_ref[pl.ds(base, ROWS), :] = (gate * _sigmoid(gate) * val).astype(BF16)
            return 0
        lax.fori_loop(0, _steps(t), step, 0)

    return _pcall(body, grid=(ns,),
                  in_specs=[pl.BlockSpec((t, c), lambda s: (0, s)), pl.BlockSpec((t, c), lambda s: (0, s + ns)),
                            pl.BlockSpec((kw, c), lambda s: (0, s)), pl.BlockSpec((kw, c), lambda s: (0, s + ns)),
                            pl.BlockSpec((1, c), lambda s: (0, s)), pl.BlockSpec((1, c), lambda s: (0, s + ns))],
                  out_specs=pl.BlockSpec((t, c), lambda s: (0, s)), out_shape=jax.ShapeDtypeStruct((t, f), BF16),
                  compiler_params=_params(("parallel",)), name=name)(u0, u0, dw_w, dw_w, dw_b, dw_b)


def _ffn_act_bwd(da, u0, dw_w, dw_b, name):
    t, f = da.shape
    c = LANES
    ns = f // c
    kw = dw_w.shape[0]

    def body(da_ref, g_ref, v_ref, wg_ref, wv_ref, bg_ref, bv_ref, du_ref, dww_ref, dwb_ref, padg, padv, accw, accb):
        padg[t:t + HALO, :] = jnp.zeros((HALO, c), F32)
        padv[t:t + HALO, :] = jnp.zeros((HALO, c), F32)
        accw[...] = jnp.zeros_like(accw)
        accb[...] = jnp.zeros_like(accb)

        def first(i, _):
            base = pl.multiple_of(i * ROWS, ROWS)
            lo = pl.multiple_of(jnp.maximum(base - HALO, 0), HALO)
            keep = jnp.where(i > 0, 1.0, 0.0)
            eg = jnp.concatenate([g_ref[pl.ds(lo, HALO), :] * keep, g_ref[pl.ds(base, ROWS), :]], axis=0)
            ev = jnp.concatenate([v_ref[pl.ds(lo, HALO), :] * keep, v_ref[pl.ds(base, ROWS), :]], axis=0)
            gate = jnp.zeros((ROWS, c), F32) + bg_ref[...]
            val = jnp.zeros((ROWS, c), F32) + bv_ref[...]
            sh_g, sh_v = [], []
            for j in range(kw):
                sh_g.append(_rows(eg, HALO - (kw - 1) + j, ROWS))
                sh_v.append(_rows(ev, HALO - (kw - 1) + j, ROWS))
                gate = gate + wg_ref[j:j + 1, :] * sh_g[j]
                val = val + wv_ref[j:j + 1, :] * sh_v[j]
            dav = da_ref[pl.ds(base, ROWS), :]
            sg = _sigmoid(gate)
            dgate = dav * val * (sg * (1.0 + gate * (1.0 - sg)))
            dval = dav * (gate * sg)
            padg[pl.ds(base, ROWS), :] = dgate
            padv[pl.ds(base, ROWS), :] = dval
            for j in range(kw):
                accw[j] = accw[j] + _fold8(dgate * sh_g[j])
                accw[kw + j] = accw[kw + j] + _fold8(dval * sh_v[j])
            accb[0] = accb[0] + _fold8(dgate)
            accb[1] = accb[1] + _fold8(dval)
            return 0
        lax.fori_loop(0, _steps(t), first, 0)

        def second(i, _):
            base = pl.multiple_of(i * ROWS, ROWS)
            eg = padg[pl.ds(base, ROWS + HALO), :]
            ev = padv[pl.ds(base, ROWS + HALO), :]
            dg = jnp.zeros((ROWS, c), F32)
            dv = jnp.zeros((ROWS, c), F32)
            for j in range(kw):
                dg = dg + wg_ref[j:j + 1, :] * _rows(eg, kw - 1 - j, ROWS)
                dv = dv + wv_ref[j:j + 1, :] * _rows(ev, kw - 1 - j, ROWS)
            du_ref[0, pl.ds(base, ROWS), :] = dg.astype(BF16)
            du_ref[1, pl.ds(base, ROWS), :] = dv.astype(BF16)
            return 0
        lax.fori_loop(0, _steps(t), second, 0)
        for j in range(kw):
            dww_ref[0, j:j + 1, :] = jnp.sum(accw[j], axis=0, keepdims=True)
            dww_ref[1, j:j + 1, :] = jnp.sum(accw[kw + j], axis=0, keepdims=True)
        dwb_ref[0] = jnp.sum(accb[0], axis=0, keepdims=True)
        dwb_ref[1] = jnp.sum(accb[1], axis=0, keepdims=True)

    return _pcall(body, grid=(ns,),
                  in_specs=[pl.BlockSpec((t, c), lambda s: (0, s)),
                            pl.BlockSpec((t, c), lambda s: (0, s)), pl.BlockSpec((t, c), lambda s: (0, s + ns)),
                            pl.BlockSpec((kw, c), lambda s: (0, s)), pl.BlockSpec((kw, c), lambda s: (0, s + ns)),
                            pl.BlockSpec((1, c), lambda s: (0, s)), pl.BlockSpec((1, c), lambda s: (0, s + ns))],
                  out_specs=[pl.BlockSpec((2, t, c), lambda s: (0, 0, s)), pl.BlockSpec((2, kw, c), lambda s: (0, 0, s)),
                             pl.BlockSpec((2, 1, c), lambda s: (0, 0, s))],
                  out_shape=[jax.ShapeDtypeStruct((2, t, f), BF16), jax.ShapeDtypeStruct((2, kw, f), F32),
                             jax.ShapeDtypeStruct((2, 1, f), F32)],
                  scratch_shapes=[pltpu.VMEM((t + HALO, c), F32), pltpu.VMEM((t + HALO, c), F32),
                                  pltpu.VMEM((2 * kw, 8, c), F32), pltpu.VMEM((2, 8, c), F32)],
                  compiler_params=_params(("parallel",)), name=name)(da, u0, u0, dw_w, dw_w, dw_b, dw_b)


def _window_of(group):
    w = jnp.float32(POOL_WINDOWS[-1])
    for k in range(len(POOL_WINDOWS) - 2, -1, -1):
        w = jnp.where(group == k, jnp.float32(POOL_WINDOWS[k]), w)
    return w


def _select_level(group, levels):
    out = levels[-1]
    for k in range(len(levels) - 2, -1, -1):
        out = jnp.where(group == k, levels[k], out)
    return out


def _pool_fwd(h, name):
    t, d = h.shape
    c = LANES
    per = d // len(POOL_WINDOWS) // c

    def body(h_ref, o_ref):
        group = pl.program_id(0)
        wf = _window_of(group)

        def step(i, _):
            base = pl.multiple_of(i * ROWS, ROWS)
            lo = pl.multiple_of(jnp.maximum(base - HALO, 0), HALO)
            keep = jnp.where(i > 0, 1.0, 0.0)
            cur = h_ref[pl.ds(base, ROWS), :]
            e = jnp.concatenate([h_ref[pl.ds(lo, HALO), :] * keep, cur], axis=0)
            n = ROWS + HALO
            levels = []
            s = e
            for k in range(len(POOL_WINDOWS)):
                s = s + pltpu.roll(s, 1 << k, 0)
                levels.append(s[HALO:n])
            tpos = (base + lax.broadcasted_iota(jnp.int32, (ROWS, c), 0) + 1).astype(F32)
            pooled = _select_level(group, levels) / jnp.minimum(tpos, wf)
            o_ref[pl.ds(base, ROWS), :] = (pooled - cur).astype(BF16)
            return 0
        lax.fori_loop(0, _steps(t), step, 0)

    return _pcall(body, grid=(len(POOL_WINDOWS), per), in_specs=[pl.BlockSpec((t, c), lambda g, s: (0, g * per + s))],
                  out_specs=pl.BlockSpec((t, c), lambda g, s: (0, g * per + s)), out_shape=jax.ShapeDtypeStruct((t, d), BF16),
                  compiler_params=_params(("parallel", "parallel")), name=name)(h)


def _pool_bwd(dm, name):
    t, d = dm.shape
    c = LANES
    per = d // len(POOL_WINDOWS) // c

    def body(dm_ref, o_ref, pad):
        group = pl.program_id(0)
        wf = _window_of(group)
        pad[t:t + HALO, :] = jnp.zeros((HALO, c), F32)

        def fill(i, _):
            base = pl.multiple_of(i * ROWS, ROWS)
            tpos = (base + lax.broadcasted_iota(jnp.int32, (ROWS, c), 0) + 1).astype(F32)
            pad[pl.ds(base, ROWS), :] = dm_ref[pl.ds(base, ROWS), :] / jnp.minimum(tpos, wf)
            return 0
        lax.fori_loop(0, _steps(t), fill, 0)

        def step(i, _):
            base = pl.multiple_of(i * ROWS, ROWS)
            n = ROWS + HALO
            s = pad[pl.ds(base, n), :]
            levels = []
            for k in range(len(POOL_WINDOWS)):
                s = s + pltpu.roll(s, n - (1 << k), 0)
                levels.append(s[0:ROWS])
            o_ref[pl.ds(base, ROWS), :] = _select_level(group, levels) - dm_ref[pl.ds(base, ROWS), :]
            return 0
        lax.fori_loop(0, _steps(t), step, 0)

    return _pcall(body, grid=(len(POOL_WINDOWS), per), in_specs=[pl.BlockSpec((t, c), lambda g, s: (0, g * per + s))],
                  out_specs=pl.BlockSpec((t, c), lambda g, s: (0, g * per + s)), out_shape=jax.ShapeDtypeStruct((t, d), F32),
                  scratch_shapes=[pltpu.VMEM((t + HALO, c), F32)], compiler_params=_params(("parallel", "parallel")), name=name)(dm)


def _pool_mm_fwd(mixed, wg, scale, res, name):
    t, d = mixed.shape
    ng, gd, _ = wg.shape
    tm = _pick(t, 1024, 16)

    def body(a_ref, w_ref, s_ref, r_ref, o_ref):
        y = jnp.dot(a_ref[...], w_ref[...], preferred_element_type=F32)
        o_ref[...] = r_ref[...] + y * s_ref[...]

    blk = pl.BlockSpec((tm, gd), lambda g, i: (i, g))
    return _pcall(body, grid=(ng, t // tm),
                  in_specs=[blk, pl.BlockSpec((None, gd, gd), lambda g, i: (g, 0, 0)), pl.BlockSpec((1, gd), lambda g, i: (0, g)), blk],
                  out_specs=blk, out_shape=jax.ShapeDtypeStruct((t, d), F32),
                  compiler_params=_params(("parallel", "parallel")), name=name)(mixed, wg, scale, res)


def _pool_mm_bwd(dy, mixed, wg, scale, name):
    t, d = mixed.shape
    ng, gd, _ = wg.shape
    tm = _pick(t, 1024, 16)

    def body(dy_ref, a_ref, w_ref, s_ref, dm_ref, dw_ref, ds_ref):
        a, w, dyv = a_ref[...], w_ref[...], dy_ref[...]
        y = jnp.dot(a, w, preferred_element_type=F32)
        dyp = (dyv * s_ref[...]).astype(BF16)
        dm_ref[...] = lax.dot_general(dyp, w, (((1,), (1,)), ((), ())), preferred_element_type=F32)
        pw = lax.dot_general(a, dyp, (((0,), (0,)), ((), ())), preferred_element_type=F32)
        ps = jnp.sum(dyv * y, axis=0, keepdims=True)

        @pl.when(pl.program_id(1) == 0)
        def _():
            dw_ref[...] = pw
            ds_ref[...] = ps

        @pl.when(pl.program_id(1) > 0)
        def _():
            dw_ref[...] += pw
            ds_ref[...] += ps

    blk = pl.BlockSpec((tm, gd), lambda g, i: (i, g))
    wsp = pl.BlockSpec((None, gd, gd), lambda g, i: (g, 0, 0))
    vec = pl.BlockSpec((1, gd), lambda g, i: (0, g))
    return _pcall(body, grid=(ng, t // tm), in_specs=[blk, blk, wsp, vec], out_specs=[blk, wsp, vec],
                  out_shape=[jax.ShapeDtypeStruct((t, d), F32), jax.ShapeDtypeStruct((ng, gd, gd), F32), jax.ShapeDtypeStruct((1, d), F32)],
                  compiler_params=_params(("parallel", "arbitrary")), name=name)(dy, mixed, wg, scale)


def _rope_tables(positions):
    half = ROT_DIM // 2
    inv_freq = ROPE_THETA ** (-jnp.arange(0, ROT_DIM, 2, dtype=F32) / ROT_DIM)
    ang = positions.astype(F32)[:, None] * inv_freq
    cos, sin = jnp.cos(ang), jnp.sin(ang)
    t = positions.shape[0]
    ones = jnp.ones((t, HEAD - ROT_DIM), F32)
    zeros = jnp.zeros((t, HEAD - ROT_DIM), F32)
    zh = jnp.zeros((t, half), F32)
    c = jnp.concatenate([cos, cos, ones], axis=1)
    s1 = jnp.concatenate([-sin, zh, zeros], axis=1)
    s2 = jnp.concatenate([zh, sin, zeros], axis=1)
    return tuple(jnp.concatenate([a, a], axis=1) for a in (c, s1, s2))


def _half_mean(v, lo):
    s_lo = jnp.sum(jnp.where(lo, v, 0.0), axis=-1, keepdims=True)
    s_hi = jnp.sum(jnp.where(lo, 0.0, v), axis=-1, keepdims=True)
    return jnp.where(lo, s_lo, s_hi) * (1.0 / HEAD)


def _qk_prep_fwd(qkv, tabs, gq2, gk2, n_q, n_kv, name):
    t, width = qkv.shape
    tm = _pick(t, 256, 16)
    nqc, nkc = n_q * HEAD // LANES, n_kv * HEAD // LANES

    def body(x_ref, c_ref, s1_ref, s2_ref, gq_ref, gk_ref, q_ref, k2_ref, v2_ref):
        lo = lax.broadcasted_iota(jnp.int32, (tm, LANES), 1) < HEAD
        cv, s1, s2 = c_ref[...], s1_ref[...], s2_ref[...]

        def normrot(xc, g2):
            y = xc * lax.rsqrt(_half_mean(xc * xc, lo) + EPS) * g2
            return y * cv + pltpu.roll(y, LANES - ROT_DIM // 2, 1) * s1 + pltpu.roll(y, ROT_DIM // 2, 1) * s2

        def twice(y, j):
            sw = pltpu.roll(y, HEAD, 1)
            k2 = jnp.where(lo, y, sw) if j == 0 else jnp.where(lo, sw, y)
            return k2.astype(BF16)

        for ch in range(nqc):
            q_ref[:, ch * LANES:(ch + 1) * LANES] = normrot(x_ref[:, ch * LANES:(ch + 1) * LANES], gq_ref[...]).astype(BF16)
        for ch in range(nkc):
            off = (nqc + ch) * LANES
            y = normrot(x_ref[:, off:off + LANES], gk_ref[...])
            voff = (nqc + nkc + ch) * LANES
            vv = x_ref[:, voff:voff + LANES]
            for j in range(2):
                k2_ref[:, (2 * ch + j) * LANES:(2 * ch + j + 1) * LANES] = twice(y, j)
                v2_ref[:, (2 * ch + j) * LANES:(2 * ch + j + 1) * LANES] = twice(vv, j)

    row = lambda w: pl.BlockSpec((tm, w), lambda i: (i, 0))
    vec = pl.BlockSpec((1, LANES), lambda i: (0, 0))
    return _pcall(body, grid=(t // tm,), in_specs=[row(width), row(LANES), row(LANES), row(LANES), vec, vec],
                  out_specs=[row(n_q * HEAD), row(n_kv * LANES), row(n_kv * LANES)],
                  out_shape=[jax.ShapeDtypeStruct((t, n_q * HEAD), BF16), jax.ShapeDtypeStruct((t, n_kv * LANES), BF16),
                             jax.ShapeDtypeStruct((t, n_kv * LANES), BF16)],
                  compiler_params=_params(("parallel",)), name=name)(qkv, *tabs, gq2, gk2)


def _qk_prep_bwd(dq, dk_cur, dk_prev, dv_cur, dv_prev, qkv, tabs, gq2, gk2, n_q, n_kv, name):
    t, width = qkv.shape
    tm = Q_BLOCK
    nb = t // tm
    nqc, nkc = n_q * HEAD // LANES, n_kv * HEAD // LANES

    def body(dq_ref, kc_ref, kp_ref, vc_ref, vp_ref, x_ref, c_ref, s1_ref, s2_ref, gq_ref, gk_ref, o_ref, dgq_ref, dgk_ref):
        lo = lax.broadcasted_iota(jnp.int32, (tm, LANES), 1) < HEAD
        cv, s1, s2 = c_ref[...], s1_ref[...], s2_ref[...]
        more = jnp.where(pl.program_id(0) < nb - 1, 1.0, 0.0)

        def back(dy, xc, g2):
            dyn = dy * cv + pltpu.roll(dy * s1, ROT_DIM // 2, 1) + pltpu.roll(dy * s2, LANES - ROT_DIM // 2, 1)
            r = lax.rsqrt(_half_mean(xc * xc, lo) + EPS)
            xh = xc * r
            dxh = dyn * g2
            return r * (dxh - xh * _half_mean(dxh * xh, lo)), jnp.sum(dyn * xh, axis=0, keepdims=True)

        def unfold(cur_ref, prev_ref, ch):
            d0 = cur_ref[:, (2 * ch) * LANES:(2 * ch + 1) * LANES] + more * prev_ref[:, (2 * ch) * LANES:(2 * ch + 1) * LANES]
            d1 = cur_ref[:, (2 * ch + 1) * LANES:(2 * ch + 2) * LANES] + more * prev_ref[:, (2 * ch + 1) * LANES:(2 * ch + 2) * LANES]
            return jnp.where(lo, d0 + pltpu.roll(d0, HEAD, 1), d1 + pltpu.roll(d1, HEAD, 1))

        pq = jnp.zeros((1, LANES), F32)
        for ch in range(nqc):
            sl = slice(ch * LANES, (ch + 1) * LANES)
            dx, pg = back(dq_ref[:, sl], x_ref[:, sl], gq_ref[...])
            o_ref[:, sl] = dx.astype(BF16)
            pq = pq + pg
        pk = jnp.zeros((1, LANES), F32)
        for ch in range(nkc):
            sl = slice((nqc + ch) * LANES, (nqc + ch + 1) * LANES)
            dx, pg = back(unfold(kc_ref, kp_ref, ch), x_ref[:, sl], gk_ref[...])
            o_ref[:, sl] = dx.astype(BF16)
            pk = pk + pg
            vs = slice((nqc + nkc + ch) * LANES, (nqc + nkc + ch + 1) * LANES)
            o_ref[:, vs] = unfold(vc_ref, vp_ref, ch).astype(BF16)

        @pl.when(pl.program_id(0) == 0)
        def _():
            dgq_ref[...] = pq
            dgk_ref[...] = pk

        @pl.when(pl.program_id(0) > 0)
        def _():
            dgq_ref[...] += pq
            dgk_ref[...] += pk

    row = lambda w: pl.BlockSpec((tm, w), lambda i: (i, 0))
    nxt = lambda w: pl.BlockSpec((tm, w), lambda i: (jnp.minimum(i + 1, nb - 1), 0))
    vec = pl.BlockSpec((1, LANES), lambda i: (0, 0))
    kvw = n_kv * LANES
    return _pcall(body, grid=(nb,),
                  in_specs=[row(n_q * HEAD), row(kvw), nxt(kvw), row(kvw), nxt(kvw), row(width), row(LANES), row(LANES), row(LANES), vec, vec],
                  out_specs=[row(width), vec, vec],
                  out_shape=[jax.ShapeDtypeStruct((t, width), BF16), jax.ShapeDtypeStruct((1, LANES), F32), jax.ShapeDtypeStruct((1, LANES), F32)],
                  compiler_params=_params(("arbitrary",)), name=name)(dq, dk_cur, dk_prev, dv_cur, dv_prev, qkv, *tabs, gq2, gk2)


def _band_scores(qh, kc, kp, n, sink_row, lo_row, is_lo):
    scale = 1.0 / math.sqrt(HEAD)
    nt = (((1,), (1,)), ((), ()))
    s_c = lax.dot_general(qh, kc, nt, preferred_element_type=F32) * scale
    s_p = lax.dot_general(qh, kp, nt, preferred_element_type=F32) * scale
    qi = lax.broadcasted_iota(jnp.int32, (Q_BLOCK, Q_BLOCK), 0)
    kj = lax.broadcasted_iota(jnp.int32, (Q_BLOCK, Q_BLOCK), 1)
    s_c = jnp.where(kj <= qi, s_c, -jnp.inf)
    s_p = jnp.where((kj > qi) & (n > 0), s_p, -jnp.inf)
    pick = lo_row if is_lo else jnp.logical_not(lo_row)
    sink = jnp.max(jnp.where(pick, sink_row, -jnp.inf), axis=-1, keepdims=True)
    return s_c, s_p, sink


def _attn_fwd(q, k2, v2, sink_tab, name):
    t, dq = q.shape
    nc = dq // LANES
    nb = t // Q_BLOCK
    per_kv = nc // (k2.shape[1] // LANES)

    def body(q_ref, kc_ref, kp_ref, vc_ref, vp_ref, s_ref, o_ref, lse_ref):
        n = pl.program_id(1)
        lo = lax.broadcasted_iota(jnp.int32, (Q_BLOCK, LANES), 1) < HEAD
        lo_row = lax.broadcasted_iota(jnp.int32, (1, LANES), 1) < HEAD
        qv = q_ref[...].astype(F32)
        kc, kp, vc, vp = kc_ref[...], kp_ref[...], vc_ref[...], vp_ref[...]
        outs, lses = [], []
        for is_lo in (True, False):
            qh = jnp.where(lo, qv, 0.0) if is_lo else jnp.where(lo, 0.0, qv)
            s_c, s_p, sink = _band_scores(qh.astype(BF16), kc, kp, n, s_ref[0:1, :], lo_row, is_lo)
            m = jnp.maximum(jnp.maximum(jnp.max(s_c, axis=-1, keepdims=True), jnp.max(s_p, axis=-1, keepdims=True)), sink)
            p_c, p_p = jnp.exp(s_c - m), jnp.exp(s_p - m)
            denom = jnp.sum(p_c, axis=-1, keepdims=True) + jnp.sum(p_p, axis=-1, keepdims=True) + jnp.exp(sink - m)
            pv = jnp.dot(p_c.astype(BF16), vc, preferred_element_type=F32) + jnp.dot(p_p.astype(BF16), vp, preferred_element_type=F32)
            outs.append(pv / denom)
            lses.append(m + jnp.log(denom))
        o_ref[...] = jnp.where(lo, outs[0], outs[1]).astype(BF16)
        lse_ref[...] = jnp.where(lo, lses[0], lses[1])

    qs = pl.BlockSpec((Q_BLOCK, LANES), lambda c, n: (n, c))
    cur = pl.BlockSpec((Q_BLOCK, LANES), lambda c, n: (n, c // per_kv))
    prev = pl.BlockSpec((Q_BLOCK, LANES), lambda c, n: (jnp.maximum(n - 1, 0), c // per_kv))
    return _pcall(body, carry_us=ATTN_US_PER_STEP * nc * nb, grid=(nc, nb),
                  in_specs=[qs, cur, prev, cur, prev, pl.BlockSpec((8, LANES), lambda c, n: (c, 0))],
                  out_specs=[qs, pl.BlockSpec((None, Q_BLOCK, LANES), lambda c, n: (c, n, 0))],
                  out_shape=[jax.ShapeDtypeStruct((t, dq), BF16), jax.ShapeDtypeStruct((nc, t, LANES), F32)],
                  compiler_params=_params(("parallel", "parallel")), name=name)(q, k2, k2, v2, v2, sink_tab)


def _attn_bwd(do, q, o, lse, k2, v2, sink_tab, name):
    t, dq = q.shape
    nc = dq // LANES
    nb = t // Q_BLOCK
    nkv = k2.shape[1] // LANES
    per_kv = nc // nkv
    scale = 1.0 / math.sqrt(HEAD)
    tn_ = (((0,), (0,)), ((), ()))
    nt = (((1,), (1,)), ((), ()))

    def body(do_ref, q_ref, o_ref, lse_ref, kc_ref, kp_ref, vc_ref, vp_ref, s_ref,
             dq_ref, dkc_ref, dkp_ref, dvc_ref, dvp_ref, dsk_ref):
        n = pl.program_id(1)
        cc = pl.program_id(2)
        lo = lax.broadcasted_iota(jnp.int32, (Q_BLOCK, LANES), 1) < HEAD
        lo_row = lax.broadcasted_iota(jnp.int32, (1, LANES), 1) < HEAD
        qv, dov, ov, lsev = q_ref[...].astype(F32), do_ref[...], o_ref[...].astype(F32), lse_ref[...]
        kc, kp, vc, vp = kc_ref[...], kp_ref[...], vc_ref[...], vp_ref[...]
        dqs, dsinks = [], []
        dkc = dkp = dvc = dvp = None
        for is_lo in (True, False):
            half = lo if is_lo else jnp.logical_not(lo)
            qh = jnp.where(half, qv, 0.0).astype(BF16)
            doh = jnp.where(half, dov, 0.0)
            s_c, s_p, sink = _band_scores(qh, kc, kp, n, s_ref[0:1, :], lo_row, is_lo)
            lse_h = jnp.max(jnp.where(half, lsev, -jnp.inf), axis=-1, keepdims=True)
            p_c, p_p = jnp.exp(s_c - lse_h), jnp.exp(s_p - lse_h)
            delta = jnp.sum(doh * ov, axis=-1, keepdims=True)
            dob = doh.astype(BF16)
            ds_c = (p_c * (lax.dot_general(dob, vc, nt, preferred_element_type=F32) - delta)).astype(BF16)
            ds_p = (p_p * (lax.dot_general(dob, vp, nt, preferred_element_type=F32) - delta)).astype(BF16)
            dsinks.append(-jnp.sum(jnp.exp(sink - lse_h) * delta, axis=0, keepdims=True))
            dqs.append((jnp.dot(ds_c, kc, preferred_element_type=F32) + jnp.dot(ds_p, kp, preferred_element_type=F32)) * scale)
            parts = (lax.dot_general(ds_c, qh, tn_, preferred_element_type=F32) * scale,
                     lax.dot_general(ds_p, qh, tn_, preferred_element_type=F32) * scale,
                     lax.dot_general(p_c.astype(BF16), dob, tn_, preferred_element_type=F32),
                     lax.dot_general(p_p.astype(BF16), dob, tn_, preferred_element_type=F32))
            if dkc is None:
                dkc, dkp, dvc, dvp = parts
            else:
                dkc, dkp, dvc, dvp = dkc + parts[0], dkp + parts[1], dvc + parts[2], dvp + parts[3]
        dq_ref[...] = jnp.where(lo, dqs[0], dqs[1])
        dsk_ref[...] = jnp.zeros((8, LANES), F32) + jnp.where(lo_row, dsinks[0], dsinks[1])

        @pl.when(cc == 0)
        def _():
            dkc_ref[...] = dkc
            dkp_ref[...] = dkp
            dvc_ref[...] = dvc
            dvp_ref[...] = dvp

        @pl.when(cc > 0)
        def _():
            dkc_ref[...] += dkc
            dkp_ref[...] += dkp
            dvc_ref[...] += dvc
            dvp_ref[...] += dvp

    qs = pl.BlockSpec((Q_BLOCK, LANES), lambda k, n, cc: (n, k * per_kv + cc))
    cur = pl.BlockSpec((Q_BLOCK, LANES), lambda k, n, cc: (n, k))
    prev = pl.BlockSpec((Q_BLOCK, LANES), lambda k, n, cc: (jnp.maximum(n - 1, 0), k))
    kv_shape = jax.ShapeDtypeStruct((t, nkv * LANES), F32)
    return _pcall(body, carry_us=ATTN_US_PER_STEP * nc * nb, grid=(nkv, nb, per_kv),
                  in_specs=[qs, qs, qs, pl.BlockSpec((None, Q_BLOCK, LANES), lambda k, n, cc: (k * per_kv + cc, n, 0)),
                            cur, prev, cur, prev, pl.BlockSpec((8, LANES), lambda k, n, cc: (k * per_kv + cc, 0))],
                  out_specs=[qs, cur, cur, cur, cur, pl.BlockSpec((None, 8, LANES), lambda k, n, cc: ((k * per_kv + cc) * nb + n, 0, 0))],
                  out_shape=[jax.ShapeDtypeStruct((t, dq), F32), kv_shape, kv_shape, kv_shape, kv_shape,
                             jax.ShapeDtypeStruct((nc * nb, 8, LANES), F32)],
                  compiler_params=_params(("parallel", "parallel", "arbitrary")), name=name)(do, q, o, lse, k2, k2, v2, v2, sink_tab)


def _peer(k):
    x, y, c = lax.axis_index("x"), lax.axis_index("y"), lax.axis_index("c")
    flip = lambda v, bit: 1 - v if bit else v
    return (flip(x, k & 4), flip(y, k & 2), flip(c, k & 1))


def _my_index():
    return 4 * lax.axis_index("x") + 2 * lax.axis_index("y") + lax.axis_index("c")


def _peer_index(k):
    px, py, pc = _peer(k)
    return 4 * px + 2 * py + pc


def _all_gather(shards, name):
    n = len(shards)
    any_spec = pl.BlockSpec(memory_space=pl.ANY)

    def body(*refs):
        ins, outs = refs[:n], refs[n:2 * n]
        send_sems, recv_sems, local_sems = refs[2 * n:]
        me = _my_index()
        local = [pltpu.make_async_copy(ins[a], outs[a].at[me], local_sems.at[a]) for a in range(n)]
        for cp in local:
            cp.start()
        sends = []
        for k in range(1, N_DEV):
            for a in range(n):
                cp = pltpu.make_async_remote_copy(src_ref=ins[a], dst_ref=outs[a].at[me], send_sem=send_sems.at[a, k - 1],
                                                  recv_sem=recv_sems.at[a, k - 1], device_id=_peer(k), device_id_type=MESH_ID)
                cp.start()
                sends.append(cp)
        for k in range(1, N_DEV):
            for a in range(n):
                pltpu.make_async_remote_copy(src_ref=ins[a], dst_ref=outs[a].at[_peer_index(k)], send_sem=send_sems.at[a, k - 1],
                                             recv_sem=recv_sems.at[a, k - 1], device_id=_peer(k), device_id_type=MESH_ID).wait_recv()
        for cp in sends:
            cp.wait_send()
        for cp in local:
            cp.wait()

    return _pcall(body, in_specs=[any_spec] * n, out_specs=[any_spec] * n,
                  out_shape=[jax.ShapeDtypeStruct((N_DEV,) + s.shape, s.dtype) for s in shards],
                  scratch_shapes=[pltpu.SemaphoreType.DMA((n, N_DEV - 1)), pltpu.SemaphoreType.DMA((n, N_DEV - 1)),
                                  pltpu.SemaphoreType.DMA((n,))],
                  name=name)(*shards)


GATHER1_PEERS = (1, 2, 4, 6)
GATHER2_PEERS = (2, 4, 6)
SCATTER_PEERS = tuple(range(1, N_DEV))
MAX_SEMS = N_DEV - 1
MAX_JOBS = 6
US_PER_MB = {"gather1": 5.4, "gather2": 0.6, "scatter": 10.8}
SCATTER_PIECE_US = 110.0


class _Job:
    def __init__(self, key, kind, src, lo=0, hi=None, dst=None):
        self.key, self.kind, self.src, self.dst = key, kind, src, dst
        shape = src.shape if kind != "gather1" else (N_DEV,) + src.shape
        self.out_shape = jax.ShapeDtypeStruct(shape, src.dtype)
        self.rows = shape[1]
        self.lo, self.hi = lo, self.rows if hi is None else hi
        self.row_us = US_PER_MB[kind] * math.prod(shape) * src.dtype.itemsize / 1e6 / self.rows
        pieces = max(1, round(self.row_us * self.rows / SCATTER_PIECE_US)) if kind == "scatter" else 1
        while pieces > 1 and self.rows % (16 * pieces):
            pieces -= 1
        self.piece = self.rows // pieces

    @property
    def cost_us(self):
        return self.row_us * (self.hi - self.lo)


class _Comm:
    def __init__(self):
        self.queue, self.gathered, self.scattered = [], {}, []

    def push(self, key, kind, src):
        self.queue.append(_Job(key, kind, src))

    def take(self, budget_us):
        jobs = [j for j in self.queue if j.kind == "gather2"][:MAX_JOBS]
        used = sum(j.cost_us for j in jobs)
        for j in [j for j in self.queue if j.kind != "gather2"]:
            if len(jobs) >= MAX_JOBS:
                break
            if j.kind == "gather1":
                if used >= budget_us:
                    break
                jobs.append(j)
                used += j.cost_us
                continue
            n = 0
            while j.lo + (n + 1) * j.piece <= j.hi and used + 0.5 * j.row_us * j.piece <= budget_us:
                n += 1
                used += j.row_us * j.piece
            if n == 0:
                break
            part = _Job(j.key, "scatter", j.src, j.lo, j.lo + n * j.piece, j.dst)
            part.parent = j
            j.lo = part.hi
            jobs.append(part)
            if j.lo < j.hi:
                break
        self.queue = [j for j in self.queue if j not in jobs and j.lo < j.hi]
        return jobs

    def finish(self, job, result):
        if job.kind == "gather1":
            self.queue.insert(0, _Job(job.key, "gather2", result))
        elif job.kind == "gather2":
            self.gathered[job.key] = result
        elif job.hi == job.rows:
            self.scattered.append((job.key, result))
        else:
            job.parent.dst = result

    def need(self, key):
        while key not in self.gathered:
            assert any(j.key == key for j in self.queue), key
            jobs = [j for j in self.queue if j.kind == "gather2"][:MAX_JOBS]
            if not any(j.key == key for j in jobs):
                for j in self.queue:
                    if j.kind == "gather1" and len(jobs) < MAX_JOBS:
                        jobs.append(j)
                        if j.key == key:
                            break
            self.flush(jobs)
        return self.gathered[key]

    def flush(self, jobs):
        self.queue = [j for j in self.queue if j not in jobs]

        def body(o_ref):
            o_ref[...] = jnp.zeros_like(o_ref)
        _carry(body, jobs, self, dict(in_specs=[], out_specs=pl.BlockSpec(memory_space=pltpu.VMEM),
                                      out_shape=jax.ShapeDtypeStruct((8, LANES), F32), name="exchange"))()


def _job_copies(job, src, dst, send_sems, recv_sems, local_sem):
    me = _my_index()
    peers = {"gather1": GATHER1_PEERS, "gather2": GATHER2_PEERS, "scatter": SCATTER_PEERS}[job.kind]
    sends, recvs = [], []
    for i, k in enumerate(peers):
        if job.kind == "gather1":
            s_ref, d_ref, to, got = src, dst.at[me], _peer(k), dst.at[_peer_index(k)]
        elif job.kind == "gather2":
            s_ref, d_ref, to, got = src.at[_peer_index(k)], dst.at[_peer_index(k)], _peer(1), dst.at[_peer_index(k | 1)]
        else:
            rows = pl.ds(job.lo, job.hi - job.lo)
            s_ref, d_ref, to, got = src.at[_peer_index(k), rows], dst.at[me, rows], _peer(k), dst.at[_peer_index(k), rows]
        sends.append(pltpu.make_async_remote_copy(src_ref=s_ref, dst_ref=d_ref, send_sem=send_sems.at[i], recv_sem=recv_sems.at[i],
                                                  device_id=to, device_id_type=MESH_ID))
        recvs.append(pltpu.make_async_remote_copy(src_ref=s_ref, dst_ref=got, send_sem=send_sems.at[i], recv_sem=recv_sems.at[i],
                                                  device_id=to, device_id_type=MESH_ID))
    local = None
    if job.kind == "gather1":
        local = pltpu.make_async_copy(src, dst.at[me], local_sem)
    elif job.kind == "scatter":
        rows = pl.ds(job.lo, job.hi - job.lo)
        local = pltpu.make_async_copy(src.at[me, rows], dst.at[me, rows], local_sem)
    return sends, recvs, local


def _carry(body, jobs, comm, kw):
    kw = dict(kw)
    grid = tuple(kw.get("grid", ()))
    in_specs = list(kw["in_specs"])
    single = not isinstance(kw["out_specs"], (list, tuple))
    out_specs = [kw["out_specs"]] if single else list(kw["out_specs"])
    out_shape = [kw["out_shape"]] if single else list(kw["out_shape"])
    scratch = list(kw.get("scratch_shapes", []))
    n_in, n_out, n_scr, nj = len(in_specs), len(out_specs), len(scratch), len(jobs)
    any_spec = pl.BlockSpec(memory_space=pl.ANY)
    landed = [a for a, job in enumerate(jobs) if job.dst is not None]
    n_land = len(landed)

    def wrapped(*refs):
        pos = 0

        def take(k):
            nonlocal pos
            part = refs[pos:pos + k]
            pos += k
            return part
        ins, rin, _, outs, rout, scr = take(n_in), take(nj), take(n_land), take(n_out), take(nj), take(n_scr)
        send_sems, recv_sems, local_sems = take(3)

        def copies():
            return [_job_copies(job, rin[a], rout[a], send_sems.at[a], recv_sems.at[a], local_sems.at[a]) for a, job in enumerate(jobs)]

        def start():
            for sends, _, local in copies():
                if local is not None:
                    local.start()
                for cp in sends:
                    cp.start()

        def finish():
            for sends, recvs, local in copies():
                for cp in recvs:
                    cp.wait_recv()
                for cp in sends:
                    cp.wait_send()
                if local is not None:
                    local.wait()

        if grid:
            first = functools.reduce(jnp.logical_and, [pl.program_id(a) == 0 for a in range(len(grid))])
            last = functools.reduce(jnp.logical_and, [pl.program_id(a) == grid[a] - 1 for a in range(len(grid))])
            pl.when(first)(start)
            body(*ins, *outs, *scr)
            pl.when(last)(finish)
        else:
            start()
            body(*ins, *outs, *scr)
            finish()

    aliases = {n_in + a: n_out + a for a, job in enumerate(jobs) if job.kind == "gather2"}
    aliases.update({n_in + nj + i: n_out + a for i, a in enumerate(landed)})
    extra = dict(dimension_semantics=("arbitrary",) * len(grid)) if grid else {}
    call = _raw_call(wrapped, in_specs=in_specs + [any_spec] * (nj + n_land), out_specs=out_specs + [any_spec] * nj,
                     out_shape=out_shape + [job.out_shape for job in jobs],
                     scratch_shapes=scratch + [pltpu.SemaphoreType.DMA((nj, MAX_SEMS)), pltpu.SemaphoreType.DMA((nj, MAX_SEMS)),
                                               pltpu.SemaphoreType.DMA((nj,))],
                     input_output_aliases=aliases, compiler_params=_params(**extra), name=kw["name"],
                     **({"grid": grid} if grid else {}))

    def run(*args):
        res = call(*args, *[job.src for job in jobs], *[jobs[a].dst for a in landed])
        for job, r in zip(jobs, res[n_out:]):
            comm.finish(job, r)
        return res[0] if single else list(res[:n_out])
    return run


def _adam(g, w, m, v):
    m2 = ADAM_B1 * m + (1.0 - ADAM_B1) * g
    v2 = ADAM_B2 * v + (1.0 - ADAM_B2) * (g * g)
    m_hat = m2 / (1.0 - ADAM_B1 ** ADAM_STEP)
    v_hat = v2 / (1.0 - ADAM_B2 ** ADAM_STEP)
    delta = -ADAM_LR * (m_hat / (jnp.sqrt(v_hat) + ADAM_EPS) + ADAM_WD * w)
    return delta, m2, v2


def _sum_adam(parts, w, m, v, name):
    r, c = w.shape
    tr = _pick(r, max(8, (1 << 19) // c), 8)

    def body(p_ref, w_ref, m_ref, v_ref, g_ref, d_ref, m2_ref, v2_ref):
        g = p_ref[0].astype(F32)
        for j in range(1, N_DEV):
            g = g + p_ref[j].astype(F32)
        delta, m2, v2 = _adam(g, w_ref[...], m_ref[...], v_ref[...])
        g_ref[...] = g
        d_ref[...] = delta
        m2_ref[...] = m2
        v2_ref[...] = v2

    blk = pl.BlockSpec((tr, c), lambda i: (i, 0))
    shp = jax.ShapeDtypeStruct((r, c), F32)
    return _pcall(body, carry_us=44.0 * r * c / HBM_BYTES_PER_US, grid=(r // tr,),
                  in_specs=[pl.BlockSpec((N_DEV, tr, c), lambda i: (0, i, 0)), blk, blk, blk],
                  out_specs=[blk] * 4, out_shape=[shp] * 4, compiler_params=_params(("parallel",)), name=name)(parts, w, m, v)


def _small_step(rep_parts, rep_w, rep_m, rep_v, sh_parts, sh_w, sh_m, sh_v, name):
    nr, ns = len(rep_parts), len(sh_parts)
    rows_r = [-(-p.shape[1] // LANES) for p in rep_parts]
    off_r = [sum(rows_r[:i]) for i in range(nr)]
    tot_r = -(-max(sum(rows_r), 8) // 8) * 8
    rows_s = [-(-p.shape[1] // 8) * 8 for p in sh_parts]
    cmax = max([p.shape[2] for p in sh_parts] + [LANES])
    off_s = [sum(rows_s[:i]) for i in range(ns)]
    tot_s = max(sum(rows_s), 8)
    vm = pl.BlockSpec(memory_space=pltpu.VMEM)

    def body(*refs):
        pos = 0

        def take(k):
            nonlocal pos
            out = refs[pos:pos + k]
            pos += k
            return out
        rp, rw, rm, rv = take(nr), take(nr), take(nr), take(nr)
        sp, sw, sm, sv = take(ns), take(ns), take(ns), take(ns)
        rg, rd, rm2, rv2 = take(nr), take(nr), take(nr), take(nr)
        sg, sd, sm2, sv2 = take(ns), take(ns), take(ns), take(ns)
        pack_r, got_r, pack_s, got_s, send_r, recv_r, send_s, recv_s = take(8)
        me = _my_index()
        pack_r[...] = jnp.zeros_like(pack_r)
        pack_s[...] = jnp.zeros_like(pack_s)
        for i in range(nr):
            nfull = rep_parts[i].shape[1]
            for rr in range(rows_r[i]):
                wdt = min(LANES, nfull - rr * LANES)
                pack_r[off_r[i] + rr:off_r[i] + rr + 1, 0:wdt] = rp[i][0:1, rr * LANES:rr * LANES + wdt]
        for i in range(ns):
            _, r_i, c_i = sh_parts[i].shape
            for j in range(N_DEV):
                pack_s[j, off_s[i]:off_s[i] + r_i, 0:c_i] = sp[i][j]
        got_r[me] = pack_r[...]
        got_s[me] = pack_s[me]
        sends = []
        for k in range(1, N_DEV):
            a = pltpu.make_async_remote_copy(src_ref=pack_r, dst_ref=got_r.at[me], send_sem=send_r.at[k - 1], recv_sem=recv_r.at[k - 1],
                                             device_id=_peer(k), device_id_type=MESH_ID)
            b = pltpu.make_async_remote_copy(src_ref=pack_s.at[_peer_index(k)], dst_ref=got_s.at[me], send_sem=send_s.at[k - 1],
                                             recv_sem=recv_s.at[k - 1], device_id=_peer(k), device_id_type=MESH_ID)
            a.start()
            b.start()
            sends += [a, b]
        for k in range(1, N_DEV):
            pltpu.make_async_remote_copy(src_ref=pack_r, dst_ref=got_r.at[_peer_index(k)], send_sem=send_r.at[k - 1],
                                         recv_sem=recv_r.at[k - 1], device_id=_peer(k), device_id_type=MESH_ID).wait_recv()
            pltpu.make_async_remote_copy(src_ref=pack_s.at[me], dst_ref=got_s.at[_peer_index(k)], send_sem=send_s.at[k - 1],
                                         recv_sem=recv_s.at[k - 1], device_id=_peer(k), device_id_type=MESH_ID).wait_recv()
        for cp in sends:
            cp.wait_send()
        tot_rep = got_r[0]
        tot_sh = got_s[0]
        for j in range(1, N_DEV):
            tot_rep = tot_rep + got_r[j]
            tot_sh = tot_sh + got_s[j]
        pack_r[...] = tot_rep
        pack_s[0] = tot_sh
        for i in range(nr):
            nfull = rep_parts[i].shape[1]
            for rr in range(rows_r[i]):
                wdt = min(LANES, nfull - rr * LANES)
                rg[i][0:1, rr * LANES:rr * LANES + wdt] = pack_r[off_r[i] + rr:off_r[i] + rr + 1, 0:wdt]
            g = rg[i][...]
            delta, m2, v2 = _adam(g, rw[i][...], rm[i][...], rv[i][...])
            rd[i][...] = delta
            rm2[i][...] = m2
            rv2[i][...] = v2
        for i in range(ns):
            _, r_i, c_i = sh_parts[i].shape
            g = pack_s[0, off_s[i]:off_s[i] + r_i, 0:c_i]
            delta, m2, v2 = _adam(g, sw[i][...], sm[i][...], sv[i][...])
            sg[i][...] = g
            sd[i][...] = delta
            sm2[i][...] = m2
            sv2[i][...] = v2

    rep_shapes = [jax.ShapeDtypeStruct(p.shape, F32) for p in rep_parts]
    sh_shapes = [jax.ShapeDtypeStruct(p.shape[1:], F32) for p in sh_parts]
    n_in = 4 * nr + 4 * ns
    outs = _pcall(body, in_specs=[vm] * n_in, out_specs=[vm] * n_in, out_shape=rep_shapes * 4 + sh_shapes * 4,
                  scratch_shapes=[pltpu.VMEM((tot_r, LANES), F32), pltpu.VMEM((N_DEV, tot_r, LANES), F32),
                                  pltpu.VMEM((N_DEV, tot_s, cmax), F32), pltpu.VMEM((N_DEV, tot_s, cmax), F32),
                                  pltpu.SemaphoreType.DMA((N_DEV - 1,)), pltpu.SemaphoreType.DMA((N_DEV - 1,)),
                                  pltpu.SemaphoreType.DMA((N_DEV - 1,)), pltpu.SemaphoreType.DMA((N_DEV - 1,))],
                  compiler_params=_params(), name=name)(
        *rep_parts, *rep_w, *rep_m, *rep_v, *sh_parts, *sh_w, *sh_m, *sh_v)
    rep_out = [outs[i * nr:(i + 1) * nr] for i in range(4)]
    sh_out = [outs[4 * nr + i * ns:4 * nr + (i + 1) * ns] for i in range(4)]
    return rep_out, sh_out


_CONF = ("norm_g", "a_w_in", "a_b_in", "a_dw_w", "a_dw_b", "a_ln_g", "a_ln_b", "a_w_out", "a_b_out")
_FFN = ("ffn_norm_g", "ffn_w_up", "ffn_dw_w", "ffn_dw_b", "ffn_w_down")
_POOL = ("norm_g", "b_w_group", "b_scale")
_ATTN = ("norm_g", "c_w_qkv", "c_q_norm_g", "c_k_norm_g", "c_sinks", "c_w_o")
_LAYERS = (_CONF + _FFN, _POOL + _FFN, _ATTN + _FFN, _CONF + _FFN)
_NAMES = tuple("l%d_%s" % (i, n) for i, names in enumerate(_LAYERS) for n in names)
_BIG = ("a_w_in", "a_w_out", "ffn_w_up", "ffn_w_down", "b_w_group", "c_w_qkv", "c_w_o")
_SHARDED_SMALL = ("a_dw_w", "ffn_dw_w")


def _pad_rows(a, mult=8):
    r = a.shape[0]
    rp = -(-r // mult) * mult
    return a if rp == r else jnp.pad(a, ((0, rp - r), (0, 0)))


def _unstack_cols(st, rows):
    s, r, cs = st.shape
    return jnp.transpose(st, (1, 0, 2)).reshape(r, s * cs)[:rows]


def _stack_cols(a):
    r, c = a.shape
    return jnp.transpose(a.reshape(r, N_DEV, c // N_DEV), (1, 0, 2))


def _row(v):
    return v.reshape(1, -1)


def _ffn_forward(x_mid, p, tag):
    h2 = _rms_fwd(x_mid, _row(p["ffn_norm_g"]), BF16, "rms_fwd_bf16")
    u0 = _mm(h2, p["ffn_w_up"], "nn", name="ffn_up", tn=1408, tk=2048)
    a = _ffn_act_fwd(u0, p["ffn_dw_w"], _row(p["ffn_dw_b"]), "ffn_act_fwd")
    x_out = _mm(a, p["ffn_w_down"], "nn", res=x_mid, name="ffn_down", tk=1408)
    return x_out, dict(h2=h2, u0=u0, a=a)


def _ffn_backward(dx_out, x_mid, p, sv, grads):
    dwd = _mm(sv["a"], dx_out, "tn", out_dtype=BF16, name="ffn_down_dw", tm=1408)
    grads["ffn_w_down"] = dwd.reshape(N_DEV, dwd.shape[0] // N_DEV, dwd.shape[1])
    da = _mm(dx_out, p["ffn_w_down"], "nt", name="ffn_down_dx", tn=1408, tk=2048)
    du0, dww, dwb = _ffn_act_bwd(da, sv["u0"], p["ffn_dw_w"], _row(p["ffn_dw_b"]), "ffn_act_bwd")
    kw = dww.shape[1]
    grads["ffn_dw_w"] = _stack_cols(jnp.transpose(dww, (1, 0, 2)).reshape(kw, -1))
    grads["ffn_dw_b"] = dwb.reshape(1, -1)
    grads["ffn_w_up"] = _mm(sv["h2"], du0, "tn", out_dtype=BF16, out_stack=N_DEV, name="ffn_up_dw", tn=1408)
    dh2 = _mm(du0, p["ffn_w_up"], "nt", name="ffn_up_dx", tk=1408)
    dx_mid, dg, _ = _rms_bwd(dh2, x_mid, _row(p["ffn_norm_g"]), dx_out, "rms_bwd")
    grads["ffn_norm_g"] = dg
    return dx_mid


def _conf_forward(x, p):
    h = _rms_fwd(x, _row(p["norm_g"]), BF16, "rms_fwd_bf16")
    u = _mm(h, p["a_w_in"], "nn", bias=_row(p["a_b_in"]), name="conf_in", tn=512, tk=2048)
    cpre = _conf_conv_fwd(u, p["a_dw_w"], _row(p["a_dw_b"]), "conf_conv_fwd")
    s = _ln_silu_fwd(cpre, _row(p["a_ln_g"]), _row(p["a_ln_b"]), "ln_silu_fwd")
    x_mid = _mm(s, p["a_w_out"], "nn", bias=_row(p["a_b_out"]), res=x, name="conf_out", tk=2048)
    return x_mid, dict(h=h, u=u, cpre=cpre, s=s)


def _conf_backward(dx_mid, x, p, sv, grads):
    dwo = _mm(sv["s"], dx_mid, "tn", out_dtype=BF16, name="conf_out_dw")
    grads["a_w_out"] = dwo.reshape(N_DEV, dwo.shape[0] // N_DEV, dwo.shape[1])
    ds = _mm(dx_mid, p["a_w_out"], "nt", name="conf_out_dx", tk=2048)
    dc, dlg, dlb = _ln_silu_bwd(ds, sv["cpre"], _row(p["a_ln_g"]), _row(p["a_ln_b"]), "ln_silu_bwd")
    grads["a_ln_g"], grads["a_ln_b"] = dlg, dlb
    du, dww, dwb, dbin = _conf_conv_bwd(dc, sv["u"], p["a_dw_w"], "conf_conv_bwd")
    grads["a_dw_w"] = _stack_cols(dww)
    grads["a_dw_b"] = dwb
    grads["a_b_in"] = dbin.reshape(1, -1)
    grads["a_w_in"] = _mm(sv["h"], du, "tn", out_dtype=BF16, out_stack=N_DEV, name="conf_in_dw", tn=512)
    dh = _mm(du, p["a_w_in"], "nt", name="conf_in_dx", tk=512)
    dx, dg, dbo = _rms_bwd(dh, x, _row(p["norm_g"]), dx_mid, "rms_bwd")
    grads["norm_g"] = dg
    grads["a_b_out"] = dbo
    return dx


def _pool_forward(x, p):
    h = _rms_fwd(x, _row(p["norm_g"]), F32, "rms_fwd_f32")
    mixed = _pool_fwd(h, "pool_fwd")
    x_mid = _pool_mm_fwd(mixed, p["b_w_group"], _row(p["b_scale"]), x, "pool_mm_fwd")
    return x_mid, dict(mixed=mixed)


def _pool_backward(dx_mid, x, p, sv, grads):
    dmixed, dwg, dscale = _pool_mm_bwd(dx_mid, sv["mixed"], p["b_w_group"], _row(p["b_scale"]), "pool_mm_bwd")
    ng, gd, _ = dwg.shape
    grads["b_w_group"] = jnp.transpose(dwg.reshape(ng, N_DEV, gd // N_DEV, gd), (1, 0, 2, 3)).reshape(N_DEV, ng * gd // N_DEV, gd).astype(BF16)
    grads["b_scale"] = dscale
    dh = _pool_bwd(dmixed, "pool_bwd")
    dx, dg, _ = _rms_bwd(dh, x, _row(p["norm_g"]), dx_mid, "rms_bwd")
    grads["norm_g"] = dg
    return dx


def _attn_tables(p, positions, d_model):
    n_q = d_model // HEAD
    n_kv = n_q // 8
    tabs = _rope_tables(positions)
    gq2 = jnp.concatenate([p["c_q_norm_g"], p["c_q_norm_g"]]).reshape(1, LANES)
    gk2 = jnp.concatenate([p["c_k_norm_g"], p["c_k_norm_g"]]).reshape(1, LANES)
    sink_tab = jnp.repeat(jnp.repeat(p["c_sinks"].reshape(-1, 2), HEAD, axis=1), 8, axis=0)
    return n_q, n_kv, tabs, gq2, gk2, sink_tab


def _attn_forward(x, p, positions):
    n_q, n_kv, tabs, gq2, gk2, sink_tab = _attn_tables(p, positions, x.shape[1])
    h = _rms_fwd(x, _row(p["norm_g"]), BF16, "rms_fwd_bf16")
    qkv = _mm(h, p["c_w_qkv"], "nn", name="attn_qkv", tn=1280, tk=2048)
    q, k2, v2 = _qk_prep_fwd(qkv, tabs, gq2, gk2, n_q, n_kv, "qk_prep_fwd")
    o, lse = _attn_fwd(q, k2, v2, sink_tab, "attn_fwd")
    x_mid = _mm(o, p["c_w_o"], "nn", res=x, name="attn_out", tk=2048)
    return x_mid, dict(h=h, qkv=qkv, q=q, k2=k2, v2=v2, o=o, lse=lse)


def _attn_backward(dx_mid, x, p, positions, sv, grads):
    n_q, n_kv, tabs, gq2, gk2, sink_tab = _attn_tables(p, positions, x.shape[1])
    dwo = _mm(sv["o"], dx_mid, "tn", out_dtype=BF16, name="attn_out_dw")
    grads["c_w_o"] = dwo.reshape(N_DEV, dwo.shape[0] // N_DEV, dwo.shape[1])
    do = _mm(dx_mid, p["c_w_o"], "nt", name="attn_out_dx", tk=2048)
    dq, dkc, dkp, dvc, dvp, dsk = _attn_bwd(do, sv["q"], sv["o"], sv["lse"], sv["k2"], sv["v2"], sink_tab, "attn_bwd")
    nb = x.shape[0] // Q_BLOCK
    dsk = dsk.reshape(-1, nb, 8, LANES)[:, :, 0, :].sum(axis=1)
    grads["c_sinks"] = jnp.stack([dsk[:, 0], dsk[:, HEAD]], axis=1).reshape(1, -1)
    dqkv, dgq, dgk = _qk_prep_bwd(dq, dkc, dkp, dvc, dvp, sv["qkv"], tabs, gq2, gk2, n_q, n_kv, "qk_prep_bwd")
    grads["c_q_norm_g"] = dgq[:, :HEAD] + dgq[:, HEAD:]
    grads["c_k_norm_g"] = dgk[:, :HEAD] + dgk[:, HEAD:]
    dwq = _mm(sv["h"], dqkv, "tn", out_dtype=BF16, name="attn_qkv_dw", tn=1280)
    grads["c_w_qkv"] = _stack_cols(dwq)
    dh = _mm(dqkv, p["c_w_qkv"], "nt", name="attn_qkv_dx", tk=1280)
    dx, dg, _ = _rms_bwd(dh, x, _row(p["norm_g"]), dx_mid, "rms_bwd")
    grads["norm_g"] = dg
    return dx


class _LayerWeights:
    def __init__(self, li, weights, small_full, comm):
        self.li, self.weights, self.small_full, self.comm, self.cache = li, weights, small_full, comm, {}

    def __getitem__(self, nme):
        if nme not in self.cache:
            self.cache[nme] = self.fetch(nme)
        return self.cache[nme]

    def fetch(self, nme):
        full = "l%d_%s" % (self.li, nme)
        w = self.weights[full]
        if nme in _SHARDED_SMALL:
            return _unstack_cols(self.small_full[full], w.shape[0])
        if nme not in _BIG:
            return w
        got = self.comm.need(full)
        if nme in ("a_w_in", "ffn_w_up"):
            return got
        if nme == "c_w_qkv":
            return _unstack_cols(got, w.shape[0])
        if nme == "b_w_group":
            ng, gs, gd = w.shape
            return jnp.transpose(got.reshape(N_DEV, ng, gs, gd), (1, 0, 2, 3)).reshape(ng, N_DEV * gs, gd)
        return got.reshape(-1, w.shape[1])


class _LayerGrads(dict):
    def __init__(self, li, comm):
        super().__init__()
        self.li, self.comm = li, comm

    def __setitem__(self, nme, value):
        if nme in _BIG:
            self.comm.push("l%d_%s" % (self.li, nme), "scatter", value)
        else:
            super().__setitem__(nme, value)


def kernel(*args):
    n_w = len(_NAMES)
    x, positions = args[0], args[1]
    weights = dict(zip(_NAMES, args[2:2 + n_w]))
    loss_target = args[2 + n_w]
    moms = dict(zip(_NAMES, args[3 + n_w:3 + 2 * n_w]))
    vels = dict(zip(_NAMES, args[3 + 2 * n_w:3 + 3 * n_w]))
    x0 = x[0]
    pos = positions[0]
    kinds = ("conf", "pool", "attn", "conf")
    comm = _Comm()
    _STATE["comm"], _STATE["last"] = comm, None
    shd = [n for n in _NAMES if n.split("_", 1)[1] in _SHARDED_SMALL]
    small_full = dict(zip(shd, _all_gather([_pad_rows(weights[n]) for n in shd], "gather_small")))
    for n in _NAMES:
        if n.split("_", 1)[1] in _BIG:
            w = weights[n]
            comm.push(n, "gather1", w.astype(BF16).reshape(-1, w.shape[-1]))
    results = {}

    def update_ready():
        while comm.scattered:
            full, parts = comm.scattered.pop(0)
            w = weights[full]
            w2 = w.reshape(-1, w.shape[-1])
            outs = _sum_adam(parts, w2, moms[full].reshape(w2.shape), vels[full].reshape(w2.shape), "adam_" + full.split("_", 1)[1])
            results[full] = tuple(o.reshape(w.shape) for o in outs)

    params, saved = [], []
    cur = x0
    for li, names in enumerate(_LAYERS):
        p = _LayerWeights(li, weights, small_full, comm)
        if kinds[li] == "conf":
            x_mid, sv = _conf_forward(cur, p)
        elif kinds[li] == "pool":
            x_mid, sv = _pool_forward(cur, p)
        else:
            x_mid, sv = _attn_forward(cur, p, pos)
        x_out, sv_f = _ffn_forward(x_mid, p, kinds[li])
        params.append(p)
        saved.append((sv, sv_f, cur, x_mid))
        cur = x_out
    dy, loss_part = _loss_head(cur, loss_target[0], "loss_head")
    loss = lax.psum(loss_part[0, 0], ("x", "y", "c"))

    small_grads = {}
    dcur = dy
    for li in range(len(_LAYERS) - 1, -1, -1):
        p = params[li]
        sv, sv_f, x_in, x_mid = saved[li]
        grads = _LayerGrads(li, comm)
        dmid = _ffn_backward(dcur, x_mid, p, sv_f, grads)
        update_ready()
        if kinds[li] == "conf":
            dcur = _conf_backward(dmid, x_in, p, sv, grads)
        elif kinds[li] == "pool":
            dcur = _pool_backward(dmid, x_in, p, sv, grads)
        else:
            dcur = _attn_backward(dmid, x_in, p, pos, sv, grads)
        update_ready()
        for n in _LAYERS[li]:
            if n not in _BIG:
                small_grads["l%d_%s" % (li, n)] = grads[n]
    while comm.queue or comm.scattered:
        if not comm.scattered:
            comm.flush(comm.take(1e9))
        update_ready()
    _STATE["comm"] = None

    rep = [n for n in _NAMES if n.split("_", 1)[1] not in _BIG and n.split("_", 1)[1] not in _SHARDED_SMALL]
    rep_out, sh_out = _small_step(
        [small_grads[n] for n in rep], [_row(weights[n]) for n in rep], [_row(moms[n]) for n in rep], [_row(vels[n]) for n in rep],
        [jnp.stack([_pad_rows(small_grads[n][j]) for j in range(N_DEV)]) for n in shd],
        [_pad_rows(weights[n]) for n in shd], [_pad_rows(moms[n]) for n in shd], [_pad_rows(vels[n]) for n in shd], "small_step")
    for i, n in enumerate(rep):
        results[n] = tuple(rep_out[k][i].reshape(weights[n].shape) for k in range(4))
    for i, n in enumerate(shd):
        results[n] = tuple(sh_out[k][i][:weights[n].shape[0]] for k in range(4))

    _STATE["last"] = None
    grad_x = dcur[None]
    out = [loss, grad_x]
    for k in range(4):
        out += [results[n][k] for n in _NAMES]
    return tuple(out)
```

```python
import functools
import math

import jax
import jax.numpy as jnp
from jax import lax
from jax.experimental import pallas as pl
from jax.experimental.pallas import tpu as pltpu

F32 = jnp.float32
BF16 = jnp.bfloat16
N_DEV = 8
EPS = 1e-6
LANES = 128
HEAD = 64
Q_BLOCK = 128
ROT_DIM = 16
ROPE_THETA = 500000.0
POOL_WINDOWS = (2, 4, 8, 16)
HALO = 32
ROWS = 256
VMEM_LIMIT = 56 * 1024 * 1024
ADAM_LR, ADAM_B1, ADAM_B2, ADAM_EPS, ADAM_WD, ADAM_STEP = 0.001, 0.9, 0.999, 1e-08, 0.01, 10
MESH_ID = pl.DeviceIdType.MESH
MXU_FLOPS_PER_US = 7.5e8
HBM_BYTES_PER_US = 2.5e6
ATTN_US_PER_STEP = 0.85
CONV_FWD_US_PER_ELEM = 1.29e-5
ACT_FWD_US_PER_ELEM = 4.9e-6
SMALL_EXCHANGE_CARRY_US = 100.0


def _make_call(body, **kw):
    return pl.pallas_call(body, **kw)


_STATE = {"comm": None, "last": None}


def _raw_call(body, **kw):
    call = _make_call(body, **kw)

    def run(*args):
        last = _STATE["last"]
        if last is not None and args:
            first, _ = lax.optimization_barrier((args[0], last))
            args = (first,) + tuple(args[1:])
        res = call(*args)
        _STATE["last"] = res[0] if isinstance(res, (list, tuple)) else res
        return res
    return run


def _pcall(body, carry_us=0.0, **kw):
    comm = _STATE["comm"]
    jobs = comm.take(carry_us) if (comm is not None and carry_us > 0) else []
    if not jobs:
        return _raw_call(body, **kw)
    return _carry(body, jobs, comm, kw)


def _params(sem=None, **kw):
    if sem is not None:
        kw["dimension_semantics"] = sem
    return pltpu.CompilerParams(vmem_limit_bytes=VMEM_LIMIT, **kw)


def _pick(dim, pref, mult=LANES):
    best = None
    d = mult
    while d <= min(dim, pref):
        if dim % d == 0:
            best = d
        d += mult
    return dim if best is None else best


def _sigmoid(x):
    return 1.0 / (1.0 + jnp.exp(-x))


def _fold8(p):
    r, c = p.shape
    return p.reshape(r // 8, 8, c).sum(axis=0)


def _rows(e, k, r):
    n = e.shape[0]
    if k % 8 == 0:
        return e[k:k + r]
    return pltpu.roll(e, n - k, 0)[0:r]


def _lshape(a):
    return a.shape if a.ndim == 2 else (a.shape[1], a.shape[0] * a.shape[2])


def _panel(a):
    return a.shape[1] if a.ndim == 2 else a.shape[2]


def _lspec(a, br, bc, rc):
    if a.ndim == 2:
        return pl.BlockSpec((br, bc), rc)
    per = a.shape[2] // bc

    def idx(i, j, k):
        r, c = rc(i, j, k)
        return (c // per, r, c % per)
    return pl.BlockSpec((None, br, bc), idx)


def _mm(a, b, dims, *, name, out_dtype=F32, out_stack=None, bias=None, res=None, tm=1024, tn=1024, tk=1024):
    (ar, ac), (br_, bc_) = _lshape(a), _lshape(b)
    if dims == "nn":
        m, k, n = ar, ac, bc_
        lim_m, lim_k, lim_n = m, min(_panel(a), k), _panel(b)
    elif dims == "nt":
        m, k, n = ar, ac, br_
        lim_m, lim_k, lim_n = m, math.gcd(_panel(a), _panel(b)), n
    else:
        m, k, n = ac, ar, bc_
        lim_m, lim_k, lim_n = _panel(a), k, _panel(b)
    if out_stack is not None:
        lim_n = math.gcd(lim_n, n // out_stack)
    sub = 16 if (out_dtype == BF16 or a.dtype == BF16) else 8
    tm = _pick(lim_m, tm, LANES if dims == "tn" else sub)
    tn = _pick(lim_n, tn)
    tk = _pick(lim_k, tk, sub if dims == "tn" else LANES)
    nk = k // tk
    if dims == "tn":
        a_spec = _lspec(a, tk, tm, lambda i, j, kk: (kk, i))
    else:
        a_spec = _lspec(a, tm, tk, lambda i, j, kk: (i, kk))
    if dims == "nt":
        b_spec = _lspec(b, tn, tk, lambda i, j, kk: (j, kk))
    else:
        b_spec = _lspec(b, tk, tn, lambda i, j, kk: (kk, j))
    contract = {"nn": ((1,), (0,)), "nt": ((1,), (1,)), "tn": ((0,), (0,))}[dims]
    in_specs, args = [a_spec, b_spec], [a, b]
    if bias is not None:
        in_specs.append(pl.BlockSpec((1, tn), lambda i, j, kk: (0, j)))
        args.append(bias)
    if res is not None:
        in_specs.append(pl.BlockSpec((tm, tn), lambda i, j, kk: (i, j)))
        args.append(res)
    if out_stack is None:
        out_shape = jax.ShapeDtypeStruct((m, n), out_dtype)
    else:
        out_shape = jax.ShapeDtypeStruct((out_stack, m, n // out_stack), out_dtype)
    o_spec = _lspec(out_shape, tm, tn, lambda i, j, kk: (i, j))
    has_bias, has_res = bias is not None, res is not None

    def body(*refs):
        a_ref, b_ref = refs[0], refs[1]
        pos = 2
        bias_ref = res_ref = None
        if has_bias:
            bias_ref = refs[pos]
            pos += 1
        if has_res:
            res_ref = refs[pos]
            pos += 1
        o_ref = refs[pos]

        def part():
            return lax.dot_general(a_ref[...].astype(BF16), b_ref[...].astype(BF16), (contract, ((), ())),
                                   preferred_element_type=F32)

        def finish(r):
            if has_bias:
                r = r + bias_ref[...]
            if has_res:
                r = r + res_ref[...]
            o_ref[...] = r.astype(out_dtype)

        if nk == 1:
            finish(part())
        else:
            acc = refs[pos + 1]
            kk = pl.program_id(2)

            @pl.when(kk == 0)
            def _():
                acc[...] = part()

            @pl.when(kk > 0)
            def _():
                acc[...] += part()

            @pl.when(kk == nk - 1)
            def _():
                finish(acc[...])

    scratch = [] if nk == 1 else [pltpu.VMEM((tm, tn), F32)]
    return _pcall(body, carry_us=2.0 * m * n * k / MXU_FLOPS_PER_US, grid=(m // tm, n // tn, nk), in_specs=in_specs, out_specs=o_spec, out_shape=out_shape,
                  scratch_shapes=scratch, compiler_params=_params(("parallel", "parallel", "arbitrary")), name=name)(*args)


def _rms_fwd(x, g, out_dtype, name):
    t, d = x.shape
    tm = _pick(t, 512, 16)

    def body(x_ref, g_ref, o_ref):
        xv = x_ref[...]
        r = lax.rsqrt(jnp.mean(xv * xv, axis=-1, keepdims=True) + EPS)
        o_ref[...] = ((xv * r) * g_ref[...]).astype(out_dtype)

    return _pcall(body, grid=(t // tm,), in_specs=[pl.BlockSpec((tm, d), lambda i: (i, 0)), pl.BlockSpec((1, d), lambda i: (0, 0))],
                  out_specs=pl.BlockSpec((tm, d), lambda i: (i, 0)), out_shape=jax.ShapeDtypeStruct((t, d), out_dtype),
                  compiler_params=_params(("parallel",)), name=name)(x, g)


def _rms_bwd(dh, x, g, dres, name):
    t, d = x.shape
    tm = _pick(t, 256, 8)

    def body(dh_ref, x_ref, g_ref, dres_ref, dx_ref, dg_ref, cs_ref):
        xv, dhv, dr = x_ref[...], dh_ref[...], dres_ref[...]
        r = lax.rsqrt(jnp.mean(xv * xv, axis=-1, keepdims=True) + EPS)
        xh = xv * r
        dxh = dhv * g_ref[...]
        dx_ref[...] = dr + r * (dxh - xh * jnp.mean(dxh * xh, axis=-1, keepdims=True))
        pg = jnp.sum(dhv * xh, axis=0, keepdims=True)
        pc = jnp.sum(dr, axis=0, keepdims=True)

        @pl.when(pl.program_id(0) == 0)
        def _():
            dg_ref[...] = pg
            cs_ref[...] = pc

        @pl.when(pl.program_id(0) > 0)
        def _():
            dg_ref[...] += pg
            cs_ref[...] += pc

    row = pl.BlockSpec((tm, d), lambda i: (i, 0))
    vec = pl.BlockSpec((1, d), lambda i: (0, 0))
    return _pcall(body, carry_us=16.0 * t * d / HBM_BYTES_PER_US, grid=(t // tm,), in_specs=[row, row, vec, row], out_specs=[row, vec, vec],
                  out_shape=[jax.ShapeDtypeStruct((t, d), F32), jax.ShapeDtypeStruct((1, d), F32), jax.ShapeDtypeStruct((1, d), F32)],
                  compiler_params=_params(("arbitrary",)), name=name)(dh, x, g, dres)


def _loss_head(y, target, name):
    t, d = y.shape
    tm = _pick(t, 512, 8)

    def body(y_ref, t_ref, dy_ref, l_ref):
        e = y_ref[...] - t_ref[...]
        dy_ref[...] = e * (1.0 / d)
        part = 0.5 * jnp.sum(jnp.mean(e * e, axis=-1, keepdims=True), axis=0, keepdims=True)

        @pl.when(pl.program_id(0) == 0)
        def _():
            l_ref[...] = part

        @pl.when(pl.program_id(0) > 0)
        def _():
            l_ref[...] += part

    row = pl.BlockSpec((tm, d), lambda i: (i, 0))
    return _pcall(body, grid=(t // tm,), in_specs=[row, row], out_specs=[row, pl.BlockSpec((1, 1), lambda i: (0, 0))],
                  out_shape=[jax.ShapeDtypeStruct((t, d), F32), jax.ShapeDtypeStruct((1, 1), F32)],
                  compiler_params=_params(("arbitrary",)), name=name)(y, target)


def _ln_silu_fwd(c, g, b, name):
    t, d = c.shape
    tm = _pick(t, 512, 16)

    def body(c_ref, g_ref, b_ref, o_ref):
        cv = c_ref[...]
        xc = cv - jnp.mean(cv, axis=-1, keepdims=True)
        z = xc * lax.rsqrt(jnp.mean(xc * xc, axis=-1, keepdims=True) + EPS) * g_ref[...] + b_ref[...]
        o_ref[...] = (z * _sigmoid(z)).astype(BF16)

    row = pl.BlockSpec((tm, d), lambda i: (i, 0))
    vec = pl.BlockSpec((1, d), lambda i: (0, 0))
    return _pcall(body, grid=(t // tm,), in_specs=[row, vec, vec], out_specs=row, out_shape=jax.ShapeDtypeStruct((t, d), BF16),
                  compiler_params=_params(("parallel",)), name=name)(c, g, b)


def _ln_silu_bwd(ds, c, g, b, name):
    t, d = c.shape
    tm = _pick(t, 256, 8)

    def body(ds_ref, c_ref, g_ref, b_ref, dc_ref, dg_ref, db_ref):
        cv = c_ref[...]
        xc = cv - jnp.mean(cv, axis=-1, keepdims=True)
        r = lax.rsqrt(jnp.mean(xc * xc, axis=-1, keepdims=True) + EPS)
        ch = xc * r
        z = ch * g_ref[...] + b_ref[...]
        sg = _sigmoid(z)
        dz = ds_ref[...] * (sg * (1.0 + z * (1.0 - sg)))
        dch = dz * g_ref[...]
        dc_ref[...] = r * (dch - jnp.mean(dch, axis=-1, keepdims=True) - ch * jnp.mean(dch * ch, axis=-1, keepdims=True))
        pg = jnp.sum(dz * ch, axis=0, keepdims=True)
        pb = jnp.sum(dz, axis=0, keepdims=True)

        @pl.when(pl.program_id(0) == 0)
        def _():
            dg_ref[...] = pg
            db_ref[...] = pb

        @pl.when(pl.program_id(0) > 0)
        def _():
            dg_ref[...] += pg
            db_ref[...] += pb

    row = pl.BlockSpec((tm, d), lambda i: (i, 0))
    vec = pl.BlockSpec((1, d), lambda i: (0, 0))
    return _pcall(body, grid=(t // tm,), in_specs=[row, row, vec, vec], out_specs=[row, vec, vec],
                  out_shape=[jax.ShapeDtypeStruct((t, d), F32), jax.ShapeDtypeStruct((1, d), F32), jax.ShapeDtypeStruct((1, d), F32)],
                  compiler_params=_params(("arbitrary",)), name=name)(ds, c, g, b)


def _steps(t):
    return t // ROWS


def _conf_conv_fwd(u, dw_w, dw_b, name):
    t, d2 = u.shape
    d = d2 // 2
    c = LANES
    ns = d // c
    kc = dw_w.shape[0]

    def body(a_ref, g_ref, w_ref, b_ref, o_ref, pad):
        pad[0:HALO, :] = jnp.zeros((HALO, c), F32)

        def glu(i, _):
            base = pl.multiple_of(i * ROWS, ROWS)
            pad[pl.ds(base + HALO, ROWS), :] = a_ref[pl.ds(base, ROWS), :] * _sigmoid(g_ref[pl.ds(base, ROWS), :])
            return 0
        lax.fori_loop(0, _steps(t), glu, 0)

        def conv(i, _):
            base = pl.multiple_of(i * ROWS, ROWS)
            e = pad[pl.ds(base, ROWS + HALO), :]
            acc = jnp.zeros((ROWS, c), F32) + b_ref[...]
            for j in range(kc):
                acc = acc + w_ref[j:j + 1, :] * _rows(e, HALO - (kc - 1) + j, ROWS)
            o_ref[pl.ds(base, ROWS), :] = acc
            return 0
        lax.fori_loop(0, _steps(t), conv, 0)

    return _pcall(body, carry_us=CONV_FWD_US_PER_ELEM * t * d, grid=(ns,),
                  in_specs=[pl.BlockSpec((t, c), lambda s: (0, s)), pl.BlockSpec((t, c), lambda s: (0, s + ns)),
                            pl.BlockSpec((kc, c), lambda s: (0, s)), pl.BlockSpec((1, c), lambda s: (0, s))],
                  out_specs=pl.BlockSpec((t, c), lambda s: (0, s)), out_shape=jax.ShapeDtypeStruct((t, d), F32),
                  scratch_shapes=[pltpu.VMEM((t + HALO, c), F32)], compiler_params=_params(("parallel",)), name=name)(u, u, dw_w, dw_b)


def _conf_conv_bwd(dc, u, dw_w, name):
    t, d = dc.shape
    c = LANES
    ns = d // c
    kc = dw_w.shape[0]

    def body(dc_ref, a_ref, g_ref, w_ref, du_ref, dww_ref, dwb_ref, db_ref, padv, padd, accw, accb):
        padv[0:HALO, :] = jnp.zeros((HALO, c), F32)
        padd[t:t + HALO, :] = jnp.zeros((HALO, c), F32)
        accw[...] = jnp.zeros_like(accw)
        accb[...] = jnp.zeros_like(accb)

        def fill(i, _):
            base = pl.multiple_of(i * ROWS, ROWS)
            padv[pl.ds(base + HALO, ROWS), :] = a_ref[pl.ds(base, ROWS), :] * _sigmoid(g_ref[pl.ds(base, ROWS), :])
            padd[pl.ds(base, ROWS), :] = dc_ref[pl.ds(base, ROWS), :]
            return 0
        lax.fori_loop(0, _steps(t), fill, 0)

        def step(i, _):
            base = pl.multiple_of(i * ROWS, ROWS)
            ev = padv[pl.ds(base, ROWS + HALO), :]
            ed = padd[pl.ds(base, ROWS + HALO), :]
            dcc = ed[0:ROWS]
            dv = jnp.zeros((ROWS, c), F32)
            for j in range(kc):
                dv = dv + w_ref[j:j + 1, :] * _rows(ed, kc - 1 - j, ROWS)
                accw[j] = accw[j] + _fold8(dcc * _rows(ev, HALO - (kc - 1) + j, ROWS))
            accb[0] = accb[0] + _fold8(dcc)
            av = a_ref[pl.ds(base, ROWS), :]
            sg = _sigmoid(g_ref[pl.ds(base, ROWS), :])
            da = dv * sg
            dg = dv * av * sg * (1.0 - sg)
            du_ref[0, pl.ds(base, ROWS), :] = da.astype(BF16)
            du_ref[1, pl.ds(base, ROWS), :] = dg.astype(BF16)
            accb[1] = accb[1] + _fold8(da)
            accb[2] = accb[2] + _fold8(dg)
            return 0
        lax.fori_loop(0, _steps(t), step, 0)
        for j in range(kc):
            dww_ref[j:j + 1, :] = jnp.sum(accw[j], axis=0, keepdims=True)
        dwb_ref[...] = jnp.sum(accb[0], axis=0, keepdims=True)
        db_ref[0] = jnp.sum(accb[1], axis=0, keepdims=True)
        db_ref[1] = jnp.sum(accb[2], axis=0, keepdims=True)

    return _pcall(body, grid=(ns,),
                  in_specs=[pl.BlockSpec((t, c), lambda s: (0, s)), pl.BlockSpec((t, c), lambda s: (0, s)),
                            pl.BlockSpec((t, c), lambda s: (0, s + ns)), pl.BlockSpec((kc, c), lambda s: (0, s))],
                  out_specs=[pl.BlockSpec((2, t, c), lambda s: (0, 0, s)), pl.BlockSpec((kc, c), lambda s: (0, s)),
                             pl.BlockSpec((1, c), lambda s: (0, s)), pl.BlockSpec((2, 1, c), lambda s: (0, 0, s))],
                  out_shape=[jax.ShapeDtypeStruct((2, t, d), BF16), jax.ShapeDtypeStruct((kc, d), F32),
                             jax.ShapeDtypeStruct((1, d), F32), jax.ShapeDtypeStruct((2, 1, d), F32)],
                  scratch_shapes=[pltpu.VMEM((t + HALO, c), F32), pltpu.VMEM((t + HALO, c), F32),
                                  pltpu.VMEM((kc, 8, c), F32), pltpu.VMEM((3, 8, c), F32)],
                  compiler_params=_params(("parallel",)), name=name)(dc, u, u, dw_w)


def _ffn_act_fwd(u0, dw_w, dw_b, name):
    t, f2 = u0.shape
    f = f2 // 2
    c = LANES
    ns = f // c
    kw = dw_w.shape[0]

    def body(g_ref, v_ref, wg_ref, wv_ref, bg_ref, bv_ref, o_ref):
        def step(i, _):
            base = pl.multiple_of(i * ROWS, ROWS)
            lo = pl.multiple_of(jnp.maximum(base - HALO, 0), HALO)
            keep = jnp.where(i > 0, 1.0, 0.0)
            eg = jnp.concatenate([g_ref[pl.ds(lo, HALO), :] * keep, g_ref[pl.ds(base, ROWS), :]], axis=0)
            ev = jnp.concatenate([v_ref[pl.ds(lo, HALO), :] * keep, v_ref[pl.ds(base, ROWS), :]], axis=0)
            gate = jnp.zeros((ROWS, c), F32) + bg_ref[...]
            val = jnp.zeros((ROWS, c), F32) + bv_ref[...]
            for j in range(kw):
                gate = gate + wg_ref[j:j + 1, :] * _rows(eg, HALO - (kw - 1) + j, ROWS)
                val = val + wv_ref[j:j + 1, :] * _rows(ev, HALO - (kw - 1) + j, ROWS)
            o_ref[pl.ds(base, ROWS), :] = (gate * _sigmoid(gate) * val).astype(BF16)
            return 0
        lax.fori_loop(0, _steps(t), step, 0)

    return _pcall(body, carry_us=ACT_FWD_US_PER_ELEM * t * f, grid=(ns,),
                  in_specs=[pl.BlockSpec((t, c), lambda s: (0, s)), pl.BlockSpec((t, c), lambda s: (0, s + ns)),
                            pl.BlockSpec((kw, c), lambda s: (0, s)), pl.BlockSpec((kw, c), lambda s: (0, s + ns)),
                            pl.BlockSpec((1, c), lambda s: (0, s)), pl.BlockSpec((1, c), lambda s: (0, s + ns))],
                  out_specs=pl.BlockSpec((t, c), lambda s: (0, s)), out_shape=jax.ShapeDtypeStruct((t, f), BF16),
                  compiler_params=_params(("parallel",)), name=name)(u0, u0, dw_w, dw_w, dw_b, dw_b)


def _ffn_act_bwd(da, u0, dw_w, dw_b, name):
    t, f = da.shape
    c = LANES
    ns = f // c
    kw = dw_w.shape[0]

    def body(da_ref, g_ref, v_ref, wg_ref, wv_ref, bg_ref, bv_ref, du_ref, dww_ref, dwb_ref, padg, padv, accw, accb):
        padg[t:t + HALO, :] = jnp.zeros((HALO, c), F32)
        padv[t:t + HALO, :] = jnp.zeros((HALO, c), F32)
        accw[...] = jnp.zeros_like(accw)
        accb[...] = jnp.zeros_like(accb)

        def first(i, _):
            base = pl.multiple_of(i * ROWS, ROWS)
            lo = pl.multiple_of(jnp.maximum(base - HALO, 0), HALO)
            keep = jnp.where(i > 0, 1.0, 0.0)
            eg = jnp.concatenate([g_ref[pl.ds(lo, HALO), :] * keep, g_ref[pl.ds(base, ROWS), :]], axis=0)
            ev = jnp.concatenate([v_ref[pl.ds(lo, HALO), :] * keep, v_ref[pl.ds(base, ROWS), :]], axis=0)
            gate = jnp.zeros((ROWS, c), F32) + bg_ref[...]
            val = jnp.zeros((ROWS, c), F32) + bv_ref[...]
            sh_g, sh_v = [], []
            for j in range(kw):
                sh_g.append(_rows(eg, HALO - (kw - 1) + j, ROWS))
                sh_v.append(_rows(ev, HALO - (kw - 1) + j, ROWS))
                gate = gate + wg_ref[j:j + 1, :] * sh_g[j]
                val = val + wv_ref[j:j + 1, :] * sh_v[j]
            dav = da_ref[pl.ds(base, ROWS), :]
            sg = _sigmoid(gate)
            dgate = dav * val * (sg * (1.0 + gate * (1.0 - sg)))
            dval = dav * (gate * sg)
            padg[pl.ds(base, ROWS), :] = dgate
            padv[pl.ds(base, ROWS), :] = dval
            for j in range(kw):
                accw[j] = accw[j] + _fold8(dgate * sh_g[j])
                accw[kw + j] = accw[kw + j] + _fold8(dval * sh_v[j])
            accb[0] = accb[0] + _fold8(dgate)
            accb[1] = accb[1] + _fold8(dval)
            return 0
        lax.fori_loop(0, _steps(t), first, 0)

        def second(i, _):
            base = pl.multiple_of(i * ROWS, ROWS)
            eg = padg[pl.ds(base, ROWS + HALO), :]
            ev = padv[pl.ds(base, ROWS + HALO), :]
            dg = jnp.zeros((ROWS, c), F32)
            dv = jnp.zeros((ROWS, c), F32)
            for j in range(kw):
                dg = dg + wg_ref[j:j + 1, :] * _rows(eg, kw - 1 - j, ROWS)
                dv = dv + wv_ref[j:j + 1, :] * _rows(ev, kw - 1 - j, ROWS)
            du_ref[0, pl.ds(base, ROWS), :] = dg.astype(BF16)
            du_ref[1, pl.ds(base, ROWS), :] = dv.astype(BF16)
            return 0
        lax.fori_loop(0, _steps(t), second, 0)
        for j in range(kw):
            dww_ref[0, j:j + 1, :] = jnp.sum(accw[j], axis=0, keepdims=True)
            dww_ref[1, j:j + 1, :] = jnp.sum(accw[kw + j], axis=0, keepdims=True)
        dwb_ref[0] = jnp.sum(accb[0], axis=0, keepdims=True)
        dwb_ref[1] = jnp.sum(accb[1], axis=0, keepdims=True)

    return _pcall(body, grid=(ns,),
                  in_specs=[pl.BlockSpec((t, c), lambda s: (0, s)),
                            pl.BlockSpec((t, c), lambda s: (0, s)), pl.BlockSpec((t, c), lambda s: (0, s + ns)),
                            pl.BlockSpec((kw, c), lambda s: (0, s)), pl.BlockSpec((kw, c), lambda s: (0, s + ns)),
                            pl.BlockSpec((1, c), lambda s: (0, s)), pl.BlockSpec((1, c), lambda s: (0, s + ns))],
                  out_specs=[pl.BlockSpec((2, t, c), lambda s: (0, 0, s)), pl.BlockSpec((2, kw, c), lambda s: (0, 0, s)),
                             pl.BlockSpec((2, 1, c), lambda s: (0, 0, s))],
                  out_shape=[jax.ShapeDtypeStruct((2, t, f), BF16), jax.ShapeDtypeStruct((2, kw, f), F32),
                             jax.ShapeDtypeStruct((2, 1, f), F32)],
                  scratch_shapes=[pltpu.VMEM((t + HALO, c), F32), pltpu.VMEM((t + HALO, c), F32),
                                  pltpu.VMEM((2 * kw, 8, c), F32), pltpu.VMEM((2, 8, c), F32)],
                  compiler_params=_params(("parallel",)), name=name)(da, u0, u0, dw_w, dw_w, dw_b, dw_b)


def _window_of(group):
    w = jnp.float32(POOL_WINDOWS[-1])
    for k in range(len(POOL_WINDOWS) - 2, -1, -1):
        w = jnp.where(group == k, jnp.float32(POOL_WINDOWS[k]), w)
    return w


def _select_level(group, levels):
    out = levels[-1]
    for k in range(len(levels) - 2, -1, -1):
        out = jnp.where(group == k, levels[k], out)
    return out


def _pool_fwd(h, name):
    t, d = h.shape
    c = LANES
    per = d // len(POOL_WINDOWS) // c

    def body(h_ref, o_ref):
        group = pl.program_id(0)
        wf = _window_of(group)

        def step(i, _):
            base = pl.multiple_of(i * ROWS, ROWS)
            lo = pl.multiple_of(jnp.maximum(base - HALO, 0), HALO)
            keep = jnp.where(i > 0, 1.0, 0.0)
            cur = h_ref[pl.ds(base, ROWS), :]
            e = jnp.concatenate([h_ref[pl.ds(lo, HALO), :] * keep, cur], axis=0)
            n = ROWS + HALO
            levels = []
            s = e
            for k in range(len(POOL_WINDOWS)):
                s = s + pltpu.roll(s, 1 << k, 0)
                levels.append(s[HALO:n])
            tpos = (base + lax.broadcasted_iota(jnp.int32, (ROWS, c), 0) + 1).astype(F32)
            pooled = _select_level(group, levels) / jnp.minimum(tpos, wf)
            o_ref[pl.ds(base, ROWS), :] = (pooled - cur).astype(BF16)
            return 0
        lax.fori_loop(0, _steps(t), step, 0)

    return _pcall(body, grid=(len(POOL_WINDOWS), per), in_specs=[pl.BlockSpec((t, c), lambda g, s: (0, g * per + s))],
                  out_specs=pl.BlockSpec((t, c), lambda g, s: (0, g * per + s)), out_shape=jax.ShapeDtypeStruct((t, d), BF16),
                  compiler_params=_params(("parallel", "parallel")), name=name)(h)


def _pool_bwd(dm, name):
    t, d = dm.shape
    c = LANES
    per = d // len(POOL_WINDOWS) // c

    def body(dm_ref, o_ref, pad):
        group = pl.program_id(0)
        wf = _window_of(group)
        pad[t:t + HALO, :] = jnp.zeros((HALO, c), F32)

        def fill(i, _):
            base = pl.multiple_of(i * ROWS, ROWS)
            tpos = (base + lax.broadcasted_iota(jnp.int32, (ROWS, c), 0) + 1).astype(F32)
            pad[pl.ds(base, ROWS), :] = dm_ref[pl.ds(base, ROWS), :] / jnp.minimum(tpos, wf)
            return 0
        lax.fori_loop(0, _steps(t), fill, 0)

        def step(i, _):
            base = pl.multiple_of(i * ROWS, ROWS)
            n = ROWS + HALO
            s = pad[pl.ds(base, n), :]
            levels = []
            for k in range(len(POOL_WINDOWS)):
                s = s + pltpu.roll(s, n - (1 << k), 0)
                levels.append(s[0:ROWS])
            o_ref[pl.ds(base, ROWS), :] = _select_level(group, levels) - dm_ref[pl.ds(base, ROWS), :]
            return 0
        lax.fori_loop(0, _steps(t), step, 0)

    return _pcall(body, grid=(len(POOL_WINDOWS), per), in_specs=[pl.BlockSpec((t, c), lambda g, s: (0, g * per + s))],
                  out_specs=pl.BlockSpec((t, c), lambda g, s: (0, g * per + s)), out_shape=jax.ShapeDtypeStruct((t, d), F32),
                  scratch_shapes=[pltpu.VMEM((t + HALO, c), F32)], compiler_params=_params(("parallel", "parallel")), name=name)(dm)


def _pool_mm_fwd(mixed, wg, scale, res, name):
    t, d = mixed.shape
    ng, gd, _ = wg.shape
    tm = _pick(t, 1024, 16)

    def body(a_ref, w_ref, s_ref, r_ref, o_ref):
        y = jnp.dot(a_ref[...], w_ref[...], preferred_element_type=F32)
        o_ref[...] = r_ref[...] + y * s_ref[...]

    blk = pl.BlockSpec((tm, gd), lambda g, i: (i, g))
    return _pcall(body, grid=(ng, t // tm),
                  in_specs=[blk, pl.BlockSpec((None, gd, gd), lambda g, i: (g, 0, 0)), pl.BlockSpec((1, gd), lambda g, i: (0, g)), blk],
                  out_specs=blk, out_shape=jax.ShapeDtypeStruct((t, d), F32),
                  compiler_params=_params(("parallel", "parallel")), name=name)(mixed, wg, scale, res)


def _pool_mm_bwd(dy, mixed, wg, scale, name):
    t, d = mixed.shape
    ng, gd, _ = wg.shape
    tm = _pick(t, 1024, 16)

    def body(dy_ref, a_ref, w_ref, s_ref, dm_ref, dw_ref, ds_ref):
        a, w, dyv = a_ref[...], w_ref[...], dy_ref[...]
        y = jnp.dot(a, w, preferred_element_type=F32)
        dyp = (dyv * s_ref[...]).astype(BF16)
        dm_ref[...] = lax.dot_general(dyp, w, (((1,), (1,)), ((), ())), preferred_element_type=F32)
        pw = lax.dot_general(a, dyp, (((0,), (0,)), ((), ())), preferred_element_type=F32)
        ps = jnp.sum(dyv * y, axis=0, keepdims=True)

        @pl.when(pl.program_id(1) == 0)
        def _():
            dw_ref[...] = pw
            ds_ref[...] = ps

        @pl.when(pl.program_id(1) > 0)
        def _():
            dw_ref[...] += pw
            ds_ref[...] += ps

    blk = pl.BlockSpec((tm, gd), lambda g, i: (i, g))
    wsp = pl.BlockSpec((None, gd, gd), lambda g, i: (g, 0, 0))
    vec = pl.BlockSpec((1, gd), lambda g, i: (0, g))
    return _pcall(body, grid=(ng, t // tm), in_specs=[blk, blk, wsp, vec], out_specs=[blk, wsp, vec],
                  out_shape=[jax.ShapeDtypeStruct((t, d), F32), jax.ShapeDtypeStruct((ng, gd, gd), F32), jax.ShapeDtypeStruct((1, d), F32)],
                  compiler_params=_params(("parallel", "arbitrary")), name=name)(dy, mixed, wg, scale)


def _rope_tables(positions):
    half = ROT_DIM // 2
    inv_freq = ROPE_THETA ** (-jnp.arange(0, ROT_DIM, 2, dtype=F32) / ROT_DIM)
    ang = positions.astype(F32)[:, None] * inv_freq
    cos, sin = jnp.cos(ang), jnp.sin(ang)
    t = positions.shape[0]
    ones = jnp.ones((t, HEAD - ROT_DIM), F32)
    zeros = jnp.zeros((t, HEAD - ROT_DIM), F32)
    zh = jnp.zeros((t, half), F32)
    c = jnp.concatenate([cos, cos, ones], axis=1)
    s1 = jnp.concatenate([-sin, zh, zeros], axis=1)
    s2 = jnp.concatenate([zh, sin, zeros], axis=1)
    return tuple(jnp.concatenate([a, a], axis=1) for a in (c, s1, s2))


def _half_mean(v, lo):
    s_lo = jnp.sum(jnp.where(lo, v, 0.0), axis=-1, keepdims=True)
    s_hi = jnp.sum(jnp.where(lo, 0.0, v), axis=-1, keepdims=True)
    return jnp.where(lo, s_lo, s_hi) * (1.0 / HEAD)


def _qk_prep_fwd(qkv, tabs, gq2, gk2, n_q, n_kv, name):
    t, width = qkv.shape
    tm = _pick(t, 256, 16)
    nqc, nkc = n_q * HEAD // LANES, n_kv * HEAD // LANES

    def body(x_ref, c_ref, s1_ref, s2_ref, gq_ref, gk_ref, q_ref, k2_ref, v2_ref):
        lo = lax.broadcasted_iota(jnp.int32, (tm, LANES), 1) < HEAD
        cv, s1, s2 = c_ref[...], s1_ref[...], s2_ref[...]

        def normrot(xc, g2):
            y = xc * lax.rsqrt(_half_mean(xc * xc, lo) + EPS) * g2
            return y * cv + pltpu.roll(y, LANES - ROT_DIM // 2, 1) * s1 + pltpu.roll(y, ROT_DIM // 2, 1) * s2

        def twice(y, j):
            sw = pltpu.roll(y, HEAD, 1)
            k2 = jnp.where(lo, y, sw) if j == 0 else jnp.where(lo, sw, y)
            return k2.astype(BF16)

        for ch in range(nqc):
            q_ref[:, ch * LANES:(ch + 1) * LANES] = normrot(x_ref[:, ch * LANES:(ch + 1) * LANES], gq_ref[...]).astype(BF16)
        for ch in range(nkc):
            off = (nqc + ch) * LANES
            y = normrot(x_ref[:, off:off + LANES], gk_ref[...])
            voff = (nqc + nkc + ch) * LANES
            vv = x_ref[:, voff:voff + LANES]
            for j in range(2):
                k2_ref[:, (2 * ch + j) * LANES:(2 * ch + j + 1) * LANES] = twice(y, j)
                v2_ref[:, (2 * ch + j) * LANES:(2 * ch + j + 1) * LANES] = twice(vv, j)

    row = lambda w: pl.BlockSpec((tm, w), lambda i: (i, 0))
    vec = pl.BlockSpec((1, LANES), lambda i: (0, 0))
    return _pcall(body, grid=(t // tm,), in_specs=[row(width), row(LANES), row(LANES), row(LANES), vec, vec],
                  out_specs=[row(n_q * HEAD), row(n_kv * LANES), row(n_kv * LANES)],
                  out_shape=[jax.ShapeDtypeStruct((t, n_q * HEAD), BF16), jax.ShapeDtypeStruct((t, n_kv * LANES), BF16),
                             jax.ShapeDtypeStruct((t, n_kv * LANES), BF16)],
                  compiler_params=_params(("parallel",)), name=name)(qkv, *tabs, gq2, gk2)


def _qk_prep_bwd(dq, dk_cur, dk_prev, dv_cur, dv_prev, qkv, tabs, gq2, gk2, n_q, n_kv, name):
    t, width = qkv.shape
    tm = Q_BLOCK
    nb = t // tm
    nqc, nkc = n_q * HEAD // LANES, n_kv * HEAD // LANES

    def body(dq_ref, kc_ref, kp_ref, vc_ref, vp_ref, x_ref, c_ref, s1_ref, s2_ref, gq_ref, gk_ref, o_ref, dgq_ref, dgk_ref):
        lo = lax.broadcasted_iota(jnp.int32, (tm, LANES), 1) < HEAD
        cv, s1, s2 = c_ref[...], s1_ref[...], s2_ref[...]
        more = jnp.where(pl.program_id(0) < nb - 1, 1.0, 0.0)

        def back(dy, xc, g2):
            dyn = dy * cv + pltpu.roll(dy * s1, ROT_DIM // 2, 1) + pltpu.roll(dy * s2, LANES - ROT_DIM // 2, 1)
            r = lax.rsqrt(_half_mean(xc * xc, lo) + EPS)
            xh = xc * r
            dxh = dyn * g2
            return r * (dxh - xh * _half_mean(dxh * xh, lo)), jnp.sum(dyn * xh, axis=0, keepdims=True)

        def unfold(cur_ref, prev_ref, ch):
            d0 = cur_ref[:, (2 * ch) * LANES:(2 * ch + 1) * LANES] + more * prev_ref[:, (2 * ch) * LANES:(2 * ch + 1) * LANES]
            d1 = cur_ref[:, (2 * ch + 1) * LANES:(2 * ch + 2) * LANES] + more * prev_ref[:, (2 * ch + 1) * LANES:(2 * ch + 2) * LANES]
            return jnp.where(lo, d0 + pltpu.roll(d0, HEAD, 1), d1 + pltpu.roll(d1, HEAD, 1))

        pq = jnp.zeros((1, LANES), F32)
        for ch in range(nqc):
            sl = slice(ch * LANES, (ch + 1) * LANES)
            dx, pg = back(dq_ref[:, sl], x_ref[:, sl], gq_ref[...])
            o_ref[:, sl] = dx.astype(BF16)
            pq = pq + pg
        pk = jnp.zeros((1, LANES), F32)
        for ch in range(nkc):
            sl = slice((nqc + ch) * LANES, (nqc + ch + 1) * LANES)
            dx, pg = back(unfold(kc_ref, kp_ref, ch), x_ref[:, sl], gk_ref[...])
            o_ref[:, sl] = dx.astype(BF16)
            pk = pk + pg
            vs = slice((nqc + nkc + ch) * LANES, (nqc + nkc + ch + 1) * LANES)
            o_ref[:, vs] = unfold(vc_ref, vp_ref, ch).astype(BF16)

        @pl.when(pl.program_id(0) == 0)
        def _():
            dgq_ref[...] = pq
            dgk_ref[...] = pk

        @pl.when(pl.program_id(0) > 0)
        def _():
            dgq_ref[...] += pq
            dgk_ref[...] += pk

    row = lambda w: pl.BlockSpec((tm, w), lambda i: (i, 0))
    nxt = lambda w: pl.BlockSpec((tm, w), lambda i: (jnp.minimum(i + 1, nb - 1), 0))
    vec = pl.BlockSpec((1, LANES), lambda i: (0, 0))
    kvw = n_kv * LANES
    return _pcall(body, grid=(nb,),
                  in_specs=[row(n_q * HEAD), row(kvw), nxt(kvw), row(kvw), nxt(kvw), row(width), row(LANES), row(LANES), row(LANES), vec, vec],
                  out_specs=[row(width), vec, vec],
                  out_shape=[jax.ShapeDtypeStruct((t, width), BF16), jax.ShapeDtypeStruct((1, LANES), F32), jax.ShapeDtypeStruct((1, LANES), F32)],
                  compiler_params=_params(("arbitrary",)), name=name)(dq, dk_cur, dk_prev, dv_cur, dv_prev, qkv, *tabs, gq2, gk2)


def _band_scores(qh, kc, kp, n, sink_row, lo_row, is_lo):
    scale = 1.0 / math.sqrt(HEAD)
    nt = (((1,), (1,)), ((), ()))
    s_c = lax.dot_general(qh, kc, nt, preferred_element_type=F32) * scale
    s_p = lax.dot_general(qh, kp, nt, preferred_element_type=F32) * scale
    qi = lax.broadcasted_iota(jnp.int32, (Q_BLOCK, Q_BLOCK), 0)
    kj = lax.broadcasted_iota(jnp.int32, (Q_BLOCK, Q_BLOCK), 1)
    s_c = jnp.where(kj <= qi, s_c, -jnp.inf)
    s_p = jnp.where((kj > qi) & (n > 0), s_p, -jnp.inf)
    pick = lo_row if is_lo else jnp.logical_not(lo_row)
    sink = jnp.max(jnp.where(pick, sink_row, -jnp.inf), axis=-1, keepdims=True)
    return s_c, s_p, sink


def _attn_fwd(q, k2, v2, sink_tab, name):
    t, dq = q.shape
    nc = dq // LANES
    nb = t // Q_BLOCK
    per_kv = nc // (k2.shape[1] // LANES)

    def body(q_ref, kc_ref, kp_ref, vc_ref, vp_ref, s_ref, o_ref, lse_ref):
        n = pl.program_id(1)
        lo = lax.broadcasted_iota(jnp.int32, (Q_BLOCK, LANES), 1) < HEAD
        lo_row = lax.broadcasted_iota(jnp.int32, (1, LANES), 1) < HEAD
        qv = q_ref[...].astype(F32)
        kc, kp, vc, vp = kc_ref[...], kp_ref[...], vc_ref[...], vp_ref[...]
        outs, lses = [], []
        for is_lo in (True, False):
            qh = jnp.where(lo, qv, 0.0) if is_lo else jnp.where(lo, 0.0, qv)
            s_c, s_p, sink = _band_scores(qh.astype(BF16), kc, kp, n, s_ref[0:1, :], lo_row, is_lo)
            m = jnp.maximum(jnp.maximum(jnp.max(s_c, axis=-1, keepdims=True), jnp.max(s_p, axis=-1, keepdims=True)), sink)
            p_c, p_p = jnp.exp(s_c - m), jnp.exp(s_p - m)
            denom = jnp.sum(p_c, axis=-1, keepdims=True) + jnp.sum(p_p, axis=-1, keepdims=True) + jnp.exp(sink - m)
            pv = jnp.dot(p_c.astype(BF16), vc, preferred_element_type=F32) + jnp.dot(p_p.astype(BF16), vp, preferred_element_type=F32)
            outs.append(pv / denom)
            lses.append(m + jnp.log(denom))
        o_ref[...] = jnp.where(lo, outs[0], outs[1]).astype(BF16)
        lse_ref[...] = jnp.where(lo, lses[0], lses[1])

    qs = pl.BlockSpec((Q_BLOCK, LANES), lambda c, n: (n, c))
    cur = pl.BlockSpec((Q_BLOCK, LANES), lambda c, n: (n, c // per_kv))
    prev = pl.BlockSpec((Q_BLOCK, LANES), lambda c, n: (jnp.maximum(n - 1, 0), c // per_kv))
    return _pcall(body, carry_us=ATTN_US_PER_STEP * nc * nb, grid=(nc, nb),
                  in_specs=[qs, cur, prev, cur, prev, pl.BlockSpec((8, LANES), lambda c, n: (c, 0))],
                  out_specs=[qs, pl.BlockSpec((None, Q_BLOCK, LANES), lambda c, n: (c, n, 0))],
                  out_shape=[jax.ShapeDtypeStruct((t, dq), BF16), jax.ShapeDtypeStruct((nc, t, LANES), F32)],
                  compiler_params=_params(("parallel", "parallel")), name=name)(q, k2, k2, v2, v2, sink_tab)


def _attn_bwd(do, q, o, lse, k2, v2, sink_tab, name):
    t, dq = q.shape
    nc = dq // LANES
    nb = t // Q_BLOCK
    nkv = k2.shape[1] // LANES
    per_kv = nc // nkv
    scale = 1.0 / math.sqrt(HEAD)
    tn_ = (((0,), (0,)), ((), ()))
    nt = (((1,), (1,)), ((), ()))

    def body(do_ref, q_ref, o_ref, lse_ref, kc_ref, kp_ref, vc_ref, vp_ref, s_ref,
             dq_ref, dkc_ref, dkp_ref, dvc_ref, dvp_ref, dsk_ref):
        n = pl.program_id(1)
        cc = pl.program_id(2)
        lo = lax.broadcasted_iota(jnp.int32, (Q_BLOCK, LANES), 1) < HEAD
        lo_row = lax.broadcasted_iota(jnp.int32, (1, LANES), 1) < HEAD
        qv, dov, ov, lsev = q_ref[...].astype(F32), do_ref[...], o_ref[...].astype(F32), lse_ref[...]
        kc, kp, vc, vp = kc_ref[...], kp_ref[...], vc_ref[...], vp_ref[...]
        dqs, dsinks = [], []
        dkc = dkp = dvc = dvp = None
        for is_lo in (True, False):
            half = lo if is_lo else jnp.logical_not(lo)
            qh = jnp.where(half, qv, 0.0).astype(BF16)
            doh = jnp.where(half, dov, 0.0)
            s_c, s_p, sink = _band_scores(qh, kc, kp, n, s_ref[0:1, :], lo_row, is_lo)
            lse_h = jnp.max(jnp.where(half, lsev, -jnp.inf), axis=-1, keepdims=True)
            p_c, p_p = jnp.exp(s_c - lse_h), jnp.exp(s_p - lse_h)
            delta = jnp.sum(doh * ov, axis=-1, keepdims=True)
            dob = doh.astype(BF16)
            ds_c = (p_c * (lax.dot_general(dob, vc, nt, preferred_element_type=F32) - delta)).astype(BF16)
            ds_p = (p_p * (lax.dot_general(dob, vp, nt, preferred_element_type=F32) - delta)).astype(BF16)
            dsinks.append(-jnp.sum(jnp.exp(sink - lse_h) * delta, axis=0, keepdims=True))
            dqs.append((jnp.dot(ds_c, kc, preferred_element_type=F32) + jnp.dot(ds_p, kp, preferred_element_type=F32)) * scale)
            parts = (lax.dot_general(ds_c, qh, tn_, preferred_element_type=F32) * scale,
                     lax.dot_general(ds_p, qh, tn_, preferred_element_type=F32) * scale,
                     lax.dot_general(p_c.astype(BF16), dob, tn_, preferred_element_type=F32),
                     lax.dot_general(p_p.astype(BF16), dob, tn_, preferred_element_type=F32))
            if dkc is None:
                dkc, dkp, dvc, dvp = parts
            else:
                dkc, dkp, dvc, dvp = dkc + parts[0], dkp + parts[1], dvc + parts[2], dvp + parts[3]
        dq_ref[...] = jnp.where(lo, dqs[0], dqs[1])
        dsk_ref[...] = jnp.zeros((8, LANES), F32) + jnp.where(lo_row, dsinks[0], dsinks[1])

        @pl.when(cc == 0)
        def _():
            dkc_ref[...] = dkc
            dkp_ref[...] = dkp
            dvc_ref[...] = dvc
            dvp_ref[...] = dvp

        @pl.when(cc > 0)
        def _():
            dkc_ref[...] += dkc
            dkp_ref[...] += dkp
            dvc_ref[...] += dvc
            dvp_ref[...] += dvp

    qs = pl.BlockSpec((Q_BLOCK, LANES), lambda k, n, cc: (n, k * per_kv + cc))
    cur = pl.BlockSpec((Q_BLOCK, LANES), lambda k, n, cc: (n, k))
    prev = pl.BlockSpec((Q_BLOCK, LANES), lambda k, n, cc: (jnp.maximum(n - 1, 0), k))
    kv_shape = jax.ShapeDtypeStruct((t, nkv * LANES), F32)
    return _pcall(body, carry_us=ATTN_US_PER_STEP * nc * nb, grid=(nkv, nb, per_kv),
                  in_specs=[qs, qs, qs, pl.BlockSpec((None, Q_BLOCK, LANES), lambda k, n, cc: (k * per_kv + cc, n, 0)),
                            cur, prev, cur, prev, pl.BlockSpec((8, LANES), lambda k, n, cc: (k * per_kv + cc, 0))],
                  out_specs=[qs, cur, cur, cur, cur, pl.BlockSpec((None, 8, LANES), lambda k, n, cc: ((k * per_kv + cc) * nb + n, 0, 0))],
                  out_shape=[jax.ShapeDtypeStruct((t, dq), F32), kv_shape, kv_shape, kv_shape, kv_shape,
                             jax.ShapeDtypeStruct((nc * nb, 8, LANES), F32)],
                  compiler_params=_params(("parallel", "parallel", "arbitrary")), name=name)(do, q, o, lse, k2, k2, v2, v2, sink_tab)


def _peer(k):
    x, y, c = lax.axis_index("x"), lax.axis_index("y"), lax.axis_index("c")
    flip = lambda v, bit: 1 - v if bit else v
    return (flip(x, k & 4), flip(y, k & 2), flip(c, k & 1))


def _my_index():
    return 4 * lax.axis_index("x") + 2 * lax.axis_index("y") + lax.axis_index("c")


def _peer_index(k):
    px, py, pc = _peer(k)
    return 4 * px + 2 * py + pc


def _all_gather(shards, name):
    n = len(shards)
    any_spec = pl.BlockSpec(memory_space=pl.ANY)

    def body(*refs):
        ins, outs = refs[:n], refs[n:2 * n]
        send_sems, recv_sems, local_sems = refs[2 * n:]
        me = _my_index()
        local = [pltpu.make_async_copy(ins[a], outs[a].at[me], local_sems.at[a]) for a in range(n)]
        for cp in local:
            cp.start()
        sends = []
        for k in range(1, N_DEV):
            for a in range(n):
                cp = pltpu.make_async_remote_copy(src_ref=ins[a], dst_ref=outs[a].at[me], send_sem=send_sems.at[a, k - 1],
                                                  recv_sem=recv_sems.at[a, k - 1], device_id=_peer(k), device_id_type=MESH_ID)
                cp.start()
                sends.append(cp)
        for k in range(1, N_DEV):
            for a in range(n):
                pltpu.make_async_remote_copy(src_ref=ins[a], dst_ref=outs[a].at[_peer_index(k)], send_sem=send_sems.at[a, k - 1],
                                             recv_sem=recv_sems.at[a, k - 1], device_id=_peer(k), device_id_type=MESH_ID).wait_recv()
        for cp in sends:
            cp.wait_send()
        for cp in local:
            cp.wait()

    return _pcall(body, in_specs=[any_spec] * n, out_specs=[any_spec] * n,
                  out_shape=[jax.ShapeDtypeStruct((N_DEV,) + s.shape, s.dtype) for s in shards],
                  scratch_shapes=[pltpu.SemaphoreType.DMA((n, N_DEV - 1)), pltpu.SemaphoreType.DMA((n, N_DEV - 1)),
                                  pltpu.SemaphoreType.DMA((n,))],
                  name=name)(*shards)


GATHER1_PEERS = (1, 2, 4, 6)
GATHER2_PEERS = (2, 4, 6)
SCATTER_PEERS = tuple(range(1, N_DEV))
MAX_SEMS = N_DEV - 1
MAX_JOBS = 6
US_PER_MB = {"gather1": 5.4, "gather2": 0.6, "scatter": 10.8}
SCATTER_PIECE_US = 110.0


class _Job:
    def __init__(self, key, kind, src, lo=0, hi=None, dst=None):
        self.key, self.kind, self.src, self.dst = key, kind, src, dst
        shape = src.shape if kind != "gather1" else (N_DEV,) + src.shape
        self.out_shape = jax.ShapeDtypeStruct(shape, src.dtype)
        self.rows = shape[1]
        self.lo, self.hi = lo, self.rows if hi is None else hi
        self.row_us = US_PER_MB[kind] * math.prod(shape) * src.dtype.itemsize / 1e6 / self.rows
        pieces = max(1, round(self.row_us * self.rows / SCATTER_PIECE_US)) if kind == "scatter" else 1
        while pieces > 1 and self.rows % (16 * pieces):
            pieces -= 1
        self.piece = self.rows // pieces

    @property
    def cost_us(self):
        return self.row_us * (self.hi - self.lo)


class _Comm:
    def __init__(self):
        self.queue, self.gathered, self.scattered, self.layer = [], {}, [], 0

    def push(self, key, kind, src):
        self.queue.append(_Job(key, kind, src))

    def take(self, budget_us):
        jobs = [j for j in self.queue if j.kind == "gather2"][:MAX_JOBS]
        used = sum(j.cost_us for j in jobs)
        for j in [j for j in self.queue if j.kind != "gather2"]:
            if len(jobs) >= MAX_JOBS:
                break
            if j.kind == "gather1":
                urgent = int(j.key[1]) <= self.layer
                if (used >= budget_us) if urgent else (used + 0.5 * j.cost_us > budget_us):
                    break
                jobs.append(j)
                used += j.cost_us
                continue
            n = 0
            while j.lo + (n + 1) * j.piece <= j.hi and used + 0.5 * j.row_us * j.piece <= budget_us:
                n += 1
                used += j.row_us * j.piece
            if n == 0:
                break
            part = _Job(j.key, "scatter", j.src, j.lo, j.lo + n * j.piece, j.dst)
            part.parent = j
            j.lo = part.hi
            jobs.append(part)
            if j.lo < j.hi:
                break
        self.queue = [j for j in self.queue if j not in jobs and j.lo < j.hi]
        return jobs

    def finish(self, job, result):
        if job.kind == "gather1":
            self.queue.insert(0, _Job(job.key, "gather2", result))
        elif job.kind == "gather2":
            self.gathered[job.key] = result
        elif job.hi == job.rows:
            self.scattered.append((job.key, result))
        else:
            job.parent.dst = result

    def need(self, key):
        while key not in self.gathered:
            assert any(j.key == key for j in self.queue), key
            jobs = [j for j in self.queue if j.kind == "gather2"][:MAX_JOBS]
            if not any(j.key == key for j in jobs):
                for j in self.queue:
                    if j.kind == "gather1" and len(jobs) < MAX_JOBS:
                        jobs.append(j)
                        if j.key == key:
                            break
            self.flush(jobs)
        return self.gathered[key]

    def flush(self, jobs):
        self.queue = [j for j in self.queue if j not in jobs]

        def body(o_ref):
            o_ref[...] = jnp.zeros_like(o_ref)
        _carry(body, jobs, self, dict(in_specs=[], out_specs=pl.BlockSpec(memory_space=pltpu.VMEM),
                                      out_shape=jax.ShapeDtypeStruct((8, LANES), F32), name="exchange"))()


def _job_copies(job, src, dst, send_sems, recv_sems, local_sem):
    me = _my_index()
    peers = {"gather1": GATHER1_PEERS, "gather2": GATHER2_PEERS, "scatter": SCATTER_PEERS}[job.kind]
    sends, recvs = [], []
    for i, k in enumerate(peers):
        if job.kind == "gather1":
            s_ref, d_ref, to, got = src, dst.at[me], _peer(k), dst.at[_peer_index(k)]
        elif job.kind == "gather2":
            s_ref, d_ref, to, got = src.at[_peer_index(k)], dst.at[_peer_index(k)], _peer(1), dst.at[_peer_index(k | 1)]
        else:
            rows = pl.ds(job.lo, job.hi - job.lo)
            s_ref, d_ref, to, got = src.at[_peer_index(k), rows], dst.at[me, rows], _peer(k), dst.at[_peer_index(k), rows]
        sends.append(pltpu.make_async_remote_copy(src_ref=s_ref, dst_ref=d_ref, send_sem=send_sems.at[i], recv_sem=recv_sems.at[i],
                                                  device_id=to, device_id_type=MESH_ID))
        recvs.append(pltpu.make_async_remote_copy(src_ref=s_ref, dst_ref=got, send_sem=send_sems.at[i], recv_sem=recv_sems.at[i],
                                                  device_id=to, device_id_type=MESH_ID))
    local = None
    if job.kind == "gather1":
        local = pltpu.make_async_copy(src, dst.at[me], local_sem)
    elif job.kind == "scatter":
        rows = pl.ds(job.lo, job.hi - job.lo)
        local = pltpu.make_async_copy(src.at[me, rows], dst.at[me, rows], local_sem)
    return sends, recvs, local


def _carry(body, jobs, comm, kw):
    kw = dict(kw)
    grid = tuple(kw.get("grid", ()))
    in_specs = list(kw["in_specs"])
    single = not isinstance(kw["out_specs"], (list, tuple))
    out_specs = [kw["out_specs"]] if single else list(kw["out_specs"])
    out_shape = [kw["out_shape"]] if single else list(kw["out_shape"])
    scratch = list(kw.get("scratch_shapes", []))
    n_in, n_out, n_scr, nj = len(in_specs), len(out_specs), len(scratch), len(jobs)
    any_spec = pl.BlockSpec(memory_space=pl.ANY)
    landed = [a for a, job in enumerate(jobs) if job.dst is not None]
    n_land = len(landed)

    def wrapped(*refs):
        pos = 0

        def take(k):
            nonlocal pos
            part = refs[pos:pos + k]
            pos += k
            return part
        ins, rin, _, outs, rout, scr = take(n_in), take(nj), take(n_land), take(n_out), take(nj), take(n_scr)
        send_sems, recv_sems, local_sems = take(3)

        def copies():
            return [_job_copies(job, rin[a], rout[a], send_sems.at[a], recv_sems.at[a], local_sems.at[a]) for a, job in enumerate(jobs)]

        def start():
            for sends, _, local in copies():
                if local is not None:
                    local.start()
                for cp in sends:
                    cp.start()

        def finish():
            for sends, recvs, local in copies():
                for cp in recvs:
                    cp.wait_recv()
                for cp in sends:
                    cp.wait_send()
                if local is not None:
                    local.wait()

        if grid:
            first = functools.reduce(jnp.logical_and, [pl.program_id(a) == 0 for a in range(len(grid))])
            last = functools.reduce(jnp.logical_and, [pl.program_id(a) == grid[a] - 1 for a in range(len(grid))])
            pl.when(first)(start)
            body(*ins, *outs, *scr)
            pl.when(last)(finish)
        else:
            start()
            body(*ins, *outs, *scr)
            finish()

    aliases = {n_in + a: n_out + a for a, job in enumerate(jobs) if job.kind == "gather2"}
    aliases.update({n_in + nj + i: n_out + a for i, a in enumerate(landed)})
    extra = dict(dimension_semantics=("arbitrary",) * len(grid)) if grid else {}
    call = _raw_call(wrapped, in_specs=in_specs + [any_spec] * (nj + n_land), out_specs=out_specs + [any_spec] * nj,
                     out_shape=out_shape + [job.out_shape for job in jobs],
                     scratch_shapes=scratch + [pltpu.SemaphoreType.DMA((nj, MAX_SEMS)), pltpu.SemaphoreType.DMA((nj, MAX_SEMS)),
                                               pltpu.SemaphoreType.DMA((nj,))],
                     input_output_aliases=aliases, compiler_params=_params(**extra), name=kw["name"],
                     **({"grid": grid} if grid else {}))

    def run(*args):
        res = call(*args, *[job.src for job in jobs], *[jobs[a].dst for a in landed])
        for job, r in zip(jobs, res[n_out:]):
            comm.finish(job, r)
        return res[0] if single else list(res[:n_out])
    return run


def _adam(g, w, m, v):
    m2 = ADAM_B1 * m + (1.0 - ADAM_B1) * g
    v2 = ADAM_B2 * v + (1.0 - ADAM_B2) * (g * g)
    m_hat = m2 / (1.0 - ADAM_B1 ** ADAM_STEP)
    v_hat = v2 / (1.0 - ADAM_B2 ** ADAM_STEP)
    delta = -ADAM_LR * (m_hat / (jnp.sqrt(v_hat) + ADAM_EPS) + ADAM_WD * w)
    return delta, m2, v2


def _sum_adam(parts, w, m, v, name):
    r, c = w.shape
    tr = _pick(r, max(8, (1 << 19) // c), 8)

    def body(p_ref, w_ref, m_ref, v_ref, g_ref, d_ref, m2_ref, v2_ref):
        g = p_ref[0].astype(F32)
        for j in range(1, N_DEV):
            g = g + p_ref[j].astype(F32)
        delta, m2, v2 = _adam(g, w_ref[...], m_ref[...], v_ref[...])
        g_ref[...] = g
        d_ref[...] = delta
        m2_ref[...] = m2
        v2_ref[...] = v2

    blk = pl.BlockSpec((tr, c), lambda i: (i, 0))
    shp = jax.ShapeDtypeStruct((r, c), F32)
    return _pcall(body, grid=(r // tr,),
                  in_specs=[pl.BlockSpec((N_DEV, tr, c), lambda i: (0, i, 0)), blk, blk, blk],
                  out_specs=[blk] * 4, out_shape=[shp] * 4, compiler_params=_params(("parallel",)), name=name)(parts, w, m, v)


def _small_layout(rep_shapes, sh_shapes):
    rows_r = [-(-s[1] // LANES) for s in rep_shapes]
    off_r = [sum(rows_r[:i]) for i in range(len(rows_r))]
    tot_r = -(-max(sum(rows_r), 8) // 8) * 8
    rows_s = [-(-s[-2] // 8) * 8 for s in sh_shapes]
    off_s = [sum(rows_s[:i]) for i in range(len(rows_s))]
    tot_s = max(sum(rows_s), 8)
    cmax = max([s[-1] for s in sh_shapes] + [LANES])
    return rows_r, off_r, tot_r, off_s, tot_s, cmax


def _small_exchange(rep_parts, sh_parts, name):
    nr, ns = len(rep_parts), len(sh_parts)
    rows_r, off_r, tot_r, off_s, tot_s, cmax = _small_layout([p.shape for p in rep_parts], [p.shape for p in sh_parts])
    vm = pl.BlockSpec(memory_space=pltpu.VMEM)

    def body(*refs):
        pos = 0

        def take(k):
            nonlocal pos
            out = refs[pos:pos + k]
            pos += k
            return out
        rp, sp = take(nr), take(ns)
        out_r, out_s = take(2)
        pack_r, got_r, pack_s, got_s, send_r, recv_r, send_s, recv_s = take(8)
        me = _my_index()
        pack_r[...] = jnp.zeros_like(pack_r)
        pack_s[...] = jnp.zeros_like(pack_s)
        for i in range(nr):
            nfull = rep_parts[i].shape[1]
            for rr in range(rows_r[i]):
                wdt = min(LANES, nfull - rr * LANES)
                pack_r[off_r[i] + rr:off_r[i] + rr + 1, 0:wdt] = rp[i][0:1, rr * LANES:rr * LANES + wdt]
        for i in range(ns):
            _, r_i, c_i = sh_parts[i].shape
            for j in range(N_DEV):
                pack_s[j, off_s[i]:off_s[i] + r_i, 0:c_i] = sp[i][j]
        got_r[me] = pack_r[...]
        got_s[me] = pack_s[me]
        sends = []
        for k in range(1, N_DEV):
            a = pltpu.make_async_remote_copy(src_ref=pack_r, dst_ref=got_r.at[me], send_sem=send_r.at[k - 1], recv_sem=recv_r.at[k - 1],
                                             device_id=_peer(k), device_id_type=MESH_ID)
            b = pltpu.make_async_remote_copy(src_ref=pack_s.at[_peer_index(k)], dst_ref=got_s.at[me], send_sem=send_s.at[k - 1],
                                             recv_sem=recv_s.at[k - 1], device_id=_peer(k), device_id_type=MESH_ID)
            a.start()
            b.start()
            sends += [a, b]
        for k in range(1, N_DEV):
            pltpu.make_async_remote_copy(src_ref=pack_r, dst_ref=got_r.at[_peer_index(k)], send_sem=send_r.at[k - 1],
                                         recv_sem=recv_r.at[k - 1], device_id=_peer(k), device_id_type=MESH_ID).wait_recv()
            pltpu.make_async_remote_copy(src_ref=pack_s.at[me], dst_ref=got_s.at[_peer_index(k)], send_sem=send_s.at[k - 1],
                                         recv_sem=recv_s.at[k - 1], device_id=_peer(k), device_id_type=MESH_ID).wait_recv()
        for cp in sends:
            cp.wait_send()
        tot_rep = got_r[0]
        tot_sh = got_s[0]
        for j in range(1, N_DEV):
            tot_rep = tot_rep + got_r[j]
            tot_sh = tot_sh + got_s[j]
        out_r[...] = tot_rep
        out_s[...] = tot_sh

    return _pcall(body, carry_us=SMALL_EXCHANGE_CARRY_US, in_specs=[vm] * (nr + ns), out_specs=[vm] * 2,
                  out_shape=[jax.ShapeDtypeStruct((tot_r, LANES), F32), jax.ShapeDtypeStruct((tot_s, cmax), F32)],
                  scratch_shapes=[pltpu.VMEM((tot_r, LANES), F32), pltpu.VMEM((N_DEV, tot_r, LANES), F32),
                                  pltpu.VMEM((N_DEV, tot_s, cmax), F32), pltpu.VMEM((N_DEV, tot_s, cmax), F32),
                                  pltpu.SemaphoreType.DMA((N_DEV - 1,)), pltpu.SemaphoreType.DMA((N_DEV - 1,)),
                                  pltpu.SemaphoreType.DMA((N_DEV - 1,)), pltpu.SemaphoreType.DMA((N_DEV - 1,))],
                  compiler_params=_params(), name=name)(*rep_parts, *sh_parts)


def _small_adam(tot_rep, tot_sh, rep_w, rep_m, rep_v, sh_w, sh_m, sh_v, name):
    nr, ns = len(rep_w), len(sh_w)
    rows_r, off_r, _, off_s, _, _ = _small_layout([w.shape for w in rep_w], [w.shape for w in sh_w])
    vm = pl.BlockSpec(memory_space=pltpu.VMEM)

    def body(*refs):
        pos = 0

        def take(k):
            nonlocal pos
            out = refs[pos:pos + k]
            pos += k
            return out
        (tr_ref, ts_ref), rw, rm, rv, sw, sm, sv = take(2), take(nr), take(nr), take(nr), take(ns), take(ns), take(ns)
        rg, rd, rm2, rv2 = take(nr), take(nr), take(nr), take(nr)
        sg, sd, sm2, sv2 = take(ns), take(ns), take(ns), take(ns)
        for i in range(nr):
            nfull = rep_w[i].shape[1]
            for rr in range(rows_r[i]):
                wdt = min(LANES, nfull - rr * LANES)
                rg[i][0:1, rr * LANES:rr * LANES + wdt] = tr_ref[off_r[i] + rr:off_r[i] + rr + 1, 0:wdt]
            delta, m2, v2 = _adam(rg[i][...], rw[i][...], rm[i][...], rv[i][...])
            rd[i][...] = delta
            rm2[i][...] = m2
            rv2[i][...] = v2
        for i in range(ns):
            r_i, c_i = sh_w[i].shape
            g = ts_ref[off_s[i]:off_s[i] + r_i, 0:c_i]
            delta, m2, v2 = _adam(g, sw[i][...], sm[i][...], sv[i][...])
            sg[i][...] = g
            sd[i][...] = delta
            sm2[i][...] = m2
            sv2[i][...] = v2

    shapes = [jax.ShapeDtypeStruct(w.shape, F32) for w in rep_w] * 4 + [jax.ShapeDtypeStruct(w.shape, F32) for w in sh_w] * 4
    outs = _pcall(body, in_specs=[vm] * (2 + 3 * nr + 3 * ns), out_specs=[vm] * len(shapes), out_shape=shapes,
                  compiler_params=_params(), name=name)(tot_rep, tot_sh, *rep_w, *rep_m, *rep_v, *sh_w, *sh_m, *sh_v)
    rep_out = [outs[i * nr:(i + 1) * nr] for i in range(4)]
    sh_out = [outs[4 * nr + i * ns:4 * nr + (i + 1) * ns] for i in range(4)]
    return rep_out, sh_out


_CONF = ("norm_g", "a_w_in", "a_b_in", "a_dw_w", "a_dw_b", "a_ln_g", "a_ln_b", "a_w_out", "a_b_out")
_FFN = ("ffn_norm_g", "ffn_w_up", "ffn_dw_w", "ffn_dw_b", "ffn_w_down")
_POOL = ("norm_g", "b_w_group", "b_scale")
_ATTN = ("norm_g", "c_w_qkv", "c_q_norm_g", "c_k_norm_g", "c_sinks", "c_w_o")
_LAYERS = (_CONF + _FFN, _POOL + _FFN, _ATTN + _FFN, _CONF + _FFN)
_NAMES = tuple("l%d_%s" % (i, n) for i, names in enumerate(_LAYERS) for n in names)
_BIG = ("a_w_in", "a_w_out", "ffn_w_up", "ffn_w_down", "b_w_group", "c_w_qkv", "c_w_o")
_SHARDED_SMALL = ("a_dw_w", "ffn_dw_w")


def _pad_rows(a, mult=8):
    r = a.shape[0]
    rp = -(-r // mult) * mult
    return a if rp == r else jnp.pad(a, ((0, rp - r), (0, 0)))


def _unstack_cols(st, rows):
    s, r, cs = st.shape
    return jnp.transpose(st, (1, 0, 2)).reshape(r, s * cs)[:rows]


def _stack_cols(a):
    r, c = a.shape
    return jnp.transpose(a.reshape(r, N_DEV, c // N_DEV), (1, 0, 2))


def _row(v):
    return v.reshape(1, -1)


def _ffn_forward(x_mid, p, tag):
    h2 = _rms_fwd(x_mid, _row(p["ffn_norm_g"]), BF16, "rms_fwd_bf16")
    u0 = _mm(h2, p["ffn_w_up"], "nn", name="ffn_up", tn=1408, tk=2048)
    a = _ffn_act_fwd(u0, p["ffn_dw_w"], _row(p["ffn_dw_b"]), "ffn_act_fwd")
    x_out = _mm(a, p["ffn_w_down"], "nn", res=x_mid, name="ffn_down", tk=1408)
    return x_out, dict(h2=h2, u0=u0, a=a)


def _ffn_backward(dx_out, x_mid, p, sv, grads):
    dwd = _mm(sv["a"], dx_out, "tn", out_dtype=BF16, name="ffn_down_dw", tm=1408)
    grads["ffn_w_down"] = dwd.reshape(N_DEV, dwd.shape[0] // N_DEV, dwd.shape[1])
    da = _mm(dx_out, p["ffn_w_down"], "nt", name="ffn_down_dx", tn=1408, tk=2048)
    du0, dww, dwb = _ffn_act_bwd(da, sv["u0"], p["ffn_dw_w"], _row(p["ffn_dw_b"]), "ffn_act_bwd")
    kw = dww.shape[1]
    grads["ffn_dw_w"] = _stack_cols(jnp.transpose(dww, (1, 0, 2)).reshape(kw, -1))
    grads["ffn_dw_b"] = dwb.reshape(1, -1)
    grads["ffn_w_up"] = _mm(sv["h2"], du0, "tn", out_dtype=BF16, out_stack=N_DEV, name="ffn_up_dw", tn=1408)
    dh2 = _mm(du0, p["ffn_w_up"], "nt", name="ffn_up_dx", tk=1408)
    dx_mid, dg, _ = _rms_bwd(dh2, x_mid, _row(p["ffn_norm_g"]), dx_out, "rms_bwd")
    grads["ffn_norm_g"] = dg
    return dx_mid


def _conf_forward(x, p):
    h = _rms_fwd(x, _row(p["norm_g"]), BF16, "rms_fwd_bf16")
    u = _mm(h, p["a_w_in"], "nn", bias=_row(p["a_b_in"]), name="conf_in", tn=512, tk=2048)
    cpre = _conf_conv_fwd(u, p["a_dw_w"], _row(p["a_dw_b"]), "conf_conv_fwd")
    s = _ln_silu_fwd(cpre, _row(p["a_ln_g"]), _row(p["a_ln_b"]), "ln_silu_fwd")
    x_mid = _mm(s, p["a_w_out"], "nn", bias=_row(p["a_b_out"]), res=x, name="conf_out", tk=2048)
    return x_mid, dict(h=h, u=u, cpre=cpre, s=s)


def _conf_backward(dx_mid, x, p, sv, grads):
    dwo = _mm(sv["s"], dx_mid, "tn", out_dtype=BF16, name="conf_out_dw")
    grads["a_w_out"] = dwo.reshape(N_DEV, dwo.shape[0] // N_DEV, dwo.shape[1])
    ds = _mm(dx_mid, p["a_w_out"], "nt", name="conf_out_dx", tk=2048)
    dc, dlg, dlb = _ln_silu_bwd(ds, sv["cpre"], _row(p["a_ln_g"]), _row(p["a_ln_b"]), "ln_silu_bwd")
    grads["a_ln_g"], grads["a_ln_b"] = dlg, dlb
    du, dww, dwb, dbin = _conf_conv_bwd(dc, sv["u"], p["a_dw_w"], "conf_conv_bwd")
    grads["a_dw_w"] = _stack_cols(dww)
    grads["a_dw_b"] = dwb
    grads["a_b_in"] = dbin.reshape(1, -1)
    grads["a_w_in"] = _mm(sv["h"], du, "tn", out_dtype=BF16, out_stack=N_DEV, name="conf_in_dw", tn=512)
    dh = _mm(du, p["a_w_in"], "nt", name="conf_in_dx", tk=512)
    dx, dg, dbo = _rms_bwd(dh, x, _row(p["norm_g"]), dx_mid, "rms_bwd")
    grads["norm_g"] = dg
    grads["a_b_out"] = dbo
    return dx


def _pool_forward(x, p):
    h = _rms_fwd(x, _row(p["norm_g"]), F32, "rms_fwd_f32")
    mixed = _pool_fwd(h, "pool_fwd")
    x_mid = _pool_mm_fwd(mixed, p["b_w_group"], _row(p["b_scale"]), x, "pool_mm_fwd")
    return x_mid, dict(mixed=mixed)


def _pool_backward(dx_mid, x, p, sv, grads):
    dmixed, dwg, dscale = _pool_mm_bwd(dx_mid, sv["mixed"], p["b_w_group"], _row(p["b_scale"]), "pool_mm_bwd")
    ng, gd, _ = dwg.shape
    grads["b_w_group"] = jnp.transpose(dwg.reshape(ng, N_DEV, gd // N_DEV, gd), (1, 0, 2, 3)).reshape(N_DEV, ng * gd // N_DEV, gd).astype(BF16)
    grads["b_scale"] = dscale
    dh = _pool_bwd(dmixed, "pool_bwd")
    dx, dg, _ = _rms_bwd(dh, x, _row(p["norm_g"]), dx_mid, "rms_bwd")
    grads["norm_g"] = dg
    return dx


def _attn_tables(p, positions, d_model):
    n_q = d_model // HEAD
    n_kv = n_q // 8
    tabs = _rope_tables(positions)
    gq2 = jnp.concatenate([p["c_q_norm_g"], p["c_q_norm_g"]]).reshape(1, LANES)
    gk2 = jnp.concatenate([p["c_k_norm_g"], p["c_k_norm_g"]]).reshape(1, LANES)
    sink_tab = jnp.repeat(jnp.repeat(p["c_sinks"].reshape(-1, 2), HEAD, axis=1), 8, axis=0)
    return n_q, n_kv, tabs, gq2, gk2, sink_tab


def _attn_forward(x, p, positions):
    n_q, n_kv, tabs, gq2, gk2, sink_tab = _attn_tables(p, positions, x.shape[1])
    h = _rms_fwd(x, _row(p["norm_g"]), BF16, "rms_fwd_bf16")
    qkv = _mm(h, p["c_w_qkv"], "nn", name="attn_qkv", tn=1280, tk=2048)
    q, k2, v2 = _qk_prep_fwd(qkv, tabs, gq2, gk2, n_q, n_kv, "qk_prep_fwd")
    o, lse = _attn_fwd(q, k2, v2, sink_tab, "attn_fwd")
    x_mid = _mm(o, p["c_w_o"], "nn", res=x, name="attn_out", tk=2048)
    return x_mid, dict(h=h, qkv=qkv, q=q, k2=k2, v2=v2, o=o, lse=lse)


def _attn_backward(dx_mid, x, p, positions, sv, grads):
    n_q, n_kv, tabs, gq2, gk2, sink_tab = _attn_tables(p, positions, x.shape[1])
    dwo = _mm(sv["o"], dx_mid, "tn", out_dtype=BF16, name="attn_out_dw")
    grads["c_w_o"] = dwo.reshape(N_DEV, dwo.shape[0] // N_DEV, dwo.shape[1])
    do = _mm(dx_mid, p["c_w_o"], "nt", name="attn_out_dx", tk=2048)
    dq, dkc, dkp, dvc, dvp, dsk = _attn_bwd(do, sv["q"], sv["o"], sv["lse"], sv["k2"], sv["v2"], sink_tab, "attn_bwd")
    nb = x.shape[0] // Q_BLOCK
    dsk = dsk.reshape(-1, nb, 8, LANES)[:, :, 0, :].sum(axis=1)
    grads["c_sinks"] = jnp.stack([dsk[:, 0], dsk[:, HEAD]], axis=1).reshape(1, -1)
    dqkv, dgq, dgk = _qk_prep_bwd(dq, dkc, dkp, dvc, dvp, sv["qkv"], tabs, gq2, gk2, n_q, n_kv, "qk_prep_bwd")
    grads["c_q_norm_g"] = dgq[:, :HEAD] + dgq[:, HEAD:]
    grads["c_k_norm_g"] = dgk[:, :HEAD] + dgk[:, HEAD:]
    dwq = _mm(sv["h"], dqkv, "tn", out_dtype=BF16, name="attn_qkv_dw", tn=1280)
    grads["c_w_qkv"] = _stack_cols(dwq)
    dh = _mm(dqkv, p["c_w_qkv"], "nt", name="attn_qkv_dx", tk=1280)
    dx, dg, _ = _rms_bwd(dh, x, _row(p["norm_g"]), dx_mid, "rms_bwd")
    grads["norm_g"] = dg
    return dx


class _LayerWeights:
    def __init__(self, li, weights, small_full, comm):
        self.li, self.weights, self.small_full, self.comm, self.cache = li, weights, small_full, comm, {}

    def __getitem__(self, nme):
        if nme not in self.cache:
            self.cache[nme] = self.fetch(nme)
        return self.cache[nme]

    def fetch(self, nme):
        full = "l%d_%s" % (self.li, nme)
        w = self.weights[full]
        if nme in _SHARDED_SMALL:
            return _unstack_cols(self.small_full[full], w.shape[0])
        if nme not in _BIG:
            return w
        got = self.comm.need(full)
        if nme in ("a_w_in", "ffn_w_up"):
            return got
        if nme == "c_w_qkv":
            return _unstack_cols(got, w.shape[0])
        if nme == "b_w_group":
            ng, gs, gd = w.shape
            return jnp.transpose(got.reshape(N_DEV, ng, gs, gd), (1, 0, 2, 3)).reshape(ng, N_DEV * gs, gd)
        return got.reshape(-1, w.shape[1])


class _LayerGrads(dict):
    def __init__(self, li, comm):
        super().__init__()
        self.li, self.comm = li, comm

    def __setitem__(self, nme, value):
        if nme in _BIG:
            self.comm.push("l%d_%s" % (self.li, nme), "scatter", value)
        else:
            super().__setitem__(nme, value)


def kernel(*args):
    n_w = len(_NAMES)
    x, positions = args[0], args[1]
    weights = dict(zip(_NAMES, args[2:2 + n_w]))
    loss_target = args[2 + n_w]
    moms = dict(zip(_NAMES, args[3 + n_w:3 + 2 * n_w]))
    vels = dict(zip(_NAMES, args[3 + 2 * n_w:3 + 3 * n_w]))
    x0 = x[0]
    pos = positions[0]
    kinds = ("conf", "pool", "attn", "conf")
    comm = _Comm()
    _STATE["comm"], _STATE["last"] = comm, None
    shd = [n for n in _NAMES if n.split("_", 1)[1] in _SHARDED_SMALL]
    small_full = dict(zip(shd, _all_gather([_pad_rows(weights[n]) for n in shd], "gather_small")))
    for n in _NAMES:
        if n.split("_", 1)[1] in _BIG:
            w = weights[n]
            comm.push(n, "gather1", w.astype(BF16).reshape(-1, w.shape[-1]))
    results = {}

    def update_ready():
        while comm.scattered:
            full, parts = comm.scattered.pop(0)
            w = weights[full]
            w2 = w.reshape(-1, w.shape[-1])
            outs = _sum_adam(parts, w2, moms[full].reshape(w2.shape), vels[full].reshape(w2.shape), "adam_" + full.split("_", 1)[1])
            results[full] = tuple(o.reshape(w.shape) for o in outs)

    params, saved = [], []
    cur = x0
    for li, names in enumerate(_LAYERS):
        comm.layer = li
        p = _LayerWeights(li, weights, small_full, comm)
        if kinds[li] == "conf":
            x_mid, sv = _conf_forward(cur, p)
        elif kinds[li] == "pool":
            x_mid, sv = _pool_forward(cur, p)
        else:
            x_mid, sv = _attn_forward(cur, p, pos)
        x_out, sv_f = _ffn_forward(x_mid, p, kinds[li])
        params.append(p)
        saved.append((sv, sv_f, cur, x_mid))
        cur = x_out
    dy, loss_part = _loss_head(cur, loss_target[0], "loss_head")
    loss = lax.psum(loss_part[0, 0], ("x", "y", "c"))

    small_grads = {}
    dcur = dy
    for li in range(len(_LAYERS) - 1, -1, -1):
        p = params[li]
        sv, sv_f, x_in, x_mid = saved[li]
        grads = _LayerGrads(li, comm)
        dmid = _ffn_backward(dcur, x_mid, p, sv_f, grads)
        update_ready()
        if kinds[li] == "conf":
            dcur = _conf_backward(dmid, x_in, p, sv, grads)
        elif kinds[li] == "pool":
            dcur = _pool_backward(dmid, x_in, p, sv, grads)
        else:
            dcur = _attn_backward(dmid, x_in, p, pos, sv, grads)
        update_ready()
        for n in _LAYERS[li]:
            if n not in _BIG:
                small_grads["l%d_%s" % (li, n)] = grads[n]
    rep = [n for n in _NAMES if n.split("_", 1)[1] not in _BIG and n.split("_", 1)[1] not in _SHARDED_SMALL]
    tot_rep, tot_sh = _small_exchange([small_grads[n] for n in rep], [small_grads[n] for n in shd], "small_exchange")
    while comm.queue or comm.scattered:
        if not comm.scattered:
            comm.flush(comm.take(1e9))
        update_ready()
    _STATE["comm"] = None
    rep_out, sh_out = _small_adam(tot_rep, tot_sh, [_row(weights[n]) for n in rep], [_row(moms[n]) for n in rep], [_row(vels[n]) for n in rep],
                                  [weights[n] for n in shd], [moms[n] for n in shd], [vels[n] for n in shd], "small_adam")
    for i, n in enumerate(rep):
        results[n] = tuple(rep_out[k][i].reshape(weights[n].shape) for k in range(4))
    for i, n in enumerate(shd):
        results[n] = tuple(sh_out[k][i] for k in range(4))

    _STATE["last"] = None
    grad_x = dcur[None]
    out = [loss, grad_x]
    for k in range(4):
        out += [results[n][k] for n in _NAMES]
    return tuple(out)
```

```python
import functools
import math

import jax
import jax.numpy as jnp
from jax import lax
from jax.experimental import pallas as pl
from jax.experimental.pallas import tpu as pltpu

F32 = jnp.float32
BF16 = jnp.bfloat16
N_DEV = 8
EPS = 1e-6
LANES = 128
HEAD = 64
Q_BLOCK = 128
ROT_DIM = 16
ROPE_THETA = 500000.0
POOL_WINDOWS = (2, 4, 8, 16)
HALO = 32
ROWS = 64
VMEM_LIMIT = 56 * 1024 * 1024
ADAM_LR, ADAM_B1, ADAM_B2, ADAM_EPS, ADAM_WD, ADAM_STEP = 0.001, 0.9, 0.999, 1e-08, 0.01, 10
MESH_ID = pl.DeviceIdType.MESH
MXU_FLOPS_PER_US = 7.5e8
HBM_BYTES_PER_US = 2.5e6
ATTN_US_PER_STEP = 0.85
CONV_FWD_US_PER_ELEM = 0.7e-5
CONV_BWD_US_PER_ELEM = 2.3e-5
ACT_FWD_US_PER_ELEM = 3.4e-6
SMALL_EXCHANGE_CARRY_US = 100.0


def _make_call(body, **kw):
    return pl.pallas_call(body, **kw)


_STATE = {"comm": None, "last": None}


def _raw_call(body, **kw):
    call = _make_call(body, **kw)

    def run(*args):
        last = _STATE["last"]
        if last is not None and args:
            first, _ = lax.optimization_barrier((args[0], last))
            args = (first,) + tuple(args[1:])
        res = call(*args)
        _STATE["last"] = res[0] if isinstance(res, (list, tuple)) else res
        return res
    return run


def _pcall(body, carry_us=0.0, **kw):
    comm = _STATE["comm"]
    jobs = comm.take(carry_us) if (comm is not None and carry_us > 0) else []
    if not jobs:
        return _raw_call(body, **kw)
    return _carry(body, jobs, comm, kw)


def _params(sem=None, **kw):
    if sem is not None:
        kw["dimension_semantics"] = sem
    return pltpu.CompilerParams(vmem_limit_bytes=VMEM_LIMIT, **kw)


def _pick(dim, pref, mult=LANES):
    best = None
    d = mult
    while d <= min(dim, pref):
        if dim % d == 0:
            best = d
        d += mult
    return dim if best is None else best


def _sigmoid(x):
    return 1.0 / (1.0 + jnp.exp(-x))


def _fold8(p):
    r, c = p.shape
    return p.reshape(r // 8, 8, c).sum(axis=0)


def _window(win_ref, e):
    win_ref[...] = e
    return win_ref


def _rows(win_ref, k, r):
    return win_ref[k:k + r, :]


def _lshape(a):
    return a.shape if a.ndim == 2 else (a.shape[1], a.shape[0] * a.shape[2])


def _panel(a):
    return a.shape[1] if a.ndim == 2 else a.shape[2]


def _lspec(a, br, bc, rc):
    if a.ndim == 2:
        return pl.BlockSpec((br, bc), rc)
    per = a.shape[2] // bc

    def idx(i, j, k):
        r, c = rc(i, j, k)
        return (c // per, r, c % per)
    return pl.BlockSpec((None, br, bc), idx)


def _mm(a, b, dims, *, name, out_dtype=F32, out_stack=None, bias=None, res=None, tm=1024, tn=1024, tk=1024):
    (ar, ac), (br_, bc_) = _lshape(a), _lshape(b)
    if dims == "nn":
        m, k, n = ar, ac, bc_
        lim_m, lim_k, lim_n = m, min(_panel(a), k), _panel(b)
    elif dims == "nt":
        m, k, n = ar, ac, br_
        lim_m, lim_k, lim_n = m, math.gcd(_panel(a), _panel(b)), n
    else:
        m, k, n = ac, ar, bc_
        lim_m, lim_k, lim_n = _panel(a), k, _panel(b)
    if out_stack is not None:
        lim_n = math.gcd(lim_n, n // out_stack)
    sub = 16 if (out_dtype == BF16 or a.dtype == BF16) else 8
    tm = _pick(lim_m, tm, LANES if dims == "tn" else sub)
    tn = _pick(lim_n, tn)
    tk = _pick(lim_k, tk, sub if dims == "tn" else LANES)
    nk = k // tk
    if dims == "tn":
        a_spec = _lspec(a, tk, tm, lambda i, j, kk: (kk, i))
    else:
        a_spec = _lspec(a, tm, tk, lambda i, j, kk: (i, kk))
    if dims == "nt":
        b_spec = _lspec(b, tn, tk, lambda i, j, kk: (j, kk))
    else:
        b_spec = _lspec(b, tk, tn, lambda i, j, kk: (kk, j))
    contract = {"nn": ((1,), (0,)), "nt": ((1,), (1,)), "tn": ((0,), (0,))}[dims]
    in_specs, args = [a_spec, b_spec], [a, b]
    if bias is not None:
        in_specs.append(pl.BlockSpec((1, tn), lambda i, j, kk: (0, j)))
        args.append(bias)
    if res is not None:
        in_specs.append(pl.BlockSpec((tm, tn), lambda i, j, kk: (i, j)))
        args.append(res)
    if out_stack is None:
        out_shape = jax.ShapeDtypeStruct((m, n), out_dtype)
    else:
        out_shape = jax.ShapeDtypeStruct((out_stack, m, n // out_stack), out_dtype)
    o_spec = _lspec(out_shape, tm, tn, lambda i, j, kk: (i, j))
    has_bias, has_res = bias is not None, res is not None

    def body(*refs):
        a_ref, b_ref = refs[0], refs[1]
        pos = 2
        bias_ref = res_ref = None
        if has_bias:
            bias_ref = refs[pos]
            pos += 1
        if has_res:
            res_ref = refs[pos]
            pos += 1
        o_ref = refs[pos]

        def part():
            return lax.dot_general(a_ref[...].astype(BF16), b_ref[...].astype(BF16), (contract, ((), ())),
                                   preferred_element_type=F32)

        def finish(r):
            if has_bias:
                r = r + bias_ref[...]
            if has_res:
                r = r + res_ref[...]
            o_ref[...] = r.astype(out_dtype)

        if nk == 1:
            finish(part())
        else:
            acc = refs[pos + 1]
            kk = pl.program_id(2)

            @pl.when(kk == 0)
            def _():
                acc[...] = part()

            @pl.when(kk > 0)
            def _():
                acc[...] += part()

            @pl.when(kk == nk - 1)
            def _():
                finish(acc[...])

    scratch = [] if nk == 1 else [pltpu.VMEM((tm, tn), F32)]
    return _pcall(body, carry_us=2.0 * m * n * k / MXU_FLOPS_PER_US, grid=(m // tm, n // tn, nk), in_specs=in_specs, out_specs=o_spec, out_shape=out_shape,
                  scratch_shapes=scratch, compiler_params=_params(("parallel", "parallel", "arbitrary")), name=name)(*args)


def _rms_fwd(x, g, out_dtype, name):
    t, d = x.shape
    tm = _pick(t, 512, 16)

    def body(x_ref, g_ref, o_ref):
        xv = x_ref[...]
        r = lax.rsqrt(jnp.mean(xv * xv, axis=-1, keepdims=True) + EPS)
        o_ref[...] = ((xv * r) * g_ref[...]).astype(out_dtype)

    return _pcall(body, grid=(t // tm,), in_specs=[pl.BlockSpec((tm, d), lambda i: (i, 0)), pl.BlockSpec((1, d), lambda i: (0, 0))],
                  out_specs=pl.BlockSpec((tm, d), lambda i: (i, 0)), out_shape=jax.ShapeDtypeStruct((t, d), out_dtype),
                  compiler_params=_params(("parallel",)), name=name)(x, g)


def _rms_bwd(dh, x, g, dres, name):
    t, d = x.shape
    tm = _pick(t, 256, 8)

    def body(dh_ref, x_ref, g_ref, dres_ref, dx_ref, dg_ref, cs_ref):
        xv, dhv, dr = x_ref[...], dh_ref[...], dres_ref[...]
        r = lax.rsqrt(jnp.mean(xv * xv, axis=-1, keepdims=True) + EPS)
        xh = xv * r
        dxh = dhv * g_ref[...]
        dx_ref[...] = dr + r * (dxh - xh * jnp.mean(dxh * xh, axis=-1, keepdims=True))
        pg = jnp.sum(dhv * xh, axis=0, keepdims=True)
        pc = jnp.sum(dr, axis=0, keepdims=True)

        @pl.when(pl.program_id(0) == 0)
        def _():
            dg_ref[...] = pg
            cs_ref[...] = pc

        @pl.when(pl.program_id(0) > 0)
        def _():
            dg_ref[...] += pg
            cs_ref[...] += pc

    row = pl.BlockSpec((tm, d), lambda i: (i, 0))
    vec = pl.BlockSpec((1, d), lambda i: (0, 0))
    return _pcall(body, carry_us=16.0 * t * d / HBM_BYTES_PER_US, grid=(t // tm,), in_specs=[row, row, vec, row], out_specs=[row, vec, vec],
                  out_shape=[jax.ShapeDtypeStruct((t, d), F32), jax.ShapeDtypeStruct((1, d), F32), jax.ShapeDtypeStruct((1, d), F32)],
                  compiler_params=_params(("arbitrary",)), name=name)(dh, x, g, dres)


def _loss_head(y, target, name):
    t, d = y.shape
    tm = _pick(t, 512, 8)

    def body(y_ref, t_ref, dy_ref, l_ref):
        e = y_ref[...] - t_ref[...]
        dy_ref[...] = e * (1.0 / d)
        part = 0.5 * jnp.sum(jnp.mean(e * e, axis=-1, keepdims=True), axis=0, keepdims=True)

        @pl.when(pl.program_id(0) == 0)
        def _():
            l_ref[...] = part

        @pl.when(pl.program_id(0) > 0)
        def _():
            l_ref[...] += part

    row = pl.BlockSpec((tm, d), lambda i: (i, 0))
    return _pcall(body, grid=(t // tm,), in_specs=[row, row], out_specs=[row, pl.BlockSpec((1, 1), lambda i: (0, 0))],
                  out_shape=[jax.ShapeDtypeStruct((t, d), F32), jax.ShapeDtypeStruct((1, 1), F32)],
                  compiler_params=_params(("arbitrary",)), name=name)(y, target)


def _ln_silu_fwd(c, g, b, name):
    t, d = c.shape
    tm = _pick(t, 512, 16)

    def body(c_ref, g_ref, b_ref, o_ref):
        cv = c_ref[...]
        xc = cv - jnp.mean(cv, axis=-1, keepdims=True)
        z = xc * lax.rsqrt(jnp.mean(xc * xc, axis=-1, keepdims=True) + EPS) * g_ref[...] + b_ref[...]
        o_ref[...] = (z * _sigmoid(z)).astype(BF16)

    row = pl.BlockSpec((tm, d), lambda i: (i, 0))
    vec = pl.BlockSpec((1, d), lambda i: (0, 0))
    return _pcall(body, grid=(t // tm,), in_specs=[row, vec, vec], out_specs=row, out_shape=jax.ShapeDtypeStruct((t, d), BF16),
                  compiler_params=_params(("parallel",)), name=name)(c, g, b)


def _ln_silu_bwd(ds, c, g, b, name):
    t, d = c.shape
    tm = _pick(t, 256, 8)

    def body(ds_ref, c_ref, g_ref, b_ref, dc_ref, dg_ref, db_ref):
        cv = c_ref[...]
        xc = cv - jnp.mean(cv, axis=-1, keepdims=True)
        r = lax.rsqrt(jnp.mean(xc * xc, axis=-1, keepdims=True) + EPS)
        ch = xc * r
        z = ch * g_ref[...] + b_ref[...]
        sg = _sigmoid(z)
        dz = ds_ref[...] * (sg * (1.0 + z * (1.0 - sg)))
        dch = dz * g_ref[...]
        dc_ref[...] = r * (dch - jnp.mean(dch, axis=-1, keepdims=True) - ch * jnp.mean(dch * ch, axis=-1, keepdims=True))
        pg = jnp.sum(dz * ch, axis=0, keepdims=True)
        pb = jnp.sum(dz, axis=0, keepdims=True)

        @pl.when(pl.program_id(0) == 0)
        def _():
            dg_ref[...] = pg
            db_ref[...] = pb

        @pl.when(pl.program_id(0) > 0)
        def _():
            dg_ref[...] += pg
            db_ref[...] += pb

    row = pl.BlockSpec((tm, d), lambda i: (i, 0))
    vec = pl.BlockSpec((1, d), lambda i: (0, 0))
    return _pcall(body, grid=(t // tm,), in_specs=[row, row, vec, vec], out_specs=[row, vec, vec],
                  out_shape=[jax.ShapeDtypeStruct((t, d), F32), jax.ShapeDtypeStruct((1, d), F32), jax.ShapeDtypeStruct((1, d), F32)],
                  compiler_params=_params(("arbitrary",)), name=name)(ds, c, g, b)


def _steps(t):
    return t // ROWS


def _conf_conv_fwd(u, dw_w, dw_b, name):
    t, d2 = u.shape
    d = d2 // 2
    c = LANES
    ns = d // c
    kc = dw_w.shape[0]

    def body(a_ref, g_ref, w_ref, b_ref, o_ref, pad, win):
        pad[0:HALO, :] = jnp.zeros((HALO, c), F32)

        def glu(i, _):
            base = pl.multiple_of(i * ROWS, ROWS)
            pad[pl.ds(base + HALO, ROWS), :] = a_ref[pl.ds(base, ROWS), :] * _sigmoid(g_ref[pl.ds(base, ROWS), :])
            return 0
        lax.fori_loop(0, _steps(t), glu, 0)

        def conv(i, _):
            base = pl.multiple_of(i * ROWS, ROWS)
            e = _window(win, pad[pl.ds(base, ROWS + HALO), :])
            acc = jnp.zeros((ROWS, c), F32) + b_ref[...]
            for j in range(kc):
                acc = acc + w_ref[j:j + 1, :] * _rows(e, HALO - (kc - 1) + j, ROWS)
            o_ref[pl.ds(base, ROWS), :] = acc
            return 0
        lax.fori_loop(0, _steps(t), conv, 0)

    return _pcall(body, carry_us=CONV_FWD_US_PER_ELEM * t * d, grid=(ns,),
                  in_specs=[pl.BlockSpec((t, c), lambda s: (0, s)), pl.BlockSpec((t, c), lambda s: (0, s + ns)),
                            pl.BlockSpec((kc, c), lambda s: (0, s)), pl.BlockSpec((1, c), lambda s: (0, s))],
                  out_specs=pl.BlockSpec((t, c), lambda s: (0, s)), out_shape=jax.ShapeDtypeStruct((t, d), F32),
                  scratch_shapes=[pltpu.VMEM((t + HALO, c), F32), pltpu.VMEM((ROWS + HALO, c), F32)],
                  compiler_params=_params(("parallel",)), name=name)(u, u, dw_w, dw_b)


def _conf_conv_bwd(dc, u, dw_w, name):
    t, d = dc.shape
    c = LANES
    ns = d // c
    kc = dw_w.shape[0]

    def body(dc_ref, a_ref, g_ref, w_ref, du_ref, dww_ref, dwb_ref, db_ref, padv, padd, accw, accb, winv, wind):
        padv[0:HALO, :] = jnp.zeros((HALO, c), F32)
        padd[t:t + HALO, :] = jnp.zeros((HALO, c), F32)
        accw[...] = jnp.zeros_like(accw)
        accb[...] = jnp.zeros_like(accb)

        def fill(i, _):
            base = pl.multiple_of(i * ROWS, ROWS)
            padv[pl.ds(base + HALO, ROWS), :] = a_ref[pl.ds(base, ROWS), :] * _sigmoid(g_ref[pl.ds(base, ROWS), :])
            padd[pl.ds(base, ROWS), :] = dc_ref[pl.ds(base, ROWS), :]
            return 0
        lax.fori_loop(0, _steps(t), fill, 0)

        def step(i, _):
            base = pl.multiple_of(i * ROWS, ROWS)
            ev = _window(winv, padv[pl.ds(base, ROWS + HALO), :])
            ed = _window(wind, padd[pl.ds(base, ROWS + HALO), :])
            dcc = _rows(ed, 0, ROWS)
            dv = jnp.zeros((ROWS, c), F32)
            for j in range(kc):
                dv = dv + w_ref[j:j + 1, :] * _rows(ed, kc - 1 - j, ROWS)
                accw[j] = accw[j] + _fold8(dcc * _rows(ev, HALO - (kc - 1) + j, ROWS))
            accb[0] = accb[0] + _fold8(dcc)
            av = a_ref[pl.ds(base, ROWS), :]
            sg = _sigmoid(g_ref[pl.ds(base, ROWS), :])
            da = dv * sg
            dg = dv * av * sg * (1.0 - sg)
            du_ref[0, pl.ds(base, ROWS), :] = da.astype(BF16)
            du_ref[1, pl.ds(base, ROWS), :] = dg.astype(BF16)
            accb[1] = accb[1] + _fold8(da)
            accb[2] = accb[2] + _fold8(dg)
            return 0
        lax.fori_loop(0, _steps(t), step, 0)
        for j in range(kc):
            dww_ref[j:j + 1, :] = jnp.sum(accw[j], axis=0, keepdims=True)
        dwb_ref[...] = jnp.sum(accb[0], axis=0, keepdims=True)
        db_ref[0] = jnp.sum(accb[1], axis=0, keepdims=True)
        db_ref[1] = jnp.sum(accb[2], axis=0, keepdims=True)

    return _pcall(body, carry_us=CONV_BWD_US_PER_ELEM * t * d, grid=(ns,),
                  in_specs=[pl.BlockSpec((t, c), lambda s: (0, s)), pl.BlockSpec((t, c), lambda s: (0, s)),
                            pl.BlockSpec((t, c), lambda s: (0, s + ns)), pl.BlockSpec((kc, c), lambda s: (0, s))],
                  out_specs=[pl.BlockSpec((2, t, c), lambda s: (0, 0, s)), pl.BlockSpec((kc, c), lambda s: (0, s)),
                             pl.BlockSpec((1, c), lambda s: (0, s)), pl.BlockSpec((2, 1, c), lambda s: (0, 0, s))],
                  out_shape=[jax.ShapeDtypeStruct((2, t, d), BF16), jax.ShapeDtypeStruct((kc, d), F32),
                             jax.ShapeDtypeStruct((1, d), F32), jax.ShapeDtypeStruct((2, 1, d), F32)],
                  scratch_shapes=[pltpu.VMEM((t + HALO, c), F32), pltpu.VMEM((t + HALO, c), F32),
                                  pltpu.VMEM((kc, 8, c), F32), pltpu.VMEM((3, 8, c), F32),
                                  pltpu.VMEM((ROWS + HALO, c), F32), pltpu.VMEM((ROWS + HALO, c), F32)],
                  compiler_params=_params(("parallel",)), name=name)(dc, u, u, dw_w)


def _ffn_act_fwd(u0, dw_w, dw_b, name):
    t, f2 = u0.shape
    f = f2 // 2
    c = LANES
    ns = f // c
    kw = dw_w.shape[0]

    def body(g_ref, v_ref, wg_ref, wv_ref, bg_ref, bv_ref, o_ref, wing, winv):
        def step(i, _):
            base = pl.multiple_of(i * ROWS, ROWS)
            lo = pl.multiple_of(jnp.maximum(base - HALO, 0), HALO)
            keep = jnp.where(i > 0, 1.0, 0.0)
            eg = _window(wing, jnp.concatenate([g_ref[pl.ds(lo, HALO), :] * keep, g_ref[pl.ds(base, ROWS), :]], axis=0))
            ev = _window(winv, jnp.concatenate([v_ref[pl.ds(lo, HALO), :] * keep, v_ref[pl.ds(base, ROWS), :]], axis=0))
            gate = jnp.zeros((ROWS, c), F32) + bg_ref[...]
            val = jnp.zeros((ROWS, c), F32) + bv_ref[...]
            for j in range(kw):
                gate = gate + wg_ref[j:j + 1, :] * _rows(eg, HALO - (kw - 1) + j, ROWS)
                val = val + wv_ref[j:j + 1, :] * _rows(ev, HALO - (kw - 1) + j, ROWS)
            o_ref[pl.ds(base, ROWS), :] = (gate * _sigmoid(gate) * val).astype(BF16)
            return 0
        lax.fori_loop(0, _steps(t), step, 0)

    return _pcall(body, carry_us=ACT_FWD_US_PER_ELEM * t * f, grid=(ns,),
                  in_specs=[pl.BlockSpec((t, c), lambda s: (0, s)), pl.BlockSpec((t, c), lambda s: (0, s + ns)),
                            pl.BlockSpec((kw, c), lambda s: (0, s)), pl.BlockSpec((kw, c), lambda s: (0, s + ns)),
                            pl.BlockSpec((1, c), lambda s: (0, s)), pl.BlockSpec((1, c), lambda s: (0, s + ns))],
                  out_specs=pl.BlockSpec((t, c), lambda s: (0, s)), out_shape=jax.ShapeDtypeStruct((t, f), BF16),
                  scratch_shapes=[pltpu.VMEM((ROWS + HALO, c), F32), pltpu.VMEM((ROWS + HALO, c), F32)],
                  compiler_params=_params(("parallel",)), name=name)(u0, u0, dw_w, dw_w, dw_b, dw_b)


def _ffn_act_bwd(da, u0, dw_w, dw_b, name):
    t, f = da.shape
    c = LANES
    ns = f // c
    kw = dw_w.shape[0]

    def body(da_ref, g_ref, v_ref, wg_ref, wv_ref, bg_ref, bv_ref, du_ref, dww_ref, dwb_ref, padg, padv, accw, accb, wing, winv):
        padg[t:t + HALO, :] = jnp.zeros((HALO, c), F32)
        padv[t:t + HALO, :] = jnp.zeros((HALO, c), F32)
        accw[...] = jnp.zeros_like(accw)
        accb[...] = jnp.zeros_like(accb)

        def first(i, _):
            base = pl.multiple_of(i * ROWS, ROWS)
            lo = pl.multiple_of(jnp.maximum(base - HALO, 0), HALO)
            keep = jnp.where(i > 0, 1.0, 0.0)
            eg = _window(wing, jnp.concatenate([g_ref[pl.ds(lo, HALO), :] * keep, g_ref[pl.ds(base, ROWS), :]], axis=0))
            ev = _window(winv, jnp.concatenate([v_ref[pl.ds(lo, HALO), :] * keep, v_ref[pl.ds(base, ROWS), :]], axis=0))
            gate = jnp.zeros((ROWS, c), F32) + bg_ref[...]
            val = jnp.zeros((ROWS, c), F32) + bv_ref[...]
            for j in range(kw):
                gate = gate + wg_ref[j:j + 1, :] * _rows(eg, HALO - (kw - 1) + j, ROWS)
                val = val + wv_ref[j:j + 1, :] * _rows(ev, HALO - (kw - 1) + j, ROWS)
            dav = da_ref[pl.ds(base, ROWS), :]
            sg = _sigmoid(gate)
            dgate = dav * val * (sg * (1.0 + gate * (1.0 - sg)))
            dval = dav * (gate * sg)
            padg[pl.ds(base, ROWS), :] = dgate
            padv[pl.ds(base, ROWS), :] = dval
            for j in range(kw):
                accw[j] = accw[j] + _fold8(dgate * _rows(eg, HALO - (kw - 1) + j, ROWS))
                accw[kw + j] = accw[kw + j] + _fold8(dval * _rows(ev, HALO - (kw - 1) + j, ROWS))
            accb[0] = accb[0] + _fold8(dgate)
            accb[1] = accb[1] + _fold8(dval)
            return 0
        lax.fori_loop(0, _steps(t), first, 0)

        def second(i, _):
            base = pl.multiple_of(i * ROWS, ROWS)
            eg = _window(wing, padg[pl.ds(base, ROWS + HALO), :])
            ev = _window(winv, padv[pl.ds(base, ROWS + HALO), :])
            dg = jnp.zeros((ROWS, c), F32)
            dv = jnp.zeros((ROWS, c), F32)
            for j in range(kw):
                dg = dg + wg_ref[j:j + 1, :] * _rows(eg, kw - 1 - j, ROWS)
                dv = dv + wv_ref[j:j + 1, :] * _rows(ev, kw - 1 - j, ROWS)
            du_ref[0, pl.ds(base, ROWS), :] = dg.astype(BF16)
            du_ref[1, pl.ds(base, ROWS), :] = dv.astype(BF16)
            return 0
        lax.fori_loop(0, _steps(t), second, 0)
        for j in range(kw):
            dww_ref[0, j:j + 1, :] = jnp.sum(accw[j], axis=0, keepdims=True)
            dww_ref[1, j:j + 1, :] = jnp.sum(accw[kw + j], axis=0, keepdims=True)
        dwb_ref[0] = jnp.sum(accb[0], axis=0, keepdims=True)
        dwb_ref[1] = jnp.sum(accb[1], axis=0, keepdims=True)

    return _pcall(body, grid=(ns,),
                  in_specs=[pl.BlockSpec((t, c), lambda s: (0, s)),
                            pl.BlockSpec((t, c), lambda s: (0, s)), pl.BlockSpec((t, c), lambda s: (0, s + ns)),
                            pl.BlockSpec((kw, c), lambda s: (0, s)), pl.BlockSpec((kw, c), lambda s: (0, s + ns)),
                            pl.BlockSpec((1, c), lambda s: (0, s)), pl.BlockSpec((1, c), lambda s: (0, s + ns))],
                  out_specs=[pl.BlockSpec((2, t, c), lambda s: (0, 0, s)), pl.BlockSpec((2, kw, c), lambda s: (0, 0, s)),
                             pl.BlockSpec((2, 1, c), lambda s: (0, 0, s))],
                  out_shape=[jax.ShapeDtypeStruct((2, t, f), BF16), jax.ShapeDtypeStruct((2, kw, f), F32),
                             jax.ShapeDtypeStruct((2, 1, f), F32)],
                  scratch_shapes=[pltpu.VMEM((t + HALO, c), F32), pltpu.VMEM((t + HALO, c), F32),
                                  pltpu.VMEM((2 * kw, 8, c), F32), pltpu.VMEM((2, 8, c), F32),
                                  pltpu.VMEM((ROWS + HALO, c), F32), pltpu.VMEM((ROWS + HALO, c), F32)],
                  compiler_params=_params(("parallel",)), name=name)(da, u0, u0, dw_w, dw_w, dw_b, dw_b)


def _window_of(group):
    w = jnp.float32(POOL_WINDOWS[-1])
    for k in range(len(POOL_WINDOWS) - 2, -1, -1):
        w = jnp.where(group == k, jnp.float32(POOL_WINDOWS[k]), w)
    return w


def _select_level(group, levels):
    out = levels[-1]
    for k in range(len(levels) - 2, -1, -1):
        out = jnp.where(group == k, levels[k], out)
    return out


def _pool_fwd(h, name):
    t, d = h.shape
    c = LANES
    per = d // len(POOL_WINDOWS) // c

    def body(h_ref, o_ref):
        group = pl.program_id(0)
        wf = _window_of(group)

        def step(i, _):
            base = pl.multiple_of(i * ROWS, ROWS)
            lo = pl.multiple_of(jnp.maximum(base - HALO, 0), HALO)
            keep = jnp.where(i > 0, 1.0, 0.0)
            cur = h_ref[pl.ds(base, ROWS), :]
            e = jnp.concatenate([h_ref[pl.ds(lo, HALO), :] * keep, cur], axis=0)
            n = ROWS + HALO
            levels = []
            s = e
            for k in range(len(POOL_WINDOWS)):
                s = s + pltpu.roll(s, 1 << k, 0)
                levels.append(s[HALO:n])
            tpos = (base + lax.broadcasted_iota(jnp.int32, (ROWS, c), 0) + 1).astype(F32)
            pooled = _select_level(group, levels) / jnp.minimum(tpos, wf)
            o_ref[pl.ds(base, ROWS), :] = (pooled - cur).astype(BF16)
            return 0
        lax.fori_loop(0, _steps(t), step, 0)

    return _pcall(body, grid=(len(POOL_WINDOWS), per), in_specs=[pl.BlockSpec((t, c), lambda g, s: (0, g * per + s))],
                  out_specs=pl.BlockSpec((t, c), lambda g, s: (0, g * per + s)), out_shape=jax.ShapeDtypeStruct((t, d), BF16),
                  compiler_params=_params(("parallel", "parallel")), name=name)(h)


def _pool_bwd(dm, name):
    t, d = dm.shape
    c = LANES
    per = d // len(POOL_WINDOWS) // c

    def body(dm_ref, o_ref, pad):
        group = pl.program_id(0)
        wf = _window_of(group)
        pad[t:t + HALO, :] = jnp.zeros((HALO, c), F32)

        def fill(i, _):
            base = pl.multiple_of(i * ROWS, ROWS)
            tpos = (base + lax.broadcasted_iota(jnp.int32, (ROWS, c), 0) + 1).astype(F32)
            pad[pl.ds(base, ROWS), :] = dm_ref[pl.ds(base, ROWS), :] / jnp.minimum(tpos, wf)
            return 0
        lax.fori_loop(0, _steps(t), fill, 0)

        def step(i, _):
            base = pl.multiple_of(i * ROWS, ROWS)
            n = ROWS + HALO
            s = pad[pl.ds(base, n), :]
            levels = []
            for k in range(len(POOL_WINDOWS)):
                s = s + pltpu.roll(s, n - (1 << k), 0)
                levels.append(s[0:ROWS])
            o_ref[pl.ds(base, ROWS), :] = _select_level(group, levels) - dm_ref[pl.ds(base, ROWS), :]
            return 0
        lax.fori_loop(0, _steps(t), step, 0)

    return _pcall(body, grid=(len(POOL_WINDOWS), per), in_specs=[pl.BlockSpec((t, c), lambda g, s: (0, g * per + s))],
                  out_specs=pl.BlockSpec((t, c), lambda g, s: (0, g * per + s)), out_shape=jax.ShapeDtypeStruct((t, d), F32),
                  scratch_shapes=[pltpu.VMEM((t + HALO, c), F32)], compiler_params=_params(("parallel", "parallel")), name=name)(dm)


def _pool_mm_fwd(mixed, wg, scale, res, name):
    t, d = mixed.shape
    ng, gd, _ = wg.shape
    tm = _pick(t, 1024, 16)

    def body(a_ref, w_ref, s_ref, r_ref, o_ref):
        y = jnp.dot(a_ref[...], w_ref[...], preferred_element_type=F32)
        o_ref[...] = r_ref[...] + y * s_ref[...]

    blk = pl.BlockSpec((tm, gd), lambda g, i: (i, g))
    return _pcall(body, grid=(ng, t // tm),
                  in_specs=[blk, pl.BlockSpec((None, gd, gd), lambda g, i: (g, 0, 0)), pl.BlockSpec((1, gd), lambda g, i: (0, g)), blk],
                  out_specs=blk, out_shape=jax.ShapeDtypeStruct((t, d), F32),
                  compiler_params=_params(("parallel", "parallel")), name=name)(mixed, wg, scale, res)


def _pool_mm_bwd(dy, mixed, wg, scale, name):
    t, d = mixed.shape
    ng, gd, _ = wg.shape
    tm = _pick(t, 1024, 16)

    def body(dy_ref, a_ref, w_ref, s_ref, dm_ref, dw_ref, ds_ref):
        a, w, dyv = a_ref[...], w_ref[...], dy_ref[...]
        y = jnp.dot(a, w, preferred_element_type=F32)
        dyp = (dyv * s_ref[...]).astype(BF16)
        dm_ref[...] = lax.dot_general(dyp, w, (((1,), (1,)), ((), ())), preferred_element_type=F32)
        pw = lax.dot_general(a, dyp, (((0,), (0,)), ((), ())), preferred_element_type=F32)
        ps = jnp.sum(dyv * y, axis=0, keepdims=True)

        @pl.when(pl.program_id(1) == 0)
        def _():
            dw_ref[...] = pw
            ds_ref[...] = ps

        @pl.when(pl.program_id(1) > 0)
        def _():
            dw_ref[...] += pw
            ds_ref[...] += ps

    blk = pl.BlockSpec((tm, gd), lambda g, i: (i, g))
    wsp = pl.BlockSpec((None, gd, gd), lambda g, i: (g, 0, 0))
    vec = pl.BlockSpec((1, gd), lambda g, i: (0, g))
    return _pcall(body, grid=(ng, t // tm), in_specs=[blk, blk, wsp, vec], out_specs=[blk, wsp, vec],
                  out_shape=[jax.ShapeDtypeStruct((t, d), F32), jax.ShapeDtypeStruct((ng, gd, gd), F32), jax.ShapeDtypeStruct((1, d), F32)],
                  compiler_params=_params(("parallel", "arbitrary")), name=name)(dy, mixed, wg, scale)


def _rope_tables(positions):
    half = ROT_DIM // 2
    inv_freq = ROPE_THETA ** (-jnp.arange(0, ROT_DIM, 2, dtype=F32) / ROT_DIM)
    ang = positions.astype(F32)[:, None] * inv_freq
    cos, sin = jnp.cos(ang), jnp.sin(ang)
    t = positions.shape[0]
    ones = jnp.ones((t, HEAD - ROT_DIM), F32)
    zeros = jnp.zeros((t, HEAD - ROT_DIM), F32)
    zh = jnp.zeros((t, half), F32)
    c = jnp.concatenate([cos, cos, ones], axis=1)
    s1 = jnp.concatenate([-sin, zh, zeros], axis=1)
    s2 = jnp.concatenate([zh, sin, zeros], axis=1)
    return tuple(jnp.concatenate([a, a], axis=1) for a in (c, s1, s2))


def _half_mean(v, lo):
    s_lo = jnp.sum(jnp.where(lo, v, 0.0), axis=-1, keepdims=True)
    s_hi = jnp.sum(jnp.where(lo, 0.0, v), axis=-1, keepdims=True)
    return jnp.where(lo, s_lo, s_hi) * (1.0 / HEAD)


def _qk_prep_fwd(qkv, tabs, gq2, gk2, n_q, n_kv, name):
    t, width = qkv.shape
    tm = _pick(t, 256, 16)
    nqc, nkc = n_q * HEAD // LANES, n_kv * HEAD // LANES

    def body(x_ref, c_ref, s1_ref, s2_ref, gq_ref, gk_ref, q_ref, k2_ref, v2_ref):
        lo = lax.broadcasted_iota(jnp.int32, (tm, LANES), 1) < HEAD
        cv, s1, s2 = c_ref[...], s1_ref[...], s2_ref[...]

        def normrot(xc, g2):
            y = xc * lax.rsqrt(_half_mean(xc * xc, lo) + EPS) * g2
            return y * cv + pltpu.roll(y, LANES - ROT_DIM // 2, 1) * s1 + pltpu.roll(y, ROT_DIM // 2, 1) * s2

        def twice(y, j):
            sw = pltpu.roll(y, HEAD, 1)
            k2 = jnp.where(lo, y, sw) if j == 0 else jnp.where(lo, sw, y)
            return k2.astype(BF16)

        for ch in range(nqc):
            q_ref[:, ch * LANES:(ch + 1) * LANES] = normrot(x_ref[:, ch * LANES:(ch + 1) * LANES], gq_ref[...]).astype(BF16)
        for ch in range(nkc):
            off = (nqc + ch) * LANES
            y = normrot(x_ref[:, off:off + LANES], gk_ref[...])
            voff = (nqc + nkc + ch) * LANES
            vv = x_ref[:, voff:voff + LANES]
            for j in range(2):
                k2_ref[:, (2 * ch + j) * LANES:(2 * ch + j + 1) * LANES] = twice(y, j)
                v2_ref[:, (2 * ch + j) * LANES:(2 * ch + j + 1) * LANES] = twice(vv, j)

    row = lambda w: pl.BlockSpec((tm, w), lambda i: (i, 0))
    vec = pl.BlockSpec((1, LANES), lambda i: (0, 0))
    return _pcall(body, grid=(t // tm,), in_specs=[row(width), row(LANES), row(LANES), row(LANES), vec, vec],
                  out_specs=[row(n_q * HEAD), row(n_kv * LANES), row(n_kv * LANES)],
                  out_shape=[jax.ShapeDtypeStruct((t, n_q * HEAD), BF16), jax.ShapeDtypeStruct((t, n_kv * LANES), BF16),
                             jax.ShapeDtypeStruct((t, n_kv * LANES), BF16)],
                  compiler_params=_params(("parallel",)), name=name)(qkv, *tabs, gq2, gk2)


def _qk_prep_bwd(dq, dk_cur, dk_prev, dv_cur, dv_prev, qkv, tabs, gq2, gk2, n_q, n_kv, name):
    t, width = qkv.shape
    tm = Q_BLOCK
    nb = t // tm
    nqc, nkc = n_q * HEAD // LANES, n_kv * HEAD // LANES

    def body(dq_ref, kc_ref, kp_ref, vc_ref, vp_ref, x_ref, c_ref, s1_ref, s2_ref, gq_ref, gk_ref, o_ref, dgq_ref, dgk_ref):
        lo = lax.broadcasted_iota(jnp.int32, (tm, LANES), 1) < HEAD
        cv, s1, s2 = c_ref[...], s1_ref[...], s2_ref[...]
        more = jnp.where(pl.program_id(0) < nb - 1, 1.0, 0.0)

        def back(dy, xc, g2):
            dyn = dy * cv + pltpu.roll(dy * s1, ROT_DIM // 2, 1) + pltpu.roll(dy * s2, LANES - ROT_DIM // 2, 1)
            r = lax.rsqrt(_half_mean(xc * xc, lo) + EPS)
            xh = xc * r
            dxh = dyn * g2
            return r * (dxh - xh * _half_mean(dxh * xh, lo)), jnp.sum(dyn * xh, axis=0, keepdims=True)

        def unfold(cur_ref, prev_ref, ch):
            d0 = cur_ref[:, (2 * ch) * LANES:(2 * ch + 1) * LANES] + more * prev_ref[:, (2 * ch) * LANES:(2 * ch + 1) * LANES]
            d1 = cur_ref[:, (2 * ch + 1) * LANES:(2 * ch + 2) * LANES] + more * prev_ref[:, (2 * ch + 1) * LANES:(2 * ch + 2) * LANES]
            return jnp.where(lo, d0 + pltpu.roll(d0, HEAD, 1), d1 + pltpu.roll(d1, HEAD, 1))

        pq = jnp.zeros((1, LANES), F32)
        for ch in range(nqc):
            sl = slice(ch * LANES, (ch + 1) * LANES)
            dx, pg = back(dq_ref[:, sl], x_ref[:, sl], gq_ref[...])
            o_ref[:, sl] = dx.astype(BF16)
            pq = pq + pg
        pk = jnp.zeros((1, LANES), F32)
        for ch in range(nkc):
            sl = slice((nqc + ch) * LANES, (nqc + ch + 1) * LANES)
            dx, pg = back(unfold(kc_ref, kp_ref, ch), x_ref[:, sl], gk_ref[...])
            o_ref[:, sl] = dx.astype(BF16)
            pk = pk + pg
            vs = slice((nqc + nkc + ch) * LANES, (nqc + nkc + ch + 1) * LANES)
            o_ref[:, vs] = unfold(vc_ref, vp_ref, ch).astype(BF16)

        @pl.when(pl.program_id(0) == 0)
        def _():
            dgq_ref[...] = pq
            dgk_ref[...] = pk

        @pl.when(pl.program_id(0) > 0)
        def _():
            dgq_ref[...] += pq
            dgk_ref[...] += pk

    row = lambda w: pl.BlockSpec((tm, w), lambda i: (i, 0))
    nxt = lambda w: pl.BlockSpec((tm, w), lambda i: (jnp.minimum(i + 1, nb - 1), 0))
    vec = pl.BlockSpec((1, LANES), lambda i: (0, 0))
    kvw = n_kv * LANES
    return _pcall(body, grid=(nb,),
                  in_specs=[row(n_q * HEAD), row(kvw), nxt(kvw), row(kvw), nxt(kvw), row(width), row(LANES), row(LANES), row(LANES), vec, vec],
                  out_specs=[row(width), vec, vec],
                  out_shape=[jax.ShapeDtypeStruct((t, width), BF16), jax.ShapeDtypeStruct((1, LANES), F32), jax.ShapeDtypeStruct((1, LANES), F32)],
                  compiler_params=_params(("arbitrary",)), name=name)(dq, dk_cur, dk_prev, dv_cur, dv_prev, qkv, *tabs, gq2, gk2)


def _band_scores(qh, kc, kp, n, sink_row, lo_row, is_lo):
    scale = 1.0 / math.sqrt(HEAD)
    nt = (((1,), (1,)), ((), ()))
    s_c = lax.dot_general(qh, kc, nt, preferred_element_type=F32) * scale
    s_p = lax.dot_general(qh, kp, nt, preferred_element_type=F32) * scale
    qi = lax.broadcasted_iota(jnp.int32, (Q_BLOCK, Q_BLOCK), 0)
    kj = lax.broadcasted_iota(jnp.int32, (Q_BLOCK, Q_BLOCK), 1)
    s_c = jnp.where(kj <= qi, s_c, -jnp.inf)
    s_p = jnp.where((kj > qi) & (n > 0), s_p, -jnp.inf)
    pick = lo_row if is_lo else jnp.logical_not(lo_row)
    sink = jnp.max(jnp.where(pick, sink_row, -jnp.inf), axis=-1, keepdims=True)
    return s_c, s_p, sink


def _attn_fwd(q, k2, v2, sink_tab, name):
    t, dq = q.shape
    nc = dq // LANES
    nb = t // Q_BLOCK
    per_kv = nc // (k2.shape[1] // LANES)

    def body(q_ref, kc_ref, kp_ref, vc_ref, vp_ref, s_ref, o_ref, lse_ref):
        n = pl.program_id(1)
        lo = lax.broadcasted_iota(jnp.int32, (Q_BLOCK, LANES), 1) < HEAD
        lo_row = lax.broadcasted_iota(jnp.int32, (1, LANES), 1) < HEAD
        qv = q_ref[...].astype(F32)
        kc, kp, vc, vp = kc_ref[...], kp_ref[...], vc_ref[...], vp_ref[...]
        outs, lses = [], []
        for is_lo in (True, False):
            qh = jnp.where(lo, qv, 0.0) if is_lo else jnp.where(lo, 0.0, qv)
            s_c, s_p, sink = _band_scores(qh.astype(BF16), kc, kp, n, s_ref[0:1, :], lo_row, is_lo)
            m = jnp.maximum(jnp.maximum(jnp.max(s_c, axis=-1, keepdims=True), jnp.max(s_p, axis=-1, keepdims=True)), sink)
            p_c, p_p = jnp.exp(s_c - m), jnp.exp(s_p - m)
            denom = jnp.sum(p_c, axis=-1, keepdims=True) + jnp.sum(p_p, axis=-1, keepdims=True) + jnp.exp(sink - m)
            pv = jnp.dot(p_c.astype(BF16), vc, preferred_element_type=F32) + jnp.dot(p_p.astype(BF16), vp, preferred_element_type=F32)
            outs.append(pv / denom)
            lses.append(m + jnp.log(denom))
        o_ref[...] = jnp.where(lo, outs[0], outs[1]).astype(BF16)
        lse_ref[...] = jnp.where(lo, lses[0], lses[1])

    qs = pl.BlockSpec((Q_BLOCK, LANES), lambda c, n: (n, c))
    cur = pl.BlockSpec((Q_BLOCK, LANES), lambda c, n: (n, c // per_kv))
    prev = pl.BlockSpec((Q_BLOCK, LANES), lambda c, n: (jnp.maximum(n - 1, 0), c // per_kv))
    return _pcall(body, carry_us=ATTN_US_PER_STEP * nc * nb, grid=(nc, nb),
                  in_specs=[qs, cur, prev, cur, prev, pl.BlockSpec((8, LANES), lambda c, n: (c, 0))],
                  out_specs=[qs, pl.BlockSpec((None, Q_BLOCK, LANES), lambda c, n: (c, n, 0))],
                  out_shape=[jax.ShapeDtypeStruct((t, dq), BF16), jax.ShapeDtypeStruct((nc, t, LANES), F32)],
                  compiler_params=_params(("parallel", "parallel")), name=name)(q, k2, k2, v2, v2, sink_tab)


def _attn_bwd(do, q, o, lse, k2, v2, sink_tab, name):
    t, dq = q.shape
    nc = dq // LANES
    nb = t // Q_BLOCK
    nkv = k2.shape[1] // LANES
    per_kv = nc // nkv
    scale = 1.0 / math.sqrt(HEAD)
    tn_ = (((0,), (0,)), ((), ()))
    nt = (((1,), (1,)), ((), ()))

    def body(do_ref, q_ref, o_ref, lse_ref, kc_ref, kp_ref, vc_ref, vp_ref, s_ref,
             dq_ref, dkc_ref, dkp_ref, dvc_ref, dvp_ref, dsk_ref):
        n = pl.program_id(1)
        cc = pl.program_id(2)
        lo = lax.broadcasted_iota(jnp.int32, (Q_BLOCK, LANES), 1) < HEAD
        lo_row = lax.broadcasted_iota(jnp.int32, (1, LANES), 1) < HEAD
        qv, dov, ov, lsev = q_ref[...].astype(F32), do_ref[...], o_ref[...].astype(F32), lse_ref[...]
        kc, kp, vc, vp = kc_ref[...], kp_ref[...], vc_ref[...], vp_ref[...]
        dqs, dsinks = [], []
        dkc = dkp = dvc = dvp = None
        for is_lo in (True, False):
            half = lo if is_lo else jnp.logical_not(lo)
            qh = jnp.where(half, qv, 0.0).astype(BF16)
            doh = jnp.where(half, dov, 0.0)
            s_c, s_p, sink = _band_scores(qh, kc, kp, n, s_ref[0:1, :], lo_row, is_lo)
            lse_h = jnp.max(jnp.where(half, lsev, -jnp.inf), axis=-1, keepdims=True)
            p_c, p_p = jnp.exp(s_c - lse_h), jnp.exp(s_p - lse_h)
            delta = jnp.sum(doh * ov, axis=-1, keepdims=True)
            dob = doh.astype(BF16)
            ds_c = (p_c * (lax.dot_general(dob, vc, nt, preferred_element_type=F32) - delta)).astype(BF16)
            ds_p = (p_p * (lax.dot_general(dob, vp, nt, preferred_element_type=F32) - delta)).astype(BF16)
            dsinks.append(-jnp.sum(jnp.exp(sink - lse_h) * delta, axis=0, keepdims=True))
            dqs.append((jnp.dot(ds_c, kc, preferred_element_type=F32) + jnp.dot(ds_p, kp, preferred_element_type=F32)) * scale)
            parts = (lax.dot_general(ds_c, qh, tn_, preferred_element_type=F32) * scale,
                     lax.dot_general(ds_p, qh, tn_, preferred_element_type=F32) * scale,
                     lax.dot_general(p_c.astype(BF16), dob, tn_, preferred_element_type=F32),
                     lax.dot_general(p_p.astype(BF16), dob, tn_, preferred_element_type=F32))
            if dkc is None:
                dkc, dkp, dvc, dvp = parts
            else:
                dkc, dkp, dvc, dvp = dkc + parts[0], dkp + parts[1], dvc + parts[2], dvp + parts[3]
        dq_ref[...] = jnp.where(lo, dqs[0], dqs[1])
        dsk_ref[...] = jnp.zeros((8, LANES), F32) + jnp.where(lo_row, dsinks[0], dsinks[1])

        @pl.when(cc == 0)
        def _():
            dkc_ref[...] = dkc
            dkp_ref[...] = dkp
            dvc_ref[...] = dvc
            dvp_ref[...] = dvp

        @pl.when(cc > 0)
        def _():
            dkc_ref[...] += dkc
            dkp_ref[...] += dkp
            dvc_ref[...] += dvc
            dvp_ref[...] += dvp

    qs = pl.BlockSpec((Q_BLOCK, LANES), lambda k, n, cc: (n, k * per_kv + cc))
    cur = pl.BlockSpec((Q_BLOCK, LANES), lambda k, n, cc: (n, k))
    prev = pl.BlockSpec((Q_BLOCK, LANES), lambda k, n, cc: (jnp.maximum(n - 1, 0), k))
    kv_shape = jax.ShapeDtypeStruct((t, nkv * LANES), F32)
    return _pcall(body, carry_us=ATTN_US_PER_STEP * nc * nb, grid=(nkv, nb, per_kv),
                  in_specs=[qs, qs, qs, pl.BlockSpec((None, Q_BLOCK, LANES), lambda k, n, cc: (k * per_kv + cc, n, 0)),
                            cur, prev, cur, prev, pl.BlockSpec((8, LANES), lambda k, n, cc: (k * per_kv + cc, 0))],
                  out_specs=[qs, cur, cur, cur, cur, pl.BlockSpec((None, 8, LANES), lambda k, n, cc: ((k * per_kv + cc) * nb + n, 0, 0))],
                  out_shape=[jax.ShapeDtypeStruct((t, dq), F32), kv_shape, kv_shape, kv_shape, kv_shape,
                             jax.ShapeDtypeStruct((nc * nb, 8, LANES), F32)],
                  compiler_params=_params(("parallel", "parallel", "arbitrary")), name=name)(do, q, o, lse, k2, k2, v2, v2, sink_tab)


def _peer(k):
    x, y, c = lax.axis_index("x"), lax.axis_index("y"), lax.axis_index("c")
    flip = lambda v, bit: 1 - v if bit else v
    return (flip(x, k & 4), flip(y, k & 2), flip(c, k & 1))


def _my_index():
    return 4 * lax.axis_index("x") + 2 * lax.axis_index("y") + lax.axis_index("c")


def _peer_index(k):
    px, py, pc = _peer(k)
    return 4 * px + 2 * py + pc


def _all_gather(shards, name):
    n = len(shards)
    any_spec = pl.BlockSpec(memory_space=pl.ANY)

    def body(*refs):
        ins, outs = refs[:n], refs[n:2 * n]
        send_sems, recv_sems, local_sems = refs[2 * n:]
        me = _my_index()
        local = [pltpu.make_async_copy(ins[a], outs[a].at[me], local_sems.at[a]) for a in range(n)]
        for cp in local:
            cp.start()
        sends = []
        for k in range(1, N_DEV):
            for a in range(n):
                cp = pltpu.make_async_remote_copy(src_ref=ins[a], dst_ref=outs[a].at[me], send_sem=send_sems.at[a, k - 1],
                                                  recv_sem=recv_sems.at[a, k - 1], device_id=_peer(k), device_id_type=MESH_ID)
                cp.start()
                sends.append(cp)
        for k in range(1, N_DEV):
            for a in range(n):
                pltpu.make_async_remote_copy(src_ref=ins[a], dst_ref=outs[a].at[_peer_index(k)], send_sem=send_sems.at[a, k - 1],
                                             recv_sem=recv_sems.at[a, k - 1], device_id=_peer(k), device_id_type=MESH_ID).wait_recv()
        for cp in sends:
            cp.wait_send()
        for cp in local:
            cp.wait()

    return _pcall(body, in_specs=[any_spec] * n, out_specs=[any_spec] * n,
                  out_shape=[jax.ShapeDtypeStruct((N_DEV,) + s.shape, s.dtype) for s in shards],
                  scratch_shapes=[pltpu.SemaphoreType.DMA((n, N_DEV - 1)), pltpu.SemaphoreType.DMA((n, N_DEV - 1)),
                                  pltpu.SemaphoreType.DMA((n,))],
                  name=name)(*shards)


GATHER1_PEERS = (1, 2, 4, 6)
GATHER2_PEERS = (2, 4, 6)
SCATTER_PEERS = tuple(range(1, N_DEV))
MAX_SEMS = N_DEV - 1
MAX_JOBS = 6
US_PER_MB = {"gather1": 5.4, "gather2": 0.6, "scatter": 10.8}
SCATTER_PIECE_US = 110.0


class _Job:
    def __init__(self, key, kind, src, lo=0, hi=None, dst=None):
        self.key, self.kind, self.src, self.dst = key, kind, src, dst
        shape = src.shape if kind != "gather1" else (N_DEV,) + src.shape
        self.out_shape = jax.ShapeDtypeStruct(shape, src.dtype)
        self.rows = shape[1]
        self.lo, self.hi = lo, self.rows if hi is None else hi
        self.row_us = US_PER_MB[kind] * math.prod(shape) * src.dtype.itemsize / 1e6 / self.rows
        pieces = max(1, round(self.row_us * self.rows / SCATTER_PIECE_US)) if kind == "scatter" else 1
        while pieces > 1 and self.rows % (16 * pieces):
            pieces -= 1
        self.piece = self.rows // pieces

    @property
    def cost_us(self):
        return self.row_us * (self.hi - self.lo)


class _Comm:
    def __init__(self):
        self.queue, self.gathered, self.scattered, self.layer = [], {}, [], 0

    def push(self, key, kind, src):
        self.queue.append(_Job(key, kind, src))

    def take(self, budget_us):
        jobs = [j for j in self.queue if j.kind == "gather2"][:MAX_JOBS]
        used = sum(j.cost_us for j in jobs)
        for j in [j for j in self.queue if j.kind != "gather2"]:
            if len(jobs) >= MAX_JOBS:
                break
            if j.kind == "gather1":
                urgent = int(j.key[1]) <= self.layer
                if (used >= budget_us) if urgent else (used + 0.5 * j.cost_us > budget_us):
                    break
                jobs.append(j)
                used += j.cost_us
                continue
            n = 0
            while j.lo + (n + 1) * j.piece <= j.hi and used + 0.5 * j.row_us * j.piece <= budget_us:
                n += 1
                used += j.row_us * j.piece
            if n == 0:
                break
            part = _Job(j.key, "scatter", j.src, j.lo, j.lo + n * j.piece, j.dst)
            part.parent = j
            j.lo = part.hi
            jobs.append(part)
            if j.lo < j.hi:
                break
        self.queue = [j for j in self.queue if j not in jobs and j.lo < j.hi]
        return jobs

    def finish(self, job, result):
        if job.kind == "gather1":
            self.queue.insert(0, _Job(job.key, "gather2", result))
        elif job.kind == "gather2":
            self.gathered[job.key] = result
        elif job.hi == job.rows:
            self.scattered.append((job.key, result))
        else:
            job.parent.dst = result

    def need(self, key):
        while key not in self.gathered:
            assert any(j.key == key for j in self.queue), key
            jobs = [j for j in self.queue if j.kind == "gather2"][:MAX_JOBS]
            if not any(j.key == key for j in jobs):
                for j in self.queue:
                    if j.kind == "gather1" and len(jobs) < MAX_JOBS:
                        jobs.append(j)
                        if j.key == key:
                            break
            self.flush(jobs)
        return self.gathered[key]

    def flush(self, jobs):
        self.queue = [j for j in self.queue if j not in jobs]

        def body(o_ref):
            o_ref[...] = jnp.zeros_like(o_ref)
        _carry(body, jobs, self, dict(in_specs=[], out_specs=pl.BlockSpec(memory_space=pltpu.VMEM),
                                      out_shape=jax.ShapeDtypeStruct((8, LANES), F32), name="exchange"))()


def _job_copies(job, src, dst, send_sems, recv_sems, local_sem):
    me = _my_index()
    peers = {"gather1": GATHER1_PEERS, "gather2": GATHER2_PEERS, "scatter": SCATTER_PEERS}[job.kind]
    sends, recvs = [], []
    for i, k in enumerate(peers):
        if job.kind == "gather1":
            s_ref, d_ref, to, got = src, dst.at[me], _peer(k), dst.at[_peer_index(k)]
        elif job.kind == "gather2":
            s_ref, d_ref, to, got = src.at[_peer_index(k)], dst.at[_peer_index(k)], _peer(1), dst.at[_peer_index(k | 1)]
        else:
            rows = pl.ds(job.lo, job.hi - job.lo)
            s_ref, d_ref, to, got = src.at[_peer_index(k), rows], dst.at[me, rows], _peer(k), dst.at[_peer_index(k), rows]
        sends.append(pltpu.make_async_remote_copy(src_ref=s_ref, dst_ref=d_ref, send_sem=send_sems.at[i], recv_sem=recv_sems.at[i],
                                                  device_id=to, device_id_type=MESH_ID))
        recvs.append(pltpu.make_async_remote_copy(src_ref=s_ref, dst_ref=got, send_sem=send_sems.at[i], recv_sem=recv_sems.at[i],
                                                  device_id=to, device_id_type=MESH_ID))
    local = None
    if job.kind == "gather1":
        local = pltpu.make_async_copy(src, dst.at[me], local_sem)
    elif job.kind == "scatter":
        rows = pl.ds(job.lo, job.hi - job.lo)
        local = pltpu.make_async_copy(src.at[me, rows], dst.at[me, rows], local_sem)
    return sends, recvs, local


def _carry(body, jobs, comm, kw):
    kw = dict(kw)
    grid = tuple(kw.get("grid", ()))
    in_specs = list(kw["in_specs"])
    single = not isinstance(kw["out_specs"], (list, tuple))
    out_specs = [kw["out_specs"]] if single else list(kw["out_specs"])
    out_shape = [kw["out_shape"]] if single else list(kw["out_shape"])
    scratch = list(kw.get("scratch_shapes", []))
    n_in, n_out, n_scr, nj = len(in_specs), len(out_specs), len(scratch), len(jobs)
    any_spec = pl.BlockSpec(memory_space=pl.ANY)
    landed = [a for a, job in enumerate(jobs) if job.dst is not None]
    n_land = len(landed)

    def wrapped(*refs):
        pos = 0

        def take(k):
            nonlocal pos
            part = refs[pos:pos + k]
            pos += k
            return part
        ins, rin, _, outs, rout, scr = take(n_in), take(nj), take(n_land), take(n_out), take(nj), take(n_scr)
        send_sems, recv_sems, local_sems = take(3)

        def copies():
            return [_job_copies(job, rin[a], rout[a], send_sems.at[a], recv_sems.at[a], local_sems.at[a]) for a, job in enumerate(jobs)]

        def start():
            for sends, _, local in copies():
                if local is not None:
                    local.start()
                for cp in sends:
                    cp.start()

        def finish():
            for sends, recvs, local in copies():
                for cp in recvs:
                    cp.wait_recv()
                for cp in sends:
                    cp.wait_send()
                if local is not None:
                    local.wait()

        if grid:
            first = functools.reduce(jnp.logical_and, [pl.program_id(a) == 0 for a in range(len(grid))])
            last = functools.reduce(jnp.logical_and, [pl.program_id(a) == grid[a] - 1 for a in range(len(grid))])
            pl.when(first)(start)
            body(*ins, *outs, *scr)
            pl.when(last)(finish)
        else:
            start()
            body(*ins, *outs, *scr)
            finish()

    aliases = {n_in + a: n_out + a for a, job in enumerate(jobs) if job.kind == "gather2"}
    aliases.update({n_in + nj + i: n_out + a for i, a in enumerate(landed)})
    extra = dict(dimension_semantics=("arbitrary",) * len(grid)) if grid else {}
    call = _raw_call(wrapped, in_specs=in_specs + [any_spec] * (nj + n_land), out_specs=out_specs + [any_spec] * nj,
                     out_shape=out_shape + [job.out_shape for job in jobs],
                     scratch_shapes=scratch + [pltpu.SemaphoreType.DMA((nj, MAX_SEMS)), pltpu.SemaphoreType.DMA((nj, MAX_SEMS)),
                                               pltpu.SemaphoreType.DMA((nj,))],
                     input_output_aliases=aliases, compiler_params=_params(**extra), name=kw["name"],
                     **({"grid": grid} if grid else {}))

    def run(*args):
        res = call(*args, *[job.src for job in jobs], *[jobs[a].dst for a in landed])
        for job, r in zip(jobs, res[n_out:]):
            comm.finish(job, r)
        return res[0] if single else list(res[:n_out])
    return run


def _adam(g, w, m, v):
    m2 = ADAM_B1 * m + (1.0 - ADAM_B1) * g
    v2 = ADAM_B2 * v + (1.0 - ADAM_B2) * (g * g)
    m_hat = m2 / (1.0 - ADAM_B1 ** ADAM_STEP)
    v_hat = v2 / (1.0 - ADAM_B2 ** ADAM_STEP)
    delta = -ADAM_LR * (m_hat / (jnp.sqrt(v_hat) + ADAM_EPS) + ADAM_WD * w)
    return delta, m2, v2


def _sum_adam(parts, w, m, v, name):
    r, c = w.shape
    tr = _pick(r, max(8, (1 << 19) // c), 8)

    def body(p_ref, w_ref, m_ref, v_ref, g_ref, d_ref, m2_ref, v2_ref):
        g = p_ref[0].astype(F32)
        for j in range(1, N_DEV):
            g = g + p_ref[j].astype(F32)
        delta, m2, v2 = _adam(g, w_ref[...], m_ref[...], v_ref[...])
        g_ref[...] = g
        d_ref[...] = delta
        m2_ref[...] = m2
        v2_ref[...] = v2

    blk = pl.BlockSpec((tr, c), lambda i: (i, 0))
    shp = jax.ShapeDtypeStruct((r, c), F32)
    return _pcall(body, grid=(r // tr,),
                  in_specs=[pl.BlockSpec((N_DEV, tr, c), lambda i: (0, i, 0)), blk, blk, blk],
                  out_specs=[blk] * 4, out_shape=[shp] * 4, compiler_params=_params(("parallel",)), name=name)(parts, w, m, v)


def _small_layout(rep_shapes, sh_shapes):
    rows_r = [-(-s[1] // LANES) for s in rep_shapes]
    off_r = [sum(rows_r[:i]) for i in range(len(rows_r))]
    tot_r = -(-max(sum(rows_r), 8) // 8) * 8
    rows_s = [-(-s[-2] // 8) * 8 for s in sh_shapes]
    off_s = [sum(rows_s[:i]) for i in range(len(rows_s))]
    tot_s = max(sum(rows_s), 8)
    cmax = max([s[-1] for s in sh_shapes] + [LANES])
    return rows_r, off_r, tot_r, off_s, tot_s, cmax


def _small_exchange(rep_parts, sh_parts, name):
    nr, ns = len(rep_parts), len(sh_parts)
    rows_r, off_r, tot_r, off_s, tot_s, cmax = _small_layout([p.shape for p in rep_parts], [p.shape for p in sh_parts])
    vm = pl.BlockSpec(memory_space=pltpu.VMEM)

    def body(*refs):
        pos = 0

        def take(k):
            nonlocal pos
            out = refs[pos:pos + k]
            pos += k
            return out
        rp, sp = take(nr), take(ns)
        out_r, out_s = take(2)
        pack_r, got_r, pack_s, got_s, send_r, recv_r, send_s, recv_s = take(8)
        me = _my_index()
        pack_r[...] = jnp.zeros_like(pack_r)
        pack_s[...] = jnp.zeros_like(pack_s)
        for i in range(nr):
            nfull = rep_parts[i].shape[1]
            for rr in range(rows_r[i]):
                wdt = min(LANES, nfull - rr * LANES)
                pack_r[off_r[i] + rr:off_r[i] + rr + 1, 0:wdt] = rp[i][0:1, rr * LANES:rr * LANES + wdt]
        for i in range(ns):
            _, r_i, c_i = sh_parts[i].shape
            for j in range(N_DEV):
                pack_s[j, off_s[i]:off_s[i] + r_i, 0:c_i] = sp[i][j]
        got_r[me] = pack_r[...]
        got_s[me] = pack_s[me]
        sends = []
        for k in range(1, N_DEV):
            a = pltpu.make_async_remote_copy(src_ref=pack_r, dst_ref=got_r.at[me], send_sem=send_r.at[k - 1], recv_sem=recv_r.at[k - 1],
                                             device_id=_peer(k), device_id_type=MESH_ID)
            b = pltpu.make_async_remote_copy(src_ref=pack_s.at[_peer_index(k)], dst_ref=got_s.at[me], send_sem=send_s.at[k - 1],
                                             recv_sem=recv_s.at[k - 1], device_id=_peer(k), device_id_type=MESH_ID)
            a.start()
            b.start()
            sends += [a, b]
        for k in range(1, N_DEV):
            pltpu.make_async_remote_copy(src_ref=pack_r, dst_ref=got_r.at[_peer_index(k)], send_sem=send_r.at[k - 1],
                                         recv_sem=recv_r.at[k - 1], device_id=_peer(k), device_id_type=MESH_ID).wait_recv()
            pltpu.make_async_remote_copy(src_ref=pack_s.at[me], dst_ref=got_s.at[_peer_index(k)], send_sem=send_s.at[k - 1],
                                         recv_sem=recv_s.at[k - 1], device_id=_peer(k), device_id_type=MESH_ID).wait_recv()
        for cp in sends:
            cp.wait_send()
        tot_rep = got_r[0]
        tot_sh = got_s[0]
        for j in range(1, N_DEV):
            tot_rep = tot_rep + got_r[j]
            tot_sh = tot_sh + got_s[j]
        out_r[...] = tot_rep
        out_s[...] = tot_sh

    return _pcall(body, carry_us=SMALL_EXCHANGE_CARRY_US, in_specs=[vm] * (nr + ns), out_specs=[vm] * 2,
                  out_shape=[jax.ShapeDtypeStruct((tot_r, LANES), F32), jax.ShapeDtypeStruct((tot_s, cmax), F32)],
                  scratch_shapes=[pltpu.VMEM((tot_r, LANES), F32), pltpu.VMEM((N_DEV, tot_r, LANES), F32),
                                  pltpu.VMEM((N_DEV, tot_s, cmax), F32), pltpu.VMEM((N_DEV, tot_s, cmax), F32),
                                  pltpu.SemaphoreType.DMA((N_DEV - 1,)), pltpu.SemaphoreType.DMA((N_DEV - 1,)),
                                  pltpu.SemaphoreType.DMA((N_DEV - 1,)), pltpu.SemaphoreType.DMA((N_DEV - 1,))],
                  compiler_params=_params(), name=name)(*rep_parts, *sh_parts)


def _small_adam(tot_rep, tot_sh, rep_w, rep_m, rep_v, sh_w, sh_m, sh_v, name):
    nr, ns = len(rep_w), len(sh_w)
    rows_r, off_r, _, off_s, _, _ = _small_layout([w.shape for w in rep_w], [w.shape for w in sh_w])
    vm = pl.BlockSpec(memory_space=pltpu.VMEM)

    def body(*refs):
        pos = 0

        def take(k):
            nonlocal pos
            out = refs[pos:pos + k]
            pos += k
            return out
        (tr_ref, ts_ref), rw, rm, rv, sw, sm, sv = take(2), take(nr), take(nr), take(nr), take(ns), take(ns), take(ns)
        rg, rd, rm2, rv2 = take(nr), take(nr), take(nr), take(nr)
        sg, sd, sm2, sv2 = take(ns), take(ns), take(ns), take(ns)
        for i in range(nr):
            nfull = rep_w[i].shape[1]
            for rr in range(rows_r[i]):
                wdt = min(LANES, nfull - rr * LANES)
                rg[i][0:1, rr * LANES:rr * LANES + wdt] = tr_ref[off_r[i] + rr:off_r[i] + rr + 1, 0:wdt]
            delta, m2, v2 = _adam(rg[i][...], rw[i][...], rm[i][...], rv[i][...])
            rd[i][...] = delta
            rm2[i][...] = m2
            rv2[i][...] = v2
        for i in range(ns):
            r_i, c_i = sh_w[i].shape
            g = ts_ref[off_s[i]:off_s[i] + r_i, 0:c_i]
            delta, m2, v2 = _adam(g, sw[i][...], sm[i][...], sv[i][...])
            sg[i][...] = g
            sd[i][...] = delta
            sm2[i][...] = m2
            sv2[i][...] = v2

    shapes = [jax.ShapeDtypeStruct(w.shape, F32) for w in rep_w] * 4 + [jax.ShapeDtypeStruct(w.shape, F32) for w in sh_w] * 4
    outs = _pcall(body, in_specs=[vm] * (2 + 3 * nr + 3 * ns), out_specs=[vm] * len(shapes), out_shape=shapes,
                  compiler_params=_params(), name=name)(tot_rep, tot_sh, *rep_w, *rep_m, *rep_v, *sh_w, *sh_m, *sh_v)
    rep_out = [outs[i * nr:(i + 1) * nr] for i in range(4)]
    sh_out = [outs[4 * nr + i * ns:4 * nr + (i + 1) * ns] for i in range(4)]
    return rep_out, sh_out


_CONF = ("norm_g", "a_w_in", "a_b_in", "a_dw_w", "a_dw_b", "a_ln_g", "a_ln_b", "a_w_out", "a_b_out")
_FFN = ("ffn_norm_g", "ffn_w_up", "ffn_dw_w", "ffn_dw_b", "ffn_w_down")
_POOL = ("norm_g", "b_w_group", "b_scale")
_ATTN = ("norm_g", "c_w_qkv", "c_q_norm_g", "c_k_norm_g", "c_sinks", "c_w_o")
_LAYERS = (_CONF + _FFN, _POOL + _FFN, _ATTN + _FFN, _CONF + _FFN)
_NAMES = tuple("l%d_%s" % (i, n) for i, names in enumerate(_LAYERS) for n in names)
_BIG = ("a_w_in", "a_w_out", "ffn_w_up", "ffn_w_down", "b_w_group", "c_w_qkv", "c_w_o")
_SHARDED_SMALL = ("a_dw_w", "ffn_dw_w")


def _pad_rows(a, mult=8):
    r = a.shape[0]
    rp = -(-r // mult) * mult
    return a if rp == r else jnp.pad(a, ((0, rp - r), (0, 0)))


def _unstack_cols(st, rows):
    s, r, cs = st.shape
    return jnp.transpose(st, (1, 0, 2)).reshape(r, s * cs)[:rows]


def _stack_cols(a):
    r, c = a.shape
    return jnp.transpose(a.reshape(r, N_DEV, c // N_DEV), (1, 0, 2))


def _row(v):
    return v.reshape(1, -1)


def _ffn_forward(x_mid, p, tag):
    h2 = _rms_fwd(x_mid, _row(p["ffn_norm_g"]), BF16, "rms_fwd_bf16")
    u0 = _mm(h2, p["ffn_w_up"], "nn", name="ffn_up", tn=1408, tk=2048)
    a = _ffn_act_fwd(u0, p["ffn_dw_w"], _row(p["ffn_dw_b"]), "ffn_act_fwd")
    x_out = _mm(a, p["ffn_w_down"], "nn", res=x_mid, name="ffn_down", tk=1408)
    return x_out, dict(h2=h2, u0=u0, a=a)


def _ffn_backward(dx_out, x_mid, p, sv, grads):
    dwd = _mm(sv["a"], dx_out, "tn", out_dtype=BF16, name="ffn_down_dw", tm=1408)
    grads["ffn_w_down"] = dwd.reshape(N_DEV, dwd.shape[0] // N_DEV, dwd.shape[1])
    da = _mm(dx_out, p["ffn_w_down"], "nt", name="ffn_down_dx", tn=1408, tk=2048)
    du0, dww, dwb = _ffn_act_bwd(da, sv["u0"], p["ffn_dw_w"], _row(p["ffn_dw_b"]), "ffn_act_bwd")
    kw = dww.shape[1]
    grads["ffn_dw_w"] = _stack_cols(jnp.transpose(dww, (1, 0, 2)).reshape(kw, -1))
    grads["ffn_dw_b"] = dwb.reshape(1, -1)
    grads["ffn_w_up"] = _mm(sv["h2"], du0, "tn", out_dtype=BF16, out_stack=N_DEV, name="ffn_up_dw", tn=1408)
    dh2 = _mm(du0, p["ffn_w_up"], "nt", name="ffn_up_dx", tk=1408)
    dx_mid, dg, _ = _rms_bwd(dh2, x_mid, _row(p["ffn_norm_g"]), dx_out, "rms_bwd")
    grads["ffn_norm_g"] = dg
    return dx_mid


def _conf_forward(x, p):
    h = _rms_fwd(x, _row(p["norm_g"]), BF16, "rms_fwd_bf16")
    u = _mm(h, p["a_w_in"], "nn", bias=_row(p["a_b_in"]), name="conf_in", tn=512, tk=2048)
    cpre = _conf_conv_fwd(u, p["a_dw_w"], _row(p["a_dw_b"]), "conf_conv_fwd")
    s = _ln_silu_fwd(cpre, _row(p["a_ln_g"]), _row(p["a_ln_b"]), "ln_silu_fwd")
    x_mid = _mm(s, p["a_w_out"], "nn", bias=_row(p["a_b_out"]), res=x, name="conf_out", tk=2048)
    return x_mid, dict(h=h, u=u, cpre=cpre, s=s)


def _conf_backward(dx_mid, x, p, sv, grads):
    dwo = _mm(sv["s"], dx_mid, "tn", out_dtype=BF16, name="conf_out_dw")
    grads["a_w_out"] = dwo.reshape(N_DEV, dwo.shape[0] // N_DEV, dwo.shape[1])
    ds = _mm(dx_mid, p["a_w_out"], "nt", name="conf_out_dx", tk=2048)
    dc, dlg, dlb = _ln_silu_bwd(ds, sv["cpre"], _row(p["a_ln_g"]), _row(p["a_ln_b"]), "ln_silu_bwd")
    grads["a_ln_g"], grads["a_ln_b"] = dlg, dlb
    du, dww, dwb, dbin = _conf_conv_bwd(dc, sv["u"], p["a_dw_w"], "conf_conv_bwd")
    grads["a_dw_w"] = _stack_cols(dww)
    grads["a_dw_b"] = dwb
    grads["a_b_in"] = dbin.reshape(1, -1)
    grads["a_w_in"] = _mm(sv["h"], du, "tn", out_dtype=BF16, out_stack=N_DEV, name="conf_in_dw", tn=512)
    dh = _mm(du, p["a_w_in"], "nt", name="conf_in_dx", tk=512)
    dx, dg, dbo = _rms_bwd(dh, x, _row(p["norm_g"]), dx_mid, "rms_bwd")
    grads["norm_g"] = dg
    grads["a_b_out"] = dbo
    return dx


def _pool_forward(x, p):
    h = _rms_fwd(x, _row(p["norm_g"]), F32, "rms_fwd_f32")
    mixed = _pool_fwd(h, "pool_fwd")
    x_mid = _pool_mm_fwd(mixed, p["b_w_group"], _row(p["b_scale"]), x, "pool_mm_fwd")
    return x_mid, dict(mixed=mixed)


def _pool_backward(dx_mid, x, p, sv, grads):
    dmixed, dwg, dscale = _pool_mm_bwd(dx_mid, sv["mixed"], p["b_w_group"], _row(p["b_scale"]), "pool_mm_bwd")
    ng, gd, _ = dwg.shape
    grads["b_w_group"] = jnp.transpose(dwg.reshape(ng, N_DEV, gd // N_DEV, gd), (1, 0, 2, 3)).reshape(N_DEV, ng * gd // N_DEV, gd).astype(BF16)
    grads["b_scale"] = dscale
    dh = _pool_bwd(dmixed, "pool_bwd")
    dx, dg, _ = _rms_bwd(dh, x, _row(p["norm_g"]), dx_mid, "rms_bwd")
    grads["norm_g"] = dg
    return dx


def _attn_tables(p, positions, d_model):
    n_q = d_model // HEAD
    n_kv = n_q // 8
    tabs = _rope_tables(positions)
    gq2 = jnp.concatenate([p["c_q_norm_g"], p["c_q_norm_g"]]).reshape(1, LANES)
    gk2 = jnp.concatenate([p["c_k_norm_g"], p["c_k_norm_g"]]).reshape(1, LANES)
    sink_tab = jnp.repeat(jnp.repeat(p["c_sinks"].reshape(-1, 2), HEAD, axis=1), 8, axis=0)
    return n_q, n_kv, tabs, gq2, gk2, sink_tab


def _attn_forward(x, p, positions):
    n_q, n_kv, tabs, gq2, gk2, sink_tab = _attn_tables(p, positions, x.shape[1])
    h = _rms_fwd(x, _row(p["norm_g"]), BF16, "rms_fwd_bf16")
    qkv = _mm(h, p["c_w_qkv"], "nn", name="attn_qkv", tn=1280, tk=2048)
    q, k2, v2 = _qk_prep_fwd(qkv, tabs, gq2, gk2, n_q, n_kv, "qk_prep_fwd")
    o, lse = _attn_fwd(q, k2, v2, sink_tab, "attn_fwd")
    x_mid = _mm(o, p["c_w_o"], "nn", res=x, name="attn_out", tk=2048)
    return x_mid, dict(h=h, qkv=qkv, q=q, k2=k2, v2=v2, o=o, lse=lse)


def _attn_backward(dx_mid, x, p, positions, sv, grads):
    n_q, n_kv, tabs, gq2, gk2, sink_tab = _attn_tables(p, positions, x.shape[1])
    dwo = _mm(sv["o"], dx_mid, "tn", out_dtype=BF16, name="attn_out_dw")
    grads["c_w_o"] = dwo.reshape(N_DEV, dwo.shape[0] // N_DEV, dwo.shape[1])
    do = _mm(dx_mid, p["c_w_o"], "nt", name="attn_out_dx", tk=2048)
    dq, dkc, dkp, dvc, dvp, dsk = _attn_bwd(do, sv["q"], sv["o"], sv["lse"], sv["k2"], sv["v2"], sink_tab, "attn_bwd")
    nb = x.shape[0] // Q_BLOCK
    dsk = dsk.reshape(-1, nb, 8, LANES)[:, :, 0, :].sum(axis=1)
    grads["c_sinks"] = jnp.stack([dsk[:, 0], dsk[:, HEAD]], axis=1).reshape(1, -1)
    dqkv, dgq, dgk = _qk_prep_bwd(dq, dkc, dkp, dvc, dvp, sv["qkv"], tabs, gq2, gk2, n_q, n_kv, "qk_prep_bwd")
    grads["c_q_norm_g"] = dgq[:, :HEAD] + dgq[:, HEAD:]
    grads["c_k_norm_g"] = dgk[:, :HEAD] + dgk[:, HEAD:]
    dwq = _mm(sv["h"], dqkv, "tn", out_dtype=BF16, name="attn_qkv_dw", tn=1280)
    grads["c_w_qkv"] = _stack_cols(dwq)
    dh = _mm(dqkv, p["c_w_qkv"], "nt", name="attn_qkv_dx", tk=1280)
    dx, dg, _ = _rms_bwd(dh, x, _row(p["norm_g"]), dx_mid, "rms_bwd")
    grads["norm_g"] = dg
    return dx


class _LayerWeights:
    def __init__(self, li, weights, small_full, comm):
        self.li, self.weights, self.small_full, self.comm, self.cache = li, weights, small_full, comm, {}

    def __getitem__(self, nme):
        if nme not in self.cache:
            self.cache[nme] = self.fetch(nme)
        return self.cache[nme]

    def fetch(self, nme):
        full = "l%d_%s" % (self.li, nme)
        w = self.weights[full]
        if nme in _SHARDED_SMALL:
            return _unstack_cols(self.small_full[full], w.shape[0])
        if nme not in _BIG:
            return w
        got = self.comm.need(full)
        if nme in ("a_w_in", "ffn_w_up"):
            return got
        if nme == "c_w_qkv":
            return _unstack_cols(got, w.shape[0])
        if nme == "b_w_group":
            ng, gs, gd = w.shape
            return jnp.transpose(got.reshape(N_DEV, ng, gs, gd), (1, 0, 2, 3)).reshape(ng, N_DEV * gs, gd)
        return got.reshape(-1, w.shape[1])


class _LayerGrads(dict):
    def __init__(self, li, comm):
        super().__init__()
        self.li, self.comm = li, comm

    def __setitem__(self, nme, value):
        if nme in _BIG:
            self.comm.push("l%d_%s" % (self.li, nme), "scatter", value)
        else:
            super().__setitem__(nme, value)


def kernel(*args):
    n_w = len(_NAMES)
    x, positions = args[0], args[1]
    weights = dict(zip(_NAMES, args[2:2 + n_w]))
    loss_target = args[2 + n_w]
    moms = dict(zip(_NAMES, args[3 + n_w:3 + 2 * n_w]))
    vels = dict(zip(_NAMES, args[3 + 2 * n_w:3 + 3 * n_w]))
    x0 = x[0]
    pos = positions[0]
    kinds = ("conf", "pool", "attn", "conf")
    comm = _Comm()
    _STATE["comm"], _STATE["last"] = comm, None
    shd = [n for n in _NAMES if n.split("_", 1)[1] in _SHARDED_SMALL]
    small_full = dict(zip(shd, _all_gather([_pad_rows(weights[n]) for n in shd], "gather_small")))
    for n in _NAMES:
        if n.split("_", 1)[1] in _BIG:
            w = weights[n]
            comm.push(n, "gather1", w.astype(BF16).reshape(-1, w.shape[-1]))
    results = {}

    def update_ready():
        while comm.scattered:
            full, parts = comm.scattered.pop(0)
            w = weights[full]
            w2 = w.reshape(-1, w.shape[-1])
            outs = _sum_adam(parts, w2, moms[full].reshape(w2.shape), vels[full].reshape(w2.shape), "adam_" + full.split("_", 1)[1])
            results[full] = tuple(o.reshape(w.shape) for o in outs)

    params, saved = [], []
    cur = x0
    for li, names in enumerate(_LAYERS):
        comm.layer = li
        p = _LayerWeights(li, weights, small_full, comm)
        if kinds[li] == "conf":
            x_mid, sv = _conf_forward(cur, p)
        elif kinds[li] == "pool":
            x_mid, sv = _pool_forward(cur, p)
        else:
            x_mid, sv = _attn_forward(cur, p, pos)
        x_out, sv_f = _ffn_forward(x_mid, p, kinds[li])
        params.append(p)
        saved.append((sv, sv_f, cur, x_mid))
        cur = x_out
    dy, loss_part = _loss_head(cur, loss_target[0], "loss_head")
    loss = lax.psum(loss_part[0, 0], ("x", "y", "c"))

    small_grads = {}
    dcur = dy
    for li in range(len(_LAYERS) - 1, -1, -1):
        p = params[li]
        sv, sv_f, x_in, x_mid = saved[li]
        grads = _LayerGrads(li, comm)
        dmid = _ffn_backward(dcur, x_mid, p, sv_f, grads)
        update_ready()
        if kinds[li] == "conf":
            dcur = _conf_backward(dmid, x_in, p, sv, grads)
        elif kinds[li] == "pool":
            dcur = _pool_backward(dmid, x_in, p, sv, grads)
        else:
            dcur = _attn_backward(dmid, x_in, p, pos, sv, grads)
        update_ready()
        for n in _LAYERS[li]:
            if n not in _BIG:
                small_grads["l%d_%s" % (li, n)] = grads[n]
    rep = [n for n in _NAMES if n.split("_", 1)[1] not in _BIG and n.split("_", 1)[1] not in _SHARDED_SMALL]
    tot_rep, tot_sh = _small_exchange([small_grads[n] for n in rep], [small_grads[n] for n in shd], "small_exchange")
    while comm.queue or comm.scattered:
        if not comm.scattered:
            comm.flush(comm.take(1e9))
        update_ready()
    _STATE["comm"] = None
    rep_out, sh_out = _small_adam(tot_rep, tot_sh, [_row(weights[n]) for n in rep], [_row(moms[n]) for n in rep], [_row(vels[n]) for n in rep],
                                  [weights[n] for n in shd], [moms[n] for n in shd], [vels[n] for n in shd], "small_adam")
    for i, n in enumerate(rep):
        results[n] = tuple(rep_out[k][i].reshape(weights[n].shape) for k in range(4))
    for i, n in enumerate(shd):
        results[n] = tuple(sh_out[k][i] for k in range(4))

    _STATE["last"] = None
    grad_x = dcur[None]
    out = [loss, grad_x]
    for k in range(4):
        out += [results[n][k] for n in _NAMES]
    return tuple(out)
```

```python
import functools
import math

import jax
import jax.numpy as jnp
from jax import lax
from jax.experimental import pallas as pl
from jax.experimental.pallas import tpu as pltpu

F32 = jnp.float32
BF16 = jnp.bfloat16
N_DEV = 8
EPS = 1e-6
LANES = 128
HEAD = 64
Q_BLOCK = 128
ROT_DIM = 16
ROPE_THETA = 500000.0
POOL_WINDOWS = (2, 4, 8, 16)
HALO = 32
ROWS = 128
VMEM_LIMIT = 56 * 1024 * 1024
ADAM_LR, ADAM_B1, ADAM_B2, ADAM_EPS, ADAM_WD, ADAM_STEP = 0.001, 0.9, 0.999, 1e-08, 0.01, 10
MESH_ID = pl.DeviceIdType.MESH
MXU_FLOPS_PER_US = 7.5e8
HBM_BYTES_PER_US = 2.5e6
ATTN_US_PER_STEP = 1.5
CONV_FWD_US_PER_ELEM = 0.7e-5
CONV_BWD_US_PER_ELEM = 2.3e-5
ACT_FWD_US_PER_ELEM = 3.4e-6
SMALL_EXCHANGE_CARRY_US = 100.0


def _make_call(body, **kw):
    return pl.pallas_call(body, **kw)


_STATE = {"comm": None, "last": None}


def _raw_call(body, **kw):
    call = _make_call(body, **kw)

    def run(*args):
        last = _STATE["last"]
        if last is not None and args:
            first, _ = lax.optimization_barrier((args[0], last))
            args = (first,) + tuple(args[1:])
        res = call(*args)
        _STATE["last"] = res[0] if isinstance(res, (list, tuple)) else res
        return res
    return run


def _pcall(body, carry_us=0.0, **kw):
    comm = _STATE["comm"]
    jobs = comm.take(carry_us) if (comm is not None and carry_us > 0) else []
    if not jobs:
        return _raw_call(body, **kw)
    return _carry(body, jobs, comm, kw)


def _params(sem=None, **kw):
    if sem is not None:
        kw["dimension_semantics"] = sem
    return pltpu.CompilerParams(vmem_limit_bytes=VMEM_LIMIT, **kw)


def _pick(dim, pref, mult=LANES):
    best = None
    d = mult
    while d <= min(dim, pref):
        if dim % d == 0:
            best = d
        d += mult
    return dim if best is None else best


def _sigmoid(x):
    return 1.0 / (1.0 + jnp.exp(-x))


def _fold8(p):
    r, c = p.shape
    return p.reshape(r // 8, 8, c).sum(axis=0)


def _window(win_ref, e):
    win_ref[...] = e
    return win_ref


def _rows(win_ref, k, r):
    return win_ref[k:k + r, :]


def _lshape(a):
    return a.shape if a.ndim == 2 else (a.shape[1], a.shape[0] * a.shape[2])


def _panel(a):
    return a.shape[1] if a.ndim == 2 else a.shape[2]


def _lspec(a, br, bc, rc):
    if a.ndim == 2:
        return pl.BlockSpec((br, bc), rc)
    per = a.shape[2] // bc

    def idx(i, j, k):
        r, c = rc(i, j, k)
        return (c // per, r, c % per)
    return pl.BlockSpec((None, br, bc), idx)


def _mm(a, b, dims, *, name, out_dtype=F32, out_stack=None, bias=None, res=None, tm=1024, tn=1024, tk=1024):
    (ar, ac), (br_, bc_) = _lshape(a), _lshape(b)
    if dims == "nn":
        m, k, n = ar, ac, bc_
        lim_m, lim_k, lim_n = m, min(_panel(a), k), _panel(b)
    elif dims == "nt":
        m, k, n = ar, ac, br_
        lim_m, lim_k, lim_n = m, math.gcd(_panel(a), _panel(b)), n
    else:
        m, k, n = ac, ar, bc_
        lim_m, lim_k, lim_n = _panel(a), k, _panel(b)
    if out_stack is not None:
        lim_n = math.gcd(lim_n, n // out_stack)
    sub = 16 if (out_dtype == BF16 or a.dtype == BF16) else 8
    tm = _pick(lim_m, tm, LANES if dims == "tn" else sub)
    tn = _pick(lim_n, tn)
    tk = _pick(lim_k, tk, sub if dims == "tn" else LANES)
    nk = k // tk
    if dims == "tn":
        a_spec = _lspec(a, tk, tm, lambda i, j, kk: (kk, i))
    else:
        a_spec = _lspec(a, tm, tk, lambda i, j, kk: (i, kk))
    if dims == "nt":
        b_spec = _lspec(b, tn, tk, lambda i, j, kk: (j, kk))
    else:
        b_spec = _lspec(b, tk, tn, lambda i, j, kk: (kk, j))
    contract = {"nn": ((1,), (0,)), "nt": ((1,), (1,)), "tn": ((0,), (0,))}[dims]
    in_specs, args = [a_spec, b_spec], [a, b]
    if bias is not None:
        in_specs.append(pl.BlockSpec((1, tn), lambda i, j, kk: (0, j)))
        args.append(bias)
    if res is not None:
        in_specs.append(pl.BlockSpec((tm, tn), lambda i, j, kk: (i, j)))
        args.append(res)
    if out_stack is None:
        out_shape = jax.ShapeDtypeStruct((m, n), out_dtype)
    else:
        out_shape = jax.ShapeDtypeStruct((out_stack, m, n // out_stack), out_dtype)
    o_spec = _lspec(out_shape, tm, tn, lambda i, j, kk: (i, j))
    has_bias, has_res = bias is not None, res is not None

    def body(*refs):
        a_ref, b_ref = refs[0], refs[1]
        pos = 2
        bias_ref = res_ref = None
        if has_bias:
            bias_ref = refs[pos]
            pos += 1
        if has_res:
            res_ref = refs[pos]
            pos += 1
        o_ref = refs[pos]

        def part():
            return lax.dot_general(a_ref[...].astype(BF16), b_ref[...].astype(BF16), (contract, ((), ())),
                                   preferred_element_type=F32)

        def finish(r):
            if has_bias:
                r = r + bias_ref[...]
            if has_res:
                r = r + res_ref[...]
            o_ref[...] = r.astype(out_dtype)

        if nk == 1:
            finish(part())
        else:
            acc = refs[pos + 1]
            kk = pl.program_id(2)

            @pl.when(kk == 0)
            def _():
                acc[...] = part()

            @pl.when(kk > 0)
            def _():
                acc[...] += part()

            @pl.when(kk == nk - 1)
            def _():
                finish(acc[...])

    scratch = [] if nk == 1 else [pltpu.VMEM((tm, tn), F32)]
    return _pcall(body, carry_us=2.0 * m * n * k / MXU_FLOPS_PER_US, grid=(m // tm, n // tn, nk), in_specs=in_specs, out_specs=o_spec, out_shape=out_shape,
                  scratch_shapes=scratch, compiler_params=_params(("parallel", "parallel", "arbitrary")), name=name)(*args)


def _rms_fwd(x, g, out_dtype, name):
    t, d = x.shape
    tm = _pick(t, 512, 16)

    def body(x_ref, g_ref, o_ref):
        xv = x_ref[...]
        r = lax.rsqrt(jnp.mean(xv * xv, axis=-1, keepdims=True) + EPS)
        o_ref[...] = ((xv * r) * g_ref[...]).astype(out_dtype)

    return _pcall(body, grid=(t // tm,), in_specs=[pl.BlockSpec((tm, d), lambda i: (i, 0)), pl.BlockSpec((1, d), lambda i: (0, 0))],
                  out_specs=pl.BlockSpec((tm, d), lambda i: (i, 0)), out_shape=jax.ShapeDtypeStruct((t, d), out_dtype),
                  compiler_params=_params(("parallel",)), name=name)(x, g)


def _rms_bwd(dh, x, g, dres, name):
    t, d = x.shape
    tm = _pick(t, 256, 8)

    def body(dh_ref, x_ref, g_ref, dres_ref, dx_ref, dg_ref, cs_ref):
        xv, dhv, dr = x_ref[...], dh_ref[...], dres_ref[...]
        r = lax.rsqrt(jnp.mean(xv * xv, axis=-1, keepdims=True) + EPS)
        xh = xv * r
        dxh = dhv * g_ref[...]
        dx_ref[...] = dr + r * (dxh - xh * jnp.mean(dxh * xh, axis=-1, keepdims=True))
        pg = jnp.sum(dhv * xh, axis=0, keepdims=True)
        pc = jnp.sum(dr, axis=0, keepdims=True)

        @pl.when(pl.program_id(0) == 0)
        def _():
            dg_ref[...] = pg
            cs_ref[...] = pc

        @pl.when(pl.program_id(0) > 0)
        def _():
            dg_ref[...] += pg
            cs_ref[...] += pc

    row = pl.BlockSpec((tm, d), lambda i: (i, 0))
    vec = pl.BlockSpec((1, d), lambda i: (0, 0))
    return _pcall(body, carry_us=16.0 * t * d / HBM_BYTES_PER_US, grid=(t // tm,), in_specs=[row, row, vec, row], out_specs=[row, vec, vec],
                  out_shape=[jax.ShapeDtypeStruct((t, d), F32), jax.ShapeDtypeStruct((1, d), F32), jax.ShapeDtypeStruct((1, d), F32)],
                  compiler_params=_params(("arbitrary",)), name=name)(dh, x, g, dres)


def _loss_head(y, target, name):
    t, d = y.shape
    tm = _pick(t, 512, 8)

    def body(y_ref, t_ref, dy_ref, l_ref):
        e = y_ref[...] - t_ref[...]
        dy_ref[...] = e * (1.0 / d)
        part = 0.5 * jnp.sum(jnp.mean(e * e, axis=-1, keepdims=True), axis=0, keepdims=True)

        @pl.when(pl.program_id(0) == 0)
        def _():
            l_ref[...] = part

        @pl.when(pl.program_id(0) > 0)
        def _():
            l_ref[...] += part

    row = pl.BlockSpec((tm, d), lambda i: (i, 0))
    return _pcall(body, grid=(t // tm,), in_specs=[row, row], out_specs=[row, pl.BlockSpec((1, 1), lambda i: (0, 0))],
                  out_shape=[jax.ShapeDtypeStruct((t, d), F32), jax.ShapeDtypeStruct((1, 1), F32)],
                  compiler_params=_params(("arbitrary",)), name=name)(y, target)


def _ln_silu_fwd(c, g, b, name):
    t, d = c.shape
    tm = _pick(t, 512, 16)

    def body(c_ref, g_ref, b_ref, o_ref):
        cv = c_ref[...]
        xc = cv - jnp.mean(cv, axis=-1, keepdims=True)
        z = xc * lax.rsqrt(jnp.mean(xc * xc, axis=-1, keepdims=True) + EPS) * g_ref[...] + b_ref[...]
        o_ref[...] = (z * _sigmoid(z)).astype(BF16)

    row = pl.BlockSpec((tm, d), lambda i: (i, 0))
    vec = pl.BlockSpec((1, d), lambda i: (0, 0))
    return _pcall(body, grid=(t // tm,), in_specs=[row, vec, vec], out_specs=row, out_shape=jax.ShapeDtypeStruct((t, d), BF16),
                  compiler_params=_params(("parallel",)), name=name)(c, g, b)


def _ln_silu_bwd(ds, c, g, b, name):
    t, d = c.shape
    tm = _pick(t, 256, 8)

    def body(ds_ref, c_ref, g_ref, b_ref, dc_ref, dg_ref, db_ref):
        cv = c_ref[...]
        xc = cv - jnp.mean(cv, axis=-1, keepdims=True)
        r = lax.rsqrt(jnp.mean(xc * xc, axis=-1, keepdims=True) + EPS)
        ch = xc * r
        z = ch * g_ref[...] + b_ref[...]
        sg = _sigmoid(z)
        dz = ds_ref[...] * (sg * (1.0 + z * (1.0 - sg)))
        dch = dz * g_ref[...]
        dc_ref[...] = r * (dch - jnp.mean(dch, axis=-1, keepdims=True) - ch * jnp.mean(dch * ch, axis=-1, keepdims=True))
        pg = jnp.sum(dz * ch, axis=0, keepdims=True)
        pb = jnp.sum(dz, axis=0, keepdims=True)

        @pl.when(pl.program_id(0) == 0)
        def _():
            dg_ref[...] = pg
            db_ref[...] = pb

        @pl.when(pl.program_id(0) > 0)
        def _():
            dg_ref[...] += pg
            db_ref[...] += pb

    row = pl.BlockSpec((tm, d), lambda i: (i, 0))
    vec = pl.BlockSpec((1, d), lambda i: (0, 0))
    return _pcall(body, grid=(t // tm,), in_specs=[row, row, vec, vec], out_specs=[row, vec, vec],
                  out_shape=[jax.ShapeDtypeStruct((t, d), F32), jax.ShapeDtypeStruct((1, d), F32), jax.ShapeDtypeStruct((1, d), F32)],
                  compiler_params=_params(("arbitrary",)), name=name)(ds, c, g, b)


def _steps(t):
    return t // ROWS


def _conf_conv_fwd(u, dw_w, dw_b, name):
    t, d2 = u.shape
    d = d2 // 2
    c = LANES
    ns = d // c
    kc = dw_w.shape[0]

    def body(a_ref, g_ref, w_ref, b_ref, o_ref, pad, win):
        pad[0:HALO, :] = jnp.zeros((HALO, c), F32)

        def glu(i, _):
            base = pl.multiple_of(i * ROWS, ROWS)
            pad[pl.ds(base + HALO, ROWS), :] = a_ref[pl.ds(base, ROWS), :] * _sigmoid(g_ref[pl.ds(base, ROWS), :])
            return 0
        lax.fori_loop(0, _steps(t), glu, 0)

        def conv(i, _):
            base = pl.multiple_of(i * ROWS, ROWS)
            e = _window(win, pad[pl.ds(base, ROWS + HALO), :])
            acc = jnp.zeros((ROWS, c), F32) + b_ref[...]
            for j in range(kc):
                acc = acc + w_ref[j:j + 1, :] * _rows(e, HALO - (kc - 1) + j, ROWS)
            o_ref[pl.ds(base, ROWS), :] = acc
            return 0
        lax.fori_loop(0, _steps(t), conv, 0)

    return _pcall(body, carry_us=CONV_FWD_US_PER_ELEM * t * d, grid=(ns,),
                  in_specs=[pl.BlockSpec((t, c), lambda s: (0, s)), pl.BlockSpec((t, c), lambda s: (0, s + ns)),
                            pl.BlockSpec((kc, c), lambda s: (0, s)), pl.BlockSpec((1, c), lambda s: (0, s))],
                  out_specs=pl.BlockSpec((t, c), lambda s: (0, s)), out_shape=jax.ShapeDtypeStruct((t, d), F32),
                  scratch_shapes=[pltpu.VMEM((t + HALO, c), F32), pltpu.VMEM((ROWS + HALO, c), F32)],
                  compiler_params=_params(("parallel",)), name=name)(u, u, dw_w, dw_b)


def _conf_conv_bwd(dc, u, dw_w, name):
    t, d = dc.shape
    c = LANES
    ns = d // c
    kc = dw_w.shape[0]

    def body(dc_ref, a_ref, g_ref, w_ref, du_ref, dww_ref, dwb_ref, db_ref, padv, padd, accw, accb, winv, wind):
        padv[0:HALO, :] = jnp.zeros((HALO, c), F32)
        padd[t:t + HALO, :] = jnp.zeros((HALO, c), F32)
        accw[...] = jnp.zeros_like(accw)
        accb[...] = jnp.zeros_like(accb)

        def fill(i, _):
            base = pl.multiple_of(i * ROWS, ROWS)
            padv[pl.ds(base + HALO, ROWS), :] = a_ref[pl.ds(base, ROWS), :] * _sigmoid(g_ref[pl.ds(base, ROWS), :])
            padd[pl.ds(base, ROWS), :] = dc_ref[pl.ds(base, ROWS), :]
            return 0
        lax.fori_loop(0, _steps(t), fill, 0)

        def step(i, _):
            base = pl.multiple_of(i * ROWS, ROWS)
            ev = _window(winv, padv[pl.ds(base, ROWS + HALO), :])
            ed = _window(wind, padd[pl.ds(base, ROWS + HALO), :])
            dcc = _rows(ed, 0, ROWS)
            dv = jnp.zeros((ROWS, c), F32)
            for j in range(kc):
                dv = dv + w_ref[j:j + 1, :] * _rows(ed, kc - 1 - j, ROWS)
                accw[j] = accw[j] + _fold8(dcc * _rows(ev, HALO - (kc - 1) + j, ROWS))
            accb[0] = accb[0] + _fold8(dcc)
            av = a_ref[pl.ds(base, ROWS), :]
            sg = _sigmoid(g_ref[pl.ds(base, ROWS), :])
            da = dv * sg
            dg = dv * av * sg * (1.0 - sg)
            du_ref[0, pl.ds(base, ROWS), :] = da.astype(BF16)
            du_ref[1, pl.ds(base, ROWS), :] = dg.astype(BF16)
            accb[1] = accb[1] + _fold8(da)
            accb[2] = accb[2] + _fold8(dg)
            return 0
        lax.fori_loop(0, _steps(t), step, 0)
        for j in range(kc):
            dww_ref[j:j + 1, :] = jnp.sum(accw[j], axis=0, keepdims=True)
        dwb_ref[...] = jnp.sum(accb[0], axis=0, keepdims=True)
        db_ref[0] = jnp.sum(accb[1], axis=0, keepdims=True)
        db_ref[1] = jnp.sum(accb[2], axis=0, keepdims=True)

    return _pcall(body, carry_us=CONV_BWD_US_PER_ELEM * t * d, grid=(ns,),
                  in_specs=[pl.BlockSpec((t, c), lambda s: (0, s)), pl.BlockSpec((t, c), lambda s: (0, s)),
                            pl.BlockSpec((t, c), lambda s: (0, s + ns)), pl.BlockSpec((kc, c), lambda s: (0, s))],
                  out_specs=[pl.BlockSpec((2, t, c), lambda s: (0, 0, s)), pl.BlockSpec((kc, c), lambda s: (0, s)),
                             pl.BlockSpec((1, c), lambda s: (0, s)), pl.BlockSpec((2, 1, c), lambda s: (0, 0, s))],
                  out_shape=[jax.ShapeDtypeStruct((2, t, d), BF16), jax.ShapeDtypeStruct((kc, d), F32),
                             jax.ShapeDtypeStruct((1, d), F32), jax.ShapeDtypeStruct((2, 1, d), F32)],
                  scratch_shapes=[pltpu.VMEM((t + HALO, c), F32), pltpu.VMEM((t + HALO, c), F32),
                                  pltpu.VMEM((kc, 8, c), F32), pltpu.VMEM((3, 8, c), F32),
                                  pltpu.VMEM((ROWS + HALO, c), F32), pltpu.VMEM((ROWS + HALO, c), F32)],
                  compiler_params=_params(("parallel",)), name=name)(dc, u, u, dw_w)


def _ffn_act_fwd(u0, dw_w, dw_b, name):
    t, f2 = u0.shape
    f = f2 // 2
    c = LANES
    ns = f // c
    kw = dw_w.shape[0]

    def body(g_ref, v_ref, wg_ref, wv_ref, bg_ref, bv_ref, o_ref, wing, winv):
        def step(i, _):
            base = pl.multiple_of(i * ROWS, ROWS)
            lo = pl.multiple_of(jnp.maximum(base - HALO, 0), HALO)
            keep = jnp.where(i > 0, 1.0, 0.0)
            eg = _window(wing, jnp.concatenate([g_ref[pl.ds(lo, HALO), :] * keep, g_ref[pl.ds(base, ROWS), :]], axis=0))
            ev = _window(winv, jnp.concatenate([v_ref[pl.ds(lo, HALO), :] * keep, v_ref[pl.ds(base, ROWS), :]], axis=0))
            gate = jnp.zeros((ROWS, c), F32) + bg_ref[...]
            val = jnp.zeros((ROWS, c), F32) + bv_ref[...]
            for j in range(kw):
                gate = gate + wg_ref[j:j + 1, :] * _rows(eg, HALO - (kw - 1) + j, ROWS)
                val = val + wv_ref[j:j + 1, :] * _rows(ev, HALO - (kw - 1) + j, ROWS)
            o_ref[pl.ds(base, ROWS), :] = (gate * _sigmoid(gate) * val).astype(BF16)
            return 0
        lax.fori_loop(0, _steps(t), step, 0)

    return _pcall(body, carry_us=ACT_FWD_US_PER_ELEM * t * f, grid=(ns,),
                  in_specs=[pl.BlockSpec((t, c), lambda s: (0, s)), pl.BlockSpec((t, c), lambda s: (0, s + ns)),
                            pl.BlockSpec((kw, c), lambda s: (0, s)), pl.BlockSpec((kw, c), lambda s: (0, s + ns)),
                            pl.BlockSpec((1, c), lambda s: (0, s)), pl.BlockSpec((1, c), lambda s: (0, s + ns))],
                  out_specs=pl.BlockSpec((t, c), lambda s: (0, s)), out_shape=jax.ShapeDtypeStruct((t, f), BF16),
                  scratch_shapes=[pltpu.VMEM((ROWS + HALO, c), F32), pltpu.VMEM((ROWS + HALO, c), F32)],
                  compiler_params=_params(("parallel",)), name=name)(u0, u0, dw_w, dw_w, dw_b, dw_b)


def _ffn_act_bwd(da, u0, dw_w, dw_b, name):
    t, f = da.shape
    c = LANES
    ns = f // c
    kw = dw_w.shape[0]

    def body(da_ref, g_ref, v_ref, wg_ref, wv_ref, bg_ref, bv_ref, du_ref, dww_ref, dwb_ref, padg, padv, accw, accb, wing, winv):
        padg[t:t + HALO, :] = jnp.zeros((HALO, c), F32)
        padv[t:t + HALO, :] = jnp.zeros((HALO, c), F32)
        accw[...] = jnp.zeros_like(accw)
        accb[...] = jnp.zeros_like(accb)

        def first(i, _):
            base = pl.multiple_of(i * ROWS, ROWS)
            lo = pl.multiple_of(jnp.maximum(base - HALO, 0), HALO)
            keep = jnp.where(i > 0, 1.0, 0.0)
            eg = _window(wing, jnp.concatenate([g_ref[pl.ds(lo, HALO), :] * keep, g_ref[pl.ds(base, ROWS), :]], axis=0))
            ev = _window(winv, jnp.concatenate([v_ref[pl.ds(lo, HALO), :] * keep, v_ref[pl.ds(base, ROWS), :]], axis=0))
            gate = jnp.zeros((ROWS, c), F32) + bg_ref[...]
            val = jnp.zeros((ROWS, c), F32) + bv_ref[...]
            for j in range(kw):
                gate = gate + wg_ref[j:j + 1, :] * _rows(eg, HALO - (kw - 1) + j, ROWS)
                val = val + wv_ref[j:j + 1, :] * _rows(ev, HALO - (kw - 1) + j, ROWS)
            dav = da_ref[pl.ds(base, ROWS), :]
            sg = _sigmoid(gate)
            dgate = dav * val * (sg * (1.0 + gate * (1.0 - sg)))
            dval = dav * (gate * sg)
            padg[pl.ds(base, ROWS), :] = dgate
            padv[pl.ds(base, ROWS), :] = dval
            for j in range(kw):
                accw[j] = accw[j] + _fold8(dgate * _rows(eg, HALO - (kw - 1) + j, ROWS))
                accw[kw + j] = accw[kw + j] + _fold8(dval * _rows(ev, HALO - (kw - 1) + j, ROWS))
            accb[0] = accb[0] + _fold8(dgate)
            accb[1] = accb[1] + _fold8(dval)
            return 0
        lax.fori_loop(0, _steps(t), first, 0)

        def second(i, _):
            base = pl.multiple_of(i * ROWS, ROWS)
            eg = _window(wing, padg[pl.ds(base, ROWS + HALO), :])
            ev = _window(winv, padv[pl.ds(base, ROWS + HALO), :])
            dg = jnp.zeros((ROWS, c), F32)
            dv = jnp.zeros((ROWS, c), F32)
            for j in range(kw):
                dg = dg + wg_ref[j:j + 1, :] * _rows(eg, kw - 1 - j, ROWS)
                dv = dv + wv_ref[j:j + 1, :] * _rows(ev, kw - 1 - j, ROWS)
            du_ref[0, pl.ds(base, ROWS), :] = dg.astype(BF16)
            du_ref[1, pl.ds(base, ROWS), :] = dv.astype(BF16)
            return 0
        lax.fori_loop(0, _steps(t), second, 0)
        for j in range(kw):
            dww_ref[0, j:j + 1, :] = jnp.sum(accw[j], axis=0, keepdims=True)
            dww_ref[1, j:j + 1, :] = jnp.sum(accw[kw + j], axis=0, keepdims=True)
        dwb_ref[0] = jnp.sum(accb[0], axis=0, keepdims=True)
        dwb_ref[1] = jnp.sum(accb[1], axis=0, keepdims=True)

    return _pcall(body, grid=(ns,),
                  in_specs=[pl.BlockSpec((t, c), lambda s: (0, s)),
                            pl.BlockSpec((t, c), lambda s: (0, s)), pl.BlockSpec((t, c), lambda s: (0, s + ns)),
                            pl.BlockSpec((kw, c), lambda s: (0, s)), pl.BlockSpec((kw, c), lambda s: (0, s + ns)),
                            pl.BlockSpec((1, c), lambda s: (0, s)), pl.BlockSpec((1, c), lambda s: (0, s + ns))],
                  out_specs=[pl.BlockSpec((2, t, c), lambda s: (0, 0, s)), pl.BlockSpec((2, kw, c), lambda s: (0, 0, s)),
                             pl.BlockSpec((2, 1, c), lambda s: (0, 0, s))],
                  out_shape=[jax.ShapeDtypeStruct((2, t, f), BF16), jax.ShapeDtypeStruct((2, kw, f), F32),
                             jax.ShapeDtypeStruct((2, 1, f), F32)],
                  scratch_shapes=[pltpu.VMEM((t + HALO, c), F32), pltpu.VMEM((t + HALO, c), F32),
                                  pltpu.VMEM((2 * kw, 8, c), F32), pltpu.VMEM((2, 8, c), F32),
                                  pltpu.VMEM((ROWS + HALO, c), F32), pltpu.VMEM((ROWS + HALO, c), F32)],
                  compiler_params=_params(("parallel",)), name=name)(da, u0, u0, dw_w, dw_w, dw_b, dw_b)


def _window_of(group):
    w = jnp.float32(POOL_WINDOWS[-1])
    for k in range(len(POOL_WINDOWS) - 2, -1, -1):
        w = jnp.where(group == k, jnp.float32(POOL_WINDOWS[k]), w)
    return w


def _select_level(group, levels):
    out = levels[-1]
    for k in range(len(levels) - 2, -1, -1):
        out = jnp.where(group == k, levels[k], out)
    return out


def _pool_fwd(h, name):
    t, d = h.shape
    c = LANES
    per = d // len(POOL_WINDOWS) // c

    def body(h_ref, o_ref):
        group = pl.program_id(0)
        wf = _window_of(group)

        def step(i, _):
            base = pl.multiple_of(i * ROWS, ROWS)
            lo = pl.multiple_of(jnp.maximum(base - HALO, 0), HALO)
            keep = jnp.where(i > 0, 1.0, 0.0)
            cur = h_ref[pl.ds(base, ROWS), :]
            e = jnp.concatenate([h_ref[pl.ds(lo, HALO), :] * keep, cur], axis=0)
            n = ROWS + HALO
            levels = []
            s = e
            for k in range(len(POOL_WINDOWS)):
                s = s + pltpu.roll(s, 1 << k, 0)
                levels.append(s[HALO:n])
            tpos = (base + lax.broadcasted_iota(jnp.int32, (ROWS, c), 0) + 1).astype(F32)
            pooled = _select_level(group, levels) / jnp.minimum(tpos, wf)
            o_ref[pl.ds(base, ROWS), :] = (pooled - cur).astype(BF16)
            return 0
        lax.fori_loop(0, _steps(t), step, 0)

    return _pcall(body, grid=(len(POOL_WINDOWS), per), in_specs=[pl.BlockSpec((t, c), lambda g, s: (0, g * per + s))],
                  out_specs=pl.BlockSpec((t, c), lambda g, s: (0, g * per + s)), out_shape=jax.ShapeDtypeStruct((t, d), BF16),
                  compiler_params=_params(("parallel", "parallel")), name=name)(h)


def _pool_bwd(dm, name):
    t, d = dm.shape
    c = LANES
    per = d // len(POOL_WINDOWS) // c

    def body(dm_ref, o_ref, pad):
        group = pl.program_id(0)
        wf = _window_of(group)
        pad[t:t + HALO, :] = jnp.zeros((HALO, c), F32)

        def fill(i, _):
            base = pl.multiple_of(i * ROWS, ROWS)
            tpos = (base + lax.broadcasted_iota(jnp.int32, (ROWS, c), 0) + 1).astype(F32)
            pad[pl.ds(base, ROWS), :] = dm_ref[pl.ds(base, ROWS), :] / jnp.minimum(tpos, wf)
            return 0
        lax.fori_loop(0, _steps(t), fill, 0)

        def step(i, _):
            base = pl.multiple_of(i * ROWS, ROWS)
            n = ROWS + HALO
            s = pad[pl.ds(base, n), :]
            levels = []
            for k in range(len(POOL_WINDOWS)):
                s = s + pltpu.roll(s, n - (1 << k), 0)
                levels.append(s[0:ROWS])
            o_ref[pl.ds(base, ROWS), :] = _select_level(group, levels) - dm_ref[pl.ds(base, ROWS), :]
            return 0
        lax.fori_loop(0, _steps(t), step, 0)

    return _pcall(body, grid=(len(POOL_WINDOWS), per), in_specs=[pl.BlockSpec((t, c), lambda g, s: (0, g * per + s))],
                  out_specs=pl.BlockSpec((t, c), lambda g, s: (0, g * per + s)), out_shape=jax.ShapeDtypeStruct((t, d), F32),
                  scratch_shapes=[pltpu.VMEM((t + HALO, c), F32)], compiler_params=_params(("parallel", "parallel")), name=name)(dm)


def _pool_mm_fwd(mixed, wg, scale, res, name):
    t, d = mixed.shape
    ng, gd, _ = wg.shape
    tm = _pick(t, 1024, 16)

    def body(a_ref, w_ref, s_ref, r_ref, o_ref):
        y = jnp.dot(a_ref[...], w_ref[...], preferred_element_type=F32)
        o_ref[...] = r_ref[...] + y * s_ref[...]

    blk = pl.BlockSpec((tm, gd), lambda g, i: (i, g))
    return _pcall(body, grid=(ng, t // tm),
                  in_specs=[blk, pl.BlockSpec((None, gd, gd), lambda g, i: (g, 0, 0)), pl.BlockSpec((1, gd), lambda g, i: (0, g)), blk],
                  out_specs=blk, out_shape=jax.ShapeDtypeStruct((t, d), F32),
                  compiler_params=_params(("parallel", "parallel")), name=name)(mixed, wg, scale, res)


def _pool_mm_bwd(dy, mixed, wg, scale, name):
    t, d = mixed.shape
    ng, gd, _ = wg.shape
    tm = _pick(t, 1024, 16)

    def body(dy_ref, a_ref, w_ref, s_ref, dm_ref, dw_ref, ds_ref):
        a, w, dyv = a_ref[...], w_ref[...], dy_ref[...]
        y = jnp.dot(a, w, preferred_element_type=F32)
        dyp = (dyv * s_ref[...]).astype(BF16)
        dm_ref[...] = lax.dot_general(dyp, w, (((1,), (1,)), ((), ())), preferred_element_type=F32)
        pw = lax.dot_general(a, dyp, (((0,), (0,)), ((), ())), preferred_element_type=F32)
        ps = jnp.sum(dyv * y, axis=0, keepdims=True)

        @pl.when(pl.program_id(1) == 0)
        def _():
            dw_ref[...] = pw
            ds_ref[...] = ps

        @pl.when(pl.program_id(1) > 0)
        def _():
            dw_ref[...] += pw
            ds_ref[...] += ps

    blk = pl.BlockSpec((tm, gd), lambda g, i: (i, g))
    wsp = pl.BlockSpec((None, gd, gd), lambda g, i: (g, 0, 0))
    vec = pl.BlockSpec((1, gd), lambda g, i: (0, g))
    return _pcall(body, grid=(ng, t // tm), in_specs=[blk, blk, wsp, vec], out_specs=[blk, wsp, vec],
                  out_shape=[jax.ShapeDtypeStruct((t, d), F32), jax.ShapeDtypeStruct((ng, gd, gd), F32), jax.ShapeDtypeStruct((1, d), F32)],
                  compiler_params=_params(("parallel", "arbitrary")), name=name)(dy, mixed, wg, scale)


def _rope_tables(positions):
    half = ROT_DIM // 2
    inv_freq = ROPE_THETA ** (-jnp.arange(0, ROT_DIM, 2, dtype=F32) / ROT_DIM)
    ang = positions.astype(F32)[:, None] * inv_freq
    cos, sin = jnp.cos(ang), jnp.sin(ang)
    t = positions.shape[0]
    ones = jnp.ones((t, HEAD - ROT_DIM), F32)
    zeros = jnp.zeros((t, HEAD - ROT_DIM), F32)
    zh = jnp.zeros((t, half), F32)
    c = jnp.concatenate([cos, cos, ones], axis=1)
    s1 = jnp.concatenate([-sin, zh, zeros], axis=1)
    s2 = jnp.concatenate([zh, sin, zeros], axis=1)
    return tuple(jnp.concatenate([a, a], axis=1) for a in (c, s1, s2))


def _half_mean(v, lo):
    s_lo = jnp.sum(jnp.where(lo, v, 0.0), axis=-1, keepdims=True)
    s_hi = jnp.sum(jnp.where(lo, 0.0, v), axis=-1, keepdims=True)
    return jnp.where(lo, s_lo, s_hi) * (1.0 / HEAD)


def _qk_prep_fwd(qkv, tabs, gq2, gk2, n_q, n_kv, name):
    t, width = qkv.shape
    tm = _pick(t, 256, 16)
    nqc, nkc = n_q * HEAD // LANES, n_kv * HEAD // LANES

    def body(x_ref, c_ref, s1_ref, s2_ref, gq_ref, gk_ref, q_ref, k2_ref, v2_ref):
        lo = lax.broadcasted_iota(jnp.int32, (tm, LANES), 1) < HEAD
        cv, s1, s2 = c_ref[...], s1_ref[...], s2_ref[...]

        def normrot(xc, g2):
            y = xc * lax.rsqrt(_half_mean(xc * xc, lo) + EPS) * g2
            return y * cv + pltpu.roll(y, LANES - ROT_DIM // 2, 1) * s1 + pltpu.roll(y, ROT_DIM // 2, 1) * s2

        def twice(y, j):
            sw = pltpu.roll(y, HEAD, 1)
            k2 = jnp.where(lo, y, sw) if j == 0 else jnp.where(lo, sw, y)
            return k2.astype(BF16)

        for ch in range(nqc):
            q_ref[:, ch * LANES:(ch + 1) * LANES] = normrot(x_ref[:, ch * LANES:(ch + 1) * LANES], gq_ref[...]).astype(BF16)
        for ch in range(nkc):
            off = (nqc + ch) * LANES
            y = normrot(x_ref[:, off:off + LANES], gk_ref[...])
            voff = (nqc + nkc + ch) * LANES
            vv = x_ref[:, voff:voff + LANES]
            for j in range(2):
                k2_ref[:, (2 * ch + j) * LANES:(2 * ch + j + 1) * LANES] = twice(y, j)
                v2_ref[:, (2 * ch + j) * LANES:(2 * ch + j + 1) * LANES] = twice(vv, j)

    row = lambda w: pl.BlockSpec((tm, w), lambda i: (i, 0))
    vec = pl.BlockSpec((1, LANES), lambda i: (0, 0))
    return _pcall(body, grid=(t // tm,), in_specs=[row(width), row(LANES), row(LANES), row(LANES), vec, vec],
                  out_specs=[row(n_q * HEAD), row(n_kv * LANES), row(n_kv * LANES)],
                  out_shape=[jax.ShapeDtypeStruct((t, n_q * HEAD), BF16), jax.ShapeDtypeStruct((t, n_kv * LANES), BF16),
                             jax.ShapeDtypeStruct((t, n_kv * LANES), BF16)],
                  compiler_params=_params(("parallel",)), name=name)(qkv, *tabs, gq2, gk2)


def _qk_prep_bwd(dq, dk_cur, dk_prev, dv_cur, dv_prev, qkv, tabs, gq2, gk2, n_q, n_kv, name):
    t, width = qkv.shape
    tm = Q_BLOCK
    nb = t // tm
    nqc, nkc = n_q * HEAD // LANES, n_kv * HEAD // LANES

    def body(dq_ref, kc_ref, kp_ref, vc_ref, vp_ref, x_ref, c_ref, s1_ref, s2_ref, gq_ref, gk_ref, o_ref, dgq_ref, dgk_ref):
        lo = lax.broadcasted_iota(jnp.int32, (tm, LANES), 1) < HEAD
        cv, s1, s2 = c_ref[...], s1_ref[...], s2_ref[...]
        more = jnp.where(pl.program_id(0) < nb - 1, 1.0, 0.0)

        def back(dy, xc, g2):
            dyn = dy * cv + pltpu.roll(dy * s1, ROT_DIM // 2, 1) + pltpu.roll(dy * s2, LANES - ROT_DIM // 2, 1)
            r = lax.rsqrt(_half_mean(xc * xc, lo) + EPS)
            xh = xc * r
            dxh = dyn * g2
            return r * (dxh - xh * _half_mean(dxh * xh, lo)), jnp.sum(dyn * xh, axis=0, keepdims=True)

        def unfold(cur_ref, prev_ref, ch):
            d0 = cur_ref[:, (2 * ch) * LANES:(2 * ch + 1) * LANES] + more * prev_ref[:, (2 * ch) * LANES:(2 * ch + 1) * LANES]
            d1 = cur_ref[:, (2 * ch + 1) * LANES:(2 * ch + 2) * LANES] + more * prev_ref[:, (2 * ch + 1) * LANES:(2 * ch + 2) * LANES]
            return jnp.where(lo, d0 + pltpu.roll(d0, HEAD, 1), d1 + pltpu.roll(d1, HEAD, 1))

        pq = jnp.zeros((1, LANES), F32)
        for ch in range(nqc):
            sl = slice(ch * LANES, (ch + 1) * LANES)
            dx, pg = back(dq_ref[:, sl], x_ref[:, sl], gq_ref[...])
            o_ref[:, sl] = dx.astype(BF16)
            pq = pq + pg
        pk = jnp.zeros((1, LANES), F32)
        for ch in range(nkc):
            sl = slice((nqc + ch) * LANES, (nqc + ch + 1) * LANES)
            dx, pg = back(unfold(kc_ref, kp_ref, ch), x_ref[:, sl], gk_ref[...])
            o_ref[:, sl] = dx.astype(BF16)
            pk = pk + pg
            vs = slice((nqc + nkc + ch) * LANES, (nqc + nkc + ch + 1) * LANES)
            o_ref[:, vs] = unfold(vc_ref, vp_ref, ch).astype(BF16)

        @pl.when(pl.program_id(0) == 0)
        def _():
            dgq_ref[...] = pq
            dgk_ref[...] = pk

        @pl.when(pl.program_id(0) > 0)
        def _():
            dgq_ref[...] += pq
            dgk_ref[...] += pk

    row = lambda w: pl.BlockSpec((tm, w), lambda i: (i, 0))
    nxt = lambda w: pl.BlockSpec((tm, w), lambda i: (jnp.minimum(i + 1, nb - 1), 0))
    vec = pl.BlockSpec((1, LANES), lambda i: (0, 0))
    kvw = n_kv * LANES
    return _pcall(body, grid=(nb,),
                  in_specs=[row(n_q * HEAD), row(kvw), nxt(kvw), row(kvw), nxt(kvw), row(width), row(LANES), row(LANES), row(LANES), vec, vec],
                  out_specs=[row(width), vec, vec],
                  out_shape=[jax.ShapeDtypeStruct((t, width), BF16), jax.ShapeDtypeStruct((1, LANES), F32), jax.ShapeDtypeStruct((1, LANES), F32)],
                  compiler_params=_params(("arbitrary",)), name=name)(dq, dk_cur, dk_prev, dv_cur, dv_prev, qkv, *tabs, gq2, gk2)


def _band_scores(qh, kc, kp, n, sink_row, lo_row, is_lo):
    scale = 1.0 / math.sqrt(HEAD)
    nt = (((1,), (1,)), ((), ()))
    s_c = lax.dot_general(qh, kc, nt, preferred_element_type=F32) * scale
    s_p = lax.dot_general(qh, kp, nt, preferred_element_type=F32) * scale
    qi = lax.broadcasted_iota(jnp.int32, (Q_BLOCK, Q_BLOCK), 0)
    kj = lax.broadcasted_iota(jnp.int32, (Q_BLOCK, Q_BLOCK), 1)
    s_c = jnp.where(kj <= qi, s_c, -jnp.inf)
    s_p = jnp.where((kj > qi) & (n > 0), s_p, -jnp.inf)
    pick = lo_row if is_lo else jnp.logical_not(lo_row)
    sink = jnp.max(jnp.where(pick, sink_row, -jnp.inf), axis=-1, keepdims=True)
    return s_c, s_p, sink


def _attn_fwd(q, k2, v2, sink_tab, name):
    t, dq = q.shape
    nc = dq // LANES
    nb = t // Q_BLOCK
    nkv = k2.shape[1] // LANES
    per_kv = nc // nkv

    def body(q_ref, kc_ref, kp_ref, vc_ref, vp_ref, s_ref, o_ref, lse_ref):
        n = pl.program_id(1)
        lo = lax.broadcasted_iota(jnp.int32, (Q_BLOCK, LANES), 1) < HEAD
        lo_row = lax.broadcasted_iota(jnp.int32, (1, LANES), 1) < HEAD
        kc, kp, vc, vp = kc_ref[...], kp_ref[...], vc_ref[...], vp_ref[...]
        for cc in range(per_kv):
            cols = slice(cc * LANES, (cc + 1) * LANES)
            qv = q_ref[:, cols].astype(F32)
            outs, lses = [], []
            for is_lo in (True, False):
                qh = jnp.where(lo, qv, 0.0) if is_lo else jnp.where(lo, 0.0, qv)
                s_c, s_p, sink = _band_scores(qh.astype(BF16), kc, kp, n, s_ref[8 * cc:8 * cc + 1, :], lo_row, is_lo)
                m = jnp.maximum(jnp.maximum(jnp.max(s_c, axis=-1, keepdims=True), jnp.max(s_p, axis=-1, keepdims=True)), sink)
                p_c, p_p = jnp.exp(s_c - m), jnp.exp(s_p - m)
                denom = jnp.sum(p_c, axis=-1, keepdims=True) + jnp.sum(p_p, axis=-1, keepdims=True) + jnp.exp(sink - m)
                pv = jnp.dot(p_c.astype(BF16), vc, preferred_element_type=F32) + jnp.dot(p_p.astype(BF16), vp, preferred_element_type=F32)
                outs.append(pv / denom)
                lses.append(m + jnp.log(denom))
            o_ref[:, cols] = jnp.where(lo, outs[0], outs[1]).astype(BF16)
            lse_ref[cc] = jnp.where(lo, lses[0], lses[1])

    qs = pl.BlockSpec((Q_BLOCK, per_kv * LANES), lambda k, n: (n, k))
    cur = pl.BlockSpec((Q_BLOCK, LANES), lambda k, n: (n, k))
    prev = pl.BlockSpec((Q_BLOCK, LANES), lambda k, n: (jnp.maximum(n - 1, 0), k))
    return _pcall(body, carry_us=ATTN_US_PER_STEP * nkv * nb, grid=(nkv, nb),
                  in_specs=[qs, cur, prev, cur, prev, pl.BlockSpec((8 * per_kv, LANES), lambda k, n: (k, 0))],
                  out_specs=[qs, pl.BlockSpec((per_kv, Q_BLOCK, LANES), lambda k, n: (k, n, 0))],
                  out_shape=[jax.ShapeDtypeStruct((t, dq), BF16), jax.ShapeDtypeStruct((nc, t, LANES), F32)],
                  compiler_params=_params(("parallel", "parallel")), name=name)(q, k2, k2, v2, v2, sink_tab)


def _attn_bwd(do, q, o, lse, k2, v2, sink_tab, name):
    t, dq = q.shape
    nc = dq // LANES
    nb = t // Q_BLOCK
    nkv = k2.shape[1] // LANES
    per_kv = nc // nkv
    scale = 1.0 / math.sqrt(HEAD)
    tn_ = (((0,), (0,)), ((), ()))
    nt = (((1,), (1,)), ((), ()))

    def body(do_ref, q_ref, o_ref, lse_ref, kc_ref, kp_ref, vc_ref, vp_ref, s_ref,
             dq_ref, dkc_ref, dkp_ref, dvc_ref, dvp_ref, dsk_ref):
        n = pl.program_id(1)
        lo = lax.broadcasted_iota(jnp.int32, (Q_BLOCK, LANES), 1) < HEAD
        lo_row = lax.broadcasted_iota(jnp.int32, (1, LANES), 1) < HEAD
        kc, kp, vc, vp = kc_ref[...], kp_ref[...], vc_ref[...], vp_ref[...]
        dkc = dkp = dvc = dvp = None
        for cc in range(per_kv):
            cols = slice(cc * LANES, (cc + 1) * LANES)
            qv, dov, ov, lsev = q_ref[:, cols].astype(F32), do_ref[:, cols], o_ref[:, cols].astype(F32), lse_ref[cc]
            dqs, dsinks = [], []
            for is_lo in (True, False):
                half = lo if is_lo else jnp.logical_not(lo)
                qh = jnp.where(half, qv, 0.0).astype(BF16)
                doh = jnp.where(half, dov, 0.0)
                s_c, s_p, sink = _band_scores(qh, kc, kp, n, s_ref[8 * cc:8 * cc + 1, :], lo_row, is_lo)
                lse_h = jnp.max(jnp.where(half, lsev, -jnp.inf), axis=-1, keepdims=True)
                p_c, p_p = jnp.exp(s_c - lse_h), jnp.exp(s_p - lse_h)
                delta = jnp.sum(doh * ov, axis=-1, keepdims=True)
                dob = doh.astype(BF16)
                ds_c = (p_c * (lax.dot_general(dob, vc, nt, preferred_element_type=F32) - delta)).astype(BF16)
                ds_p = (p_p * (lax.dot_general(dob, vp, nt, preferred_element_type=F32) - delta)).astype(BF16)
                dsinks.append(-jnp.sum(jnp.exp(sink - lse_h) * delta, axis=0, keepdims=True))
                dqs.append((jnp.dot(ds_c, kc, preferred_element_type=F32) + jnp.dot(ds_p, kp, preferred_element_type=F32)) * scale)
                parts = (lax.dot_general(ds_c, qh, tn_, preferred_element_type=F32) * scale,
                         lax.dot_general(ds_p, qh, tn_, preferred_element_type=F32) * scale,
                         lax.dot_general(p_c.astype(BF16), dob, tn_, preferred_element_type=F32),
                         lax.dot_general(p_p.astype(BF16), dob, tn_, preferred_element_type=F32))
                if dkc is None:
                    dkc, dkp, dvc, dvp = parts
                else:
                    dkc, dkp, dvc, dvp = dkc + parts[0], dkp + parts[1], dvc + parts[2], dvp + parts[3]
            dq_ref[:, cols] = jnp.where(lo, dqs[0], dqs[1])
            dsk_ref[8 * cc:8 * cc + 8, :] = jnp.zeros((8, LANES), F32) + jnp.where(lo_row, dsinks[0], dsinks[1])
        dkc_ref[...] = dkc
        dkp_ref[...] = dkp
        dvc_ref[...] = dvc
        dvp_ref[...] = dvp

    qs = pl.BlockSpec((Q_BLOCK, per_kv * LANES), lambda k, n: (n, k))
    cur = pl.BlockSpec((Q_BLOCK, LANES), lambda k, n: (n, k))
    prev = pl.BlockSpec((Q_BLOCK, LANES), lambda k, n: (jnp.maximum(n - 1, 0), k))
    kv_shape = jax.ShapeDtypeStruct((t, nkv * LANES), F32)
    return _pcall(body, carry_us=ATTN_US_PER_STEP * nkv * nb, grid=(nkv, nb),
                  in_specs=[qs, qs, qs, pl.BlockSpec((per_kv, Q_BLOCK, LANES), lambda k, n: (k, n, 0)),
                            cur, prev, cur, prev, pl.BlockSpec((8 * per_kv, LANES), lambda k, n: (k, 0))],
                  out_specs=[qs, cur, cur, cur, cur, pl.BlockSpec((None, None, 8 * per_kv, LANES), lambda k, n: (k, n, 0, 0))],
                  out_shape=[jax.ShapeDtypeStruct((t, dq), F32), kv_shape, kv_shape, kv_shape, kv_shape,
                             jax.ShapeDtypeStruct((nkv, nb, 8 * per_kv, LANES), F32)],
                  compiler_params=_params(("parallel", "parallel")), name=name)(do, q, o, lse, k2, k2, v2, v2, sink_tab)


def _peer(k):
    x, y, c = lax.axis_index("x"), lax.axis_index("y"), lax.axis_index("c")
    flip = lambda v, bit: 1 - v if bit else v
    return (flip(x, k & 4), flip(y, k & 2), flip(c, k & 1))


def _my_index():
    return 4 * lax.axis_index("x") + 2 * lax.axis_index("y") + lax.axis_index("c")


def _peer_index(k):
    px, py, pc = _peer(k)
    return 4 * px + 2 * py + pc


def _all_gather(shards, name):
    n = len(shards)
    any_spec = pl.BlockSpec(memory_space=pl.ANY)

    def body(*refs):
        ins, outs = refs[:n], refs[n:2 * n]
        send_sems, recv_sems, local_sems = refs[2 * n:]
        me = _my_index()
        local = [pltpu.make_async_copy(ins[a], outs[a].at[me], local_sems.at[a]) for a in range(n)]
        for cp in local:
            cp.start()
        sends = []
        for k in range(1, N_DEV):
            for a in range(n):
                cp = pltpu.make_async_remote_copy(src_ref=ins[a], dst_ref=outs[a].at[me], send_sem=send_sems.at[a, k - 1],
                                                  recv_sem=recv_sems.at[a, k - 1], device_id=_peer(k), device_id_type=MESH_ID)
                cp.start()
                sends.append(cp)
        for k in range(1, N_DEV):
            for a in range(n):
                pltpu.make_async_remote_copy(src_ref=ins[a], dst_ref=outs[a].at[_peer_index(k)], send_sem=send_sems.at[a, k - 1],
                                             recv_sem=recv_sems.at[a, k - 1], device_id=_peer(k), device_id_type=MESH_ID).wait_recv()
        for cp in sends:
            cp.wait_send()
        for cp in local:
            cp.wait()

    return _pcall(body, in_specs=[any_spec] * n, out_specs=[any_spec] * n,
                  out_shape=[jax.ShapeDtypeStruct((N_DEV,) + s.shape, s.dtype) for s in shards],
                  scratch_shapes=[pltpu.SemaphoreType.DMA((n, N_DEV - 1)), pltpu.SemaphoreType.DMA((n, N_DEV - 1)),
                                  pltpu.SemaphoreType.DMA((n,))],
                  name=name)(*shards)


GATHER1_PEERS = (1, 2, 4, 6)
GATHER2_PEERS = (2, 4, 6)
SCATTER_PEERS = tuple(range(1, N_DEV))
MAX_SEMS = N_DEV - 1
MAX_JOBS = 6
US_PER_MB = {"gather1": 5.4, "gather2": 0.6, "scatter": 10.8}
SCATTER_PIECE_US = 110.0


class _Job:
    def __init__(self, key, kind, src, lo=0, hi=None, dst=None):
        self.key, self.kind, self.src, self.dst = key, kind, src, dst
        shape = src.shape if kind != "gather1" else (N_DEV,) + src.shape
        self.out_shape = jax.ShapeDtypeStruct(shape, src.dtype)
        self.rows = shape[1]
        self.lo, self.hi = lo, self.rows if hi is None else hi
        self.row_us = US_PER_MB[kind] * math.prod(shape) * src.dtype.itemsize / 1e6 / self.rows
        pieces = max(1, round(self.row_us * self.rows / SCATTER_PIECE_US)) if kind == "scatter" else 1
        while pieces > 1 and self.rows % (16 * pieces):
            pieces -= 1
        self.piece = self.rows // pieces

    @property
    def cost_us(self):
        return self.row_us * (self.hi - self.lo)


class _Comm:
    def __init__(self):
        self.queue, self.gathered, self.scattered, self.layer = [], {}, [], 0

    def push(self, key, kind, src):
        self.queue.append(_Job(key, kind, src))

    def take(self, budget_us):
        jobs = [j for j in self.queue if j.kind == "gather2"][:MAX_JOBS]
        used = sum(j.cost_us for j in jobs)
        for j in [j for j in self.queue if j.kind != "gather2"]:
            if len(jobs) >= MAX_JOBS:
                break
            if j.kind == "gather1":
                urgent = int(j.key[1]) <= self.layer
                if (used >= budget_us) if urgent else (used + 0.5 * j.cost_us > budget_us):
                    break
                jobs.append(j)
                used += j.cost_us
                continue
            n = 0
            while j.lo + (n + 1) * j.piece <= j.hi and used + 0.5 * j.row_us * j.piece <= budget_us:
                n += 1
                used += j.row_us * j.piece
            if n == 0:
                break
            part = _Job(j.key, "scatter", j.src, j.lo, j.lo + n * j.piece, j.dst)
            part.parent = j
            j.lo = part.hi
            jobs.append(part)
            if j.lo < j.hi:
                break
        self.queue = [j for j in self.queue if j not in jobs and j.lo < j.hi]
        return jobs

    def finish(self, job, result):
        if job.kind == "gather1":
            self.queue.insert(0, _Job(job.key, "gather2", result))
        elif job.kind == "gather2":
            self.gathered[job.key] = result
        elif job.hi == job.rows:
            self.scattered.append((job.key, result))
        else:
            job.parent.dst = result

    def need(self, key):
        while key not in self.gathered:
            assert any(j.key == key for j in self.queue), key
            jobs = [j for j in self.queue if j.kind == "gather2"][:MAX_JOBS]
            if not any(j.key == key for j in jobs):
                for j in self.queue:
                    if j.kind == "gather1" and len(jobs) < MAX_JOBS:
                        jobs.append(j)
                        if j.key == key:
                            break
            self.flush(jobs)
        return self.gathered[key]

    def flush(self, jobs):
        self.queue = [j for j in self.queue if j not in jobs]

        def body(o_ref):
            o_ref[...] = jnp.zeros_like(o_ref)
        _carry(body, jobs, self, dict(in_specs=[], out_specs=pl.BlockSpec(memory_space=pltpu.VMEM),
                                      out_shape=jax.ShapeDtypeStruct((8, LANES), F32), name="exchange"))()


def _job_copies(job, src, dst, send_sems, recv_sems, local_sem):
    me = _my_index()
    peers = {"gather1": GATHER1_PEERS, "gather2": GATHER2_PEERS, "scatter": SCATTER_PEERS}[job.kind]
    sends, recvs = [], []
    for i, k in enumerate(peers):
        if job.kind == "gather1":
            s_ref, d_ref, to, got = src, dst.at[me], _peer(k), dst.at[_peer_index(k)]
        elif job.kind == "gather2":
            s_ref, d_ref, to, got = src.at[_peer_index(k)], dst.at[_peer_index(k)], _peer(1), dst.at[_peer_index(k | 1)]
        else:
            rows = pl.ds(job.lo, job.hi - job.lo)
            s_ref, d_ref, to, got = src.at[_peer_index(k), rows], dst.at[me, rows], _peer(k), dst.at[_peer_index(k), rows]
        sends.append(pltpu.make_async_remote_copy(src_ref=s_ref, dst_ref=d_ref, send_sem=send_sems.at[i], recv_sem=recv_sems.at[i],
                                                  device_id=to, device_id_type=MESH_ID))
        recvs.append(pltpu.make_async_remote_copy(src_ref=s_ref, dst_ref=got, send_sem=send_sems.at[i], recv_sem=recv_sems.at[i],
                                                  device_id=to, device_id_type=MESH_ID))
    local = None
    if job.kind == "gather1":
        local = pltpu.make_async_copy(src, dst.at[me], local_sem)
    elif job.kind == "scatter":
        rows = pl.ds(job.lo, job.hi - job.lo)
        local = pltpu.make_async_copy(src.at[me, rows], dst.at[me, rows], local_sem)
    return sends, recvs, local


def _carry(body, jobs, comm, kw):
    kw = dict(kw)
    grid = tuple(kw.get("grid", ()))
    in_specs = list(kw["in_specs"])
    single = not isinstance(kw["out_specs"], (list, tuple))
    out_specs = [kw["out_specs"]] if single else list(kw["out_specs"])
    out_shape = [kw["out_shape"]] if single else list(kw["out_shape"])
    scratch = list(kw.get("scratch_shapes", []))
    n_in, n_out, n_scr, nj = len(in_specs), len(out_specs), len(scratch), len(jobs)
    any_spec = pl.BlockSpec(memory_space=pl.ANY)
    landed = [a for a, job in enumerate(jobs) if job.dst is not None]
    n_land = len(landed)

    def wrapped(*refs):
        pos = 0

        def take(k):
            nonlocal pos
            part = refs[pos:pos + k]
            pos += k
            return part
        ins, rin, _, outs, rout, scr = take(n_in), take(nj), take(n_land), take(n_out), take(nj), take(n_scr)
        send_sems, recv_sems, local_sems = take(3)

        def copies():
            return [_job_copies(job, rin[a], rout[a], send_sems.at[a], recv_sems.at[a], local_sems.at[a]) for a, job in enumerate(jobs)]

        def start():
            for sends, _, local in copies():
                if local is not None:
                    local.start()
                for cp in sends:
                    cp.start()

        def finish():
            for sends, recvs, local in copies():
                for cp in recvs:
                    cp.wait_recv()
                for cp in sends:
                    cp.wait_send()
                if local is not None:
                    local.wait()

        if grid:
            first = functools.reduce(jnp.logical_and, [pl.program_id(a) == 0 for a in range(len(grid))])
            last = functools.reduce(jnp.logical_and, [pl.program_id(a) == grid[a] - 1 for a in range(len(grid))])
            pl.when(first)(start)
            body(*ins, *outs, *scr)
            pl.when(last)(finish)
        else:
            start()
            body(*ins, *outs, *scr)
            finish()

    aliases = {n_in + a: n_out + a for a, job in enumerate(jobs) if job.kind == "gather2"}
    aliases.update({n_in + nj + i: n_out + a for i, a in enumerate(landed)})
    extra = dict(dimension_semantics=("arbitrary",) * len(grid)) if grid else {}
    call = _raw_call(wrapped, in_specs=in_specs + [any_spec] * (nj + n_land), out_specs=out_specs + [any_spec] * nj,
                     out_shape=out_shape + [job.out_shape for job in jobs],
                     scratch_shapes=scratch + [pltpu.SemaphoreType.DMA((nj, MAX_SEMS)), pltpu.SemaphoreType.DMA((nj, MAX_SEMS)),
                                               pltpu.SemaphoreType.DMA((nj,))],
                     input_output_aliases=aliases, compiler_params=_params(**extra), name=kw["name"],
                     **({"grid": grid} if grid else {}))

    def run(*args):
        res = call(*args, *[job.src for job in jobs], *[jobs[a].dst for a in landed])
        for job, r in zip(jobs, res[n_out:]):
            comm.finish(job, r)
        return res[0] if single else list(res[:n_out])
    return run


def _adam(g, w, m, v):
    m2 = ADAM_B1 * m + (1.0 - ADAM_B1) * g
    v2 = ADAM_B2 * v + (1.0 - ADAM_B2) * (g * g)
    m_hat = m2 / (1.0 - ADAM_B1 ** ADAM_STEP)
    v_hat = v2 / (1.0 - ADAM_B2 ** ADAM_STEP)
    delta = -ADAM_LR * (m_hat / (jnp.sqrt(v_hat) + ADAM_EPS) + ADAM_WD * w)
    return delta, m2, v2


def _sum_adam(parts, w, m, v, name):
    r, c = w.shape
    tr = _pick(r, max(8, (1 << 19) // c), 8)

    def body(p_ref, w_ref, m_ref, v_ref, g_ref, d_ref, m2_ref, v2_ref):
        g = p_ref[0].astype(F32)
        for j in range(1, N_DEV):
            g = g + p_ref[j].astype(F32)
        delta, m2, v2 = _adam(g, w_ref[...], m_ref[...], v_ref[...])
        g_ref[...] = g
        d_ref[...] = delta
        m2_ref[...] = m2
        v2_ref[...] = v2

    blk = pl.BlockSpec((tr, c), lambda i: (i, 0))
    shp = jax.ShapeDtypeStruct((r, c), F32)
    return _pcall(body, grid=(r // tr,),
                  in_specs=[pl.BlockSpec((N_DEV, tr, c), lambda i: (0, i, 0)), blk, blk, blk],
                  out_specs=[blk] * 4, out_shape=[shp] * 4, compiler_params=_params(("parallel",)), name=name)(parts, w, m, v)


def _small_layout(rep_shapes, sh_shapes):
    rows_r = [-(-s[1] // LANES) for s in rep_shapes]
    off_r = [sum(rows_r[:i]) for i in range(len(rows_r))]
    tot_r = -(-max(sum(rows_r), 8) // 8) * 8
    rows_s = [-(-s[-2] // 8) * 8 for s in sh_shapes]
    off_s = [sum(rows_s[:i]) for i in range(len(rows_s))]
    tot_s = max(sum(rows_s), 8)
    cmax = max([s[-1] for s in sh_shapes] + [LANES])
    return rows_r, off_r, tot_r, off_s, tot_s, cmax


def _small_exchange(rep_parts, sh_parts, name):
    nr, ns = len(rep_parts), len(sh_parts)
    rows_r, off_r, tot_r, off_s, tot_s, cmax = _small_layout([p.shape for p in rep_parts], [p.shape for p in sh_parts])
    vm = pl.BlockSpec(memory_space=pltpu.VMEM)

    def body(*refs):
        pos = 0

        def take(k):
            nonlocal pos
            out = refs[pos:pos + k]
            pos += k
            return out
        rp, sp = take(nr), take(ns)
        out_r, out_s = take(2)
        pack_r, got_r, pack_s, got_s, send_r, recv_r, send_s, recv_s = take(8)
        me = _my_index()
        pack_r[...] = jnp.zeros_like(pack_r)
        pack_s[...] = jnp.zeros_like(pack_s)
        for i in range(nr):
            nfull = rep_parts[i].shape[1]
            for rr in range(rows_r[i]):
                wdt = min(LANES, nfull - rr * LANES)
                pack_r[off_r[i] + rr:off_r[i] + rr + 1, 0:wdt] = rp[i][0:1, rr * LANES:rr * LANES + wdt]
        for i in range(ns):
            _, r_i, c_i = sh_parts[i].shape
            for j in range(N_DEV):
                pack_s[j, off_s[i]:off_s[i] + r_i, 0:c_i] = sp[i][j]
        got_r[me] = pack_r[...]
        got_s[me] = pack_s[me]
        sends = []
        for k in range(1, N_DEV):
            a = pltpu.make_async_remote_copy(src_ref=pack_r, dst_ref=got_r.at[me], send_sem=send_r.at[k - 1], recv_sem=recv_r.at[k - 1],
                                             device_id=_peer(k), device_id_type=MESH_ID)
            b = pltpu.make_async_remote_copy(src_ref=pack_s.at[_peer_index(k)], dst_ref=got_s.at[me], send_sem=send_s.at[k - 1],
                                             recv_sem=recv_s.at[k - 1], device_id=_peer(k), device_id_type=MESH_ID)
            a.start()
            b.start()
            sends += [a, b]
        for k in range(1, N_DEV):
            pltpu.make_async_remote_copy(src_ref=pack_r, dst_ref=got_r.at[_peer_index(k)], send_sem=send_r.at[k - 1],
                                         recv_sem=recv_r.at[k - 1], device_id=_peer(k), device_id_type=MESH_ID).wait_recv()
            pltpu.make_async_remote_copy(src_ref=pack_s.at[me], dst_ref=got_s.at[_peer_index(k)], send_sem=send_s.at[k - 1],
                                         recv_sem=recv_s.at[k - 1], device_id=_peer(k), device_id_type=MESH_ID).wait_recv()
        for cp in sends:
            cp.wait_send()
        tot_rep = got_r[0]
        tot_sh = got_s[0]
        for j in range(1, N_DEV):
            tot_rep = tot_rep + got_r[j]
            tot_sh = tot_sh + got_s[j]
        out_r[...] = tot_rep
        out_s[...] = tot_sh

    return _pcall(body, carry_us=SMALL_EXCHANGE_CARRY_US, in_specs=[vm] * (nr + ns), out_specs=[vm] * 2,
                  out_shape=[jax.ShapeDtypeStruct((tot_r, LANES), F32), jax.ShapeDtypeStruct((tot_s, cmax), F32)],
                  scratch_shapes=[pltpu.VMEM((tot_r, LANES), F32), pltpu.VMEM((N_DEV, tot_r, LANES), F32),
                                  pltpu.VMEM((N_DEV, tot_s, cmax), F32), pltpu.VMEM((N_DEV, tot_s, cmax), F32),
                                  pltpu.SemaphoreType.DMA((N_DEV - 1,)), pltpu.SemaphoreType.DMA((N_DEV - 1,)),
                                  pltpu.SemaphoreType.DMA((N_DEV - 1,)), pltpu.SemaphoreType.DMA((N_DEV - 1,))],
                  compiler_params=_params(), name=name)(*rep_parts, *sh_parts)


def _small_adam(tot_rep, tot_sh, rep_w, rep_m, rep_v, sh_w, sh_m, sh_v, name):
    nr, ns = len(rep_w), len(sh_w)
    rows_r, off_r, _, off_s, _, _ = _small_layout([w.shape for w in rep_w], [w.shape for w in sh_w])
    vm = pl.BlockSpec(memory_space=pltpu.VMEM)

    def body(*refs):
        pos = 0

        def take(k):
            nonlocal pos
            out = refs[pos:pos + k]
            pos += k
            return out
        (tr_ref, ts_ref), rw, rm, rv, sw, sm, sv = take(2), take(nr), take(nr), take(nr), take(ns), take(ns), take(ns)
        rg, rd, rm2, rv2 = take(nr), take(nr), take(nr), take(nr)
        sg, sd, sm2, sv2 = take(ns), take(ns), take(ns), take(ns)
        for i in range(nr):
            nfull = rep_w[i].shape[1]
            for rr in range(rows_r[i]):
                wdt = min(LANES, nfull - rr * LANES)
                rg[i][0:1, rr * LANES:rr * LANES + wdt] = tr_ref[off_r[i] + rr:off_r[i] + rr + 1, 0:wdt]
            delta, m2, v2 = _adam(rg[i][...], rw[i][...], rm[i][...], rv[i][...])
            rd[i][...] = delta
            rm2[i][...] = m2
            rv2[i][...] = v2
        for i in range(ns):
            r_i, c_i = sh_w[i].shape
            g = ts_ref[off_s[i]:off_s[i] + r_i, 0:c_i]
            delta, m2, v2 = _adam(g, sw[i][...], sm[i][...], sv[i][...])
            sg[i][...] = g
            sd[i][...] = delta
            sm2[i][...] = m2
            sv2[i][...] = v2

    shapes = [jax.ShapeDtypeStruct(w.shape, F32) for w in rep_w] * 4 + [jax.ShapeDtypeStruct(w.shape, F32) for w in sh_w] * 4
    outs = _pcall(body, in_specs=[vm] * (2 + 3 * nr + 3 * ns), out_specs=[vm] * len(shapes), out_shape=shapes,
                  compiler_params=_params(), name=name)(tot_rep, tot_sh, *rep_w, *rep_m, *rep_v, *sh_w, *sh_m, *sh_v)
    rep_out = [outs[i * nr:(i + 1) * nr] for i in range(4)]
    sh_out = [outs[4 * nr + i * ns:4 * nr + (i + 1) * ns] for i in range(4)]
    return rep_out, sh_out


_CONF = ("norm_g", "a_w_in", "a_b_in", "a_dw_w", "a_dw_b", "a_ln_g", "a_ln_b", "a_w_out", "a_b_out")
_FFN = ("ffn_norm_g", "ffn_w_up", "ffn_dw_w", "ffn_dw_b", "ffn_w_down")
_POOL = ("norm_g", "b_w_group", "b_scale")
_ATTN = ("norm_g", "c_w_qkv", "c_q_norm_g", "c_k_norm_g", "c_sinks", "c_w_o")
_LAYERS = (_CONF + _FFN, _POOL + _FFN, _ATTN + _FFN, _CONF + _FFN)
_NAMES = tuple("l%d_%s" % (i, n) for i, names in enumerate(_LAYERS) for n in names)
_BIG = ("a_w_in", "a_w_out", "ffn_w_up", "ffn_w_down", "b_w_group", "c_w_qkv", "c_w_o")
_SHARDED_SMALL = ("a_dw_w", "ffn_dw_w")


def _pad_rows(a, mult=8):
    r = a.shape[0]
    rp = -(-r // mult) * mult
    return a if rp == r else jnp.pad(a, ((0, rp - r), (0, 0)))


def _unstack_cols(st, rows):
    s, r, cs = st.shape
    return jnp.transpose(st, (1, 0, 2)).reshape(r, s * cs)[:rows]


def _stack_cols(a):
    r, c = a.shape
    return jnp.transpose(a.reshape(r, N_DEV, c // N_DEV), (1, 0, 2))


def _row(v):
    return v.reshape(1, -1)


def _ffn_forward(x_mid, p, tag):
    h2 = _rms_fwd(x_mid, _row(p["ffn_norm_g"]), BF16, "rms_fwd_bf16")
    u0 = _mm(h2, p["ffn_w_up"], "nn", name="ffn_up", tn=1408, tk=2048)
    a = _ffn_act_fwd(u0, p["ffn_dw_w"], _row(p["ffn_dw_b"]), "ffn_act_fwd")
    x_out = _mm(a, p["ffn_w_down"], "nn", res=x_mid, name="ffn_down", tk=1408)
    return x_out, dict(h2=h2, u0=u0, a=a)


def _ffn_backward(dx_out, x_mid, p, sv, grads):
    dwd = _mm(sv["a"], dx_out, "tn", out_dtype=BF16, name="ffn_down_dw", tm=1408)
    grads["ffn_w_down"] = dwd.reshape(N_DEV, dwd.shape[0] // N_DEV, dwd.shape[1])
    da = _mm(dx_out, p["ffn_w_down"], "nt", name="ffn_down_dx", tn=1408, tk=2048)
    du0, dww, dwb = _ffn_act_bwd(da, sv["u0"], p["ffn_dw_w"], _row(p["ffn_dw_b"]), "ffn_act_bwd")
    kw = dww.shape[1]
    grads["ffn_dw_w"] = _stack_cols(jnp.transpose(dww, (1, 0, 2)).reshape(kw, -1))
    grads["ffn_dw_b"] = dwb.reshape(1, -1)
    grads["ffn_w_up"] = _mm(sv["h2"], du0, "tn", out_dtype=BF16, out_stack=N_DEV, name="ffn_up_dw", tn=1408)
    dh2 = _mm(du0, p["ffn_w_up"], "nt", name="ffn_up_dx", tk=1408)
    dx_mid, dg, _ = _rms_bwd(dh2, x_mid, _row(p["ffn_norm_g"]), dx_out, "rms_bwd")
    grads["ffn_norm_g"] = dg
    return dx_mid


def _conf_forward(x, p):
    h = _rms_fwd(x, _row(p["norm_g"]), BF16, "rms_fwd_bf16")
    u = _mm(h, p["a_w_in"], "nn", bias=_row(p["a_b_in"]), name="conf_in", tn=512, tk=2048)
    cpre = _conf_conv_fwd(u, p["a_dw_w"], _row(p["a_dw_b"]), "conf_conv_fwd")
    s = _ln_silu_fwd(cpre, _row(p["a_ln_g"]), _row(p["a_ln_b"]), "ln_silu_fwd")
    x_mid = _mm(s, p["a_w_out"], "nn", bias=_row(p["a_b_out"]), res=x, name="conf_out", tk=2048)
    return x_mid, dict(h=h, u=u, cpre=cpre, s=s)


def _conf_backward(dx_mid, x, p, sv, grads):
    dwo = _mm(sv["s"], dx_mid, "tn", out_dtype=BF16, name="conf_out_dw")
    grads["a_w_out"] = dwo.reshape(N_DEV, dwo.shape[0] // N_DEV, dwo.shape[1])
    ds = _mm(dx_mid, p["a_w_out"], "nt", name="conf_out_dx", tk=2048)
    dc, dlg, dlb = _ln_silu_bwd(ds, sv["cpre"], _row(p["a_ln_g"]), _row(p["a_ln_b"]), "ln_silu_bwd")
    grads["a_ln_g"], grads["a_ln_b"] = dlg, dlb
    du, dww, dwb, dbin = _conf_conv_bwd(dc, sv["u"], p["a_dw_w"], "conf_conv_bwd")
    grads["a_dw_w"] = _stack_cols(dww)
    grads["a_dw_b"] = dwb
    grads["a_b_in"] = dbin.reshape(1, -1)
    grads["a_w_in"] = _mm(sv["h"], du, "tn", out_dtype=BF16, out_stack=N_DEV, name="conf_in_dw", tn=512)
    dh = _mm(du, p["a_w_in"], "nt", name="conf_in_dx", tk=512)
    dx, dg, dbo = _rms_bwd(dh, x, _row(p["norm_g"]), dx_mid, "rms_bwd")
    grads["norm_g"] = dg
    grads["a_b_out"] = dbo
    return dx


def _pool_forward(x, p):
    h = _rms_fwd(x, _row(p["norm_g"]), F32, "rms_fwd_f32")
    mixed = _pool_fwd(h, "pool_fwd")
    x_mid = _pool_mm_fwd(mixed, p["b_w_group"], _row(p["b_scale"]), x, "pool_mm_fwd")
    return x_mid, dict(mixed=mixed)


def _pool_backward(dx_mid, x, p, sv, grads):
    dmixed, dwg, dscale = _pool_mm_bwd(dx_mid, sv["mixed"], p["b_w_group"], _row(p["b_scale"]), "pool_mm_bwd")
    ng, gd, _ = dwg.shape
    grads["b_w_group"] = jnp.transpose(dwg.reshape(ng, N_DEV, gd // N_DEV, gd), (1, 0, 2, 3)).reshape(N_DEV, ng * gd // N_DEV, gd).astype(BF16)
    grads["b_scale"] = dscale
    dh = _pool_bwd(dmixed, "pool_bwd")
    dx, dg, _ = _rms_bwd(dh, x, _row(p["norm_g"]), dx_mid, "rms_bwd")
    grads["norm_g"] = dg
    return dx


def _attn_tables(p, positions, d_model):
    n_q = d_model // HEAD
    n_kv = n_q // 8
    tabs = _rope_tables(positions)
    gq2 = jnp.concatenate([p["c_q_norm_g"], p["c_q_norm_g"]]).reshape(1, LANES)
    gk2 = jnp.concatenate([p["c_k_norm_g"], p["c_k_norm_g"]]).reshape(1, LANES)
    sink_tab = jnp.repeat(jnp.repeat(p["c_sinks"].reshape(-1, 2), HEAD, axis=1), 8, axis=0)
    return n_q, n_kv, tabs, gq2, gk2, sink_tab


def _attn_forward(x, p, positions):
    n_q, n_kv, tabs, gq2, gk2, sink_tab = _attn_tables(p, positions, x.shape[1])
    h = _rms_fwd(x, _row(p["norm_g"]), BF16, "rms_fwd_bf16")
    qkv = _mm(h, p["c_w_qkv"], "nn", name="attn_qkv", tn=1280, tk=2048)
    q, k2, v2 = _qk_prep_fwd(qkv, tabs, gq2, gk2, n_q, n_kv, "qk_prep_fwd")
    o, lse = _attn_fwd(q, k2, v2, sink_tab, "attn_fwd")
    x_mid = _mm(o, p["c_w_o"], "nn", res=x, name="attn_out", tk=2048)
    return x_mid, dict(h=h, qkv=qkv, q=q, k2=k2, v2=v2, o=o, lse=lse)


def _attn_backward(dx_mid, x, p, positions, sv, grads):
    n_q, n_kv, tabs, gq2, gk2, sink_tab = _attn_tables(p, positions, x.shape[1])
    dwo = _mm(sv["o"], dx_mid, "tn", out_dtype=BF16, name="attn_out_dw")
    grads["c_w_o"] = dwo.reshape(N_DEV, dwo.shape[0] // N_DEV, dwo.shape[1])
    do = _mm(dx_mid, p["c_w_o"], "nt", name="attn_out_dx", tk=2048)
    dq, dkc, dkp, dvc, dvp, dsk = _attn_bwd(do, sv["q"], sv["o"], sv["lse"], sv["k2"], sv["v2"], sink_tab, "attn_bwd")
    nkv_, nb = dsk.shape[0], dsk.shape[1]
    dsk = dsk.reshape(nkv_, nb, -1, 8, LANES)[:, :, :, 0, :].sum(axis=1).reshape(-1, LANES)
    grads["c_sinks"] = jnp.stack([dsk[:, 0], dsk[:, HEAD]], axis=1).reshape(1, -1)
    dqkv, dgq, dgk = _qk_prep_bwd(dq, dkc, dkp, dvc, dvp, sv["qkv"], tabs, gq2, gk2, n_q, n_kv, "qk_prep_bwd")
    grads["c_q_norm_g"] = dgq[:, :HEAD] + dgq[:, HEAD:]
    grads["c_k_norm_g"] = dgk[:, :HEAD] + dgk[:, HEAD:]
    dwq = _mm(sv["h"], dqkv, "tn", out_dtype=BF16, name="attn_qkv_dw", tn=1280)
    grads["c_w_qkv"] = _stack_cols(dwq)
    dh = _mm(dqkv, p["c_w_qkv"], "nt", name="attn_qkv_dx", tk=1280)
    dx, dg, _ = _rms_bwd(dh, x, _row(p["norm_g"]), dx_mid, "rms_bwd")
    grads["norm_g"] = dg
    return dx


class _LayerWeights:
    def __init__(self, li, weights, small_full, comm):
        self.li, self.weights, self.small_full, self.comm, self.cache = li, weights, small_full, comm, {}

    def __getitem__(self, nme):
        if nme not in self.cache:
            self.cache[nme] = self.fetch(nme)
        return self.cache[nme]

    def fetch(self, nme):
        full = "l%d_%s" % (self.li, nme)
        w = self.weights[full]
        if nme in _SHARDED_SMALL:
            return _unstack_cols(self.small_full[full], w.shape[0])
        if nme not in _BIG:
            return w
        got = self.comm.need(full)
        if nme in ("a_w_in", "ffn_w_up"):
            return got
        if nme == "c_w_qkv":
            return _unstack_cols(got, w.shape[0])
        if nme == "b_w_group":
            ng, gs, gd = w.shape
            return jnp.transpose(got.reshape(N_DEV, ng, gs, gd), (1, 0, 2, 3)).reshape(ng, N_DEV * gs, gd)
        return got.reshape(-1, w.shape[1])


class _LayerGrads(dict):
    def __init__(self, li, comm):
        super().__init__()
        self.li, self.comm = li, comm

    def __setitem__(self, nme, value):
        if nme in _BIG:
            self.comm.push("l%d_%s" % (self.li, nme), "scatter", value)
        else:
            super().__setitem__(nme, value)


def kernel(*args):
    n_w = len(_NAMES)
    x, positions = args[0], args[1]
    weights = dict(zip(_NAMES, args[2:2 + n_w]))
    loss_target = args[2 + n_w]
    moms = dict(zip(_NAMES, args[3 + n_w:3 + 2 * n_w]))
    vels = dict(zip(_NAMES, args[3 + 2 * n_w:3 + 3 * n_w]))
    x0 = x[0]
    pos = positions[0]
    kinds = ("conf", "pool", "attn", "conf")
    comm = _Comm()
    _STATE["comm"], _STATE["last"] = comm, None
    shd = [n for n in _NAMES if n.split("_", 1)[1] in _SHARDED_SMALL]
    small_full = dict(zip(shd, _all_gather([_pad_rows(weights[n]) for n in shd], "gather_small")))
    for n in _NAMES:
        if n.split("_", 1)[1] in _BIG:
            w = weights[n]
            comm.push(n, "gather1", w.astype(BF16).reshape(-1, w.shape[-1]))
    results = {}

    def update_ready():
        while comm.scattered:
            full, parts = comm.scattered.pop(0)
            w = weights[full]
            w2 = w.reshape(-1, w.shape[-1])
            outs = _sum_adam(parts, w2, moms[full].reshape(w2.shape), vels[full].reshape(w2.shape), "adam_" + full.split("_", 1)[1])
            results[full] = tuple(o.reshape(w.shape) for o in outs)

    params, saved = [], []
    cur = x0
    for li, names in enumerate(_LAYERS):
        comm.layer = li
        p = _LayerWeights(li, weights, small_full, comm)
        if kinds[li] == "conf":
            x_mid, sv = _conf_forward(cur, p)
        elif kinds[li] == "pool":
            x_mid, sv = _pool_forward(cur, p)
        else:
            x_mid, sv = _attn_forward(cur, p, pos)
        x_out, sv_f = _ffn_forward(x_mid, p, kinds[li])
        params.append(p)
        saved.append((sv, sv_f, cur, x_mid))
        cur = x_out
    dy, loss_part = _loss_head(cur, loss_target[0], "loss_head")
    loss = lax.psum(loss_part[0, 0], ("x", "y", "c"))

    small_grads = {}
    dcur = dy
    for li in range(len(_LAYERS) - 1, -1, -1):
        p = params[li]
        sv, sv_f, x_in, x_mid = saved[li]
        grads = _LayerGrads(li, comm)
        dmid = _ffn_backward(dcur, x_mid, p, sv_f, grads)
        update_ready()
        if kinds[li] == "conf":
            dcur = _conf_backward(dmid, x_in, p, sv, grads)
        elif kinds[li] == "pool":
            dcur = _pool_backward(dmid, x_in, p, sv, grads)
        else:
            dcur = _attn_backward(dmid, x_in, p, pos, sv, grads)
        update_ready()
        for n in _LAYERS[li]:
            if n not in _BIG:
                small_grads["l%d_%s" % (li, n)] = grads[n]
    rep = [n for n in _NAMES if n.split("_", 1)[1] not in _BIG and n.split("_", 1)[1] not in _SHARDED_SMALL]
    tot_rep, tot_sh = _small_exchange([small_grads[n] for n in rep], [small_grads[n] for n in shd], "small_exchange")
    while comm.queue or comm.scattered:
        if not comm.scattered:
            comm.flush(comm.take(1e9))
        update_ready()
    _STATE["comm"] = None
    rep_out, sh_out = _small_adam(tot_rep, tot_sh, [_row(weights[n]) for n in rep], [_row(moms[n]) for n in rep], [_row(vels[n]) for n in rep],
                                  [weights[n] for n in shd], [moms[n] for n in shd], [vels[n] for n in shd], "small_adam")
    for i, n in enumerate(rep):
        results[n] = tuple(rep_out[k][i].reshape(weights[n].shape) for k in range(4))
    for i, n in enumerate(shd):
        results[n] = tuple(sh_out[k][i] for k in range(4))

    _STATE["last"] = None
    grad_x = dcur[None]
    out = [loss, grad_x]
    for k in range(4):
        out += [results[n][k] for n in _NAMES]
    return tuple(out)
```

```python
import functools
import math

import jax
import jax.numpy as jnp
from jax import lax
from jax.experimental import pallas as pl
from jax.experimental.pallas import tpu as pltpu

F32 = jnp.float32
BF16 = jnp.bfloat16
N_DEV = 8
EPS = 1e-6
LANES = 128
HEAD = 64
Q_BLOCK = 128
ROT_DIM = 16
ROPE_THETA = 500000.0
POOL_WINDOWS = (2, 4, 8, 16)
HALO = 32
ROWS = 128
VMEM_LIMIT = 56 * 1024 * 1024
ADAM_LR, ADAM_B1, ADAM_B2, ADAM_EPS, ADAM_WD, ADAM_STEP = 0.001, 0.9, 0.999, 1e-08, 0.01, 10
MESH_ID = pl.DeviceIdType.MESH
MXU_FLOPS_PER_US = 7.5e8
HBM_BYTES_PER_US = 2.5e6
ATTN_US_PER_STEP = 1.5
CONV_FWD_US_PER_ELEM = 1.1e-5
CONV_BWD_US_PER_ELEM = 2.3e-5
ACT_FWD_US_PER_ELEM = 4.8e-6
SMALL_EXCHANGE_CARRY_US = 100.0


def _make_call(body, **kw):
    return pl.pallas_call(body, **kw)


_STATE = {"comm": None, "last": None}


def _raw_call(body, **kw):
    call = _make_call(body, **kw)

    def run(*args):
        last = _STATE["last"]
        if last is not None and args:
            first, _ = lax.optimization_barrier((args[0], last))
            args = (first,) + tuple(args[1:])
        res = call(*args)
        _STATE["last"] = res[0] if isinstance(res, (list, tuple)) else res
        return res
    return run


def _pcall(body, carry_us=0.0, **kw):
    comm = _STATE["comm"]
    jobs = comm.take(carry_us) if (comm is not None and carry_us > 0) else []
    if not jobs:
        return _raw_call(body, **kw)
    return _carry(body, jobs, comm, kw)


def _params(sem=None, **kw):
    if sem is not None:
        kw["dimension_semantics"] = sem
    return pltpu.CompilerParams(vmem_limit_bytes=VMEM_LIMIT, **kw)


def _pick(dim, pref, mult=LANES):
    best = None
    d = mult
    while d <= min(dim, pref):
        if dim % d == 0:
            best = d
        d += mult
    return dim if best is None else best


def _sigmoid(x):
    return 1.0 / (1.0 + jnp.exp(-x))


def _fold8(p):
    r, c = p.shape
    return p.reshape(r // 8, 8, c).sum(axis=0)


def _window(win_ref, e):
    win_ref[...] = e
    return win_ref


def _rows(win_ref, k, r):
    return win_ref[k:k + r, :]


def _lshape(a):
    return a.shape if a.ndim == 2 else (a.shape[1], a.shape[0] * a.shape[2])


def _panel(a):
    return a.shape[1] if a.ndim == 2 else a.shape[2]


def _lspec(a, br, bc, rc):
    if a.ndim == 2:
        return pl.BlockSpec((br, bc), rc)
    per = a.shape[2] // bc

    def idx(i, j, k):
        r, c = rc(i, j, k)
        return (c // per, r, c % per)
    return pl.BlockSpec((None, br, bc), idx)


def _mm(a, b, dims, *, name, out_dtype=F32, out_stack=None, bias=None, res=None, tm=1024, tn=1024, tk=1024):
    (ar, ac), (br_, bc_) = _lshape(a), _lshape(b)
    if dims == "nn":
        m, k, n = ar, ac, bc_
        lim_m, lim_k, lim_n = m, min(_panel(a), k), _panel(b)
    elif dims == "nt":
        m, k, n = ar, ac, br_
        lim_m, lim_k, lim_n = m, math.gcd(_panel(a), _panel(b)), n
    else:
        m, k, n = ac, ar, bc_
        lim_m, lim_k, lim_n = _panel(a), k, _panel(b)
    if out_stack is not None:
        lim_n = math.gcd(lim_n, n // out_stack)
    sub = 16 if (out_dtype == BF16 or a.dtype == BF16) else 8
    tm = _pick(lim_m, tm, LANES if dims == "tn" else sub)
    tn = _pick(lim_n, tn)
    tk = _pick(lim_k, tk, sub if dims == "tn" else LANES)
    nk = k // tk
    if dims == "tn":
        a_spec = _lspec(a, tk, tm, lambda i, j, kk: (kk, i))
    else:
        a_spec = _lspec(a, tm, tk, lambda i, j, kk: (i, kk))
    if dims == "nt":
        b_spec = _lspec(b, tn, tk, lambda i, j, kk: (j, kk))
    else:
        b_spec = _lspec(b, tk, tn, lambda i, j, kk: (kk, j))
    contract = {"nn": ((1,), (0,)), "nt": ((1,), (1,)), "tn": ((0,), (0,))}[dims]
    in_specs, args = [a_spec, b_spec], [a, b]
    if bias is not None:
        in_specs.append(pl.BlockSpec((1, tn), lambda i, j, kk: (0, j)))
        args.append(bias)
    if res is not None:
        in_specs.append(pl.BlockSpec((tm, tn), lambda i, j, kk: (i, j)))
        args.append(res)
    if out_stack is None:
        out_shape = jax.ShapeDtypeStruct((m, n), out_dtype)
    else:
        out_shape = jax.ShapeDtypeStruct((out_stack, m, n // out_stack), out_dtype)
    o_spec = _lspec(out_shape, tm, tn, lambda i, j, kk: (i, j))
    has_bias, has_res = bias is not None, res is not None

    def body(*refs):
        a_ref, b_ref = refs[0], refs[1]
        pos = 2
        bias_ref = res_ref = None
        if has_bias:
            bias_ref = refs[pos]
            pos += 1
        if has_res:
            res_ref = refs[pos]
            pos += 1
        o_ref = refs[pos]

        def part():
            return lax.dot_general(a_ref[...].astype(BF16), b_ref[...].astype(BF16), (contract, ((), ())),
                                   preferred_element_type=F32)

        def finish(r):
            if has_bias:
                r = r + bias_ref[...]
            if has_res:
                r = r + res_ref[...]
            o_ref[...] = r.astype(out_dtype)

        if nk == 1:
            finish(part())
        else:
            acc = refs[pos + 1]
            kk = pl.program_id(2)

            @pl.when(kk == 0)
            def _():
                acc[...] = part()

            @pl.when(kk > 0)
            def _():
                acc[...] += part()

            @pl.when(kk == nk - 1)
            def _():
                finish(acc[...])

    scratch = [] if nk == 1 else [pltpu.VMEM((tm, tn), F32)]
    return _pcall(body, carry_us=2.0 * m * n * k / MXU_FLOPS_PER_US, grid=(m // tm, n // tn, nk), in_specs=in_specs, out_specs=o_spec, out_shape=out_shape,
                  scratch_shapes=scratch, compiler_params=_params(("parallel", "parallel", "arbitrary")), name=name)(*args)


def _rms_fwd(x, g, out_dtype, name):
    t, d = x.shape
    tm = _pick(t, 512, 16)

    def body(x_ref, g_ref, o_ref):
        xv = x_ref[...]
        r = lax.rsqrt(jnp.mean(xv * xv, axis=-1, keepdims=True) + EPS)
        o_ref[...] = ((xv * r) * g_ref[...]).astype(out_dtype)

    return _pcall(body, carry_us=6.0 * t * d / HBM_BYTES_PER_US, grid=(t // tm,),
                  in_specs=[pl.BlockSpec((tm, d), lambda i: (i, 0)), pl.BlockSpec((1, d), lambda i: (0, 0))],
                  out_specs=pl.BlockSpec((tm, d), lambda i: (i, 0)), out_shape=jax.ShapeDtypeStruct((t, d), out_dtype),
                  compiler_params=_params(("parallel",)), name=name)(x, g)


def _rms_bwd(dh, x, g, dres, name):
    t, d = x.shape
    tm = _pick(t, 256, 8)

    def body(dh_ref, x_ref, g_ref, dres_ref, dx_ref, dg_ref, cs_ref):
        xv, dhv, dr = x_ref[...], dh_ref[...], dres_ref[...]
        r = lax.rsqrt(jnp.mean(xv * xv, axis=-1, keepdims=True) + EPS)
        xh = xv * r
        dxh = dhv * g_ref[...]
        dx_ref[...] = dr + r * (dxh - xh * jnp.mean(dxh * xh, axis=-1, keepdims=True))
        pg = jnp.sum(dhv * xh, axis=0, keepdims=True)
        pc = jnp.sum(dr, axis=0, keepdims=True)

        @pl.when(pl.program_id(0) == 0)
        def _():
            dg_ref[...] = pg
            cs_ref[...] = pc

        @pl.when(pl.program_id(0) > 0)
        def _():
            dg_ref[...] += pg
            cs_ref[...] += pc

    row = pl.BlockSpec((tm, d), lambda i: (i, 0))
    vec = pl.BlockSpec((1, d), lambda i: (0, 0))
    return _pcall(body, carry_us=16.0 * t * d / HBM_BYTES_PER_US, grid=(t // tm,), in_specs=[row, row, vec, row], out_specs=[row, vec, vec],
                  out_shape=[jax.ShapeDtypeStruct((t, d), F32), jax.ShapeDtypeStruct((1, d), F32), jax.ShapeDtypeStruct((1, d), F32)],
                  compiler_params=_params(("arbitrary",)), name=name)(dh, x, g, dres)


def _loss_head(y, target, name):
    t, d = y.shape
    tm = _pick(t, 512, 8)

    def body(y_ref, t_ref, dy_ref, l_ref):
        e = y_ref[...] - t_ref[...]
        dy_ref[...] = e * (1.0 / d)
        part = 0.5 * jnp.sum(jnp.mean(e * e, axis=-1, keepdims=True), axis=0, keepdims=True)

        @pl.when(pl.program_id(0) == 0)
        def _():
            l_ref[...] = part

        @pl.when(pl.program_id(0) > 0)
        def _():
            l_ref[...] += part

    row = pl.BlockSpec((tm, d), lambda i: (i, 0))
    return _pcall(body, grid=(t // tm,), in_specs=[row, row], out_specs=[row, pl.BlockSpec((1, 1), lambda i: (0, 0))],
                  out_shape=[jax.ShapeDtypeStruct((t, d), F32), jax.ShapeDtypeStruct((1, 1), F32)],
                  compiler_params=_params(("arbitrary",)), name=name)(y, target)


def _ln_silu_fwd(c, g, b, name):
    t, d = c.shape
    tm = _pick(t, 512, 16)

    def body(c_ref, g_ref, b_ref, o_ref):
        cv = c_ref[...]
        xc = cv - jnp.mean(cv, axis=-1, keepdims=True)
        z = xc * lax.rsqrt(jnp.mean(xc * xc, axis=-1, keepdims=True) + EPS) * g_ref[...] + b_ref[...]
        o_ref[...] = (z * _sigmoid(z)).astype(BF16)

    row = pl.BlockSpec((tm, d), lambda i: (i, 0))
    vec = pl.BlockSpec((1, d), lambda i: (0, 0))
    return _pcall(body, carry_us=7.0 * t * d / HBM_BYTES_PER_US, grid=(t // tm,), in_specs=[row, vec, vec], out_specs=row,
                  out_shape=jax.ShapeDtypeStruct((t, d), BF16),
                  compiler_params=_params(("parallel",)), name=name)(c, g, b)


def _ln_silu_bwd(ds, c, g, b, name):
    t, d = c.shape
    tm = _pick(t, 256, 8)

    def body(ds_ref, c_ref, g_ref, b_ref, dc_ref, dg_ref, db_ref):
        cv = c_ref[...]
        xc = cv - jnp.mean(cv, axis=-1, keepdims=True)
        r = lax.rsqrt(jnp.mean(xc * xc, axis=-1, keepdims=True) + EPS)
        ch = xc * r
        z = ch * g_ref[...] + b_ref[...]
        sg = _sigmoid(z)
        dz = ds_ref[...] * (sg * (1.0 + z * (1.0 - sg)))
        dch = dz * g_ref[...]
        dc_ref[...] = r * (dch - jnp.mean(dch, axis=-1, keepdims=True) - ch * jnp.mean(dch * ch, axis=-1, keepdims=True))
        pg = jnp.sum(dz * ch, axis=0, keepdims=True)
        pb = jnp.sum(dz, axis=0, keepdims=True)

        @pl.when(pl.program_id(0) == 0)
        def _():
            dg_ref[...] = pg
            db_ref[...] = pb

        @pl.when(pl.program_id(0) > 0)
        def _():
            dg_ref[...] += pg
            db_ref[...] += pb

    row = pl.BlockSpec((tm, d), lambda i: (i, 0))
    vec = pl.BlockSpec((1, d), lambda i: (0, 0))
    return _pcall(body, grid=(t // tm,), in_specs=[row, row, vec, vec], out_specs=[row, vec, vec],
                  out_shape=[jax.ShapeDtypeStruct((t, d), F32), jax.ShapeDtypeStruct((1, d), F32), jax.ShapeDtypeStruct((1, d), F32)],
                  compiler_params=_params(("arbitrary",)), name=name)(ds, c, g, b)


def _steps(t):
    return t // ROWS


def _conf_conv_fwd(u, dw_w, dw_b, name):
    t, d2 = u.shape
    d = d2 // 2
    c = LANES
    ns = d // c
    kc = dw_w.shape[0]

    def body(a_ref, g_ref, w_ref, b_ref, o_ref, pad, win):
        pad[0:HALO, :] = jnp.zeros((HALO, c), F32)

        def glu(i, _):
            base = pl.multiple_of(i * ROWS, ROWS)
            pad[pl.ds(base + HALO, ROWS), :] = a_ref[pl.ds(base, ROWS), :] * _sigmoid(g_ref[pl.ds(base, ROWS), :])
            return 0
        lax.fori_loop(0, _steps(t), glu, 0)

        def conv(i, _):
            base = pl.multiple_of(i * ROWS, ROWS)
            e = _window(win, pad[pl.ds(base, ROWS + HALO), :])
            acc = jnp.zeros((ROWS, c), F32) + b_ref[...]
            for j in range(kc):
                acc = acc + w_ref[j:j + 1, :] * _rows(e, HALO - (kc - 1) + j, ROWS)
            o_ref[pl.ds(base, ROWS), :] = acc
            return 0
        lax.fori_loop(0, _steps(t), conv, 0)

    return _pcall(body, carry_us=CONV_FWD_US_PER_ELEM * t * d, grid=(ns,),
                  in_specs=[pl.BlockSpec((t, c), lambda s: (0, s)), pl.BlockSpec((t, c), lambda s: (0, s + ns)),
                            pl.BlockSpec((kc, c), lambda s: (0, s)), pl.BlockSpec((1, c), lambda s: (0, s))],
                  out_specs=pl.BlockSpec((t, c), lambda s: (0, s)), out_shape=jax.ShapeDtypeStruct((t, d), F32),
                  scratch_shapes=[pltpu.VMEM((t + HALO, c), F32), pltpu.VMEM((ROWS + HALO, c), F32)],
                  compiler_params=_params(("parallel",)), name=name)(u, u, dw_w, dw_b)


def _conf_conv_bwd(dc, u, dw_w, name):
    t, d = dc.shape
    c = LANES
    ns = d // c
    kc = dw_w.shape[0]

    def body(dc_ref, a_ref, g_ref, w_ref, du_ref, dww_ref, dwb_ref, db_ref, padv, padd, accw, accb, winv, wind):
        padv[0:HALO, :] = jnp.zeros((HALO, c), F32)
        padd[t:t + HALO, :] = jnp.zeros((HALO, c), F32)
        accw[...] = jnp.zeros_like(accw)
        accb[...] = jnp.zeros_like(accb)

        def fill(i, _):
            base = pl.multiple_of(i * ROWS, ROWS)
            padv[pl.ds(base + HALO, ROWS), :] = a_ref[pl.ds(base, ROWS), :] * _sigmoid(g_ref[pl.ds(base, ROWS), :])
            padd[pl.ds(base, ROWS), :] = dc_ref[pl.ds(base, ROWS), :]
            return 0
        lax.fori_loop(0, _steps(t), fill, 0)

        def step(i, _):
            base = pl.multiple_of(i * ROWS, ROWS)
            ev = _window(winv, padv[pl.ds(base, ROWS + HALO), :])
            ed = _window(wind, padd[pl.ds(base, ROWS + HALO), :])
            dcc = _rows(ed, 0, ROWS)
            dv = jnp.zeros((ROWS, c), F32)
            for j in range(kc):
                dv = dv + w_ref[j:j + 1, :] * _rows(ed, kc - 1 - j, ROWS)
                accw[j] = accw[j] + _fold8(dcc * _rows(ev, HALO - (kc - 1) + j, ROWS))
            accb[0] = accb[0] + _fold8(dcc)
            av = a_ref[pl.ds(base, ROWS), :]
            sg = _sigmoid(g_ref[pl.ds(base, ROWS), :])
            da = dv * sg
            dg = dv * av * sg * (1.0 - sg)
            du_ref[0, pl.ds(base, ROWS), :] = da.astype(BF16)
            du_ref[1, pl.ds(base, ROWS), :] = dg.astype(BF16)
            accb[1] = accb[1] + _fold8(da)
            accb[2] = accb[2] + _fold8(dg)
            return 0
        lax.fori_loop(0, _steps(t), step, 0)
        for j in range(kc):
            dww_ref[j:j + 1, :] = jnp.sum(accw[j], axis=0, keepdims=True)
        dwb_ref[...] = jnp.sum(accb[0], axis=0, keepdims=True)
        db_ref[0] = jnp.sum(accb[1], axis=0, keepdims=True)
        db_ref[1] = jnp.sum(accb[2], axis=0, keepdims=True)

    return _pcall(body, carry_us=CONV_BWD_US_PER_ELEM * t * d, grid=(ns,),
                  in_specs=[pl.BlockSpec((t, c), lambda s: (0, s)), pl.BlockSpec((t, c), lambda s: (0, s)),
                            pl.BlockSpec((t, c), lambda s: (0, s + ns)), pl.BlockSpec((kc, c), lambda s: (0, s))],
                  out_specs=[pl.BlockSpec((2, t, c), lambda s: (0, 0, s)), pl.BlockSpec((kc, c), lambda s: (0, s)),
                             pl.BlockSpec((1, c), lambda s: (0, s)), pl.BlockSpec((2, 1, c), lambda s: (0, 0, s))],
                  out_shape=[jax.ShapeDtypeStruct((2, t, d), BF16), jax.ShapeDtypeStruct((kc, d), F32),
                             jax.ShapeDtypeStruct((1, d), F32), jax.ShapeDtypeStruct((2, 1, d), F32)],
                  scratch_shapes=[pltpu.VMEM((t + HALO, c), F32), pltpu.VMEM((t + HALO, c), F32),
                                  pltpu.VMEM((kc, 8, c), F32), pltpu.VMEM((3, 8, c), F32),
                                  pltpu.VMEM((ROWS + HALO, c), F32), pltpu.VMEM((ROWS + HALO, c), F32)],
                  compiler_params=_params(("parallel",)), name=name)(dc, u, u, dw_w)


def _ffn_act_fwd(u0, dw_w, dw_b, name):
    t, f2 = u0.shape
    f = f2 // 2
    c = LANES
    ns = f // c
    kw = dw_w.shape[0]

    def body(g_ref, v_ref, wg_ref, wv_ref, bg_ref, bv_ref, o_ref, wing, winv):
        def step(i, _):
            base = pl.multiple_of(i * ROWS, ROWS)
            lo = pl.multiple_of(jnp.maximum(base - HALO, 0), HALO)
            keep = jnp.where(i > 0, 1.0, 0.0)
            eg = _window(wing, jnp.concatenate([g_ref[pl.ds(lo, HALO), :] * keep, g_ref[pl.ds(base, ROWS), :]], axis=0))
            ev = _window(winv, jnp.concatenate([v_ref[pl.ds(lo, HALO), :] * keep, v_ref[pl.ds(base, ROWS), :]], axis=0))
            gate = jnp.zeros((ROWS, c), F32) + bg_ref[...]
            val = jnp.zeros((ROWS, c), F32) + bv_ref[...]
            for j in range(kw):
                gate = gate + wg_ref[j:j + 1, :] * _rows(eg, HALO - (kw - 1) + j, ROWS)
                val = val + wv_ref[j:j + 1, :] * _rows(ev, HALO - (kw - 1) + j, ROWS)
            o_ref[pl.ds(base, ROWS), :] = (gate * _sigmoid(gate) * val).astype(BF16)
            return 0
        lax.fori_loop(0, _steps(t), step, 0)

    return _pcall(body, carry_us=ACT_FWD_US_PER_ELEM * t * f, grid=(ns,),
                  in_specs=[pl.BlockSpec((t, c), lambda s: (0, s)), pl.BlockSpec((t, c), lambda s: (0, s + ns)),
                            pl.BlockSpec((kw, c), lambda s: (0, s)), pl.BlockSpec((kw, c), lambda s: (0, s + ns)),
                            pl.BlockSpec((1, c), lambda s: (0, s)), pl.BlockSpec((1, c), lambda s: (0, s + ns))],
                  out_specs=pl.BlockSpec((t, c), lambda s: (0, s)), out_shape=jax.ShapeDtypeStruct((t, f), BF16),
                  scratch_shapes=[pltpu.VMEM((ROWS + HALO, c), F32), pltpu.VMEM((ROWS + HALO, c), F32)],
                  compiler_params=_params(("parallel",)), name=name)(u0, u0, dw_w, dw_w, dw_b, dw_b)


def _ffn_act_bwd(da, u0, dw_w, dw_b, name):
    t, f = da.shape
    c = LANES
    ns = f // c
    kw = dw_w.shape[0]

    def body(da_ref, g_ref, v_ref, wg_ref, wv_ref, bg_ref, bv_ref, du_ref, dww_ref, dwb_ref, padg, padv, accw, accb, wing, winv):
        padg[t:t + HALO, :] = jnp.zeros((HALO, c), F32)
        padv[t:t + HALO, :] = jnp.zeros((HALO, c), F32)
        accw[...] = jnp.zeros_like(accw)
        accb[...] = jnp.zeros_like(accb)

        def first(i, _):
            base = pl.multiple_of(i * ROWS, ROWS)
            lo = pl.multiple_of(jnp.maximum(base - HALO, 0), HALO)
            keep = jnp.where(i > 0, 1.0, 0.0)
            eg = _window(wing, jnp.concatenate([g_ref[pl.ds(lo, HALO), :] * keep, g_ref[pl.ds(base, ROWS), :]], axis=0))
            ev = _window(winv, jnp.concatenate([v_ref[pl.ds(lo, HALO), :] * keep, v_ref[pl.ds(base, ROWS), :]], axis=0))
            gate = jnp.zeros((ROWS, c), F32) + bg_ref[...]
            val = jnp.zeros((ROWS, c), F32) + bv_ref[...]
            for j in range(kw):
                gate = gate + wg_ref[j:j + 1, :] * _rows(eg, HALO - (kw - 1) + j, ROWS)
                val = val + wv_ref[j:j + 1, :] * _rows(ev, HALO - (kw - 1) + j, ROWS)
            dav = da_ref[pl.ds(base, ROWS), :]
            sg = _sigmoid(gate)
            dgate = dav * val * (sg * (1.0 + gate * (1.0 - sg)))
            dval = dav * (gate * sg)
            padg[pl.ds(base, ROWS), :] = dgate
            padv[pl.ds(base, ROWS), :] = dval
            for j in range(kw):
                accw[j] = accw[j] + _fold8(dgate * _rows(eg, HALO - (kw - 1) + j, ROWS))
                accw[kw + j] = accw[kw + j] + _fold8(dval * _rows(ev, HALO - (kw - 1) + j, ROWS))
            accb[0] = accb[0] + _fold8(dgate)
            accb[1] = accb[1] + _fold8(dval)
            return 0
        lax.fori_loop(0, _steps(t), first, 0)

        def second(i, _):
            base = pl.multiple_of(i * ROWS, ROWS)
            eg = _window(wing, padg[pl.ds(base, ROWS + HALO), :])
            ev = _window(winv, padv[pl.ds(base, ROWS + HALO), :])
            dg = jnp.zeros((ROWS, c), F32)
            dv = jnp.zeros((ROWS, c), F32)
            for j in range(kw):
                dg = dg + wg_ref[j:j + 1, :] * _rows(eg, kw - 1 - j, ROWS)
                dv = dv + wv_ref[j:j + 1, :] * _rows(ev, kw - 1 - j, ROWS)
            du_ref[0, pl.ds(base, ROWS), :] = dg.astype(BF16)
            du_ref[1, pl.ds(base, ROWS), :] = dv.astype(BF16)
            return 0
        lax.fori_loop(0, _steps(t), second, 0)
        for j in range(kw):
            dww_ref[0, j:j + 1, :] = jnp.sum(accw[j], axis=0, keepdims=True)
            dww_ref[1, j:j + 1, :] = jnp.sum(accw[kw + j], axis=0, keepdims=True)
        dwb_ref[0] = jnp.sum(accb[0], axis=0, keepdims=True)
        dwb_ref[1] = jnp.sum(accb[1], axis=0, keepdims=True)

    return _pcall(body, grid=(ns,),
                  in_specs=[pl.BlockSpec((t, c), lambda s: (0, s)),
                            pl.BlockSpec((t, c), lambda s: (0, s)), pl.BlockSpec((t, c), lambda s: (0, s + ns)),
                            pl.BlockSpec((kw, c), lambda s: (0, s)), pl.BlockSpec((kw, c), lambda s: (0, s + ns)),
                            pl.BlockSpec((1, c), lambda s: (0, s)), pl.BlockSpec((1, c), lambda s: (0, s + ns))],
                  out_specs=[pl.BlockSpec((2, t, c), lambda s: (0, 0, s)), pl.BlockSpec((2, kw, c), lambda s: (0, 0, s)),
                             pl.BlockSpec((2, 1, c), lambda s: (0, 0, s))],
                  out_shape=[jax.ShapeDtypeStruct((2, t, f), BF16), jax.ShapeDtypeStruct((2, kw, f), F32),
                             jax.ShapeDtypeStruct((2, 1, f), F32)],
                  scratch_shapes=[pltpu.VMEM((t + HALO, c), F32), pltpu.VMEM((t + HALO, c), F32),
                                  pltpu.VMEM((2 * kw, 8, c), F32), pltpu.VMEM((2, 8, c), F32),
                                  pltpu.VMEM((ROWS + HALO, c), F32), pltpu.VMEM((ROWS + HALO, c), F32)],
                  compiler_params=_params(("parallel",)), name=name)(da, u0, u0, dw_w, dw_w, dw_b, dw_b)


def _window_of(group):
    w = jnp.float32(POOL_WINDOWS[-1])
    for k in range(len(POOL_WINDOWS) - 2, -1, -1):
        w = jnp.where(group == k, jnp.float32(POOL_WINDOWS[k]), w)
    return w


def _select_level(group, levels):
    out = levels[-1]
    for k in range(len(levels) - 2, -1, -1):
        out = jnp.where(group == k, levels[k], out)
    return out


def _pool_fwd(h, name):
    t, d = h.shape
    c = LANES
    per = d // len(POOL_WINDOWS) // c

    def body(h_ref, o_ref):
        group = pl.program_id(0)
        wf = _window_of(group)

        def step(i, _):
            base = pl.multiple_of(i * ROWS, ROWS)
            lo = pl.multiple_of(jnp.maximum(base - HALO, 0), HALO)
            keep = jnp.where(i > 0, 1.0, 0.0)
            cur = h_ref[pl.ds(base, ROWS), :]
            e = jnp.concatenate([h_ref[pl.ds(lo, HALO), :] * keep, cur], axis=0)
            n = ROWS + HALO
            levels = []
            s = e
            for k in range(len(POOL_WINDOWS)):
                s = s + pltpu.roll(s, 1 << k, 0)
                levels.append(s[HALO:n])
            tpos = (base + lax.broadcasted_iota(jnp.int32, (ROWS, c), 0) + 1).astype(F32)
            pooled = _select_level(group, levels) / jnp.minimum(tpos, wf)
            o_ref[pl.ds(base, ROWS), :] = (pooled - cur).astype(BF16)
            return 0
        lax.fori_loop(0, _steps(t), step, 0)

    return _pcall(body, grid=(len(POOL_WINDOWS), per), in_specs=[pl.BlockSpec((t, c), lambda g, s: (0, g * per + s))],
                  out_specs=pl.BlockSpec((t, c), lambda g, s: (0, g * per + s)), out_shape=jax.ShapeDtypeStruct((t, d), BF16),
                  compiler_params=_params(("parallel", "parallel")), name=name)(h)


def _pool_bwd(dm, name):
    t, d = dm.shape
    c = LANES
    per = d // len(POOL_WINDOWS) // c

    def body(dm_ref, o_ref, pad):
        group = pl.program_id(0)
        wf = _window_of(group)
        pad[t:t + HALO, :] = jnp.zeros((HALO, c), F32)

        def fill(i, _):
            base = pl.multiple_of(i * ROWS, ROWS)
            tpos = (base + lax.broadcasted_iota(jnp.int32, (ROWS, c), 0) + 1).astype(F32)
            pad[pl.ds(base, ROWS), :] = dm_ref[pl.ds(base, ROWS), :] / jnp.minimum(tpos, wf)
            return 0
        lax.fori_loop(0, _steps(t), fill, 0)

        def step(i, _):
            base = pl.multiple_of(i * ROWS, ROWS)
            n = ROWS + HALO
            s = pad[pl.ds(base, n), :]
            levels = []
            for k in range(len(POOL_WINDOWS)):
                s = s + pltpu.roll(s, n - (1 << k), 0)
                levels.append(s[0:ROWS])
            o_ref[pl.ds(base, ROWS), :] = _select_level(group, levels) - dm_ref[pl.ds(base, ROWS), :]
            return 0
        lax.fori_loop(0, _steps(t), step, 0)

    return _pcall(body, grid=(len(POOL_WINDOWS), per), in_specs=[pl.BlockSpec((t, c), lambda g, s: (0, g * per + s))],
                  out_specs=pl.BlockSpec((t, c), lambda g, s: (0, g * per + s)), out_shape=jax.ShapeDtypeStruct((t, d), F32),
                  scratch_shapes=[pltpu.VMEM((t + HALO, c), F32)], compiler_params=_params(("parallel", "parallel")), name=name)(dm)


def _pool_mm_fwd(mixed, wg, scale, res, name):
    t, d = mixed.shape
    ng, gd, _ = wg.shape
    tm = _pick(t, 1024, 16)

    def body(a_ref, w_ref, s_ref, r_ref, o_ref):
        y = jnp.dot(a_ref[...], w_ref[...], preferred_element_type=F32)
        o_ref[...] = r_ref[...] + y * s_ref[...]

    blk = pl.BlockSpec((tm, gd), lambda g, i: (i, g))
    return _pcall(body, grid=(ng, t // tm),
                  in_specs=[blk, pl.BlockSpec((None, gd, gd), lambda g, i: (g, 0, 0)), pl.BlockSpec((1, gd), lambda g, i: (0, g)), blk],
                  out_specs=blk, out_shape=jax.ShapeDtypeStruct((t, d), F32),
                  compiler_params=_params(("parallel", "parallel")), name=name)(mixed, wg, scale, res)


def _pool_mm_bwd(dy, mixed, wg, scale, name):
    t, d = mixed.shape
    ng, gd, _ = wg.shape
    tm = _pick(t, 1024, 16)

    def body(dy_ref, a_ref, w_ref, s_ref, dm_ref, dw_ref, ds_ref):
        a, w, dyv = a_ref[...], w_ref[...], dy_ref[...]
        y = jnp.dot(a, w, preferred_element_type=F32)
        dyp = (dyv * s_ref[...]).astype(BF16)
        dm_ref[...] = lax.dot_general(dyp, w, (((1,), (1,)), ((), ())), preferred_element_type=F32)
        pw = lax.dot_general(a, dyp, (((0,), (0,)), ((), ())), preferred_element_type=F32)
        ps = jnp.sum(dyv * y, axis=0, keepdims=True)

        @pl.when(pl.program_id(1) == 0)
        def _():
            dw_ref[...] = pw
            ds_ref[...] = ps

        @pl.when(pl.program_id(1) > 0)
        def _():
            dw_ref[...] += pw
            ds_ref[...] += ps

    blk = pl.BlockSpec((tm, gd), lambda g, i: (i, g))
    wsp = pl.BlockSpec((None, gd, gd), lambda g, i: (g, 0, 0))
    vec = pl.BlockSpec((1, gd), lambda g, i: (0, g))
    return _pcall(body, grid=(ng, t // tm), in_specs=[blk, blk, wsp, vec], out_specs=[blk, wsp, vec],
                  out_shape=[jax.ShapeDtypeStruct((t, d), F32), jax.ShapeDtypeStruct((ng, gd, gd), F32), jax.ShapeDtypeStruct((1, d), F32)],
                  compiler_params=_params(("parallel", "arbitrary")), name=name)(dy, mixed, wg, scale)


def _rope_tables(positions):
    half = ROT_DIM // 2
    inv_freq = ROPE_THETA ** (-jnp.arange(0, ROT_DIM, 2, dtype=F32) / ROT_DIM)
    ang = positions.astype(F32)[:, None] * inv_freq
    cos, sin = jnp.cos(ang), jnp.sin(ang)
    t = positions.shape[0]
    ones = jnp.ones((t, HEAD - ROT_DIM), F32)
    zeros = jnp.zeros((t, HEAD - ROT_DIM), F32)
    zh = jnp.zeros((t, half), F32)
    c = jnp.concatenate([cos, cos, ones], axis=1)
    s1 = jnp.concatenate([-sin, zh, zeros], axis=1)
    s2 = jnp.concatenate([zh, sin, zeros], axis=1)
    return tuple(jnp.concatenate([a, a], axis=1) for a in (c, s1, s2))


def _half_mean(v, lo):
    s_lo = jnp.sum(jnp.where(lo, v, 0.0), axis=-1, keepdims=True)
    s_hi = jnp.sum(jnp.where(lo, 0.0, v), axis=-1, keepdims=True)
    return jnp.where(lo, s_lo, s_hi) * (1.0 / HEAD)


def _qk_prep_fwd(qkv, tabs, gq2, gk2, n_q, n_kv, name):
    t, width = qkv.shape
    tm = _pick(t, 256, 16)
    nqc, nkc = n_q * HEAD // LANES, n_kv * HEAD // LANES

    def body(x_ref, c_ref, s1_ref, s2_ref, gq_ref, gk_ref, q_ref, k2_ref, v2_ref):
        lo = lax.broadcasted_iota(jnp.int32, (tm, LANES), 1) < HEAD
        cv, s1, s2 = c_ref[...], s1_ref[...], s2_ref[...]

        def normrot(xc, g2):
            y = xc * lax.rsqrt(_half_mean(xc * xc, lo) + EPS) * g2
            return y * cv + pltpu.roll(y, LANES - ROT_DIM // 2, 1) * s1 + pltpu.roll(y, ROT_DIM // 2, 1) * s2

        def twice(y, j):
            sw = pltpu.roll(y, HEAD, 1)
            k2 = jnp.where(lo, y, sw) if j == 0 else jnp.where(lo, sw, y)
            return k2.astype(BF16)

        for ch in range(nqc):
            q_ref[:, ch * LANES:(ch + 1) * LANES] = normrot(x_ref[:, ch * LANES:(ch + 1) * LANES], gq_ref[...]).astype(BF16)
        for ch in range(nkc):
            off = (nqc + ch) * LANES
            y = normrot(x_ref[:, off:off + LANES], gk_ref[...])
            voff = (nqc + nkc + ch) * LANES
            vv = x_ref[:, voff:voff + LANES]
            for j in range(2):
                k2_ref[:, (2 * ch + j) * LANES:(2 * ch + j + 1) * LANES] = twice(y, j)
                v2_ref[:, (2 * ch + j) * LANES:(2 * ch + j + 1) * LANES] = twice(vv, j)

    row = lambda w: pl.BlockSpec((tm, w), lambda i: (i, 0))
    vec = pl.BlockSpec((1, LANES), lambda i: (0, 0))
    return _pcall(body, grid=(t // tm,), in_specs=[row(width), row(LANES), row(LANES), row(LANES), vec, vec],
                  out_specs=[row(n_q * HEAD), row(n_kv * LANES), row(n_kv * LANES)],
                  out_shape=[jax.ShapeDtypeStruct((t, n_q * HEAD), BF16), jax.ShapeDtypeStruct((t, n_kv * LANES), BF16),
                             jax.ShapeDtypeStruct((t, n_kv * LANES), BF16)],
                  compiler_params=_params(("parallel",)), name=name)(qkv, *tabs, gq2, gk2)


def _qk_prep_bwd(dq, dk_cur, dk_prev, dv_cur, dv_prev, qkv, tabs, gq2, gk2, n_q, n_kv, name):
    t, width = qkv.shape
    tm = Q_BLOCK
    nb = t // tm
    nqc, nkc = n_q * HEAD // LANES, n_kv * HEAD // LANES

    def body(dq_ref, kc_ref, kp_ref, vc_ref, vp_ref, x_ref, c_ref, s1_ref, s2_ref, gq_ref, gk_ref, o_ref, dgq_ref, dgk_ref):
        lo = lax.broadcasted_iota(jnp.int32, (tm, LANES), 1) < HEAD
        cv, s1, s2 = c_ref[...], s1_ref[...], s2_ref[...]
        more = jnp.where(pl.program_id(0) < nb - 1, 1.0, 0.0)

        def back(dy, xc, g2):
            dyn = dy * cv + pltpu.roll(dy * s1, ROT_DIM // 2, 1) + pltpu.roll(dy * s2, LANES - ROT_DIM // 2, 1)
            r = lax.rsqrt(_half_mean(xc * xc, lo) + EPS)
            xh = xc * r
            dxh = dyn * g2
            return r * (dxh - xh * _half_mean(dxh * xh, lo)), jnp.sum(dyn * xh, axis=0, keepdims=True)

        def unfold(cur_ref, prev_ref, ch):
            d0 = cur_ref[:, (2 * ch) * LANES:(2 * ch + 1) * LANES] + more * prev_ref[:, (2 * ch) * LANES:(2 * ch + 1) * LANES]
            d1 = cur_ref[:, (2 * ch + 1) * LANES:(2 * ch + 2) * LANES] + more * prev_ref[:, (2 * ch + 1) * LANES:(2 * ch + 2) * LANES]
            return jnp.where(lo, d0 + pltpu.roll(d0, HEAD, 1), d1 + pltpu.roll(d1, HEAD, 1))

        pq = jnp.zeros((1, LANES), F32)
        for ch in range(nqc):
            sl = slice(ch * LANES, (ch + 1) * LANES)
            dx, pg = back(dq_ref[:, sl], x_ref[:, sl], gq_ref[...])
            o_ref[:, sl] = dx.astype(BF16)
            pq = pq + pg
        pk = jnp.zeros((1, LANES), F32)
        for ch in range(nkc):
            sl = slice((nqc + ch) * LANES, (nqc + ch + 1) * LANES)
            dx, pg = back(unfold(kc_ref, kp_ref, ch), x_ref[:, sl], gk_ref[...])
            o_ref[:, sl] = dx.astype(BF16)
            pk = pk + pg
            vs = slice((nqc + nkc + ch) * LANES, (nqc + nkc + ch + 1) * LANES)
            o_ref[:, vs] = unfold(vc_ref, vp_ref, ch).astype(BF16)

        @pl.when(pl.program_id(0) == 0)
        def _():
            dgq_ref[...] = pq
            dgk_ref[...] = pk

        @pl.when(pl.program_id(0) > 0)
        def _():
            dgq_ref[...] += pq
            dgk_ref[...] += pk

    row = lambda w: pl.BlockSpec((tm, w), lambda i: (i, 0))
    nxt = lambda w: pl.BlockSpec((tm, w), lambda i: (jnp.minimum(i + 1, nb - 1), 0))
    vec = pl.BlockSpec((1, LANES), lambda i: (0, 0))
    kvw = n_kv * LANES
    return _pcall(body, grid=(nb,),
                  in_specs=[row(n_q * HEAD), row(kvw), nxt(kvw), row(kvw), nxt(kvw), row(width), row(LANES), row(LANES), row(LANES), vec, vec],
                  out_specs=[row(width), vec, vec],
                  out_shape=[jax.ShapeDtypeStruct((t, width), BF16), jax.ShapeDtypeStruct((1, LANES), F32), jax.ShapeDtypeStruct((1, LANES), F32)],
                  compiler_params=_params(("arbitrary",)), name=name)(dq, dk_cur, dk_prev, dv_cur, dv_prev, qkv, *tabs, gq2, gk2)


def _band_scores(qh, kc, kp, n, sink_row, lo_row, is_lo):
    scale = 1.0 / math.sqrt(HEAD)
    nt = (((1,), (1,)), ((), ()))
    s_c = lax.dot_general(qh, kc, nt, preferred_element_type=F32) * scale
    s_p = lax.dot_general(qh, kp, nt, preferred_element_type=F32) * scale
    qi = lax.broadcasted_iota(jnp.int32, (Q_BLOCK, Q_BLOCK), 0)
    kj = lax.broadcasted_iota(jnp.int32, (Q_BLOCK, Q_BLOCK), 1)
    s_c = jnp.where(kj <= qi, s_c, -jnp.inf)
    s_p = jnp.where((kj > qi) & (n > 0), s_p, -jnp.inf)
    pick = lo_row if is_lo else jnp.logical_not(lo_row)
    sink = jnp.max(jnp.where(pick, sink_row, -jnp.inf), axis=-1, keepdims=True)
    return s_c, s_p, sink


def _attn_fwd(q, k2, v2, sink_tab, name):
    t, dq = q.shape
    nc = dq // LANES
    nb = t // Q_BLOCK
    nkv = k2.shape[1] // LANES
    per_kv = nc // nkv

    def body(q_ref, kc_ref, kp_ref, vc_ref, vp_ref, s_ref, o_ref, lse_ref):
        n = pl.program_id(1)
        lo = lax.broadcasted_iota(jnp.int32, (Q_BLOCK, LANES), 1) < HEAD
        lo_row = lax.broadcasted_iota(jnp.int32, (1, LANES), 1) < HEAD
        kc, kp, vc, vp = kc_ref[...], kp_ref[...], vc_ref[...], vp_ref[...]
        for cc in range(per_kv):
            cols = slice(cc * LANES, (cc + 1) * LANES)
            qv = q_ref[:, cols].astype(F32)
            outs, lses = [], []
            for is_lo in (True, False):
                qh = jnp.where(lo, qv, 0.0) if is_lo else jnp.where(lo, 0.0, qv)
                s_c, s_p, sink = _band_scores(qh.astype(BF16), kc, kp, n, s_ref[8 * cc:8 * cc + 1, :], lo_row, is_lo)
                m = jnp.maximum(jnp.maximum(jnp.max(s_c, axis=-1, keepdims=True), jnp.max(s_p, axis=-1, keepdims=True)), sink)
                p_c, p_p = jnp.exp(s_c - m), jnp.exp(s_p - m)
                denom = jnp.sum(p_c, axis=-1, keepdims=True) + jnp.sum(p_p, axis=-1, keepdims=True) + jnp.exp(sink - m)
                pv = jnp.dot(p_c.astype(BF16), vc, preferred_element_type=F32) + jnp.dot(p_p.astype(BF16), vp, preferred_element_type=F32)
                outs.append(pv / denom)
                lses.append(m + jnp.log(denom))
            o_ref[:, cols] = jnp.where(lo, outs[0], outs[1]).astype(BF16)
            lse_ref[cc] = jnp.where(lo, lses[0], lses[1])

    qs = pl.BlockSpec((Q_BLOCK, per_kv * LANES), lambda k, n: (n, k))
    cur = pl.BlockSpec((Q_BLOCK, LANES), lambda k, n: (n, k))
    prev = pl.BlockSpec((Q_BLOCK, LANES), lambda k, n: (jnp.maximum(n - 1, 0), k))
    return _pcall(body, carry_us=ATTN_US_PER_STEP * nkv * nb, grid=(nkv, nb),
                  in_specs=[qs, cur, prev, cur, prev, pl.BlockSpec((8 * per_kv, LANES), lambda k, n: (k, 0))],
                  out_specs=[qs, pl.BlockSpec((per_kv, Q_BLOCK, LANES), lambda k, n: (k, n, 0))],
                  out_shape=[jax.ShapeDtypeStruct((t, dq), BF16), jax.ShapeDtypeStruct((nc, t, LANES), F32)],
                  compiler_params=_params(("parallel", "parallel")), name=name)(q, k2, k2, v2, v2, sink_tab)


def _attn_bwd(do, q, o, lse, k2, v2, sink_tab, name):
    t, dq = q.shape
    nc = dq // LANES
    nb = t // Q_BLOCK
    nkv = k2.shape[1] // LANES
    per_kv = nc // nkv
    scale = 1.0 / math.sqrt(HEAD)
    tn_ = (((0,), (0,)), ((), ()))
    nt = (((1,), (1,)), ((), ()))

    def body(do_ref, q_ref, o_ref, lse_ref, kc_ref, kp_ref, vc_ref, vp_ref, s_ref,
             dq_ref, dkc_ref, dkp_ref, dvc_ref, dvp_ref, dsk_ref):
        n = pl.program_id(1)
        lo = lax.broadcasted_iota(jnp.int32, (Q_BLOCK, LANES), 1) < HEAD
        lo_row = lax.broadcasted_iota(jnp.int32, (1, LANES), 1) < HEAD
        kc, kp, vc, vp = kc_ref[...], kp_ref[...], vc_ref[...], vp_ref[...]
        dkc = dkp = dvc = dvp = None
        for cc in range(per_kv):
            cols = slice(cc * LANES, (cc + 1) * LANES)
            qv, dov, ov, lsev = q_ref[:, cols].astype(F32), do_ref[:, cols], o_ref[:, cols].astype(F32), lse_ref[cc]
            dqs, dsinks = [], []
            for is_lo in (True, False):
                half = lo if is_lo else jnp.logical_not(lo)
                qh = jnp.where(half, qv, 0.0).astype(BF16)
                doh = jnp.where(half, dov, 0.0)
                s_c, s_p, sink = _band_scores(qh, kc, kp, n, s_ref[8 * cc:8 * cc + 1, :], lo_row, is_lo)
                lse_h = jnp.max(jnp.where(half, lsev, -jnp.inf), axis=-1, keepdims=True)
                p_c, p_p = jnp.exp(s_c - lse_h), jnp.exp(s_p - lse_h)
                delta = jnp.sum(doh * ov, axis=-1, keepdims=True)
                dob = doh.astype(BF16)
                ds_c = (p_c * (lax.dot_general(dob, vc, nt, preferred_element_type=F32) - delta)).astype(BF16)
                ds_p = (p_p * (lax.dot_general(dob, vp, nt, preferred_element_type=F32) - delta)).astype(BF16)
                dsinks.append(-jnp.sum(jnp.exp(sink - lse_h) * delta, axis=0, keepdims=True))
                dqs.append((jnp.dot(ds_c, kc, preferred_element_type=F32) + jnp.dot(ds_p, kp, preferred_element_type=F32)) * scale)
                parts = (lax.dot_general(ds_c, qh, tn_, preferred_element_type=F32) * scale,
                         lax.dot_general(ds_p, qh, tn_, preferred_element_type=F32) * scale,
                         lax.dot_general(p_c.astype(BF16), dob, tn_, preferred_element_type=F32),
                         lax.dot_general(p_p.astype(BF16), dob, tn_, preferred_element_type=F32))
                if dkc is None:
                    dkc, dkp, dvc, dvp = parts
                else:
                    dkc, dkp, dvc, dvp = dkc + parts[0], dkp + parts[1], dvc + parts[2], dvp + parts[3]
            dq_ref[:, cols] = jnp.where(lo, dqs[0], dqs[1])
            dsk_ref[8 * cc:8 * cc + 8, :] = jnp.zeros((8, LANES), F32) + jnp.where(lo_row, dsinks[0], dsinks[1])
        dkc_ref[...] = dkc
        dkp_ref[...] = dkp
        dvc_ref[...] = dvc
        dvp_ref[...] = dvp

    qs = pl.BlockSpec((Q_BLOCK, per_kv * LANES), lambda k, n: (n, k))
    cur = pl.BlockSpec((Q_BLOCK, LANES), lambda k, n: (n, k))
    prev = pl.BlockSpec((Q_BLOCK, LANES), lambda k, n: (jnp.maximum(n - 1, 0), k))
    kv_shape = jax.ShapeDtypeStruct((t, nkv * LANES), F32)
    return _pcall(body, carry_us=ATTN_US_PER_STEP * nkv * nb, grid=(nkv, nb),
                  in_specs=[qs, qs, qs, pl.BlockSpec((per_kv, Q_BLOCK, LANES), lambda k, n: (k, n, 0)),
                            cur, prev, cur, prev, pl.BlockSpec((8 * per_kv, LANES), lambda k, n: (k, 0))],
                  out_specs=[qs, cur, cur, cur, cur, pl.BlockSpec((None, None, 8 * per_kv, LANES), lambda k, n: (k, n, 0, 0))],
                  out_shape=[jax.ShapeDtypeStruct((t, dq), F32), kv_shape, kv_shape, kv_shape, kv_shape,
                             jax.ShapeDtypeStruct((nkv, nb, 8 * per_kv, LANES), F32)],
                  compiler_params=_params(("parallel", "parallel")), name=name)(do, q, o, lse, k2, k2, v2, v2, sink_tab)


def _peer(k):
    x, y, c = lax.axis_index("x"), lax.axis_index("y"), lax.axis_index("c")
    flip = lambda v, bit: 1 - v if bit else v
    return (flip(x, k & 4), flip(y, k & 2), flip(c, k & 1))


def _my_index():
    return 4 * lax.axis_index("x") + 2 * lax.axis_index("y") + lax.axis_index("c")


def _peer_index(k):
    px, py, pc = _peer(k)
    return 4 * px + 2 * py + pc


def _all_gather(shards, name):
    n = len(shards)
    any_spec = pl.BlockSpec(memory_space=pl.ANY)

    def body(*refs):
        ins, outs = refs[:n], refs[n:2 * n]
        send_sems, recv_sems, local_sems = refs[2 * n:]
        me = _my_index()
        local = [pltpu.make_async_copy(ins[a], outs[a].at[me], local_sems.at[a]) for a in range(n)]
        for cp in local:
            cp.start()
        sends = []
        for k in range(1, N_DEV):
            for a in range(n):
                cp = pltpu.make_async_remote_copy(src_ref=ins[a], dst_ref=outs[a].at[me], send_sem=send_sems.at[a, k - 1],
                                                  recv_sem=recv_sems.at[a, k - 1], device_id=_peer(k), device_id_type=MESH_ID)
                cp.start()
                sends.append(cp)
        for k in range(1, N_DEV):
            for a in range(n):
                pltpu.make_async_remote_copy(src_ref=ins[a], dst_ref=outs[a].at[_peer_index(k)], send_sem=send_sems.at[a, k - 1],
                                             recv_sem=recv_sems.at[a, k - 1], device_id=_peer(k), device_id_type=MESH_ID).wait_recv()
        for cp in sends:
            cp.wait_send()
        for cp in local:
            cp.wait()

    return _pcall(body, in_specs=[any_spec] * n, out_specs=[any_spec] * n,
                  out_shape=[jax.ShapeDtypeStruct((N_DEV,) + s.shape, s.dtype) for s in shards],
                  scratch_shapes=[pltpu.SemaphoreType.DMA((n, N_DEV - 1)), pltpu.SemaphoreType.DMA((n, N_DEV - 1)),
                                  pltpu.SemaphoreType.DMA((n,))],
                  name=name)(*shards)


GATHER1_PEERS = (1, 2, 4, 6)
GATHER2_PEERS = (2, 4, 6)
SCATTER_PEERS = tuple(range(1, N_DEV))
MAX_SEMS = N_DEV - 1
MAX_JOBS = 6
US_PER_MB = {"gather1": 5.4, "gather2": 0.6, "scatter": 10.8}
SCATTER_PIECE_US = 110.0
PIECE_US = {"gather1": 62.0, "gather2": 1e9, "scatter": SCATTER_PIECE_US}


class _Job:
    def __init__(self, key, kind, src, lo=0, hi=None, dst=None):
        self.key, self.kind, self.src, self.dst = key, kind, src, dst
        shape = src.shape if kind != "gather1" else (N_DEV,) + src.shape
        self.out_shape = jax.ShapeDtypeStruct(shape, src.dtype)
        self.rows = shape[1]
        self.lo, self.hi = lo, self.rows if hi is None else hi
        self.row_us = US_PER_MB[kind] * math.prod(shape) * src.dtype.itemsize / 1e6 / self.rows
        pieces = max(1, round(self.row_us * self.rows / PIECE_US[kind]))
        while pieces > 1 and self.rows % (16 * pieces):
            pieces -= 1
        self.piece = self.rows // pieces

    @property
    def cost_us(self):
        return self.row_us * (self.hi - self.lo)


class _Comm:
    def __init__(self):
        self.queue, self.gathered, self.scattered, self.layer = [], {}, [], 0

    def push(self, key, kind, src):
        self.queue.append(_Job(key, kind, src))

    def take(self, budget_us, upto=None):
        jobs = [j for j in self.queue if j.kind == "gather2"][:MAX_JOBS]
        used = sum(j.cost_us for j in jobs)
        if upto is not None and any(j.key == upto for j in jobs):
            self.queue = [j for j in self.queue if j not in jobs]
            return jobs
        for j in [j for j in self.queue if j.kind != "gather2"]:
            if len(jobs) >= MAX_JOBS:
                break
            urgent = j.kind == "gather1" and int(j.key[1]) <= self.layer
            piece_us = j.row_us * j.piece
            n = 0
            while j.lo + (n + 1) * j.piece <= j.hi and ((used < budget_us) if urgent else (used + 0.5 * piece_us <= budget_us)):
                n += 1
                used += piece_us
            if n == 0:
                break
            part = _Job(j.key, j.kind, j.src, j.lo, j.lo + n * j.piece, j.dst)
            part.parent = j
            j.lo = part.hi
            jobs.append(part)
            if j.lo < j.hi or j.key == upto:
                break
        self.queue = [j for j in self.queue if j not in jobs and j.lo < j.hi]
        return jobs

    def finish(self, job, result):
        if job.kind == "gather2":
            self.gathered[job.key] = result
        elif job.hi < job.rows:
            job.parent.dst = result
        elif job.kind == "gather1":
            self.queue.insert(0, _Job(job.key, "gather2", result))
        else:
            self.scattered.append((job.key, result))

    def need(self, key):
        while key not in self.gathered:
            assert any(j.key == key for j in self.queue), key
            self.flush(self.take(1e9, upto=key))
        return self.gathered[key]

    def flush(self, jobs):
        def body(o_ref):
            o_ref[...] = jnp.zeros_like(o_ref)
        _carry(body, jobs, self, dict(in_specs=[], out_specs=pl.BlockSpec(memory_space=pltpu.VMEM),
                                      out_shape=jax.ShapeDtypeStruct((8, LANES), F32), name="exchange"))()


def _job_copies(job, src, dst, send_sems, recv_sems, local_sem):
    me = _my_index()
    peers = {"gather1": GATHER1_PEERS, "gather2": GATHER2_PEERS, "scatter": SCATTER_PEERS}[job.kind]
    sends, recvs = [], []
    rows = pl.ds(job.lo, job.hi - job.lo)
    for i, k in enumerate(peers):
        if job.kind == "gather1":
            s_ref, d_ref, to, got = src.at[rows], dst.at[me, rows], _peer(k), dst.at[_peer_index(k), rows]
        elif job.kind == "gather2":
            s_ref, d_ref, to, got = src.at[_peer_index(k)], dst.at[_peer_index(k)], _peer(1), dst.at[_peer_index(k | 1)]
        else:
            s_ref, d_ref, to, got = src.at[_peer_index(k), rows], dst.at[me, rows], _peer(k), dst.at[_peer_index(k), rows]
        sends.append(pltpu.make_async_remote_copy(src_ref=s_ref, dst_ref=d_ref, send_sem=send_sems.at[i], recv_sem=recv_sems.at[i],
                                                  device_id=to, device_id_type=MESH_ID))
        recvs.append(pltpu.make_async_remote_copy(src_ref=s_ref, dst_ref=got, send_sem=send_sems.at[i], recv_sem=recv_sems.at[i],
                                                  device_id=to, device_id_type=MESH_ID))
    local = None
    if job.kind == "gather1":
        local = pltpu.make_async_copy(src.at[rows], dst.at[me, rows], local_sem)
    elif job.kind == "scatter":
        local = pltpu.make_async_copy(src.at[me, rows], dst.at[me, rows], local_sem)
    return sends, recvs, local


def _carry(body, jobs, comm, kw):
    kw = dict(kw)
    grid = tuple(kw.get("grid", ()))
    in_specs = list(kw["in_specs"])
    single = not isinstance(kw["out_specs"], (list, tuple))
    out_specs = [kw["out_specs"]] if single else list(kw["out_specs"])
    out_shape = [kw["out_shape"]] if single else list(kw["out_shape"])
    scratch = list(kw.get("scratch_shapes", []))
    n_in, n_out, n_scr, nj = len(in_specs), len(out_specs), len(scratch), len(jobs)
    any_spec = pl.BlockSpec(memory_space=pl.ANY)
    landed = [a for a, job in enumerate(jobs) if job.dst is not None]
    n_land = len(landed)

    def wrapped(*refs):
        pos = 0

        def take(k):
            nonlocal pos
            part = refs[pos:pos + k]
            pos += k
            return part
        ins, rin, _, outs, rout, scr = take(n_in), take(nj), take(n_land), take(n_out), take(nj), take(n_scr)
        send_sems, recv_sems, local_sems = take(3)

        def copies():
            return [_job_copies(job, rin[a], rout[a], send_sems.at[a], recv_sems.at[a], local_sems.at[a]) for a, job in enumerate(jobs)]

        def start():
            for sends, _, local in copies():
                if local is not None:
                    local.start()
                for cp in sends:
                    cp.start()

        def finish():
            for sends, recvs, local in copies():
                for cp in recvs:
                    cp.wait_recv()
                for cp in sends:
                    cp.wait_send()
                if local is not None:
                    local.wait()

        if grid:
            first = functools.reduce(jnp.logical_and, [pl.program_id(a) == 0 for a in range(len(grid))])
            last = functools.reduce(jnp.logical_and, [pl.program_id(a) == grid[a] - 1 for a in range(len(grid))])
            pl.when(first)(start)
            body(*ins, *outs, *scr)
            pl.when(last)(finish)
        else:
            start()
            body(*ins, *outs, *scr)
            finish()

    aliases = {n_in + a: n_out + a for a, job in enumerate(jobs) if job.kind == "gather2"}
    aliases.update({n_in + nj + i: n_out + a for i, a in enumerate(landed)})
    extra = dict(dimension_semantics=("arbitrary",) * len(grid)) if grid else {}
    call = _raw_call(wrapped, in_specs=in_specs + [any_spec] * (nj + n_land), out_specs=out_specs + [any_spec] * nj,
                     out_shape=out_shape + [job.out_shape for job in jobs],
                     scratch_shapes=scratch + [pltpu.SemaphoreType.DMA((nj, MAX_SEMS)), pltpu.SemaphoreType.DMA((nj, MAX_SEMS)),
                                               pltpu.SemaphoreType.DMA((nj,))],
                     input_output_aliases=aliases, compiler_params=_params(**extra), name=kw["name"],
                     **({"grid": grid} if grid else {}))

    def run(*args):
        res = call(*args, *[job.src for job in jobs], *[jobs[a].dst for a in landed])
        for job, r in zip(jobs, res[n_out:]):
            comm.finish(job, r)
        return res[0] if single else list(res[:n_out])
    return run


def _adam(g, w, m, v):
    m2 = ADAM_B1 * m + (1.0 - ADAM_B1) * g
    v2 = ADAM_B2 * v + (1.0 - ADAM_B2) * (g * g)
    m_hat = m2 / (1.0 - ADAM_B1 ** ADAM_STEP)
    v_hat = v2 / (1.0 - ADAM_B2 ** ADAM_STEP)
    delta = -ADAM_LR * (m_hat / (jnp.sqrt(v_hat) + ADAM_EPS) + ADAM_WD * w)
    return delta, m2, v2


def _sum_adam(parts, w, m, v, name):
    r, c = w.shape
    tr = _pick(r, max(8, (1 << 19) // c), 8)

    def body(p_ref, w_ref, m_ref, v_ref, g_ref, d_ref, m2_ref, v2_ref):
        g = p_ref[0].astype(F32)
        for j in range(1, N_DEV):
            g = g + p_ref[j].astype(F32)
        delta, m2, v2 = _adam(g, w_ref[...], m_ref[...], v_ref[...])
        g_ref[...] = g
        d_ref[...] = delta
        m2_ref[...] = m2
        v2_ref[...] = v2

    blk = pl.BlockSpec((tr, c), lambda i: (i, 0))
    shp = jax.ShapeDtypeStruct((r, c), F32)
    return _pcall(body, grid=(r // tr,),
                  in_specs=[pl.BlockSpec((N_DEV, tr, c), lambda i: (0, i, 0)), blk, blk, blk],
                  out_specs=[blk] * 4, out_shape=[shp] * 4, compiler_params=_params(("parallel",)), name=name)(parts, w, m, v)


def _small_layout(rep_shapes, sh_shapes):
    rows_r = [-(-s[1] // LANES) for s in rep_shapes]
    off_r = [sum(rows_r[:i]) for i in range(len(rows_r))]
    tot_r = -(-max(sum(rows_r), 8) // 8) * 8
    rows_s = [-(-s[-2] // 8) * 8 for s in sh_shapes]
    off_s = [sum(rows_s[:i]) for i in range(len(rows_s))]
    tot_s = max(sum(rows_s), 8)
    cmax = max([s[-1] for s in sh_shapes] + [LANES])
    return rows_r, off_r, tot_r, off_s, tot_s, cmax


def _small_exchange(rep_parts, sh_parts, name):
    nr, ns = len(rep_parts), len(sh_parts)
    rows_r, off_r, tot_r, off_s, tot_s, cmax = _small_layout([p.shape for p in rep_parts], [p.shape for p in sh_parts])
    vm = pl.BlockSpec(memory_space=pltpu.VMEM)

    def body(*refs):
        pos = 0

        def take(k):
            nonlocal pos
            out = refs[pos:pos + k]
            pos += k
            return out
        rp, sp = take(nr), take(ns)
        out_r, out_s = take(2)
        pack_r, got_r, pack_s, got_s, send_r, recv_r, send_s, recv_s = take(8)
        me = _my_index()
        pack_r[...] = jnp.zeros_like(pack_r)
        pack_s[...] = jnp.zeros_like(pack_s)
        for i in range(nr):
            nfull = rep_parts[i].shape[1]
            for rr in range(rows_r[i]):
                wdt = min(LANES, nfull - rr * LANES)
                pack_r[off_r[i] + rr:off_r[i] + rr + 1, 0:wdt] = rp[i][0:1, rr * LANES:rr * LANES + wdt]
        for i in range(ns):
            _, r_i, c_i = sh_parts[i].shape
            for j in range(N_DEV):
                pack_s[j, off_s[i]:off_s[i] + r_i, 0:c_i] = sp[i][j]
        got_r[me] = pack_r[...]
        got_s[me] = pack_s[me]
        sends = []
        for k in range(1, N_DEV):
            a = pltpu.make_async_remote_copy(src_ref=pack_r, dst_ref=got_r.at[me], send_sem=send_r.at[k - 1], recv_sem=recv_r.at[k - 1],
                                             device_id=_peer(k), device_id_type=MESH_ID)
            b = pltpu.make_async_remote_copy(src_ref=pack_s.at[_peer_index(k)], dst_ref=got_s.at[me], send_sem=send_s.at[k - 1],
                                             recv_sem=recv_s.at[k - 1], device_id=_peer(k), device_id_type=MESH_ID)
            a.start()
            b.start()
            sends += [a, b]
        for k in range(1, N_DEV):
            pltpu.make_async_remote_copy(src_ref=pack_r, dst_ref=got_r.at[_peer_index(k)], send_sem=send_r.at[k - 1],
                                         recv_sem=recv_r.at[k - 1], device_id=_peer(k), device_id_type=MESH_ID).wait_recv()
            pltpu.make_async_remote_copy(src_ref=pack_s.at[me], dst_ref=got_s.at[_peer_index(k)], send_sem=send_s.at[k - 1],
                                         recv_sem=recv_s.at[k - 1], device_id=_peer(k), device_id_type=MESH_ID).wait_recv()
        for cp in sends:
            cp.wait_send()
        tot_rep = got_r[0]
        tot_sh = got_s[0]
        for j in range(1, N_DEV):
            tot_rep = tot_rep + got_r[j]
            tot_sh = tot_sh + got_s[j]
        out_r[...] = tot_rep
        out_s[...] = tot_sh

    return _pcall(body, carry_us=SMALL_EXCHANGE_CARRY_US, in_specs=[vm] * (nr + ns), out_specs=[vm] * 2,
                  out_shape=[jax.ShapeDtypeStruct((tot_r, LANES), F32), jax.ShapeDtypeStruct((tot_s, cmax), F32)],
                  scratch_shapes=[pltpu.VMEM((tot_r, LANES), F32), pltpu.VMEM((N_DEV, tot_r, LANES), F32),
                                  pltpu.VMEM((N_DEV, tot_s, cmax), F32), pltpu.VMEM((N_DEV, tot_s, cmax), F32),
                                  pltpu.SemaphoreType.DMA((N_DEV - 1,)), pltpu.SemaphoreType.DMA((N_DEV - 1,)),
                                  pltpu.SemaphoreType.DMA((N_DEV - 1,)), pltpu.SemaphoreType.DMA((N_DEV - 1,))],
                  compiler_params=_params(), name=name)(*rep_parts, *sh_parts)


def _small_adam(tot_rep, tot_sh, rep_w, rep_m, rep_v, sh_w, sh_m, sh_v, name):
    nr, ns = len(rep_w), len(sh_w)
    rows_r, off_r, _, off_s, _, _ = _small_layout([w.shape for w in rep_w], [w.shape for w in sh_w])
    vm = pl.BlockSpec(memory_space=pltpu.VMEM)

    def body(*refs):
        pos = 0

        def take(k):
            nonlocal pos
            out = refs[pos:pos + k]
            pos += k
            return out
        (tr_ref, ts_ref), rw, rm, rv, sw, sm, sv = take(2), take(nr), take(nr), take(nr), take(ns), take(ns), take(ns)
        rg, rd, rm2, rv2 = take(nr), take(nr), take(nr), take(nr)
        sg, sd, sm2, sv2 = take(ns), take(ns), take(ns), take(ns)
        for i in range(nr):
            nfull = rep_w[i].shape[1]
            for rr in range(rows_r[i]):
                wdt = min(LANES, nfull - rr * LANES)
                rg[i][0:1, rr * LANES:rr * LANES + wdt] = tr_ref[off_r[i] + rr:off_r[i] + rr + 1, 0:wdt]
            delta, m2, v2 = _adam(rg[i][...], rw[i][...], rm[i][...], rv[i][...])
            rd[i][...] = delta
            rm2[i][...] = m2
            rv2[i][...] = v2
        for i in range(ns):
            r_i, c_i = sh_w[i].shape
            g = ts_ref[off_s[i]:off_s[i] + r_i, 0:c_i]
            delta, m2, v2 = _adam(g, sw[i][...], sm[i][...], sv[i][...])
            sg[i][...] = g
            sd[i][...] = delta
            sm2[i][...] = m2
            sv2[i][...] = v2

    shapes = [jax.ShapeDtypeStruct(w.shape, F32) for w in rep_w] * 4 + [jax.ShapeDtypeStruct(w.shape, F32) for w in sh_w] * 4
    outs = _pcall(body, in_specs=[vm] * (2 + 3 * nr + 3 * ns), out_specs=[vm] * len(shapes), out_shape=shapes,
                  compiler_params=_params(), name=name)(tot_rep, tot_sh, *rep_w, *rep_m, *rep_v, *sh_w, *sh_m, *sh_v)
    rep_out = [outs[i * nr:(i + 1) * nr] for i in range(4)]
    sh_out = [outs[4 * nr + i * ns:4 * nr + (i + 1) * ns] for i in range(4)]
    return rep_out, sh_out


_CONF = ("norm_g", "a_w_in", "a_b_in", "a_dw_w", "a_dw_b", "a_ln_g", "a_ln_b", "a_w_out", "a_b_out")
_FFN = ("ffn_norm_g", "ffn_w_up", "ffn_dw_w", "ffn_dw_b", "ffn_w_down")
_POOL = ("norm_g", "b_w_group", "b_scale")
_ATTN = ("norm_g", "c_w_qkv", "c_q_norm_g", "c_k_norm_g", "c_sinks", "c_w_o")
_LAYERS = (_CONF + _FFN, _POOL + _FFN, _ATTN + _FFN, _CONF + _FFN)
_NAMES = tuple("l%d_%s" % (i, n) for i, names in enumerate(_LAYERS) for n in names)
_BIG = ("a_w_in", "a_w_out", "ffn_w_up", "ffn_w_down", "b_w_group", "c_w_qkv", "c_w_o")
_SHARDED_SMALL = ("a_dw_w", "ffn_dw_w")


def _pad_rows(a, mult=8):
    r = a.shape[0]
    rp = -(-r // mult) * mult
    return a if rp == r else jnp.pad(a, ((0, rp - r), (0, 0)))


def _unstack_cols(st, rows):
    s, r, cs = st.shape
    return jnp.transpose(st, (1, 0, 2)).reshape(r, s * cs)[:rows]


def _stack_cols(a):
    r, c = a.shape
    return jnp.transpose(a.reshape(r, N_DEV, c // N_DEV), (1, 0, 2))


def _row(v):
    return v.reshape(1, -1)


def _ffn_forward(x_mid, p, tag):
    h2 = _rms_fwd(x_mid, _row(p["ffn_norm_g"]), BF16, "rms_fwd_bf16")
    u0 = _mm(h2, p["ffn_w_up"], "nn", name="ffn_up", tn=1408, tk=2048)
    a = _ffn_act_fwd(u0, p["ffn_dw_w"], _row(p["ffn_dw_b"]), "ffn_act_fwd")
    x_out = _mm(a, p["ffn_w_down"], "nn", res=x_mid, name="ffn_down", tk=1408)
    return x_out, dict(h2=h2, u0=u0, a=a)


def _ffn_backward(dx_out, x_mid, p, sv, grads):
    dwd = _mm(sv["a"], dx_out, "tn", out_dtype=BF16, name="ffn_down_dw", tm=1408)
    grads["ffn_w_down"] = dwd.reshape(N_DEV, dwd.shape[0] // N_DEV, dwd.shape[1])
    da = _mm(dx_out, p["ffn_w_down"], "nt", name="ffn_down_dx", tn=1408, tk=2048)
    du0, dww, dwb = _ffn_act_bwd(da, sv["u0"], p["ffn_dw_w"], _row(p["ffn_dw_b"]), "ffn_act_bwd")
    kw = dww.shape[1]
    grads["ffn_dw_w"] = _stack_cols(jnp.transpose(dww, (1, 0, 2)).reshape(kw, -1))
    grads["ffn_dw_b"] = dwb.reshape(1, -1)
    grads["ffn_w_up"] = _mm(sv["h2"], du0, "tn", out_dtype=BF16, out_stack=N_DEV, name="ffn_up_dw", tn=1408)
    dh2 = _mm(du0, p["ffn_w_up"], "nt", name="ffn_up_dx", tk=1408)
    dx_mid, dg, _ = _rms_bwd(dh2, x_mid, _row(p["ffn_norm_g"]), dx_out, "rms_bwd")
    grads["ffn_norm_g"] = dg
    return dx_mid


def _conf_forward(x, p):
    h = _rms_fwd(x, _row(p["norm_g"]), BF16, "rms_fwd_bf16")
    u = _mm(h, p["a_w_in"], "nn", bias=_row(p["a_b_in"]), name="conf_in", tn=512, tk=2048)
    cpre = _conf_conv_fwd(u, p["a_dw_w"], _row(p["a_dw_b"]), "conf_conv_fwd")
    s = _ln_silu_fwd(cpre, _row(p["a_ln_g"]), _row(p["a_ln_b"]), "ln_silu_fwd")
    x_mid = _mm(s, p["a_w_out"], "nn", bias=_row(p["a_b_out"]), res=x, name="conf_out", tk=2048)
    return x_mid, dict(h=h, u=u, cpre=cpre, s=s)


def _conf_backward(dx_mid, x, p, sv, grads):
    dwo = _mm(sv["s"], dx_mid, "tn", out_dtype=BF16, name="conf_out_dw")
    grads["a_w_out"] = dwo.reshape(N_DEV, dwo.shape[0] // N_DEV, dwo.shape[1])
    ds = _mm(dx_mid, p["a_w_out"], "nt", name="conf_out_dx", tk=2048)
    dc, dlg, dlb = _ln_silu_bwd(ds, sv["cpre"], _row(p["a_ln_g"]), _row(p["a_ln_b"]), "ln_silu_bwd")
    grads["a_ln_g"], grads["a_ln_b"] = dlg, dlb
    du, dww, dwb, dbin = _conf_conv_bwd(dc, sv["u"], p["a_dw_w"], "conf_conv_bwd")
    grads["a_dw_w"] = _stack_cols(dww)
    grads["a_dw_b"] = dwb
    grads["a_b_in"] = dbin.reshape(1, -1)
    grads["a_w_in"] = _mm(sv["h"], du, "tn", out_dtype=BF16, out_stack=N_DEV, name="conf_in_dw", tn=512)
    dh = _mm(du, p["a_w_in"], "nt", name="conf_in_dx", tk=512)
    dx, dg, dbo = _rms_bwd(dh, x, _row(p["norm_g"]), dx_mid, "rms_bwd")
    grads["norm_g"] = dg
    grads["a_b_out"] = dbo
    return dx


def _pool_forward(x, p):
    h = _rms_fwd(x, _row(p["norm_g"]), F32, "rms_fwd_f32")
    mixed = _pool_fwd(h, "pool_fwd")
    x_mid = _pool_mm_fwd(mixed, p["b_w_group"], _row(p["b_scale"]), x, "pool_mm_fwd")
    return x_mid, dict(mixed=mixed)


def _pool_backward(dx_mid, x, p, sv, grads):
    dmixed, dwg, dscale = _pool_mm_bwd(dx_mid, sv["mixed"], p["b_w_group"], _row(p["b_scale"]), "pool_mm_bwd")
    ng, gd, _ = dwg.shape
    grads["b_w_group"] = jnp.transpose(dwg.reshape(ng, N_DEV, gd // N_DEV, gd), (1, 0, 2, 3)).reshape(N_DEV, ng * gd // N_DEV, gd).astype(BF16)
    grads["b_scale"] = dscale
    dh = _pool_bwd(dmixed, "pool_bwd")
    dx, dg, _ = _rms_bwd(dh, x, _row(p["norm_g"]), dx_mid, "rms_bwd")
    grads["norm_g"] = dg
    return dx


def _attn_tables(p, positions, d_model):
    n_q = d_model // HEAD
    n_kv = n_q // 8
    tabs = _rope_tables(positions)
    gq2 = jnp.concatenate([p["c_q_norm_g"], p["c_q_norm_g"]]).reshape(1, LANES)
    gk2 = jnp.concatenate([p["c_k_norm_g"], p["c_k_norm_g"]]).reshape(1, LANES)
    sink_tab = jnp.repeat(jnp.repeat(p["c_sinks"].reshape(-1, 2), HEAD, axis=1), 8, axis=0)
    return n_q, n_kv, tabs, gq2, gk2, sink_tab


def _attn_forward(x, p, positions):
    n_q, n_kv, tabs, gq2, gk2, sink_tab = _attn_tables(p, positions, x.shape[1])
    h = _rms_fwd(x, _row(p["norm_g"]), BF16, "rms_fwd_bf16")
    qkv = _mm(h, p["c_w_qkv"], "nn", name="attn_qkv", tn=1280, tk=2048)
    q, k2, v2 = _qk_prep_fwd(qkv, tabs, gq2, gk2, n_q, n_kv, "qk_prep_fwd")
    o, lse = _attn_fwd(q, k2, v2, sink_tab, "attn_fwd")
    x_mid = _mm(o, p["c_w_o"], "nn", res=x, name="attn_out", tk=2048)
    return x_mid, dict(h=h, qkv=qkv, q=q, k2=k2, v2=v2, o=o, lse=lse)


def _attn_backward(dx_mid, x, p, positions, sv, grads):
    n_q, n_kv, tabs, gq2, gk2, sink_tab = _attn_tables(p, positions, x.shape[1])
    dwo = _mm(sv["o"], dx_mid, "tn", out_dtype=BF16, name="attn_out_dw")
    grads["c_w_o"] = dwo.reshape(N_DEV, dwo.shape[0] // N_DEV, dwo.shape[1])
    do = _mm(dx_mid, p["c_w_o"], "nt", name="attn_out_dx", tk=2048)
    dq, dkc, dkp, dvc, dvp, dsk = _attn_bwd(do, sv["q"], sv["o"], sv["lse"], sv["k2"], sv["v2"], sink_tab, "attn_bwd")
    nkv_, nb = dsk.shape[0], dsk.shape[1]
    dsk = dsk.reshape(nkv_, nb, -1, 8, LANES)[:, :, :, 0, :].sum(axis=1).reshape(-1, LANES)
    grads["c_sinks"] = jnp.stack([dsk[:, 0], dsk[:, HEAD]], axis=1).reshape(1, -1)
    dqkv, dgq, dgk = _qk_prep_bwd(dq, dkc, dkp, dvc, dvp, sv["qkv"], tabs, gq2, gk2, n_q, n_kv, "qk_prep_bwd")
    grads["c_q_norm_g"] = dgq[:, :HEAD] + dgq[:, HEAD:]
    grads["c_k_norm_g"] = dgk[:, :HEAD] + dgk[:, HEAD:]
    dwq = _mm(sv["h"], dqkv, "tn", out_dtype=BF16, name="attn_qkv_dw", tn=1280)
    grads["c_w_qkv"] = _stack_cols(dwq)
    dh = _mm(dqkv, p["c_w_qkv"], "nt", name="attn_qkv_dx", tk=1280)
    dx, dg, _ = _rms_bwd(dh, x, _row(p["norm_g"]), dx_mid, "rms_bwd")
    grads["norm_g"] = dg
    return dx


class _LayerWeights:
    def __init__(self, li, weights, small_full, comm):
        self.li, self.weights, self.small_full, self.comm, self.cache = li, weights, small_full, comm, {}

    def __getitem__(self, nme):
        if nme not in self.cache:
            self.cache[nme] = self.fetch(nme)
        return self.cache[nme]

    def fetch(self, nme):
        full = "l%d_%s" % (self.li, nme)
        w = self.weights[full]
        if nme in _SHARDED_SMALL:
            return _unstack_cols(self.small_full[full], w.shape[0])
        if nme not in _BIG:
            return w
        got = self.comm.need(full)
        if nme in ("a_w_in", "ffn_w_up"):
            return got
        if nme == "c_w_qkv":
            return _unstack_cols(got, w.shape[0])
        if nme == "b_w_group":
            ng, gs, gd = w.shape
            return jnp.transpose(got.reshape(N_DEV, ng, gs, gd), (1, 0, 2, 3)).reshape(ng, N_DEV * gs, gd)
        return got.reshape(-1, w.shape[1])


class _LayerGrads(dict):
    def __init__(self, li, comm):
        super().__init__()
        self.li, self.comm = li, comm

    def __setitem__(self, nme, value):
        if nme in _BIG:
            self.comm.push("l%d_%s" % (self.li, nme), "scatter", value)
        else:
            super().__setitem__(nme, value)


def kernel(*args):
    n_w = len(_NAMES)
    x, positions = args[0], args[1]
    weights = dict(zip(_NAMES, args[2:2 + n_w]))
    loss_target = args[2 + n_w]
    moms = dict(zip(_NAMES, args[3 + n_w:3 + 2 * n_w]))
    vels = dict(zip(_NAMES, args[3 + 2 * n_w:3 + 3 * n_w]))
    x0 = x[0]
    pos = positions[0]
    kinds = ("conf", "pool", "attn", "conf")
    comm = _Comm()
    _STATE["comm"], _STATE["last"] = comm, None
    shd = [n for n in _NAMES if n.split("_", 1)[1] in _SHARDED_SMALL]
    small_full = dict(zip(shd, _all_gather([_pad_rows(weights[n]) for n in shd], "gather_small")))
    for n in _NAMES:
        if n.split("_", 1)[1] in _BIG:
            w = weights[n]
            comm.push(n, "gather1", w.astype(BF16).reshape(-1, w.shape[-1]))
    results = {}

    def update_ready():
        while comm.scattered:
            full, parts = comm.scattered.pop(0)
            w = weights[full]
            w2 = w.reshape(-1, w.shape[-1])
            outs = _sum_adam(parts, w2, moms[full].reshape(w2.shape), vels[full].reshape(w2.shape), "adam_" + full.split("_", 1)[1])
            results[full] = tuple(o.reshape(w.shape) for o in outs)

    params, saved = [], []
    cur = x0
    for li, names in enumerate(_LAYERS):
        comm.layer = li
        p = _LayerWeights(li, weights, small_full, comm)
        if kinds[li] == "conf":
            x_mid, sv = _conf_forward(cur, p)
        elif kinds[li] == "pool":
            x_mid, sv = _pool_forward(cur, p)
        else:
            x_mid, sv = _attn_forward(cur, p, pos)
        x_out, sv_f = _ffn_forward(x_mid, p, kinds[li])
        params.append(p)
        saved.append((sv, sv_f, cur, x_mid))
        cur = x_out
    dy, loss_part = _loss_head(cur, loss_target[0], "loss_head")
    loss = lax.psum(loss_part[0, 0], ("x", "y", "c"))

    small_grads = {}
    dcur = dy
    for li in range(len(_LAYERS) - 1, -1, -1):
        p = params[li]
        sv, sv_f, x_in, x_mid = saved[li]
        grads = _LayerGrads(li, comm)
        dmid = _ffn_backward(dcur, x_mid, p, sv_f, grads)
        update_ready()
        if kinds[li] == "conf":
            dcur = _conf_backward(dmid, x_in, p, sv, grads)
        elif kinds[li] == "pool":
            dcur = _pool_backward(dmid, x_in, p, sv, grads)
        else:
            dcur = _attn_backward(dmid, x_in, p, pos, sv, grads)
        update_ready()
        for n in _LAYERS[li]:
            if n not in _BIG:
                small_grads["l%d_%s" % (li, n)] = grads[n]
    rep = [n for n in _NAMES if n.split("_", 1)[1] not in _BIG and n.split("_", 1)[1] not in _SHARDED_SMALL]
    tot_rep, tot_sh = _small_exchange([small_grads[n] for n in rep], [small_grads[n] for n in shd], "small_exchange")
    while comm.queue or comm.scattered:
        if not comm.scattered:
            comm.flush(comm.take(1e9))
        update_ready()
    _STATE["comm"] = None
    rep_out, sh_out = _small_adam(tot_rep, tot_sh, [_row(weights[n]) for n in rep], [_row(moms[n]) for n in rep], [_row(vels[n]) for n in rep],
                                  [weights[n] for n in shd], [moms[n] for n in shd], [vels[n] for n in shd], "small_adam")
    for i, n in enumerate(rep):
        results[n] = tuple(rep_out[k][i].reshape(weights[n].shape) for k in range(4))
    for i, n in enumerate(shd):
        results[n] = tuple(sh_out[k][i] for k in range(4))

    _STATE["last"] = None
    grad_x = dcur[None]
    out = [loss, grad_x]
    for k in range(4):
        out += [results[n][k] for n in _NAMES]
    return tuple(out)
```

```python
import functools
import math

import jax
import jax.numpy as jnp
from jax import lax
from jax.experimental import pallas as pl
from jax.experimental.pallas import tpu as pltpu

F32 = jnp.float32
BF16 = jnp.bfloat16
N_DEV = 8
EPS = 1e-6
LANES = 128
HEAD = 64
Q_BLOCK = 128
ROT_DIM = 16
ROPE_THETA = 500000.0
POOL_WINDOWS = (2, 4, 8, 16)
HALO = 32
ROWS = 128
VMEM_LIMIT = 56 * 1024 * 1024
ADAM_LR, ADAM_B1, ADAM_B2, ADAM_EPS, ADAM_WD, ADAM_STEP = 0.001, 0.9, 0.999, 1e-08, 0.01, 10
MESH_ID = pl.DeviceIdType.MESH
MXU_FLOPS_PER_US = 7.5e8
HBM_BYTES_PER_US = 2.5e6
ATTN_US_PER_STEP = 1.5
CONV_FWD_US_PER_ELEM = 1.1e-5
CONV_BWD_US_PER_ELEM = 2.3e-5
ACT_FWD_US_PER_ELEM = 4.8e-6
SMALL_EXCHANGE_CARRY_US = 100.0


def _make_call(body, **kw):
    return pl.pallas_call(body, **kw)


_STATE = {"comm": None, "last": None}


def _raw_call(body, **kw):
    call = _make_call(body, **kw)

    def run(*args):
        last = _STATE["last"]
        if last is not None and args:
            first, _ = lax.optimization_barrier((args[0], last))
            args = (first,) + tuple(args[1:])
        res = call(*args)
        _STATE["last"] = res[0] if isinstance(res, (list, tuple)) else res
        return res
    return run


def _pcall(body, carry_us=0.0, **kw):
    comm = _STATE["comm"]
    jobs = comm.take(carry_us) if (comm is not None and carry_us > 0) else []
    if not jobs:
        return _raw_call(body, **kw)
    return _carry(body, jobs, comm, kw)


def _params(sem=None, **kw):
    if sem is not None:
        kw["dimension_semantics"] = sem
    return pltpu.CompilerParams(vmem_limit_bytes=VMEM_LIMIT, **kw)


def _pick(dim, pref, mult=LANES):
    best = None
    d = mult
    while d <= min(dim, pref):
        if dim % d == 0:
            best = d
        d += mult
    return dim if best is None else best


def _sigmoid(x):
    return 1.0 / (1.0 + jnp.exp(-x))


def _fold8(p):
    r, c = p.shape
    return p.reshape(r // 8, 8, c).sum(axis=0)


def _window(win_ref, e):
    win_ref[...] = e
    return win_ref


def _rows(win_ref, k, r):
    return win_ref[k:k + r, :]


def _lshape(a):
    return a.shape if a.ndim == 2 else (a.shape[1], a.shape[0] * a.shape[2])


def _panel(a):
    return a.shape[1] if a.ndim == 2 else a.shape[2]


def _lspec(a, br, bc, rc):
    if a.ndim == 2:
        return pl.BlockSpec((br, bc), rc)
    per = a.shape[2] // bc

    def idx(*g):
        r, c = rc(*g)
        return (c // per, r, c % per)
    return pl.BlockSpec((None, br, bc), idx)


def _mm(a, b, dims, *, name, out_dtype=F32, out_stack=None, bias=None, res=None, tm=1024, tn=1024, tk=1024, panels=1):
    (ar, ac), (br_, bc_) = _lshape(a), _lshape(b)
    if dims == "nn":
        m, k, n = ar, ac, bc_
        lim_m, lim_k, lim_n = m, min(_panel(a), k), _panel(b)
    elif dims == "nt":
        m, k, n = ar, ac, br_
        lim_m, lim_k, lim_n = m, math.gcd(_panel(a), _panel(b)), n
    else:
        m, k, n = ac, ar, bc_
        lim_m, lim_k, lim_n = _panel(a), k, _panel(b)
    if out_stack is not None:
        lim_n = math.gcd(lim_n, n // out_stack)
    sub = 16 if (out_dtype == BF16 or a.dtype == BF16) else 8
    tm = _pick(lim_m, tm, LANES if dims == "tn" else sub)
    tn = _pick(lim_n, tn)
    tk = _pick(lim_k, tk, sub if dims == "tn" else LANES)
    if panels > 1:
        assert dims == "nt" and b.ndim == 3 and tk == b.shape[2] and b.shape[0] % panels == 0 and _panel(a) % (panels * tk) == 0
    nk = k // (tk * panels)
    if dims == "tn":
        a_spec = _lspec(a, tk, tm, lambda i, j, kk: (kk, i))
    else:
        a_spec = _lspec(a, tm, tk * panels, lambda i, j, kk: (i, kk))
    if panels > 1:
        b_spec = pl.BlockSpec((panels, tn, tk), lambda i, j, kk: (kk, j, 0))
    elif dims == "nt":
        b_spec = _lspec(b, tn, tk, lambda i, j, kk: (j, kk))
    else:
        b_spec = _lspec(b, tk, tn, lambda i, j, kk: (kk, j))
    contract = {"nn": ((1,), (0,)), "nt": ((1,), (1,)), "tn": ((0,), (0,))}[dims]
    in_specs, args = [a_spec, b_spec], [a, b]
    if bias is not None:
        in_specs.append(pl.BlockSpec((1, tn), lambda i, j, kk: (0, j)))
        args.append(bias)
    if res is not None:
        in_specs.append(pl.BlockSpec((tm, tn), lambda i, j, kk: (i, j)))
        args.append(res)
    if out_stack is None:
        out_shape = jax.ShapeDtypeStruct((m, n), out_dtype)
    else:
        out_shape = jax.ShapeDtypeStruct((out_stack, m, n // out_stack), out_dtype)
    o_spec = _lspec(out_shape, tm, tn, lambda i, j, kk: (i, j))
    has_bias, has_res = bias is not None, res is not None

    def body(*refs):
        a_ref, b_ref = refs[0], refs[1]
        pos = 2
        bias_ref = res_ref = None
        if has_bias:
            bias_ref = refs[pos]
            pos += 1
        if has_res:
            res_ref = refs[pos]
            pos += 1
        o_ref = refs[pos]

        def part():
            if panels == 1:
                return lax.dot_general(a_ref[...].astype(BF16), b_ref[...].astype(BF16), (contract, ((), ())),
                                       preferred_element_type=F32)
            r = None
            for q in range(panels):
                t_ = lax.dot_general(a_ref[:, q * tk:(q + 1) * tk].astype(BF16), b_ref[q].astype(BF16), (contract, ((), ())),
                                     preferred_element_type=F32)
                r = t_ if r is None else r + t_
            return r

        def finish(r):
            if has_bias:
                r = r + bias_ref[...]
            if has_res:
                r = r + res_ref[...]
            o_ref[...] = r.astype(out_dtype)

        if nk == 1:
            finish(part())
        else:
            acc = refs[pos + 1]
            kk = pl.program_id(2)

            @pl.when(kk == 0)
            def _():
                acc[...] = part()

            @pl.when(kk > 0)
            def _():
                acc[...] += part()

            @pl.when(kk == nk - 1)
            def _():
                finish(acc[...])

    scratch = [] if nk == 1 else [pltpu.VMEM((tm, tn), F32)]
    return _pcall(body, carry_us=2.0 * m * n * k / MXU_FLOPS_PER_US, grid=(m // tm, n // tn, nk), in_specs=in_specs, out_specs=o_spec, out_shape=out_shape,
                  scratch_shapes=scratch, compiler_params=_params(("parallel", "parallel", "arbitrary")), name=name)(*args)


def _rms_fwd(x, g, out_dtype, name):
    t, d = x.shape
    tm = _pick(t, 512, 16)

    def body(x_ref, g_ref, o_ref):
        xv = x_ref[...]
        r = lax.rsqrt(jnp.mean(xv * xv, axis=-1, keepdims=True) + EPS)
        o_ref[...] = ((xv * r) * g_ref[...]).astype(out_dtype)

    return _pcall(body, carry_us=6.0 * t * d / HBM_BYTES_PER_US, grid=(t // tm,),
                  in_specs=[pl.BlockSpec((tm, d), lambda i: (i, 0)), pl.BlockSpec((1, d), lambda i: (0, 0))],
                  out_specs=pl.BlockSpec((tm, d), lambda i: (i, 0)), out_shape=jax.ShapeDtypeStruct((t, d), out_dtype),
                  compiler_params=_params(("parallel",)), name=name)(x, g)


def _rms_bwd(dh, x, g, dres, name):
    t, d = x.shape
    tm = _pick(t, 256, 8)

    def body(dh_ref, x_ref, g_ref, dres_ref, dx_ref, dg_ref, cs_ref):
        xv, dhv, dr = x_ref[...], dh_ref[...], dres_ref[...]
        r = lax.rsqrt(jnp.mean(xv * xv, axis=-1, keepdims=True) + EPS)
        xh = xv * r
        dxh = dhv * g_ref[...]
        dx_ref[...] = dr + r * (dxh - xh * jnp.mean(dxh * xh, axis=-1, keepdims=True))
        pg = jnp.sum(dhv * xh, axis=0, keepdims=True)
        pc = jnp.sum(dr, axis=0, keepdims=True)

        @pl.when(pl.program_id(0) == 0)
        def _():
            dg_ref[...] = pg
            cs_ref[...] = pc

        @pl.when(pl.program_id(0) > 0)
        def _():
            dg_ref[...] += pg
            cs_ref[...] += pc

    row = pl.BlockSpec((tm, d), lambda i: (i, 0))
    vec = pl.BlockSpec((1, d), lambda i: (0, 0))
    return _pcall(body, carry_us=16.0 * t * d / HBM_BYTES_PER_US, grid=(t // tm,), in_specs=[row, row, vec, row], out_specs=[row, vec, vec],
                  out_shape=[jax.ShapeDtypeStruct((t, d), F32), jax.ShapeDtypeStruct((1, d), F32), jax.ShapeDtypeStruct((1, d), F32)],
                  compiler_params=_params(("arbitrary",)), name=name)(dh, x, g, dres)


def _loss_head(y, target, name):
    t, d = y.shape
    tm = _pick(t, 512, 8)

    def body(y_ref, t_ref, dy_ref, l_ref):
        e = y_ref[...] - t_ref[...]
        dy_ref[...] = e * (1.0 / d)
        part = 0.5 * jnp.sum(jnp.mean(e * e, axis=-1, keepdims=True), axis=0, keepdims=True)

        @pl.when(pl.program_id(0) == 0)
        def _():
            l_ref[...] = part

        @pl.when(pl.program_id(0) > 0)
        def _():
            l_ref[...] += part

    row = pl.BlockSpec((tm, d), lambda i: (i, 0))
    return _pcall(body, grid=(t // tm,), in_specs=[row, row], out_specs=[row, pl.BlockSpec((1, 1), lambda i: (0, 0))],
                  out_shape=[jax.ShapeDtypeStruct((t, d), F32), jax.ShapeDtypeStruct((1, 1), F32)],
                  compiler_params=_params(("arbitrary",)), name=name)(y, target)


def _ln_silu_fwd(c, g, b, name):
    t, d = c.shape
    tm = _pick(t, 512, 16)

    def body(c_ref, g_ref, b_ref, o_ref):
        cv = c_ref[...]
        xc = cv - jnp.mean(cv, axis=-1, keepdims=True)
        z = xc * lax.rsqrt(jnp.mean(xc * xc, axis=-1, keepdims=True) + EPS) * g_ref[...] + b_ref[...]
        o_ref[...] = (z * _sigmoid(z)).astype(BF16)

    row = pl.BlockSpec((tm, d), lambda i: (i, 0))
    vec = pl.BlockSpec((1, d), lambda i: (0, 0))
    return _pcall(body, carry_us=7.0 * t * d / HBM_BYTES_PER_US, grid=(t // tm,), in_specs=[row, vec, vec], out_specs=row,
                  out_shape=jax.ShapeDtypeStruct((t, d), BF16),
                  compiler_params=_params(("parallel",)), name=name)(c, g, b)


def _ln_silu_bwd(ds, c, g, b, name):
    t, d = c.shape
    tm = _pick(t, 256, 8)

    def body(ds_ref, c_ref, g_ref, b_ref, dc_ref, dg_ref, db_ref):
        cv = c_ref[...]
        xc = cv - jnp.mean(cv, axis=-1, keepdims=True)
        r = lax.rsqrt(jnp.mean(xc * xc, axis=-1, keepdims=True) + EPS)
        ch = xc * r
        z = ch * g_ref[...] + b_ref[...]
        sg = _sigmoid(z)
        dz = ds_ref[...] * (sg * (1.0 + z * (1.0 - sg)))
        dch = dz * g_ref[...]
        dc_ref[...] = r * (dch - jnp.mean(dch, axis=-1, keepdims=True) - ch * jnp.mean(dch * ch, axis=-1, keepdims=True))
        pg = jnp.sum(dz * ch, axis=0, keepdims=True)
        pb = jnp.sum(dz, axis=0, keepdims=True)

        @pl.when(pl.program_id(0) == 0)
        def _():
            dg_ref[...] = pg
            db_ref[...] = pb

        @pl.when(pl.program_id(0) > 0)
        def _():
            dg_ref[...] += pg
            db_ref[...] += pb

    row = pl.BlockSpec((tm, d), lambda i: (i, 0))
    vec = pl.BlockSpec((1, d), lambda i: (0, 0))
    return _pcall(body, grid=(t // tm,), in_specs=[row, row, vec, vec], out_specs=[row, vec, vec],
                  out_shape=[jax.ShapeDtypeStruct((t, d), F32), jax.ShapeDtypeStruct((1, d), F32), jax.ShapeDtypeStruct((1, d), F32)],
                  compiler_params=_params(("arbitrary",)), name=name)(ds, c, g, b)


def _steps(t):
    return t // ROWS


def _conf_conv_fwd(u, dw_w, dw_b, name):
    t, d2 = u.shape
    d = d2 // 2
    c = LANES
    ns = d // c
    kc = dw_w.shape[0]

    def body(a_ref, g_ref, w_ref, b_ref, o_ref, pad, win):
        pad[0:HALO, :] = jnp.zeros((HALO, c), F32)

        def glu(i, _):
            base = pl.multiple_of(i * ROWS, ROWS)
            pad[pl.ds(base + HALO, ROWS), :] = a_ref[pl.ds(base, ROWS), :] * _sigmoid(g_ref[pl.ds(base, ROWS), :])
            return 0
        lax.fori_loop(0, _steps(t), glu, 0)

        def conv(i, _):
            base = pl.multiple_of(i * ROWS, ROWS)
            e = _window(win, pad[pl.ds(base, ROWS + HALO), :])
            acc = jnp.zeros((ROWS, c), F32) + b_ref[...]
            for j in range(kc):
                acc = acc + w_ref[j:j + 1, :] * _rows(e, HALO - (kc - 1) + j, ROWS)
            o_ref[pl.ds(base, ROWS), :] = acc
            return 0
        lax.fori_loop(0, _steps(t), conv, 0)

    return _pcall(body, carry_us=CONV_FWD_US_PER_ELEM * t * d, grid=(ns,),
                  in_specs=[pl.BlockSpec((t, c), lambda s: (0, s)), pl.BlockSpec((t, c), lambda s: (0, s + ns)),
                            pl.BlockSpec((kc, c), lambda s: (0, s)), pl.BlockSpec((1, c), lambda s: (0, s))],
                  out_specs=pl.BlockSpec((t, c), lambda s: (0, s)), out_shape=jax.ShapeDtypeStruct((t, d), F32),
                  scratch_shapes=[pltpu.VMEM((t + HALO, c), F32), pltpu.VMEM((ROWS + HALO, c), F32)],
                  compiler_params=_params(("parallel",)), name=name)(u, u, dw_w, dw_b)


def _conf_conv_bwd(dc, u, dw_w, name):
    t, d = dc.shape
    c = LANES
    ns = d // c
    kc = dw_w.shape[0]

    def body(dc_ref, a_ref, g_ref, w_ref, du_ref, dww_ref, dwb_ref, db_ref, padv, padd, accw, accb, winv, wind):
        padv[0:HALO, :] = jnp.zeros((HALO, c), F32)
        padd[t:t + HALO, :] = jnp.zeros((HALO, c), F32)
        accw[...] = jnp.zeros_like(accw)
        accb[...] = jnp.zeros_like(accb)

        def fill(i, _):
            base = pl.multiple_of(i * ROWS, ROWS)
            padv[pl.ds(base + HALO, ROWS), :] = a_ref[pl.ds(base, ROWS), :] * _sigmoid(g_ref[pl.ds(base, ROWS), :])
            padd[pl.ds(base, ROWS), :] = dc_ref[pl.ds(base, ROWS), :]
            return 0
        lax.fori_loop(0, _steps(t), fill, 0)

        def step(i, _):
            base = pl.multiple_of(i * ROWS, ROWS)
            ev = _window(winv, padv[pl.ds(base, ROWS + HALO), :])
            ed = _window(wind, padd[pl.ds(base, ROWS + HALO), :])
            dcc = _rows(ed, 0, ROWS)
            dv = jnp.zeros((ROWS, c), F32)
            for j in range(kc):
                dv = dv + w_ref[j:j + 1, :] * _rows(ed, kc - 1 - j, ROWS)
                accw[j] = accw[j] + _fold8(dcc * _rows(ev, HALO - (kc - 1) + j, ROWS))
            accb[0] = accb[0] + _fold8(dcc)
            av = a_ref[pl.ds(base, ROWS), :]
            sg = _sigmoid(g_ref[pl.ds(base, ROWS), :])
            da = dv * sg
            dg = dv * av * sg * (1.0 - sg)
            du_ref[0, pl.ds(base, ROWS), :] = da.astype(BF16)
            du_ref[1, pl.ds(base, ROWS), :] = dg.astype(BF16)
            accb[1] = accb[1] + _fold8(da)
            accb[2] = accb[2] + _fold8(dg)
            return 0
        lax.fori_loop(0, _steps(t), step, 0)
        for j in range(kc):
            dww_ref[j:j + 1, :] = jnp.sum(accw[j], axis=0, keepdims=True)
        dwb_ref[...] = jnp.sum(accb[0], axis=0, keepdims=True)
        db_ref[0] = jnp.sum(accb[1], axis=0, keepdims=True)
        db_ref[1] = jnp.sum(accb[2], axis=0, keepdims=True)

    return _pcall(body, carry_us=CONV_BWD_US_PER_ELEM * t * d, grid=(ns,),
                  in_specs=[pl.BlockSpec((t, c), lambda s: (0, s)), pl.BlockSpec((t, c), lambda s: (0, s)),
                            pl.BlockSpec((t, c), lambda s: (0, s + ns)), pl.BlockSpec((kc, c), lambda s: (0, s))],
                  out_specs=[pl.BlockSpec((2, t, c), lambda s: (0, 0, s)), pl.BlockSpec((kc, c), lambda s: (0, s)),
                             pl.BlockSpec((1, c), lambda s: (0, s)), pl.BlockSpec((2, 1, c), lambda s: (0, 0, s))],
                  out_shape=[jax.ShapeDtypeStruct((2, t, d), BF16), jax.ShapeDtypeStruct((kc, d), F32),
                             jax.ShapeDtypeStruct((1, d), F32), jax.ShapeDtypeStruct((2, 1, d), F32)],
                  scratch_shapes=[pltpu.VMEM((t + HALO, c), F32), pltpu.VMEM((t + HALO, c), F32),
                                  pltpu.VMEM((kc, 8, c), F32), pltpu.VMEM((3, 8, c), F32),
                                  pltpu.VMEM((ROWS + HALO, c), F32), pltpu.VMEM((ROWS + HALO, c), F32)],
                  compiler_params=_params(("parallel",)), name=name)(dc, u, u, dw_w)


def _ffn_act_fwd(u0, dw_w, dw_b, name):
    t, f2 = u0.shape
    f = f2 // 2
    c = LANES
    ns = f // c
    kw = dw_w.shape[0]

    def body(g_ref, v_ref, wg_ref, wv_ref, bg_ref, bv_ref, o_ref, wing, winv):
        def step(i, _):
            base = pl.multiple_of(i * ROWS, ROWS)
            lo = pl.multiple_of(jnp.maximum(base - HALO, 0), HALO)
            keep = jnp.where(i > 0, 1.0, 0.0)
            eg = _window(wing, jnp.concatenate([g_ref[pl.ds(lo, HALO), :] * keep, g_ref[pl.ds(base, ROWS), :]], axis=0))
            ev = _window(winv, jnp.concatenate([v_ref[pl.ds(lo, HALO), :] * keep, v_ref[pl.ds(base, ROWS), :]], axis=0))
            gate = jnp.zeros((ROWS, c), F32) + bg_ref[...]
            val = jnp.zeros((ROWS, c), F32) + bv_ref[...]
            for j in range(kw):
                gate = gate + wg_ref[j:j + 1, :] * _rows(eg, HALO - (kw - 1) + j, ROWS)
                val = val + wv_ref[j:j + 1, :] * _rows(ev, HALO - (kw - 1) + j, ROWS)
            o_ref[pl.ds(base, ROWS), :] = (gate * _sigmoid(gate) * val).astype(BF16)
            return 0
        lax.fori_loop(0, _steps(t), step, 0)

    return _pcall(body, carry_us=ACT_FWD_US_PER_ELEM * t * f, grid=(ns,),
                  in_specs=[pl.BlockSpec((t, c), lambda s: (0, s)), pl.BlockSpec((t, c), lambda s: (0, s + ns)),
                            pl.BlockSpec((kw, c), lambda s: (0, s)), pl.BlockSpec((kw, c), lambda s: (0, s + ns)),
                            pl.BlockSpec((1, c), lambda s: (0, s)), pl.BlockSpec((1, c), lambda s: (0, s + ns))],
                  out_specs=pl.BlockSpec((t, c), lambda s: (0, s)), out_shape=jax.ShapeDtypeStruct((t, f), BF16),
                  scratch_shapes=[pltpu.VMEM((ROWS + HALO, c), F32), pltpu.VMEM((ROWS + HALO, c), F32)],
                  compiler_params=_params(("parallel",)), name=name)(u0, u0, dw_w, dw_w, dw_b, dw_b)


def _ffn_act_bwd(da, u0, dw_w, dw_b, name):
    t, f = da.shape
    c = LANES
    ns = f // c
    kw = dw_w.shape[0]

    def body(da_ref, g_ref, v_ref, wg_ref, wv_ref, bg_ref, bv_ref, du_ref, dww_ref, dwb_ref, padg, padv, accw, accb, wing, winv):
        padg[t:t + HALO, :] = jnp.zeros((HALO, c), F32)
        padv[t:t + HALO, :] = jnp.zeros((HALO, c), F32)
        accw[...] = jnp.zeros_like(accw)
        accb[...] = jnp.zeros_like(accb)

        def first(i, _):
            base = pl.multiple_of(i * ROWS, ROWS)
            lo = pl.multiple_of(jnp.maximum(base - HALO, 0), HALO)
            keep = jnp.where(i > 0, 1.0, 0.0)
            eg = _window(wing, jnp.concatenate([g_ref[pl.ds(lo, HALO), :] * keep, g_ref[pl.ds(base, ROWS), :]], axis=0))
            ev = _window(winv, jnp.concatenate([v_ref[pl.ds(lo, HALO), :] * keep, v_ref[pl.ds(base, ROWS), :]], axis=0))
            gate = jnp.zeros((ROWS, c), F32) + bg_ref[...]
            val = jnp.zeros((ROWS, c), F32) + bv_ref[...]
            for j in range(kw):
                gate = gate + wg_ref[j:j + 1, :] * _rows(eg, HALO - (kw - 1) + j, ROWS)
                val = val + wv_ref[j:j + 1, :] * _rows(ev, HALO - (kw - 1) + j, ROWS)
            dav = da_ref[pl.ds(base, ROWS), :]
            sg = _sigmoid(gate)
            dgate = dav * val * (sg * (1.0 + gate * (1.0 - sg)))
            dval = dav * (gate * sg)
            padg[pl.ds(base, ROWS), :] = dgate
            padv[pl.ds(base, ROWS), :] = dval
            for j in range(kw):
                accw[j] = accw[j] + _fold8(dgate * _rows(eg, HALO - (kw - 1) + j, ROWS))
                accw[kw + j] = accw[kw + j] + _fold8(dval * _rows(ev, HALO - (kw - 1) + j, ROWS))
            accb[0] = accb[0] + _fold8(dgate)
            accb[1] = accb[1] + _fold8(dval)
            return 0
        lax.fori_loop(0, _steps(t), first, 0)

        def second(i, _):
            base = pl.multiple_of(i * ROWS, ROWS)
            eg = _window(wing, padg[pl.ds(base, ROWS + HALO), :])
            ev = _window(winv, padv[pl.ds(base, ROWS + HALO), :])
            dg = jnp.zeros((ROWS, c), F32)
            dv = jnp.zeros((ROWS, c), F32)
            for j in range(kw):
                dg = dg + wg_ref[j:j + 1, :] * _rows(eg, kw - 1 - j, ROWS)
                dv = dv + wv_ref[j:j + 1, :] * _rows(ev, kw - 1 - j, ROWS)
            du_ref[0, pl.ds(base, ROWS), :] = dg.astype(BF16)
            du_ref[1, pl.ds(base, ROWS), :] = dv.astype(BF16)
            return 0
        lax.fori_loop(0, _steps(t), second, 0)
        for j in range(kw):
            dww_ref[0, j:j + 1, :] = jnp.sum(accw[j], axis=0, keepdims=True)
            dww_ref[1, j:j + 1, :] = jnp.sum(accw[kw + j], axis=0, keepdims=True)
        dwb_ref[0] = jnp.sum(accb[0], axis=0, keepdims=True)
        dwb_ref[1] = jnp.sum(accb[1], axis=0, keepdims=True)

    return _pcall(body, grid=(ns,),
                  in_specs=[pl.BlockSpec((t, c), lambda s: (0, s)),
                            pl.BlockSpec((t, c), lambda s: (0, s)), pl.BlockSpec((t, c), lambda s: (0, s + ns)),
                            pl.BlockSpec((kw, c), lambda s: (0, s)), pl.BlockSpec((kw, c), lambda s: (0, s + ns)),
                            pl.BlockSpec((1, c), lambda s: (0, s)), pl.BlockSpec((1, c), lambda s: (0, s + ns))],
                  out_specs=[pl.BlockSpec((2, t, c), lambda s: (0, 0, s)), pl.BlockSpec((2, kw, c), lambda s: (0, 0, s)),
                             pl.BlockSpec((2, 1, c), lambda s: (0, 0, s))],
                  out_shape=[jax.ShapeDtypeStruct((2, t, f), BF16), jax.ShapeDtypeStruct((2, kw, f), F32),
                             jax.ShapeDtypeStruct((2, 1, f), F32)],
                  scratch_shapes=[pltpu.VMEM((t + HALO, c), F32), pltpu.VMEM((t + HALO, c), F32),
                                  pltpu.VMEM((2 * kw, 8, c), F32), pltpu.VMEM((2, 8, c), F32),
                                  pltpu.VMEM((ROWS + HALO, c), F32), pltpu.VMEM((ROWS + HALO, c), F32)],
                  compiler_params=_params(("parallel",)), name=name)(da, u0, u0, dw_w, dw_w, dw_b, dw_b)


def _window_of(group):
    w = jnp.float32(POOL_WINDOWS[-1])
    for k in range(len(POOL_WINDOWS) - 2, -1, -1):
        w = jnp.where(group == k, jnp.float32(POOL_WINDOWS[k]), w)
    return w


def _select_level(group, levels):
    out = levels[-1]
    for k in range(len(levels) - 2, -1, -1):
        out = jnp.where(group == k, levels[k], out)
    return out


def _pool_fwd(h, name):
    t, d = h.shape
    c = LANES
    per = d // len(POOL_WINDOWS) // c

    def body(h_ref, o_ref):
        group = pl.program_id(0)
        wf = _window_of(group)

        def step(i, _):
            base = pl.multiple_of(i * ROWS, ROWS)
            lo = pl.multiple_of(jnp.maximum(base - HALO, 0), HALO)
            keep = jnp.where(i > 0, 1.0, 0.0)
            cur = h_ref[pl.ds(base, ROWS), :]
            e = jnp.concatenate([h_ref[pl.ds(lo, HALO), :] * keep, cur], axis=0)
            n = ROWS + HALO
            levels = []
            s = e
            for k in range(len(POOL_WINDOWS)):
                s = s + pltpu.roll(s, 1 << k, 0)
                levels.append(s[HALO:n])
            tpos = (base + lax.broadcasted_iota(jnp.int32, (ROWS, c), 0) + 1).astype(F32)
            pooled = _select_level(group, levels) / jnp.minimum(tpos, wf)
            o_ref[pl.ds(base, ROWS), :] = (pooled - cur).astype(BF16)
            return 0
        lax.fori_loop(0, _steps(t), step, 0)

    return _pcall(body, grid=(len(POOL_WINDOWS), per), in_specs=[pl.BlockSpec((t, c), lambda g, s: (0, g * per + s))],
                  out_specs=pl.BlockSpec((t, c), lambda g, s: (0, g * per + s)), out_shape=jax.ShapeDtypeStruct((t, d), BF16),
                  compiler_params=_params(("parallel", "parallel")), name=name)(h)


def _pool_bwd(dm, name):
    t, d = dm.shape
    c = LANES
    per = d // len(POOL_WINDOWS) // c

    def body(dm_ref, o_ref, pad):
        group = pl.program_id(0)
        wf = _window_of(group)
        pad[t:t + HALO, :] = jnp.zeros((HALO, c), F32)

        def fill(i, _):
            base = pl.multiple_of(i * ROWS, ROWS)
            tpos = (base + lax.broadcasted_iota(jnp.int32, (ROWS, c), 0) + 1).astype(F32)
            pad[pl.ds(base, ROWS), :] = dm_ref[pl.ds(base, ROWS), :] / jnp.minimum(tpos, wf)
            return 0
        lax.fori_loop(0, _steps(t), fill, 0)

        def step(i, _):
            base = pl.multiple_of(i * ROWS, ROWS)
            n = ROWS + HALO
            s = pad[pl.ds(base, n), :]
            levels = []
            for k in range(len(POOL_WINDOWS)):
                s = s + pltpu.roll(s, n - (1 << k), 0)
                levels.append(s[0:ROWS])
            o_ref[pl.ds(base, ROWS), :] = _select_level(group, levels) - dm_ref[pl.ds(base, ROWS), :]
            return 0
        lax.fori_loop(0, _steps(t), step, 0)

    return _pcall(body, grid=(len(POOL_WINDOWS), per), in_specs=[pl.BlockSpec((t, c), lambda g, s: (0, g * per + s))],
                  out_specs=pl.BlockSpec((t, c), lambda g, s: (0, g * per + s)), out_shape=jax.ShapeDtypeStruct((t, d), F32),
                  scratch_shapes=[pltpu.VMEM((t + HALO, c), F32)], compiler_params=_params(("parallel", "parallel")), name=name)(dm)


def _pool_mm_fwd(mixed, wg, scale, res, name):
    t, d = mixed.shape
    ng, gd, _ = wg.shape
    tm = _pick(t, 1024, 16)

    def body(a_ref, w_ref, s_ref, r_ref, o_ref):
        y = jnp.dot(a_ref[...], w_ref[...], preferred_element_type=F32)
        o_ref[...] = r_ref[...] + y * s_ref[...]

    blk = pl.BlockSpec((tm, gd), lambda g, i: (i, g))
    return _pcall(body, grid=(ng, t // tm),
                  in_specs=[blk, pl.BlockSpec((None, gd, gd), lambda g, i: (g, 0, 0)), pl.BlockSpec((1, gd), lambda g, i: (0, g)), blk],
                  out_specs=blk, out_shape=jax.ShapeDtypeStruct((t, d), F32),
                  compiler_params=_params(("parallel", "parallel")), name=name)(mixed, wg, scale, res)


def _pool_mm_bwd(dy, mixed, wg, scale, name):
    t, d = mixed.shape
    ng, gd, _ = wg.shape
    tm = _pick(t, 1024, 16)

    def body(dy_ref, a_ref, w_ref, s_ref, dm_ref, dw_ref, ds_ref):
        a, w, dyv = a_ref[...], w_ref[...], dy_ref[...]
        y = jnp.dot(a, w, preferred_element_type=F32)
        dyp = (dyv * s_ref[...]).astype(BF16)
        dm_ref[...] = lax.dot_general(dyp, w, (((1,), (1,)), ((), ())), preferred_element_type=F32)
        pw = lax.dot_general(a, dyp, (((0,), (0,)), ((), ())), preferred_element_type=F32)
        ps = jnp.sum(dyv * y, axis=0, keepdims=True)

        @pl.when(pl.program_id(1) == 0)
        def _():
            dw_ref[...] = pw
            ds_ref[...] = ps

        @pl.when(pl.program_id(1) > 0)
        def _():
            dw_ref[...] += pw
            ds_ref[...] += ps

    blk = pl.BlockSpec((tm, gd), lambda g, i: (i, g))
    wsp = pl.BlockSpec((None, gd, gd), lambda g, i: (g, 0, 0))
    vec = pl.BlockSpec((1, gd), lambda g, i: (0, g))
    return _pcall(body, grid=(ng, t // tm), in_specs=[blk, blk, wsp, vec], out_specs=[blk, wsp, vec],
                  out_shape=[jax.ShapeDtypeStruct((t, d), F32), jax.ShapeDtypeStruct((ng, gd, gd), F32), jax.ShapeDtypeStruct((1, d), F32)],
                  compiler_params=_params(("parallel", "arbitrary")), name=name)(dy, mixed, wg, scale)


def _rope_tables(positions):
    half = ROT_DIM // 2
    inv_freq = ROPE_THETA ** (-jnp.arange(0, ROT_DIM, 2, dtype=F32) / ROT_DIM)
    ang = positions.astype(F32)[:, None] * inv_freq
    cos, sin = jnp.cos(ang), jnp.sin(ang)
    t = positions.shape[0]
    ones = jnp.ones((t, HEAD - ROT_DIM), F32)
    zeros = jnp.zeros((t, HEAD - ROT_DIM), F32)
    zh = jnp.zeros((t, half), F32)
    c = jnp.concatenate([cos, cos, ones], axis=1)
    s1 = jnp.concatenate([-sin, zh, zeros], axis=1)
    s2 = jnp.concatenate([zh, sin, zeros], axis=1)
    return tuple(jnp.concatenate([a, a], axis=1) for a in (c, s1, s2))


def _half_mean(v, lo):
    s_lo = jnp.sum(jnp.where(lo, v, 0.0), axis=-1, keepdims=True)
    s_hi = jnp.sum(jnp.where(lo, 0.0, v), axis=-1, keepdims=True)
    return jnp.where(lo, s_lo, s_hi) * (1.0 / HEAD)


def _qk_prep_fwd(qkv, tabs, gq2, gk2, n_q, n_kv, name):
    t, width = qkv.shape
    tm = _pick(t, 256, 16)
    nqc, nkc = n_q * HEAD // LANES, n_kv * HEAD // LANES

    def body(x_ref, c_ref, s1_ref, s2_ref, gq_ref, gk_ref, q_ref, k2_ref, v2_ref):
        lo = lax.broadcasted_iota(jnp.int32, (tm, LANES), 1) < HEAD
        cv, s1, s2 = c_ref[...], s1_ref[...], s2_ref[...]

        def normrot(xc, g2):
            y = xc * lax.rsqrt(_half_mean(xc * xc, lo) + EPS) * g2
            return y * cv + pltpu.roll(y, LANES - ROT_DIM // 2, 1) * s1 + pltpu.roll(y, ROT_DIM // 2, 1) * s2

        def twice(y, j):
            sw = pltpu.roll(y, HEAD, 1)
            k2 = jnp.where(lo, y, sw) if j == 0 else jnp.where(lo, sw, y)
            return k2.astype(BF16)

        for ch in range(nqc):
            q_ref[:, ch * LANES:(ch + 1) * LANES] = normrot(x_ref[:, ch * LANES:(ch + 1) * LANES], gq_ref[...]).astype(BF16)
        for ch in range(nkc):
            off = (nqc + ch) * LANES
            y = normrot(x_ref[:, off:off + LANES], gk_ref[...])
            voff = (nqc + nkc + ch) * LANES
            vv = x_ref[:, voff:voff + LANES]
            for j in range(2):
                k2_ref[:, (2 * ch + j) * LANES:(2 * ch + j + 1) * LANES] = twice(y, j)
                v2_ref[:, (2 * ch + j) * LANES:(2 * ch + j + 1) * LANES] = twice(vv, j)

    row = lambda w: pl.BlockSpec((tm, w), lambda i: (i, 0))
    vec = pl.BlockSpec((1, LANES), lambda i: (0, 0))
    return _pcall(body, grid=(t // tm,), in_specs=[row(width), row(LANES), row(LANES), row(LANES), vec, vec],
                  out_specs=[row(n_q * HEAD), row(n_kv * LANES), row(n_kv * LANES)],
                  out_shape=[jax.ShapeDtypeStruct((t, n_q * HEAD), BF16), jax.ShapeDtypeStruct((t, n_kv * LANES), BF16),
                             jax.ShapeDtypeStruct((t, n_kv * LANES), BF16)],
                  compiler_params=_params(("parallel",)), name=name)(qkv, *tabs, gq2, gk2)


def _qk_prep_bwd(dq, dk_cur, dk_prev, dv_cur, dv_prev, qkv, tabs, gq2, gk2, n_q, n_kv, name):
    t, width = qkv.shape
    tm = Q_BLOCK
    nb = t // tm
    nqc, nkc = n_q * HEAD // LANES, n_kv * HEAD // LANES

    def body(dq_ref, kc_ref, kp_ref, vc_ref, vp_ref, x_ref, c_ref, s1_ref, s2_ref, gq_ref, gk_ref, o_ref, dgq_ref, dgk_ref):
        lo = lax.broadcasted_iota(jnp.int32, (tm, LANES), 1) < HEAD
        cv, s1, s2 = c_ref[...], s1_ref[...], s2_ref[...]
        more = jnp.where(pl.program_id(0) < nb - 1, 1.0, 0.0)

        def back(dy, xc, g2):
            dyn = dy * cv + pltpu.roll(dy * s1, ROT_DIM // 2, 1) + pltpu.roll(dy * s2, LANES - ROT_DIM // 2, 1)
            r = lax.rsqrt(_half_mean(xc * xc, lo) + EPS)
            xh = xc * r
            dxh = dyn * g2
            return r * (dxh - xh * _half_mean(dxh * xh, lo)), jnp.sum(dyn * xh, axis=0, keepdims=True)

        def unfold(cur_ref, prev_ref, ch):
            d0 = cur_ref[:, (2 * ch) * LANES:(2 * ch + 1) * LANES] + more * prev_ref[:, (2 * ch) * LANES:(2 * ch + 1) * LANES]
            d1 = cur_ref[:, (2 * ch + 1) * LANES:(2 * ch + 2) * LANES] + more * prev_ref[:, (2 * ch + 1) * LANES:(2 * ch + 2) * LANES]
            return jnp.where(lo, d0 + pltpu.roll(d0, HEAD, 1), d1 + pltpu.roll(d1, HEAD, 1))

        pq = jnp.zeros((1, LANES), F32)
        for ch in range(nqc):
            sl = slice(ch * LANES, (ch + 1) * LANES)
            dx, pg = back(dq_ref[:, sl], x_ref[:, sl], gq_ref[...])
            o_ref[:, sl] = dx.astype(BF16)
            pq = pq + pg
        pk = jnp.zeros((1, LANES), F32)
        for ch in range(nkc):
            sl = slice((nqc + ch) * LANES, (nqc + ch + 1) * LANES)
            dx, pg = back(unfold(kc_ref, kp_ref, ch), x_ref[:, sl], gk_ref[...])
            o_ref[:, sl] = dx.astype(BF16)
            pk = pk + pg
            vs = slice((nqc + nkc + ch) * LANES, (nqc + nkc + ch + 1) * LANES)
            o_ref[:, vs] = unfold(vc_ref, vp_ref, ch).astype(BF16)

        @pl.when(pl.program_id(0) == 0)
        def _():
            dgq_ref[...] = pq
            dgk_ref[...] = pk

        @pl.when(pl.program_id(0) > 0)
        def _():
            dgq_ref[...] += pq
            dgk_ref[...] += pk

    row = lambda w: pl.BlockSpec((tm, w), lambda i: (i, 0))
    nxt = lambda w: pl.BlockSpec((tm, w), lambda i: (jnp.minimum(i + 1, nb - 1), 0))
    vec = pl.BlockSpec((1, LANES), lambda i: (0, 0))
    kvw = n_kv * LANES
    return _pcall(body, grid=(nb,),
                  in_specs=[row(n_q * HEAD), row(kvw), nxt(kvw), row(kvw), nxt(kvw), row(width), row(LANES), row(LANES), row(LANES), vec, vec],
                  out_specs=[row(width), vec, vec],
                  out_shape=[jax.ShapeDtypeStruct((t, width), BF16), jax.ShapeDtypeStruct((1, LANES), F32), jax.ShapeDtypeStruct((1, LANES), F32)],
                  compiler_params=_params(("arbitrary",)), name=name)(dq, dk_cur, dk_prev, dv_cur, dv_prev, qkv, *tabs, gq2, gk2)


def _band_scores(qh, kc, kp, n, sink_row, lo_row, is_lo):
    scale = 1.0 / math.sqrt(HEAD)
    nt = (((1,), (1,)), ((), ()))
    s_c = lax.dot_general(qh, kc, nt, preferred_element_type=F32) * scale
    s_p = lax.dot_general(qh, kp, nt, preferred_element_type=F32) * scale
    qi = lax.broadcasted_iota(jnp.int32, (Q_BLOCK, Q_BLOCK), 0)
    kj = lax.broadcasted_iota(jnp.int32, (Q_BLOCK, Q_BLOCK), 1)
    s_c = jnp.where(kj <= qi, s_c, -jnp.inf)
    s_p = jnp.where((kj > qi) & (n > 0), s_p, -jnp.inf)
    pick = lo_row if is_lo else jnp.logical_not(lo_row)
    sink = jnp.max(jnp.where(pick, sink_row, -jnp.inf), axis=-1, keepdims=True)
    return s_c, s_p, sink


def _attn_fwd(q, k2, v2, sink_tab, name):
    t, dq = q.shape
    nc = dq // LANES
    nb = t // Q_BLOCK
    nkv = k2.shape[1] // LANES
    per_kv = nc // nkv

    def body(q_ref, kc_ref, kp_ref, vc_ref, vp_ref, s_ref, o_ref, lse_ref):
        n = pl.program_id(1)
        lo = lax.broadcasted_iota(jnp.int32, (Q_BLOCK, LANES), 1) < HEAD
        lo_row = lax.broadcasted_iota(jnp.int32, (1, LANES), 1) < HEAD
        kc, kp, vc, vp = kc_ref[...], kp_ref[...], vc_ref[...], vp_ref[...]
        for cc in range(per_kv):
            cols = slice(cc * LANES, (cc + 1) * LANES)
            qv = q_ref[:, cols].astype(F32)
            outs, lses = [], []
            for is_lo in (True, False):
                qh = jnp.where(lo, qv, 0.0) if is_lo else jnp.where(lo, 0.0, qv)
                s_c, s_p, sink = _band_scores(qh.astype(BF16), kc, kp, n, s_ref[8 * cc:8 * cc + 1, :], lo_row, is_lo)
                m = jnp.maximum(jnp.maximum(jnp.max(s_c, axis=-1, keepdims=True), jnp.max(s_p, axis=-1, keepdims=True)), sink)
                p_c, p_p = jnp.exp(s_c - m), jnp.exp(s_p - m)
                denom = jnp.sum(p_c, axis=-1, keepdims=True) + jnp.sum(p_p, axis=-1, keepdims=True) + jnp.exp(sink - m)
                pv = jnp.dot(p_c.astype(BF16), vc, preferred_element_type=F32) + jnp.dot(p_p.astype(BF16), vp, preferred_element_type=F32)
                outs.append(pv / denom)
                lses.append(m + jnp.log(denom))
            o_ref[:, cols] = jnp.where(lo, outs[0], outs[1]).astype(BF16)
            lse_ref[cc] = jnp.where(lo, lses[0], lses[1])

    qs = pl.BlockSpec((Q_BLOCK, per_kv * LANES), lambda k, n: (n, k))
    cur = pl.BlockSpec((Q_BLOCK, LANES), lambda k, n: (n, k))
    prev = pl.BlockSpec((Q_BLOCK, LANES), lambda k, n: (jnp.maximum(n - 1, 0), k))
    return _pcall(body, carry_us=ATTN_US_PER_STEP * nkv * nb, grid=(nkv, nb),
                  in_specs=[qs, cur, prev, cur, prev, pl.BlockSpec((8 * per_kv, LANES), lambda k, n: (k, 0))],
                  out_specs=[qs, pl.BlockSpec((per_kv, Q_BLOCK, LANES), lambda k, n: (k, n, 0))],
                  out_shape=[jax.ShapeDtypeStruct((t, dq), BF16), jax.ShapeDtypeStruct((nc, t, LANES), F32)],
                  compiler_params=_params(("parallel", "parallel")), name=name)(q, k2, k2, v2, v2, sink_tab)


def _attn_bwd(do, q, o, lse, k2, v2, sink_tab, name):
    t, dq = q.shape
    nc = dq // LANES
    nb = t // Q_BLOCK
    nkv = k2.shape[1] // LANES
    per_kv = nc // nkv
    scale = 1.0 / math.sqrt(HEAD)
    tn_ = (((0,), (0,)), ((), ()))
    nt = (((1,), (1,)), ((), ()))

    def body(do_ref, q_ref, o_ref, lse_ref, kc_ref, kp_ref, vc_ref, vp_ref, s_ref,
             dq_ref, dkc_ref, dkp_ref, dvc_ref, dvp_ref, dsk_ref):
        n = pl.program_id(1)
        lo = lax.broadcasted_iota(jnp.int32, (Q_BLOCK, LANES), 1) < HEAD
        lo_row = lax.broadcasted_iota(jnp.int32, (1, LANES), 1) < HEAD
        kc, kp, vc, vp = kc_ref[...], kp_ref[...], vc_ref[...], vp_ref[...]
        dkc = dkp = dvc = dvp = None
        for cc in range(per_kv):
            cols = slice(cc * LANES, (cc + 1) * LANES)
            qv, dov, ov, lsev = q_ref[:, cols].astype(F32), do_ref[:, cols], o_ref[:, cols].astype(F32), lse_ref[cc]
            dqs, dsinks = [], []
            for is_lo in (True, False):
                half = lo if is_lo else jnp.logical_not(lo)
                qh = jnp.where(half, qv, 0.0).astype(BF16)
                doh = jnp.where(half, dov, 0.0)
                s_c, s_p, sink = _band_scores(qh, kc, kp, n, s_ref[8 * cc:8 * cc + 1, :], lo_row, is_lo)
                lse_h = jnp.max(jnp.where(half, lsev, -jnp.inf), axis=-1, keepdims=True)
                p_c, p_p = jnp.exp(s_c - lse_h), jnp.exp(s_p - lse_h)
                delta = jnp.sum(doh * ov, axis=-1, keepdims=True)
                dob = doh.astype(BF16)
                ds_c = (p_c * (lax.dot_general(dob, vc, nt, preferred_element_type=F32) - delta)).astype(BF16)
                ds_p = (p_p * (lax.dot_general(dob, vp, nt, preferred_element_type=F32) - delta)).astype(BF16)
                dsinks.append(-jnp.sum(jnp.exp(sink - lse_h) * delta, axis=0, keepdims=True))
                dqs.append((jnp.dot(ds_c, kc, preferred_element_type=F32) + jnp.dot(ds_p, kp, preferred_element_type=F32)) * scale)
                parts = (lax.dot_general(ds_c, qh, tn_, preferred_element_type=F32) * scale,
                         lax.dot_general(ds_p, qh, tn_, preferred_element_type=F32) * scale,
                         lax.dot_general(p_c.astype(BF16), dob, tn_, preferred_element_type=F32),
                         lax.dot_general(p_p.astype(BF16), dob, tn_, preferred_element_type=F32))
                if dkc is None:
                    dkc, dkp, dvc, dvp = parts
                else:
                    dkc, dkp, dvc, dvp = dkc + parts[0], dkp + parts[1], dvc + parts[2], dvp + parts[3]
            dq_ref[:, cols] = jnp.where(lo, dqs[0], dqs[1])
            dsk_ref[8 * cc:8 * cc + 8, :] = jnp.zeros((8, LANES), F32) + jnp.where(lo_row, dsinks[0], dsinks[1])
        dkc_ref[...] = dkc
        dkp_ref[...] = dkp
        dvc_ref[...] = dvc
        dvp_ref[...] = dvp

    qs = pl.BlockSpec((Q_BLOCK, per_kv * LANES), lambda k, n: (n, k))
    cur = pl.BlockSpec((Q_BLOCK, LANES), lambda k, n: (n, k))
    prev = pl.BlockSpec((Q_BLOCK, LANES), lambda k, n: (jnp.maximum(n - 1, 0), k))
    kv_shape = jax.ShapeDtypeStruct((t, nkv * LANES), F32)
    return _pcall(body, carry_us=ATTN_US_PER_STEP * nkv * nb, grid=(nkv, nb),
                  in_specs=[qs, qs, qs, pl.BlockSpec((per_kv, Q_BLOCK, LANES), lambda k, n: (k, n, 0)),
                            cur, prev, cur, prev, pl.BlockSpec((8 * per_kv, LANES), lambda k, n: (k, 0))],
                  out_specs=[qs, cur, cur, cur, cur, pl.BlockSpec((None, None, 8 * per_kv, LANES), lambda k, n: (k, n, 0, 0))],
                  out_shape=[jax.ShapeDtypeStruct((t, dq), F32), kv_shape, kv_shape, kv_shape, kv_shape,
                             jax.ShapeDtypeStruct((nkv, nb, 8 * per_kv, LANES), F32)],
                  compiler_params=_params(("parallel", "parallel")), name=name)(do, q, o, lse, k2, k2, v2, v2, sink_tab)


def _peer(k):
    x, y, c = lax.axis_index("x"), lax.axis_index("y"), lax.axis_index("c")
    flip = lambda v, bit: 1 - v if bit else v
    return (flip(x, k & 4), flip(y, k & 2), flip(c, k & 1))


def _my_index():
    return 4 * lax.axis_index("x") + 2 * lax.axis_index("y") + lax.axis_index("c")


def _peer_index(k):
    px, py, pc = _peer(k)
    return 4 * px + 2 * py + pc


def _all_gather(shards, name):
    n = len(shards)
    any_spec = pl.BlockSpec(memory_space=pl.ANY)

    def body(*refs):
        ins, outs = refs[:n], refs[n:2 * n]
        send_sems, recv_sems, local_sems = refs[2 * n:]
        me = _my_index()
        local = [pltpu.make_async_copy(ins[a], outs[a].at[me], local_sems.at[a]) for a in range(n)]
        for cp in local:
            cp.start()
        sends = []
        for k in range(1, N_DEV):
            for a in range(n):
                cp = pltpu.make_async_remote_copy(src_ref=ins[a], dst_ref=outs[a].at[me], send_sem=send_sems.at[a, k - 1],
                                                  recv_sem=recv_sems.at[a, k - 1], device_id=_peer(k), device_id_type=MESH_ID)
                cp.start()
                sends.append(cp)
        for k in range(1, N_DEV):
            for a in range(n):
                pltpu.make_async_remote_copy(src_ref=ins[a], dst_ref=outs[a].at[_peer_index(k)], send_sem=send_sems.at[a, k - 1],
                                             recv_sem=recv_sems.at[a, k - 1], device_id=_peer(k), device_id_type=MESH_ID).wait_recv()
        for cp in sends:
            cp.wait_send()
        for cp in local:
            cp.wait()

    return _pcall(body, in_specs=[any_spec] * n, out_specs=[any_spec] * n,
                  out_shape=[jax.ShapeDtypeStruct((N_DEV,) + s.shape, s.dtype) for s in shards],
                  scratch_shapes=[pltpu.SemaphoreType.DMA((n, N_DEV - 1)), pltpu.SemaphoreType.DMA((n, N_DEV - 1)),
                                  pltpu.SemaphoreType.DMA((n,))],
                  name=name)(*shards)


GATHER1_PEERS = (1, 2, 4, 6)
GATHER2_PEERS = (2, 4, 6)
SCATTER_PEERS = tuple(range(1, N_DEV))
MAX_SEMS = N_DEV - 1
MAX_JOBS = 6
US_PER_MB = {"gather1": 5.4, "gather2": 0.6, "scatter": 10.8}
SCATTER_PIECE_US = 110.0
PIECE_US = {"gather1": 62.0, "gather2": 1e9, "scatter": SCATTER_PIECE_US}


class _Job:
    def __init__(self, key, kind, src, lo=0, hi=None, dst=None):
        self.key, self.kind, self.src, self.dst = key, kind, src, dst
        shape = src.shape if kind != "gather1" else (N_DEV,) + src.shape
        self.out_shape = jax.ShapeDtypeStruct(shape, src.dtype)
        self.rows = shape[1]
        self.lo, self.hi = lo, self.rows if hi is None else hi
        self.row_us = US_PER_MB[kind] * math.prod(shape) * src.dtype.itemsize / 1e6 / self.rows
        pieces = max(1, round(self.row_us * self.rows / PIECE_US[kind]))
        while pieces > 1 and self.rows % (16 * pieces):
            pieces -= 1
        self.piece = self.rows // pieces

    @property
    def cost_us(self):
        return self.row_us * (self.hi - self.lo)


class _Comm:
    def __init__(self):
        self.queue, self.gathered, self.scattered, self.layer = [], {}, [], 0

    def push(self, key, kind, src):
        self.queue.append(_Job(key, kind, src))

    def take(self, budget_us, upto=None):
        jobs = [j for j in self.queue if j.kind == "gather2"][:MAX_JOBS]
        used = sum(j.cost_us for j in jobs)
        if upto is not None and any(j.key == upto for j in jobs):
            self.queue = [j for j in self.queue if j not in jobs]
            return jobs
        for j in [j for j in self.queue if j.kind != "gather2"]:
            if len(jobs) >= MAX_JOBS:
                break
            urgent = j.kind == "gather1" and int(j.key[1]) <= self.layer
            piece_us = j.row_us * j.piece
            n = 0
            while j.lo + (n + 1) * j.piece <= j.hi and ((used < budget_us) if urgent else (used + 0.5 * piece_us <= budget_us)):
                n += 1
                used += piece_us
            if n == 0:
                break
            part = _Job(j.key, j.kind, j.src, j.lo, j.lo + n * j.piece, j.dst)
            part.parent = j
            j.lo = part.hi
            jobs.append(part)
            if j.lo < j.hi or j.key == upto:
                break
        self.queue = [j for j in self.queue if j not in jobs and j.lo < j.hi]
        return jobs

    def finish(self, job, result):
        if job.kind == "gather2":
            self.gathered[job.key] = result
        elif job.hi < job.rows:
            job.parent.dst = result
        elif job.kind == "gather1":
            self.queue.insert(0, _Job(job.key, "gather2", result))
        else:
            self.scattered.append((job.key, result))

    def need(self, key):
        while key not in self.gathered:
            assert any(j.key == key for j in self.queue), key
            self.flush(self.take(1e9, upto=key))
        return self.gathered[key]

    def flush(self, jobs):
        def body(o_ref):
            o_ref[...] = jnp.zeros_like(o_ref)
        _carry(body, jobs, self, dict(in_specs=[], out_specs=pl.BlockSpec(memory_space=pltpu.VMEM),
                                      out_shape=jax.ShapeDtypeStruct((8, LANES), F32), name="exchange"))()


def _job_copies(job, src, dst, send_sems, recv_sems, local_sem):
    me = _my_index()
    peers = {"gather1": GATHER1_PEERS, "gather2": GATHER2_PEERS, "scatter": SCATTER_PEERS}[job.kind]
    sends, recvs = [], []
    rows = pl.ds(job.lo, job.hi - job.lo)
    for i, k in enumerate(peers):
        if job.kind == "gather1":
            s_ref, d_ref, to, got = src.at[rows], dst.at[me, rows], _peer(k), dst.at[_peer_index(k), rows]
        elif job.kind == "gather2":
            s_ref, d_ref, to, got = src.at[_peer_index(k)], dst.at[_peer_index(k)], _peer(1), dst.at[_peer_index(k | 1)]
        else:
            s_ref, d_ref, to, got = src.at[_peer_index(k), rows], dst.at[me, rows], _peer(k), dst.at[_peer_index(k), rows]
        sends.append(pltpu.make_async_remote_copy(src_ref=s_ref, dst_ref=d_ref, send_sem=send_sems.at[i], recv_sem=recv_sems.at[i],
                                                  device_id=to, device_id_type=MESH_ID))
        recvs.append(pltpu.make_async_remote_copy(src_ref=s_ref, dst_ref=got, send_sem=send_sems.at[i], recv_sem=recv_sems.at[i],
                                                  device_id=to, device_id_type=MESH_ID))
    local = None
    if job.kind == "gather1":
        local = pltpu.make_async_copy(src.at[rows], dst.at[me, rows], local_sem)
    elif job.kind == "scatter":
        local = pltpu.make_async_copy(src.at[me, rows], dst.at[me, rows], local_sem)
    return sends, recvs, local


def _carry(body, jobs, comm, kw):
    kw = dict(kw)
    grid = tuple(kw.get("grid", ()))
    in_specs = list(kw["in_specs"])
    single = not isinstance(kw["out_specs"], (list, tuple))
    out_specs = [kw["out_specs"]] if single else list(kw["out_specs"])
    out_shape = [kw["out_shape"]] if single else list(kw["out_shape"])
    scratch = list(kw.get("scratch_shapes", []))
    n_in, n_out, n_scr, nj = len(in_specs), len(out_specs), len(scratch), len(jobs)
    any_spec = pl.BlockSpec(memory_space=pl.ANY)
    landed = [a for a, job in enumerate(jobs) if job.dst is not None]
    n_land = len(landed)

    def wrapped(*refs):
        pos = 0

        def take(k):
            nonlocal pos
            part = refs[pos:pos + k]
            pos += k
            return part
        ins, rin, _, outs, rout, scr = take(n_in), take(nj), take(n_land), take(n_out), take(nj), take(n_scr)
        send_sems, recv_sems, local_sems = take(3)

        def copies():
            return [_job_copies(job, rin[a], rout[a], send_sems.at[a], recv_sems.at[a], local_sems.at[a]) for a, job in enumerate(jobs)]

        def start():
            for sends, _, local in copies():
                if local is not None:
                    local.start()
                for cp in sends:
                    cp.start()

        def finish():
            for sends, recvs, local in copies():
                for cp in recvs:
                    cp.wait_recv()
                for cp in sends:
                    cp.wait_send()
                if local is not None:
                    local.wait()

        if grid:
            first = functools.reduce(jnp.logical_and, [pl.program_id(a) == 0 for a in range(len(grid))])
            last = functools.reduce(jnp.logical_and, [pl.program_id(a) == grid[a] - 1 for a in range(len(grid))])
            pl.when(first)(start)
            body(*ins, *outs, *scr)
            pl.when(last)(finish)
        else:
            start()
            body(*ins, *outs, *scr)
            finish()

    aliases = {n_in + a: n_out + a for a, job in enumerate(jobs) if job.kind == "gather2"}
    aliases.update({n_in + nj + i: n_out + a for i, a in enumerate(landed)})
    extra = dict(dimension_semantics=("arbitrary",) * len(grid)) if grid else {}
    call = _raw_call(wrapped, in_specs=in_specs + [any_spec] * (nj + n_land), out_specs=out_specs + [any_spec] * nj,
                     out_shape=out_shape + [job.out_shape for job in jobs],
                     scratch_shapes=scratch + [pltpu.SemaphoreType.DMA((nj, MAX_SEMS)), pltpu.SemaphoreType.DMA((nj, MAX_SEMS)),
                                               pltpu.SemaphoreType.DMA((nj,))],
                     input_output_aliases=aliases, compiler_params=_params(**extra), name=kw["name"],
                     **({"grid": grid} if grid else {}))

    def run(*args):
        res = call(*args, *[job.src for job in jobs], *[jobs[a].dst for a in landed])
        for job, r in zip(jobs, res[n_out:]):
            comm.finish(job, r)
        return res[0] if single else list(res[:n_out])
    return run


def _adam(g, w, m, v):
    m2 = ADAM_B1 * m + (1.0 - ADAM_B1) * g
    v2 = ADAM_B2 * v + (1.0 - ADAM_B2) * (g * g)
    m_hat = m2 / (1.0 - ADAM_B1 ** ADAM_STEP)
    v_hat = v2 / (1.0 - ADAM_B2 ** ADAM_STEP)
    delta = -ADAM_LR * (m_hat / (jnp.sqrt(v_hat) + ADAM_EPS) + ADAM_WD * w)
    return delta, m2, v2


def _sum_adam(parts, w, m, v, name):
    r, c = w.shape
    tr = _pick(r, max(8, (1 << 19) // c), 8)

    def body(p_ref, w_ref, m_ref, v_ref, g_ref, d_ref, m2_ref, v2_ref):
        g = p_ref[0].astype(F32)
        for j in range(1, N_DEV):
            g = g + p_ref[j].astype(F32)
        delta, m2, v2 = _adam(g, w_ref[...], m_ref[...], v_ref[...])
        g_ref[...] = g
        d_ref[...] = delta
        m2_ref[...] = m2
        v2_ref[...] = v2

    blk = pl.BlockSpec((tr, c), lambda i: (i, 0))
    shp = jax.ShapeDtypeStruct((r, c), F32)
    return _pcall(body, grid=(r // tr,),
                  in_specs=[pl.BlockSpec((N_DEV, tr, c), lambda i: (0, i, 0)), blk, blk, blk],
                  out_specs=[blk] * 4, out_shape=[shp] * 4, compiler_params=_params(("parallel",)), name=name)(parts, w, m, v)


def _small_layout(rep_shapes, sh_shapes):
    rows_r = [-(-s[1] // LANES) for s in rep_shapes]
    off_r = [sum(rows_r[:i]) for i in range(len(rows_r))]
    tot_r = -(-max(sum(rows_r), 8) // 8) * 8
    rows_s = [-(-s[-2] // 8) * 8 for s in sh_shapes]
    off_s = [sum(rows_s[:i]) for i in range(len(rows_s))]
    tot_s = max(sum(rows_s), 8)
    cmax = max([s[-1] for s in sh_shapes] + [LANES])
    return rows_r, off_r, tot_r, off_s, tot_s, cmax


def _small_exchange(rep_parts, sh_parts, name):
    nr, ns = len(rep_parts), len(sh_parts)
    rows_r, off_r, tot_r, off_s, tot_s, cmax = _small_layout([p.shape for p in rep_parts], [p.shape for p in sh_parts])
    vm = pl.BlockSpec(memory_space=pltpu.VMEM)

    def body(*refs):
        pos = 0

        def take(k):
            nonlocal pos
            out = refs[pos:pos + k]
            pos += k
            return out
        rp, sp = take(nr), take(ns)
        out_r, out_s = take(2)
        pack_r, got_r, pack_s, got_s, send_r, recv_r, send_s, recv_s = take(8)
        me = _my_index()
        pack_r[...] = jnp.zeros_like(pack_r)
        pack_s[...] = jnp.zeros_like(pack_s)
        for i in range(nr):
            nfull = rep_parts[i].shape[1]
            for rr in range(rows_r[i]):
                wdt = min(LANES, nfull - rr * LANES)
                pack_r[off_r[i] + rr:off_r[i] + rr + 1, 0:wdt] = rp[i][0:1, rr * LANES:rr * LANES + wdt]
        for i in range(ns):
            _, r_i, c_i = sh_parts[i].shape
            for j in range(N_DEV):
                pack_s[j, off_s[i]:off_s[i] + r_i, 0:c_i] = sp[i][j]
        got_r[me] = pack_r[...]
        got_s[me] = pack_s[me]
        sends = []
        for k in range(1, N_DEV):
            a = pltpu.make_async_remote_copy(src_ref=pack_r, dst_ref=got_r.at[me], send_sem=send_r.at[k - 1], recv_sem=recv_r.at[k - 1],
                                             device_id=_peer(k), device_id_type=MESH_ID)
            b = pltpu.make_async_remote_copy(src_ref=pack_s.at[_peer_index(k)], dst_ref=got_s.at[me], send_sem=send_s.at[k - 1],
                                             recv_sem=recv_s.at[k - 1], device_id=_peer(k), device_id_type=MESH_ID)
            a.start()
            b.start()
            sends += [a, b]
        for k in range(1, N_DEV):
            pltpu.make_async_remote_copy(src_ref=pack_r, dst_ref=got_r.at[_peer_index(k)], send_sem=send_r.at[k - 1],
                                         recv_sem=recv_r.at[k - 1], device_id=_peer(k), device_id_type=MESH_ID).wait_recv()
            pltpu.make_async_remote_copy(src_ref=pack_s.at[me], dst_ref=got_s.at[_peer_index(k)], send_sem=send_s.at[k - 1],
                                         recv_sem=recv_s.at[k - 1], device_id=_peer(k), device_id_type=MESH_ID).wait_recv()
        for cp in sends:
            cp.wait_send()
        tot_rep = got_r[0]
        tot_sh = got_s[0]
        for j in range(1, N_DEV):
            tot_rep = tot_rep + got_r[j]
            tot_sh = tot_sh + got_s[j]
        out_r[...] = tot_rep
        out_s[...] = tot_sh

    return _pcall(body, carry_us=SMALL_EXCHANGE_CARRY_US, in_specs=[vm] * (nr + ns), out_specs=[vm] * 2,
                  out_shape=[jax.ShapeDtypeStruct((tot_r, LANES), F32), jax.ShapeDtypeStruct((tot_s, cmax), F32)],
                  scratch_shapes=[pltpu.VMEM((tot_r, LANES), F32), pltpu.VMEM((N_DEV, tot_r, LANES), F32),
                                  pltpu.VMEM((N_DEV, tot_s, cmax), F32), pltpu.VMEM((N_DEV, tot_s, cmax), F32),
                                  pltpu.SemaphoreType.DMA((N_DEV - 1,)), pltpu.SemaphoreType.DMA((N_DEV - 1,)),
                                  pltpu.SemaphoreType.DMA((N_DEV - 1,)), pltpu.SemaphoreType.DMA((N_DEV - 1,))],
                  compiler_params=_params(), name=name)(*rep_parts, *sh_parts)


def _small_adam(tot_rep, tot_sh, rep_w, rep_m, rep_v, sh_w, sh_m, sh_v, name):
    nr, ns = len(rep_w), len(sh_w)
    rows_r, off_r, _, off_s, _, _ = _small_layout([w.shape for w in rep_w], [w.shape for w in sh_w])
    vm = pl.BlockSpec(memory_space=pltpu.VMEM)

    def body(*refs):
        pos = 0

        def take(k):
            nonlocal pos
            out = refs[pos:pos + k]
            pos += k
            return out
        (tr_ref, ts_ref), rw, rm, rv, sw, sm, sv = take(2), take(nr), take(nr), take(nr), take(ns), take(ns), take(ns)
        rg, rd, rm2, rv2 = take(nr), take(nr), take(nr), take(nr)
        sg, sd, sm2, sv2 = take(ns), take(ns), take(ns), take(ns)
        for i in range(nr):
            nfull = rep_w[i].shape[1]
            for rr in range(rows_r[i]):
                wdt = min(LANES, nfull - rr * LANES)
                rg[i][0:1, rr * LANES:rr * LANES + wdt] = tr_ref[off_r[i] + rr:off_r[i] + rr + 1, 0:wdt]
            delta, m2, v2 = _adam(rg[i][...], rw[i][...], rm[i][...], rv[i][...])
            rd[i][...] = delta
            rm2[i][...] = m2
            rv2[i][...] = v2
        for i in range(ns):
            r_i, c_i = sh_w[i].shape
            g = ts_ref[off_s[i]:off_s[i] + r_i, 0:c_i]
            delta, m2, v2 = _adam(g, sw[i][...], sm[i][...], sv[i][...])
            sg[i][...] = g
            sd[i][...] = delta
            sm2[i][...] = m2
            sv2[i][...] = v2

    shapes = [jax.ShapeDtypeStruct(w.shape, F32) for w in rep_w] * 4 + [jax.ShapeDtypeStruct(w.shape, F32) for w in sh_w] * 4
    outs = _pcall(body, in_specs=[vm] * (2 + 3 * nr + 3 * ns), out_specs=[vm] * len(shapes), out_shape=shapes,
                  compiler_params=_params(), name=name)(tot_rep, tot_sh, *rep_w, *rep_m, *rep_v, *sh_w, *sh_m, *sh_v)
    rep_out = [outs[i * nr:(i + 1) * nr] for i in range(4)]
    sh_out = [outs[4 * nr + i * ns:4 * nr + (i + 1) * ns] for i in range(4)]
    return rep_out, sh_out


_CONF = ("norm_g", "a_w_in", "a_b_in", "a_dw_w", "a_dw_b", "a_ln_g", "a_ln_b", "a_w_out", "a_b_out")
_FFN = ("ffn_norm_g", "ffn_w_up", "ffn_dw_w", "ffn_dw_b", "ffn_w_down")
_POOL = ("norm_g", "b_w_group", "b_scale")
_ATTN = ("norm_g", "c_w_qkv", "c_q_norm_g", "c_k_norm_g", "c_sinks", "c_w_o")
_LAYERS = (_CONF + _FFN, _POOL + _FFN, _ATTN + _FFN, _CONF + _FFN)
_NAMES = tuple("l%d_%s" % (i, n) for i, names in enumerate(_LAYERS) for n in names)
_BIG = ("a_w_in", "a_w_out", "ffn_w_up", "ffn_w_down", "b_w_group", "c_w_qkv", "c_w_o")
_SHARDED_SMALL = ("a_dw_w", "ffn_dw_w")


def _pad_rows(a, mult=8):
    r = a.shape[0]
    rp = -(-r // mult) * mult
    return a if rp == r else jnp.pad(a, ((0, rp - r), (0, 0)))


def _unstack_cols(st, rows):
    s, r, cs = st.shape
    return jnp.transpose(st, (1, 0, 2)).reshape(r, s * cs)[:rows]


def _stack_cols(a):
    r, c = a.shape
    return jnp.transpose(a.reshape(r, N_DEV, c // N_DEV), (1, 0, 2))


def _row(v):
    return v.reshape(1, -1)


def _ffn_forward(x_mid, p, tag):
    h2 = _rms_fwd(x_mid, _row(p["ffn_norm_g"]), BF16, "rms_fwd_bf16")
    u0 = _mm(h2, p["ffn_w_up"], "nn", name="ffn_up", tn=1408, tk=2048)
    a = _ffn_act_fwd(u0, p["ffn_dw_w"], _row(p["ffn_dw_b"]), "ffn_act_fwd")
    x_out = _mm(a, p["ffn_w_down"], "nn", res=x_mid, name="ffn_down", tk=2816)
    return x_out, dict(h2=h2, u0=u0, a=a)


def _ffn_backward(dx_out, x_mid, p, sv, grads):
    dwd = _mm(sv["a"], dx_out, "tn", out_dtype=BF16, name="ffn_down_dw", tm=1408, tk=2048)
    grads["ffn_w_down"] = dwd.reshape(N_DEV, dwd.shape[0] // N_DEV, dwd.shape[1])
    da = _mm(dx_out, p["ffn_w_down"], "nt", name="ffn_down_dx", tn=1408, tk=2048)
    du0, dww, dwb = _ffn_act_bwd(da, sv["u0"], p["ffn_dw_w"], _row(p["ffn_dw_b"]), "ffn_act_bwd")
    kw = dww.shape[1]
    grads["ffn_dw_w"] = _stack_cols(jnp.transpose(dww, (1, 0, 2)).reshape(kw, -1))
    grads["ffn_dw_b"] = dwb.reshape(1, -1)
    grads["ffn_w_up"] = _mm(sv["h2"], du0, "tn", out_dtype=BF16, out_stack=N_DEV, name="ffn_up_dw", tn=1408, tk=2048)
    dh2 = _mm(du0, p["ffn_w_up"], "nt", name="ffn_up_dx", tk=1408, panels=2)
    dx_mid, dg, _ = _rms_bwd(dh2, x_mid, _row(p["ffn_norm_g"]), dx_out, "rms_bwd")
    grads["ffn_norm_g"] = dg
    return dx_mid


def _conf_forward(x, p):
    h = _rms_fwd(x, _row(p["norm_g"]), BF16, "rms_fwd_bf16")
    u = _mm(h, p["a_w_in"], "nn", bias=_row(p["a_b_in"]), name="conf_in", tn=512, tk=2048)
    cpre = _conf_conv_fwd(u, p["a_dw_w"], _row(p["a_dw_b"]), "conf_conv_fwd")
    s = _ln_silu_fwd(cpre, _row(p["a_ln_g"]), _row(p["a_ln_b"]), "ln_silu_fwd")
    x_mid = _mm(s, p["a_w_out"], "nn", bias=_row(p["a_b_out"]), res=x, name="conf_out", tk=2048)
    return x_mid, dict(h=h, u=u, cpre=cpre, s=s)


def _conf_backward(dx_mid, x, p, sv, grads):
    dwo = _mm(sv["s"], dx_mid, "tn", out_dtype=BF16, name="conf_out_dw", tk=2048)
    grads["a_w_out"] = dwo.reshape(N_DEV, dwo.shape[0] // N_DEV, dwo.shape[1])
    ds = _mm(dx_mid, p["a_w_out"], "nt", name="conf_out_dx", tk=2048)
    dc, dlg, dlb = _ln_silu_bwd(ds, sv["cpre"], _row(p["a_ln_g"]), _row(p["a_ln_b"]), "ln_silu_bwd")
    grads["a_ln_g"], grads["a_ln_b"] = dlg, dlb
    du, dww, dwb, dbin = _conf_conv_bwd(dc, sv["u"], p["a_dw_w"], "conf_conv_bwd")
    grads["a_dw_w"] = _stack_cols(dww)
    grads["a_dw_b"] = dwb
    grads["a_b_in"] = dbin.reshape(1, -1)
    grads["a_w_in"] = _mm(sv["h"], du, "tn", out_dtype=BF16, out_stack=N_DEV, name="conf_in_dw", tn=512, tk=2048)
    dh = _mm(du, p["a_w_in"], "nt", name="conf_in_dx", tk=512, panels=4)
    dx, dg, dbo = _rms_bwd(dh, x, _row(p["norm_g"]), dx_mid, "rms_bwd")
    grads["norm_g"] = dg
    grads["a_b_out"] = dbo
    return dx


def _pool_forward(x, p):
    h = _rms_fwd(x, _row(p["norm_g"]), F32, "rms_fwd_f32")
    mixed = _pool_fwd(h, "pool_fwd")
    x_mid = _pool_mm_fwd(mixed, p["b_w_group"], _row(p["b_scale"]), x, "pool_mm_fwd")
    return x_mid, dict(mixed=mixed)


def _pool_backward(dx_mid, x, p, sv, grads):
    dmixed, dwg, dscale = _pool_mm_bwd(dx_mid, sv["mixed"], p["b_w_group"], _row(p["b_scale"]), "pool_mm_bwd")
    ng, gd, _ = dwg.shape
    grads["b_w_group"] = jnp.transpose(dwg.reshape(ng, N_DEV, gd // N_DEV, gd), (1, 0, 2, 3)).reshape(N_DEV, ng * gd // N_DEV, gd).astype(BF16)
    grads["b_scale"] = dscale
    dh = _pool_bwd(dmixed, "pool_bwd")
    dx, dg, _ = _rms_bwd(dh, x, _row(p["norm_g"]), dx_mid, "rms_bwd")
    grads["norm_g"] = dg
    return dx


def _attn_tables(p, positions, d_model):
    n_q = d_model // HEAD
    n_kv = n_q // 8
    tabs = _rope_tables(positions)
    gq2 = jnp.concatenate([p["c_q_norm_g"], p["c_q_norm_g"]]).reshape(1, LANES)
    gk2 = jnp.concatenate([p["c_k_norm_g"], p["c_k_norm_g"]]).reshape(1, LANES)
    sink_tab = jnp.repeat(jnp.repeat(p["c_sinks"].reshape(-1, 2), HEAD, axis=1), 8, axis=0)
    return n_q, n_kv, tabs, gq2, gk2, sink_tab


def _attn_forward(x, p, positions):
    n_q, n_kv, tabs, gq2, gk2, sink_tab = _attn_tables(p, positions, x.shape[1])
    h = _rms_fwd(x, _row(p["norm_g"]), BF16, "rms_fwd_bf16")
    qkv = _mm(h, p["c_w_qkv"], "nn", name="attn_qkv", tn=1280, tk=2048)
    q, k2, v2 = _qk_prep_fwd(qkv, tabs, gq2, gk2, n_q, n_kv, "qk_prep_fwd")
    o, lse = _attn_fwd(q, k2, v2, sink_tab, "attn_fwd")
    x_mid = _mm(o, p["c_w_o"], "nn", res=x, name="attn_out", tk=2048)
    return x_mid, dict(h=h, qkv=qkv, q=q, k2=k2, v2=v2, o=o, lse=lse)


def _attn_backward(dx_mid, x, p, positions, sv, grads):
    n_q, n_kv, tabs, gq2, gk2, sink_tab = _attn_tables(p, positions, x.shape[1])
    dwo = _mm(sv["o"], dx_mid, "tn", out_dtype=BF16, name="attn_out_dw", tk=2048)
    grads["c_w_o"] = dwo.reshape(N_DEV, dwo.shape[0] // N_DEV, dwo.shape[1])
    do = _mm(dx_mid, p["c_w_o"], "nt", name="attn_out_dx", tk=2048)
    dq, dkc, dkp, dvc, dvp, dsk = _attn_bwd(do, sv["q"], sv["o"], sv["lse"], sv["k2"], sv["v2"], sink_tab, "attn_bwd")
    nkv_, nb = dsk.shape[0], dsk.shape[1]
    dsk = dsk.reshape(nkv_, nb, -1, 8, LANES)[:, :, :, 0, :].sum(axis=1).reshape(-1, LANES)
    grads["c_sinks"] = jnp.stack([dsk[:, 0], dsk[:, HEAD]], axis=1).reshape(1, -1)
    dqkv, dgq, dgk = _qk_prep_bwd(dq, dkc, dkp, dvc, dvp, sv["qkv"], tabs, gq2, gk2, n_q, n_kv, "qk_prep_bwd")
    grads["c_q_norm_g"] = dgq[:, :HEAD] + dgq[:, HEAD:]
    grads["c_k_norm_g"] = dgk[:, :HEAD] + dgk[:, HEAD:]
    dwq = _mm(sv["h"], dqkv, "tn", out_dtype=BF16, name="attn_qkv_dw", tn=1280, tk=2048)
    grads["c_w_qkv"] = _stack_cols(dwq)
    dh = _mm(dqkv, p["c_w_qkv"], "nt", name="attn_qkv_dx", tk=1280)
    dx, dg, _ = _rms_bwd(dh, x, _row(p["norm_g"]), dx_mid, "rms_bwd")
    grads["norm_g"] = dg
    return dx


class _LayerWeights:
    def __init__(self, li, weights, small_full, comm):
        self.li, self.weights, self.small_full, self.comm, self.cache = li, weights, small_full, comm, {}

    def __getitem__(self, nme):
        if nme not in self.cache:
            self.cache[nme] = self.fetch(nme)
        return self.cache[nme]

    def fetch(self, nme):
        full = "l%d_%s" % (self.li, nme)
        w = self.weights[full]
        if nme in _SHARDED_SMALL:
            return _unstack_cols(self.small_full[full], w.shape[0])
        if nme not in _BIG:
            return w
        got = self.comm.need(full)
        if nme in ("a_w_in", "ffn_w_up"):
            return got
        if nme == "c_w_qkv":
            return _unstack_cols(got, w.shape[0])
        if nme == "b_w_group":
            ng, gs, gd = w.shape
            return jnp.transpose(got.reshape(N_DEV, ng, gs, gd), (1, 0, 2, 3)).reshape(ng, N_DEV * gs, gd)
        return got.reshape(-1, w.shape[1])


class _LayerGrads(dict):
    def __init__(self, li, comm):
        super().__init__()
        self.li, self.comm = li, comm

    def __setitem__(self, nme, value):
        if nme in _BIG:
            self.comm.push("l%d_%s" % (self.li, nme), "scatter", value)
        else:
            super().__setitem__(nme, value)


def kernel(*args):
    n_w = len(_NAMES)
    x, positions = args[0], args[1]
    weights = dict(zip(_NAMES, args[2:2 + n_w]))
    loss_target = args[2 + n_w]
    moms = dict(zip(_NAMES, args[3 + n_w:3 + 2 * n_w]))
    vels = dict(zip(_NAMES, args[3 + 2 * n_w:3 + 3 * n_w]))
    x0 = x[0]
    pos = positions[0]
    kinds = ("conf", "pool", "attn", "conf")
    comm = _Comm()
    _STATE["comm"], _STATE["last"] = comm, None
    shd = [n for n in _NAMES if n.split("_", 1)[1] in _SHARDED_SMALL]
    small_full = dict(zip(shd, _all_gather([_pad_rows(weights[n]) for n in shd], "gather_small")))
    for n in _NAMES:
        if n.split("_", 1)[1] in _BIG:
            w = weights[n]
            comm.push(n, "gather1", w.astype(BF16).reshape(-1, w.shape[-1]))
    results = {}

    def update_ready():
        while comm.scattered:
            full, parts = comm.scattered.pop(0)
            w = weights[full]
            w2 = w.reshape(-1, w.shape[-1])
            outs = _sum_adam(parts, w2, moms[full].reshape(w2.shape), vels[full].reshape(w2.shape), "adam_" + full.split("_", 1)[1])
            results[full] = tuple(o.reshape(w.shape) for o in outs)

    params, saved = [], []
    cur = x0
    for li, names in enumerate(_LAYERS):
        comm.layer = li
        p = _LayerWeights(li, weights, small_full, comm)
        if kinds[li] == "conf":
            x_mid, sv = _conf_forward(cur, p)
        elif kinds[li] == "pool":
            x_mid, sv = _pool_forward(cur, p)
        else:
            x_mid, sv = _attn_forward(cur, p, pos)
        x_out, sv_f = _ffn_forward(x_mid, p, kinds[li])
        params.append(p)
        saved.append((sv, sv_f, cur, x_mid))
        cur = x_out
    dy, loss_part = _loss_head(cur, loss_target[0], "loss_head")
    loss = lax.psum(loss_part[0, 0], ("x", "y", "c"))

    small_grads = {}
    dcur = dy
    for li in range(len(_LAYERS) - 1, -1, -1):
        p = params[li]
        sv, sv_f, x_in, x_mid = saved[li]
        grads = _LayerGrads(li, comm)
        dmid = _ffn_backward(dcur, x_mid, p, sv_f, grads)
        update_ready()
        if kinds[li] == "conf":
            dcur = _conf_backward(dmid, x_in, p, sv, grads)
        elif kinds[li] == "pool":
            dcur = _pool_backward(dmid, x_in, p, sv, grads)
        else:
            dcur = _attn_backward(dmid, x_in, p, pos, sv, grads)
        update_ready()
        for n in _LAYERS[li]:
            if n not in _BIG:
                small_grads["l%d_%s" % (li, n)] = grads[n]
    rep = [n for n in _NAMES if n.split("_", 1)[1] not in _BIG and n.split("_", 1)[1] not in _SHARDED_SMALL]
    tot_rep, tot_sh = _small_exchange([small_grads[n] for n in rep], [small_grads[n] for n in shd], "small_exchange")
    while comm.queue or comm.scattered:
        if not comm.scattered:
            comm.flush(comm.take(1e9))
        update_ready()
    _STATE["comm"] = None
    rep_out, sh_out = _small_adam(tot_rep, tot_sh, [_row(weights[n]) for n in rep], [_row(moms[n]) for n in rep], [_row(vels[n]) for n in rep],
                                  [weights[n] for n in shd], [moms[n] for n in shd], [vels[n] for n in shd], "small_adam")
    for i, n in enumerate(rep):
        results[n] = tuple(rep_out[k][i].reshape(weights[n].shape) for k in range(4))
    for i, n in enumerate(shd):
        results[n] = tuple(sh_out[k][i] for k in range(4))

    _STATE["last"] = None
    grad_x = dcur[None]
    out = [loss, grad_x]
    for k in range(4):
        out += [results[n][k] for n in _NAMES]
    return tuple(out)
```

```python
import functools
import math

import jax
import jax.numpy as jnp
from jax import lax
from jax.experimental import pallas as pl
from jax.experimental.pallas import tpu as pltpu

F32 = jnp.float32
BF16 = jnp.bfloat16
N_DEV = 8
EPS = 1e-6
LANES = 128
HEAD = 64
Q_BLOCK = 128
ROT_DIM = 16
ROPE_THETA = 500000.0
POOL_WINDOWS = (2, 4, 8, 16)
HALO = 32
ROWS = 128
VMEM_LIMIT = 56 * 1024 * 1024
ADAM_LR, ADAM_B1, ADAM_B2, ADAM_EPS, ADAM_WD, ADAM_STEP = 0.001, 0.9, 0.999, 1e-08, 0.01, 10
MESH_ID = pl.DeviceIdType.MESH
MXU_FLOPS_PER_US = 7.5e8
HBM_BYTES_PER_US = 2.5e6
ATTN_US_PER_STEP = 1.5
ATTN_BLOCKS_PER_STEP = 1
CONV_FWD_US_PER_ELEM = 1.1e-5
CONV_BWD_US_PER_ELEM = 2.3e-5
ACT_FWD_US_PER_ELEM = 4.8e-6
QK_PREP_BWD_US_PER_ELEM = 1.2e-5
SMALL_EXCHANGE_CARRY_US = 100.0


def _make_call(body, **kw):
    return pl.pallas_call(body, **kw)


_STATE = {"comm": None, "last": None}


def _raw_call(body, **kw):
    call = _make_call(body, **kw)

    def run(*args):
        last = _STATE["last"]
        if last is not None and args:
            first, _ = lax.optimization_barrier((args[0], last))
            args = (first,) + tuple(args[1:])
        res = call(*args)
        _STATE["last"] = res[0] if isinstance(res, (list, tuple)) else res
        return res
    return run


def _pcall(body, carry_us=0.0, **kw):
    comm = _STATE["comm"]
    jobs = comm.take(carry_us) if (comm is not None and carry_us > 0) else []
    if not jobs:
        return _raw_call(body, **kw)
    return _carry(body, jobs, comm, kw)


def _params(sem=None, **kw):
    if sem is not None:
        kw["dimension_semantics"] = sem
    return pltpu.CompilerParams(vmem_limit_bytes=VMEM_LIMIT, **kw)


def _pick(dim, pref, mult=LANES):
    best = None
    d = mult
    while d <= min(dim, pref):
        if dim % d == 0:
            best = d
        d += mult
    return dim if best is None else best


def _sigmoid(x):
    return 1.0 / (1.0 + jnp.exp(-x))


def _fold8(p):
    r, c = p.shape
    return p.reshape(r // 8, 8, c).sum(axis=0)


def _window(win_ref, e):
    win_ref[...] = e
    return win_ref


def _rows(win_ref, k, r):
    return win_ref[k:k + r, :]


def _lshape(a):
    return a.shape if a.ndim == 2 else (a.shape[1], a.shape[0] * a.shape[2])


def _panel(a):
    return a.shape[1] if a.ndim == 2 else a.shape[2]


def _lspec(a, br, bc, rc):
    if a.ndim == 2:
        return pl.BlockSpec((br, bc), rc)
    per = a.shape[2] // bc

    def idx(*g):
        r, c = rc(*g)
        return (c // per, r, c % per)
    return pl.BlockSpec((None, br, bc), idx)


def _mm(a, b, dims, *, name, out_dtype=F32, out_stack=None, bias=None, res=None, tm=1024, tn=1024, tk=1024, panels=1):
    (ar, ac), (br_, bc_) = _lshape(a), _lshape(b)
    if dims == "nn":
        m, k, n = ar, ac, bc_
        lim_m, lim_k, lim_n = m, min(_panel(a), k), _panel(b)
    elif dims == "nt":
        m, k, n = ar, ac, br_
        lim_m, lim_k, lim_n = m, math.gcd(_panel(a), _panel(b)), n
    else:
        m, k, n = ac, ar, bc_
        lim_m, lim_k, lim_n = _panel(a), k, _panel(b)
    if out_stack is not None:
        lim_n = math.gcd(lim_n, n // out_stack)
    sub = 16 if (out_dtype == BF16 or a.dtype == BF16) else 8
    tm = _pick(lim_m, tm, LANES if dims == "tn" else sub)
    tn = _pick(lim_n, tn)
    tk = _pick(lim_k, tk, sub if dims == "tn" else LANES)
    if panels > 1:
        assert dims == "nt" and b.ndim == 3 and tk == b.shape[2] and b.shape[0] % panels == 0 and _panel(a) % (panels * tk) == 0
    nk = k // (tk * panels)
    if dims == "tn":
        a_spec = _lspec(a, tk, tm, lambda i, j, kk: (kk, i))
    else:
        a_spec = _lspec(a, tm, tk * panels, lambda i, j, kk: (i, kk))
    if panels > 1:
        b_spec = pl.BlockSpec((panels, tn, tk), lambda i, j, kk: (kk, j, 0))
    elif dims == "nt":
        b_spec = _lspec(b, tn, tk, lambda i, j, kk: (j, kk))
    else:
        b_spec = _lspec(b, tk, tn, lambda i, j, kk: (kk, j))
    contract = {"nn": ((1,), (0,)), "nt": ((1,), (1,)), "tn": ((0,), (0,))}[dims]
    in_specs, args = [a_spec, b_spec], [a, b]
    if bias is not None:
        in_specs.append(pl.BlockSpec((1, tn), lambda i, j, kk: (0, j)))
        args.append(bias)
    if res is not None:
        in_specs.append(pl.BlockSpec((tm, tn), lambda i, j, kk: (i, j)))
        args.append(res)
    if out_stack is None:
        out_shape = jax.ShapeDtypeStruct((m, n), out_dtype)
    else:
        out_shape = jax.ShapeDtypeStruct((out_stack, m, n // out_stack), out_dtype)
    o_spec = _lspec(out_shape, tm, tn, lambda i, j, kk: (i, j))
    has_bias, has_res = bias is not None, res is not None

    def body(*refs):
        a_ref, b_ref = refs[0], refs[1]
        pos = 2
        bias_ref = res_ref = None
        if has_bias:
            bias_ref = refs[pos]
            pos += 1
        if has_res:
            res_ref = refs[pos]
            pos += 1
        o_ref = refs[pos]

        def part():
            if panels == 1:
                return lax.dot_general(a_ref[...].astype(BF16), b_ref[...].astype(BF16), (contract, ((), ())),
                                       preferred_element_type=F32)
            r = None
            for q in range(panels):
                t_ = lax.dot_general(a_ref[:, q * tk:(q + 1) * tk].astype(BF16), b_ref[q].astype(BF16), (contract, ((), ())),
                                     preferred_element_type=F32)
                r = t_ if r is None else r + t_
            return r

        def finish(r):
            if has_bias:
                r = r + bias_ref[...]
            if has_res:
                r = r + res_ref[...]
            o_ref[...] = r.astype(out_dtype)

        if nk == 1:
            finish(part())
        else:
            acc = refs[pos + 1]
            kk = pl.program_id(2)

            @pl.when(kk == 0)
            def _():
                acc[...] = part()

            @pl.when(kk > 0)
            def _():
                acc[...] += part()

            @pl.when(kk == nk - 1)
            def _():
                finish(acc[...])

    scratch = [] if nk == 1 else [pltpu.VMEM((tm, tn), F32)]
    return _pcall(body, carry_us=2.0 * m * n * k / MXU_FLOPS_PER_US, grid=(m // tm, n // tn, nk), in_specs=in_specs, out_specs=o_spec, out_shape=out_shape,
                  scratch_shapes=scratch, compiler_params=_params(("parallel", "parallel", "arbitrary")), name=name)(*args)


def _rms_fwd(x, g, out_dtype, name):
    t, d = x.shape
    tm = _pick(t, 512, 16)

    def body(x_ref, g_ref, o_ref):
        xv = x_ref[...]
        r = lax.rsqrt(jnp.mean(xv * xv, axis=-1, keepdims=True) + EPS)
        o_ref[...] = ((xv * r) * g_ref[...]).astype(out_dtype)

    return _pcall(body, carry_us=6.0 * t * d / HBM_BYTES_PER_US, grid=(t // tm,),
                  in_specs=[pl.BlockSpec((tm, d), lambda i: (i, 0)), pl.BlockSpec((1, d), lambda i: (0, 0))],
                  out_specs=pl.BlockSpec((tm, d), lambda i: (i, 0)), out_shape=jax.ShapeDtypeStruct((t, d), out_dtype),
                  compiler_params=_params(("parallel",)), name=name)(x, g)


def _rms_bwd(dh, x, g, dres, name):
    t, d = x.shape
    tm = _pick(t, 256, 8)

    def body(dh_ref, x_ref, g_ref, dres_ref, dx_ref, dg_ref, cs_ref):
        xv, dhv, dr = x_ref[...], dh_ref[...], dres_ref[...]
        r = lax.rsqrt(jnp.mean(xv * xv, axis=-1, keepdims=True) + EPS)
        xh = xv * r
        dxh = dhv * g_ref[...]
        dx_ref[...] = dr + r * (dxh - xh * jnp.mean(dxh * xh, axis=-1, keepdims=True))
        pg = jnp.sum(dhv * xh, axis=0, keepdims=True)
        pc = jnp.sum(dr, axis=0, keepdims=True)

        @pl.when(pl.program_id(0) == 0)
        def _():
            dg_ref[...] = pg
            cs_ref[...] = pc

        @pl.when(pl.program_id(0) > 0)
        def _():
            dg_ref[...] += pg
            cs_ref[...] += pc

    row = pl.BlockSpec((tm, d), lambda i: (i, 0))
    vec = pl.BlockSpec((1, d), lambda i: (0, 0))
    return _pcall(body, carry_us=16.0 * t * d / HBM_BYTES_PER_US, grid=(t // tm,), in_specs=[row, row, vec, row], out_specs=[row, vec, vec],
                  out_shape=[jax.ShapeDtypeStruct((t, d), F32), jax.ShapeDtypeStruct((1, d), F32), jax.ShapeDtypeStruct((1, d), F32)],
                  compiler_params=_params(("arbitrary",)), name=name)(dh, x, g, dres)


def _loss_head(y, target, name):
    t, d = y.shape
    tm = _pick(t, 512, 8)

    def body(y_ref, t_ref, dy_ref, l_ref):
        e = y_ref[...] - t_ref[...]
        dy_ref[...] = e * (1.0 / d)
        part = 0.5 * jnp.sum(jnp.mean(e * e, axis=-1, keepdims=True), axis=0, keepdims=True)

        @pl.when(pl.program_id(0) == 0)
        def _():
            l_ref[...] = part

        @pl.when(pl.program_id(0) > 0)
        def _():
            l_ref[...] += part

    row = pl.BlockSpec((tm, d), lambda i: (i, 0))
    return _pcall(body, grid=(t // tm,), in_specs=[row, row], out_specs=[row, pl.BlockSpec((1, 1), lambda i: (0, 0))],
                  out_shape=[jax.ShapeDtypeStruct((t, d), F32), jax.ShapeDtypeStruct((1, 1), F32)],
                  compiler_params=_params(("arbitrary",)), name=name)(y, target)


def _ln_silu_fwd(c, g, b, name):
    t, d = c.shape
    tm = _pick(t, 512, 16)

    def body(c_ref, g_ref, b_ref, o_ref):
        cv = c_ref[...]
        xc = cv - jnp.mean(cv, axis=-1, keepdims=True)
        z = xc * lax.rsqrt(jnp.mean(xc * xc, axis=-1, keepdims=True) + EPS) * g_ref[...] + b_ref[...]
        o_ref[...] = (z * _sigmoid(z)).astype(BF16)

    row = pl.BlockSpec((tm, d), lambda i: (i, 0))
    vec = pl.BlockSpec((1, d), lambda i: (0, 0))
    return _pcall(body, carry_us=7.0 * t * d / HBM_BYTES_PER_US, grid=(t // tm,), in_specs=[row, vec, vec], out_specs=row,
                  out_shape=jax.ShapeDtypeStruct((t, d), BF16),
                  compiler_params=_params(("parallel",)), name=name)(c, g, b)


def _ln_silu_bwd(ds, c, g, b, name):
    t, d = c.shape
    tm = _pick(t, 256, 8)

    def body(ds_ref, c_ref, g_ref, b_ref, dc_ref, dg_ref, db_ref):
        cv = c_ref[...]
        xc = cv - jnp.mean(cv, axis=-1, keepdims=True)
        r = lax.rsqrt(jnp.mean(xc * xc, axis=-1, keepdims=True) + EPS)
        ch = xc * r
        z = ch * g_ref[...] + b_ref[...]
        sg = _sigmoid(z)
        dz = ds_ref[...] * (sg * (1.0 + z * (1.0 - sg)))
        dch = dz * g_ref[...]
        dc_ref[...] = r * (dch - jnp.mean(dch, axis=-1, keepdims=True) - ch * jnp.mean(dch * ch, axis=-1, keepdims=True))
        pg = jnp.sum(dz * ch, axis=0, keepdims=True)
        pb = jnp.sum(dz, axis=0, keepdims=True)

        @pl.when(pl.program_id(0) == 0)
        def _():
            dg_ref[...] = pg
            db_ref[...] = pb

        @pl.when(pl.program_id(0) > 0)
        def _():
            dg_ref[...] += pg
            db_ref[...] += pb

    row = pl.BlockSpec((tm, d), lambda i: (i, 0))
    vec = pl.BlockSpec((1, d), lambda i: (0, 0))
    return _pcall(body, grid=(t // tm,), in_specs=[row, row, vec, vec], out_specs=[row, vec, vec],
                  out_shape=[jax.ShapeDtypeStruct((t, d), F32), jax.ShapeDtypeStruct((1, d), F32), jax.ShapeDtypeStruct((1, d), F32)],
                  compiler_params=_params(("arbitrary",)), name=name)(ds, c, g, b)


def _steps(t):
    return t // ROWS


def _conf_conv_fwd(u, dw_w, dw_b, name):
    t, d2 = u.shape
    d = d2 // 2
    c = LANES
    ns = d // c
    kc = dw_w.shape[0]

    def body(a_ref, g_ref, w_ref, b_ref, o_ref, pad, win):
        pad[0:HALO, :] = jnp.zeros((HALO, c), F32)

        def glu(i, _):
            base = pl.multiple_of(i * ROWS, ROWS)
            pad[pl.ds(base + HALO, ROWS), :] = a_ref[pl.ds(base, ROWS), :] * _sigmoid(g_ref[pl.ds(base, ROWS), :])
            return 0
        lax.fori_loop(0, _steps(t), glu, 0)

        def conv(i, _):
            base = pl.multiple_of(i * ROWS, ROWS)
            e = _window(win, pad[pl.ds(base, ROWS + HALO), :])
            acc = jnp.zeros((ROWS, c), F32) + b_ref[...]
            for j in range(kc):
                acc = acc + w_ref[j:j + 1, :] * _rows(e, HALO - (kc - 1) + j, ROWS)
            o_ref[pl.ds(base, ROWS), :] = acc
            return 0
        lax.fori_loop(0, _steps(t), conv, 0)

    return _pcall(body, carry_us=CONV_FWD_US_PER_ELEM * t * d, grid=(ns,),
                  in_specs=[pl.BlockSpec((t, c), lambda s: (0, s)), pl.BlockSpec((t, c), lambda s: (0, s + ns)),
                            pl.BlockSpec((kc, c), lambda s: (0, s)), pl.BlockSpec((1, c), lambda s: (0, s))],
                  out_specs=pl.BlockSpec((t, c), lambda s: (0, s)), out_shape=jax.ShapeDtypeStruct((t, d), F32),
                  scratch_shapes=[pltpu.VMEM((t + HALO, c), F32), pltpu.VMEM((ROWS + HALO, c), F32)],
                  compiler_params=_params(("parallel",)), name=name)(u, u, dw_w, dw_b)


def _conf_conv_bwd(dc, u, dw_w, name):
    t, d = dc.shape
    c = LANES
    ns = d // c
    kc = dw_w.shape[0]

    def body(dc_ref, a_ref, g_ref, w_ref, du_ref, dww_ref, dwb_ref, db_ref, padv, padd, accw, accb, winv, wind):
        padv[0:HALO, :] = jnp.zeros((HALO, c), F32)
        padd[t:t + HALO, :] = jnp.zeros((HALO, c), F32)
        accw[...] = jnp.zeros_like(accw)
        accb[...] = jnp.zeros_like(accb)

        def fill(i, _):
            base = pl.multiple_of(i * ROWS, ROWS)
            padv[pl.ds(base + HALO, ROWS), :] = a_ref[pl.ds(base, ROWS), :] * _sigmoid(g_ref[pl.ds(base, ROWS), :])
            padd[pl.ds(base, ROWS), :] = dc_ref[pl.ds(base, ROWS), :]
            return 0
        lax.fori_loop(0, _steps(t), fill, 0)

        def step(i, _):
            base = pl.multiple_of(i * ROWS, ROWS)
            ev = _window(winv, padv[pl.ds(base, ROWS + HALO), :])
            ed = _window(wind, padd[pl.ds(base, ROWS + HALO), :])
            dcc = _rows(ed, 0, ROWS)
            dv = jnp.zeros((ROWS, c), F32)
            for j in range(kc):
                dv = dv + w_ref[j:j + 1, :] * _rows(ed, kc - 1 - j, ROWS)
                accw[j] = accw[j] + _fold8(dcc * _rows(ev, HALO - (kc - 1) + j, ROWS))
            accb[0] = accb[0] + _fold8(dcc)
            av = a_ref[pl.ds(base, ROWS), :]
            sg = _sigmoid(g_ref[pl.ds(base, ROWS), :])
            da = dv * sg
            dg = dv * av * sg * (1.0 - sg)
            du_ref[0, pl.ds(base, ROWS), :] = da.astype(BF16)
            du_ref[1, pl.ds(base, ROWS), :] = dg.astype(BF16)
            accb[1] = accb[1] + _fold8(da)
            accb[2] = accb[2] + _fold8(dg)
            return 0
        lax.fori_loop(0, _steps(t), step, 0)
        for j in range(kc):
            dww_ref[j:j + 1, :] = jnp.sum(accw[j], axis=0, keepdims=True)
        dwb_ref[...] = jnp.sum(accb[0], axis=0, keepdims=True)
        db_ref[0] = jnp.sum(accb[1], axis=0, keepdims=True)
        db_ref[1] = jnp.sum(accb[2], axis=0, keepdims=True)

    return _pcall(body, carry_us=CONV_BWD_US_PER_ELEM * t * d, grid=(ns,),
                  in_specs=[pl.BlockSpec((t, c), lambda s: (0, s)), pl.BlockSpec((t, c), lambda s: (0, s)),
                            pl.BlockSpec((t, c), lambda s: (0, s + ns)), pl.BlockSpec((kc, c), lambda s: (0, s))],
                  out_specs=[pl.BlockSpec((2, t, c), lambda s: (0, 0, s)), pl.BlockSpec((kc, c), lambda s: (0, s)),
                             pl.BlockSpec((1, c), lambda s: (0, s)), pl.BlockSpec((2, 1, c), lambda s: (0, 0, s))],
                  out_shape=[jax.ShapeDtypeStruct((2, t, d), BF16), jax.ShapeDtypeStruct((kc, d), F32),
                             jax.ShapeDtypeStruct((1, d), F32), jax.ShapeDtypeStruct((2, 1, d), F32)],
                  scratch_shapes=[pltpu.VMEM((t + HALO, c), F32), pltpu.VMEM((t + HALO, c), F32),
                                  pltpu.VMEM((kc, 8, c), F32), pltpu.VMEM((3, 8, c), F32),
                                  pltpu.VMEM((ROWS + HALO, c), F32), pltpu.VMEM((ROWS + HALO, c), F32)],
                  compiler_params=_params(("parallel",)), name=name)(dc, u, u, dw_w)


def _ffn_act_fwd(u0, dw_w, dw_b, name):
    t, f2 = u0.shape
    f = f2 // 2
    c = LANES
    ns = f // c
    kw = dw_w.shape[0]

    def body(g_ref, v_ref, wg_ref, wv_ref, bg_ref, bv_ref, o_ref, wing, winv):
        def step(i, _):
            base = pl.multiple_of(i * ROWS, ROWS)
            lo = pl.multiple_of(jnp.maximum(base - HALO, 0), HALO)
            keep = jnp.where(i > 0, 1.0, 0.0)
            eg = _window(wing, jnp.concatenate([g_ref[pl.ds(lo, HALO), :] * keep, g_ref[pl.ds(base, ROWS), :]], axis=0))
            ev = _window(winv, jnp.concatenate([v_ref[pl.ds(lo, HALO), :] * keep, v_ref[pl.ds(base, ROWS), :]], axis=0))
            gate = jnp.zeros((ROWS, c), F32) + bg_ref[...]
            val = jnp.zeros((ROWS, c), F32) + bv_ref[...]
            for j in range(kw):
                gate = gate + wg_ref[j:j + 1, :] * _rows(eg, HALO - (kw - 1) + j, ROWS)
                val = val + wv_ref[j:j + 1, :] * _rows(ev, HALO - (kw - 1) + j, ROWS)
            o_ref[pl.ds(base, ROWS), :] = (gate * _sigmoid(gate) * val).astype(BF16)
            return 0
        lax.fori_loop(0, _steps(t), step, 0)

    return _pcall(body, carry_us=ACT_FWD_US_PER_ELEM * t * f, grid=(ns,),
                  in_specs=[pl.BlockSpec((t, c), lambda s: (0, s)), pl.BlockSpec((t, c), lambda s: (0, s + ns)),
                            pl.BlockSpec((kw, c), lambda s: (0, s)), pl.BlockSpec((kw, c), lambda s: (0, s + ns)),
                            pl.BlockSpec((1, c), lambda s: (0, s)), pl.BlockSpec((1, c), lambda s: (0, s + ns))],
                  out_specs=pl.BlockSpec((t, c), lambda s: (0, s)), out_shape=jax.ShapeDtypeStruct((t, f), BF16),
                  scratch_shapes=[pltpu.VMEM((ROWS + HALO, c), F32), pltpu.VMEM((ROWS + HALO, c), F32)],
                  compiler_params=_params(("parallel",)), name=name)(u0, u0, dw_w, dw_w, dw_b, dw_b)


def _ffn_act_bwd(da, u0, dw_w, dw_b, name):
    t, f = da.shape
    c = LANES
    ns = f // c
    kw = dw_w.shape[0]

    def body(da_ref, g_ref, v_ref, wg_ref, wv_ref, bg_ref, bv_ref, du_ref, dww_ref, dwb_ref, padg, padv, accw, accb, wing, winv):
        padg[t:t + HALO, :] = jnp.zeros((HALO, c), F32)
        padv[t:t + HALO, :] = jnp.zeros((HALO, c), F32)
        accw[...] = jnp.zeros_like(accw)
        accb[...] = jnp.zeros_like(accb)

        def first(i, _):
            base = pl.multiple_of(i * ROWS, ROWS)
            lo = pl.multiple_of(jnp.maximum(base - HALO, 0), HALO)
            keep = jnp.where(i > 0, 1.0, 0.0)
            eg = _window(wing, jnp.concatenate([g_ref[pl.ds(lo, HALO), :] * keep, g_ref[pl.ds(base, ROWS), :]], axis=0))
            ev = _window(winv, jnp.concatenate([v_ref[pl.ds(lo, HALO), :] * keep, v_ref[pl.ds(base, ROWS), :]], axis=0))
            gate = jnp.zeros((ROWS, c), F32) + bg_ref[...]
            val = jnp.zeros((ROWS, c), F32) + bv_ref[...]
            for j in range(kw):
                gate = gate + wg_ref[j:j + 1, :] * _rows(eg, HALO - (kw - 1) + j, ROWS)
                val = val + wv_ref[j:j + 1, :] * _rows(ev, HALO - (kw - 1) + j, ROWS)
            dav = da_ref[pl.ds(base, ROWS), :]
            sg = _sigmoid(gate)
            dgate = dav * val * (sg * (1.0 + gate * (1.0 - sg)))
            dval = dav * (gate * sg)
            padg[pl.ds(base, ROWS), :] = dgate
            padv[pl.ds(base, ROWS), :] = dval
            for j in range(kw):
                accw[j] = accw[j] + _fold8(dgate * _rows(eg, HALO - (kw - 1) + j, ROWS))
                accw[kw + j] = accw[kw + j] + _fold8(dval * _rows(ev, HALO - (kw - 1) + j, ROWS))
            accb[0] = accb[0] + _fold8(dgate)
            accb[1] = accb[1] + _fold8(dval)
            return 0
        lax.fori_loop(0, _steps(t), first, 0)

        def second(i, _):
            base = pl.multiple_of(i * ROWS, ROWS)
            eg = _window(wing, padg[pl.ds(base, ROWS + HALO), :])
            ev = _window(winv, padv[pl.ds(base, ROWS + HALO), :])
            dg = jnp.zeros((ROWS, c), F32)
            dv = jnp.zeros((ROWS, c), F32)
            for j in range(kw):
                dg = dg + wg_ref[j:j + 1, :] * _rows(eg, kw - 1 - j, ROWS)
                dv = dv + wv_ref[j:j + 1, :] * _rows(ev, kw - 1 - j, ROWS)
            du_ref[0, pl.ds(base, ROWS), :] = dg.astype(BF16)
            du_ref[1, pl.ds(base, ROWS), :] = dv.astype(BF16)
            return 0
        lax.fori_loop(0, _steps(t), second, 0)
        for j in range(kw):
            dww_ref[0, j:j + 1, :] = jnp.sum(accw[j], axis=0, keepdims=True)
            dww_ref[1, j:j + 1, :] = jnp.sum(accw[kw + j], axis=0, keepdims=True)
        dwb_ref[0] = jnp.sum(accb[0], axis=0, keepdims=True)
        dwb_ref[1] = jnp.sum(accb[1], axis=0, keepdims=True)

    return _pcall(body, grid=(ns,),
                  in_specs=[pl.BlockSpec((t, c), lambda s: (0, s)),
                            pl.BlockSpec((t, c), lambda s: (0, s)), pl.BlockSpec((t, c), lambda s: (0, s + ns)),
                            pl.BlockSpec((kw, c), lambda s: (0, s)), pl.BlockSpec((kw, c), lambda s: (0, s + ns)),
                            pl.BlockSpec((1, c), lambda s: (0, s)), pl.BlockSpec((1, c), lambda s: (0, s + ns))],
                  out_specs=[pl.BlockSpec((2, t, c), lambda s: (0, 0, s)), pl.BlockSpec((2, kw, c), lambda s: (0, 0, s)),
                             pl.BlockSpec((2, 1, c), lambda s: (0, 0, s))],
                  out_shape=[jax.ShapeDtypeStruct((2, t, f), BF16), jax.ShapeDtypeStruct((2, kw, f), F32),
                             jax.ShapeDtypeStruct((2, 1, f), F32)],
                  scratch_shapes=[pltpu.VMEM((t + HALO, c), F32), pltpu.VMEM((t + HALO, c), F32),
                                  pltpu.VMEM((2 * kw, 8, c), F32), pltpu.VMEM((2, 8, c), F32),
                                  pltpu.VMEM((ROWS + HALO, c), F32), pltpu.VMEM((ROWS + HALO, c), F32)],
                  compiler_params=_params(("parallel",)), name=name)(da, u0, u0, dw_w, dw_w, dw_b, dw_b)


def _window_of(group):
    w = jnp.float32(POOL_WINDOWS[-1])
    for k in range(len(POOL_WINDOWS) - 2, -1, -1):
        w = jnp.where(group == k, jnp.float32(POOL_WINDOWS[k]), w)
    return w


def _select_level(group, levels):
    out = levels[-1]
    for k in range(len(levels) - 2, -1, -1):
        out = jnp.where(group == k, levels[k], out)
    return out


def _pool_fwd(h, name):
    t, d = h.shape
    c = LANES
    per = d // len(POOL_WINDOWS) // c

    def body(h_ref, o_ref):
        group = pl.program_id(0)
        wf = _window_of(group)

        def step(i, _):
            base = pl.multiple_of(i * ROWS, ROWS)
            lo = pl.multiple_of(jnp.maximum(base - HALO, 0), HALO)
            keep = jnp.where(i > 0, 1.0, 0.0)
            cur = h_ref[pl.ds(base, ROWS), :]
            e = jnp.concatenate([h_ref[pl.ds(lo, HALO), :] * keep, cur], axis=0)
            n = ROWS + HALO
            levels = []
            s = e
            for k in range(len(POOL_WINDOWS)):
                s = s + pltpu.roll(s, 1 << k, 0)
                levels.append(s[HALO:n])
            tpos = (base + lax.broadcasted_iota(jnp.int32, (ROWS, c), 0) + 1).astype(F32)
            pooled = _select_level(group, levels) / jnp.minimum(tpos, wf)
            o_ref[pl.ds(base, ROWS), :] = (pooled - cur).astype(BF16)
            return 0
        lax.fori_loop(0, _steps(t), step, 0)

    return _pcall(body, grid=(len(POOL_WINDOWS), per), in_specs=[pl.BlockSpec((t, c), lambda g, s: (0, g * per + s))],
                  out_specs=pl.BlockSpec((t, c), lambda g, s: (0, g * per + s)), out_shape=jax.ShapeDtypeStruct((t, d), BF16),
                  compiler_params=_params(("parallel", "parallel")), name=name)(h)


def _pool_bwd(dm, name):
    t, d = dm.shape
    c = LANES
    per = d // len(POOL_WINDOWS) // c

    def body(dm_ref, o_ref, pad):
        group = pl.program_id(0)
        wf = _window_of(group)
        pad[t:t + HALO, :] = jnp.zeros((HALO, c), F32)

        def fill(i, _):
            base = pl.multiple_of(i * ROWS, ROWS)
            tpos = (base + lax.broadcasted_iota(jnp.int32, (ROWS, c), 0) + 1).astype(F32)
            pad[pl.ds(base, ROWS), :] = dm_ref[pl.ds(base, ROWS), :] / jnp.minimum(tpos, wf)
            return 0
        lax.fori_loop(0, _steps(t), fill, 0)

        def step(i, _):
            base = pl.multiple_of(i * ROWS, ROWS)
            n = ROWS + HALO
            s = pad[pl.ds(base, n), :]
            levels = []
            for k in range(len(POOL_WINDOWS)):
                s = s + pltpu.roll(s, n - (1 << k), 0)
                levels.append(s[0:ROWS])
            o_ref[pl.ds(base, ROWS), :] = _select_level(group, levels) - dm_ref[pl.ds(base, ROWS), :]
            return 0
        lax.fori_loop(0, _steps(t), step, 0)

    return _pcall(body, grid=(len(POOL_WINDOWS), per), in_specs=[pl.BlockSpec((t, c), lambda g, s: (0, g * per + s))],
                  out_specs=pl.BlockSpec((t, c), lambda g, s: (0, g * per + s)), out_shape=jax.ShapeDtypeStruct((t, d), F32),
                  scratch_shapes=[pltpu.VMEM((t + HALO, c), F32)], compiler_params=_params(("parallel", "parallel")), name=name)(dm)


def _pool_mm_fwd(mixed, wg, scale, res, name):
    t, d = mixed.shape
    ng, gd, _ = wg.shape
    tm = _pick(t, 1024, 16)

    def body(a_ref, w_ref, s_ref, r_ref, o_ref):
        y = jnp.dot(a_ref[...], w_ref[...], preferred_element_type=F32)
        o_ref[...] = r_ref[...] + y * s_ref[...]

    blk = pl.BlockSpec((tm, gd), lambda g, i: (i, g))
    return _pcall(body, grid=(ng, t // tm),
                  in_specs=[blk, pl.BlockSpec((None, gd, gd), lambda g, i: (g, 0, 0)), pl.BlockSpec((1, gd), lambda g, i: (0, g)), blk],
                  out_specs=blk, out_shape=jax.ShapeDtypeStruct((t, d), F32),
                  compiler_params=_params(("parallel", "parallel")), name=name)(mixed, wg, scale, res)


def _pool_mm_bwd(dy, mixed, wg, scale, name):
    t, d = mixed.shape
    ng, gd, _ = wg.shape
    tm = _pick(t, 1024, 16)

    def body(dy_ref, a_ref, w_ref, s_ref, dm_ref, dw_ref, ds_ref):
        a, w, dyv = a_ref[...], w_ref[...], dy_ref[...]
        y = jnp.dot(a, w, preferred_element_type=F32)
        dyp = (dyv * s_ref[...]).astype(BF16)
        dm_ref[...] = lax.dot_general(dyp, w, (((1,), (1,)), ((), ())), preferred_element_type=F32)
        pw = lax.dot_general(a, dyp, (((0,), (0,)), ((), ())), preferred_element_type=F32)
        ps = jnp.sum(dyv * y, axis=0, keepdims=True)

        @pl.when(pl.program_id(1) == 0)
        def _():
            dw_ref[...] = pw
            ds_ref[...] = ps

        @pl.when(pl.program_id(1) > 0)
        def _():
            dw_ref[...] += pw
            ds_ref[...] += ps

    blk = pl.BlockSpec((tm, gd), lambda g, i: (i, g))
    wsp = pl.BlockSpec((None, gd, gd), lambda g, i: (g, 0, 0))
    vec = pl.BlockSpec((1, gd), lambda g, i: (0, g))
    return _pcall(body, grid=(ng, t // tm), in_specs=[blk, blk, wsp, vec], out_specs=[blk, wsp, vec],
                  out_shape=[jax.ShapeDtypeStruct((t, d), F32), jax.ShapeDtypeStruct((ng, gd, gd), F32), jax.ShapeDtypeStruct((1, d), F32)],
                  compiler_params=_params(("parallel", "arbitrary")), name=name)(dy, mixed, wg, scale)


def _rope_tables(positions):
    half = ROT_DIM // 2
    inv_freq = ROPE_THETA ** (-jnp.arange(0, ROT_DIM, 2, dtype=F32) / ROT_DIM)
    ang = positions.astype(F32)[:, None] * inv_freq
    cos, sin = jnp.cos(ang), jnp.sin(ang)
    t = positions.shape[0]
    ones = jnp.ones((t, HEAD - ROT_DIM), F32)
    zeros = jnp.zeros((t, HEAD - ROT_DIM), F32)
    zh = jnp.zeros((t, half), F32)
    c = jnp.concatenate([cos, cos, ones], axis=1)
    s1 = jnp.concatenate([-sin, zh, zeros], axis=1)
    s2 = jnp.concatenate([zh, sin, zeros], axis=1)
    return tuple(jnp.concatenate([a, a], axis=1) for a in (c, s1, s2))


def _half_mean(v, lo):
    s_lo = jnp.sum(jnp.where(lo, v, 0.0), axis=-1, keepdims=True)
    s_hi = jnp.sum(jnp.where(lo, 0.0, v), axis=-1, keepdims=True)
    return jnp.where(lo, s_lo, s_hi) * (1.0 / HEAD)


def _qk_prep_fwd(qkv, tabs, gq2, gk2, n_q, n_kv, name):
    t, width = qkv.shape
    tm = _pick(t, 256, 16)
    nqc, nkc = n_q * HEAD // LANES, n_kv * HEAD // LANES

    def body(x_ref, c_ref, s1_ref, s2_ref, gq_ref, gk_ref, q_ref, k2_ref, v2_ref):
        lo = lax.broadcasted_iota(jnp.int32, (tm, LANES), 1) < HEAD
        cv, s1, s2 = c_ref[...], s1_ref[...], s2_ref[...]

        def normrot(xc, g2):
            y = xc * lax.rsqrt(_half_mean(xc * xc, lo) + EPS) * g2
            return y * cv + pltpu.roll(y, LANES - ROT_DIM // 2, 1) * s1 + pltpu.roll(y, ROT_DIM // 2, 1) * s2

        def twice(y, j):
            sw = pltpu.roll(y, HEAD, 1)
            k2 = jnp.where(lo, y, sw) if j == 0 else jnp.where(lo, sw, y)
            return k2.astype(BF16)

        for ch in range(nqc):
            q_ref[:, ch * LANES:(ch + 1) * LANES] = normrot(x_ref[:, ch * LANES:(ch + 1) * LANES], gq_ref[...]).astype(BF16)
        for ch in range(nkc):
            off = (nqc + ch) * LANES
            y = normrot(x_ref[:, off:off + LANES], gk_ref[...])
            voff = (nqc + nkc + ch) * LANES
            vv = x_ref[:, voff:voff + LANES]
            for j in range(2):
                k2_ref[:, (2 * ch + j) * LANES:(2 * ch + j + 1) * LANES] = twice(y, j)
                v2_ref[:, (2 * ch + j) * LANES:(2 * ch + j + 1) * LANES] = twice(vv, j)

    row = lambda w: pl.BlockSpec((tm, w), lambda i: (i, 0))
    vec = pl.BlockSpec((1, LANES), lambda i: (0, 0))
    return _pcall(body, grid=(t // tm,), in_specs=[row(width), row(LANES), row(LANES), row(LANES), vec, vec],
                  out_specs=[row(n_q * HEAD), row(n_kv * LANES), row(n_kv * LANES)],
                  out_shape=[jax.ShapeDtypeStruct((t, n_q * HEAD), BF16), jax.ShapeDtypeStruct((t, n_kv * LANES), BF16),
                             jax.ShapeDtypeStruct((t, n_kv * LANES), BF16)],
                  compiler_params=_params(("parallel",)), name=name)(qkv, *tabs, gq2, gk2)


def _qk_prep_bwd(dq, dk_cur, dk_prev, dv_cur, dv_prev, qkv, tabs, gq2, gk2, n_q, n_kv, name):
    t, width = qkv.shape
    tm = Q_BLOCK
    nb = t // tm
    nqc, nkc = n_q * HEAD // LANES, n_kv * HEAD // LANES

    def body(dq_ref, kc_ref, kp_ref, vc_ref, vp_ref, x_ref, c_ref, s1_ref, s2_ref, gq_ref, gk_ref, o_ref, dgq_ref, dgk_ref):
        lo = lax.broadcasted_iota(jnp.int32, (tm, LANES), 1) < HEAD
        cv, s1, s2 = c_ref[...], s1_ref[...], s2_ref[...]
        more = jnp.where(pl.program_id(0) < nb - 1, 1.0, 0.0)

        def back(dy, xc, g2):
            dyn = dy * cv + pltpu.roll(dy * s1, ROT_DIM // 2, 1) + pltpu.roll(dy * s2, LANES - ROT_DIM // 2, 1)
            r = lax.rsqrt(_half_mean(xc * xc, lo) + EPS)
            xh = xc * r
            dxh = dyn * g2
            return r * (dxh - xh * _half_mean(dxh * xh, lo)), jnp.sum(dyn * xh, axis=0, keepdims=True)

        def unfold(cur_ref, prev_ref, ch):
            d0 = cur_ref[:, (2 * ch) * LANES:(2 * ch + 1) * LANES] + more * prev_ref[:, (2 * ch) * LANES:(2 * ch + 1) * LANES]
            d1 = cur_ref[:, (2 * ch + 1) * LANES:(2 * ch + 2) * LANES] + more * prev_ref[:, (2 * ch + 1) * LANES:(2 * ch + 2) * LANES]
            return jnp.where(lo, d0 + pltpu.roll(d0, HEAD, 1), d1 + pltpu.roll(d1, HEAD, 1))

        pq = jnp.zeros((1, LANES), F32)
        for ch in range(nqc):
            sl = slice(ch * LANES, (ch + 1) * LANES)
            dx, pg = back(dq_ref[:, sl], x_ref[:, sl], gq_ref[...])
            o_ref[:, sl] = dx.astype(BF16)
            pq = pq + pg
        pk = jnp.zeros((1, LANES), F32)
        for ch in range(nkc):
            sl = slice((nqc + ch) * LANES, (nqc + ch + 1) * LANES)
            dx, pg = back(unfold(kc_ref, kp_ref, ch), x_ref[:, sl], gk_ref[...])
            o_ref[:, sl] = dx.astype(BF16)
            pk = pk + pg
            vs = slice((nqc + nkc + ch) * LANES, (nqc + nkc + ch + 1) * LANES)
            o_ref[:, vs] = unfold(vc_ref, vp_ref, ch).astype(BF16)

        @pl.when(pl.program_id(0) == 0)
        def _():
            dgq_ref[...] = pq
            dgk_ref[...] = pk

        @pl.when(pl.program_id(0) > 0)
        def _():
            dgq_ref[...] += pq
            dgk_ref[...] += pk

    row = lambda w: pl.BlockSpec((tm, w), lambda i: (i, 0))
    nxt = lambda w: pl.BlockSpec((tm, w), lambda i: (jnp.minimum(i + 1, nb - 1), 0))
    vec = pl.BlockSpec((1, LANES), lambda i: (0, 0))
    kvw = n_kv * LANES
    return _pcall(body, carry_us=QK_PREP_BWD_US_PER_ELEM * t * width, grid=(nb,),
                  in_specs=[row(n_q * HEAD), row(kvw), nxt(kvw), row(kvw), nxt(kvw), row(width), row(LANES), row(LANES), row(LANES), vec, vec],
                  out_specs=[row(width), vec, vec],
                  out_shape=[jax.ShapeDtypeStruct((t, width), BF16), jax.ShapeDtypeStruct((1, LANES), F32), jax.ShapeDtypeStruct((1, LANES), F32)],
                  compiler_params=_params(("arbitrary",)), name=name)(dq, dk_cur, dk_prev, dv_cur, dv_prev, qkv, *tabs, gq2, gk2)


def _band_scores(qh, kc, kp, n, sink_row, lo_row, is_lo):
    scale = 1.0 / math.sqrt(HEAD)
    nt = (((1,), (1,)), ((), ()))
    s_c = lax.dot_general(qh, kc, nt, preferred_element_type=F32) * scale
    s_p = lax.dot_general(qh, kp, nt, preferred_element_type=F32) * scale
    qi = lax.broadcasted_iota(jnp.int32, (Q_BLOCK, Q_BLOCK), 0)
    kj = lax.broadcasted_iota(jnp.int32, (Q_BLOCK, Q_BLOCK), 1)
    s_c = jnp.where(kj <= qi, s_c, -jnp.inf)
    s_p = jnp.where((kj > qi) & (n > 0), s_p, -jnp.inf)
    pick = lo_row if is_lo else jnp.logical_not(lo_row)
    sink = jnp.max(jnp.where(pick, sink_row, -jnp.inf), axis=-1, keepdims=True)
    return s_c, s_p, sink


def _attn_fwd(q, k2, v2, sink_tab, name):
    t, dq = q.shape
    nc = dq // LANES
    nb = t // Q_BLOCK
    nkv = k2.shape[1] // LANES
    per_kv = nc // nkv

    def body(q_ref, kc_ref, kp_ref, vc_ref, vp_ref, s_ref, o_ref, lse_ref):
        lo = lax.broadcasted_iota(jnp.int32, (Q_BLOCK, LANES), 1) < HEAD
        lo_row = lax.broadcasted_iota(jnp.int32, (1, LANES), 1) < HEAD
        for blk in range(QB):
            n = pl.program_id(1) * QB + blk
            rows = slice(blk * Q_BLOCK, (blk + 1) * Q_BLOCK)
            before = slice((blk - 1) * Q_BLOCK, blk * Q_BLOCK)
            kc, vc = kc_ref[rows, :], vc_ref[rows, :]
            kp, vp = (kp_ref[...], vp_ref[...]) if blk == 0 else (kc_ref[before, :], vc_ref[before, :])
            for cc in range(per_kv):
                cols = slice(cc * LANES, (cc + 1) * LANES)
                qv = q_ref[rows, cols].astype(F32)
                outs, lses = [], []
                for is_lo in (True, False):
                    qh = jnp.where(lo, qv, 0.0) if is_lo else jnp.where(lo, 0.0, qv)
                    s_c, s_p, sink = _band_scores(qh.astype(BF16), kc, kp, n, s_ref[8 * cc:8 * cc + 1, :], lo_row, is_lo)
                    m = jnp.maximum(jnp.maximum(jnp.max(s_c, axis=-1, keepdims=True), jnp.max(s_p, axis=-1, keepdims=True)), sink)
                    p_c, p_p = jnp.exp(s_c - m), jnp.exp(s_p - m)
                    denom = jnp.sum(p_c, axis=-1, keepdims=True) + jnp.sum(p_p, axis=-1, keepdims=True) + jnp.exp(sink - m)
                    pv = jnp.dot(p_c.astype(BF16), vc, preferred_element_type=F32) + jnp.dot(p_p.astype(BF16), vp, preferred_element_type=F32)
                    outs.append(pv / denom)
                    lses.append(m + jnp.log(denom))
                o_ref[rows, cols] = jnp.where(lo, outs[0], outs[1]).astype(BF16)
                lse_ref[cc, rows, :] = jnp.where(lo, lses[0], lses[1])

    QB = ATTN_BLOCKS_PER_STEP
    qs = pl.BlockSpec((QB * Q_BLOCK, per_kv * LANES), lambda k, n: (n, k))
    cur = pl.BlockSpec((QB * Q_BLOCK, LANES), lambda k, n: (n, k))
    prev = pl.BlockSpec((Q_BLOCK, LANES), lambda k, n: (jnp.maximum(n * QB - 1, 0), k))
    return _pcall(body, carry_us=ATTN_US_PER_STEP * nkv * nb, grid=(nkv, nb // QB),
                  in_specs=[qs, cur, prev, cur, prev, pl.BlockSpec((8 * per_kv, LANES), lambda k, n: (k, 0))],
                  out_specs=[qs, pl.BlockSpec((per_kv, QB * Q_BLOCK, LANES), lambda k, n: (k, n, 0))],
                  out_shape=[jax.ShapeDtypeStruct((t, dq), BF16), jax.ShapeDtypeStruct((nc, t, LANES), F32)],
                  compiler_params=_params(("parallel", "parallel")), name=name)(q, k2, k2, v2, v2, sink_tab)


def _attn_bwd(do, q, o, lse, k2, v2, sink_tab, name):
    t, dq = q.shape
    nc = dq // LANES
    nb = t // Q_BLOCK
    nkv = k2.shape[1] // LANES
    per_kv = nc // nkv
    scale = 1.0 / math.sqrt(HEAD)
    tn_ = (((0,), (0,)), ((), ()))
    nt = (((1,), (1,)), ((), ()))

    def body(do_ref, q_ref, o_ref, lse_ref, kc_ref, kp_ref, vc_ref, vp_ref, s_ref,
             dq_ref, dkc_ref, dkp_ref, dvc_ref, dvp_ref, dsk_ref):
        lo = lax.broadcasted_iota(jnp.int32, (Q_BLOCK, LANES), 1) < HEAD
        lo_row = lax.broadcasted_iota(jnp.int32, (1, LANES), 1) < HEAD
        for blk in range(QB):
            n = pl.program_id(1) * QB + blk
            rows = slice(blk * Q_BLOCK, (blk + 1) * Q_BLOCK)
            before = slice((blk - 1) * Q_BLOCK, blk * Q_BLOCK)
            kc, vc = kc_ref[rows, :], vc_ref[rows, :]
            kp, vp = (kp_ref[...], vp_ref[...]) if blk == 0 else (kc_ref[before, :], vc_ref[before, :])
            dkc = dkp = dvc = dvp = None
            for cc in range(per_kv):
                cols = slice(cc * LANES, (cc + 1) * LANES)
                qv, dov, ov, lsev = q_ref[rows, cols].astype(F32), do_ref[rows, cols], o_ref[rows, cols].astype(F32), lse_ref[cc, rows, :]
                dqs, dsinks = [], []
                for is_lo in (True, False):
                    half = lo if is_lo else jnp.logical_not(lo)
                    qh = jnp.where(half, qv, 0.0).astype(BF16)
                    doh = jnp.where(half, dov, 0.0)
                    s_c, s_p, sink = _band_scores(qh, kc, kp, n, s_ref[8 * cc:8 * cc + 1, :], lo_row, is_lo)
                    lse_h = jnp.max(jnp.where(half, lsev, -jnp.inf), axis=-1, keepdims=True)
                    p_c, p_p = jnp.exp(s_c - lse_h), jnp.exp(s_p - lse_h)
                    delta = jnp.sum(doh * ov, axis=-1, keepdims=True)
                    dob = doh.astype(BF16)
                    ds_c = (p_c * (lax.dot_general(dob, vc, nt, preferred_element_type=F32) - delta)).astype(BF16)
                    ds_p = (p_p * (lax.dot_general(dob, vp, nt, preferred_element_type=F32) - delta)).astype(BF16)
                    dsinks.append(-jnp.sum(jnp.exp(sink - lse_h) * delta, axis=0, keepdims=True))
                    dqs.append((jnp.dot(ds_c, kc, preferred_element_type=F32) + jnp.dot(ds_p, kp, preferred_element_type=F32)) * scale)
                    parts = (lax.dot_general(ds_c, qh, tn_, preferred_element_type=F32) * scale,
                             lax.dot_general(ds_p, qh, tn_, preferred_element_type=F32) * scale,
                             lax.dot_general(p_c.astype(BF16), dob, tn_, preferred_element_type=F32),
                             lax.dot_general(p_p.astype(BF16), dob, tn_, preferred_element_type=F32))
                    if dkc is None:
                        dkc, dkp, dvc, dvp = parts
                    else:
                        dkc, dkp, dvc, dvp = dkc + parts[0], dkp + parts[1], dvc + parts[2], dvp + parts[3]
                dq_ref[rows, cols] = jnp.where(lo, dqs[0], dqs[1])
                dsk_ref[blk, 8 * cc:8 * cc + 8, :] = jnp.zeros((8, LANES), F32) + jnp.where(lo_row, dsinks[0], dsinks[1])
            dkc_ref[rows, :] = dkc
            dkp_ref[rows, :] = dkp
            dvc_ref[rows, :] = dvc
            dvp_ref[rows, :] = dvp

    QB = ATTN_BLOCKS_PER_STEP
    qs = pl.BlockSpec((QB * Q_BLOCK, per_kv * LANES), lambda k, n: (n, k))
    cur = pl.BlockSpec((QB * Q_BLOCK, LANES), lambda k, n: (n, k))
    prev = pl.BlockSpec((Q_BLOCK, LANES), lambda k, n: (jnp.maximum(n * QB - 1, 0), k))
    kv_shape = jax.ShapeDtypeStruct((t, nkv * LANES), F32)
    return _pcall(body, carry_us=ATTN_US_PER_STEP * nkv * nb, grid=(nkv, nb // QB),
                  in_specs=[qs, qs, qs, pl.BlockSpec((per_kv, QB * Q_BLOCK, LANES), lambda k, n: (k, n, 0)),
                            cur, prev, cur, prev, pl.BlockSpec((8 * per_kv, LANES), lambda k, n: (k, 0))],
                  out_specs=[qs, cur, cur, cur, cur, pl.BlockSpec((None, QB, 8 * per_kv, LANES), lambda k, n: (k, n, 0, 0))],
                  out_shape=[jax.ShapeDtypeStruct((t, dq), F32), kv_shape, kv_shape, kv_shape, kv_shape,
                             jax.ShapeDtypeStruct((nkv, nb, 8 * per_kv, LANES), F32)],
                  compiler_params=_params(("parallel", "parallel")), name=name)(do, q, o, lse, k2, k2, v2, v2, sink_tab)


def _peer(k):
    x, y, c = lax.axis_index("x"), lax.axis_index("y"), lax.axis_index("c")
    flip = lambda v, bit: 1 - v if bit else v
    return (flip(x, k & 4), flip(y, k & 2), flip(c, k & 1))


def _my_index():
    return 4 * lax.axis_index("x") + 2 * lax.axis_index("y") + lax.axis_index("c")


def _peer_index(k):
    px, py, pc = _peer(k)
    return 4 * px + 2 * py + pc


def _all_gather(shards, name):
    n = len(shards)
    any_spec = pl.BlockSpec(memory_space=pl.ANY)

    def body(*refs):
        ins, outs = refs[:n], refs[n:2 * n]
        send_sems, recv_sems, local_sems = refs[2 * n:]
        me = _my_index()
        local = [pltpu.make_async_copy(ins[a], outs[a].at[me], local_sems.at[a]) for a in range(n)]
        for cp in local:
            cp.start()
        sends = []
        for k in range(1, N_DEV):
            for a in range(n):
                cp = pltpu.make_async_remote_copy(src_ref=ins[a], dst_ref=outs[a].at[me], send_sem=send_sems.at[a, k - 1],
                                                  recv_sem=recv_sems.at[a, k - 1], device_id=_peer(k), device_id_type=MESH_ID)
                cp.start()
                sends.append(cp)
        for k in range(1, N_DEV):
            for a in range(n):
                pltpu.make_async_remote_copy(src_ref=ins[a], dst_ref=outs[a].at[_peer_index(k)], send_sem=send_sems.at[a, k - 1],
                                             recv_sem=recv_sems.at[a, k - 1], device_id=_peer(k), device_id_type=MESH_ID).wait_recv()
        for cp in sends:
            cp.wait_send()
        for cp in local:
            cp.wait()

    return _pcall(body, in_specs=[any_spec] * n, out_specs=[any_spec] * n,
                  out_shape=[jax.ShapeDtypeStruct((N_DEV,) + s.shape, s.dtype) for s in shards],
                  scratch_shapes=[pltpu.SemaphoreType.DMA((n, N_DEV - 1)), pltpu.SemaphoreType.DMA((n, N_DEV - 1)),
                                  pltpu.SemaphoreType.DMA((n,))],
                  name=name)(*shards)


GATHER1_PEERS = (1, 2, 4, 6)
GATHER2_PEERS = (2, 4, 6)
SCATTER_PEERS = tuple(range(1, N_DEV))
MAX_SEMS = N_DEV - 1
MAX_JOBS = 6
US_PER_MB = {"gather1": 5.4, "gather2": 0.6, "scatter": 10.8}
SCATTER_PIECE_US = 110.0
PIECE_US = {"gather1": 62.0, "gather2": 1e9, "scatter": SCATTER_PIECE_US}


class _Job:
    def __init__(self, key, kind, src, lo=0, hi=None, dst=None):
        self.key, self.kind, self.src, self.dst = key, kind, src, dst
        shape = src.shape if kind != "gather1" else (N_DEV,) + src.shape
        self.out_shape = jax.ShapeDtypeStruct(shape, src.dtype)
        self.rows = shape[1]
        self.lo, self.hi = lo, self.rows if hi is None else hi
        self.row_us = US_PER_MB[kind] * math.prod(shape) * src.dtype.itemsize / 1e6 / self.rows
        pieces = max(1, round(self.row_us * self.rows / PIECE_US[kind]))
        while pieces > 1 and self.rows % (16 * pieces):
            pieces -= 1
        self.piece = self.rows // pieces

    @property
    def cost_us(self):
        return self.row_us * (self.hi - self.lo)


class _Comm:
    def __init__(self):
        self.queue, self.gathered, self.scattered, self.layer = [], {}, [], 0

    def push(self, key, kind, src):
        self.queue.append(_Job(key, kind, src))

    def take(self, budget_us, upto=None):
        jobs = [j for j in self.queue if j.kind == "gather2"][:MAX_JOBS]
        used = sum(j.cost_us for j in jobs)
        if upto is not None and any(j.key == upto for j in jobs):
            self.queue = [j for j in self.queue if j not in jobs]
            return jobs
        for j in [j for j in self.queue if j.kind != "gather2"]:
            if len(jobs) >= MAX_JOBS:
                break
            urgent = j.kind == "gather1" and int(j.key[1]) <= self.layer
            piece_us = j.row_us * j.piece
            n = 0
            while j.lo + (n + 1) * j.piece <= j.hi and ((used < budget_us) if urgent else (used + 0.5 * piece_us <= budget_us)):
                n += 1
                used += piece_us
            if n == 0:
                break
            part = _Job(j.key, j.kind, j.src, j.lo, j.lo + n * j.piece, j.dst)
            part.parent = j
            j.lo = part.hi
            jobs.append(part)
            if j.lo < j.hi or j.key == upto:
                break
        self.queue = [j for j in self.queue if j not in jobs and j.lo < j.hi]
        return jobs

    def finish(self, job, result):
        if job.kind == "gather2":
            self.gathered[job.key] = result
        elif job.hi < job.rows:
            job.parent.dst = result
        elif job.kind == "gather1":
            self.queue.insert(0, _Job(job.key, "gather2", result))
        else:
            self.scattered.append((job.key, result))

    def need(self, key):
        while key not in self.gathered:
            assert any(j.key == key for j in self.queue), key
            self.flush(self.take(1e9, upto=key))
        return self.gathered[key]

    def flush(self, jobs):
        def body(o_ref):
            o_ref[...] = jnp.zeros_like(o_ref)
        _carry(body, jobs, self, dict(in_specs=[], out_specs=pl.BlockSpec(memory_space=pltpu.VMEM),
                                      out_shape=jax.ShapeDtypeStruct((8, LANES), F32), name="exchange"))()


def _job_copies(job, src, dst, send_sems, recv_sems, local_sem):
    me = _my_index()
    peers = {"gather1": GATHER1_PEERS, "gather2": GATHER2_PEERS, "scatter": SCATTER_PEERS}[job.kind]
    sends, recvs = [], []
    rows = pl.ds(job.lo, job.hi - job.lo)
    for i, k in enumerate(peers):
        if job.kind == "gather1":
            s_ref, d_ref, to, got = src.at[rows], dst.at[me, rows], _peer(k), dst.at[_peer_index(k), rows]
        elif job.kind == "gather2":
            s_ref, d_ref, to, got = src.at[_peer_index(k)], dst.at[_peer_index(k)], _peer(1), dst.at[_peer_index(k | 1)]
        else:
            s_ref, d_ref, to, got = src.at[_peer_index(k), rows], dst.at[me, rows], _peer(k), dst.at[_peer_index(k), rows]
        sends.append(pltpu.make_async_remote_copy(src_ref=s_ref, dst_ref=d_ref, send_sem=send_sems.at[i], recv_sem=recv_sems.at[i],
                                                  device_id=to, device_id_type=MESH_ID))
        recvs.append(pltpu.make_async_remote_copy(src_ref=s_ref, dst_ref=got, send_sem=send_sems.at[i], recv_sem=recv_sems.at[i],
                                                  device_id=to, device_id_type=MESH_ID))
    local = None
    if job.kind == "gather1":
        local = pltpu.make_async_copy(src.at[rows], dst.at[me, rows], local_sem)
    elif job.kind == "scatter":
        local = pltpu.make_async_copy(src.at[me, rows], dst.at[me, rows], local_sem)
    return sends, recvs, local


def _carry(body, jobs, comm, kw):
    kw = dict(kw)
    grid = tuple(kw.get("grid", ()))
    in_specs = list(kw["in_specs"])
    single = not isinstance(kw["out_specs"], (list, tuple))
    out_specs = [kw["out_specs"]] if single else list(kw["out_specs"])
    out_shape = [kw["out_shape"]] if single else list(kw["out_shape"])
    scratch = list(kw.get("scratch_shapes", []))
    n_in, n_out, n_scr, nj = len(in_specs), len(out_specs), len(scratch), len(jobs)
    any_spec = pl.BlockSpec(memory_space=pl.ANY)
    landed = [a for a, job in enumerate(jobs) if job.dst is not None]
    n_land = len(landed)

    def wrapped(*refs):
        pos = 0

        def take(k):
            nonlocal pos
            part = refs[pos:pos + k]
            pos += k
            return part
        ins, rin, _, outs, rout, scr = take(n_in), take(nj), take(n_land), take(n_out), take(nj), take(n_scr)
        send_sems, recv_sems, local_sems = take(3)

        def copies():
            return [_job_copies(job, rin[a], rout[a], send_sems.at[a], recv_sems.at[a], local_sems.at[a]) for a, job in enumerate(jobs)]

        def start():
            for sends, _, local in copies():
                if local is not None:
                    local.start()
                for cp in sends:
                    cp.start()

        def finish():
            for sends, recvs, local in copies():
                for cp in recvs:
                    cp.wait_recv()
                for cp in sends:
                    cp.wait_send()
                if local is not None:
                    local.wait()

        if grid:
            first = functools.reduce(jnp.logical_and, [pl.program_id(a) == 0 for a in range(len(grid))])
            last = functools.reduce(jnp.logical_and, [pl.program_id(a) == grid[a] - 1 for a in range(len(grid))])
            pl.when(first)(start)
            body(*ins, *outs, *scr)
            pl.when(last)(finish)
        else:
            start()
            body(*ins, *outs, *scr)
            finish()

    aliases = {n_in + a: n_out + a for a, job in enumerate(jobs) if job.kind == "gather2"}
    aliases.update({n_in + nj + i: n_out + a for i, a in enumerate(landed)})
    extra = dict(dimension_semantics=("arbitrary",) * len(grid)) if grid else {}
    call = _raw_call(wrapped, in_specs=in_specs + [any_spec] * (nj + n_land), out_specs=out_specs + [any_spec] * nj,
                     out_shape=out_shape + [job.out_shape for job in jobs],
                     scratch_shapes=scratch + [pltpu.SemaphoreType.DMA((nj, MAX_SEMS)), pltpu.SemaphoreType.DMA((nj, MAX_SEMS)),
                                               pltpu.SemaphoreType.DMA((nj,))],
                     input_output_aliases=aliases, compiler_params=_params(**extra), name=kw["name"],
                     **({"grid": grid} if grid else {}))

    def run(*args):
        res = call(*args, *[job.src for job in jobs], *[jobs[a].dst for a in landed])
        for job, r in zip(jobs, res[n_out:]):
            comm.finish(job, r)
        return res[0] if single else list(res[:n_out])
    return run


def _adam(g, w, m, v):
    m2 = ADAM_B1 * m + (1.0 - ADAM_B1) * g
    v2 = ADAM_B2 * v + (1.0 - ADAM_B2) * (g * g)
    m_hat = m2 / (1.0 - ADAM_B1 ** ADAM_STEP)
    v_hat = v2 / (1.0 - ADAM_B2 ** ADAM_STEP)
    delta = -ADAM_LR * (m_hat / (jnp.sqrt(v_hat) + ADAM_EPS) + ADAM_WD * w)
    return delta, m2, v2


def _sum_adam(parts, w, m, v, name):
    r, c = w.shape
    tr = _pick(r, max(8, (1 << 19) // c), 8)

    def body(p_ref, w_ref, m_ref, v_ref, g_ref, d_ref, m2_ref, v2_ref):
        g = p_ref[0].astype(F32)
        for j in range(1, N_DEV):
            g = g + p_ref[j].astype(F32)
        delta, m2, v2 = _adam(g, w_ref[...], m_ref[...], v_ref[...])
        g_ref[...] = g
        d_ref[...] = delta
        m2_ref[...] = m2
        v2_ref[...] = v2

    blk = pl.BlockSpec((tr, c), lambda i: (i, 0))
    shp = jax.ShapeDtypeStruct((r, c), F32)
    return _pcall(body, grid=(r // tr,),
                  in_specs=[pl.BlockSpec((N_DEV, tr, c), lambda i: (0, i, 0)), blk, blk, blk],
                  out_specs=[blk] * 4, out_shape=[shp] * 4, compiler_params=_params(("parallel",)), name=name)(parts, w, m, v)


def _small_layout(rep_shapes, sh_shapes):
    rows_r = [-(-s[1] // LANES) for s in rep_shapes]
    off_r = [sum(rows_r[:i]) for i in range(len(rows_r))]
    tot_r = -(-max(sum(rows_r), 8) // 8) * 8
    rows_s = [-(-s[-2] // 8) * 8 for s in sh_shapes]
    off_s = [sum(rows_s[:i]) for i in range(len(rows_s))]
    tot_s = max(sum(rows_s), 8)
    cmax = max([s[-1] for s in sh_shapes] + [LANES])
    return rows_r, off_r, tot_r, off_s, tot_s, cmax


def _small_exchange(rep_parts, sh_parts, name):
    nr, ns = len(rep_parts), len(sh_parts)
    rows_r, off_r, tot_r, off_s, tot_s, cmax = _small_layout([p.shape for p in rep_parts], [p.shape for p in sh_parts])
    vm = pl.BlockSpec(memory_space=pltpu.VMEM)

    def body(*refs):
        pos = 0

        def take(k):
            nonlocal pos
            out = refs[pos:pos + k]
            pos += k
            return out
        rp, sp = take(nr), take(ns)
        out_r, out_s = take(2)
        pack_r, got_r, pack_s, got_s, send_r, recv_r, send_s, recv_s = take(8)
        me = _my_index()
        pack_r[...] = jnp.zeros_like(pack_r)
        pack_s[...] = jnp.zeros_like(pack_s)
        for i in range(nr):
            nfull = rep_parts[i].shape[1]
            for rr in range(rows_r[i]):
                wdt = min(LANES, nfull - rr * LANES)
                pack_r[off_r[i] + rr:off_r[i] + rr + 1, 0:wdt] = rp[i][0:1, rr * LANES:rr * LANES + wdt]
        for i in range(ns):
            _, r_i, c_i = sh_parts[i].shape
            for j in range(N_DEV):
                pack_s[j, off_s[i]:off_s[i] + r_i, 0:c_i] = sp[i][j]
        got_r[me] = pack_r[...]
        got_s[me] = pack_s[me]
        sends = []
        for k in range(1, N_DEV):
            a = pltpu.make_async_remote_copy(src_ref=pack_r, dst_ref=got_r.at[me], send_sem=send_r.at[k - 1], recv_sem=recv_r.at[k - 1],
                                             device_id=_peer(k), device_id_type=MESH_ID)
            b = pltpu.make_async_remote_copy(src_ref=pack_s.at[_peer_index(k)], dst_ref=got_s.at[me], send_sem=send_s.at[k - 1],
                                             recv_sem=recv_s.at[k - 1], device_id=_peer(k), device_id_type=MESH_ID)
            a.start()
            b.start()
            sends += [a, b]
        for k in range(1, N_DEV):
            pltpu.make_async_remote_copy(src_ref=pack_r, dst_ref=got_r.at[_peer_index(k)], send_sem=send_r.at[k - 1],
                                         recv_sem=recv_r.at[k - 1], device_id=_peer(k), device_id_type=MESH_ID).wait_recv()
            pltpu.make_async_remote_copy(src_ref=pack_s.at[me], dst_ref=got_s.at[_peer_index(k)], send_sem=send_s.at[k - 1],
                                         recv_sem=recv_s.at[k - 1], device_id=_peer(k), device_id_type=MESH_ID).wait_recv()
        for cp in sends:
            cp.wait_send()
        tot_rep = got_r[0]
        tot_sh = got_s[0]
        for j in range(1, N_DEV):
            tot_rep = tot_rep + got_r[j]
            tot_sh = tot_sh + got_s[j]
        out_r[...] = tot_rep
        out_s[...] = tot_sh

    return _pcall(body, carry_us=SMALL_EXCHANGE_CARRY_US, in_specs=[vm] * (nr + ns), out_specs=[vm] * 2,
                  out_shape=[jax.ShapeDtypeStruct((tot_r, LANES), F32), jax.ShapeDtypeStruct((tot_s, cmax), F32)],
                  scratch_shapes=[pltpu.VMEM((tot_r, LANES), F32), pltpu.VMEM((N_DEV, tot_r, LANES), F32),
                                  pltpu.VMEM((N_DEV, tot_s, cmax), F32), pltpu.VMEM((N_DEV, tot_s, cmax), F32),
                                  pltpu.SemaphoreType.DMA((N_DEV - 1,)), pltpu.SemaphoreType.DMA((N_DEV - 1,)),
                                  pltpu.SemaphoreType.DMA((N_DEV - 1,)), pltpu.SemaphoreType.DMA((N_DEV - 1,))],
                  compiler_params=_params(), name=name)(*rep_parts, *sh_parts)


def _small_adam(tot_rep, tot_sh, rep_w, rep_m, rep_v, sh_w, sh_m, sh_v, name):
    nr, ns = len(rep_w), len(sh_w)
    rows_r, off_r, _, off_s, _, _ = _small_layout([w.shape for w in rep_w], [w.shape for w in sh_w])
    vm = pl.BlockSpec(memory_space=pltpu.VMEM)

    def body(*refs):
        pos = 0

        def take(k):
            nonlocal pos
            out = refs[pos:pos + k]
            pos += k
            return out
        (tr_ref, ts_ref), rw, rm, rv, sw, sm, sv = take(2), take(nr), take(nr), take(nr), take(ns), take(ns), take(ns)
        rg, rd, rm2, rv2 = take(nr), take(nr), take(nr), take(nr)
        sg, sd, sm2, sv2 = take(ns), take(ns), take(ns), take(ns)
        for i in range(nr):
            nfull = rep_w[i].shape[1]
            for rr in range(rows_r[i]):
                wdt = min(LANES, nfull - rr * LANES)
                rg[i][0:1, rr * LANES:rr * LANES + wdt] = tr_ref[off_r[i] + rr:off_r[i] + rr + 1, 0:wdt]
            delta, m2, v2 = _adam(rg[i][...], rw[i][...], rm[i][...], rv[i][...])
            rd[i][...] = delta
            rm2[i][...] = m2
            rv2[i][...] = v2
        for i in range(ns):
            r_i, c_i = sh_w[i].shape
            g = ts_ref[off_s[i]:off_s[i] + r_i, 0:c_i]
            delta, m2, v2 = _adam(g, sw[i][...], sm[i][...], sv[i][...])
            sg[i][...] = g
            sd[i][...] = delta
            sm2[i][...] = m2
            sv2[i][...] = v2

    shapes = [jax.ShapeDtypeStruct(w.shape, F32) for w in rep_w] * 4 + [jax.ShapeDtypeStruct(w.shape, F32) for w in sh_w] * 4
    outs = _pcall(body, in_specs=[vm] * (2 + 3 * nr + 3 * ns), out_specs=[vm] * len(shapes), out_shape=shapes,
                  compiler_params=_params(), name=name)(tot_rep, tot_sh, *rep_w, *rep_m, *rep_v, *sh_w, *sh_m, *sh_v)
    rep_out = [outs[i * nr:(i + 1) * nr] for i in range(4)]
    sh_out = [outs[4 * nr + i * ns:4 * nr + (i + 1) * ns] for i in range(4)]
    return rep_out, sh_out


_CONF = ("norm_g", "a_w_in", "a_b_in", "a_dw_w", "a_dw_b", "a_ln_g", "a_ln_b", "a_w_out", "a_b_out")
_FFN = ("ffn_norm_g", "ffn_w_up", "ffn_dw_w", "ffn_dw_b", "ffn_w_down")
_POOL = ("norm_g", "b_w_group", "b_scale")
_ATTN = ("norm_g", "c_w_qkv", "c_q_norm_g", "c_k_norm_g", "c_sinks", "c_w_o")
_LAYERS = (_CONF + _FFN, _POOL + _FFN, _ATTN + _FFN, _CONF + _FFN)
_NAMES = tuple("l%d_%s" % (i, n) for i, names in enumerate(_LAYERS) for n in names)
_BIG = ("a_w_in", "a_w_out", "ffn_w_up", "ffn_w_down", "b_w_group", "c_w_qkv", "c_w_o")
_SHARDED_SMALL = ("a_dw_w", "ffn_dw_w")


def _pad_rows(a, mult=8):
    r = a.shape[0]
    rp = -(-r // mult) * mult
    return a if rp == r else jnp.pad(a, ((0, rp - r), (0, 0)))


def _unstack_cols(st, rows):
    s, r, cs = st.shape
    return jnp.transpose(st, (1, 0, 2)).reshape(r, s * cs)[:rows]


def _stack_cols(a):
    r, c = a.shape
    return jnp.transpose(a.reshape(r, N_DEV, c // N_DEV), (1, 0, 2))


def _row(v):
    return v.reshape(1, -1)


def _ffn_forward(x_mid, p, tag):
    h2 = _rms_fwd(x_mid, _row(p["ffn_norm_g"]), BF16, "rms_fwd_bf16")
    u0 = _mm(h2, p["ffn_w_up"], "nn", name="ffn_up", tn=1408, tk=2048)
    a = _ffn_act_fwd(u0, p["ffn_dw_w"], _row(p["ffn_dw_b"]), "ffn_act_fwd")
    x_out = _mm(a, p["ffn_w_down"], "nn", res=x_mid, name="ffn_down", tk=2816)
    return x_out, dict(h2=h2, u0=u0, a=a)


def _ffn_backward(dx_out, x_mid, p, sv, grads):
    dwd = _mm(sv["a"], dx_out, "tn", out_dtype=BF16, name="ffn_down_dw", tm=1408, tk=2048)
    grads["ffn_w_down"] = dwd.reshape(N_DEV, dwd.shape[0] // N_DEV, dwd.shape[1])
    da = _mm(dx_out, p["ffn_w_down"], "nt", name="ffn_down_dx", tn=1408, tk=2048)
    du0, dww, dwb = _ffn_act_bwd(da, sv["u0"], p["ffn_dw_w"], _row(p["ffn_dw_b"]), "ffn_act_bwd")
    kw = dww.shape[1]
    grads["ffn_dw_w"] = _stack_cols(jnp.transpose(dww, (1, 0, 2)).reshape(kw, -1))
    grads["ffn_dw_b"] = dwb.reshape(1, -1)
    grads["ffn_w_up"] = _mm(sv["h2"], du0, "tn", out_dtype=BF16, out_stack=N_DEV, name="ffn_up_dw", tn=1408, tk=2048)
    dh2 = _mm(du0, p["ffn_w_up"], "nt", name="ffn_up_dx", tk=1408, panels=2)
    dx_mid, dg, _ = _rms_bwd(dh2, x_mid, _row(p["ffn_norm_g"]), dx_out, "rms_bwd")
    grads["ffn_norm_g"] = dg
    return dx_mid


def _conf_forward(x, p):
    h = _rms_fwd(x, _row(p["norm_g"]), BF16, "rms_fwd_bf16")
    u = _mm(h, p["a_w_in"], "nn", bias=_row(p["a_b_in"]), name="conf_in", tn=512, tk=2048)
    cpre = _conf_conv_fwd(u, p["a_dw_w"], _row(p["a_dw_b"]), "conf_conv_fwd")
    s = _ln_silu_fwd(cpre, _row(p["a_ln_g"]), _row(p["a_ln_b"]), "ln_silu_fwd")
    x_mid = _mm(s, p["a_w_out"], "nn", bias=_row(p["a_b_out"]), res=x, name="conf_out", tk=2048)
    return x_mid, dict(h=h, u=u, cpre=cpre, s=s)


def _conf_backward(dx_mid, x, p, sv, grads):
    dwo = _mm(sv["s"], dx_mid, "tn", out_dtype=BF16, name="conf_out_dw", tk=2048)
    grads["a_w_out"] = dwo.reshape(N_DEV, dwo.shape[0] // N_DEV, dwo.shape[1])
    ds = _mm(dx_mid, p["a_w_out"], "nt", name="conf_out_dx", tk=2048)
    dc, dlg, dlb = _ln_silu_bwd(ds, sv["cpre"], _row(p["a_ln_g"]), _row(p["a_ln_b"]), "ln_silu_bwd")
    grads["a_ln_g"], grads["a_ln_b"] = dlg, dlb
    du, dww, dwb, dbin = _conf_conv_bwd(dc, sv["u"], p["a_dw_w"], "conf_conv_bwd")
    grads["a_dw_w"] = _stack_cols(dww)
    grads["a_dw_b"] = dwb
    grads["a_b_in"] = dbin.reshape(1, -1)
    grads["a_w_in"] = _mm(sv["h"], du, "tn", out_dtype=BF16, out_stack=N_DEV, name="conf_in_dw", tn=512, tk=2048)
    dh = _mm(du, p["a_w_in"], "nt", name="conf_in_dx", tk=512, panels=4)
    dx, dg, dbo = _rms_bwd(dh, x, _row(p["norm_g"]), dx_mid, "rms_bwd")
    grads["norm_g"] = dg
    grads["a_b_out"] = dbo
    return dx


def _pool_forward(x, p):
    h = _rms_fwd(x, _row(p["norm_g"]), F32, "rms_fwd_f32")
    mixed = _pool_fwd(h, "pool_fwd")
    x_mid = _pool_mm_fwd(mixed, p["b_w_group"], _row(p["b_scale"]), x, "pool_mm_fwd")
    return x_mid, dict(mixed=mixed)


def _pool_backward(dx_mid, x, p, sv, grads):
    dmixed, dwg, dscale = _pool_mm_bwd(dx_mid, sv["mixed"], p["b_w_group"], _row(p["b_scale"]), "pool_mm_bwd")
    ng, gd, _ = dwg.shape
    grads["b_w_group"] = jnp.transpose(dwg.reshape(ng, N_DEV, gd // N_DEV, gd), (1, 0, 2, 3)).reshape(N_DEV, ng * gd // N_DEV, gd).astype(BF16)
    grads["b_scale"] = dscale
    dh = _pool_bwd(dmixed, "pool_bwd")
    dx, dg, _ = _rms_bwd(dh, x, _row(p["norm_g"]), dx_mid, "rms_bwd")
    grads["norm_g"] = dg
    return dx


def _attn_tables(p, positions, d_model):
    n_q = d_model // HEAD
    n_kv = n_q // 8
    tabs = _rope_tables(positions)
    gq2 = jnp.concatenate([p["c_q_norm_g"], p["c_q_norm_g"]]).reshape(1, LANES)
    gk2 = jnp.concatenate([p["c_k_norm_g"], p["c_k_norm_g"]]).reshape(1, LANES)
    sink_tab = jnp.repeat(jnp.repeat(p["c_sinks"].reshape(-1, 2), HEAD, axis=1), 8, axis=0)
    return n_q, n_kv, tabs, gq2, gk2, sink_tab


def _attn_forward(x, p, positions):
    n_q, n_kv, tabs, gq2, gk2, sink_tab = _attn_tables(p, positions, x.shape[1])
    h = _rms_fwd(x, _row(p["norm_g"]), BF16, "rms_fwd_bf16")
    qkv = _mm(h, p["c_w_qkv"], "nn", name="attn_qkv", tn=1280, tk=2048)
    q, k2, v2 = _qk_prep_fwd(qkv, tabs, gq2, gk2, n_q, n_kv, "qk_prep_fwd")
    o, lse = _attn_fwd(q, k2, v2, sink_tab, "attn_fwd")
    x_mid = _mm(o, p["c_w_o"], "nn", res=x, name="attn_out", tk=2048)
    return x_mid, dict(h=h, qkv=qkv, q=q, k2=k2, v2=v2, o=o, lse=lse)


def _attn_backward(dx_mid, x, p, positions, sv, grads):
    n_q, n_kv, tabs, gq2, gk2, sink_tab = _attn_tables(p, positions, x.shape[1])
    dwo = _mm(sv["o"], dx_mid, "tn", out_dtype=BF16, name="attn_out_dw", tk=2048)
    grads["c_w_o"] = dwo.reshape(N_DEV, dwo.shape[0] // N_DEV, dwo.shape[1])
    do = _mm(dx_mid, p["c_w_o"], "nt", name="attn_out_dx", tk=2048)
    dq, dkc, dkp, dvc, dvp, dsk = _attn_bwd(do, sv["q"], sv["o"], sv["lse"], sv["k2"], sv["v2"], sink_tab, "attn_bwd")
    nkv_, nb = dsk.shape[0], dsk.shape[1]
    dsk = dsk.reshape(nkv_, nb, -1, 8, LANES)[:, :, :, 0, :].sum(axis=1).reshape(-1, LANES)
    grads["c_sinks"] = jnp.stack([dsk[:, 0], dsk[:, HEAD]], axis=1).reshape(1, -1)
    dqkv, dgq, dgk = _qk_prep_bwd(dq, dkc, dkp, dvc, dvp, sv["qkv"], tabs, gq2, gk2, n_q, n_kv, "qk_prep_bwd")
    grads["c_q_norm_g"] = dgq[:, :HEAD] + dgq[:, HEAD:]
    grads["c_k_norm_g"] = dgk[:, :HEAD] + dgk[:, HEAD:]
    dh = _mm(dqkv, p["c_w_qkv"], "nt", name="attn_qkv_dx", tk=1280)
    dwq = _mm(sv["h"], dqkv, "tn", out_dtype=BF16, name="attn_qkv_dw", tn=1280, tk=2048)
    grads["c_w_qkv"] = _stack_cols(dwq)
    dx, dg, _ = _rms_bwd(dh, x, _row(p["norm_g"]), dx_mid, "rms_bwd")
    grads["norm_g"] = dg
    return dx


class _LayerWeights:
    def __init__(self, li, weights, small_full, comm):
        self.li, self.weights, self.small_full, self.comm, self.cache = li, weights, small_full, comm, {}

    def __getitem__(self, nme):
        if nme not in self.cache:
            self.cache[nme] = self.fetch(nme)
        return self.cache[nme]

    def fetch(self, nme):
        full = "l%d_%s" % (self.li, nme)
        w = self.weights[full]
        if nme in _SHARDED_SMALL:
            return _unstack_cols(self.small_full[full], w.shape[0])
        if nme not in _BIG:
            return w
        got = self.comm.need(full)
        if nme in ("a_w_in", "ffn_w_up"):
            return got
        if nme == "c_w_qkv":
            return _unstack_cols(got, w.shape[0])
        if nme == "b_w_group":
            ng, gs, gd = w.shape
            return jnp.transpose(got.reshape(N_DEV, ng, gs, gd), (1, 0, 2, 3)).reshape(ng, N_DEV * gs, gd)
        return got.reshape(-1, w.shape[1])


class _LayerGrads(dict):
    def __init__(self, li, comm):
        super().__init__()
        self.li, self.comm = li, comm

    def __setitem__(self, nme, value):
        if nme in _BIG:
            self.comm.push("l%d_%s" % (self.li, nme), "scatter", value)
        else:
            super().__setitem__(nme, value)


def kernel(*args):
    n_w = len(_NAMES)
    x, positions = args[0], args[1]
    weights = dict(zip(_NAMES, args[2:2 + n_w]))
    loss_target = args[2 + n_w]
    moms = dict(zip(_NAMES, args[3 + n_w:3 + 2 * n_w]))
    vels = dict(zip(_NAMES, args[3 + 2 * n_w:3 + 3 * n_w]))
    x0 = x[0]
    pos = positions[0]
    kinds = ("conf", "pool", "attn", "conf")
    comm = _Comm()
    _STATE["comm"], _STATE["last"] = comm, None
    shd = [n for n in _NAMES if n.split("_", 1)[1] in _SHARDED_SMALL]
    small_full = dict(zip(shd, _all_gather([_pad_rows(weights[n]) for n in shd], "gather_small")))
    for n in _NAMES:
        if n.split("_", 1)[1] in _BIG:
            w = weights[n]
            comm.push(n, "gather1", w.astype(BF16).reshape(-1, w.shape[-1]))
    results = {}

    def update_ready():
        while comm.scattered:
            full, parts = comm.scattered.pop(0)
            w = weights[full]
            w2 = w.reshape(-1, w.shape[-1])
            outs = _sum_adam(parts, w2, moms[full].reshape(w2.shape), vels[full].reshape(w2.shape), "adam_" + full.split("_", 1)[1])
            results[full] = tuple(o.reshape(w.shape) for o in outs)

    params, saved = [], []
    cur = x0
    for li, names in enumerate(_LAYERS):
        comm.layer = li
        p = _LayerWeights(li, weights, small_full, comm)
        if kinds[li] == "conf":
            x_mid, sv = _conf_forward(cur, p)
        elif kinds[li] == "pool":
            x_mid, sv = _pool_forward(cur, p)
        else:
            x_mid, sv = _attn_forward(cur, p, pos)
        x_out, sv_f = _ffn_forward(x_mid, p, kinds[li])
        params.append(p)
        saved.append((sv, sv_f, cur, x_mid))
        cur = x_out
    dy, loss_part = _loss_head(cur, loss_target[0], "loss_head")
    loss = lax.psum(loss_part[0, 0], ("x", "y", "c"))

    small_grads = {}
    dcur = dy
    for li in range(len(_LAYERS) - 1, -1, -1):
        p = params[li]
        sv, sv_f, x_in, x_mid = saved[li]
        grads = _LayerGrads(li, comm)
        dmid = _ffn_backward(dcur, x_mid, p, sv_f, grads)
        update_ready()
        if kinds[li] == "conf":
            dcur = _conf_backward(dmid, x_in, p, sv, grads)
        elif kinds[li] == "pool":
            dcur = _pool_backward(dmid, x_in, p, sv, grads)
        else:
            dcur = _attn_backward(dmid, x_in, p, pos, sv, grads)
        update_ready()
        for n in _LAYERS[li]:
            if n not in _BIG:
                small_grads["l%d_%s" % (li, n)] = grads[n]
    rep = [n for n in _NAMES if n.split("_", 1)[1] not in _BIG and n.split("_", 1)[1] not in _SHARDED_SMALL]
    tot_rep, tot_sh = _small_exchange([small_grads[n] for n in rep], [small_grads[n] for n in shd], "small_exchange")
    while comm.queue or comm.scattered:
        if not comm.scattered:
            comm.flush(comm.take(1e9))
        update_ready()
    _STATE["comm"] = None
    rep_out, sh_out = _small_adam(tot_rep, tot_sh, [_row(weights[n]) for n in rep], [_row(moms[n]) for n in rep], [_row(vels[n]) for n in rep],
                                  [weights[n] for n in shd], [moms[n] for n in shd], [vels[n] for n in shd], "small_adam")
    for i, n in enumerate(rep):
        results[n] = tuple(rep_out[k][i].reshape(weights[n].shape) for k in range(4))
    for i, n in enumerate(shd):
        results[n] = tuple(sh_out[k][i] for k in range(4))

    _STATE["last"] = None
    grad_x = dcur[None]
    out = [loss, grad_x]
    for k in range(4):
        out += [results[n][k] for n in _NAMES]
    return tuple(out)
```

```python
import functools
import math

import jax
import jax.numpy as jnp
from jax import lax
from jax.experimental import pallas as pl
from jax.experimental.pallas import tpu as pltpu

F32 = jnp.float32
BF16 = jnp.bfloat16
N_DEV = 8
EPS = 1e-6
LANES = 128
HEAD = 64
Q_BLOCK = 128
ROT_DIM = 16
ROPE_THETA = 500000.0
POOL_WINDOWS = (2, 4, 8, 16)
HALO = 32
ROWS = 128
VMEM_LIMIT = 56 * 1024 * 1024
ADAM_LR, ADAM_B1, ADAM_B2, ADAM_EPS, ADAM_WD, ADAM_STEP = 0.001, 0.9, 0.999, 1e-08, 0.01, 10
MESH_ID = pl.DeviceIdType.MESH
MXU_FLOPS_PER_US = 7.5e8
HBM_BYTES_PER_US = 2.5e6
ATTN_US_PER_STEP = 1.1
ATTN_BLOCKS_PER_STEP = 1
CONV_FWD_US_PER_ELEM = 1.1e-5
CONV_BWD_US_PER_ELEM = 2.3e-5
ACT_FWD_US_PER_ELEM = 4.8e-6
QK_PREP_BWD_US_PER_ELEM = 1.2e-5
SMALL_EXCHANGE_CARRY_US = 100.0


def _make_call(body, **kw):
    return pl.pallas_call(body, **kw)


_STATE = {"comm": None, "last": None}


def _raw_call(body, **kw):
    call = _make_call(body, **kw)

    def run(*args):
        last = _STATE["last"]
        if last is not None and args:
            first, _ = lax.optimization_barrier((args[0], last))
            args = (first,) + tuple(args[1:])
        res = call(*args)
        _STATE["last"] = res[0] if isinstance(res, (list, tuple)) else res
        return res
    return run


def _pcall(body, carry_us=0.0, **kw):
    comm = _STATE["comm"]
    jobs = comm.take(carry_us) if (comm is not None and carry_us > 0) else []
    if not jobs:
        return _raw_call(body, **kw)
    return _carry(body, jobs, comm, kw)


def _params(sem=None, **kw):
    if sem is not None:
        kw["dimension_semantics"] = sem
    return pltpu.CompilerParams(vmem_limit_bytes=VMEM_LIMIT, **kw)


def _pick(dim, pref, mult=LANES):
    best = None
    d = mult
    while d <= min(dim, pref):
        if dim % d == 0:
            best = d
        d += mult
    return dim if best is None else best


def _sigmoid(x):
    return 1.0 / (1.0 + jnp.exp(-x))


def _fold8(p):
    r, c = p.shape
    return p.reshape(r // 8, 8, c).sum(axis=0)


def _window(win_ref, e):
    win_ref[...] = e
    return win_ref


def _rows(win_ref, k, r):
    return win_ref[k:k + r, :]


def _lshape(a):
    return a.shape if a.ndim == 2 else (a.shape[1], a.shape[0] * a.shape[2])


def _panel(a):
    return a.shape[1] if a.ndim == 2 else a.shape[2]


def _lspec(a, br, bc, rc):
    if a.ndim == 2:
        return pl.BlockSpec((br, bc), rc)
    per = a.shape[2] // bc

    def idx(*g):
        r, c = rc(*g)
        return (c // per, r, c % per)
    return pl.BlockSpec((None, br, bc), idx)


def _mm(a, b, dims, *, name, out_dtype=F32, out_stack=None, bias=None, res=None, tm=1024, tn=1024, tk=1024, panels=1):
    (ar, ac), (br_, bc_) = _lshape(a), _lshape(b)
    if dims == "nn":
        m, k, n = ar, ac, bc_
        lim_m, lim_k, lim_n = m, min(_panel(a), k), _panel(b)
    elif dims == "nt":
        m, k, n = ar, ac, br_
        lim_m, lim_k, lim_n = m, math.gcd(_panel(a), _panel(b)), n
    else:
        m, k, n = ac, ar, bc_
        lim_m, lim_k, lim_n = _panel(a), k, _panel(b)
    if out_stack is not None:
        lim_n = math.gcd(lim_n, n // out_stack)
    sub = 16 if (out_dtype == BF16 or a.dtype == BF16) else 8
    tm = _pick(lim_m, tm, LANES if dims == "tn" else sub)
    tn = _pick(lim_n, tn)
    tk = _pick(lim_k, tk, sub if dims == "tn" else LANES)
    if panels > 1:
        assert dims == "nt" and b.ndim == 3 and tk == b.shape[2] and b.shape[0] % panels == 0 and _panel(a) % (panels * tk) == 0
    nk = k // (tk * panels)
    if dims == "tn":
        a_spec = _lspec(a, tk, tm, lambda i, j, kk: (kk, i))
    else:
        a_spec = _lspec(a, tm, tk * panels, lambda i, j, kk: (i, kk))
    if panels > 1:
        b_spec = pl.BlockSpec((panels, tn, tk), lambda i, j, kk: (kk, j, 0))
    elif dims == "nt":
        b_spec = _lspec(b, tn, tk, lambda i, j, kk: (j, kk))
    else:
        b_spec = _lspec(b, tk, tn, lambda i, j, kk: (kk, j))
    contract = {"nn": ((1,), (0,)), "nt": ((1,), (1,)), "tn": ((0,), (0,))}[dims]
    in_specs, args = [a_spec, b_spec], [a, b]
    if bias is not None:
        in_specs.append(pl.BlockSpec((1, tn), lambda i, j, kk: (0, j)))
        args.append(bias)
    if res is not None:
        in_specs.append(pl.BlockSpec((tm, tn), lambda i, j, kk: (i, j)))
        args.append(res)
    if out_stack is None:
        out_shape = jax.ShapeDtypeStruct((m, n), out_dtype)
    else:
        out_shape = jax.ShapeDtypeStruct((out_stack, m, n // out_stack), out_dtype)
    o_spec = _lspec(out_shape, tm, tn, lambda i, j, kk: (i, j))
    has_bias, has_res = bias is not None, res is not None

    def body(*refs):
        a_ref, b_ref = refs[0], refs[1]
        pos = 2
        bias_ref = res_ref = None
        if has_bias:
            bias_ref = refs[pos]
            pos += 1
        if has_res:
            res_ref = refs[pos]
            pos += 1
        o_ref = refs[pos]

        def part():
            if panels == 1:
                return lax.dot_general(a_ref[...].astype(BF16), b_ref[...].astype(BF16), (contract, ((), ())),
                                       preferred_element_type=F32)
            r = None
            for q in range(panels):
                t_ = lax.dot_general(a_ref[:, q * tk:(q + 1) * tk].astype(BF16), b_ref[q].astype(BF16), (contract, ((), ())),
                                     preferred_element_type=F32)
                r = t_ if r is None else r + t_
            return r

        def finish(r):
            if has_bias:
                r = r + bias_ref[...]
            if has_res:
                r = r + res_ref[...]
            o_ref[...] = r.astype(out_dtype)

        if nk == 1:
            finish(part())
        else:
            acc = refs[pos + 1]
            kk = pl.program_id(2)

            @pl.when(kk == 0)
            def _():
                acc[...] = part()

            @pl.when(kk > 0)
            def _():
                acc[...] += part()

            @pl.when(kk == nk - 1)
            def _():
                finish(acc[...])

    scratch = [] if nk == 1 else [pltpu.VMEM((tm, tn), F32)]
    return _pcall(body, carry_us=2.0 * m * n * k / MXU_FLOPS_PER_US, grid=(m // tm, n // tn, nk), in_specs=in_specs, out_specs=o_spec, out_shape=out_shape,
                  scratch_shapes=scratch, compiler_params=_params(("parallel", "parallel", "arbitrary")), name=name)(*args)


def _rms_fwd(x, g, out_dtype, name):
    t, d = x.shape
    tm = _pick(t, 512, 16)

    def body(x_ref, g_ref, o_ref):
        xv = x_ref[...]
        r = lax.rsqrt(jnp.mean(xv * xv, axis=-1, keepdims=True) + EPS)
        o_ref[...] = ((xv * r) * g_ref[...]).astype(out_dtype)

    return _pcall(body, carry_us=6.0 * t * d / HBM_BYTES_PER_US, grid=(t // tm,),
                  in_specs=[pl.BlockSpec((tm, d), lambda i: (i, 0)), pl.BlockSpec((1, d), lambda i: (0, 0))],
                  out_specs=pl.BlockSpec((tm, d), lambda i: (i, 0)), out_shape=jax.ShapeDtypeStruct((t, d), out_dtype),
                  compiler_params=_params(("parallel",)), name=name)(x, g)


def _rms_bwd(dh, x, g, dres, name):
    t, d = x.shape
    tm = _pick(t, 256, 8)

    def body(dh_ref, x_ref, g_ref, dres_ref, dx_ref, dg_ref, cs_ref):
        xv, dhv, dr = x_ref[...], dh_ref[...], dres_ref[...]
        r = lax.rsqrt(jnp.mean(xv * xv, axis=-1, keepdims=True) + EPS)
        xh = xv * r
        dxh = dhv * g_ref[...]
        dx_ref[...] = dr + r * (dxh - xh * jnp.mean(dxh * xh, axis=-1, keepdims=True))
        pg = jnp.sum(dhv * xh, axis=0, keepdims=True)
        pc = jnp.sum(dr, axis=0, keepdims=True)

        @pl.when(pl.program_id(0) == 0)
        def _():
            dg_ref[...] = pg
            cs_ref[...] = pc

        @pl.when(pl.program_id(0) > 0)
        def _():
            dg_ref[...] += pg
            cs_ref[...] += pc

    row = pl.BlockSpec((tm, d), lambda i: (i, 0))
    vec = pl.BlockSpec((1, d), lambda i: (0, 0))
    return _pcall(body, carry_us=16.0 * t * d / HBM_BYTES_PER_US, grid=(t // tm,), in_specs=[row, row, vec, row], out_specs=[row, vec, vec],
                  out_shape=[jax.ShapeDtypeStruct((t, d), F32), jax.ShapeDtypeStruct((1, d), F32), jax.ShapeDtypeStruct((1, d), F32)],
                  compiler_params=_params(("arbitrary",)), name=name)(dh, x, g, dres)


def _loss_head(y, target, name):
    t, d = y.shape
    tm = _pick(t, 512, 8)

    def body(y_ref, t_ref, dy_ref, l_ref):
        e = y_ref[...] - t_ref[...]
        dy_ref[...] = e * (1.0 / d)
        part = 0.5 * jnp.sum(jnp.mean(e * e, axis=-1, keepdims=True), axis=0, keepdims=True)

        @pl.when(pl.program_id(0) == 0)
        def _():
            l_ref[...] = part

        @pl.when(pl.program_id(0) > 0)
        def _():
            l_ref[...] += part

    row = pl.BlockSpec((tm, d), lambda i: (i, 0))
    return _pcall(body, grid=(t // tm,), in_specs=[row, row], out_specs=[row, pl.BlockSpec((1, 1), lambda i: (0, 0))],
                  out_shape=[jax.ShapeDtypeStruct((t, d), F32), jax.ShapeDtypeStruct((1, 1), F32)],
                  compiler_params=_params(("arbitrary",)), name=name)(y, target)


def _ln_silu_fwd(c, g, b, name):
    t, d = c.shape
    tm = _pick(t, 512, 16)

    def body(c_ref, g_ref, b_ref, o_ref):
        cv = c_ref[...]
        xc = cv - jnp.mean(cv, axis=-1, keepdims=True)
        z = xc * lax.rsqrt(jnp.mean(xc * xc, axis=-1, keepdims=True) + EPS) * g_ref[...] + b_ref[...]
        o_ref[...] = (z * _sigmoid(z)).astype(BF16)

    row = pl.BlockSpec((tm, d), lambda i: (i, 0))
    vec = pl.BlockSpec((1, d), lambda i: (0, 0))
    return _pcall(body, carry_us=7.0 * t * d / HBM_BYTES_PER_US, grid=(t // tm,), in_specs=[row, vec, vec], out_specs=row,
                  out_shape=jax.ShapeDtypeStruct((t, d), BF16),
                  compiler_params=_params(("parallel",)), name=name)(c, g, b)


def _ln_silu_bwd(ds, c, g, b, name):
    t, d = c.shape
    tm = _pick(t, 256, 8)

    def body(ds_ref, c_ref, g_ref, b_ref, dc_ref, dg_ref, db_ref):
        cv = c_ref[...]
        xc = cv - jnp.mean(cv, axis=-1, keepdims=True)
        r = lax.rsqrt(jnp.mean(xc * xc, axis=-1, keepdims=True) + EPS)
        ch = xc * r
        z = ch * g_ref[...] + b_ref[...]
        sg = _sigmoid(z)
        dz = ds_ref[...] * (sg * (1.0 + z * (1.0 - sg)))
        dch = dz * g_ref[...]
        dc_ref[...] = r * (dch - jnp.mean(dch, axis=-1, keepdims=True) - ch * jnp.mean(dch * ch, axis=-1, keepdims=True))
        pg = jnp.sum(dz * ch, axis=0, keepdims=True)
        pb = jnp.sum(dz, axis=0, keepdims=True)

        @pl.when(pl.program_id(0) == 0)
        def _():
            dg_ref[...] = pg
            db_ref[...] = pb

        @pl.when(pl.program_id(0) > 0)
        def _():
            dg_ref[...] += pg
            db_ref[...] += pb

    row = pl.BlockSpec((tm, d), lambda i: (i, 0))
    vec = pl.BlockSpec((1, d), lambda i: (0, 0))
    return _pcall(body, grid=(t // tm,), in_specs=[row, row, vec, vec], out_specs=[row, vec, vec],
                  out_shape=[jax.ShapeDtypeStruct((t, d), F32), jax.ShapeDtypeStruct((1, d), F32), jax.ShapeDtypeStruct((1, d), F32)],
                  compiler_params=_params(("arbitrary",)), name=name)(ds, c, g, b)


def _steps(t):
    return t // ROWS


def _conf_conv_fwd(u, dw_w, dw_b, name):
    t, d2 = u.shape
    d = d2 // 2
    c = LANES
    ns = d // c
    kc = dw_w.shape[0]

    def body(a_ref, g_ref, w_ref, b_ref, o_ref, pad, win):
        pad[0:HALO, :] = jnp.zeros((HALO, c), F32)

        def glu(i, _):
            base = pl.multiple_of(i * ROWS, ROWS)
            pad[pl.ds(base + HALO, ROWS), :] = a_ref[pl.ds(base, ROWS), :] * _sigmoid(g_ref[pl.ds(base, ROWS), :])
            return 0
        lax.fori_loop(0, _steps(t), glu, 0)

        def conv(i, _):
            base = pl.multiple_of(i * ROWS, ROWS)
            e = _window(win, pad[pl.ds(base, ROWS + HALO), :])
            acc = jnp.zeros((ROWS, c), F32) + b_ref[...]
            for j in range(kc):
                acc = acc + w_ref[j:j + 1, :] * _rows(e, HALO - (kc - 1) + j, ROWS)
            o_ref[pl.ds(base, ROWS), :] = acc
            return 0
        lax.fori_loop(0, _steps(t), conv, 0)

    return _pcall(body, carry_us=CONV_FWD_US_PER_ELEM * t * d, grid=(ns,),
                  in_specs=[pl.BlockSpec((t, c), lambda s: (0, s)), pl.BlockSpec((t, c), lambda s: (0, s + ns)),
                            pl.BlockSpec((kc, c), lambda s: (0, s)), pl.BlockSpec((1, c), lambda s: (0, s))],
                  out_specs=pl.BlockSpec((t, c), lambda s: (0, s)), out_shape=jax.ShapeDtypeStruct((t, d), F32),
                  scratch_shapes=[pltpu.VMEM((t + HALO, c), F32), pltpu.VMEM((ROWS + HALO, c), F32)],
                  compiler_params=_params(("parallel",)), name=name)(u, u, dw_w, dw_b)


def _conf_conv_bwd(dc, u, dw_w, name):
    t, d = dc.shape
    c = LANES
    ns = d // c
    kc = dw_w.shape[0]

    def body(dc_ref, a_ref, g_ref, w_ref, du_ref, dww_ref, dwb_ref, db_ref, padv, padd, accw, accb, winv, wind):
        padv[0:HALO, :] = jnp.zeros((HALO, c), F32)
        padd[t:t + HALO, :] = jnp.zeros((HALO, c), F32)
        accw[...] = jnp.zeros_like(accw)
        accb[...] = jnp.zeros_like(accb)

        def fill(i, _):
            base = pl.multiple_of(i * ROWS, ROWS)
            padv[pl.ds(base + HALO, ROWS), :] = a_ref[pl.ds(base, ROWS), :] * _sigmoid(g_ref[pl.ds(base, ROWS), :])
            padd[pl.ds(base, ROWS), :] = dc_ref[pl.ds(base, ROWS), :]
            return 0
        lax.fori_loop(0, _steps(t), fill, 0)

        def step(i, _):
            base = pl.multiple_of(i * ROWS, ROWS)
            ev = _window(winv, padv[pl.ds(base, ROWS + HALO), :])
            ed = _window(wind, padd[pl.ds(base, ROWS + HALO), :])
            dcc = _rows(ed, 0, ROWS)
            dv = jnp.zeros((ROWS, c), F32)
            for j in range(kc):
                dv = dv + w_ref[j:j + 1, :] * _rows(ed, kc - 1 - j, ROWS)
                accw[j] = accw[j] + _fold8(dcc * _rows(ev, HALO - (kc - 1) + j, ROWS))
            accb[0] = accb[0] + _fold8(dcc)
            av = a_ref[pl.ds(base, ROWS), :]
            sg = _sigmoid(g_ref[pl.ds(base, ROWS), :])
            da = dv * sg
            dg = dv * av * sg * (1.0 - sg)
            du_ref[0, pl.ds(base, ROWS), :] = da.astype(BF16)
            du_ref[1, pl.ds(base, ROWS), :] = dg.astype(BF16)
            accb[1] = accb[1] + _fold8(da)
            accb[2] = accb[2] + _fold8(dg)
            return 0
        lax.fori_loop(0, _steps(t), step, 0)
        for j in range(kc):
            dww_ref[j:j + 1, :] = jnp.sum(accw[j], axis=0, keepdims=True)
        dwb_ref[...] = jnp.sum(accb[0], axis=0, keepdims=True)
        db_ref[0] = jnp.sum(accb[1], axis=0, keepdims=True)
        db_ref[1] = jnp.sum(accb[2], axis=0, keepdims=True)

    return _pcall(body, carry_us=CONV_BWD_US_PER_ELEM * t * d, grid=(ns,),
                  in_specs=[pl.BlockSpec((t, c), lambda s: (0, s)), pl.BlockSpec((t, c), lambda s: (0, s)),
                            pl.BlockSpec((t, c), lambda s: (0, s + ns)), pl.BlockSpec((kc, c), lambda s: (0, s))],
                  out_specs=[pl.BlockSpec((2, t, c), lambda s: (0, 0, s)), pl.BlockSpec((kc, c), lambda s: (0, s)),
                             pl.BlockSpec((1, c), lambda s: (0, s)), pl.BlockSpec((2, 1, c), lambda s: (0, 0, s))],
                  out_shape=[jax.ShapeDtypeStruct((2, t, d), BF16), jax.ShapeDtypeStruct((kc, d), F32),
                             jax.ShapeDtypeStruct((1, d), F32), jax.ShapeDtypeStruct((2, 1, d), F32)],
                  scratch_shapes=[pltpu.VMEM((t + HALO, c), F32), pltpu.VMEM((t + HALO, c), F32),
                                  pltpu.VMEM((kc, 8, c), F32), pltpu.VMEM((3, 8, c), F32),
                                  pltpu.VMEM((ROWS + HALO, c), F32), pltpu.VMEM((ROWS + HALO, c), F32)],
                  compiler_params=_params(("parallel",)), name=name)(dc, u, u, dw_w)


def _ffn_act_fwd(u0, dw_w, dw_b, name):
    t, f2 = u0.shape
    f = f2 // 2
    c = LANES
    ns = f // c
    kw = dw_w.shape[0]

    def body(g_ref, v_ref, wg_ref, wv_ref, bg_ref, bv_ref, o_ref, wing, winv):
        def step(i, _):
            base = pl.multiple_of(i * ROWS, ROWS)
            lo = pl.multiple_of(jnp.maximum(base - HALO, 0), HALO)
            keep = jnp.where(i > 0, 1.0, 0.0)
            eg = _window(wing, jnp.concatenate([g_ref[pl.ds(lo, HALO), :] * keep, g_ref[pl.ds(base, ROWS), :]], axis=0))
            ev = _window(winv, jnp.concatenate([v_ref[pl.ds(lo, HALO), :] * keep, v_ref[pl.ds(base, ROWS), :]], axis=0))
            gate = jnp.zeros((ROWS, c), F32) + bg_ref[...]
            val = jnp.zeros((ROWS, c), F32) + bv_ref[...]
            for j in range(kw):
                gate = gate + wg_ref[j:j + 1, :] * _rows(eg, HALO - (kw - 1) + j, ROWS)
                val = val + wv_ref[j:j + 1, :] * _rows(ev, HALO - (kw - 1) + j, ROWS)
            o_ref[pl.ds(base, ROWS), :] = (gate * _sigmoid(gate) * val).astype(BF16)
            return 0
        lax.fori_loop(0, _steps(t), step, 0)

    return _pcall(body, carry_us=ACT_FWD_US_PER_ELEM * t * f, grid=(ns,),
                  in_specs=[pl.BlockSpec((t, c), lambda s: (0, s)), pl.BlockSpec((t, c), lambda s: (0, s + ns)),
                            pl.BlockSpec((kw, c), lambda s: (0, s)), pl.BlockSpec((kw, c), lambda s: (0, s + ns)),
                            pl.BlockSpec((1, c), lambda s: (0, s)), pl.BlockSpec((1, c), lambda s: (0, s + ns))],
                  out_specs=pl.BlockSpec((t, c), lambda s: (0, s)), out_shape=jax.ShapeDtypeStruct((t, f), BF16),
                  scratch_shapes=[pltpu.VMEM((ROWS + HALO, c), F32), pltpu.VMEM((ROWS + HALO, c), F32)],
                  compiler_params=_params(("parallel",)), name=name)(u0, u0, dw_w, dw_w, dw_b, dw_b)


def _ffn_act_bwd(da, u0, dw_w, dw_b, name):
    t, f = da.shape
    c = LANES
    ns = f // c
    kw = dw_w.shape[0]

    def body(da_ref, g_ref, v_ref, wg_ref, wv_ref, bg_ref, bv_ref, du_ref, dww_ref, dwb_ref, padg, padv, accw, accb, wing, winv):
        padg[t:t + HALO, :] = jnp.zeros((HALO, c), F32)
        padv[t:t + HALO, :] = jnp.zeros((HALO, c), F32)
        accw[...] = jnp.zeros_like(accw)
        accb[...] = jnp.zeros_like(accb)

        def first(i, _):
            base = pl.multiple_of(i * ROWS, ROWS)
            lo = pl.multiple_of(jnp.maximum(base - HALO, 0), HALO)
            keep = jnp.where(i > 0, 1.0, 0.0)
            eg = _window(wing, jnp.concatenate([g_ref[pl.ds(lo, HALO), :] * keep, g_ref[pl.ds(base, ROWS), :]], axis=0))
            ev = _window(winv, jnp.concatenate([v_ref[pl.ds(lo, HALO), :] * keep, v_ref[pl.ds(base, ROWS), :]], axis=0))
            gate = jnp.zeros((ROWS, c), F32) + bg_ref[...]
            val = jnp.zeros((ROWS, c), F32) + bv_ref[...]
            for j in range(kw):
                gate = gate + wg_ref[j:j + 1, :] * _rows(eg, HALO - (kw - 1) + j, ROWS)
                val = val + wv_ref[j:j + 1, :] * _rows(ev, HALO - (kw - 1) + j, ROWS)
            dav = da_ref[pl.ds(base, ROWS), :]
            sg = _sigmoid(gate)
            dgate = dav * val * (sg * (1.0 + gate * (1.0 - sg)))
            dval = dav * (gate * sg)
            padg[pl.ds(base, ROWS), :] = dgate
            padv[pl.ds(base, ROWS), :] = dval
            for j in range(kw):
                accw[j] = accw[j] + _fold8(dgate * _rows(eg, HALO - (kw - 1) + j, ROWS))
                accw[kw + j] = accw[kw + j] + _fold8(dval * _rows(ev, HALO - (kw - 1) + j, ROWS))
            accb[0] = accb[0] + _fold8(dgate)
            accb[1] = accb[1] + _fold8(dval)
            return 0
        lax.fori_loop(0, _steps(t), first, 0)

        def second(i, _):
            base = pl.multiple_of(i * ROWS, ROWS)
            eg = _window(wing, padg[pl.ds(base, ROWS + HALO), :])
            ev = _window(winv, padv[pl.ds(base, ROWS + HALO), :])
            dg = jnp.zeros((ROWS, c), F32)
            dv = jnp.zeros((ROWS, c), F32)
            for j in range(kw):
                dg = dg + wg_ref[j:j + 1, :] * _rows(eg, kw - 1 - j, ROWS)
                dv = dv + wv_ref[j:j + 1, :] * _rows(ev, kw - 1 - j, ROWS)
            du_ref[0, pl.ds(base, ROWS), :] = dg.astype(BF16)
            du_ref[1, pl.ds(base, ROWS), :] = dv.astype(BF16)
            return 0
        lax.fori_loop(0, _steps(t), second, 0)
        for j in range(kw):
            dww_ref[0, j:j + 1, :] = jnp.sum(accw[j], axis=0, keepdims=True)
            dww_ref[1, j:j + 1, :] = jnp.sum(accw[kw + j], axis=0, keepdims=True)
        dwb_ref[0] = jnp.sum(accb[0], axis=0, keepdims=True)
        dwb_ref[1] = jnp.sum(accb[1], axis=0, keepdims=True)

    return _pcall(body, grid=(ns,),
                  in_specs=[pl.BlockSpec((t, c), lambda s: (0, s)),
                            pl.BlockSpec((t, c), lambda s: (0, s)), pl.BlockSpec((t, c), lambda s: (0, s + ns)),
                            pl.BlockSpec((kw, c), lambda s: (0, s)), pl.BlockSpec((kw, c), lambda s: (0, s + ns)),
                            pl.BlockSpec((1, c), lambda s: (0, s)), pl.BlockSpec((1, c), lambda s: (0, s + ns))],
                  out_specs=[pl.BlockSpec((2, t, c), lambda s: (0, 0, s)), pl.BlockSpec((2, kw, c), lambda s: (0, 0, s)),
                             pl.BlockSpec((2, 1, c), lambda s: (0, 0, s))],
                  out_shape=[jax.ShapeDtypeStruct((2, t, f), BF16), jax.ShapeDtypeStruct((2, kw, f), F32),
                             jax.ShapeDtypeStruct((2, 1, f), F32)],
                  scratch_shapes=[pltpu.VMEM((t + HALO, c), F32), pltpu.VMEM((t + HALO, c), F32),
                                  pltpu.VMEM((2 * kw, 8, c), F32), pltpu.VMEM((2, 8, c), F32),
                                  pltpu.VMEM((ROWS + HALO, c), F32), pltpu.VMEM((ROWS + HALO, c), F32)],
                  compiler_params=_params(("parallel",)), name=name)(da, u0, u0, dw_w, dw_w, dw_b, dw_b)


def _window_of(group):
    w = jnp.float32(POOL_WINDOWS[-1])
    for k in range(len(POOL_WINDOWS) - 2, -1, -1):
        w = jnp.where(group == k, jnp.float32(POOL_WINDOWS[k]), w)
    return w


def _select_level(group, levels):
    out = levels[-1]
    for k in range(len(levels) - 2, -1, -1):
        out = jnp.where(group == k, levels[k], out)
    return out


def _pool_fwd(h, name):
    t, d = h.shape
    c = LANES
    per = d // len(POOL_WINDOWS) // c

    def body(h_ref, o_ref):
        group = pl.program_id(0)
        wf = _window_of(group)

        def step(i, _):
            base = pl.multiple_of(i * ROWS, ROWS)
            lo = pl.multiple_of(jnp.maximum(base - HALO, 0), HALO)
            keep = jnp.where(i > 0, 1.0, 0.0)
            cur = h_ref[pl.ds(base, ROWS), :]
            e = jnp.concatenate([h_ref[pl.ds(lo, HALO), :] * keep, cur], axis=0)
            n = ROWS + HALO
            levels = []
            s = e
            for k in range(len(POOL_WINDOWS)):
                s = s + pltpu.roll(s, 1 << k, 0)
                levels.append(s[HALO:n])
            tpos = (base + lax.broadcasted_iota(jnp.int32, (ROWS, c), 0) + 1).astype(F32)
            pooled = _select_level(group, levels) / jnp.minimum(tpos, wf)
            o_ref[pl.ds(base, ROWS), :] = (pooled - cur).astype(BF16)
            return 0
        lax.fori_loop(0, _steps(t), step, 0)

    return _pcall(body, grid=(len(POOL_WINDOWS), per), in_specs=[pl.BlockSpec((t, c), lambda g, s: (0, g * per + s))],
                  out_specs=pl.BlockSpec((t, c), lambda g, s: (0, g * per + s)), out_shape=jax.ShapeDtypeStruct((t, d), BF16),
                  compiler_params=_params(("parallel", "parallel")), name=name)(h)


def _pool_bwd(dm, name):
    t, d = dm.shape
    c = LANES
    per = d // len(POOL_WINDOWS) // c

    def body(dm_ref, o_ref, pad):
        group = pl.program_id(0)
        wf = _window_of(group)
        pad[t:t + HALO, :] = jnp.zeros((HALO, c), F32)

        def fill(i, _):
            base = pl.multiple_of(i * ROWS, ROWS)
            tpos = (base + lax.broadcasted_iota(jnp.int32, (ROWS, c), 0) + 1).astype(F32)
            pad[pl.ds(base, ROWS), :] = dm_ref[pl.ds(base, ROWS), :] / jnp.minimum(tpos, wf)
            return 0
        lax.fori_loop(0, _steps(t), fill, 0)

        def step(i, _):
            base = pl.multiple_of(i * ROWS, ROWS)
            n = ROWS + HALO
            s = pad[pl.ds(base, n), :]
            levels = []
            for k in range(len(POOL_WINDOWS)):
                s = s + pltpu.roll(s, n - (1 << k), 0)
                levels.append(s[0:ROWS])
            o_ref[pl.ds(base, ROWS), :] = _select_level(group, levels) - dm_ref[pl.ds(base, ROWS), :]
            return 0
        lax.fori_loop(0, _steps(t), step, 0)

    return _pcall(body, grid=(len(POOL_WINDOWS), per), in_specs=[pl.BlockSpec((t, c), lambda g, s: (0, g * per + s))],
                  out_specs=pl.BlockSpec((t, c), lambda g, s: (0, g * per + s)), out_shape=jax.ShapeDtypeStruct((t, d), F32),
                  scratch_shapes=[pltpu.VMEM((t + HALO, c), F32)], compiler_params=_params(("parallel", "parallel")), name=name)(dm)


def _pool_mm_fwd(mixed, wg, scale, res, name):
    t, d = mixed.shape
    ng, gd, _ = wg.shape
    tm = _pick(t, 1024, 16)

    def body(a_ref, w_ref, s_ref, r_ref, o_ref):
        y = jnp.dot(a_ref[...], w_ref[...], preferred_element_type=F32)
        o_ref[...] = r_ref[...] + y * s_ref[...]

    blk = pl.BlockSpec((tm, gd), lambda g, i: (i, g))
    return _pcall(body, grid=(ng, t // tm),
                  in_specs=[blk, pl.BlockSpec((None, gd, gd), lambda g, i: (g, 0, 0)), pl.BlockSpec((1, gd), lambda g, i: (0, g)), blk],
                  out_specs=blk, out_shape=jax.ShapeDtypeStruct((t, d), F32),
                  compiler_params=_params(("parallel", "parallel")), name=name)(mixed, wg, scale, res)


def _pool_mm_bwd(dy, mixed, wg, scale, name):
    t, d = mixed.shape
    ng, gd, _ = wg.shape
    tm = _pick(t, 1024, 16)

    def body(dy_ref, a_ref, w_ref, s_ref, dm_ref, dw_ref, ds_ref):
        a, w, dyv = a_ref[...], w_ref[...], dy_ref[...]
        y = jnp.dot(a, w, preferred_element_type=F32)
        dyp = (dyv * s_ref[...]).astype(BF16)
        dm_ref[...] = lax.dot_general(dyp, w, (((1,), (1,)), ((), ())), preferred_element_type=F32)
        pw = lax.dot_general(a, dyp, (((0,), (0,)), ((), ())), preferred_element_type=F32)
        ps = jnp.sum(dyv * y, axis=0, keepdims=True)

        @pl.when(pl.program_id(1) == 0)
        def _():
            dw_ref[...] = pw
            ds_ref[...] = ps

        @pl.when(pl.program_id(1) > 0)
        def _():
            dw_ref[...] += pw
            ds_ref[...] += ps

    blk = pl.BlockSpec((tm, gd), lambda g, i: (i, g))
    wsp = pl.BlockSpec((None, gd, gd), lambda g, i: (g, 0, 0))
    vec = pl.BlockSpec((1, gd), lambda g, i: (0, g))
    return _pcall(body, grid=(ng, t // tm), in_specs=[blk, blk, wsp, vec], out_specs=[blk, wsp, vec],
                  out_shape=[jax.ShapeDtypeStruct((t, d), F32), jax.ShapeDtypeStruct((ng, gd, gd), F32), jax.ShapeDtypeStruct((1, d), F32)],
                  compiler_params=_params(("parallel", "arbitrary")), name=name)(dy, mixed, wg, scale)


def _rope_tables(positions):
    half = ROT_DIM // 2
    inv_freq = ROPE_THETA ** (-jnp.arange(0, ROT_DIM, 2, dtype=F32) / ROT_DIM)
    ang = positions.astype(F32)[:, None] * inv_freq
    cos, sin = jnp.cos(ang), jnp.sin(ang)
    t = positions.shape[0]
    ones = jnp.ones((t, HEAD - ROT_DIM), F32)
    zeros = jnp.zeros((t, HEAD - ROT_DIM), F32)
    zh = jnp.zeros((t, half), F32)
    c = jnp.concatenate([cos, cos, ones], axis=1)
    s1 = jnp.concatenate([-sin, zh, zeros], axis=1)
    s2 = jnp.concatenate([zh, sin, zeros], axis=1)
    return tuple(jnp.concatenate([a, a], axis=1) for a in (c, s1, s2))


def _half_mean(v, lo):
    s_lo = jnp.sum(jnp.where(lo, v, 0.0), axis=-1, keepdims=True)
    s_hi = jnp.sum(jnp.where(lo, 0.0, v), axis=-1, keepdims=True)
    return jnp.where(lo, s_lo, s_hi) * (1.0 / HEAD)


def _qk_prep_fwd(qkv, tabs, gq2, gk2, n_q, n_kv, name):
    t, width = qkv.shape
    tm = _pick(t, 256, 16)
    nqc, nkc = n_q * HEAD // LANES, n_kv * HEAD // LANES

    def body(x_ref, c_ref, s1_ref, s2_ref, gq_ref, gk_ref, q_ref, k2_ref, v2_ref):
        lo = lax.broadcasted_iota(jnp.int32, (tm, LANES), 1) < HEAD
        cv, s1, s2 = c_ref[...], s1_ref[...], s2_ref[...]

        def normrot(xc, g2):
            y = xc * lax.rsqrt(_half_mean(xc * xc, lo) + EPS) * g2
            return y * cv + pltpu.roll(y, LANES - ROT_DIM // 2, 1) * s1 + pltpu.roll(y, ROT_DIM // 2, 1) * s2

        def twice(y, j):
            sw = pltpu.roll(y, HEAD, 1)
            k2 = jnp.where(lo, y, sw) if j == 0 else jnp.where(lo, sw, y)
            return k2.astype(BF16)

        for ch in range(nqc):
            q_ref[:, ch * LANES:(ch + 1) * LANES] = normrot(x_ref[:, ch * LANES:(ch + 1) * LANES], gq_ref[...]).astype(BF16)
        for ch in range(nkc):
            off = (nqc + ch) * LANES
            y = normrot(x_ref[:, off:off + LANES], gk_ref[...])
            voff = (nqc + nkc + ch) * LANES
            vv = x_ref[:, voff:voff + LANES]
            for j in range(2):
                k2_ref[:, (2 * ch + j) * LANES:(2 * ch + j + 1) * LANES] = twice(y, j)
                v2_ref[:, (2 * ch + j) * LANES:(2 * ch + j + 1) * LANES] = twice(vv, j)

    row = lambda w: pl.BlockSpec((tm, w), lambda i: (i, 0))
    vec = pl.BlockSpec((1, LANES), lambda i: (0, 0))
    return _pcall(body, grid=(t // tm,), in_specs=[row(width), row(LANES), row(LANES), row(LANES), vec, vec],
                  out_specs=[row(n_q * HEAD), row(n_kv * LANES), row(n_kv * LANES)],
                  out_shape=[jax.ShapeDtypeStruct((t, n_q * HEAD), BF16), jax.ShapeDtypeStruct((t, n_kv * LANES), BF16),
                             jax.ShapeDtypeStruct((t, n_kv * LANES), BF16)],
                  compiler_params=_params(("parallel",)), name=name)(qkv, *tabs, gq2, gk2)


def _qk_prep_bwd(dq, dk_cur, dk_prev, dv_cur, dv_prev, qkv, tabs, gq2, gk2, n_q, n_kv, name):
    t, width = qkv.shape
    tm = Q_BLOCK
    nb = t // tm
    nqc, nkc = n_q * HEAD // LANES, n_kv * HEAD // LANES

    def body(dq_ref, kc_ref, kp_ref, vc_ref, vp_ref, x_ref, c_ref, s1_ref, s2_ref, gq_ref, gk_ref, o_ref, dgq_ref, dgk_ref):
        lo = lax.broadcasted_iota(jnp.int32, (tm, LANES), 1) < HEAD
        cv, s1, s2 = c_ref[...], s1_ref[...], s2_ref[...]
        more = jnp.where(pl.program_id(0) < nb - 1, 1.0, 0.0)

        def back(dy, xc, g2):
            dyn = dy * cv + pltpu.roll(dy * s1, ROT_DIM // 2, 1) + pltpu.roll(dy * s2, LANES - ROT_DIM // 2, 1)
            r = lax.rsqrt(_half_mean(xc * xc, lo) + EPS)
            xh = xc * r
            dxh = dyn * g2
            return r * (dxh - xh * _half_mean(dxh * xh, lo)), jnp.sum(dyn * xh, axis=0, keepdims=True)

        def unfold(cur_ref, prev_ref, ch):
            d0 = cur_ref[:, (2 * ch) * LANES:(2 * ch + 1) * LANES] + more * prev_ref[:, (2 * ch) * LANES:(2 * ch + 1) * LANES]
            d1 = cur_ref[:, (2 * ch + 1) * LANES:(2 * ch + 2) * LANES] + more * prev_ref[:, (2 * ch + 1) * LANES:(2 * ch + 2) * LANES]
            return jnp.where(lo, d0 + pltpu.roll(d0, HEAD, 1), d1 + pltpu.roll(d1, HEAD, 1))

        pq = jnp.zeros((1, LANES), F32)
        for ch in range(nqc):
            sl = slice(ch * LANES, (ch + 1) * LANES)
            dx, pg = back(dq_ref[:, sl], x_ref[:, sl], gq_ref[...])
            o_ref[:, sl] = dx.astype(BF16)
            pq = pq + pg
        pk = jnp.zeros((1, LANES), F32)
        for ch in range(nkc):
            sl = slice((nqc + ch) * LANES, (nqc + ch + 1) * LANES)
            dx, pg = back(unfold(kc_ref, kp_ref, ch), x_ref[:, sl], gk_ref[...])
            o_ref[:, sl] = dx.astype(BF16)
            pk = pk + pg
            vs = slice((nqc + nkc + ch) * LANES, (nqc + nkc + ch + 1) * LANES)
            o_ref[:, vs] = unfold(vc_ref, vp_ref, ch).astype(BF16)

        @pl.when(pl.program_id(0) == 0)
        def _():
            dgq_ref[...] = pq
            dgk_ref[...] = pk

        @pl.when(pl.program_id(0) > 0)
        def _():
            dgq_ref[...] += pq
            dgk_ref[...] += pk

    row = lambda w: pl.BlockSpec((tm, w), lambda i: (i, 0))
    nxt = lambda w: pl.BlockSpec((tm, w), lambda i: (jnp.minimum(i + 1, nb - 1), 0))
    vec = pl.BlockSpec((1, LANES), lambda i: (0, 0))
    kvw = n_kv * LANES
    return _pcall(body, carry_us=QK_PREP_BWD_US_PER_ELEM * t * width, grid=(nb,),
                  in_specs=[row(n_q * HEAD), row(kvw), nxt(kvw), row(kvw), nxt(kvw), row(width), row(LANES), row(LANES), row(LANES), vec, vec],
                  out_specs=[row(width), vec, vec],
                  out_shape=[jax.ShapeDtypeStruct((t, width), BF16), jax.ShapeDtypeStruct((1, LANES), F32), jax.ShapeDtypeStruct((1, LANES), F32)],
                  compiler_params=_params(("arbitrary",)), name=name)(dq, dk_cur, dk_prev, dv_cur, dv_prev, qkv, *tabs, gq2, gk2)


def _band_scores(qh, kc, kp, n, sink_row, lo_row, is_lo):
    scale = 1.0 / math.sqrt(HEAD)
    nt = (((1,), (1,)), ((), ()))
    s_c = lax.dot_general(qh, kc, nt, preferred_element_type=F32) * scale
    s_p = lax.dot_general(qh, kp, nt, preferred_element_type=F32) * scale
    qi = lax.broadcasted_iota(jnp.int32, (Q_BLOCK, Q_BLOCK), 0)
    kj = lax.broadcasted_iota(jnp.int32, (Q_BLOCK, Q_BLOCK), 1)
    s_c = jnp.where(kj <= qi, s_c, -jnp.inf)
    s_p = jnp.where((kj > qi) & (n > 0), s_p, -jnp.inf)
    pick = lo_row if is_lo else jnp.logical_not(lo_row)
    sink = jnp.max(jnp.where(pick, sink_row, -jnp.inf), axis=-1, keepdims=True)
    return s_c, s_p, sink


def _stack_heads(q_ref, s_ref, per_kv, lo):
    lo_row = lax.broadcasted_iota(jnp.int32, (1, LANES), 1) < HEAD
    qs, sinks = [], []
    for cc in range(per_kv):
        qv = q_ref[:, cc * LANES:(cc + 1) * LANES].astype(F32)
        for is_lo in (True, False):
            qs.append(jnp.where(lo, qv, 0.0) if is_lo else jnp.where(lo, 0.0, qv))
            pick = lo_row if is_lo else jnp.logical_not(lo_row)
            sinks.append(jnp.zeros((Q_BLOCK, LANES), F32) + jnp.where(pick, s_ref[8 * cc:8 * cc + 1, :], -jnp.inf))
    return jnp.concatenate(qs, axis=0).astype(BF16), jnp.max(jnp.concatenate(sinks, axis=0), axis=-1, keepdims=True)


def _band_scores8(q8, kcat, n):
    rows = q8.shape[0]
    s = lax.dot_general(q8, kcat, (((1,), (1,)), ((), ())), preferred_element_type=F32) * (1.0 / math.sqrt(HEAD))
    qi = lax.broadcasted_iota(jnp.int32, (rows, 2 * Q_BLOCK), 0) & (Q_BLOCK - 1)
    kj = lax.broadcasted_iota(jnp.int32, (rows, 2 * Q_BLOCK), 1)
    valid = ((kj >= Q_BLOCK) & ((kj - Q_BLOCK) <= qi)) | ((kj < Q_BLOCK) & (kj > qi) & (n > 0))
    return jnp.where(valid, s, -jnp.inf)


def _attn_fwd(q, k2, v2, sink_tab, name):
    t, dq = q.shape
    nc = dq // LANES
    nb = t // Q_BLOCK
    nkv = k2.shape[1] // LANES
    per_kv = nc // nkv

    def body(q_ref, kc_ref, kp_ref, vc_ref, vp_ref, s_ref, o_ref, lse_ref):
        n = pl.program_id(1)
        lo = lax.broadcasted_iota(jnp.int32, (Q_BLOCK, LANES), 1) < HEAD
        q8, sink = _stack_heads(q_ref, s_ref, per_kv, lo)
        kcat = jnp.concatenate([kp_ref[...], kc_ref[...]], axis=0)
        vcat = jnp.concatenate([vp_ref[...], vc_ref[...]], axis=0)
        s = _band_scores8(q8, kcat, n)
        m = jnp.maximum(jnp.max(s, axis=-1, keepdims=True), sink)
        p = jnp.exp(s - m)
        denom = jnp.sum(p, axis=-1, keepdims=True) + jnp.exp(sink - m)
        o8 = jnp.dot(p.astype(BF16), vcat, preferred_element_type=F32) / denom
        lse8 = m + jnp.log(denom)
        for cc in range(per_kv):
            a, b = slice(2 * cc * Q_BLOCK, (2 * cc + 1) * Q_BLOCK), slice((2 * cc + 1) * Q_BLOCK, (2 * cc + 2) * Q_BLOCK)
            o_ref[:, cc * LANES:(cc + 1) * LANES] = jnp.where(lo, o8[a], o8[b]).astype(BF16)
            lse_ref[cc] = jnp.where(lo, lse8[a], lse8[b])

    QB = ATTN_BLOCKS_PER_STEP
    qs = pl.BlockSpec((QB * Q_BLOCK, per_kv * LANES), lambda k, n: (n, k))
    cur = pl.BlockSpec((QB * Q_BLOCK, LANES), lambda k, n: (n, k))
    prev = pl.BlockSpec((Q_BLOCK, LANES), lambda k, n: (jnp.maximum(n * QB - 1, 0), k))
    return _pcall(body, carry_us=ATTN_US_PER_STEP * nkv * nb, grid=(nkv, nb // QB),
                  in_specs=[qs, cur, prev, cur, prev, pl.BlockSpec((8 * per_kv, LANES), lambda k, n: (k, 0))],
                  out_specs=[qs, pl.BlockSpec((per_kv, QB * Q_BLOCK, LANES), lambda k, n: (k, n, 0))],
                  out_shape=[jax.ShapeDtypeStruct((t, dq), BF16), jax.ShapeDtypeStruct((nc, t, LANES), F32)],
                  compiler_params=_params(("parallel", "parallel")), name=name)(q, k2, k2, v2, v2, sink_tab)


def _attn_bwd(do, q, o, lse, k2, v2, sink_tab, name):
    t, dq = q.shape
    nc = dq // LANES
    nb = t // Q_BLOCK
    nkv = k2.shape[1] // LANES
    per_kv = nc // nkv
    scale = 1.0 / math.sqrt(HEAD)
    tn_ = (((0,), (0,)), ((), ()))
    nt = (((1,), (1,)), ((), ()))

    def body(do_ref, q_ref, o_ref, lse_ref, kc_ref, kp_ref, vc_ref, vp_ref, s_ref,
             dq_ref, dkc_ref, dkp_ref, dvc_ref, dvp_ref, dsk_ref):
        n = pl.program_id(1)
        lo = lax.broadcasted_iota(jnp.int32, (Q_BLOCK, LANES), 1) < HEAD
        lo_row = lax.broadcasted_iota(jnp.int32, (1, LANES), 1) < HEAD
        q8, sink = _stack_heads(q_ref, s_ref, per_kv, lo)
        kcat = jnp.concatenate([kp_ref[...], kc_ref[...]], axis=0)
        vcat = jnp.concatenate([vp_ref[...], vc_ref[...]], axis=0)
        dos, os_, lses = [], [], []
        for cc in range(per_kv):
            cols = slice(cc * LANES, (cc + 1) * LANES)
            dov, ov, lsev = do_ref[:, cols], o_ref[:, cols].astype(F32), lse_ref[cc]
            for is_lo in (True, False):
                half = lo if is_lo else jnp.logical_not(lo)
                dos.append(jnp.where(half, dov, 0.0))
                os_.append(ov)
                lses.append(jnp.where(half, lsev, -jnp.inf))
        do8f = jnp.concatenate(dos, axis=0)
        delta = jnp.sum(do8f * jnp.concatenate(os_, axis=0), axis=-1, keepdims=True)
        lse8 = jnp.max(jnp.concatenate(lses, axis=0), axis=-1, keepdims=True)
        do8 = do8f.astype(BF16)
        p = jnp.exp(_band_scores8(q8, kcat, n) - lse8)
        ds = (p * (lax.dot_general(do8, vcat, nt, preferred_element_type=F32) - delta)).astype(BF16)
        dq8 = jnp.dot(ds, kcat, preferred_element_type=F32) * scale
        dk = lax.dot_general(ds, q8, tn_, preferred_element_type=F32) * scale
        dv = lax.dot_general(p.astype(BF16), do8, tn_, preferred_element_type=F32)
        dsink = jnp.exp(sink - lse8) * delta
        for cc in range(per_kv):
            a, b = slice(2 * cc * Q_BLOCK, (2 * cc + 1) * Q_BLOCK), slice((2 * cc + 1) * Q_BLOCK, (2 * cc + 2) * Q_BLOCK)
            dq_ref[:, cc * LANES:(cc + 1) * LANES] = jnp.where(lo, dq8[a], dq8[b])
            d_lo = -jnp.sum(dsink[a], axis=0, keepdims=True)
            d_hi = -jnp.sum(dsink[b], axis=0, keepdims=True)
            dsk_ref[0, 8 * cc:8 * cc + 8, :] = jnp.zeros((8, LANES), F32) + jnp.where(lo_row, d_lo, d_hi)
        dkp_ref[...] = dk[0:Q_BLOCK]
        dkc_ref[...] = dk[Q_BLOCK:2 * Q_BLOCK]
        dvp_ref[...] = dv[0:Q_BLOCK]
        dvc_ref[...] = dv[Q_BLOCK:2 * Q_BLOCK]

    QB = ATTN_BLOCKS_PER_STEP
    qs = pl.BlockSpec((QB * Q_BLOCK, per_kv * LANES), lambda k, n: (n, k))
    cur = pl.BlockSpec((QB * Q_BLOCK, LANES), lambda k, n: (n, k))
    prev = pl.BlockSpec((Q_BLOCK, LANES), lambda k, n: (jnp.maximum(n * QB - 1, 0), k))
    kv_shape = jax.ShapeDtypeStruct((t, nkv * LANES), F32)
    return _pcall(body, carry_us=ATTN_US_PER_STEP * nkv * nb, grid=(nkv, nb // QB),
                  in_specs=[qs, qs, qs, pl.BlockSpec((per_kv, QB * Q_BLOCK, LANES), lambda k, n: (k, n, 0)),
                            cur, prev, cur, prev, pl.BlockSpec((8 * per_kv, LANES), lambda k, n: (k, 0))],
                  out_specs=[qs, cur, cur, cur, cur, pl.BlockSpec((None, QB, 8 * per_kv, LANES), lambda k, n: (k, n, 0, 0))],
                  out_shape=[jax.ShapeDtypeStruct((t, dq), F32), kv_shape, kv_shape, kv_shape, kv_shape,
                             jax.ShapeDtypeStruct((nkv, nb, 8 * per_kv, LANES), F32)],
                  compiler_params=_params(("parallel", "parallel")), name=name)(do, q, o, lse, k2, k2, v2, v2, sink_tab)


def _peer(k):
    x, y, c = lax.axis_index("x"), lax.axis_index("y"), lax.axis_index("c")
    flip = lambda v, bit: 1 - v if bit else v
    return (flip(x, k & 4), flip(y, k & 2), flip(c, k & 1))


def _my_index():
    return 4 * lax.axis_index("x") + 2 * lax.axis_index("y") + lax.axis_index("c")


def _peer_index(k):
    px, py, pc = _peer(k)
    return 4 * px + 2 * py + pc


def _all_gather(shards, name):
    n = len(shards)
    any_spec = pl.BlockSpec(memory_space=pl.ANY)

    def body(*refs):
        ins, outs = refs[:n], refs[n:2 * n]
        send_sems, recv_sems, local_sems = refs[2 * n:]
        me = _my_index()
        local = [pltpu.make_async_copy(ins[a], outs[a].at[me], local_sems.at[a]) for a in range(n)]
        for cp in local:
            cp.start()
        sends = []
        for k in range(1, N_DEV):
            for a in range(n):
                cp = pltpu.make_async_remote_copy(src_ref=ins[a], dst_ref=outs[a].at[me], send_sem=send_sems.at[a, k - 1],
                                                  recv_sem=recv_sems.at[a, k - 1], device_id=_peer(k), device_id_type=MESH_ID)
                cp.start()
                sends.append(cp)
        for k in range(1, N_DEV):
            for a in range(n):
                pltpu.make_async_remote_copy(src_ref=ins[a], dst_ref=outs[a].at[_peer_index(k)], send_sem=send_sems.at[a, k - 1],
                                             recv_sem=recv_sems.at[a, k - 1], device_id=_peer(k), device_id_type=MESH_ID).wait_recv()
        for cp in sends:
            cp.wait_send()
        for cp in local:
            cp.wait()

    return _pcall(body, in_specs=[any_spec] * n, out_specs=[any_spec] * n,
                  out_shape=[jax.ShapeDtypeStruct((N_DEV,) + s.shape, s.dtype) for s in shards],
                  scratch_shapes=[pltpu.SemaphoreType.DMA((n, N_DEV - 1)), pltpu.SemaphoreType.DMA((n, N_DEV - 1)),
                                  pltpu.SemaphoreType.DMA((n,))],
                  name=name)(*shards)


GATHER1_PEERS = (1, 2, 4, 6)
GATHER2_PEERS = (2, 4, 6)
SCATTER_PEERS = tuple(range(1, N_DEV))
MAX_SEMS = N_DEV - 1
MAX_JOBS = 6
US_PER_MB = {"gather1": 5.4, "gather2": 0.6, "scatter": 10.8}
SCATTER_PIECE_US = 110.0
PIECE_US = {"gather1": 62.0, "gather2": 1e9, "scatter": SCATTER_PIECE_US}


class _Job:
    def __init__(self, key, kind, src, lo=0, hi=None, dst=None):
        self.key, self.kind, self.src, self.dst = key, kind, src, dst
        shape = src.shape if kind != "gather1" else (N_DEV,) + src.shape
        self.out_shape = jax.ShapeDtypeStruct(shape, src.dtype)
        self.rows = shape[1]
        self.lo, self.hi = lo, self.rows if hi is None else hi
        self.row_us = US_PER_MB[kind] * math.prod(shape) * src.dtype.itemsize / 1e6 / self.rows
        pieces = max(1, round(self.row_us * self.rows / PIECE_US[kind]))
        while pieces > 1 and self.rows % (16 * pieces):
            pieces -= 1
        self.piece = self.rows // pieces

    @property
    def cost_us(self):
        return self.row_us * (self.hi - self.lo)


class _Comm:
    def __init__(self):
        self.queue, self.gathered, self.scattered, self.layer = [], {}, [], 0

    def push(self, key, kind, src):
        self.queue.append(_Job(key, kind, src))

    def take(self, budget_us, upto=None):
        jobs = [j for j in self.queue if j.kind == "gather2"][:MAX_JOBS]
        used = sum(j.cost_us for j in jobs)
        if upto is not None and any(j.key == upto for j in jobs):
            self.queue = [j for j in self.queue if j not in jobs]
            return jobs
        for j in [j for j in self.queue if j.kind != "gather2"]:
            if len(jobs) >= MAX_JOBS:
                break
            urgent = j.kind == "gather1" and int(j.key[1]) <= self.layer
            piece_us = j.row_us * j.piece
            n = 0
            while j.lo + (n + 1) * j.piece <= j.hi and ((used < budget_us) if urgent else (used + 0.5 * piece_us <= budget_us)):
                n += 1
                used += piece_us
            if n == 0:
                break
            part = _Job(j.key, j.kind, j.src, j.lo, j.lo + n * j.piece, j.dst)
            part.parent = j
            j.lo = part.hi
            jobs.append(part)
            if j.lo < j.hi or j.key == upto:
                break
        self.queue = [j for j in self.queue if j not in jobs and j.lo < j.hi]
        return jobs

    def finish(self, job, result):
        if job.kind == "gather2":
            self.gathered[job.key] = result
        elif job.hi < job.rows:
            job.parent.dst = result
        elif job.kind == "gather1":
            self.queue.insert(0, _Job(job.key, "gather2", result))
        else:
            self.scattered.append((job.key, result))

    def need(self, key):
        while key not in self.gathered:
            assert any(j.key == key for j in self.queue), key
            self.flush(self.take(1e9, upto=key))
        return self.gathered[key]

    def flush(self, jobs):
        def body(o_ref):
            o_ref[...] = jnp.zeros_like(o_ref)
        _carry(body, jobs, self, dict(in_specs=[], out_specs=pl.BlockSpec(memory_space=pltpu.VMEM),
                                      out_shape=jax.ShapeDtypeStruct((8, LANES), F32), name="exchange"))()


def _job_copies(job, src, dst, send_sems, recv_sems, local_sem):
    me = _my_index()
    peers = {"gather1": GATHER1_PEERS, "gather2": GATHER2_PEERS, "scatter": SCATTER_PEERS}[job.kind]
    sends, recvs = [], []
    rows = pl.ds(job.lo, job.hi - job.lo)
    for i, k in enumerate(peers):
        if job.kind == "gather1":
            s_ref, d_ref, to, got = src.at[rows], dst.at[me, rows], _peer(k), dst.at[_peer_index(k), rows]
        elif job.kind == "gather2":
            s_ref, d_ref, to, got = src.at[_peer_index(k)], dst.at[_peer_index(k)], _peer(1), dst.at[_peer_index(k | 1)]
        else:
            s_ref, d_ref, to, got = src.at[_peer_index(k), rows], dst.at[me, rows], _peer(k), dst.at[_peer_index(k), rows]
        sends.append(pltpu.make_async_remote_copy(src_ref=s_ref, dst_ref=d_ref, send_sem=send_sems.at[i], recv_sem=recv_sems.at[i],
                                                  device_id=to, device_id_type=MESH_ID))
        recvs.append(pltpu.make_async_remote_copy(src_ref=s_ref, dst_ref=got, send_sem=send_sems.at[i], recv_sem=recv_sems.at[i],
                                                  device_id=to, device_id_type=MESH_ID))
    local = None
    if job.kind == "gather1":
        local = pltpu.make_async_copy(src.at[rows], dst.at[me, rows], local_sem)
    elif job.kind == "scatter":
        local = pltpu.make_async_copy(src.at[me, rows], dst.at[me, rows], local_sem)
    return sends, recvs, local


def _carry(body, jobs, comm, kw):
    kw = dict(kw)
    grid = tuple(kw.get("grid", ()))
    in_specs = list(kw["in_specs"])
    single = not isinstance(kw["out_specs"], (list, tuple))
    out_specs = [kw["out_specs"]] if single else list(kw["out_specs"])
    out_shape = [kw["out_shape"]] if single else list(kw["out_shape"])
    scratch = list(kw.get("scratch_shapes", []))
    n_in, n_out, n_scr, nj = len(in_specs), len(out_specs), len(scratch), len(jobs)
    any_spec = pl.BlockSpec(memory_space=pl.ANY)
    landed = [a for a, job in enumerate(jobs) if job.dst is not None]
    n_land = len(landed)

    def wrapped(*refs):
        pos = 0

        def take(k):
            nonlocal pos
            part = refs[pos:pos + k]
            pos += k
            return part
        ins, rin, _, outs, rout, scr = take(n_in), take(nj), take(n_land), take(n_out), take(nj), take(n_scr)
        send_sems, recv_sems, local_sems = take(3)

        def copies():
            return [_job_copies(job, rin[a], rout[a], send_sems.at[a], recv_sems.at[a], local_sems.at[a]) for a, job in enumerate(jobs)]

        def start():
            for sends, _, local in copies():
                if local is not None:
                    local.start()
                for cp in sends:
                    cp.start()

        def finish():
            for sends, recvs, local in copies():
                for cp in recvs:
                    cp.wait_recv()
                for cp in sends:
                    cp.wait_send()
                if local is not None:
                    local.wait()

        if grid:
            first = functools.reduce(jnp.logical_and, [pl.program_id(a) == 0 for a in range(len(grid))])
            last = functools.reduce(jnp.logical_and, [pl.program_id(a) == grid[a] - 1 for a in range(len(grid))])
            pl.when(first)(start)
            body(*ins, *outs, *scr)
            pl.when(last)(finish)
        else:
            start()
            body(*ins, *outs, *scr)
            finish()

    aliases = {n_in + a: n_out + a for a, job in enumerate(jobs) if job.kind == "gather2"}
    aliases.update({n_in + nj + i: n_out + a for i, a in enumerate(landed)})
    extra = dict(dimension_semantics=("arbitrary",) * len(grid)) if grid else {}
    call = _raw_call(wrapped, in_specs=in_specs + [any_spec] * (nj + n_land), out_specs=out_specs + [any_spec] * nj,
                     out_shape=out_shape + [job.out_shape for job in jobs],
                     scratch_shapes=scratch + [pltpu.SemaphoreType.DMA((nj, MAX_SEMS)), pltpu.SemaphoreType.DMA((nj, MAX_SEMS)),
                                               pltpu.SemaphoreType.DMA((nj,))],
                     input_output_aliases=aliases, compiler_params=_params(**extra), name=kw["name"],
                     **({"grid": grid} if grid else {}))

    def run(*args):
        res = call(*args, *[job.src for job in jobs], *[jobs[a].dst for a in landed])
        for job, r in zip(jobs, res[n_out:]):
            comm.finish(job, r)
        return res[0] if single else list(res[:n_out])
    return run


def _adam(g, w, m, v):
    m2 = ADAM_B1 * m + (1.0 - ADAM_B1) * g
    v2 = ADAM_B2 * v + (1.0 - ADAM_B2) * (g * g)
    m_hat = m2 / (1.0 - ADAM_B1 ** ADAM_STEP)
    v_hat = v2 / (1.0 - ADAM_B2 ** ADAM_STEP)
    delta = -ADAM_LR * (m_hat / (jnp.sqrt(v_hat) + ADAM_EPS) + ADAM_WD * w)
    return delta, m2, v2


def _sum_adam(parts, w, m, v, name):
    r, c = w.shape
    tr = _pick(r, max(8, (1 << 19) // c), 8)

    def body(p_ref, w_ref, m_ref, v_ref, g_ref, d_ref, m2_ref, v2_ref):
        g = p_ref[0].astype(F32)
        for j in range(1, N_DEV):
            g = g + p_ref[j].astype(F32)
        delta, m2, v2 = _adam(g, w_ref[...], m_ref[...], v_ref[...])
        g_ref[...] = g
        d_ref[...] = delta
        m2_ref[...] = m2
        v2_ref[...] = v2

    blk = pl.BlockSpec((tr, c), lambda i: (i, 0))
    shp = jax.ShapeDtypeStruct((r, c), F32)
    return _pcall(body, grid=(r // tr,),
                  in_specs=[pl.BlockSpec((N_DEV, tr, c), lambda i: (0, i, 0)), blk, blk, blk],
                  out_specs=[blk] * 4, out_shape=[shp] * 4, compiler_params=_params(("parallel",)), name=name)(parts, w, m, v)


def _small_layout(rep_shapes, sh_shapes):
    rows_r = [-(-s[1] // LANES) for s in rep_shapes]
    off_r = [sum(rows_r[:i]) for i in range(len(rows_r))]
    tot_r = -(-max(sum(rows_r), 8) // 8) * 8
    rows_s = [-(-s[-2] // 8) * 8 for s in sh_shapes]
    off_s = [sum(rows_s[:i]) for i in range(len(rows_s))]
    tot_s = max(sum(rows_s), 8)
    cmax = max([s[-1] for s in sh_shapes] + [LANES])
    return rows_r, off_r, tot_r, off_s, tot_s, cmax


def _small_exchange(rep_parts, sh_parts, name):
    nr, ns = len(rep_parts), len(sh_parts)
    rows_r, off_r, tot_r, off_s, tot_s, cmax = _small_layout([p.shape for p in rep_parts], [p.shape for p in sh_parts])
    vm = pl.BlockSpec(memory_space=pltpu.VMEM)

    def body(*refs):
        pos = 0

        def take(k):
            nonlocal pos
            out = refs[pos:pos + k]
            pos += k
            return out
        rp, sp = take(nr), take(ns)
        out_r, out_s = take(2)
        pack_r, got_r, pack_s, got_s, send_r, recv_r, send_s, recv_s = take(8)
        me = _my_index()
        pack_r[...] = jnp.zeros_like(pack_r)
        pack_s[...] = jnp.zeros_like(pack_s)
        for i in range(nr):
            nfull = rep_parts[i].shape[1]
            for rr in range(rows_r[i]):
                wdt = min(LANES, nfull - rr * LANES)
                pack_r[off_r[i] + rr:off_r[i] + rr + 1, 0:wdt] = rp[i][0:1, rr * LANES:rr * LANES + wdt]
        for i in range(ns):
            _, r_i, c_i = sh_parts[i].shape
            for j in range(N_DEV):
                pack_s[j, off_s[i]:off_s[i] + r_i, 0:c_i] = sp[i][j]
        got_r[me] = pack_r[...]
        got_s[me] = pack_s[me]
        sends = []
        for k in range(1, N_DEV):
            a = pltpu.make_async_remote_copy(src_ref=pack_r, dst_ref=got_r.at[me], send_sem=send_r.at[k - 1], recv_sem=recv_r.at[k - 1],
                                             device_id=_peer(k), device_id_type=MESH_ID)
            b = pltpu.make_async_remote_copy(src_ref=pack_s.at[_peer_index(k)], dst_ref=got_s.at[me], send_sem=send_s.at[k - 1],
                                             recv_sem=recv_s.at[k - 1], device_id=_peer(k), device_id_type=MESH_ID)
            a.start()
            b.start()
            sends += [a, b]
        for k in range(1, N_DEV):
            pltpu.make_async_remote_copy(src_ref=pack_r, dst_ref=got_r.at[_peer_index(k)], send_sem=send_r.at[k - 1],
                                         recv_sem=recv_r.at[k - 1], device_id=_peer(k), device_id_type=MESH_ID).wait_recv()
            pltpu.make_async_remote_copy(src_ref=pack_s.at[me], dst_ref=got_s.at[_peer_index(k)], send_sem=send_s.at[k - 1],
                                         recv_sem=recv_s.at[k - 1], device_id=_peer(k), device_id_type=MESH_ID).wait_recv()
        for cp in sends:
            cp.wait_send()
        tot_rep = got_r[0]
        tot_sh = got_s[0]
        for j in range(1, N_DEV):
            tot_rep = tot_rep + got_r[j]
            tot_sh = tot_sh + got_s[j]
        out_r[...] = tot_rep
        out_s[...] = tot_sh

    return _pcall(body, carry_us=SMALL_EXCHANGE_CARRY_US, in_specs=[vm] * (nr + ns), out_specs=[vm] * 2,
                  out_shape=[jax.ShapeDtypeStruct((tot_r, LANES), F32), jax.ShapeDtypeStruct((tot_s, cmax), F32)],
                  scratch_shapes=[pltpu.VMEM((tot_r, LANES), F32), pltpu.VMEM((N_DEV, tot_r, LANES), F32),
                                  pltpu.VMEM((N_DEV, tot_s, cmax), F32), pltpu.VMEM((N_DEV, tot_s, cmax), F32),
                                  pltpu.SemaphoreType.DMA((N_DEV - 1,)), pltpu.SemaphoreType.DMA((N_DEV - 1,)),
                                  pltpu.SemaphoreType.DMA((N_DEV - 1,)), pltpu.SemaphoreType.DMA((N_DEV - 1,))],
                  compiler_params=_params(), name=name)(*rep_parts, *sh_parts)


def _small_adam(tot_rep, tot_sh, rep_w, rep_m, rep_v, sh_w, sh_m, sh_v, name):
    nr, ns = len(rep_w), len(sh_w)
    rows_r, off_r, _, off_s, _, _ = _small_layout([w.shape for w in rep_w], [w.shape for w in sh_w])
    vm = pl.BlockSpec(memory_space=pltpu.VMEM)

    def body(*refs):
        pos = 0

        def take(k):
            nonlocal pos
            out = refs[pos:pos + k]
            pos += k
            return out
        (tr_ref, ts_ref), rw, rm, rv, sw, sm, sv = take(2), take(nr), take(nr), take(nr), take(ns), take(ns), take(ns)
        rg, rd, rm2, rv2 = take(nr), take(nr), take(nr), take(nr)
        sg, sd, sm2, sv2 = take(ns), take(ns), take(ns), take(ns)
        for i in range(nr):
            nfull = rep_w[i].shape[1]
            for rr in range(rows_r[i]):
                wdt = min(LANES, nfull - rr * LANES)
                rg[i][0:1, rr * LANES:rr * LANES + wdt] = tr_ref[off_r[i] + rr:off_r[i] + rr + 1, 0:wdt]
            delta, m2, v2 = _adam(rg[i][...], rw[i][...], rm[i][...], rv[i][...])
            rd[i][...] = delta
            rm2[i][...] = m2
            rv2[i][...] = v2
        for i in range(ns):
            r_i, c_i = sh_w[i].shape
            g = ts_ref[off_s[i]:off_s[i] + r_i, 0:c_i]
            delta, m2, v2 = _adam(g, sw[i][...], sm[i][...], sv[i][...])
            sg[i][...] = g
            sd[i][...] = delta
            sm2[i][...] = m2
            sv2[i][...] = v2

    shapes = [jax.ShapeDtypeStruct(w.shape, F32) for w in rep_w] * 4 + [jax.ShapeDtypeStruct(w.shape, F32) for w in sh_w] * 4
    outs = _pcall(body, in_specs=[vm] * (2 + 3 * nr + 3 * ns), out_specs=[vm] * len(shapes), out_shape=shapes,
                  compiler_params=_params(), name=name)(tot_rep, tot_sh, *rep_w, *rep_m, *rep_v, *sh_w, *sh_m, *sh_v)
    rep_out = [outs[i * nr:(i + 1) * nr] for i in range(4)]
    sh_out = [outs[4 * nr + i * ns:4 * nr + (i + 1) * ns] for i in range(4)]
    return rep_out, sh_out


_CONF = ("norm_g", "a_w_in", "a_b_in", "a_dw_w", "a_dw_b", "a_ln_g", "a_ln_b", "a_w_out", "a_b_out")
_FFN = ("ffn_norm_g", "ffn_w_up", "ffn_dw_w", "ffn_dw_b", "ffn_w_down")
_POOL = ("norm_g", "b_w_group", "b_scale")
_ATTN = ("norm_g", "c_w_qkv", "c_q_norm_g", "c_k_norm_g", "c_sinks", "c_w_o")
_LAYERS = (_CONF + _FFN, _POOL + _FFN, _ATTN + _FFN, _CONF + _FFN)
_NAMES = tuple("l%d_%s" % (i, n) for i, names in enumerate(_LAYERS) for n in names)
_BIG = ("a_w_in", "a_w_out", "ffn_w_up", "ffn_w_down", "b_w_group", "c_w_qkv", "c_w_o")
_SHARDED_SMALL = ("a_dw_w", "ffn_dw_w")


def _pad_rows(a, mult=8):
    r = a.shape[0]
    rp = -(-r // mult) * mult
    return a if rp == r else jnp.pad(a, ((0, rp - r), (0, 0)))


def _unstack_cols(st, rows):
    s, r, cs = st.shape
    return jnp.transpose(st, (1, 0, 2)).reshape(r, s * cs)[:rows]


def _stack_cols(a):
    r, c = a.shape
    return jnp.transpose(a.reshape(r, N_DEV, c // N_DEV), (1, 0, 2))


def _row(v):
    return v.reshape(1, -1)


def _ffn_forward(x_mid, p, tag):
    h2 = _rms_fwd(x_mid, _row(p["ffn_norm_g"]), BF16, "rms_fwd_bf16")
    u0 = _mm(h2, p["ffn_w_up"], "nn", name="ffn_up", tn=1408, tk=2048)
    a = _ffn_act_fwd(u0, p["ffn_dw_w"], _row(p["ffn_dw_b"]), "ffn_act_fwd")
    x_out = _mm(a, p["ffn_w_down"], "nn", res=x_mid, name="ffn_down", tk=2816)
    return x_out, dict(h2=h2, u0=u0, a=a)


def _ffn_backward(dx_out, x_mid, p, sv, grads):
    dwd = _mm(sv["a"], dx_out, "tn", out_dtype=BF16, name="ffn_down_dw", tm=1408, tk=2048)
    grads["ffn_w_down"] = dwd.reshape(N_DEV, dwd.shape[0] // N_DEV, dwd.shape[1])
    da = _mm(dx_out, p["ffn_w_down"], "nt", name="ffn_down_dx", tn=1408, tk=2048)
    du0, dww, dwb = _ffn_act_bwd(da, sv["u0"], p["ffn_dw_w"], _row(p["ffn_dw_b"]), "ffn_act_bwd")
    kw = dww.shape[1]
    grads["ffn_dw_w"] = _stack_cols(jnp.transpose(dww, (1, 0, 2)).reshape(kw, -1))
    grads["ffn_dw_b"] = dwb.reshape(1, -1)
    grads["ffn_w_up"] = _mm(sv["h2"], du0, "tn", out_dtype=BF16, out_stack=N_DEV, name="ffn_up_dw", tn=1408, tk=2048)
    dh2 = _mm(du0, p["ffn_w_up"], "nt", name="ffn_up_dx", tk=1408, panels=2)
    dx_mid, dg, _ = _rms_bwd(dh2, x_mid, _row(p["ffn_norm_g"]), dx_out, "rms_bwd")
    grads["ffn_norm_g"] = dg
    return dx_mid


def _conf_forward(x, p):
    h = _rms_fwd(x, _row(p["norm_g"]), BF16, "rms_fwd_bf16")
    u = _mm(h, p["a_w_in"], "nn", bias=_row(p["a_b_in"]), name="conf_in", tn=512, tk=2048)
    cpre = _conf_conv_fwd(u, p["a_dw_w"], _row(p["a_dw_b"]), "conf_conv_fwd")
    s = _ln_silu_fwd(cpre, _row(p["a_ln_g"]), _row(p["a_ln_b"]), "ln_silu_fwd")
    x_mid = _mm(s, p["a_w_out"], "nn", bias=_row(p["a_b_out"]), res=x, name="conf_out", tk=2048)
    return x_mid, dict(h=h, u=u, cpre=cpre, s=s)


def _conf_backward(dx_mid, x, p, sv, grads):
    dwo = _mm(sv["s"], dx_mid, "tn", out_dtype=BF16, name="conf_out_dw", tk=2048)
    grads["a_w_out"] = dwo.reshape(N_DEV, dwo.shape[0] // N_DEV, dwo.shape[1])
    ds = _mm(dx_mid, p["a_w_out"], "nt", name="conf_out_dx", tk=2048)
    dc, dlg, dlb = _ln_silu_bwd(ds, sv["cpre"], _row(p["a_ln_g"]), _row(p["a_ln_b"]), "ln_silu_bwd")
    grads["a_ln_g"], grads["a_ln_b"] = dlg, dlb
    du, dww, dwb, dbin = _conf_conv_bwd(dc, sv["u"], p["a_dw_w"], "conf_conv_bwd")
    grads["a_dw_w"] = _stack_cols(dww)
    grads["a_dw_b"] = dwb
    grads["a_b_in"] = dbin.reshape(1, -1)
    grads["a_w_in"] = _mm(sv["h"], du, "tn", out_dtype=BF16, out_stack=N_DEV, name="conf_in_dw", tn=512, tk=2048)
    dh = _mm(du, p["a_w_in"], "nt", name="conf_in_dx", tk=512, panels=4)
    dx, dg, dbo = _rms_bwd(dh, x, _row(p["norm_g"]), dx_mid, "rms_bwd")
    grads["norm_g"] = dg
    grads["a_b_out"] = dbo
    return dx


def _pool_forward(x, p):
    h = _rms_fwd(x, _row(p["norm_g"]), F32, "rms_fwd_f32")
    mixed = _pool_fwd(h, "pool_fwd")
    x_mid = _pool_mm_fwd(mixed, p["b_w_group"], _row(p["b_scale"]), x, "pool_mm_fwd")
    return x_mid, dict(mixed=mixed)


def _pool_backward(dx_mid, x, p, sv, grads):
    dmixed, dwg, dscale = _pool_mm_bwd(dx_mid, sv["mixed"], p["b_w_group"], _row(p["b_scale"]), "pool_mm_bwd")
    ng, gd, _ = dwg.shape
    grads["b_w_group"] = jnp.transpose(dwg.reshape(ng, N_DEV, gd // N_DEV, gd), (1, 0, 2, 3)).reshape(N_DEV, ng * gd // N_DEV, gd).astype(BF16)
    grads["b_scale"] = dscale
    dh = _pool_bwd(dmixed, "pool_bwd")
    dx, dg, _ = _rms_bwd(dh, x, _row(p["norm_g"]), dx_mid, "rms_bwd")
    grads["norm_g"] = dg
    return dx


def _attn_tables(p, positions, d_model):
    n_q = d_model // HEAD
    n_kv = n_q // 8
    tabs = _rope_tables(positions)
    gq2 = jnp.concatenate([p["c_q_norm_g"], p["c_q_norm_g"]]).reshape(1, LANES)
    gk2 = jnp.concatenate([p["c_k_norm_g"], p["c_k_norm_g"]]).reshape(1, LANES)
    sink_tab = jnp.repeat(jnp.repeat(p["c_sinks"].reshape(-1, 2), HEAD, axis=1), 8, axis=0)
    return n_q, n_kv, tabs, gq2, gk2, sink_tab


def _attn_forward(x, p, positions):
    n_q, n_kv, tabs, gq2, gk2, sink_tab = _attn_tables(p, positions, x.shape[1])
    h = _rms_fwd(x, _row(p["norm_g"]), BF16, "rms_fwd_bf16")
    qkv = _mm(h, p["c_w_qkv"], "nn", name="attn_qkv", tn=1280, tk=2048)
    q, k2, v2 = _qk_prep_fwd(qkv, tabs, gq2, gk2, n_q, n_kv, "qk_prep_fwd")
    o, lse = _attn_fwd(q, k2, v2, sink_tab, "attn_fwd")
    x_mid = _mm(o, p["c_w_o"], "nn", res=x, name="attn_out", tk=2048)
    return x_mid, dict(h=h, qkv=qkv, q=q, k2=k2, v2=v2, o=o, lse=lse)


def _attn_backward(dx_mid, x, p, positions, sv, grads):
    n_q, n_kv, tabs, gq2, gk2, sink_tab = _attn_tables(p, positions, x.shape[1])
    dwo = _mm(sv["o"], dx_mid, "tn", out_dtype=BF16, name="attn_out_dw", tk=2048)
    grads["c_w_o"] = dwo.reshape(N_DEV, dwo.shape[0] // N_DEV, dwo.shape[1])
    do = _mm(dx_mid, p["c_w_o"], "nt", name="attn_out_dx", tk=2048)
    dq, dkc, dkp, dvc, dvp, dsk = _attn_bwd(do, sv["q"], sv["o"], sv["lse"], sv["k2"], sv["v2"], sink_tab, "attn_bwd")
    nkv_, nb = dsk.shape[0], dsk.shape[1]
    dsk = dsk.reshape(nkv_, nb, -1, 8, LANES)[:, :, :, 0, :].sum(axis=1).reshape(-1, LANES)
    grads["c_sinks"] = jnp.stack([dsk[:, 0], dsk[:, HEAD]], axis=1).reshape(1, -1)
    dqkv, dgq, dgk = _qk_prep_bwd(dq, dkc, dkp, dvc, dvp, sv["qkv"], tabs, gq2, gk2, n_q, n_kv, "qk_prep_bwd")
    grads["c_q_norm_g"] = dgq[:, :HEAD] + dgq[:, HEAD:]
    grads["c_k_norm_g"] = dgk[:, :HEAD] + dgk[:, HEAD:]
    dh = _mm(dqkv, p["c_w_qkv"], "nt", name="attn_qkv_dx", tk=1280)
    dwq = _mm(sv["h"], dqkv, "tn", out_dtype=BF16, name="attn_qkv_dw", tn=1280, tk=2048)
    grads["c_w_qkv"] = _stack_cols(dwq)
    dx, dg, _ = _rms_bwd(dh, x, _row(p["norm_g"]), dx_mid, "rms_bwd")
    grads["norm_g"] = dg
    return dx


class _LayerWeights:
    def __init__(self, li, weights, small_full, comm):
        self.li, self.weights, self.small_full, self.comm, self.cache = li, weights, small_full, comm, {}

    def __getitem__(self, nme):
        if nme not in self.cache:
            self.cache[nme] = self.fetch(nme)
        return self.cache[nme]

    def fetch(self, nme):
        full = "l%d_%s" % (self.li, nme)
        w = self.weights[full]
        if nme in _SHARDED_SMALL:
            return _unstack_cols(self.small_full[full], w.shape[0])
        if nme not in _BIG:
            return w
        got = self.comm.need(full)
        if nme in ("a_w_in", "ffn_w_up"):
            return got
        if nme == "c_w_qkv":
            return _unstack_cols(got, w.shape[0])
        if nme == "b_w_group":
            ng, gs, gd = w.shape
            return jnp.transpose(got.reshape(N_DEV, ng, gs, gd), (1, 0, 2, 3)).reshape(ng, N_DEV * gs, gd)
        return got.reshape(-1, w.shape[1])


class _LayerGrads(dict):
    def __init__(self, li, comm):
        super().__init__()
        self.li, self.comm = li, comm

    def __setitem__(self, nme, value):
        if nme in _BIG:
            self.comm.push("l%d_%s" % (self.li, nme), "scatter", value)
        else:
            super().__setitem__(nme, value)


def kernel(*args):
    n_w = len(_NAMES)
    x, positions = args[0], args[1]
    weights = dict(zip(_NAMES, args[2:2 + n_w]))
    loss_target = args[2 + n_w]
    moms = dict(zip(_NAMES, args[3 + n_w:3 + 2 * n_w]))
    vels = dict(zip(_NAMES, args[3 + 2 * n_w:3 + 3 * n_w]))
    x0 = x[0]
    pos = positions[0]
    kinds = ("conf", "pool", "attn", "conf")
    comm = _Comm()
    _STATE["comm"], _STATE["last"] = comm, None
    shd = [n for n in _NAMES if n.split("_", 1)[1] in _SHARDED_SMALL]
    small_full = dict(zip(shd, _all_gather([_pad_rows(weights[n]) for n in shd], "gather_small")))
    for n in _NAMES:
        if n.split("_", 1)[1] in _BIG:
            w = weights[n]
            comm.push(n, "gather1", w.astype(BF16).reshape(-1, w.shape[-1]))
    results = {}

    def update_ready():
        while comm.scattered:
            full, parts = comm.scattered.pop(0)
            w = weights[full]
            w2 = w.reshape(-1, w.shape[-1])
            outs = _sum_adam(parts, w2, moms[full].reshape(w2.shape), vels[full].reshape(w2.shape), "adam_" + full.split("_", 1)[1])
            results[full] = tuple(o.reshape(w.shape) for o in outs)

    params, saved = [], []
    cur = x0
    for li, names in enumerate(_LAYERS):
        comm.layer = li
        p = _LayerWeights(li, weights, small_full, comm)
        if kinds[li] == "conf":
            x_mid, sv = _conf_forward(cur, p)
        elif kinds[li] == "pool":
            x_mid, sv = _pool_forward(cur, p)
        else:
            x_mid, sv = _attn_forward(cur, p, pos)
        x_out, sv_f = _ffn_forward(x_mid, p, kinds[li])
        params.append(p)
        saved.append((sv, sv_f, cur, x_mid))
        cur = x_out
    dy, loss_part = _loss_head(cur, loss_target[0], "loss_head")
    loss = lax.psum(loss_part[0, 0], ("x", "y", "c"))

    small_grads = {}
    dcur = dy
    for li in range(len(_LAYERS) - 1, -1, -1):
        p = params[li]
        sv, sv_f, x_in, x_mid = saved[li]
        grads = _LayerGrads(li, comm)
        dmid = _ffn_backward(dcur, x_mid, p, sv_f, grads)
        update_ready()
        if kinds[li] == "conf":
            dcur = _conf_backward(dmid, x_in, p, sv, grads)
        elif kinds[li] == "pool":
            dcur = _pool_backward(dmid, x_in, p, sv, grads)
        else:
            dcur = _attn_backward(dmid, x_in, p, pos, sv, grads)
        update_ready()
        for n in _LAYERS[li]:
            if n not in _BIG:
                small_grads["l%d_%s" % (li, n)] = grads[n]
    rep = [n for n in _NAMES if n.split("_", 1)[1] not in _BIG and n.split("_", 1)[1] not in _SHARDED_SMALL]
    tot_rep, tot_sh = _small_exchange([small_grads[n] for n in rep], [small_grads[n] for n in shd], "small_exchange")
    while comm.queue or comm.scattered:
        if not comm.scattered:
            comm.flush(comm.take(1e9))
        update_ready()
    _STATE["comm"] = None
    rep_out, sh_out = _small_adam(tot_rep, tot_sh, [_row(weights[n]) for n in rep], [_row(moms[n]) for n in rep], [_row(vels[n]) for n in rep],
                                  [weights[n] for n in shd], [moms[n] for n in shd], [vels[n] for n in shd], "small_adam")
    for i, n in enumerate(rep):
        results[n] = tuple(rep_out[k][i].reshape(weights[n].shape) for k in range(4))
    for i, n in enumerate(shd):
        results[n] = tuple(sh_out[k][i] for k in range(4))

    _STATE["last"] = None
    grad_x = dcur[None]
    out = [loss, grad_x]
    for k in range(4):
        out += [results[n][k] for n in _NAMES]
    return tuple(out)
```

```python
import functools
import math

import jax
import jax.numpy as jnp
from jax import lax
from jax.experimental import pallas as pl
from jax.experimental.pallas import tpu as pltpu

F32 = jnp.float32
BF16 = jnp.bfloat16
N_DEV = 8
EPS = 1e-6
LANES = 128
HEAD = 64
Q_BLOCK = 128
ROT_DIM = 16
ROPE_THETA = 500000.0
POOL_WINDOWS = (2, 4, 8, 16)
HALO = 32
ROWS = 128
VMEM_LIMIT = 56 * 1024 * 1024
ADAM_LR, ADAM_B1, ADAM_B2, ADAM_EPS, ADAM_WD, ADAM_STEP = 0.001, 0.9, 0.999, 1e-08, 0.01, 10
MESH_ID = pl.DeviceIdType.MESH
MXU_FLOPS_PER_US = 7.5e8
HBM_BYTES_PER_US = 2.5e6
ATTN_US_PER_STEP = 1.1
ATTN_BLOCKS_PER_STEP = 1
CONV_FWD_US_PER_ELEM = 1.1e-5
CONV_BWD_US_PER_ELEM = 2.3e-5
ACT_FWD_US_PER_ELEM = 4.8e-6
QK_PREP_BWD_US_PER_ELEM = 1.2e-5
SMALL_EXCHANGE_CARRY_US = 100.0


def _make_call(body, **kw):
    return pl.pallas_call(body, **kw)


_STATE = {"comm": None, "last": None}


def _raw_call(body, **kw):
    call = _make_call(body, **kw)

    def run(*args):
        last = _STATE["last"]
        if last is not None and args:
            first, _ = lax.optimization_barrier((args[0], last))
            args = (first,) + tuple(args[1:])
        res = call(*args)
        _STATE["last"] = res[0] if isinstance(res, (list, tuple)) else res
        return res
    return run


def _pcall(body, carry_us=0.0, **kw):
    comm = _STATE["comm"]
    jobs = comm.take(carry_us) if (comm is not None and carry_us > 0) else []
    if not jobs:
        return _raw_call(body, **kw)
    return _carry(body, jobs, comm, kw)


def _params(sem=None, **kw):
    if sem is not None:
        kw["dimension_semantics"] = sem
    return pltpu.CompilerParams(vmem_limit_bytes=VMEM_LIMIT, **kw)


def _pick(dim, pref, mult=LANES):
    best = None
    d = mult
    while d <= min(dim, pref):
        if dim % d == 0:
            best = d
        d += mult
    return dim if best is None else best


def _sigmoid(x):
    return 1.0 / (1.0 + jnp.exp(-x))


def _fold8(p):
    r, c = p.shape
    return p.reshape(r // 8, 8, c).sum(axis=0)


def _window(win_ref, e):
    win_ref[...] = e
    return win_ref


def _rows(win_ref, k, r):
    return win_ref[k:k + r, :]


def _lshape(a):
    return a.shape if a.ndim == 2 else (a.shape[1], a.shape[0] * a.shape[2])


def _panel(a):
    return a.shape[1] if a.ndim == 2 else a.shape[2]


def _lspec(a, br, bc, rc):
    if a.ndim == 2:
        return pl.BlockSpec((br, bc), rc)
    per = a.shape[2] // bc

    def idx(*g):
        r, c = rc(*g)
        return (c // per, r, c % per)
    return pl.BlockSpec((None, br, bc), idx)


def _mm(a, b, dims, *, name, out_dtype=F32, out_stack=None, bias=None, res=None, tm=1024, tn=1024, tk=1024, panels=1):
    (ar, ac), (br_, bc_) = _lshape(a), _lshape(b)
    if dims == "nn":
        m, k, n = ar, ac, bc_
        lim_m, lim_k, lim_n = m, min(_panel(a), k), _panel(b)
    elif dims == "nt":
        m, k, n = ar, ac, br_
        lim_m, lim_k, lim_n = m, math.gcd(_panel(a), _panel(b)), n
    else:
        m, k, n = ac, ar, bc_
        lim_m, lim_k, lim_n = _panel(a), k, _panel(b)
    if out_stack is not None:
        lim_n = math.gcd(lim_n, n // out_stack)
    sub = 16 if (out_dtype == BF16 or a.dtype == BF16) else 8
    tm = _pick(lim_m, tm, LANES if dims == "tn" else sub)
    tn = _pick(lim_n, tn)
    tk = _pick(lim_k, tk, sub if dims == "tn" else LANES)
    if panels > 1:
        assert dims == "nt" and b.ndim == 3 and tk == b.shape[2] and b.shape[0] % panels == 0 and _panel(a) % (panels * tk) == 0
    nk = k // (tk * panels)
    if dims == "tn":
        a_spec = _lspec(a, tk, tm, lambda i, j, kk: (kk, i))
    else:
        a_spec = _lspec(a, tm, tk * panels, lambda i, j, kk: (i, kk))
    if panels > 1:
        b_spec = pl.BlockSpec((panels, tn, tk), lambda i, j, kk: (kk, j, 0))
    elif dims == "nt":
        b_spec = _lspec(b, tn, tk, lambda i, j, kk: (j, kk))
    else:
        b_spec = _lspec(b, tk, tn, lambda i, j, kk: (kk, j))
    contract = {"nn": ((1,), (0,)), "nt": ((1,), (1,)), "tn": ((0,), (0,))}[dims]
    in_specs, args = [a_spec, b_spec], [a, b]
    if bias is not None:
        in_specs.append(pl.BlockSpec((1, tn), lambda i, j, kk: (0, j)))
        args.append(bias)
    if res is not None:
        in_specs.append(pl.BlockSpec((tm, tn), lambda i, j, kk: (i, j)))
        args.append(res)
    if out_stack is None:
        out_shape = jax.ShapeDtypeStruct((m, n), out_dtype)
    else:
        out_shape = jax.ShapeDtypeStruct((out_stack, m, n // out_stack), out_dtype)
    o_spec = _lspec(out_shape, tm, tn, lambda i, j, kk: (i, j))
    has_bias, has_res = bias is not None, res is not None

    def body(*refs):
        a_ref, b_ref = refs[0], refs[1]
        pos = 2
        bias_ref = res_ref = None
        if has_bias:
            bias_ref = refs[pos]
            pos += 1
        if has_res:
            res_ref = refs[pos]
            pos += 1
        o_ref = refs[pos]

        def part():
            if panels == 1:
                return lax.dot_general(a_ref[...].astype(BF16), b_ref[...].astype(BF16), (contract, ((), ())),
                                       preferred_element_type=F32)
            r = None
            for q in range(panels):
                t_ = lax.dot_general(a_ref[:, q * tk:(q + 1) * tk].astype(BF16), b_ref[q].astype(BF16), (contract, ((), ())),
                                     preferred_element_type=F32)
                r = t_ if r is None else r + t_
            return r

        def finish(r):
            if has_bias:
                r = r + bias_ref[...]
            if has_res:
                r = r + res_ref[...]
            o_ref[...] = r.astype(out_dtype)

        if nk == 1:
            finish(part())
        else:
            acc = refs[pos + 1]
            kk = pl.program_id(2)

            @pl.when(kk == 0)
            def _():
                acc[...] = part()

            @pl.when(kk > 0)
            def _():
                acc[...] += part()

            @pl.when(kk == nk - 1)
            def _():
                finish(acc[...])

    scratch = [] if nk == 1 else [pltpu.VMEM((tm, tn), F32)]
    return _pcall(body, carry_us=2.0 * m * n * k / MXU_FLOPS_PER_US, grid=(m // tm, n // tn, nk), in_specs=in_specs, out_specs=o_spec, out_shape=out_shape,
                  scratch_shapes=scratch, compiler_params=_params(("parallel", "parallel", "arbitrary")), name=name)(*args)


def _rms_fwd(x, g, out_dtype, name):
    t, d = x.shape
    tm = _pick(t, 512, 16)

    def body(x_ref, g_ref, o_ref):
        xv = x_ref[...]
        r = lax.rsqrt(jnp.mean(xv * xv, axis=-1, keepdims=True) + EPS)
        o_ref[...] = ((xv * r) * g_ref[...]).astype(out_dtype)

    return _pcall(body, carry_us=6.0 * t * d / HBM_BYTES_PER_US, grid=(t // tm,),
                  in_specs=[pl.BlockSpec((tm, d), lambda i: (i, 0)), pl.BlockSpec((1, d), lambda i: (0, 0))],
                  out_specs=pl.BlockSpec((tm, d), lambda i: (i, 0)), out_shape=jax.ShapeDtypeStruct((t, d), out_dtype),
                  compiler_params=_params(("parallel",)), name=name)(x, g)


def _rms_bwd(dh, x, g, dres, name):
    t, d = x.shape
    tm = _pick(t, 256, 16)

    def body(dh_ref, x_ref, g_ref, dres_ref, dx_ref, dg_ref, cs_ref, dx16_ref):
        xv, dhv, dr = x_ref[...], dh_ref[...], dres_ref[...]
        r = lax.rsqrt(jnp.mean(xv * xv, axis=-1, keepdims=True) + EPS)
        xh = xv * r
        dxh = dhv * g_ref[...]
        dxv = dr + r * (dxh - xh * jnp.mean(dxh * xh, axis=-1, keepdims=True))
        dx_ref[...] = dxv
        dx16_ref[...] = dxv.astype(BF16)
        pg = jnp.sum(dhv * xh, axis=0, keepdims=True)
        pc = jnp.sum(dr, axis=0, keepdims=True)

        @pl.when(pl.program_id(0) == 0)
        def _():
            dg_ref[...] = pg
            cs_ref[...] = pc

        @pl.when(pl.program_id(0) > 0)
        def _():
            dg_ref[...] += pg
            cs_ref[...] += pc

    row = pl.BlockSpec((tm, d), lambda i: (i, 0))
    vec = pl.BlockSpec((1, d), lambda i: (0, 0))
    return _pcall(body, carry_us=16.0 * t * d / HBM_BYTES_PER_US, grid=(t // tm,), in_specs=[row, row, vec, row], out_specs=[row, vec, vec, row],
                  out_shape=[jax.ShapeDtypeStruct((t, d), F32), jax.ShapeDtypeStruct((1, d), F32), jax.ShapeDtypeStruct((1, d), F32),
                             jax.ShapeDtypeStruct((t, d), BF16)],
                  compiler_params=_params(("arbitrary",)), name=name)(dh, x, g, dres)


def _loss_head(y, target, name):
    t, d = y.shape
    tm = _pick(t, 512, 16)

    def body(y_ref, t_ref, dy_ref, l_ref, dy16_ref):
        e = y_ref[...] - t_ref[...]
        dy_ref[...] = e * (1.0 / d)
        dy16_ref[...] = (e * (1.0 / d)).astype(BF16)
        part = 0.5 * jnp.sum(jnp.mean(e * e, axis=-1, keepdims=True), axis=0, keepdims=True)

        @pl.when(pl.program_id(0) == 0)
        def _():
            l_ref[...] = part

        @pl.when(pl.program_id(0) > 0)
        def _():
            l_ref[...] += part

    row = pl.BlockSpec((tm, d), lambda i: (i, 0))
    return _pcall(body, grid=(t // tm,), in_specs=[row, row], out_specs=[row, pl.BlockSpec((1, 1), lambda i: (0, 0)), row],
                  out_shape=[jax.ShapeDtypeStruct((t, d), F32), jax.ShapeDtypeStruct((1, 1), F32), jax.ShapeDtypeStruct((t, d), BF16)],
                  compiler_params=_params(("arbitrary",)), name=name)(y, target)


def _ln_silu_fwd(c, g, b, name):
    t, d = c.shape
    tm = _pick(t, 512, 16)

    def body(c_ref, g_ref, b_ref, o_ref):
        cv = c_ref[...]
        xc = cv - jnp.mean(cv, axis=-1, keepdims=True)
        z = xc * lax.rsqrt(jnp.mean(xc * xc, axis=-1, keepdims=True) + EPS) * g_ref[...] + b_ref[...]
        o_ref[...] = (z * _sigmoid(z)).astype(BF16)

    row = pl.BlockSpec((tm, d), lambda i: (i, 0))
    vec = pl.BlockSpec((1, d), lambda i: (0, 0))
    return _pcall(body, carry_us=7.0 * t * d / HBM_BYTES_PER_US, grid=(t // tm,), in_specs=[row, vec, vec], out_specs=row,
                  out_shape=jax.ShapeDtypeStruct((t, d), BF16),
                  compiler_params=_params(("parallel",)), name=name)(c, g, b)


def _ln_silu_bwd(ds, c, g, b, name):
    t, d = c.shape
    tm = _pick(t, 256, 8)

    def body(ds_ref, c_ref, g_ref, b_ref, dc_ref, dg_ref, db_ref):
        cv = c_ref[...]
        xc = cv - jnp.mean(cv, axis=-1, keepdims=True)
        r = lax.rsqrt(jnp.mean(xc * xc, axis=-1, keepdims=True) + EPS)
        ch = xc * r
        z = ch * g_ref[...] + b_ref[...]
        sg = _sigmoid(z)
        dz = ds_ref[...] * (sg * (1.0 + z * (1.0 - sg)))
        dch = dz * g_ref[...]
        dc_ref[...] = r * (dch - jnp.mean(dch, axis=-1, keepdims=True) - ch * jnp.mean(dch * ch, axis=-1, keepdims=True))
        pg = jnp.sum(dz * ch, axis=0, keepdims=True)
        pb = jnp.sum(dz, axis=0, keepdims=True)

        @pl.when(pl.program_id(0) == 0)
        def _():
            dg_ref[...] = pg
            db_ref[...] = pb

        @pl.when(pl.program_id(0) > 0)
        def _():
            dg_ref[...] += pg
            db_ref[...] += pb

    row = pl.BlockSpec((tm, d), lambda i: (i, 0))
    vec = pl.BlockSpec((1, d), lambda i: (0, 0))
    return _pcall(body, grid=(t // tm,), in_specs=[row, row, vec, vec], out_specs=[row, vec, vec],
                  out_shape=[jax.ShapeDtypeStruct((t, d), F32), jax.ShapeDtypeStruct((1, d), F32), jax.ShapeDtypeStruct((1, d), F32)],
                  compiler_params=_params(("arbitrary",)), name=name)(ds, c, g, b)


def _steps(t):
    return t // ROWS


def _conf_conv_fwd(u, dw_w, dw_b, name):
    t, d2 = u.shape
    d = d2 // 2
    c = LANES
    ns = d // c
    kc = dw_w.shape[0]

    def body(a_ref, g_ref, w_ref, b_ref, o_ref, pad, win):
        pad[0:HALO, :] = jnp.zeros((HALO, c), F32)

        def glu(i, _):
            base = pl.multiple_of(i * ROWS, ROWS)
            pad[pl.ds(base + HALO, ROWS), :] = a_ref[pl.ds(base, ROWS), :] * _sigmoid(g_ref[pl.ds(base, ROWS), :])
            return 0
        lax.fori_loop(0, _steps(t), glu, 0)

        def conv(i, _):
            base = pl.multiple_of(i * ROWS, ROWS)
            e = _window(win, pad[pl.ds(base, ROWS + HALO), :])
            acc = jnp.zeros((ROWS, c), F32) + b_ref[...]
            for j in range(kc):
                acc = acc + w_ref[j:j + 1, :] * _rows(e, HALO - (kc - 1) + j, ROWS)
            o_ref[pl.ds(base, ROWS), :] = acc
            return 0
        lax.fori_loop(0, _steps(t), conv, 0)

    return _pcall(body, carry_us=CONV_FWD_US_PER_ELEM * t * d, grid=(ns,),
                  in_specs=[pl.BlockSpec((t, c), lambda s: (0, s)), pl.BlockSpec((t, c), lambda s: (0, s + ns)),
                            pl.BlockSpec((kc, c), lambda s: (0, s)), pl.BlockSpec((1, c), lambda s: (0, s))],
                  out_specs=pl.BlockSpec((t, c), lambda s: (0, s)), out_shape=jax.ShapeDtypeStruct((t, d), F32),
                  scratch_shapes=[pltpu.VMEM((t + HALO, c), F32), pltpu.VMEM((ROWS + HALO, c), F32)],
                  compiler_params=_params(("parallel",)), name=name)(u, u, dw_w, dw_b)


def _conf_conv_bwd(dc, u, dw_w, name):
    t, d = dc.shape
    c = LANES
    ns = d // c
    kc = dw_w.shape[0]

    def body(dc_ref, a_ref, g_ref, w_ref, du_ref, dww_ref, dwb_ref, db_ref, padv, padd, accw, accb, winv, wind):
        padv[0:HALO, :] = jnp.zeros((HALO, c), F32)
        padd[t:t + HALO, :] = jnp.zeros((HALO, c), F32)
        accw[...] = jnp.zeros_like(accw)
        accb[...] = jnp.zeros_like(accb)

        def fill(i, _):
            base = pl.multiple_of(i * ROWS, ROWS)
            padv[pl.ds(base + HALO, ROWS), :] = a_ref[pl.ds(base, ROWS), :] * _sigmoid(g_ref[pl.ds(base, ROWS), :])
            padd[pl.ds(base, ROWS), :] = dc_ref[pl.ds(base, ROWS), :]
            return 0
        lax.fori_loop(0, _steps(t), fill, 0)

        def step(i, _):
            base = pl.multiple_of(i * ROWS, ROWS)
            ev = _window(winv, padv[pl.ds(base, ROWS + HALO), :])
            ed = _window(wind, padd[pl.ds(base, ROWS + HALO), :])
            dcc = _rows(ed, 0, ROWS)
            dv = jnp.zeros((ROWS, c), F32)
            for j in range(kc):
                dv = dv + w_ref[j:j + 1, :] * _rows(ed, kc - 1 - j, ROWS)
                accw[j] = accw[j] + _fold8(dcc * _rows(ev, HALO - (kc - 1) + j, ROWS))
            accb[0] = accb[0] + _fold8(dcc)
            av = a_ref[pl.ds(base, ROWS), :]
            sg = _sigmoid(g_ref[pl.ds(base, ROWS), :])
            da = dv * sg
            dg = dv * av * sg * (1.0 - sg)
            du_ref[0, pl.ds(base, ROWS), :] = da.astype(BF16)
            du_ref[1, pl.ds(base, ROWS), :] = dg.astype(BF16)
            accb[1] = accb[1] + _fold8(da)
            accb[2] = accb[2] + _fold8(dg)
            return 0
        lax.fori_loop(0, _steps(t), step, 0)
        for j in range(kc):
            dww_ref[j:j + 1, :] = jnp.sum(accw[j], axis=0, keepdims=True)
        dwb_ref[...] = jnp.sum(accb[0], axis=0, keepdims=True)
        db_ref[0] = jnp.sum(accb[1], axis=0, keepdims=True)
        db_ref[1] = jnp.sum(accb[2], axis=0, keepdims=True)

    return _pcall(body, carry_us=CONV_BWD_US_PER_ELEM * t * d, grid=(ns,),
                  in_specs=[pl.BlockSpec((t, c), lambda s: (0, s)), pl.BlockSpec((t, c), lambda s: (0, s)),
                            pl.BlockSpec((t, c), lambda s: (0, s + ns)), pl.BlockSpec((kc, c), lambda s: (0, s))],
                  out_specs=[pl.BlockSpec((2, t, c), lambda s: (0, 0, s)), pl.BlockSpec((kc, c), lambda s: (0, s)),
                             pl.BlockSpec((1, c), lambda s: (0, s)), pl.BlockSpec((2, 1, c), lambda s: (0, 0, s))],
                  out_shape=[jax.ShapeDtypeStruct((2, t, d), BF16), jax.ShapeDtypeStruct((kc, d), F32),
                             jax.ShapeDtypeStruct((1, d), F32), jax.ShapeDtypeStruct((2, 1, d), F32)],
                  scratch_shapes=[pltpu.VMEM((t + HALO, c), F32), pltpu.VMEM((t + HALO, c), F32),
                                  pltpu.VMEM((kc, 8, c), F32), pltpu.VMEM((3, 8, c), F32),
                                  pltpu.VMEM((ROWS + HALO, c), F32), pltpu.VMEM((ROWS + HALO, c), F32)],
                  compiler_params=_params(("parallel",)), name=name)(dc, u, u, dw_w)


def _ffn_act_fwd(u0, dw_w, dw_b, name):
    t, f2 = u0.shape
    f = f2 // 2
    c = LANES
    ns = f // c
    kw = dw_w.shape[0]

    def body(g_ref, v_ref, wg_ref, wv_ref, bg_ref, bv_ref, o_ref, wing, winv):
        def step(i, _):
            base = pl.multiple_of(i * ROWS, ROWS)
            lo = pl.multiple_of(jnp.maximum(base - HALO, 0), HALO)
            keep = jnp.where(i > 0, 1.0, 0.0)
            eg = _window(wing, jnp.concatenate([g_ref[pl.ds(lo, HALO), :] * keep, g_ref[pl.ds(base, ROWS), :]], axis=0))
            ev = _window(winv, jnp.concatenate([v_ref[pl.ds(lo, HALO), :] * keep, v_ref[pl.ds(base, ROWS), :]], axis=0))
            gate = jnp.zeros((ROWS, c), F32) + bg_ref[...]
            val = jnp.zeros((ROWS, c), F32) + bv_ref[...]
            for j in range(kw):
                gate = gate + wg_ref[j:j + 1, :] * _rows(eg, HALO - (kw - 1) + j, ROWS)
                val = val + wv_ref[j:j + 1, :] * _rows(ev, HALO - (kw - 1) + j, ROWS)
            o_ref[pl.ds(base, ROWS), :] = (gate * _sigmoid(gate) * val).astype(BF16)
            return 0
        lax.fori_loop(0, _steps(t), step, 0)

    return _pcall(body, carry_us=ACT_FWD_US_PER_ELEM * t * f, grid=(ns,),
                  in_specs=[pl.BlockSpec((t, c), lambda s: (0, s)), pl.BlockSpec((t, c), lambda s: (0, s + ns)),
                            pl.BlockSpec((kw, c), lambda s: (0, s)), pl.BlockSpec((kw, c), lambda s: (0, s + ns)),
                            pl.BlockSpec((1, c), lambda s: (0, s)), pl.BlockSpec((1, c), lambda s: (0, s + ns))],
                  out_specs=pl.BlockSpec((t, c), lambda s: (0, s)), out_shape=jax.ShapeDtypeStruct((t, f), BF16),
                  scratch_shapes=[pltpu.VMEM((ROWS + HALO, c), F32), pltpu.VMEM((ROWS + HALO, c), F32)],
                  compiler_params=_params(("parallel",)), name=name)(u0, u0, dw_w, dw_w, dw_b, dw_b)


def _ffn_act_bwd(da, u0, dw_w, dw_b, name):
    t, f = da.shape
    c = LANES
    ns = f // c
    kw = dw_w.shape[0]

    def body(da_ref, g_ref, v_ref, wg_ref, wv_ref, bg_ref, bv_ref, du_ref, dww_ref, dwb_ref, padg, padv, accw, accb, wing, winv):
        padg[t:t + HALO, :] = jnp.zeros((HALO, c), F32)
        padv[t:t + HALO, :] = jnp.zeros((HALO, c), F32)
        accw[...] = jnp.zeros_like(accw)
        accb[...] = jnp.zeros_like(accb)

        def first(i, _):
            base = pl.multiple_of(i * ROWS, ROWS)
            lo = pl.multiple_of(jnp.maximum(base - HALO, 0), HALO)
            keep = jnp.where(i > 0, 1.0, 0.0)
            eg = _window(wing, jnp.concatenate([g_ref[pl.ds(lo, HALO), :] * keep, g_ref[pl.ds(base, ROWS), :]], axis=0))
            ev = _window(winv, jnp.concatenate([v_ref[pl.ds(lo, HALO), :] * keep, v_ref[pl.ds(base, ROWS), :]], axis=0))
            gate = jnp.zeros((ROWS, c), F32) + bg_ref[...]
            val = jnp.zeros((ROWS, c), F32) + bv_ref[...]
            for j in range(kw):
                gate = gate + wg_ref[j:j + 1, :] * _rows(eg, HALO - (kw - 1) + j, ROWS)
                val = val + wv_ref[j:j + 1, :] * _rows(ev, HALO - (kw - 1) + j, ROWS)
            dav = da_ref[pl.ds(base, ROWS), :]
            sg = _sigmoid(gate)
            dgate = dav * val * (sg * (1.0 + gate * (1.0 - sg)))
            dval = dav * (gate * sg)
            padg[pl.ds(base, ROWS), :] = dgate
            padv[pl.ds(base, ROWS), :] = dval
            for j in range(kw):
                accw[j] = accw[j] + _fold8(dgate * _rows(eg, HALO - (kw - 1) + j, ROWS))
                accw[kw + j] = accw[kw + j] + _fold8(dval * _rows(ev, HALO - (kw - 1) + j, ROWS))
            accb[0] = accb[0] + _fold8(dgate)
            accb[1] = accb[1] + _fold8(dval)
            return 0
        lax.fori_loop(0, _steps(t), first, 0)

        def second(i, _):
            base = pl.multiple_of(i * ROWS, ROWS)
            eg = _window(wing, padg[pl.ds(base, ROWS + HALO), :])
            ev = _window(winv, padv[pl.ds(base, ROWS + HALO), :])
            dg = jnp.zeros((ROWS, c), F32)
            dv = jnp.zeros((ROWS, c), F32)
            for j in range(kw):
                dg = dg + wg_ref[j:j + 1, :] * _rows(eg, kw - 1 - j, ROWS)
                dv = dv + wv_ref[j:j + 1, :] * _rows(ev, kw - 1 - j, ROWS)
            du_ref[0, pl.ds(base, ROWS), :] = dg.astype(BF16)
            du_ref[1, pl.ds(base, ROWS), :] = dv.astype(BF16)
            return 0
        lax.fori_loop(0, _steps(t), second, 0)
        for j in range(kw):
            dww_ref[0, j:j + 1, :] = jnp.sum(accw[j], axis=0, keepdims=True)
            dww_ref[1, j:j + 1, :] = jnp.sum(accw[kw + j], axis=0, keepdims=True)
        dwb_ref[0] = jnp.sum(accb[0], axis=0, keepdims=True)
        dwb_ref[1] = jnp.sum(accb[1], axis=0, keepdims=True)

    return _pcall(body, grid=(ns,),
                  in_specs=[pl.BlockSpec((t, c), lambda s: (0, s)),
                            pl.BlockSpec((t, c), lambda s: (0, s)), pl.BlockSpec((t, c), lambda s: (0, s + ns)),
                            pl.BlockSpec((kw, c), lambda s: (0, s)), pl.BlockSpec((kw, c), lambda s: (0, s + ns)),
                            pl.BlockSpec((1, c), lambda s: (0, s)), pl.BlockSpec((1, c), lambda s: (0, s + ns))],
                  out_specs=[pl.BlockSpec((2, t, c), lambda s: (0, 0, s)), pl.BlockSpec((2, kw, c), lambda s: (0, 0, s)),
                             pl.BlockSpec((2, 1, c), lambda s: (0, 0, s))],
                  out_shape=[jax.ShapeDtypeStruct((2, t, f), BF16), jax.ShapeDtypeStruct((2, kw, f), F32),
                             jax.ShapeDtypeStruct((2, 1, f), F32)],
                  scratch_shapes=[pltpu.VMEM((t + HALO, c), F32), pltpu.VMEM((t + HALO, c), F32),
                                  pltpu.VMEM((2 * kw, 8, c), F32), pltpu.VMEM((2, 8, c), F32),
                                  pltpu.VMEM((ROWS + HALO, c), F32), pltpu.VMEM((ROWS + HALO, c), F32)],
                  compiler_params=_params(("parallel",)), name=name)(da, u0, u0, dw_w, dw_w, dw_b, dw_b)


def _window_of(group):
    w = jnp.float32(POOL_WINDOWS[-1])
    for k in range(len(POOL_WINDOWS) - 2, -1, -1):
        w = jnp.where(group == k, jnp.float32(POOL_WINDOWS[k]), w)
    return w


def _select_level(group, levels):
    out = levels[-1]
    for k in range(len(levels) - 2, -1, -1):
        out = jnp.where(group == k, levels[k], out)
    return out


def _pool_fwd(h, name):
    t, d = h.shape
    c = LANES
    per = d // len(POOL_WINDOWS) // c

    def body(h_ref, o_ref):
        group = pl.program_id(0)
        wf = _window_of(group)

        def step(i, _):
            base = pl.multiple_of(i * ROWS, ROWS)
            lo = pl.multiple_of(jnp.maximum(base - HALO, 0), HALO)
            keep = jnp.where(i > 0, 1.0, 0.0)
            cur = h_ref[pl.ds(base, ROWS), :]
            e = jnp.concatenate([h_ref[pl.ds(lo, HALO), :] * keep, cur], axis=0)
            n = ROWS + HALO
            levels = []
            s = e
            for k in range(len(POOL_WINDOWS)):
                s = s + pltpu.roll(s, 1 << k, 0)
                levels.append(s[HALO:n])
            tpos = (base + lax.broadcasted_iota(jnp.int32, (ROWS, c), 0) + 1).astype(F32)
            pooled = _select_level(group, levels) / jnp.minimum(tpos, wf)
            o_ref[pl.ds(base, ROWS), :] = (pooled - cur).astype(BF16)
            return 0
        lax.fori_loop(0, _steps(t), step, 0)

    return _pcall(body, grid=(len(POOL_WINDOWS), per), in_specs=[pl.BlockSpec((t, c), lambda g, s: (0, g * per + s))],
                  out_specs=pl.BlockSpec((t, c), lambda g, s: (0, g * per + s)), out_shape=jax.ShapeDtypeStruct((t, d), BF16),
                  compiler_params=_params(("parallel", "parallel")), name=name)(h)


def _pool_bwd(dm, name):
    t, d = dm.shape
    c = LANES
    per = d // len(POOL_WINDOWS) // c

    def body(dm_ref, o_ref, pad):
        group = pl.program_id(0)
        wf = _window_of(group)
        pad[t:t + HALO, :] = jnp.zeros((HALO, c), F32)

        def fill(i, _):
            base = pl.multiple_of(i * ROWS, ROWS)
            tpos = (base + lax.broadcasted_iota(jnp.int32, (ROWS, c), 0) + 1).astype(F32)
            pad[pl.ds(base, ROWS), :] = dm_ref[pl.ds(base, ROWS), :] / jnp.minimum(tpos, wf)
            return 0
        lax.fori_loop(0, _steps(t), fill, 0)

        def step(i, _):
            base = pl.multiple_of(i * ROWS, ROWS)
            n = ROWS + HALO
            s = pad[pl.ds(base, n), :]
            levels = []
            for k in range(len(POOL_WINDOWS)):
                s = s + pltpu.roll(s, n - (1 << k), 0)
                levels.append(s[0:ROWS])
            o_ref[pl.ds(base, ROWS), :] = _select_level(group, levels) - dm_ref[pl.ds(base, ROWS), :]
            return 0
        lax.fori_loop(0, _steps(t), step, 0)

    return _pcall(body, grid=(len(POOL_WINDOWS), per), in_specs=[pl.BlockSpec((t, c), lambda g, s: (0, g * per + s))],
                  out_specs=pl.BlockSpec((t, c), lambda g, s: (0, g * per + s)), out_shape=jax.ShapeDtypeStruct((t, d), F32),
                  scratch_shapes=[pltpu.VMEM((t + HALO, c), F32)], compiler_params=_params(("parallel", "parallel")), name=name)(dm)


def _pool_mm_fwd(mixed, wg, scale, res, name):
    t, d = mixed.shape
    ng, gd, _ = wg.shape
    tm = _pick(t, 1024, 16)

    def body(a_ref, w_ref, s_ref, r_ref, o_ref):
        y = jnp.dot(a_ref[...], w_ref[...], preferred_element_type=F32)
        o_ref[...] = r_ref[...] + y * s_ref[...]

    blk = pl.BlockSpec((tm, gd), lambda g, i: (i, g))
    return _pcall(body, grid=(ng, t // tm),
                  in_specs=[blk, pl.BlockSpec((None, gd, gd), lambda g, i: (g, 0, 0)), pl.BlockSpec((1, gd), lambda g, i: (0, g)), blk],
                  out_specs=blk, out_shape=jax.ShapeDtypeStruct((t, d), F32),
                  compiler_params=_params(("parallel", "parallel")), name=name)(mixed, wg, scale, res)


def _pool_mm_bwd(dy, mixed, wg, scale, name):
    t, d = mixed.shape
    ng, gd, _ = wg.shape
    tm = _pick(t, 1024, 16)

    def body(dy_ref, a_ref, w_ref, s_ref, dm_ref, dw_ref, ds_ref):
        a, w, dyv = a_ref[...], w_ref[...], dy_ref[...]
        y = jnp.dot(a, w, preferred_element_type=F32)
        dyp = (dyv * s_ref[...]).astype(BF16)
        dm_ref[...] = lax.dot_general(dyp, w, (((1,), (1,)), ((), ())), preferred_element_type=F32)
        pw = lax.dot_general(a, dyp, (((0,), (0,)), ((), ())), preferred_element_type=F32)
        ps = jnp.sum(dyv * y, axis=0, keepdims=True)

        @pl.when(pl.program_id(1) == 0)
        def _():
            dw_ref[...] = pw
            ds_ref[...] = ps

        @pl.when(pl.program_id(1) > 0)
        def _():
            dw_ref[...] += pw
            ds_ref[...] += ps

    blk = pl.BlockSpec((tm, gd), lambda g, i: (i, g))
    wsp = pl.BlockSpec((None, gd, gd), lambda g, i: (g, 0, 0))
    vec = pl.BlockSpec((1, gd), lambda g, i: (0, g))
    return _pcall(body, grid=(ng, t // tm), in_specs=[blk, blk, wsp, vec], out_specs=[blk, wsp, vec],
                  out_shape=[jax.ShapeDtypeStruct((t, d), F32), jax.ShapeDtypeStruct((ng, gd, gd), F32), jax.ShapeDtypeStruct((1, d), F32)],
                  compiler_params=_params(("parallel", "arbitrary")), name=name)(dy, mixed, wg, scale)


def _rope_tables(positions):
    half = ROT_DIM // 2
    inv_freq = ROPE_THETA ** (-jnp.arange(0, ROT_DIM, 2, dtype=F32) / ROT_DIM)
    ang = positions.astype(F32)[:, None] * inv_freq
    cos, sin = jnp.cos(ang), jnp.sin(ang)
    t = positions.shape[0]
    ones = jnp.ones((t, HEAD - ROT_DIM), F32)
    zeros = jnp.zeros((t, HEAD - ROT_DIM), F32)
    zh = jnp.zeros((t, half), F32)
    c = jnp.concatenate([cos, cos, ones], axis=1)
    s1 = jnp.concatenate([-sin, zh, zeros], axis=1)
    s2 = jnp.concatenate([zh, sin, zeros], axis=1)
    return tuple(jnp.concatenate([a, a], axis=1) for a in (c, s1, s2))


def _half_mean(v, lo):
    s_lo = jnp.sum(jnp.where(lo, v, 0.0), axis=-1, keepdims=True)
    s_hi = jnp.sum(jnp.where(lo, 0.0, v), axis=-1, keepdims=True)
    return jnp.where(lo, s_lo, s_hi) * (1.0 / HEAD)


def _qk_prep_fwd(qkv, tabs, gq2, gk2, n_q, n_kv, name):
    t, width = qkv.shape
    tm = _pick(t, 256, 16)
    nqc, nkc = n_q * HEAD // LANES, n_kv * HEAD // LANES

    def body(x_ref, c_ref, s1_ref, s2_ref, gq_ref, gk_ref, q_ref, k2_ref, v2_ref):
        lo = lax.broadcasted_iota(jnp.int32, (tm, LANES), 1) < HEAD
        cv, s1, s2 = c_ref[...], s1_ref[...], s2_ref[...]

        def normrot(xc, g2):
            y = xc * lax.rsqrt(_half_mean(xc * xc, lo) + EPS) * g2
            return y * cv + pltpu.roll(y, LANES - ROT_DIM // 2, 1) * s1 + pltpu.roll(y, ROT_DIM // 2, 1) * s2

        def twice(y, j):
            sw = pltpu.roll(y, HEAD, 1)
            k2 = jnp.where(lo, y, sw) if j == 0 else jnp.where(lo, sw, y)
            return k2.astype(BF16)

        for ch in range(nqc):
            q_ref[:, ch * LANES:(ch + 1) * LANES] = normrot(x_ref[:, ch * LANES:(ch + 1) * LANES], gq_ref[...]).astype(BF16)
        for ch in range(nkc):
            off = (nqc + ch) * LANES
            y = normrot(x_ref[:, off:off + LANES], gk_ref[...])
            voff = (nqc + nkc + ch) * LANES
            vv = x_ref[:, voff:voff + LANES]
            for j in range(2):
                k2_ref[:, (2 * ch + j) * LANES:(2 * ch + j + 1) * LANES] = twice(y, j)
                v2_ref[:, (2 * ch + j) * LANES:(2 * ch + j + 1) * LANES] = twice(vv, j)

    row = lambda w: pl.BlockSpec((tm, w), lambda i: (i, 0))
    vec = pl.BlockSpec((1, LANES), lambda i: (0, 0))
    return _pcall(body, grid=(t // tm,), in_specs=[row(width), row(LANES), row(LANES), row(LANES), vec, vec],
                  out_specs=[row(n_q * HEAD), row(n_kv * LANES), row(n_kv * LANES)],
                  out_shape=[jax.ShapeDtypeStruct((t, n_q * HEAD), BF16), jax.ShapeDtypeStruct((t, n_kv * LANES), BF16),
                             jax.ShapeDtypeStruct((t, n_kv * LANES), BF16)],
                  compiler_params=_params(("parallel",)), name=name)(qkv, *tabs, gq2, gk2)


def _qk_prep_bwd(dq, dk_cur, dk_prev, dv_cur, dv_prev, qkv, tabs, gq2, gk2, n_q, n_kv, name):
    t, width = qkv.shape
    tm = Q_BLOCK
    nb = t // tm
    nqc, nkc = n_q * HEAD // LANES, n_kv * HEAD // LANES

    def body(dq_ref, kc_ref, kp_ref, vc_ref, vp_ref, x_ref, c_ref, s1_ref, s2_ref, gq_ref, gk_ref, o_ref, dgq_ref, dgk_ref):
        lo = lax.broadcasted_iota(jnp.int32, (tm, LANES), 1) < HEAD
        cv, s1, s2 = c_ref[...], s1_ref[...], s2_ref[...]
        more = jnp.where(pl.program_id(0) < nb - 1, 1.0, 0.0)

        def back(dy, xc, g2):
            dyn = dy * cv + pltpu.roll(dy * s1, ROT_DIM // 2, 1) + pltpu.roll(dy * s2, LANES - ROT_DIM // 2, 1)
            r = lax.rsqrt(_half_mean(xc * xc, lo) + EPS)
            xh = xc * r
            dxh = dyn * g2
            return r * (dxh - xh * _half_mean(dxh * xh, lo)), jnp.sum(dyn * xh, axis=0, keepdims=True)

        def unfold(cur_ref, prev_ref, ch):
            d0 = cur_ref[:, (2 * ch) * LANES:(2 * ch + 1) * LANES] + more * prev_ref[:, (2 * ch) * LANES:(2 * ch + 1) * LANES]
            d1 = cur_ref[:, (2 * ch + 1) * LANES:(2 * ch + 2) * LANES] + more * prev_ref[:, (2 * ch + 1) * LANES:(2 * ch + 2) * LANES]
            return jnp.where(lo, d0 + pltpu.roll(d0, HEAD, 1), d1 + pltpu.roll(d1, HEAD, 1))

        pq = jnp.zeros((1, LANES), F32)
        for ch in range(nqc):
            sl = slice(ch * LANES, (ch + 1) * LANES)
            dx, pg = back(dq_ref[:, sl], x_ref[:, sl], gq_ref[...])
            o_ref[:, sl] = dx.astype(BF16)
            pq = pq + pg
        pk = jnp.zeros((1, LANES), F32)
        for ch in range(nkc):
            sl = slice((nqc + ch) * LANES, (nqc + ch + 1) * LANES)
            dx, pg = back(unfold(kc_ref, kp_ref, ch), x_ref[:, sl], gk_ref[...])
            o_ref[:, sl] = dx.astype(BF16)
            pk = pk + pg
            vs = slice((nqc + nkc + ch) * LANES, (nqc + nkc + ch + 1) * LANES)
            o_ref[:, vs] = unfold(vc_ref, vp_ref, ch).astype(BF16)

        @pl.when(pl.program_id(0) == 0)
        def _():
            dgq_ref[...] = pq
            dgk_ref[...] = pk

        @pl.when(pl.program_id(0) > 0)
        def _():
            dgq_ref[...] += pq
            dgk_ref[...] += pk

    row = lambda w: pl.BlockSpec((tm, w), lambda i: (i, 0))
    nxt = lambda w: pl.BlockSpec((tm, w), lambda i: (jnp.minimum(i + 1, nb - 1), 0))
    vec = pl.BlockSpec((1, LANES), lambda i: (0, 0))
    kvw = n_kv * LANES
    return _pcall(body, carry_us=QK_PREP_BWD_US_PER_ELEM * t * width, grid=(nb,),
                  in_specs=[row(n_q * HEAD), row(kvw), nxt(kvw), row(kvw), nxt(kvw), row(width), row(LANES), row(LANES), row(LANES), vec, vec],
                  out_specs=[row(width), vec, vec],
                  out_shape=[jax.ShapeDtypeStruct((t, width), BF16), jax.ShapeDtypeStruct((1, LANES), F32), jax.ShapeDtypeStruct((1, LANES), F32)],
                  compiler_params=_params(("arbitrary",)), name=name)(dq, dk_cur, dk_prev, dv_cur, dv_prev, qkv, *tabs, gq2, gk2)


def _band_scores(qh, kc, kp, n, sink_row, lo_row, is_lo):
    scale = 1.0 / math.sqrt(HEAD)
    nt = (((1,), (1,)), ((), ()))
    s_c = lax.dot_general(qh, kc, nt, preferred_element_type=F32) * scale
    s_p = lax.dot_general(qh, kp, nt, preferred_element_type=F32) * scale
    qi = lax.broadcasted_iota(jnp.int32, (Q_BLOCK, Q_BLOCK), 0)
    kj = lax.broadcasted_iota(jnp.int32, (Q_BLOCK, Q_BLOCK), 1)
    s_c = jnp.where(kj <= qi, s_c, -jnp.inf)
    s_p = jnp.where((kj > qi) & (n > 0), s_p, -jnp.inf)
    pick = lo_row if is_lo else jnp.logical_not(lo_row)
    sink = jnp.max(jnp.where(pick, sink_row, -jnp.inf), axis=-1, keepdims=True)
    return s_c, s_p, sink


def _stack_heads(q_ref, s_ref, per_kv, lo):
    lo_row = lax.broadcasted_iota(jnp.int32, (1, LANES), 1) < HEAD
    qs, sinks = [], []
    for cc in range(per_kv):
        qv = q_ref[:, cc * LANES:(cc + 1) * LANES].astype(F32)
        for is_lo in (True, False):
            qs.append(jnp.where(lo, qv, 0.0) if is_lo else jnp.where(lo, 0.0, qv))
            pick = lo_row if is_lo else jnp.logical_not(lo_row)
            sinks.append(jnp.zeros((Q_BLOCK, LANES), F32) + jnp.where(pick, s_ref[8 * cc:8 * cc + 1, :], -jnp.inf))
    return jnp.concatenate(qs, axis=0).astype(BF16), jnp.max(jnp.concatenate(sinks, axis=0), axis=-1, keepdims=True)


def _band_scores8(q8, kcat, n):
    rows = q8.shape[0]
    s = lax.dot_general(q8, kcat, (((1,), (1,)), ((), ())), preferred_element_type=F32) * (1.0 / math.sqrt(HEAD))
    qi = lax.broadcasted_iota(jnp.int32, (rows, 2 * Q_BLOCK), 0) & (Q_BLOCK - 1)
    kj = lax.broadcasted_iota(jnp.int32, (rows, 2 * Q_BLOCK), 1)
    valid = ((kj >= Q_BLOCK) & ((kj - Q_BLOCK) <= qi)) | ((kj < Q_BLOCK) & (kj > qi) & (n > 0))
    return jnp.where(valid, s, -jnp.inf)


def _attn_fwd(q, k2, v2, sink_tab, name):
    t, dq = q.shape
    nc = dq // LANES
    nb = t // Q_BLOCK
    nkv = k2.shape[1] // LANES
    per_kv = nc // nkv

    def body(q_ref, kc_ref, kp_ref, vc_ref, vp_ref, s_ref, o_ref, lse_ref):
        n = pl.program_id(1)
        lo = lax.broadcasted_iota(jnp.int32, (Q_BLOCK, LANES), 1) < HEAD
        q8, sink = _stack_heads(q_ref, s_ref, per_kv, lo)
        kcat = jnp.concatenate([kp_ref[...], kc_ref[...]], axis=0)
        vcat = jnp.concatenate([vp_ref[...], vc_ref[...]], axis=0)
        s = _band_scores8(q8, kcat, n)
        m = jnp.maximum(jnp.max(s, axis=-1, keepdims=True), sink)
        p = jnp.exp(s - m)
        denom = jnp.sum(p, axis=-1, keepdims=True) + jnp.exp(sink - m)
        o8 = jnp.dot(p.astype(BF16), vcat, preferred_element_type=F32) / denom
        lse8 = m + jnp.log(denom)
        for cc in range(per_kv):
            a, b = slice(2 * cc * Q_BLOCK, (2 * cc + 1) * Q_BLOCK), slice((2 * cc + 1) * Q_BLOCK, (2 * cc + 2) * Q_BLOCK)
            o_ref[:, cc * LANES:(cc + 1) * LANES] = jnp.where(lo, o8[a], o8[b]).astype(BF16)
            lse_ref[cc] = jnp.where(lo, lse8[a], lse8[b])

    QB = ATTN_BLOCKS_PER_STEP
    qs = pl.BlockSpec((QB * Q_BLOCK, per_kv * LANES), lambda k, n: (n, k))
    cur = pl.BlockSpec((QB * Q_BLOCK, LANES), lambda k, n: (n, k))
    prev = pl.BlockSpec((Q_BLOCK, LANES), lambda k, n: (jnp.maximum(n * QB - 1, 0), k))
    return _pcall(body, carry_us=ATTN_US_PER_STEP * nkv * nb, grid=(nkv, nb // QB),
                  in_specs=[qs, cur, prev, cur, prev, pl.BlockSpec((8 * per_kv, LANES), lambda k, n: (k, 0))],
                  out_specs=[qs, pl.BlockSpec((per_kv, QB * Q_BLOCK, LANES), lambda k, n: (k, n, 0))],
                  out_shape=[jax.ShapeDtypeStruct((t, dq), BF16), jax.ShapeDtypeStruct((nc, t, LANES), F32)],
                  compiler_params=_params(("parallel", "parallel")), name=name)(q, k2, k2, v2, v2, sink_tab)


def _attn_bwd(do, q, o, lse, k2, v2, sink_tab, name):
    t, dq = q.shape
    nc = dq // LANES
    nb = t // Q_BLOCK
    nkv = k2.shape[1] // LANES
    per_kv = nc // nkv
    scale = 1.0 / math.sqrt(HEAD)
    tn_ = (((0,), (0,)), ((), ()))
    nt = (((1,), (1,)), ((), ()))

    def body(do_ref, q_ref, o_ref, lse_ref, kc_ref, kp_ref, vc_ref, vp_ref, s_ref,
             dq_ref, dkc_ref, dkp_ref, dvc_ref, dvp_ref, dsk_ref):
        n = pl.program_id(1)
        lo = lax.broadcasted_iota(jnp.int32, (Q_BLOCK, LANES), 1) < HEAD
        lo_row = lax.broadcasted_iota(jnp.int32, (1, LANES), 1) < HEAD
        q8, sink = _stack_heads(q_ref, s_ref, per_kv, lo)
        kcat = jnp.concatenate([kp_ref[...], kc_ref[...]], axis=0)
        vcat = jnp.concatenate([vp_ref[...], vc_ref[...]], axis=0)
        dos, os_, lses = [], [], []
        for cc in range(per_kv):
            cols = slice(cc * LANES, (cc + 1) * LANES)
            dov, ov, lsev = do_ref[:, cols], o_ref[:, cols].astype(F32), lse_ref[cc]
            for is_lo in (True, False):
                half = lo if is_lo else jnp.logical_not(lo)
                dos.append(jnp.where(half, dov, 0.0))
                os_.append(ov)
                lses.append(jnp.where(half, lsev, -jnp.inf))
        do8f = jnp.concatenate(dos, axis=0)
        delta = jnp.sum(do8f * jnp.concatenate(os_, axis=0), axis=-1, keepdims=True)
        lse8 = jnp.max(jnp.concatenate(lses, axis=0), axis=-1, keepdims=True)
        do8 = do8f.astype(BF16)
        p = jnp.exp(_band_scores8(q8, kcat, n) - lse8)
        ds = (p * (lax.dot_general(do8, vcat, nt, preferred_element_type=F32) - delta)).astype(BF16)
        dq8 = jnp.dot(ds, kcat, preferred_element_type=F32) * scale
        dk = lax.dot_general(ds, q8, tn_, preferred_element_type=F32) * scale
        dv = lax.dot_general(p.astype(BF16), do8, tn_, preferred_element_type=F32)
        dsink = jnp.exp(sink - lse8) * delta
        for cc in range(per_kv):
            a, b = slice(2 * cc * Q_BLOCK, (2 * cc + 1) * Q_BLOCK), slice((2 * cc + 1) * Q_BLOCK, (2 * cc + 2) * Q_BLOCK)
            dq_ref[:, cc * LANES:(cc + 1) * LANES] = jnp.where(lo, dq8[a], dq8[b])
            d_lo = -jnp.sum(dsink[a], axis=0, keepdims=True)
            d_hi = -jnp.sum(dsink[b], axis=0, keepdims=True)
            dsk_ref[0, 8 * cc:8 * cc + 8, :] = jnp.zeros((8, LANES), F32) + jnp.where(lo_row, d_lo, d_hi)
        dkp_ref[...] = dk[0:Q_BLOCK]
        dkc_ref[...] = dk[Q_BLOCK:2 * Q_BLOCK]
        dvp_ref[...] = dv[0:Q_BLOCK]
        dvc_ref[...] = dv[Q_BLOCK:2 * Q_BLOCK]

    QB = ATTN_BLOCKS_PER_STEP
    qs = pl.BlockSpec((QB * Q_BLOCK, per_kv * LANES), lambda k, n: (n, k))
    cur = pl.BlockSpec((QB * Q_BLOCK, LANES), lambda k, n: (n, k))
    prev = pl.BlockSpec((Q_BLOCK, LANES), lambda k, n: (jnp.maximum(n * QB - 1, 0), k))
    kv_shape = jax.ShapeDtypeStruct((t, nkv * LANES), F32)
    return _pcall(body, carry_us=ATTN_US_PER_STEP * nkv * nb, grid=(nkv, nb // QB),
                  in_specs=[qs, qs, qs, pl.BlockSpec((per_kv, QB * Q_BLOCK, LANES), lambda k, n: (k, n, 0)),
                            cur, prev, cur, prev, pl.BlockSpec((8 * per_kv, LANES), lambda k, n: (k, 0))],
                  out_specs=[qs, cur, cur, cur, cur, pl.BlockSpec((None, QB, 8 * per_kv, LANES), lambda k, n: (k, n, 0, 0))],
                  out_shape=[jax.ShapeDtypeStruct((t, dq), F32), kv_shape, kv_shape, kv_shape, kv_shape,
                             jax.ShapeDtypeStruct((nkv, nb, 8 * per_kv, LANES), F32)],
                  compiler_params=_params(("parallel", "parallel")), name=name)(do, q, o, lse, k2, k2, v2, v2, sink_tab)


def _peer(k):
    x, y, c = lax.axis_index("x"), lax.axis_index("y"), lax.axis_index("c")
    flip = lambda v, bit: 1 - v if bit else v
    return (flip(x, k & 4), flip(y, k & 2), flip(c, k & 1))


def _my_index():
    return 4 * lax.axis_index("x") + 2 * lax.axis_index("y") + lax.axis_index("c")


def _peer_index(k):
    px, py, pc = _peer(k)
    return 4 * px + 2 * py + pc


def _all_gather(shards, name):
    n = len(shards)
    any_spec = pl.BlockSpec(memory_space=pl.ANY)

    def body(*refs):
        ins, outs = refs[:n], refs[n:2 * n]
        send_sems, recv_sems, local_sems = refs[2 * n:]
        me = _my_index()
        local = [pltpu.make_async_copy(ins[a], outs[a].at[me], local_sems.at[a]) for a in range(n)]
        for cp in local:
            cp.start()
        sends = []
        for k in range(1, N_DEV):
            for a in range(n):
                cp = pltpu.make_async_remote_copy(src_ref=ins[a], dst_ref=outs[a].at[me], send_sem=send_sems.at[a, k - 1],
                                                  recv_sem=recv_sems.at[a, k - 1], device_id=_peer(k), device_id_type=MESH_ID)
                cp.start()
                sends.append(cp)
        for k in range(1, N_DEV):
            for a in range(n):
                pltpu.make_async_remote_copy(src_ref=ins[a], dst_ref=outs[a].at[_peer_index(k)], send_sem=send_sems.at[a, k - 1],
                                             recv_sem=recv_sems.at[a, k - 1], device_id=_peer(k), device_id_type=MESH_ID).wait_recv()
        for cp in sends:
            cp.wait_send()
        for cp in local:
            cp.wait()

    return _pcall(body, in_specs=[any_spec] * n, out_specs=[any_spec] * n,
                  out_shape=[jax.ShapeDtypeStruct((N_DEV,) + s.shape, s.dtype) for s in shards],
                  scratch_shapes=[pltpu.SemaphoreType.DMA((n, N_DEV - 1)), pltpu.SemaphoreType.DMA((n, N_DEV - 1)),
                                  pltpu.SemaphoreType.DMA((n,))],
                  name=name)(*shards)


GATHER1_PEERS = (1, 2, 4, 6)
GATHER2_PEERS = (2, 4, 6)
SCATTER_PEERS = tuple(range(1, N_DEV))
MAX_SEMS = N_DEV - 1
MAX_JOBS = 6
US_PER_MB = {"gather1": 5.4, "gather2": 0.6, "scatter": 10.8}
SCATTER_PIECE_US = 110.0
PIECE_US = {"gather1": 62.0, "gather2": 1e9, "scatter": SCATTER_PIECE_US}


class _Job:
    def __init__(self, key, kind, src, lo=0, hi=None, dst=None):
        self.key, self.kind, self.src, self.dst = key, kind, src, dst
        shape = src.shape if kind != "gather1" else (N_DEV,) + src.shape
        self.out_shape = jax.ShapeDtypeStruct(shape, src.dtype)
        self.rows = shape[1]
        self.lo, self.hi = lo, self.rows if hi is None else hi
        self.row_us = US_PER_MB[kind] * math.prod(shape) * src.dtype.itemsize / 1e6 / self.rows
        pieces = max(1, round(self.row_us * self.rows / PIECE_US[kind]))
        while pieces > 1 and self.rows % (16 * pieces):
            pieces -= 1
        self.piece = self.rows // pieces

    @property
    def cost_us(self):
        return self.row_us * (self.hi - self.lo)


class _Comm:
    def __init__(self):
        self.queue, self.gathered, self.scattered, self.layer = [], {}, [], 0

    def push(self, key, kind, src):
        self.queue.append(_Job(key, kind, src))

    def take(self, budget_us, upto=None):
        jobs = [j for j in self.queue if j.kind == "gather2"][:MAX_JOBS]
        used = sum(j.cost_us for j in jobs)
        if upto is not None and any(j.key == upto for j in jobs):
            self.queue = [j for j in self.queue if j not in jobs]
            return jobs
        for j in [j for j in self.queue if j.kind != "gather2"]:
            if len(jobs) >= MAX_JOBS:
                break
            urgent = j.kind == "gather1" and int(j.key[1]) <= self.layer
            piece_us = j.row_us * j.piece
            n = 0
            while j.lo + (n + 1) * j.piece <= j.hi and ((used < budget_us) if urgent else (used + 0.5 * piece_us <= budget_us)):
                n += 1
                used += piece_us
            if n == 0:
                break
            part = _Job(j.key, j.kind, j.src, j.lo, j.lo + n * j.piece, j.dst)
            part.parent = j
            j.lo = part.hi
            jobs.append(part)
            if j.lo < j.hi or j.key == upto:
                break
        self.queue = [j for j in self.queue if j not in jobs and j.lo < j.hi]
        return jobs

    def finish(self, job, result):
        if job.kind == "gather2":
            self.gathered[job.key] = result
        elif job.hi < job.rows:
            job.parent.dst = result
        elif job.kind == "gather1":
            self.queue.insert(0, _Job(job.key, "gather2", result))
        else:
            self.scattered.append((job.key, result))

    def need(self, key):
        while key not in self.gathered:
            assert any(j.key == key for j in self.queue), key
            self.flush(self.take(1e9, upto=key))
        return self.gathered[key]

    def flush(self, jobs):
        def body(o_ref):
            o_ref[...] = jnp.zeros_like(o_ref)
        _carry(body, jobs, self, dict(in_specs=[], out_specs=pl.BlockSpec(memory_space=pltpu.VMEM),
                                      out_shape=jax.ShapeDtypeStruct((8, LANES), F32), name="exchange"))()


def _job_copies(job, src, dst, send_sems, recv_sems, local_sem):
    me = _my_index()
    peers = {"gather1": GATHER1_PEERS, "gather2": GATHER2_PEERS, "scatter": SCATTER_PEERS}[job.kind]
    sends, recvs = [], []
    rows = pl.ds(job.lo, job.hi - job.lo)
    for i, k in enumerate(peers):
        if job.kind == "gather1":
            s_ref, d_ref, to, got = src.at[rows], dst.at[me, rows], _peer(k), dst.at[_peer_index(k), rows]
        elif job.kind == "gather2":
            s_ref, d_ref, to, got = src.at[_peer_index(k)], dst.at[_peer_index(k)], _peer(1), dst.at[_peer_index(k | 1)]
        else:
            s_ref, d_ref, to, got = src.at[_peer_index(k), rows], dst.at[me, rows], _peer(k), dst.at[_peer_index(k), rows]
        sends.append(pltpu.make_async_remote_copy(src_ref=s_ref, dst_ref=d_ref, send_sem=send_sems.at[i], recv_sem=recv_sems.at[i],
                                                  device_id=to, device_id_type=MESH_ID))
        recvs.append(pltpu.make_async_remote_copy(src_ref=s_ref, dst_ref=got, send_sem=send_sems.at[i], recv_sem=recv_sems.at[i],
                                                  device_id=to, device_id_type=MESH_ID))
    local = None
    if job.kind == "gather1":
        local = pltpu.make_async_copy(src.at[rows], dst.at[me, rows], local_sem)
    elif job.kind == "scatter":
        local = pltpu.make_async_copy(src.at[me, rows], dst.at[me, rows], local_sem)
    return sends, recvs, local


def _carry(body, jobs, comm, kw):
    kw = dict(kw)
    grid = tuple(kw.get("grid", ()))
    in_specs = list(kw["in_specs"])
    single = not isinstance(kw["out_specs"], (list, tuple))
    out_specs = [kw["out_specs"]] if single else list(kw["out_specs"])
    out_shape = [kw["out_shape"]] if single else list(kw["out_shape"])
    scratch = list(kw.get("scratch_shapes", []))
    n_in, n_out, n_scr, nj = len(in_specs), len(out_specs), len(scratch), len(jobs)
    any_spec = pl.BlockSpec(memory_space=pl.ANY)
    landed = [a for a, job in enumerate(jobs) if job.dst is not None]
    n_land = len(landed)

    def wrapped(*refs):
        pos = 0

        def take(k):
            nonlocal pos
            part = refs[pos:pos + k]
            pos += k
            return part
        ins, rin, _, outs, rout, scr = take(n_in), take(nj), take(n_land), take(n_out), take(nj), take(n_scr)
        send_sems, recv_sems, local_sems = take(3)

        def copies():
            return [_job_copies(job, rin[a], rout[a], send_sems.at[a], recv_sems.at[a], local_sems.at[a]) for a, job in enumerate(jobs)]

        def start():
            for sends, _, local in copies():
                if local is not None:
                    local.start()
                for cp in sends:
                    cp.start()

        def finish():
            for sends, recvs, local in copies():
                for cp in recvs:
                    cp.wait_recv()
                for cp in sends:
                    cp.wait_send()
                if local is not None:
                    local.wait()

        if grid:
            first = functools.reduce(jnp.logical_and, [pl.program_id(a) == 0 for a in range(len(grid))])
            last = functools.reduce(jnp.logical_and, [pl.program_id(a) == grid[a] - 1 for a in range(len(grid))])
            pl.when(first)(start)
            body(*ins, *outs, *scr)
            pl.when(last)(finish)
        else:
            start()
            body(*ins, *outs, *scr)
            finish()

    aliases = {n_in + a: n_out + a for a, job in enumerate(jobs) if job.kind == "gather2"}
    aliases.update({n_in + nj + i: n_out + a for i, a in enumerate(landed)})
    extra = dict(dimension_semantics=("arbitrary",) * len(grid)) if grid else {}
    call = _raw_call(wrapped, in_specs=in_specs + [any_spec] * (nj + n_land), out_specs=out_specs + [any_spec] * nj,
                     out_shape=out_shape + [job.out_shape for job in jobs],
                     scratch_shapes=scratch + [pltpu.SemaphoreType.DMA((nj, MAX_SEMS)), pltpu.SemaphoreType.DMA((nj, MAX_SEMS)),
                                               pltpu.SemaphoreType.DMA((nj,))],
                     input_output_aliases=aliases, compiler_params=_params(**extra), name=kw["name"],
                     **({"grid": grid} if grid else {}))

    def run(*args):
        res = call(*args, *[job.src for job in jobs], *[jobs[a].dst for a in landed])
        for job, r in zip(jobs, res[n_out:]):
            comm.finish(job, r)
        return res[0] if single else list(res[:n_out])
    return run


def _adam(g, w, m, v):
    m2 = ADAM_B1 * m + (1.0 - ADAM_B1) * g
    v2 = ADAM_B2 * v + (1.0 - ADAM_B2) * (g * g)
    m_hat = m2 / (1.0 - ADAM_B1 ** ADAM_STEP)
    v_hat = v2 / (1.0 - ADAM_B2 ** ADAM_STEP)
    delta = -ADAM_LR * (m_hat / (jnp.sqrt(v_hat) + ADAM_EPS) + ADAM_WD * w)
    return delta, m2, v2


def _sum_adam(parts, w, m, v, name):
    r, c = w.shape
    tr = _pick(r, max(8, (1 << 19) // c), 8)

    def body(p_ref, w_ref, m_ref, v_ref, g_ref, d_ref, m2_ref, v2_ref):
        g = p_ref[0].astype(F32)
        for j in range(1, N_DEV):
            g = g + p_ref[j].astype(F32)
        delta, m2, v2 = _adam(g, w_ref[...], m_ref[...], v_ref[...])
        g_ref[...] = g
        d_ref[...] = delta
        m2_ref[...] = m2
        v2_ref[...] = v2

    blk = pl.BlockSpec((tr, c), lambda i: (i, 0))
    shp = jax.ShapeDtypeStruct((r, c), F32)
    return _pcall(body, grid=(r // tr,),
                  in_specs=[pl.BlockSpec((N_DEV, tr, c), lambda i: (0, i, 0)), blk, blk, blk],
                  out_specs=[blk] * 4, out_shape=[shp] * 4, compiler_params=_params(("parallel",)), name=name)(parts, w, m, v)


def _small_layout(rep_shapes, sh_shapes):
    rows_r = [-(-s[1] // LANES) for s in rep_shapes]
    off_r = [sum(rows_r[:i]) for i in range(len(rows_r))]
    tot_r = -(-max(sum(rows_r), 8) // 8) * 8
    rows_s = [-(-s[-2] // 8) * 8 for s in sh_shapes]
    off_s = [sum(rows_s[:i]) for i in range(len(rows_s))]
    tot_s = max(sum(rows_s), 8)
    cmax = max([s[-1] for s in sh_shapes] + [LANES])
    return rows_r, off_r, tot_r, off_s, tot_s, cmax


def _small_exchange(rep_parts, sh_parts, name):
    nr, ns = len(rep_parts), len(sh_parts)
    rows_r, off_r, tot_r, off_s, tot_s, cmax = _small_layout([p.shape for p in rep_parts], [p.shape for p in sh_parts])
    vm = pl.BlockSpec(memory_space=pltpu.VMEM)

    def body(*refs):
        pos = 0

        def take(k):
            nonlocal pos
            out = refs[pos:pos + k]
            pos += k
            return out
        rp, sp = take(nr), take(ns)
        out_r, out_s = take(2)
        pack_r, got_r, pack_s, got_s, send_r, recv_r, send_s, recv_s = take(8)
        me = _my_index()
        pack_r[...] = jnp.zeros_like(pack_r)
        pack_s[...] = jnp.zeros_like(pack_s)
        for i in range(nr):
            nfull = rep_parts[i].shape[1]
            for rr in range(rows_r[i]):
                wdt = min(LANES, nfull - rr * LANES)
                pack_r[off_r[i] + rr:off_r[i] + rr + 1, 0:wdt] = rp[i][0:1, rr * LANES:rr * LANES + wdt]
        for i in range(ns):
            _, r_i, c_i = sh_parts[i].shape
            for j in range(N_DEV):
                pack_s[j, off_s[i]:off_s[i] + r_i, 0:c_i] = sp[i][j]
        got_r[me] = pack_r[...]
        got_s[me] = pack_s[me]
        sends = []
        for k in range(1, N_DEV):
            a = pltpu.make_async_remote_copy(src_ref=pack_r, dst_ref=got_r.at[me], send_sem=send_r.at[k - 1], recv_sem=recv_r.at[k - 1],
                                             device_id=_peer(k), device_id_type=MESH_ID)
            b = pltpu.make_async_remote_copy(src_ref=pack_s.at[_peer_index(k)], dst_ref=got_s.at[me], send_sem=send_s.at[k - 1],
                                             recv_sem=recv_s.at[k - 1], device_id=_peer(k), device_id_type=MESH_ID)
            a.start()
            b.start()
            sends += [a, b]
        for k in range(1, N_DEV):
            pltpu.make_async_remote_copy(src_ref=pack_r, dst_ref=got_r.at[_peer_index(k)], send_sem=send_r.at[k - 1],
                                         recv_sem=recv_r.at[k - 1], device_id=_peer(k), device_id_type=MESH_ID).wait_recv()
            pltpu.make_async_remote_copy(src_ref=pack_s.at[me], dst_ref=got_s.at[_peer_index(k)], send_sem=send_s.at[k - 1],
                                         recv_sem=recv_s.at[k - 1], device_id=_peer(k), device_id_type=MESH_ID).wait_recv()
        for cp in sends:
            cp.wait_send()
        tot_rep = got_r[0]
        tot_sh = got_s[0]
        for j in range(1, N_DEV):
            tot_rep = tot_rep + got_r[j]
            tot_sh = tot_sh + got_s[j]
        out_r[...] = tot_rep
        out_s[...] = tot_sh

    return _pcall(body, carry_us=SMALL_EXCHANGE_CARRY_US, in_specs=[vm] * (nr + ns), out_specs=[vm] * 2,
                  out_shape=[jax.ShapeDtypeStruct((tot_r, LANES), F32), jax.ShapeDtypeStruct((tot_s, cmax), F32)],
                  scratch_shapes=[pltpu.VMEM((tot_r, LANES), F32), pltpu.VMEM((N_DEV, tot_r, LANES), F32),
                                  pltpu.VMEM((N_DEV, tot_s, cmax), F32), pltpu.VMEM((N_DEV, tot_s, cmax), F32),
                                  pltpu.SemaphoreType.DMA((N_DEV - 1,)), pltpu.SemaphoreType.DMA((N_DEV - 1,)),
                                  pltpu.SemaphoreType.DMA((N_DEV - 1,)), pltpu.SemaphoreType.DMA((N_DEV - 1,))],
                  compiler_params=_params(), name=name)(*rep_parts, *sh_parts)


def _small_adam(tot_rep, tot_sh, rep_w, rep_m, rep_v, sh_w, sh_m, sh_v, name):
    nr, ns = len(rep_w), len(sh_w)
    rows_r, off_r, _, off_s, _, _ = _small_layout([w.shape for w in rep_w], [w.shape for w in sh_w])
    vm = pl.BlockSpec(memory_space=pltpu.VMEM)

    def body(*refs):
        pos = 0

        def take(k):
            nonlocal pos
            out = refs[pos:pos + k]
            pos += k
            return out
        (tr_ref, ts_ref), rw, rm, rv, sw, sm, sv = take(2), take(nr), take(nr), take(nr), take(ns), take(ns), take(ns)
        rg, rd, rm2, rv2 = take(nr), take(nr), take(nr), take(nr)
        sg, sd, sm2, sv2 = take(ns), take(ns), take(ns), take(ns)
        for i in range(nr):
            nfull = rep_w[i].shape[1]
            for rr in range(rows_r[i]):
                wdt = min(LANES, nfull - rr * LANES)
                rg[i][0:1, rr * LANES:rr * LANES + wdt] = tr_ref[off_r[i] + rr:off_r[i] + rr + 1, 0:wdt]
            delta, m2, v2 = _adam(rg[i][...], rw[i][...], rm[i][...], rv[i][...])
            rd[i][...] = delta
            rm2[i][...] = m2
            rv2[i][...] = v2
        for i in range(ns):
            r_i, c_i = sh_w[i].shape
            g = ts_ref[off_s[i]:off_s[i] + r_i, 0:c_i]
            delta, m2, v2 = _adam(g, sw[i][...], sm[i][...], sv[i][...])
            sg[i][...] = g
            sd[i][...] = delta
            sm2[i][...] = m2
            sv2[i][...] = v2

    shapes = [jax.ShapeDtypeStruct(w.shape, F32) for w in rep_w] * 4 + [jax.ShapeDtypeStruct(w.shape, F32) for w in sh_w] * 4
    outs = _pcall(body, in_specs=[vm] * (2 + 3 * nr + 3 * ns), out_specs=[vm] * len(shapes), out_shape=shapes,
                  compiler_params=_params(), name=name)(tot_rep, tot_sh, *rep_w, *rep_m, *rep_v, *sh_w, *sh_m, *sh_v)
    rep_out = [outs[i * nr:(i + 1) * nr] for i in range(4)]
    sh_out = [outs[4 * nr + i * ns:4 * nr + (i + 1) * ns] for i in range(4)]
    return rep_out, sh_out


_CONF = ("norm_g", "a_w_in", "a_b_in", "a_dw_w", "a_dw_b", "a_ln_g", "a_ln_b", "a_w_out", "a_b_out")
_FFN = ("ffn_norm_g", "ffn_w_up", "ffn_dw_w", "ffn_dw_b", "ffn_w_down")
_POOL = ("norm_g", "b_w_group", "b_scale")
_ATTN = ("norm_g", "c_w_qkv", "c_q_norm_g", "c_k_norm_g", "c_sinks", "c_w_o")
_LAYERS = (_CONF + _FFN, _POOL + _FFN, _ATTN + _FFN, _CONF + _FFN)
_NAMES = tuple("l%d_%s" % (i, n) for i, names in enumerate(_LAYERS) for n in names)
_BIG = ("a_w_in", "a_w_out", "ffn_w_up", "ffn_w_down", "b_w_group", "c_w_qkv", "c_w_o")
_SHARDED_SMALL = ("a_dw_w", "ffn_dw_w")


def _pad_rows(a, mult=8):
    r = a.shape[0]
    rp = -(-r // mult) * mult
    return a if rp == r else jnp.pad(a, ((0, rp - r), (0, 0)))


def _unstack_cols(st, rows):
    s, r, cs = st.shape
    return jnp.transpose(st, (1, 0, 2)).reshape(r, s * cs)[:rows]


def _stack_cols(a):
    r, c = a.shape
    return jnp.transpose(a.reshape(r, N_DEV, c // N_DEV), (1, 0, 2))


def _row(v):
    return v.reshape(1, -1)


def _ffn_forward(x_mid, p, tag):
    h2 = _rms_fwd(x_mid, _row(p["ffn_norm_g"]), BF16, "rms_fwd_bf16")
    u0 = _mm(h2, p["ffn_w_up"], "nn", name="ffn_up", tn=1408, tk=2048)
    a = _ffn_act_fwd(u0, p["ffn_dw_w"], _row(p["ffn_dw_b"]), "ffn_act_fwd")
    x_out = _mm(a, p["ffn_w_down"], "nn", res=x_mid, name="ffn_down", tk=2816)
    return x_out, dict(h2=h2, u0=u0, a=a)


def _ffn_backward(dx_pair, x_mid, p, sv, grads):
    dx_out, dx16 = dx_pair
    dwd = _mm(sv["a"], dx16, "tn", out_dtype=BF16, name="ffn_down_dw", tm=1408, tk=2048)
    grads["ffn_w_down"] = dwd.reshape(N_DEV, dwd.shape[0] // N_DEV, dwd.shape[1])
    da = _mm(dx16, p["ffn_w_down"], "nt", name="ffn_down_dx", tn=1408, tk=2048)
    du0, dww, dwb = _ffn_act_bwd(da, sv["u0"], p["ffn_dw_w"], _row(p["ffn_dw_b"]), "ffn_act_bwd")
    kw = dww.shape[1]
    grads["ffn_dw_w"] = _stack_cols(jnp.transpose(dww, (1, 0, 2)).reshape(kw, -1))
    grads["ffn_dw_b"] = dwb.reshape(1, -1)
    grads["ffn_w_up"] = _mm(sv["h2"], du0, "tn", out_dtype=BF16, out_stack=N_DEV, name="ffn_up_dw", tn=1408, tk=2048)
    dh2 = _mm(du0, p["ffn_w_up"], "nt", name="ffn_up_dx", tk=1408, panels=2)
    dx_mid, dg, _, dx_mid16 = _rms_bwd(dh2, x_mid, _row(p["ffn_norm_g"]), dx_out, "rms_bwd")
    grads["ffn_norm_g"] = dg
    return dx_mid, dx_mid16


def _conf_forward(x, p):
    h = _rms_fwd(x, _row(p["norm_g"]), BF16, "rms_fwd_bf16")
    u = _mm(h, p["a_w_in"], "nn", bias=_row(p["a_b_in"]), name="conf_in", tn=512, tk=2048)
    cpre = _conf_conv_fwd(u, p["a_dw_w"], _row(p["a_dw_b"]), "conf_conv_fwd")
    s = _ln_silu_fwd(cpre, _row(p["a_ln_g"]), _row(p["a_ln_b"]), "ln_silu_fwd")
    x_mid = _mm(s, p["a_w_out"], "nn", bias=_row(p["a_b_out"]), res=x, name="conf_out", tk=2048)
    return x_mid, dict(h=h, u=u, cpre=cpre, s=s)


def _conf_backward(dx_pair, x, p, sv, grads):
    dx_mid, dx16 = dx_pair
    dwo = _mm(sv["s"], dx16, "tn", out_dtype=BF16, name="conf_out_dw", tk=2048)
    grads["a_w_out"] = dwo.reshape(N_DEV, dwo.shape[0] // N_DEV, dwo.shape[1])
    ds = _mm(dx16, p["a_w_out"], "nt", name="conf_out_dx", tk=2048)
    dc, dlg, dlb = _ln_silu_bwd(ds, sv["cpre"], _row(p["a_ln_g"]), _row(p["a_ln_b"]), "ln_silu_bwd")
    grads["a_ln_g"], grads["a_ln_b"] = dlg, dlb
    du, dww, dwb, dbin = _conf_conv_bwd(dc, sv["u"], p["a_dw_w"], "conf_conv_bwd")
    grads["a_dw_w"] = _stack_cols(dww)
    grads["a_dw_b"] = dwb
    grads["a_b_in"] = dbin.reshape(1, -1)
    grads["a_w_in"] = _mm(sv["h"], du, "tn", out_dtype=BF16, out_stack=N_DEV, name="conf_in_dw", tn=512, tk=2048)
    dh = _mm(du, p["a_w_in"], "nt", name="conf_in_dx", tk=512, panels=4)
    dx, dg, dbo, dx16 = _rms_bwd(dh, x, _row(p["norm_g"]), dx_mid, "rms_bwd")
    grads["norm_g"] = dg
    grads["a_b_out"] = dbo
    return dx, dx16


def _pool_forward(x, p):
    h = _rms_fwd(x, _row(p["norm_g"]), F32, "rms_fwd_f32")
    mixed = _pool_fwd(h, "pool_fwd")
    x_mid = _pool_mm_fwd(mixed, p["b_w_group"], _row(p["b_scale"]), x, "pool_mm_fwd")
    return x_mid, dict(mixed=mixed)


def _pool_backward(dx_pair, x, p, sv, grads):
    dx_mid = dx_pair[0]
    dmixed, dwg, dscale = _pool_mm_bwd(dx_mid, sv["mixed"], p["b_w_group"], _row(p["b_scale"]), "pool_mm_bwd")
    ng, gd, _ = dwg.shape
    grads["b_w_group"] = jnp.transpose(dwg.reshape(ng, N_DEV, gd // N_DEV, gd), (1, 0, 2, 3)).reshape(N_DEV, ng * gd // N_DEV, gd).astype(BF16)
    grads["b_scale"] = dscale
    dh = _pool_bwd(dmixed, "pool_bwd")
    dx, dg, _, dx16 = _rms_bwd(dh, x, _row(p["norm_g"]), dx_mid, "rms_bwd")
    grads["norm_g"] = dg
    return dx, dx16


def _attn_tables(p, positions, d_model):
    n_q = d_model // HEAD
    n_kv = n_q // 8
    tabs = _rope_tables(positions)
    gq2 = jnp.concatenate([p["c_q_norm_g"], p["c_q_norm_g"]]).reshape(1, LANES)
    gk2 = jnp.concatenate([p["c_k_norm_g"], p["c_k_norm_g"]]).reshape(1, LANES)
    sink_tab = jnp.repeat(jnp.repeat(p["c_sinks"].reshape(-1, 2), HEAD, axis=1), 8, axis=0)
    return n_q, n_kv, tabs, gq2, gk2, sink_tab


def _attn_forward(x, p, positions):
    n_q, n_kv, tabs, gq2, gk2, sink_tab = _attn_tables(p, positions, x.shape[1])
    h = _rms_fwd(x, _row(p["norm_g"]), BF16, "rms_fwd_bf16")
    qkv = _mm(h, p["c_w_qkv"], "nn", name="attn_qkv", tn=1280, tk=2048)
    q, k2, v2 = _qk_prep_fwd(qkv, tabs, gq2, gk2, n_q, n_kv, "qk_prep_fwd")
    o, lse = _attn_fwd(q, k2, v2, sink_tab, "attn_fwd")
    x_mid = _mm(o, p["c_w_o"], "nn", res=x, name="attn_out", tk=2048)
    return x_mid, dict(h=h, qkv=qkv, q=q, k2=k2, v2=v2, o=o, lse=lse)


def _attn_backward(dx_pair, x, p, positions, sv, grads):
    dx_mid, dx16 = dx_pair
    n_q, n_kv, tabs, gq2, gk2, sink_tab = _attn_tables(p, positions, x.shape[1])
    dwo = _mm(sv["o"], dx16, "tn", out_dtype=BF16, name="attn_out_dw", tk=2048)
    grads["c_w_o"] = dwo.reshape(N_DEV, dwo.shape[0] // N_DEV, dwo.shape[1])
    do = _mm(dx16, p["c_w_o"], "nt", name="attn_out_dx", tk=2048)
    dq, dkc, dkp, dvc, dvp, dsk = _attn_bwd(do, sv["q"], sv["o"], sv["lse"], sv["k2"], sv["v2"], sink_tab, "attn_bwd")
    nkv_, nb = dsk.shape[0], dsk.shape[1]
    dsk = dsk.reshape(nkv_, nb, -1, 8, LANES)[:, :, :, 0, :].sum(axis=1).reshape(-1, LANES)
    grads["c_sinks"] = jnp.stack([dsk[:, 0], dsk[:, HEAD]], axis=1).reshape(1, -1)
    dqkv, dgq, dgk = _qk_prep_bwd(dq, dkc, dkp, dvc, dvp, sv["qkv"], tabs, gq2, gk2, n_q, n_kv, "qk_prep_bwd")
    grads["c_q_norm_g"] = dgq[:, :HEAD] + dgq[:, HEAD:]
    grads["c_k_norm_g"] = dgk[:, :HEAD] + dgk[:, HEAD:]
    dh = _mm(dqkv, p["c_w_qkv"], "nt", name="attn_qkv_dx", tk=1280)
    dwq = _mm(sv["h"], dqkv, "tn", out_dtype=BF16, name="attn_qkv_dw", tn=1280, tk=2048)
    grads["c_w_qkv"] = _stack_cols(dwq)
    dx, dg, _, dx16 = _rms_bwd(dh, x, _row(p["norm_g"]), dx_mid, "rms_bwd")
    grads["norm_g"] = dg
    return dx, dx16


class _LayerWeights:
    def __init__(self, li, weights, small_full, comm):
        self.li, self.weights, self.small_full, self.comm, self.cache = li, weights, small_full, comm, {}

    def __getitem__(self, nme):
        if nme not in self.cache:
            self.cache[nme] = self.fetch(nme)
        return self.cache[nme]

    def fetch(self, nme):
        full = "l%d_%s" % (self.li, nme)
        w = self.weights[full]
        if nme in _SHARDED_SMALL:
            return _unstack_cols(self.small_full[full], w.shape[0])
        if nme not in _BIG:
            return w
        got = self.comm.need(full)
        if nme in ("a_w_in", "ffn_w_up"):
            return got
        if nme == "c_w_qkv":
            return _unstack_cols(got, w.shape[0])
        if nme == "b_w_group":
            ng, gs, gd = w.shape
            return jnp.transpose(got.reshape(N_DEV, ng, gs, gd), (1, 0, 2, 3)).reshape(ng, N_DEV * gs, gd)
        return got.reshape(-1, w.shape[1])


class _LayerGrads(dict):
    def __init__(self, li, comm):
        super().__init__()
        self.li, self.comm = li, comm

    def __setitem__(self, nme, value):
        if nme in _BIG:
            self.comm.push("l%d_%s" % (self.li, nme), "scatter", value)
        else:
            super().__setitem__(nme, value)


def kernel(*args):
    n_w = len(_NAMES)
    x, positions = args[0], args[1]
    weights = dict(zip(_NAMES, args[2:2 + n_w]))
    loss_target = args[2 + n_w]
    moms = dict(zip(_NAMES, args[3 + n_w:3 + 2 * n_w]))
    vels = dict(zip(_NAMES, args[3 + 2 * n_w:3 + 3 * n_w]))
    x0 = x[0]
    pos = positions[0]
    kinds = ("conf", "pool", "attn", "conf")
    comm = _Comm()
    _STATE["comm"], _STATE["last"] = comm, None
    shd = [n for n in _NAMES if n.split("_", 1)[1] in _SHARDED_SMALL]
    small_full = dict(zip(shd, _all_gather([_pad_rows(weights[n]) for n in shd], "gather_small")))
    for n in _NAMES:
        if n.split("_", 1)[1] in _BIG:
            w = weights[n]
            comm.push(n, "gather1", w.astype(BF16).reshape(-1, w.shape[-1]))
    results = {}

    def update_ready():
        while comm.scattered:
            full, parts = comm.scattered.pop(0)
            w = weights[full]
            w2 = w.reshape(-1, w.shape[-1])
            outs = _sum_adam(parts, w2, moms[full].reshape(w2.shape), vels[full].reshape(w2.shape), "adam_" + full.split("_", 1)[1])
            results[full] = tuple(o.reshape(w.shape) for o in outs)

    params, saved = [], []
    cur = x0
    for li, names in enumerate(_LAYERS):
        comm.layer = li
        p = _LayerWeights(li, weights, small_full, comm)
        if kinds[li] == "conf":
            x_mid, sv = _conf_forward(cur, p)
        elif kinds[li] == "pool":
            x_mid, sv = _pool_forward(cur, p)
        else:
            x_mid, sv = _attn_forward(cur, p, pos)
        x_out, sv_f = _ffn_forward(x_mid, p, kinds[li])
        params.append(p)
        saved.append((sv, sv_f, cur, x_mid))
        cur = x_out
    dy, loss_part, dy16 = _loss_head(cur, loss_target[0], "loss_head")
    loss = lax.psum(loss_part[0, 0], ("x", "y", "c"))

    small_grads = {}
    dcur = (dy, dy16)
    for li in range(len(_LAYERS) - 1, -1, -1):
        p = params[li]
        sv, sv_f, x_in, x_mid = saved[li]
        grads = _LayerGrads(li, comm)
        dmid = _ffn_backward(dcur, x_mid, p, sv_f, grads)
        update_ready()
        if kinds[li] == "conf":
            dcur = _conf_backward(dmid, x_in, p, sv, grads)
        elif kinds[li] == "pool":
            dcur = _pool_backward(dmid, x_in, p, sv, grads)
        else:
            dcur = _attn_backward(dmid, x_in, p, pos, sv, grads)
        update_ready()
        for n in _LAYERS[li]:
            if n not in _BIG:
                small_grads["l%d_%s" % (li, n)] = grads[n]
    rep = [n for n in _NAMES if n.split("_", 1)[1] not in _BIG and n.split("_", 1)[1] not in _SHARDED_SMALL]
    tot_rep, tot_sh = _small_exchange([small_grads[n] for n in rep], [small_grads[n] for n in shd], "small_exchange")
    while comm.queue or comm.scattered:
        if not comm.scattered:
            comm.flush(comm.take(1e9))
        update_ready()
    _STATE["comm"] = None
    rep_out, sh_out = _small_adam(tot_rep, tot_sh, [_row(weights[n]) for n in rep], [_row(moms[n]) for n in rep], [_row(vels[n]) for n in rep],
                                  [weights[n] for n in shd], [moms[n] for n in shd], [vels[n] for n in shd], "small_adam")
    for i, n in enumerate(rep):
        results[n] = tuple(rep_out[k][i].reshape(weights[n].shape) for k in range(4))
    for i, n in enumerate(shd):
        results[n] = tuple(sh_out[k][i] for k in range(4))

    _STATE["last"] = None
    grad_x = dcur[0][None]
    out = [loss, grad_x]
    for k in range(4):
        out += [results[n][k] for n in _NAMES]
    return tuple(out)
```
